```python
import math
import jax, jax.numpy as jnp
from jax import lax
import numpy as np

D_MODEL = 1024
BATCH = 4
SEQ = 8192
DEPTH = 2

N_MEM = 256
D_MIX = D_MODEL
D_RWKV = D_MIX // 2
RW_HEAD = 64
RW_HEADS = D_RWKV // RW_HEAD
RW_W_RANK = 64
RW_A_RANK = 64
RW_G_RANK = 128
RW_V_RANK = 32
RW_SPLITS = [D_RWKV, 2 * D_RWKV, 3 * D_RWKV, 3 * D_RWKV + RW_W_RANK, 3 * D_RWKV + RW_W_RANK + RW_A_RANK]
RW_COLS = 3 * D_RWKV + RW_W_RANK + RW_A_RANK + RW_G_RANK
D_S5 = D_MIX - D_RWKV
S5_CH = 16
S5_GROUPS = D_S5 // S5_CH
S5_STATE = 64
D_IN = RW_COLS + D_S5
XA_HEADS = 4
XA_HEAD = D_MODEL // XA_HEADS
PEER_HEADS = 8
PEER_NKEYS = 128
PEER_N = PEER_NKEYS * PEER_NKEYS
PEER_DQ = 256
PEER_DHALF = PEER_DQ // 2
PEER_TOPK = 16
PEER_BLOCK = 128
RMS_EPS = 1e-6
GN_EPS = 64e-5

kernel_name = 'hybrid_rwkv7_s5_peer_memxattn'


def rmsnorm(x, g):
    xf = x.astype(jnp.float32)
    y = xf * lax.rsqrt(jnp.mean(xf * xf, axis=-1, keepdims=True) + RMS_EPS)
    return (y * g.astype(jnp.float32)).astype(x.dtype)


def token_shift(z):
    return jnp.pad(z[:, :-1], ((0, 0), (1, 0), (0, 0)))


def rwkv7_recurrence(r, decay, k, v, a_vec, b_vec):
    bsz, _, nh, n = r.shape

    def step(state, inp):
        r_t, w_t, k_t, v_t, a_t, b_t = inp
        sa = jnp.einsum('bhij,bhj->bhi', state, a_t)
        state = (state * w_t[:, :, None, :] + sa[..., :, None] * b_t[..., None, :]
                 + v_t[..., :, None] * k_t[..., None, :])
        return state, jnp.einsum('bhij,bhj->bhi', state, r_t)

    xs = tuple(jnp.swapaxes(t, 0, 1) for t in (r, decay, k, v, a_vec, b_vec))
    s0 = jnp.zeros((bsz, nh, n, n), jnp.float32)
    _, y = lax.scan(step, s0, xs)
    return jnp.swapaxes(y, 0, 1)


def rwkv7_mix(z, h, mu, w0, w2, a0, a2, g2, k_k, k_a, r_k, lnx_g, lnx_b, v_first, vres):
    f32 = jnp.float32
    bsz, s, _ = z.shape
    z = z + (token_shift(z) - z) * mu
    r, k, v, wd, ad, gd = jnp.split(z, RW_SPLITS, axis=-1)
    w = -jax.nn.softplus(-(w0 + jnp.tanh(wd) @ w2)) - 0.5
    decay = jnp.exp(-jnp.exp(w.astype(f32)))
    if vres is None:
        v_first = v
    else:
        v0, v1, v2 = vres
        v = v + (v_first - v) * jax.nn.sigmoid(v0 + (h @ v1) @ v2)
    a = jax.nn.sigmoid(a0 + ad @ a2)
    g = jax.nn.sigmoid(gd) @ g2

    def heads(t):
        return t.reshape(bsz, s, RW_HEADS, RW_HEAD).astype(f32)

    kk = heads(k * k_k)
    kk = kk / jnp.maximum(jnp.sqrt(jnp.sum(kk * kk, axis=-1, keepdims=True)), 1e-12)
    k = k * (1.0 + (a - 1.0) * k_a)
    rh, kh, vh, ah = heads(r), heads(k), heads(v), heads(a)
    y = rwkv7_recurrence(rh, heads(decay), kh, vh, -kk, kk * ah)
    mean = jnp.mean(y, axis=-1, keepdims=True)
    var = jnp.mean(jnp.square(y - mean), axis=-1, keepdims=True)
    y = ((y - mean) * lax.rsqrt(var + GN_EPS)).reshape(bsz, s, D_RWKV)
    y = y * lnx_g.astype(f32) + lnx_b.astype(f32)
    bonus = jnp.sum(rh * kh * r_k.astype(f32), axis=-1, keepdims=True) * vh
    out = (y + bonus.reshape(bsz, s, D_RWKV)) * g.astype(f32)
    return out.astype(z.dtype), v_first


def complex_affine_combine(e1, e2):
    a1r, a1i, b1r, b1i = e1
    a2r, a2i, b2r, b2i = e2
    ar = a2r * a1r - a2i * a1i
    ai = a2r * a1i + a2i * a1r
    wr, wi = a2r[:, None], a2i[:, None]
    br = wr * b1r - wi * b1i + b2r
    bi = wr * b1i + wi * b1r + b2i
    return ar, ai, br, bi


def s5_mix(u, a_re, a_im, log_dt, b_re, b_im, c_re, c_im, d_skip, glu_w, glu_b, out_g):
    f32 = jnp.float32
    bsz, s, _ = u.shape
    uf = u.astype(f32).reshape(bsz, s, S5_GROUPS, S5_CH)
    lam_re = jnp.minimum(a_re.astype(f32), -1e-4)
    lam_im = a_im.astype(f32)
    dt = jnp.exp(log_dt.astype(f32))[:, None]
    mag = jnp.exp(lam_re * dt)
    lb_re = mag * jnp.cos(lam_im * dt)
    lb_im = mag * jnp.sin(lam_im * dt)
    den = lam_re * lam_re + lam_im * lam_im
    c1_re = ((lb_re - 1.0) * lam_re + lb_im * lam_im) / den
    c1_im = (lb_im * lam_re - (lb_re - 1.0) * lam_im) / den
    br, bi = b_re.astype(f32), b_im.astype(f32)
    bb_re = c1_re[..., None] * br - c1_im[..., None] * bi
    bb_im = c1_re[..., None] * bi + c1_im[..., None] * br
    bu_re = jnp.einsum('bsgc,gpc->sbgp', uf, bb_re)
    bu_im = jnp.einsum('bsgc,gpc->sbgp', uf, bb_im)
    ar_seq = jnp.broadcast_to(lb_re, (s, S5_GROUPS, S5_STATE))
    ai_seq = jnp.broadcast_to(lb_im, (s, S5_GROUPS, S5_STATE))
    _, _, x_re, x_im = lax.associative_scan(complex_affine_combine, (ar_seq, ai_seq, bu_re, bu_im), axis=0)
    y = (jnp.einsum('sbgp,gcp->bsgc', x_re, c_re.astype(f32))
         - jnp.einsum('sbgp,gcp->bsgc', x_im, c_im.astype(f32)))
    y = y.reshape(bsz, s, D_S5) + d_skip.astype(f32) * uf.reshape(bsz, s, D_S5)
    y = jax.nn.gelu(y)
    y = y * jax.nn.sigmoid(y @ glu_w.astype(f32) + glu_b.astype(f32))
    return rmsnorm(y, out_g).astype(u.dtype)


def mem_cross_attention(h, m, wq, wk, wv, wo):
    bsz, s, _ = h.shape
    q = (h @ wq).reshape(bsz, s, XA_HEADS, XA_HEAD)
    k = (m @ wk).reshape(bsz, N_MEM, XA_HEADS, XA_HEAD)
    v = (m @ wv).reshape(bsz, N_MEM, XA_HEADS, XA_HEAD)
    sc = jnp.einsum('bshd,bmhd->bhsm', q, k).astype(jnp.float32) * (XA_HEAD ** -0.5)
    p = jax.nn.softmax(sc, axis=-1).astype(v.dtype)
    o = jnp.einsum('bhsm,bmhd->bshd', p, v).reshape(bsz, s, D_MODEL)
    return o @ wo


def peer_ffn(h, wq, keys, u_tab, v_tab):
    bsz, s, d = h.shape
    hb = h.reshape(-1, PEER_BLOCK, d)

    def block(hc):
        q = (hc @ wq).reshape(PEER_BLOCK, PEER_HEADS, 2, PEER_DHALF)
        sc = jnp.einsum('thcd,hckd->thck', q, keys).astype(jnp.float32)
        s1, i1 = lax.top_k(sc[:, :, 0], PEER_TOPK)
        s2, i2 = lax.top_k(sc[:, :, 1], PEER_TOPK)
        cand = (s1[..., :, None] + s2[..., None, :]).reshape(PEER_BLOCK, PEER_HEADS, PEER_TOPK * PEER_TOPK)
        cidx = (i1[..., :, None] * PEER_NKEYS + i2[..., None, :]).reshape(PEER_BLOCK, PEER_HEADS, PEER_TOPK * PEER_TOPK)
        top, pos = lax.top_k(cand, PEER_TOPK)
        eidx = jnp.take_along_axis(cidx, pos, axis=-1)
        gate = jax.nn.softmax(top, axis=-1)
        ue = jnp.take(u_tab, eidx, axis=0)
        act = jax.nn.gelu(jnp.einsum('thkd,td->thk', ue, hc).astype(jnp.float32))
        ve = jnp.take(v_tab, eidx, axis=0)
        return jnp.einsum('thk,thkd->td', (gate * act).astype(ve.dtype), ve)

    return lax.map(block, hb).reshape(bsz, s, d)


def setup_inputs(seed: int = 0) -> dict:
    key = jax.random.key(seed)
    ks = iter(jax.random.split(key, 48))
    L = DEPTH
    LV = max(DEPTH - 1, 0)
    f32 = jnp.float32

    def nrm(shape, scale):
        return scale * jax.random.normal(next(ks), shape, f32)

    def gain(shape):
        return 1.0 + nrm(shape, 0.02)

    def unif(shape, lo, hi):
        return jax.random.uniform(next(ks), shape, f32, lo, hi)

    n_idx = jnp.arange(S5_STATE, dtype=f32)
    return {
        'x': nrm((BATCH, SEQ, D_MODEL), 1.0),
        'mem': nrm((BATCH, N_MEM, D_MODEL), 1.0),
        'norm_mix': gain((L, D_MODEL)),
        'w_in': nrm((L, D_MODEL, D_IN), D_MODEL ** -0.5),
        'rw_mu': unif((L, RW_COLS), 0.0, 1.0),
        'rw_w0': unif((L, D_RWKV), -6.0, -1.0),
        'rw_w2': nrm((L, RW_W_RANK, D_RWKV), 0.3 * RW_W_RANK ** -0.5),
        'rw_a0': nrm((L, D_RWKV), 0.3),
        'rw_a2': nrm((L, RW_A_RANK, D_RWKV), 0.3 * RW_A_RANK ** -0.5),
        'rw_g2': nrm((L, RW_G_RANK, D_RWKV), RW_G_RANK ** -0.5),
        'rw_kk': 0.85 + nrm((L, D_RWKV), 0.02),
        'rw_ka': gain((L, D_RWKV)),
        'rw_rk': nrm((L, RW_HEADS, RW_HEAD), 0.1),
        'rw_v0': 1.0 + nrm((LV, D_RWKV), 0.1),
        'rw_v1': nrm((LV, D_MODEL, RW_V_RANK), D_MODEL ** -0.5),
        'rw_v2': nrm((LV, RW_V_RANK, D_RWKV), 0.3 * RW_V_RANK ** -0.5),
        'rw_lnx_g': gain((L, D_RWKV)),
        'rw_lnx_b': nrm((L, D_RWKV), 0.01),
        's5_a_re': -0.5 + nrm((L, S5_GROUPS, S5_STATE), 0.01),
        's5_a_im': math.pi * n_idx + nrm((L, S5_GROUPS, S5_STATE), 0.01),
        's5_log_dt': unif((L, S5_GROUPS), math.log(1e-3), math.log(1e-1)),
        's5_b_re': nrm((L, S5_GROUPS, S5_STATE, S5_CH), (2 * S5_CH) ** -0.5),
        's5_b_im': nrm((L, S5_GROUPS, S5_STATE, S5_CH), (2 * S5_CH) ** -0.5),
        's5_c_re': nrm((L, S5_GROUPS, S5_CH, S5_STATE), S5_STATE ** -0.5),
        's5_c_im': nrm((L, S5_GROUPS, S5_CH, S5_STATE), S5_STATE ** -0.5),
        's5_d': nrm((L, D_S5), 1.0),
        's5_glu_w': nrm((L, D_S5, D_S5), D_S5 ** -0.5),
        's5_glu_b': nrm((L, D_S5), 0.01),
        's5_out_g': gain((L, D_S5)),
        'w_out': nrm((L, D_MIX, D_MODEL), D_MIX ** -0.5),
        'norm_xa': gain((L, D_MODEL)),
        'norm_mem': gain((L, D_MODEL)),
        'xa_wq': nrm((L, D_MODEL, D_MODEL), D_MODEL ** -0.5),
        'xa_wk': nrm((L, D_MODEL, D_MODEL), D_MODEL ** -0.5),
        'xa_wv': nrm((L, D_MODEL, D_MODEL), D_MODEL ** -0.5),
        'xa_wo': nrm((L, D_MODEL, D_MODEL), D_MODEL ** -0.5),
        'norm_ffn': gain((L, D_MODEL)),
        'peer_wq': nrm((L, D_MODEL, PEER_HEADS * PEER_DQ), D_MODEL ** -0.5),
        'peer_keys': nrm((L, PEER_HEADS, 2, PEER_NKEYS, PEER_DHALF), PEER_DHALF ** -0.5),
        'peer_u': nrm((L, PEER_N, D_MODEL), D_MODEL ** -0.5),
        'peer_v': nrm((L, PEER_N, D_MODEL), PEER_HEADS ** -0.5),
        'norm_final': gain((D_MODEL,)),
    }


def reference(x, mem, norm_mix, w_in, rw_mu, rw_w0, rw_w2, rw_a0, rw_a2, rw_g2, rw_kk, rw_ka, rw_rk,
              rw_v0, rw_v1, rw_v2, rw_lnx_g, rw_lnx_b, s5_a_re, s5_a_im, s5_log_dt, s5_b_re, s5_b_im,
              s5_c_re, s5_c_im, s5_d, s5_glu_w, s5_glu_b, s5_out_g, w_out, norm_xa, norm_mem,
              xa_wq, xa_wk, xa_wv, xa_wo, norm_ffn, peer_wq, peer_keys, peer_u, peer_v, norm_final):
    v_first = None
    for l in range(DEPTH):
        h = rmsnorm(x, norm_mix[l])
        z = h @ w_in[l]
        z_rw, u_s5 = z[..., :RW_COLS], z[..., RW_COLS:]
        vres = None if l == 0 else (rw_v0[l - 1], rw_v1[l - 1], rw_v2[l - 1])
        y_rw, v_first = rwkv7_mix(z_rw, h, rw_mu[l], rw_w0[l], rw_w2[l], rw_a0[l], rw_a2[l], rw_g2[l],
                                  rw_kk[l], rw_ka[l], rw_rk[l], rw_lnx_g[l], rw_lnx_b[l], v_first, vres)
        y_s5 = s5_mix(u_s5, s5_a_re[l], s5_a_im[l], s5_log_dt[l], s5_b_re[l], s5_b_im[l],
                      s5_c_re[l], s5_c_im[l], s5_d[l], s5_glu_w[l], s5_glu_b[l], s5_out_g[l])
        x = x + jnp.concatenate([y_rw, y_s5], axis=-1) @ w_out[l]
        x = x + mem_cross_attention(rmsnorm(x, norm_xa[l]), rmsnorm(mem, norm_mem[l]),
                                    xa_wq[l], xa_wk[l], xa_wv[l], xa_wo[l])
        x = x + peer_ffn(rmsnorm(x, norm_ffn[l]), peer_wq[l], peer_keys[l], peer_u[l], peer_v[l])
    return rmsnorm(x, norm_final)
```

```python
import functools
import math

import jax
import jax.numpy as jnp
from jax import lax
from jax.experimental import pallas as pl
from jax.experimental.pallas import tpu as pltpu

F32 = jnp.float32
BF16 = jnp.bfloat16
U32 = jnp.uint32

LANES = 128
SUBLANES = 8
VMEM_LIMIT = 56 * 1024 * 1024

D_MODEL = 1024
D_RWKV = 512
RW_HEAD = 64
RW_COLS = 1792
D_S5 = 512
S5_GROUPS = 32
S5_CH = 16
S5_STATE = 64
S5_MODES = S5_GROUPS * S5_STATE
N_MEM = 256
XA_HEADS = 4
XA_HEAD = 256
PEER_HEADS = 8
PEER_NKEYS = 128
PEER_TOPK = 16
PEER_SEL = PEER_HEADS * PEER_TOPK
RMS_EPS = 1e-6
GN_EPS = 64e-5

RW_CHUNK = 128
S5_BLOCK = 256
PROJ_BLOCK = 512
XA_BLOCK = 256
SEL_BLOCK = 128
GATHER_BLOCK = 64


def _params(sem):
    return pltpu.CompilerParams(dimension_semantics=sem, vmem_limit_bytes=VMEM_LIMIT)


def _rms(x, g):
    ms = jnp.mean(x * x, axis=-1, keepdims=True)
    return x * lax.rsqrt(ms + RMS_EPS) * g


def _bdot(a, b):
    return jnp.dot(a.astype(BF16), b.astype(BF16), preferred_element_type=F32)


def _bdot_nt(a, b):
    return lax.dot_general(a.astype(BF16), b.astype(BF16), (((1,), (1,)), ((), ())),
                           preferred_element_type=F32)


def _sigmoid(x):
    return 1.0 / (1.0 + jnp.exp(-x))


def _softplus(x):
    return jnp.maximum(x, 0.0) + jnp.log(1.0 + jnp.exp(-jnp.abs(x)))


def _gelu(x):
    return 0.5 * x * (1.0 + jnp.tanh(math.sqrt(2.0 / math.pi) * (x + 0.044715 * (x * x * x))))


def _full(shape):
    n = len(shape)
    return pl.BlockSpec(shape, lambda *_: (0,) * n)


def _norm_proj_kernel(*refs, n_out):
    x_ref, g_ref = refs[0], refs[1]
    w_refs = refs[2:2 + n_out]
    o_refs = refs[2 + n_out:]
    h = _rms(x_ref[...], g_ref[...]).astype(BF16)
    for w_ref, o_ref in zip(w_refs, o_refs):
        o_ref[...] = jnp.dot(h, w_ref[...], preferred_element_type=F32).astype(o_ref.dtype)


def _norm_proj(x2, g, ws, out_dtypes, block):
    t, d = x2.shape
    block = min(block, t)
    in_specs = [pl.BlockSpec((block, d), lambda i: (i, 0)), _full((1, d))]
    in_specs += [_full(w.shape) for w in ws]
    out_specs = [pl.BlockSpec((block, w.shape[1]), lambda i: (i, 0)) for w in ws]
    out_shape = [jax.ShapeDtypeStruct((t, w.shape[1]), dt) for w, dt in zip(ws, out_dtypes)]
    return pl.pallas_call(
        functools.partial(_norm_proj_kernel, n_out=len(ws)),
        grid=(t // block,), in_specs=in_specs, out_specs=out_specs, out_shape=out_shape,
        compiler_params=_params(("parallel",)),
    )(x2, g.reshape(1, d), *ws)


def _seg_sum(x, mseg):
    hi = x.astype(BF16)
    lo = (x - hi.astype(F32)).astype(BF16)
    return (jnp.dot(hi, mseg, preferred_element_type=F32)
            + jnp.dot(lo, mseg, preferred_element_type=F32))


def _col_bcast(row):
    return jnp.broadcast_to(row, (LANES, LANES)).T


def _rwkv_kernel(*refs, first_layer):
    if first_layer:
        (z_ref, mu_ref, w0_ref, w2_ref, a0_ref, a2_ref, g2_ref, kk_ref, ka_ref, rk_ref,
         lng_ref, lnb_ref, mseg_ref, y_ref, vf_out_ref, zprev, hst) = refs
    else:
        (z_ref, hv_ref, vf_ref, v0_ref, v2_ref, mu_ref, w0_ref, w2_ref, a0_ref, a2_ref, g2_ref,
         kk_ref, ka_ref, rk_ref, lng_ref, lnb_ref, mseg_ref, y_ref, zprev, hst) = refs
    L = RW_CHUNK

    @pl.when(pl.program_id(1) == 0)
    def _():
        zprev[...] = jnp.zeros_like(zprev)
        hst[...] = jnp.zeros_like(hst)

    z = z_ref[0]
    row = lax.broadcasted_iota(jnp.int32, (L, 1), 0)
    zs = jnp.where(row == 0, zprev[...], pltpu.roll(z, 1, 0))
    zprev[...] = z[L - 1:L, :]
    z = z + (zs - z) * mu_ref[...]
    r = z[:, 0:512]
    k = z[:, 512:1024]
    v = z[:, 1024:1536]
    wa = z[:, 1536:1664]
    gd = z[:, 1664:1792]
    mseg = mseg_ref[...]

    wlin = w0_ref[...] + _bdot(jnp.tanh(wa), w2_ref[...])
    lw = -jnp.exp(-_softplus(-wlin) - 0.5)
    a = _sigmoid(a0_ref[...] + _bdot(wa, a2_ref[...]))
    g = _bdot(_sigmoid(gd), g2_ref[...])
    if first_layer:
        vf_out_ref[0] = v
    else:
        v = v + (vf_ref[0] - v) * _sigmoid(v0_ref[...] + _bdot(hv_ref[0], v2_ref[...]))
    kk = k * kk_ref[...]
    kk = kk / jnp.maximum(jnp.sqrt(_seg_sum(kk * kk, mseg)), 1e-12)
    k2 = k * (1.0 + (a - 1.0) * ka_ref[...])
    av = -kk
    bv = kk * a

    ti = lax.broadcasted_iota(jnp.int32, (L, L), 0)
    si = lax.broadcasted_iota(jnp.int32, (L, L), 1)
    tril = (ti >= si).astype(F32)
    cum = jnp.dot(tril, lw, preferred_element_type=F32, precision=lax.Precision.HIGHEST)
    mid = cum[L // 2 - 1:L // 2, :]
    cm = cum - mid
    ecum = jnp.exp(cm)
    einv = jnp.exp(-cm)
    rt = r * ecum
    kt = k2 * einv
    bt = bv * einv
    at = av * jnp.exp(cm - lw)
    p_end = jnp.exp(cum[L - 1:L, :])
    e_end = ecum[L - 1:L, :]
    e_mid = jnp.exp(mid)

    lane = lax.broadcasted_iota(jnp.int32, (1, LANES), 1)
    m0 = (lane < RW_HEAD).astype(F32)
    m1 = 1.0 - m0
    strict = ti > si
    incl = ti >= si
    bi = lax.broadcasted_iota(jnp.int32, (LANES, LANES), 0) < RW_HEAD
    bj = lax.broadcasted_iota(jnp.int32, (LANES, LANES), 1) < RW_HEAD
    bdmask = (bi == bj).astype(F32)
    zeros_ll = jnp.zeros((L, L), F32)

    ys = []
    for p in range(D_RWKV // LANES):
        sl = slice(LANES * p, LANES * (p + 1))
        A, B, K, R, V = at[:, sl], bt[:, sl], kt[:, sl], rt[:, sl], v[:, sl]
        btkt = jnp.concatenate([B.T, K.T], axis=1)
        h0 = hst[p]
        h0m = h0 * _col_bcast(e_mid[:, sl])
        sc = _bdot(jnp.concatenate([A * m0, A * m1, R * m0, R * m1], axis=0), btkt)
        aab = [jnp.where(strict, sc[e * L:(e + 1) * L, 0:L], 0.0) for e in range(2)]
        aak = [jnp.where(strict, sc[e * L:(e + 1) * L, L:2 * L], 0.0) for e in range(2)]
        arb = [jnp.where(incl, sc[(2 + e) * L:(3 + e) * L, 0:L], 0.0) for e in range(2)]
        ark = [jnp.where(incl, sc[(2 + e) * L:(3 + e) * L, L:2 * L], 0.0) for e in range(2)]
        arh = _bdot(jnp.concatenate([A, R], axis=0), h0m)
        v01 = jnp.concatenate([V * m0, V * m1], axis=0)
        x = arh[0:L] + _bdot(jnp.concatenate(aak, axis=1), v01)
        pm = jnp.concatenate(aab, axis=1)
        n_fac = int(math.log2(L))
        for it in range(n_fac):
            x = x + _bdot(pm, jnp.concatenate([x * m0, x * m1], axis=0))
            if it + 1 < n_fac:
                pd = jnp.concatenate(
                    [jnp.concatenate([pm[:, 0:L], zeros_ll], axis=1),
                     jnp.concatenate([zeros_ll, pm[:, L:2 * L]], axis=1)], axis=0)
                pm = _bdot(pm, pd)
        u = x
        yp = arh[L:2 * L] + _bdot(jnp.concatenate(arb + ark, axis=1),
                                  jnp.concatenate([u * m0, u * m1, v01], axis=0))
        upd = _bdot(btkt, jnp.concatenate([u, V], axis=0))
        hst[p] = (h0 * _col_bcast(p_end[:, sl]) + upd * _col_bcast(e_end[:, sl])) * bdmask
        ys.append(yp)
    y = jnp.concatenate(ys, axis=1)

    mean = _seg_sum(y, mseg) * (1.0 / RW_HEAD)
    d = y - mean
    var = _seg_sum(d * d, mseg) * (1.0 / RW_HEAD)
    yn = d * lax.rsqrt(var + GN_EPS) * lng_ref[...] + lnb_ref[...]
    bonus = _seg_sum(r * k2 * rk_ref[...], mseg) * v
    y_ref[0] = (yn + bonus) * g


def _pad_rows(w, rows, offset):
    out = jnp.zeros((rows, w.shape[1]), w.dtype)
    return out.at[offset:offset + w.shape[0]].set(w)


def _rwkv(z_rw, hv, v_first, prm, bsz, seq):
    L = RW_CHUNK
    first = v_first is None
    row = lambda a: a.reshape(1, -1).astype(F32)
    hid = jnp.arange(D_RWKV) // RW_HEAD
    mseg = (hid[:, None] == hid[None, :]).astype(BF16)
    w2p = _pad_rows(prm['w2'], LANES, 0).astype(BF16)
    a2p = _pad_rows(prm['a2'], LANES, 64).astype(BF16)
    common = [row(prm['mu']), row(prm['w0']), w2p, row(prm['a0']), a2p, prm['g2'].astype(BF16),
              row(prm['kk']), row(prm['ka']), row(prm['rk']), row(prm['lng']), row(prm['lnb']), mseg]
    tok = lambda w: pl.BlockSpec((1, L, w), lambda b, t: (b, t, 0))
    common_specs = [_full(c.shape) for c in common]
    y_shape = jax.ShapeDtypeStruct((bsz, seq, D_RWKV), F32)
    scratch = [pltpu.VMEM((1, RW_COLS), F32), pltpu.VMEM((D_RWKV // LANES, LANES, LANES), F32)]
    if first:
        args = [z_rw] + common
        in_specs = [tok(RW_COLS)] + common_specs
        out_shape = [y_shape, y_shape]
        out_specs = [tok(D_RWKV), tok(D_RWKV)]
    else:
        v2p = _pad_rows(prm['v2'], LANES, 0).astype(BF16)
        extra = [row(prm['v0']), v2p]
        args = [z_rw, hv, v_first] + extra + common
        in_specs = [tok(RW_COLS), tok(LANES), tok(D_RWKV)] + [_full(c.shape) for c in extra] + common_specs
        out_shape = [y_shape]
        out_specs = [tok(D_RWKV)]
    outs = pl.pallas_call(
        functools.partial(_rwkv_kernel, first_layer=first),
        grid=(bsz, seq // L), in_specs=in_specs, out_specs=out_specs, out_shape=out_shape,
        scratch_shapes=scratch, compiler_params=_params(("parallel", "arbitrary")),
    )(*args)
    return (outs[0], outs[1]) if first else (outs[0], v_first)


def _s5_kernel(u_ref, wb_ref, wc_ref, lpr_ref, lpi_ref, d_ref, gw_ref, gb_ref, og_ref, o_ref,
               car_re, car_im, xre, xim):
    tb = u_ref.shape[1]

    @pl.when(pl.program_id(1) == 0)
    def _():
        car_re[...] = jnp.zeros_like(car_re)
        car_im[...] = jnp.zeros_like(car_im)

    u = u_ref[0]
    bu = _bdot(u, wb_ref[...])
    xre[...] = bu[:, 0:S5_MODES]
    xim[...] = bu[:, S5_MODES:2 * S5_MODES]
    row = lax.broadcasted_iota(jnp.int32, (SUBLANES, 1), 0)

    def tile(i, carry):
        cr, ci = carry
        rs = pl.ds(pl.multiple_of(i * SUBLANES, SUBLANES), SUBLANES)
        br, bi = xre[rs, :], xim[rs, :]
        for dist in (1, 2, 4):
            keep = row >= dist
            sr = jnp.where(keep, pltpu.roll(br, dist, 0), 0.0)
            si = jnp.where(keep, pltpu.roll(bi, dist, 0), 0.0)
            lr = lpr_ref[dist - 1:dist, :]
            li = lpi_ref[dist - 1:dist, :]
            br, bi = br + lr * sr - li * si, bi + lr * si + li * sr
        pr, pi = lpr_ref[...], lpi_ref[...]
        xr = br + pr * cr - pi * ci
        xi = bi + pr * ci + pi * cr
        xre[rs, :] = xr
        xim[rs, :] = xi
        return xr[SUBLANES - 1:SUBLANES, :], xi[SUBLANES - 1:SUBLANES, :]

    cr, ci = lax.fori_loop(0, tb // SUBLANES, tile, (car_re[...], car_im[...]))
    car_re[...] = cr
    car_im[...] = ci
    wc = wc_ref[...]
    y = _bdot(xre[...], wc[0:S5_MODES]) + _bdot(xim[...], wc[S5_MODES:2 * S5_MODES])
    y = _gelu(y + d_ref[...] * u)
    y = y * _sigmoid(_bdot(y, gw_ref[...]) + gb_ref[...])
    o_ref[0] = _rms(y, og_ref[...])


def _s5_weights(a_re, a_im, log_dt, b_re, b_im, c_re, c_im):
    lam_re = jnp.minimum(a_re.astype(F32), -1e-4)
    lam_im = a_im.astype(F32)
    dt = jnp.exp(log_dt.astype(F32))[:, None]
    mag = jnp.exp(lam_re * dt)
    lb_re = mag * jnp.cos(lam_im * dt)
    lb_im = mag * jnp.sin(lam_im * dt)
    den = lam_re * lam_re + lam_im * lam_im
    c1_re = ((lb_re - 1.0) * lam_re + lb_im * lam_im) / den
    c1_im = (lb_im * lam_re - (lb_re - 1.0) * lam_im) / den
    br, bi = b_re.astype(F32), b_im.astype(F32)
    bb_re = c1_re[..., None] * br - c1_im[..., None] * bi
    bb_im = c1_re[..., None] * bi + c1_im[..., None] * br
    eye = jnp.eye(S5_GROUPS, dtype=F32)
    wb_re = jnp.einsum('gpc,gh->gchp', bb_re, eye).reshape(D_S5, S5_MODES)
    wb_im = jnp.einsum('gpc,gh->gchp', bb_im, eye).reshape(D_S5, S5_MODES)
    wb = jnp.concatenate([wb_re, wb_im], axis=1).astype(BF16)
    wc_re = jnp.einsum('gcp,gh->gphc', c_re.astype(F32), eye).reshape(S5_MODES, D_S5)
    wc_im = jnp.einsum('gcp,gh->gphc', c_im.astype(F32), eye).reshape(S5_MODES, D_S5)
    wc = jnp.concatenate([wc_re, -wc_im], axis=0).astype(BF16)
    pr, pi = [lb_re], [lb_im]
    for _ in range(SUBLANES - 1):
        pr, pi = pr + [pr[-1] * lb_re - pi[-1] * lb_im], pi + [pr[-1] * lb_im + pi[-1] * lb_re]
    lp_re = jnp.stack(pr).reshape(SUBLANES, S5_MODES)
    lp_im = jnp.stack(pi).reshape(SUBLANES, S5_MODES)
    return wb, wc, lp_re, lp_im


def _s5(u, prm, bsz, seq):
    tb = min(S5_BLOCK, seq)
    wb, wc, lp_re, lp_im = _s5_weights(prm['a_re'], prm['a_im'], prm['log_dt'], prm['b_re'], prm['b_im'],
                                       prm['c_re'], prm['c_im'])
    row = lambda a: a.reshape(1, -1).astype(F32)
    consts = [wb, wc, lp_re, lp_im, row(prm['d']), prm['glu_w'].astype(BF16), row(prm['glu_b']),
              row(prm['out_g'])]
    tok = pl.BlockSpec((1, tb, D_S5), lambda b, t: (b, t, 0))
    return pl.pallas_call(
        _s5_kernel, grid=(bsz, seq // tb),
        in_specs=[tok] + [_full(c.shape) for c in consts], out_specs=tok,
        out_shape=jax.ShapeDtypeStruct((bsz, seq, D_S5), F32),
        scratch_shapes=[pltpu.VMEM((1, S5_MODES), F32), pltpu.VMEM((1, S5_MODES), F32),
                        pltpu.VMEM((tb, S5_MODES), F32), pltpu.VMEM((tb, S5_MODES), F32)],
        compiler_params=_params(("parallel", "arbitrary")),
    )(u, *consts)


def _mix_xattn_kernel(x_ref, yr_ref, ys_ref, wo1_ref, wo2_ref, g_ref, wq_ref, k_ref, v_ref, wo_ref, o_ref):
    x1 = x_ref[0] + _bdot(yr_ref[0], wo1_ref[...]) + _bdot(ys_ref[0], wo2_ref[...])
    h = _rms(x1, g_ref[...])
    q = _bdot(h, wq_ref[...])
    km, vm = k_ref[0], v_ref[0]
    outs = []
    for hd in range(XA_HEADS):
        sl = slice(XA_HEAD * hd, XA_HEAD * (hd + 1))
        s = _bdot_nt(q[:, sl], km[:, sl]) * (XA_HEAD ** -0.5)
        s = s - jnp.max(s, axis=-1, keepdims=True)
        e = jnp.exp(s)
        p = e / jnp.sum(e, axis=-1, keepdims=True)
        outs.append(_bdot(p, vm[:, sl]))
    o = jnp.concatenate(outs, axis=1)
    o_ref[0] = x1 + _bdot(o, wo_ref[...])


def _mix_xattn(x, y_rw, y_s5, w_out, g, wq, kmem, vmem, wo, bsz, seq):
    tm = min(XA_BLOCK, seq)
    consts_a = [w_out[:D_RWKV].astype(BF16), w_out[D_RWKV:].astype(BF16), g.reshape(1, -1).astype(F32),
                wq.astype(BF16)]
    tok = lambda w: pl.BlockSpec((1, tm, w), lambda b, t: (b, t, 0))
    mem = pl.BlockSpec((1, N_MEM, D_MODEL), lambda b, t: (b, 0, 0))
    wo_b = wo.astype(BF16)
    return pl.pallas_call(
        _mix_xattn_kernel, grid=(bsz, seq // tm),
        in_specs=[tok(D_MODEL), tok(D_RWKV), tok(D_S5)] + [_full(c.shape) for c in consts_a]
                 + [mem, mem, _full(wo_b.shape)],
        out_specs=tok(D_MODEL), out_shape=jax.ShapeDtypeStruct((bsz, seq, D_MODEL), F32),
        compiler_params=_params(("parallel", "parallel")),
    )(x, y_rw, y_s5, *consts_a, kmem, vmem, wo_b)


def _top_rows(work, iota, n, aux, val_ref, idx_ref):
    for it in range(PEER_TOPK):
        m = jnp.max(work, axis=0, keepdims=True)
        pos = jnp.min(jnp.where(work == m, iota, float(n)), axis=0, keepdims=True)
        hit = iota == pos
        val_ref[it:it + 1, :] = m
        if aux is None:
            idx_ref[it:it + 1, :] = pos
        else:
            idx_ref[it:it + 1, :] = jnp.sum(jnp.where(hit, aux, 0.0), axis=0, keepdims=True)
        work = jnp.where(hit, -jnp.inf, work)


def _peer_select_kernel(x_ref, g_ref, wq_ref, keys_ref, h_ref, idx_ref, gate_ref,
                        q3, s1, i1, s2, i2, cand, cidx, top, eid, idx_t, gate_t):
    tm = x_ref.shape[0]
    h = _rms(x_ref[...], g_ref[...])
    h_ref[...] = h
    q = _bdot(h, wq_ref[...])
    for j in range(2 * PEER_HEADS):
        q3[j] = q[:, LANES * j:LANES * (j + 1)]
    iota_k = lax.broadcasted_iota(jnp.int32, (PEER_NKEYS, tm), 0).astype(F32)
    n_cand = PEER_TOPK * PEER_TOPK
    iota_c = lax.broadcasted_iota(jnp.int32, (n_cand, tm), 0).astype(F32)

    def head(hd, _):
        sc1 = _bdot_nt(keys_ref[2 * hd], q3[2 * hd])
        sc2 = _bdot_nt(keys_ref[2 * hd + 1], q3[2 * hd + 1])
        _top_rows(sc1, iota_k, PEER_NKEYS, None, s1, i1)
        _top_rows(sc2, iota_k, PEER_NKEYS, None, s2, i2)
        s2v, i2v = s2[...], i2[...]
        for a in range(PEER_TOPK):
            rs = slice(PEER_TOPK * a, PEER_TOPK * (a + 1))
            cand[rs, :] = s1[a:a + 1, :] + s2v
            cidx[rs, :] = i1[a:a + 1, :] * float(PEER_NKEYS) + i2v
        _top_rows(cand[...], iota_c, n_cand, cidx[...], top, eid)
        tv = top[...]
        e = jnp.exp(tv - jnp.max(tv, axis=0, keepdims=True))
        gate = e / jnp.sum(e, axis=0, keepdims=True)
        rs = pl.ds(pl.multiple_of(hd * PEER_TOPK, PEER_TOPK), PEER_TOPK)
        idx_t[rs, :] = eid[...]
        gate_t[rs, :] = gate
        return 0

    lax.fori_loop(0, PEER_HEADS, head, 0)
    idx_ref[...] = idx_t[...].T.astype(jnp.int32)
    gate_ref[...] = gate_t[...].T


def _peer_select(x2, g, wq, keys):
    t = x2.shape[0]
    tm = SEL_BLOCK
    keys_b = keys.reshape(2 * PEER_HEADS, PEER_NKEYS, LANES).astype(BF16)
    wq_b = wq.astype(BF16)
    tokspec = lambda w: pl.BlockSpec((tm, w), lambda i: (i, 0))
    vm = lambda r: pltpu.VMEM((r, tm), F32)
    return pl.pallas_call(
        _peer_select_kernel, grid=(t // tm,),
        in_specs=[tokspec(D_MODEL), _full((1, D_MODEL)), _full(wq_b.shape), _full(keys_b.shape)],
        out_specs=[tokspec(D_MODEL), tokspec(PEER_SEL), tokspec(PEER_SEL)],
        out_shape=[jax.ShapeDtypeStruct((t, D_MODEL), F32), jax.ShapeDtypeStruct((t, PEER_SEL), jnp.int32),
                   jax.ShapeDtypeStruct((t, PEER_SEL), F32)],
        scratch_shapes=[pltpu.VMEM((2 * PEER_HEADS, tm, LANES), F32),
                        vm(PEER_TOPK), vm(PEER_TOPK), vm(PEER_TOPK), vm(PEER_TOPK),
                        vm(PEER_TOPK * PEER_TOPK), vm(PEER_TOPK * PEER_TOPK),
                        vm(PEER_TOPK), vm(PEER_TOPK), vm(PEER_SEL), vm(PEER_SEL)],
        compiler_params=_params(("parallel",)),
    )(x2, g.reshape(1, -1).astype(F32), wq_b, keys_b)


def _pack_table(tab):
    n, d = tab.shape
    bits = lax.bitcast_convert_type(tab.astype(BF16), jnp.uint16).astype(U32)
    bits = bits.reshape(n // 2, 2, d)
    packed = (bits[:, 0, :] << 16) | bits[:, 1, :]
    return packed.reshape(n // 2 * SUBLANES, LANES)


def _expert_row(tab_ref, e):
    base = pl.multiple_of((e >> 1) * SUBLANES, SUBLANES)
    w = tab_ref[pl.ds(base, SUBLANES), :]
    sh = ((e & 1) * 16).astype(U32)
    return lax.bitcast_convert_type((w << sh) & jnp.uint32(0xFFFF0000), F32)


def _table_spec(shape):
    return pl.BlockSpec(shape, lambda i: (0, 0), pipeline_mode=pl.Buffered(1))


def _peer_u_kernel(idx_ref, h_ref, gate_ref, tab_ref, sel_ref, c_ref, prod):
    tg = gate_ref.shape[0]
    ones = jnp.ones((SUBLANES, LANES), BF16)

    def tok(t, _):
        ht = h_ref[pl.ds(pl.multiple_of(t * SUBLANES, SUBLANES), SUBLANES), :]
        for kx in range(PEER_SEL):
            prod[SUBLANES * kx:SUBLANES * (kx + 1), :] = _expert_row(tab_ref, idx_ref[t, kx]) * ht
        part = jnp.dot(sel_ref[...], prod[...].astype(BF16), preferred_element_type=F32)
        score = _bdot_nt(ones, part)[0:1, :]
        c_ref[pl.ds(t, 1), :] = gate_ref[pl.ds(t, 1), :] * _gelu(score)
        return 0

    lax.fori_loop(0, tg, tok, 0)


def _peer_u(idx, h2, gate, tab):
    t = idx.shape[0]
    tg = GATHER_BLOCK
    h8 = h2.reshape(t * SUBLANES, LANES)
    kid = jnp.arange(PEER_SEL * SUBLANES) // SUBLANES
    sel = (jnp.arange(PEER_SEL)[:, None] == kid[None, :]).astype(BF16)
    return pl.pallas_call(
        _peer_u_kernel, grid=(t // tg,),
        in_specs=[pl.BlockSpec((tg, PEER_SEL), lambda i: (i, 0), memory_space=pltpu.SMEM),
                  pl.BlockSpec((tg * SUBLANES, LANES), lambda i: (i, 0)),
                  pl.BlockSpec((tg, PEER_SEL), lambda i: (i, 0)),
                  _table_spec(tab.shape), _full(sel.shape)],
        out_specs=pl.BlockSpec((tg, PEER_SEL), lambda i: (i, 0)),
        out_shape=jax.ShapeDtypeStruct((t, PEER_SEL), F32),
        scratch_shapes=[pltpu.VMEM((PEER_SEL * SUBLANES, LANES), F32)],
        compiler_params=_params(("parallel",)),
    )(idx, h8, gate, tab, sel)


def _peer_v_kernel(idx_ref, c_ref, x_ref, tab_ref, o_ref):
    tg = idx_ref.shape[0]
    n_acc = 4

    def tok(t, _):
        accs = [jnp.zeros((SUBLANES, LANES), F32) for _ in range(n_acc)]
        for kx in range(PEER_SEL):
            accs[kx % n_acc] = accs[kx % n_acc] + c_ref[t, kx] * _expert_row(tab_ref, idx_ref[t, kx])
        rs = pl.ds(pl.multiple_of(t * SUBLANES, SUBLANES), SUBLANES)
        o_ref[rs, :] = x_ref[rs, :] + ((accs[0] + accs[1]) + (accs[2] + accs[3]))
        return 0

    lax.fori_loop(0, tg, tok, 0)


def _peer_v(idx, c, x2, tab):
    t = idx.shape[0]
    tg = GATHER_BLOCK
    x8 = x2.reshape(t * SUBLANES, LANES)
    smem = pl.BlockSpec((tg, PEER_SEL), lambda i: (i, 0), memory_space=pltpu.SMEM)
    tile = pl.BlockSpec((tg * SUBLANES, LANES), lambda i: (i, 0))
    out = pl.pallas_call(
        _peer_v_kernel, grid=(t // tg,),
        in_specs=[smem, smem, tile, _table_spec(tab.shape)],
        out_specs=tile, out_shape=jax.ShapeDtypeStruct((t * SUBLANES, LANES), F32),
        compiler_params=_params(("parallel",)),
    )(idx, c, x8, tab)
    return out.reshape(t, D_MODEL)


def _final_norm_kernel(x_ref, g_ref, o_ref):
    o_ref[...] = _rms(x_ref[...], g_ref[...])


def _final_norm(x2, g):
    t, d = x2.shape
    tm = min(PROJ_BLOCK, t)
    spec = pl.BlockSpec((tm, d), lambda i: (i, 0))
    return pl.pallas_call(
        _final_norm_kernel, grid=(t // tm,), in_specs=[spec, _full((1, d))], out_specs=spec,
        out_shape=jax.ShapeDtypeStruct((t, d), F32), compiler_params=_params(("parallel",)),
    )(x2, g.reshape(1, d).astype(F32))


def kernel(x, mem, norm_mix, w_in, rw_mu, rw_w0, rw_w2, rw_a0, rw_a2, rw_g2, rw_kk, rw_ka, rw_rk, rw_v0, rw_v1, rw_v2, rw_lnx_g, rw_lnx_b, s5_a_re, s5_a_im, s5_log_dt, s5_b_re, s5_b_im, s5_c_re, s5_c_im, s5_d, s5_glu_w, s5_glu_b, s5_out_g, w_out, norm_xa, norm_mem, xa_wq, xa_wk, xa_wv, xa_wo, norm_ffn, peer_wq, peer_keys, peer_u, peer_v, norm_final):
    bsz, seq, d = x.shape
    t = bsz * seq
    depth = w_in.shape[0]
    mem2 = mem.reshape(bsz * N_MEM, d)
    v_first = None
    for l in range(depth):
        x2 = x.reshape(t, d)
        w_rw = w_in[l][:, :RW_COLS].astype(BF16)
        w_s5 = w_in[l][:, RW_COLS:].astype(BF16)
        ws = [w_rw, w_s5]
        if l > 0:
            ws.append(_pad_rows(rw_v1[l - 1].T, LANES, 0).T.astype(BF16))
        outs = _norm_proj(x2, norm_mix[l], ws, [F32] * len(ws), PROJ_BLOCK)
        z_rw = outs[0].reshape(bsz, seq, RW_COLS)
        u_s5 = outs[1].reshape(bsz, seq, D_S5)
        hv = outs[2].reshape(bsz, seq, LANES) if l > 0 else None
        rw_prm = dict(mu=rw_mu[l], w0=rw_w0[l], w2=rw_w2[l], a0=rw_a0[l], a2=rw_a2[l], g2=rw_g2[l],
                      kk=rw_kk[l], ka=rw_ka[l], rk=rw_rk[l], lng=rw_lnx_g[l], lnb=rw_lnx_b[l])
        if l > 0:
            rw_prm.update(v0=rw_v0[l - 1], v2=rw_v2[l - 1])
        y_rw, v_first = _rwkv(z_rw, hv, v_first, rw_prm, bsz, seq)
        s5_prm = dict(a_re=s5_a_re[l], a_im=s5_a_im[l], log_dt=s5_log_dt[l], b_re=s5_b_re[l], b_im=s5_b_im[l],
                      c_re=s5_c_re[l], c_im=s5_c_im[l], d=s5_d[l], glu_w=s5_glu_w[l], glu_b=s5_glu_b[l],
                      out_g=s5_out_g[l])
        y_s5 = _s5(u_s5, s5_prm, bsz, seq)
        kv = _norm_proj(mem2, norm_mem[l], [xa_wk[l].astype(BF16), xa_wv[l].astype(BF16)], [BF16, BF16],
                        PROJ_BLOCK)
        kmem = kv[0].reshape(bsz, N_MEM, d)
        vmem = kv[1].reshape(bsz, N_MEM, d)
        x = _mix_xattn(x, y_rw, y_s5, w_out[l], norm_xa[l], xa_wq[l], kmem, vmem, xa_wo[l], bsz, seq)
        x2 = x.reshape(t, d)
        h3, idx, gate = _peer_select(x2, norm_ffn[l], peer_wq[l], peer_keys[l])
        c = _peer_u(idx, h3, gate, _pack_table(peer_u[l]))
        x = _peer_v(idx, c, x2, _pack_table(peer_v[l])).reshape(bsz, seq, d)
    return _final_norm(x.reshape(t, d), norm_final).reshape(bsz, seq, d)
```

```python
import functools
import math

import jax
import jax.numpy as jnp
from jax import lax
from jax.experimental import pallas as pl
from jax.experimental.pallas import tpu as pltpu

F32 = jnp.float32
BF16 = jnp.bfloat16
U32 = jnp.uint32

LANES = 128
SUBLANES = 8
VMEM_LIMIT = 56 * 1024 * 1024

D_MODEL = 1024
D_RWKV = 512
RW_HEAD = 64
RW_COLS = 1792
D_S5 = 512
S5_GROUPS = 32
S5_CH = 16
S5_STATE = 64
S5_MODES = S5_GROUPS * S5_STATE
N_MEM = 256
XA_HEADS = 4
XA_HEAD = 256
PEER_HEADS = 8
PEER_NKEYS = 128
PEER_TOPK = 16
PEER_SEL = PEER_HEADS * PEER_TOPK
RMS_EPS = 1e-6
GN_EPS = 64e-5

RW_CHUNK = 128
S5_BLOCK = 256
PROJ_BLOCK = 512
XA_BLOCK = 256
SEL_BLOCK = 128
GATHER_BLOCK = 64


def _params(sem):
    return pltpu.CompilerParams(dimension_semantics=sem, vmem_limit_bytes=VMEM_LIMIT)


def _rms(x, g):
    ms = jnp.mean(x * x, axis=-1, keepdims=True)
    return x * lax.rsqrt(ms + RMS_EPS) * g


def _bdot(a, b):
    return jnp.dot(a.astype(BF16), b.astype(BF16), preferred_element_type=F32)


def _bdot_nt(a, b):
    return lax.dot_general(a.astype(BF16), b.astype(BF16), (((1,), (1,)), ((), ())),
                           preferred_element_type=F32)


def _sigmoid(x):
    return 1.0 / (1.0 + jnp.exp(-x))


def _softplus(x):
    return jnp.maximum(x, 0.0) + jnp.log(1.0 + jnp.exp(-jnp.abs(x)))


def _gelu(x):
    return 0.5 * x * (1.0 + jnp.tanh(math.sqrt(2.0 / math.pi) * (x + 0.044715 * (x * x * x))))


def _full(shape):
    n = len(shape)
    return pl.BlockSpec(shape, lambda *_: (0,) * n)


def _norm_proj_kernel(*refs, n_out):
    x_ref, g_ref = refs[0], refs[1]
    w_refs = refs[2:2 + n_out]
    o_refs = refs[2 + n_out:]
    h = _rms(x_ref[...], g_ref[...]).astype(BF16)
    for w_ref, o_ref in zip(w_refs, o_refs):
        o_ref[...] = jnp.dot(h, w_ref[...], preferred_element_type=F32).astype(o_ref.dtype)


def _norm_proj(x2, g, ws, out_dtypes, block):
    t, d = x2.shape
    block = min(block, t)
    in_specs = [pl.BlockSpec((block, d), lambda i: (i, 0)), _full((1, d))]
    in_specs += [_full(w.shape) for w in ws]
    out_specs = [pl.BlockSpec((block, w.shape[1]), lambda i: (i, 0)) for w in ws]
    out_shape = [jax.ShapeDtypeStruct((t, w.shape[1]), dt) for w, dt in zip(ws, out_dtypes)]
    return pl.pallas_call(
        functools.partial(_norm_proj_kernel, n_out=len(ws)),
        grid=(t // block,), in_specs=in_specs, out_specs=out_specs, out_shape=out_shape,
        compiler_params=_params(("parallel",)),
    )(x2, g.reshape(1, d), *ws)


def _seg_sum(x, mseg):
    hi = x.astype(BF16)
    lo = (x - hi.astype(F32)).astype(BF16)
    return (jnp.dot(hi, mseg, preferred_element_type=F32)
            + jnp.dot(lo, mseg, preferred_element_type=F32))


def _col_bcast(row):
    return jnp.broadcast_to(row, (LANES, LANES)).T


def _rwkv_kernel(*refs, first_layer):
    if first_layer:
        (z_ref, mu_ref, w0_ref, w2_ref, a0_ref, a2_ref, g2_ref, kk_ref, ka_ref, rk_ref,
         lng_ref, lnb_ref, mseg_ref, y_ref, vf_out_ref, zprev, hst) = refs
    else:
        (z_ref, hv_ref, vf_ref, v0_ref, v2_ref, mu_ref, w0_ref, w2_ref, a0_ref, a2_ref, g2_ref,
         kk_ref, ka_ref, rk_ref, lng_ref, lnb_ref, mseg_ref, y_ref, zprev, hst) = refs
    L = RW_CHUNK

    @pl.when(pl.program_id(1) == 0)
    def _():
        zprev[...] = jnp.zeros_like(zprev)
        hst[...] = jnp.zeros_like(hst)

    z = z_ref[0]
    row = lax.broadcasted_iota(jnp.int32, (L, 1), 0)
    zs = jnp.where(row == 0, zprev[...], pltpu.roll(z, 1, 0))
    zprev[...] = z[L - 1:L, :]
    z = z + (zs - z) * mu_ref[...]
    r = z[:, 0:512]
    k = z[:, 512:1024]
    v = z[:, 1024:1536]
    wa = z[:, 1536:1664]
    gd = z[:, 1664:1792]
    mseg = mseg_ref[...]

    wlin = w0_ref[...] + _bdot(jnp.tanh(wa), w2_ref[...])
    lw = -jnp.exp(-_softplus(-wlin) - 0.5)
    a = _sigmoid(a0_ref[...] + _bdot(wa, a2_ref[...]))
    g = _bdot(_sigmoid(gd), g2_ref[...])
    if first_layer:
        vf_out_ref[0] = v
    else:
        v = v + (vf_ref[0] - v) * _sigmoid(v0_ref[...] + _bdot(hv_ref[0], v2_ref[...]))
    kk = k * kk_ref[...]
    kk = kk / jnp.maximum(jnp.sqrt(_seg_sum(kk * kk, mseg)), 1e-12)
    k2 = k * (1.0 + (a - 1.0) * ka_ref[...])
    av = -kk
    bv = kk * a

    ti = lax.broadcasted_iota(jnp.int32, (L, L), 0)
    si = lax.broadcasted_iota(jnp.int32, (L, L), 1)
    tril = (ti >= si).astype(F32)
    cum = jnp.dot(tril, lw, preferred_element_type=F32, precision=lax.Precision.HIGHEST)
    mid = cum[L // 2 - 1:L // 2, :]
    cm = cum - mid
    ecum = jnp.exp(cm)
    einv = jnp.exp(-cm)
    rt = r * ecum
    kt = k2 * einv
    bt = bv * einv
    at = av * jnp.exp(cm - lw)
    p_end = jnp.exp(cum[L - 1:L, :])
    e_end = ecum[L - 1:L, :]
    e_mid = jnp.exp(mid)

    lane = lax.broadcasted_iota(jnp.int32, (1, LANES), 1)
    m0 = (lane < RW_HEAD).astype(F32)
    m1 = 1.0 - m0
    strict = ti > si
    incl = ti >= si
    bi = lax.broadcasted_iota(jnp.int32, (LANES, LANES), 0) < RW_HEAD
    bj = lax.broadcasted_iota(jnp.int32, (LANES, LANES), 1) < RW_HEAD
    bdmask = (bi == bj).astype(F32)
    zeros_ll = jnp.zeros((L, L), F32)

    ys = []
    for p in range(D_RWKV // LANES):
        sl = slice(LANES * p, LANES * (p + 1))
        A, B, K, R, V = at[:, sl], bt[:, sl], kt[:, sl], rt[:, sl], v[:, sl]
        btkt = jnp.concatenate([B.T, K.T], axis=1)
        h0 = hst[p]
        h0m = h0 * _col_bcast(e_mid[:, sl])
        sc = _bdot(jnp.concatenate([A * m0, A * m1, R * m0, R * m1], axis=0), btkt)
        aab = [jnp.where(strict, sc[e * L:(e + 1) * L, 0:L], 0.0) for e in range(2)]
        aak = [jnp.where(strict, sc[e * L:(e + 1) * L, L:2 * L], 0.0) for e in range(2)]
        arb = [jnp.where(incl, sc[(2 + e) * L:(3 + e) * L, 0:L], 0.0) for e in range(2)]
        ark = [jnp.where(incl, sc[(2 + e) * L:(3 + e) * L, L:2 * L], 0.0) for e in range(2)]
        arh = _bdot(jnp.concatenate([A, R], axis=0), h0m)
        v01 = jnp.concatenate([V * m0, V * m1], axis=0)
        x = arh[0:L] + _bdot(jnp.concatenate(aak, axis=1), v01)
        pm = jnp.concatenate(aab, axis=1)
        n_fac = int(math.log2(L))
        for it in range(n_fac):
            x = x + _bdot(pm, jnp.concatenate([x * m0, x * m1], axis=0))
            if it + 1 < n_fac:
                pd = jnp.concatenate(
                    [jnp.concatenate([pm[:, 0:L], zeros_ll], axis=1),
                     jnp.concatenate([zeros_ll, pm[:, L:2 * L]], axis=1)], axis=0)
                pm = _bdot(pm, pd)
        u = x
        yp = arh[L:2 * L] + _bdot(jnp.concatenate(arb + ark, axis=1),
                                  jnp.concatenate([u * m0, u * m1, v01], axis=0))
        upd = _bdot(btkt, jnp.concatenate([u, V], axis=0))
        hst[p] = (h0 * _col_bcast(p_end[:, sl]) + upd * _col_bcast(e_end[:, sl])) * bdmask
        ys.append(yp)
    y = jnp.concatenate(ys, axis=1)

    mean = _seg_sum(y, mseg) * (1.0 / RW_HEAD)
    d = y - mean
    var = _seg_sum(d * d, mseg) * (1.0 / RW_HEAD)
    yn = d * lax.rsqrt(var + GN_EPS) * lng_ref[...] + lnb_ref[...]
    bonus = _seg_sum(r * k2 * rk_ref[...], mseg) * v
    y_ref[0] = (yn + bonus) * g


def _pad_rows(w, rows, offset):
    out = jnp.zeros((rows, w.shape[1]), w.dtype)
    return out.at[offset:offset + w.shape[0]].set(w)


def _rwkv(z_rw, hv, v_first, prm, bsz, seq):
    L = RW_CHUNK
    first = v_first is None
    row = lambda a: a.reshape(1, -1).astype(F32)
    hid = jnp.arange(D_RWKV) // RW_HEAD
    mseg = (hid[:, None] == hid[None, :]).astype(BF16)
    w2p = _pad_rows(prm['w2'], LANES, 0).astype(BF16)
    a2p = _pad_rows(prm['a2'], LANES, 64).astype(BF16)
    common = [row(prm['mu']), row(prm['w0']), w2p, row(prm['a0']), a2p, prm['g2'].astype(BF16),
              row(prm['kk']), row(prm['ka']), row(prm['rk']), row(prm['lng']), row(prm['lnb']), mseg]
    tok = lambda w: pl.BlockSpec((1, L, w), lambda b, t: (b, t, 0))
    common_specs = [_full(c.shape) for c in common]
    y_shape = jax.ShapeDtypeStruct((bsz, seq, D_RWKV), F32)
    scratch = [pltpu.VMEM((1, RW_COLS), F32), pltpu.VMEM((D_RWKV // LANES, LANES, LANES), F32)]
    if first:
        args = [z_rw] + common
        in_specs = [tok(RW_COLS)] + common_specs
        out_shape = [y_shape, y_shape]
        out_specs = [tok(D_RWKV), tok(D_RWKV)]
    else:
        v2p = _pad_rows(prm['v2'], LANES, 0).astype(BF16)
        extra = [row(prm['v0']), v2p]
        args = [z_rw, hv, v_first] + extra + common
        in_specs = [tok(RW_COLS), tok(LANES), tok(D_RWKV)] + [_full(c.shape) for c in extra] + common_specs
        out_shape = [y_shape]
        out_specs = [tok(D_RWKV)]
    outs = pl.pallas_call(
        functools.partial(_rwkv_kernel, first_layer=first),
        grid=(bsz, seq // L), in_specs=in_specs, out_specs=out_specs, out_shape=out_shape,
        scratch_shapes=scratch, compiler_params=_params(("parallel", "arbitrary")),
    )(*args)
    return (outs[0], outs[1]) if first else (outs[0], v_first)


def _s5_kernel(u_ref, wb_ref, wc_ref, lpr_ref, lpi_ref, d_ref, gw_ref, gb_ref, og_ref, o_ref,
               car_re, car_im, xre, xim):
    tb = u_ref.shape[1]

    @pl.when(pl.program_id(1) == 0)
    def _():
        car_re[...] = jnp.zeros_like(car_re)
        car_im[...] = jnp.zeros_like(car_im)

    u = u_ref[0]
    bu = _bdot(u, wb_ref[...])
    xre[...] = bu[:, 0:S5_MODES]
    xim[...] = bu[:, S5_MODES:2 * S5_MODES]
    row = lax.broadcasted_iota(jnp.int32, (SUBLANES, 1), 0)

    def tile(i, carry):
        cr, ci = carry
        rs = pl.ds(pl.multiple_of(i * SUBLANES, SUBLANES), SUBLANES)
        br, bi = xre[rs, :], xim[rs, :]
        for dist in (1, 2, 4):
            keep = row >= dist
            sr = jnp.where(keep, pltpu.roll(br, dist, 0), 0.0)
            si = jnp.where(keep, pltpu.roll(bi, dist, 0), 0.0)
            lr = lpr_ref[dist - 1:dist, :]
            li = lpi_ref[dist - 1:dist, :]
            br, bi = br + lr * sr - li * si, bi + lr * si + li * sr
        pr, pi = lpr_ref[...], lpi_ref[...]
        xr = br + pr * cr - pi * ci
        xi = bi + pr * ci + pi * cr
        xre[rs, :] = xr
        xim[rs, :] = xi
        return xr[SUBLANES - 1:SUBLANES, :], xi[SUBLANES - 1:SUBLANES, :]

    cr, ci = lax.fori_loop(0, tb // SUBLANES, tile, (car_re[...], car_im[...]))
    car_re[...] = cr
    car_im[...] = ci
    wc = wc_ref[...]
    y = _bdot(xre[...], wc[0:S5_MODES]) + _bdot(xim[...], wc[S5_MODES:2 * S5_MODES])
    y = _gelu(y + d_ref[...] * u)
    y = y * _sigmoid(_bdot(y, gw_ref[...]) + gb_ref[...])
    o_ref[0] = _rms(y, og_ref[...])


def _s5_weights(a_re, a_im, log_dt, b_re, b_im, c_re, c_im):
    lam_re = jnp.minimum(a_re.astype(F32), -1e-4)
    lam_im = a_im.astype(F32)
    dt = jnp.exp(log_dt.astype(F32))[:, None]
    mag = jnp.exp(lam_re * dt)
    lb_re = mag * jnp.cos(lam_im * dt)
    lb_im = mag * jnp.sin(lam_im * dt)
    den = lam_re * lam_re + lam_im * lam_im
    c1_re = ((lb_re - 1.0) * lam_re + lb_im * lam_im) / den
    c1_im = (lb_im * lam_re - (lb_re - 1.0) * lam_im) / den
    br, bi = b_re.astype(F32), b_im.astype(F32)
    bb_re = c1_re[..., None] * br - c1_im[..., None] * bi
    bb_im = c1_re[..., None] * bi + c1_im[..., None] * br
    eye = jnp.eye(S5_GROUPS, dtype=F32)
    wb_re = jnp.einsum('gpc,gh->gchp', bb_re, eye).reshape(D_S5, S5_MODES)
    wb_im = jnp.einsum('gpc,gh->gchp', bb_im, eye).reshape(D_S5, S5_MODES)
    wb = jnp.concatenate([wb_re, wb_im], axis=1).astype(BF16)
    wc_re = jnp.einsum('gcp,gh->gphc', c_re.astype(F32), eye).reshape(S5_MODES, D_S5)
    wc_im = jnp.einsum('gcp,gh->gphc', c_im.astype(F32), eye).reshape(S5_MODES, D_S5)
    wc = jnp.concatenate([wc_re, -wc_im], axis=0).astype(BF16)
    pr, pi = [lb_re], [lb_im]
    for _ in range(SUBLANES - 1):
        pr, pi = pr + [pr[-1] * lb_re - pi[-1] * lb_im], pi + [pr[-1] * lb_im + pi[-1] * lb_re]
    lp_re = jnp.stack(pr).reshape(SUBLANES, S5_MODES)
    lp_im = jnp.stack(pi).reshape(SUBLANES, S5_MODES)
    return wb, wc, lp_re, lp_im


def _s5(u, prm, bsz, seq):
    tb = min(S5_BLOCK, seq)
    wb, wc, lp_re, lp_im = _s5_weights(prm['a_re'], prm['a_im'], prm['log_dt'], prm['b_re'], prm['b_im'],
                                       prm['c_re'], prm['c_im'])
    row = lambda a: a.reshape(1, -1).astype(F32)
    consts = [wb, wc, lp_re, lp_im, row(prm['d']), prm['glu_w'].astype(BF16), row(prm['glu_b']),
              row(prm['out_g'])]
    tok = pl.BlockSpec((1, tb, D_S5), lambda b, t: (b, t, 0))
    return pl.pallas_call(
        _s5_kernel, grid=(bsz, seq // tb),
        in_specs=[tok] + [_full(c.shape) for c in consts], out_specs=tok,
        out_shape=jax.ShapeDtypeStruct((bsz, seq, D_S5), F32),
        scratch_shapes=[pltpu.VMEM((1, S5_MODES), F32), pltpu.VMEM((1, S5_MODES), F32),
                        pltpu.VMEM((tb, S5_MODES), F32), pltpu.VMEM((tb, S5_MODES), F32)],
        compiler_params=_params(("parallel", "arbitrary")),
    )(u, *consts)


def _mix_xattn_kernel(x_ref, yr_ref, ys_ref, wo1_ref, wo2_ref, g_ref, wq_ref, k_ref, v_ref, wo_ref, o_ref):
    x1 = x_ref[0] + _bdot(yr_ref[0], wo1_ref[...]) + _bdot(ys_ref[0], wo2_ref[...])
    h = _rms(x1, g_ref[...])
    q = _bdot(h, wq_ref[...])
    km, vm = k_ref[0], v_ref[0]
    outs = []
    for hd in range(XA_HEADS):
        sl = slice(XA_HEAD * hd, XA_HEAD * (hd + 1))
        s = _bdot_nt(q[:, sl], km[:, sl]) * (XA_HEAD ** -0.5)
        s = s - jnp.max(s, axis=-1, keepdims=True)
        e = jnp.exp(s)
        p = e / jnp.sum(e, axis=-1, keepdims=True)
        outs.append(_bdot(p, vm[:, sl]))
    o = jnp.concatenate(outs, axis=1)
    o_ref[0] = x1 + _bdot(o, wo_ref[...])


def _mix_xattn(x, y_rw, y_s5, w_out, g, wq, kmem, vmem, wo, bsz, seq):
    tm = min(XA_BLOCK, seq)
    consts_a = [w_out[:D_RWKV].astype(BF16), w_out[D_RWKV:].astype(BF16), g.reshape(1, -1).astype(F32),
                wq.astype(BF16)]
    tok = lambda w: pl.BlockSpec((1, tm, w), lambda b, t: (b, t, 0))
    mem = pl.BlockSpec((1, N_MEM, D_MODEL), lambda b, t: (b, 0, 0))
    wo_b = wo.astype(BF16)
    return pl.pallas_call(
        _mix_xattn_kernel, grid=(bsz, seq // tm),
        in_specs=[tok(D_MODEL), tok(D_RWKV), tok(D_S5)] + [_full(c.shape) for c in consts_a]
                 + [mem, mem, _full(wo_b.shape)],
        out_specs=tok(D_MODEL), out_shape=jax.ShapeDtypeStruct((bsz, seq, D_MODEL), F32),
        compiler_params=_params(("parallel", "parallel")),
    )(x, y_rw, y_s5, *consts_a, kmem, vmem, wo_b)


def _top_rows(work, iota, n, aux, val_ref, idx_ref):
    for it in range(PEER_TOPK):
        m = jnp.max(work, axis=0, keepdims=True)
        pos = jnp.min(jnp.where(work == m, iota, float(n)), axis=0, keepdims=True)
        hit = iota == pos
        val_ref[it:it + 1, :] = m
        if aux is None:
            idx_ref[it:it + 1, :] = pos
        else:
            idx_ref[it:it + 1, :] = jnp.sum(jnp.where(hit, aux, 0.0), axis=0, keepdims=True)
        work = jnp.where(hit, -jnp.inf, work)


def _peer_select_kernel(x_ref, g_ref, wq_ref, keys_ref, h_ref, base_ref, shift_ref, gate_ref,
                        q3, s1, i1, s2, i2, cand, cidx, top, eid, idx_t, gate_t):
    tm = x_ref.shape[0]
    h = _rms(x_ref[...], g_ref[...])
    h_ref[...] = h
    q = _bdot(h, wq_ref[...])
    for j in range(2 * PEER_HEADS):
        q3[j] = q[:, LANES * j:LANES * (j + 1)]
    iota_k = lax.broadcasted_iota(jnp.int32, (PEER_NKEYS, tm), 0).astype(F32)
    n_cand = PEER_TOPK * PEER_TOPK
    iota_c = lax.broadcasted_iota(jnp.int32, (n_cand, tm), 0).astype(F32)

    def head(hd, _):
        sc1 = _bdot_nt(keys_ref[2 * hd], q3[2 * hd])
        sc2 = _bdot_nt(keys_ref[2 * hd + 1], q3[2 * hd + 1])
        _top_rows(sc1, iota_k, PEER_NKEYS, None, s1, i1)
        _top_rows(sc2, iota_k, PEER_NKEYS, None, s2, i2)
        s2v, i2v = s2[...], i2[...]
        for a in range(PEER_TOPK):
            rs = slice(PEER_TOPK * a, PEER_TOPK * (a + 1))
            cand[rs, :] = s1[a:a + 1, :] + s2v
            cidx[rs, :] = i1[a:a + 1, :] * float(PEER_NKEYS) + i2v
        _top_rows(cand[...], iota_c, n_cand, cidx[...], top, eid)
        tv = top[...]
        e = jnp.exp(tv - jnp.max(tv, axis=0, keepdims=True))
        gate = e / jnp.sum(e, axis=0, keepdims=True)
        rs = pl.ds(pl.multiple_of(hd * PEER_TOPK, PEER_TOPK), PEER_TOPK)
        idx_t[rs, :] = eid[...]
        gate_t[rs, :] = gate
        return 0

    lax.fori_loop(0, PEER_HEADS, head, 0)
    e_t = idx_t[...].T
    pair = jnp.floor(e_t * 0.5)
    base_ref[...] = (pair * float(SUBLANES)).astype(jnp.int32)
    shift_ref[...] = (e_t - 2.0 * pair) * 16.0
    gate_ref[...] = gate_t[...].T


def _peer_select(x2, g, wq, keys):
    t = x2.shape[0]
    tm = SEL_BLOCK
    keys_b = keys.reshape(2 * PEER_HEADS, PEER_NKEYS, LANES).astype(BF16)
    wq_b = wq.astype(BF16)
    tokspec = lambda w: pl.BlockSpec((tm, w), lambda i: (i, 0))
    vm = lambda r: pltpu.VMEM((r, tm), F32)
    return pl.pallas_call(
        _peer_select_kernel, grid=(t // tm,),
        in_specs=[tokspec(D_MODEL), _full((1, D_MODEL)), _full(wq_b.shape), _full(keys_b.shape)],
        out_specs=[tokspec(D_MODEL), tokspec(PEER_SEL), tokspec(PEER_SEL), tokspec(PEER_SEL)],
        out_shape=[jax.ShapeDtypeStruct((t, D_MODEL), F32), jax.ShapeDtypeStruct((t, PEER_SEL), jnp.int32),
                   jax.ShapeDtypeStruct((t, PEER_SEL), F32), jax.ShapeDtypeStruct((t, PEER_SEL), F32)],
        scratch_shapes=[pltpu.VMEM((2 * PEER_HEADS, tm, LANES), F32),
                        vm(PEER_TOPK), vm(PEER_TOPK), vm(PEER_TOPK), vm(PEER_TOPK),
                        vm(PEER_TOPK * PEER_TOPK), vm(PEER_TOPK * PEER_TOPK),
                        vm(PEER_TOPK), vm(PEER_TOPK), vm(PEER_SEL), vm(PEER_SEL)],
        compiler_params=_params(("parallel",)),
    )(x2, g.reshape(1, -1).astype(F32), wq_b, keys_b)


def _pack_table(tab):
    n, d = tab.shape
    bits = lax.bitcast_convert_type(tab.astype(BF16), jnp.uint16).astype(U32)
    bits = bits.reshape(n // 2, 2, d)
    packed = (bits[:, 0, :] << 16) | bits[:, 1, :]
    return lax.bitcast_convert_type(packed, jnp.int32).reshape(n // 2 * SUBLANES, LANES)


SPLAT_TOKENS = 8


def _fill_shift_splats(shift_ref, eye_ref, dst_ref):
    ones = jnp.ones((LANES, LANES), BF16)

    def body(i, _):
        for j in range(SPLAT_TOKENS):
            t = i * SPLAT_TOKENS + j
            diag = jnp.broadcast_to(shift_ref[pl.ds(t, 1), :], (LANES, LANES)) * eye_ref[...]
            out = jnp.dot(diag.astype(BF16), ones, preferred_element_type=F32)
            dst_ref[pl.ds(pl.multiple_of(t * LANES, LANES), LANES), :] = out.astype(jnp.int32)
        return 0

    lax.fori_loop(0, shift_ref.shape[0] // SPLAT_TOKENS, body, 0)


def _gather_tiles(base_ref, shift_splat, tab_ref, g_ref, t):
    for pair in range(PEER_SEL // 2):
        vals = []
        for kx in (2 * pair, 2 * pair + 1):
            w = tab_ref[pl.ds(pl.multiple_of(base_ref[t, kx], SUBLANES), SUBLANES), :]
            sh = jnp.broadcast_to(shift_splat[pl.ds(t * PEER_SEL + kx, 1), :], (SUBLANES, LANES))
            vals.append(lax.bitcast_convert_type((w << sh) & jnp.int32(-65536), F32))
        rows = slice(2 * SUBLANES * pair, 2 * SUBLANES * (pair + 1))
        g_ref[rows, :] = jnp.concatenate(vals, axis=0).astype(BF16)


def _split3(x, axis):
    p1 = x.astype(BF16).astype(F32)
    r1 = x - p1
    p2 = r1.astype(BF16).astype(F32)
    return jnp.concatenate([p1, p2, r1 - p2], axis=axis).astype(BF16)


def _token_rows(t):
    return pl.ds(pl.multiple_of(t * SUBLANES, SUBLANES), SUBLANES)


def _table_spec(shape):
    return pl.BlockSpec(shape, lambda i: (0, 0), pipeline_mode=pl.Buffered(1))


GATHER_UNROLL = 2
N_CHUNK = D_MODEL // LANES


def _gather_consts():
    j = jnp.arange(PEER_SEL * N_CHUNK)
    eye = jnp.eye(LANES, dtype=F32)
    dmask = (j[None, :] % N_CHUNK == jnp.arange(N_CHUNK)[:, None]).astype(F32)
    gsum = (j[:, None] // N_CHUNK == jnp.arange(PEER_SEL)[None, :]).astype(BF16)
    return eye, dmask, gsum


def _peer_u_kernel(base_ref, shift_ref, h_ref, gate_ref, tab_ref, eye_ref, dmask_ref, gsum_ref, c_ref,
                   shift_splat, part, *gs):
    tg = gate_ref.shape[0]
    _fill_shift_splats(shift_ref, eye_ref, shift_splat)
    dmask = dmask_ref[...]

    def toks(i, _):
        for j, g in enumerate(gs):
            _gather_tiles(base_ref, shift_splat, tab_ref, g, i * len(gs) + j)
        for j, g in enumerate(gs):
            t = i * len(gs) + j
            o = _bdot_nt(h_ref[_token_rows(t), :], g[...])
            part[pl.ds(t, 1), :] = jnp.sum(o * dmask, axis=0, keepdims=True)
        return 0

    lax.fori_loop(0, tg // len(gs), toks, 0)
    p = part[...]
    hi = p.astype(BF16)
    lo = (p - hi.astype(F32)).astype(BF16)
    score = (jnp.dot(hi, gsum_ref[...], preferred_element_type=F32)
             + jnp.dot(lo, gsum_ref[...], preferred_element_type=F32))
    c_ref[...] = gate_ref[...] * _gelu(score)


def _gather_scratch(tg):
    return ([pltpu.VMEM((tg * PEER_SEL, LANES), jnp.int32), pltpu.VMEM((tg, PEER_SEL * N_CHUNK), F32)]
            + [pltpu.VMEM((PEER_SEL * N_CHUNK, LANES), BF16)] * GATHER_UNROLL)


def _peer_u(base, shift, h2, gate, tab):
    t = base.shape[0]
    tg = GATHER_BLOCK
    h8 = h2.reshape(t * SUBLANES, LANES)
    consts = _gather_consts()
    tokrow = pl.BlockSpec((tg, PEER_SEL), lambda i: (i, 0))
    return pl.pallas_call(
        _peer_u_kernel, grid=(t // tg,),
        in_specs=[pl.BlockSpec((tg, PEER_SEL), lambda i: (i, 0), memory_space=pltpu.SMEM), tokrow,
                  pl.BlockSpec((tg * SUBLANES, LANES), lambda i: (i, 0)), tokrow,
                  _table_spec(tab.shape)] + [_full(c.shape) for c in consts],
        out_specs=tokrow, out_shape=jax.ShapeDtypeStruct((t, PEER_SEL), F32),
        scratch_shapes=_gather_scratch(tg),
        compiler_params=_params(("parallel",)),
    )(base, shift, h8, gate, tab, *consts)


def _peer_v_kernel(base_ref, shift_ref, c_ref, x_ref, tab_ref, eye_ref, dmask_ref, expand_ref, o_ref,
                   shift_splat, crep, *gs):
    tg = c_ref.shape[0]
    _fill_shift_splats(shift_ref, eye_ref, shift_splat)
    crep[...] = jnp.dot(_split3(c_ref[...], 1), expand_ref[...], preferred_element_type=F32)
    dmask = dmask_ref[...]

    def toks(i, _):
        for j, g in enumerate(gs):
            _gather_tiles(base_ref, shift_splat, tab_ref, g, i * len(gs) + j)
        for j, g in enumerate(gs):
            t = i * len(gs) + j
            c8 = dmask * crep[pl.ds(t, 1), :]
            o = jnp.dot(_split3(c8, 0), g[...], preferred_element_type=F32)
            rs = _token_rows(t)
            o_ref[rs, :] = x_ref[rs, :] + (o[0:SUBLANES] + o[SUBLANES:2 * SUBLANES] + o[2 * SUBLANES:])
        return 0

    lax.fori_loop(0, tg // len(gs), toks, 0)


def _peer_v(base, shift, c, x2, tab):
    t = base.shape[0]
    tg = GATHER_BLOCK
    x8 = x2.reshape(t * SUBLANES, LANES)
    eye, dmask, gsum = _gather_consts()
    expand = jnp.concatenate([gsum.T] * 3, axis=0)
    smem = pl.BlockSpec((tg, PEER_SEL), lambda i: (i, 0), memory_space=pltpu.SMEM)
    tokrow = pl.BlockSpec((tg, PEER_SEL), lambda i: (i, 0))
    tile = pl.BlockSpec((tg * SUBLANES, LANES), lambda i: (i, 0))
    out = pl.pallas_call(
        _peer_v_kernel, grid=(t // tg,),
        in_specs=[smem, tokrow, tokrow, tile, _table_spec(tab.shape), _full(eye.shape), _full(dmask.shape),
                  _full(expand.shape)],
        out_specs=tile, out_shape=jax.ShapeDtypeStruct((t * SUBLANES, LANES), F32),
        scratch_shapes=_gather_scratch(tg),
        compiler_params=_params(("parallel",)),
    )(base, shift, c, x8, tab, eye, dmask, expand)
    return out.reshape(t, D_MODEL)


def _final_norm_kernel(x_ref, g_ref, o_ref):
    o_ref[...] = _rms(x_ref[...], g_ref[...])


def _final_norm(x2, g):
    t, d = x2.shape
    tm = min(PROJ_BLOCK, t)
    spec = pl.BlockSpec((tm, d), lambda i: (i, 0))
    return pl.pallas_call(
        _final_norm_kernel, grid=(t // tm,), in_specs=[spec, _full((1, d))], out_specs=spec,
        out_shape=jax.ShapeDtypeStruct((t, d), F32), compiler_params=_params(("parallel",)),
    )(x2, g.reshape(1, d).astype(F32))


def kernel(x, mem, norm_mix, w_in, rw_mu, rw_w0, rw_w2, rw_a0, rw_a2, rw_g2, rw_kk, rw_ka, rw_rk, rw_v0, rw_v1, rw_v2, rw_lnx_g, rw_lnx_b, s5_a_re, s5_a_im, s5_log_dt, s5_b_re, s5_b_im, s5_c_re, s5_c_im, s5_d, s5_glu_w, s5_glu_b, s5_out_g, w_out, norm_xa, norm_mem, xa_wq, xa_wk, xa_wv, xa_wo, norm_ffn, peer_wq, peer_keys, peer_u, peer_v, norm_final):
    bsz, seq, d = x.shape
    t = bsz * seq
    depth = w_in.shape[0]
    mem2 = mem.reshape(bsz * N_MEM, d)
    v_first = None
    for l in range(depth):
        x2 = x.reshape(t, d)
        w_rw = w_in[l][:, :RW_COLS].astype(BF16)
        w_s5 = w_in[l][:, RW_COLS:].astype(BF16)
        ws = [w_rw, w_s5]
        if l > 0:
            ws.append(_pad_rows(rw_v1[l - 1].T, LANES, 0).T.astype(BF16))
        outs = _norm_proj(x2, norm_mix[l], ws, [F32] * len(ws), PROJ_BLOCK)
        z_rw = outs[0].reshape(bsz, seq, RW_COLS)
        u_s5 = outs[1].reshape(bsz, seq, D_S5)
        hv = outs[2].reshape(bsz, seq, LANES) if l > 0 else None
        rw_prm = dict(mu=rw_mu[l], w0=rw_w0[l], w2=rw_w2[l], a0=rw_a0[l], a2=rw_a2[l], g2=rw_g2[l],
                      kk=rw_kk[l], ka=rw_ka[l], rk=rw_rk[l], lng=rw_lnx_g[l], lnb=rw_lnx_b[l])
        if l > 0:
            rw_prm.update(v0=rw_v0[l - 1], v2=rw_v2[l - 1])
        y_rw, v_first = _rwkv(z_rw, hv, v_first, rw_prm, bsz, seq)
        s5_prm = dict(a_re=s5_a_re[l], a_im=s5_a_im[l], log_dt=s5_log_dt[l], b_re=s5_b_re[l], b_im=s5_b_im[l],
                      c_re=s5_c_re[l], c_im=s5_c_im[l], d=s5_d[l], glu_w=s5_glu_w[l], glu_b=s5_glu_b[l],
                      out_g=s5_out_g[l])
        y_s5 = _s5(u_s5, s5_prm, bsz, seq)
        kv = _norm_proj(mem2, norm_mem[l], [xa_wk[l].astype(BF16), xa_wv[l].astype(BF16)], [BF16, BF16],
                        PROJ_BLOCK)
        kmem = kv[0].reshape(bsz, N_MEM, d)
        vmem = kv[1].reshape(bsz, N_MEM, d)
        x = _mix_xattn(x, y_rw, y_s5, w_out[l], norm_xa[l], xa_wq[l], kmem, vmem, xa_wo[l], bsz, seq)
        x2 = x.reshape(t, d)
        h3, base, shift, gate = _peer_select(x2, norm_ffn[l], peer_wq[l], peer_keys[l])
        c = _peer_u(base, shift, h3, gate, _pack_table(peer_u[l]))
        x = _peer_v(base, shift, c, x2, _pack_table(peer_v[l])).reshape(bsz, seq, d)
    return _final_norm(x.reshape(t, d), norm_final).reshape(bsz, seq, d)
```

```python
import functools
import math

import jax
import jax.numpy as jnp
from jax import lax
from jax.experimental import pallas as pl
from jax.experimental.pallas import tpu as pltpu

F32 = jnp.float32
BF16 = jnp.bfloat16
U32 = jnp.uint32

LANES = 128
SUBLANES = 8
VMEM_LIMIT = 56 * 1024 * 1024

D_MODEL = 1024
D_RWKV = 512
RW_HEAD = 64
RW_COLS = 1792
D_S5 = 512
S5_GROUPS = 32
S5_CH = 16
S5_STATE = 64
S5_MODES = S5_GROUPS * S5_STATE
N_MEM = 256
XA_HEADS = 4
XA_HEAD = 256
PEER_HEADS = 8
PEER_NKEYS = 128
PEER_TOPK = 16
PEER_SEL = PEER_HEADS * PEER_TOPK
RMS_EPS = 1e-6
GN_EPS = 64e-5

RW_CHUNK = 128
S5_BLOCK = 256
PROJ_BLOCK = 512
XA_BLOCK = 256
SEL_BLOCK = 128
GATHER_BLOCK = 64


def _params(sem):
    return pltpu.CompilerParams(dimension_semantics=sem, vmem_limit_bytes=VMEM_LIMIT)


def _rms(x, g):
    ms = jnp.mean(x * x, axis=-1, keepdims=True)
    return x * lax.rsqrt(ms + RMS_EPS) * g


def _bdot(a, b):
    return jnp.dot(a.astype(BF16), b.astype(BF16), preferred_element_type=F32)


def _bdot_nt(a, b):
    return lax.dot_general(a.astype(BF16), b.astype(BF16), (((1,), (1,)), ((), ())),
                           preferred_element_type=F32)


def _sigmoid(x):
    return 1.0 / (1.0 + jnp.exp(-x))


def _softplus(x):
    return jnp.maximum(x, 0.0) + jnp.log(1.0 + jnp.exp(-jnp.abs(x)))


def _gelu(x):
    return 0.5 * x * (1.0 + jnp.tanh(math.sqrt(2.0 / math.pi) * (x + 0.044715 * (x * x * x))))


def _full(shape):
    n = len(shape)
    return pl.BlockSpec(shape, lambda *_: (0,) * n)


def _norm_proj_kernel(*refs, n_out):
    x_ref, g_ref = refs[0], refs[1]
    w_refs = refs[2:2 + n_out]
    o_refs = refs[2 + n_out:]
    h = _rms(x_ref[...], g_ref[...]).astype(BF16)
    for w_ref, o_ref in zip(w_refs, o_refs):
        o_ref[...] = jnp.dot(h, w_ref[...], preferred_element_type=F32).astype(o_ref.dtype)


def _norm_proj(x2, g, ws, out_dtypes, block):
    t, d = x2.shape
    block = min(block, t)
    in_specs = [pl.BlockSpec((block, d), lambda i: (i, 0)), _full((1, d))]
    in_specs += [_full(w.shape) for w in ws]
    out_specs = [pl.BlockSpec((block, w.shape[1]), lambda i: (i, 0)) for w in ws]
    out_shape = [jax.ShapeDtypeStruct((t, w.shape[1]), dt) for w, dt in zip(ws, out_dtypes)]
    return pl.pallas_call(
        functools.partial(_norm_proj_kernel, n_out=len(ws)),
        grid=(t // block,), in_specs=in_specs, out_specs=out_specs, out_shape=out_shape,
        compiler_params=_params(("parallel",)),
    )(x2, g.reshape(1, d), *ws)


def _seg_sum(x, mseg):
    hi = x.astype(BF16)
    lo = (x - hi.astype(F32)).astype(BF16)
    return (jnp.dot(hi, mseg, preferred_element_type=F32)
            + jnp.dot(lo, mseg, preferred_element_type=F32))


def _col_bcast(row):
    return jnp.broadcast_to(row, (LANES, LANES)).T


def _rwkv_kernel(*refs, first_layer):
    if first_layer:
        (z_ref, mu_ref, w0_ref, w2_ref, a0_ref, a2_ref, g2_ref, kk_ref, ka_ref, rk_ref,
         lng_ref, lnb_ref, mseg_ref, y_ref, vf_out_ref, zprev, hst) = refs
    else:
        (z_ref, hv_ref, vf_ref, v0_ref, v2_ref, mu_ref, w0_ref, w2_ref, a0_ref, a2_ref, g2_ref,
         kk_ref, ka_ref, rk_ref, lng_ref, lnb_ref, mseg_ref, y_ref, zprev, hst) = refs
    L = RW_CHUNK

    @pl.when(pl.program_id(1) == 0)
    def _():
        zprev[...] = jnp.zeros_like(zprev)
        hst[...] = jnp.zeros_like(hst)

    z = z_ref[0]
    row = lax.broadcasted_iota(jnp.int32, (L, 1), 0)
    zs = jnp.where(row == 0, zprev[...], pltpu.roll(z, 1, 0))
    zprev[...] = z[L - 1:L, :]
    z = z + (zs - z) * mu_ref[...]
    r = z[:, 0:512]
    k = z[:, 512:1024]
    v = z[:, 1024:1536]
    wa = z[:, 1536:1664]
    gd = z[:, 1664:1792]
    mseg = mseg_ref[...]

    wlin = w0_ref[...] + _bdot(jnp.tanh(wa), w2_ref[...])
    lw = -jnp.exp(-_softplus(-wlin) - 0.5)
    a = _sigmoid(a0_ref[...] + _bdot(wa, a2_ref[...]))
    g = _bdot(_sigmoid(gd), g2_ref[...])
    if first_layer:
        vf_out_ref[0] = v
    else:
        v = v + (vf_ref[0] - v) * _sigmoid(v0_ref[...] + _bdot(hv_ref[0], v2_ref[...]))
    kk = k * kk_ref[...]
    kk = kk / jnp.maximum(jnp.sqrt(_seg_sum(kk * kk, mseg)), 1e-12)
    k2 = k * (1.0 + (a - 1.0) * ka_ref[...])
    av = -kk
    bv = kk * a

    ti = lax.broadcasted_iota(jnp.int32, (L, L), 0)
    si = lax.broadcasted_iota(jnp.int32, (L, L), 1)
    tril = (ti >= si).astype(F32)
    cum = jnp.dot(tril, lw, preferred_element_type=F32, precision=lax.Precision.HIGHEST)
    mid = cum[L // 2 - 1:L // 2, :]
    cm = cum - mid
    ecum = jnp.exp(cm)
    einv = jnp.exp(-cm)
    rt = r * ecum
    kt = k2 * einv
    bt = bv * einv
    at = av * jnp.exp(cm - lw)
    p_end = jnp.exp(cum[L - 1:L, :])
    e_end = ecum[L - 1:L, :]
    e_mid = jnp.exp(mid)

    lane = lax.broadcasted_iota(jnp.int32, (1, LANES), 1)
    m0 = (lane < RW_HEAD).astype(F32)
    m1 = 1.0 - m0
    strict = ti > si
    incl = ti >= si
    bi = lax.broadcasted_iota(jnp.int32, (LANES, LANES), 0) < RW_HEAD
    bj = lax.broadcasted_iota(jnp.int32, (LANES, LANES), 1) < RW_HEAD
    bdmask = (bi == bj).astype(F32)
    zeros_ll = jnp.zeros((L, L), F32)

    ys = []
    for p in range(D_RWKV // LANES):
        sl = slice(LANES * p, LANES * (p + 1))
        A, B, K, R, V = at[:, sl], bt[:, sl], kt[:, sl], rt[:, sl], v[:, sl]
        btkt = jnp.concatenate([B.T, K.T], axis=1)
        h0 = hst[p]
        h0m = h0 * _col_bcast(e_mid[:, sl])
        sc = _bdot(jnp.concatenate([A * m0, A * m1, R * m0, R * m1], axis=0), btkt)
        aab = [jnp.where(strict, sc[e * L:(e + 1) * L, 0:L], 0.0) for e in range(2)]
        aak = [jnp.where(strict, sc[e * L:(e + 1) * L, L:2 * L], 0.0) for e in range(2)]
        arb = [jnp.where(incl, sc[(2 + e) * L:(3 + e) * L, 0:L], 0.0) for e in range(2)]
        ark = [jnp.where(incl, sc[(2 + e) * L:(3 + e) * L, L:2 * L], 0.0) for e in range(2)]
        arh = _bdot(jnp.concatenate([A, R], axis=0), h0m)
        v01 = jnp.concatenate([V * m0, V * m1], axis=0)
        x = arh[0:L] + _bdot(jnp.concatenate(aak, axis=1), v01)
        pm = jnp.concatenate(aab, axis=1)
        n_fac = int(math.log2(L))
        for it in range(n_fac):
            x = x + _bdot(pm, jnp.concatenate([x * m0, x * m1], axis=0))
            if it + 1 < n_fac:
                pd = jnp.concatenate(
                    [jnp.concatenate([pm[:, 0:L], zeros_ll], axis=1),
                     jnp.concatenate([zeros_ll, pm[:, L:2 * L]], axis=1)], axis=0)
                pm = _bdot(pm, pd)
        u = x
        yp = arh[L:2 * L] + _bdot(jnp.concatenate(arb + ark, axis=1),
                                  jnp.concatenate([u * m0, u * m1, v01], axis=0))
        upd = _bdot(btkt, jnp.concatenate([u, V], axis=0))
        hst[p] = (h0 * _col_bcast(p_end[:, sl]) + upd * _col_bcast(e_end[:, sl])) * bdmask
        ys.append(yp)
    y = jnp.concatenate(ys, axis=1)

    mean = _seg_sum(y, mseg) * (1.0 / RW_HEAD)
    d = y - mean
    var = _seg_sum(d * d, mseg) * (1.0 / RW_HEAD)
    yn = d * lax.rsqrt(var + GN_EPS) * lng_ref[...] + lnb_ref[...]
    bonus = _seg_sum(r * k2 * rk_ref[...], mseg) * v
    y_ref[0] = (yn + bonus) * g


def _pad_rows(w, rows, offset):
    out = jnp.zeros((rows, w.shape[1]), w.dtype)
    return out.at[offset:offset + w.shape[0]].set(w)


def _rwkv(z_rw, hv, v_first, prm, bsz, seq):
    L = RW_CHUNK
    first = v_first is None
    row = lambda a: a.reshape(1, -1).astype(F32)
    hid = jnp.arange(D_RWKV) // RW_HEAD
    mseg = (hid[:, None] == hid[None, :]).astype(BF16)
    w2p = _pad_rows(prm['w2'], LANES, 0).astype(BF16)
    a2p = _pad_rows(prm['a2'], LANES, 64).astype(BF16)
    common = [row(prm['mu']), row(prm['w0']), w2p, row(prm['a0']), a2p, prm['g2'].astype(BF16),
              row(prm['kk']), row(prm['ka']), row(prm['rk']), row(prm['lng']), row(prm['lnb']), mseg]
    tok = lambda w: pl.BlockSpec((1, L, w), lambda b, t: (b, t, 0))
    common_specs = [_full(c.shape) for c in common]
    y_shape = jax.ShapeDtypeStruct((bsz, seq, D_RWKV), F32)
    scratch = [pltpu.VMEM((1, RW_COLS), F32), pltpu.VMEM((D_RWKV // LANES, LANES, LANES), F32)]
    if first:
        args = [z_rw] + common
        in_specs = [tok(RW_COLS)] + common_specs
        out_shape = [y_shape, y_shape]
        out_specs = [tok(D_RWKV), tok(D_RWKV)]
    else:
        v2p = _pad_rows(prm['v2'], LANES, 0).astype(BF16)
        extra = [row(prm['v0']), v2p]
        args = [z_rw, hv, v_first] + extra + common
        in_specs = [tok(RW_COLS), tok(LANES), tok(D_RWKV)] + [_full(c.shape) for c in extra] + common_specs
        out_shape = [y_shape]
        out_specs = [tok(D_RWKV)]
    outs = pl.pallas_call(
        functools.partial(_rwkv_kernel, first_layer=first),
        grid=(bsz, seq // L), in_specs=in_specs, out_specs=out_specs, out_shape=out_shape,
        scratch_shapes=scratch, compiler_params=_params(("parallel", "arbitrary")),
    )(*args)
    return (outs[0], outs[1]) if first else (outs[0], v_first)


def _s5_kernel(u_ref, wb_ref, wc_ref, lpr_ref, lpi_ref, d_ref, gw_ref, gb_ref, og_ref, o_ref,
               car_re, car_im, xre, xim):
    tb = u_ref.shape[1]

    @pl.when(pl.program_id(1) == 0)
    def _():
        car_re[...] = jnp.zeros_like(car_re)
        car_im[...] = jnp.zeros_like(car_im)

    u = u_ref[0]
    bu = _bdot(u, wb_ref[...])
    xre[...] = bu[:, 0:S5_MODES]
    xim[...] = bu[:, S5_MODES:2 * S5_MODES]
    row = lax.broadcasted_iota(jnp.int32, (SUBLANES, 1), 0)

    def tile(i, carry):
        cr, ci = carry
        rs = pl.ds(pl.multiple_of(i * SUBLANES, SUBLANES), SUBLANES)
        br, bi = xre[rs, :], xim[rs, :]
        for dist in (1, 2, 4):
            keep = row >= dist
            sr = jnp.where(keep, pltpu.roll(br, dist, 0), 0.0)
            si = jnp.where(keep, pltpu.roll(bi, dist, 0), 0.0)
            lr = lpr_ref[dist - 1:dist, :]
            li = lpi_ref[dist - 1:dist, :]
            br, bi = br + lr * sr - li * si, bi + lr * si + li * sr
        pr, pi = lpr_ref[...], lpi_ref[...]
        xr = br + pr * cr - pi * ci
        xi = bi + pr * ci + pi * cr
        xre[rs, :] = xr
        xim[rs, :] = xi
        return xr[SUBLANES - 1:SUBLANES, :], xi[SUBLANES - 1:SUBLANES, :]

    cr, ci = lax.fori_loop(0, tb // SUBLANES, tile, (car_re[...], car_im[...]))
    car_re[...] = cr
    car_im[...] = ci
    wc = wc_ref[...]
    y = _bdot(xre[...], wc[0:S5_MODES]) + _bdot(xim[...], wc[S5_MODES:2 * S5_MODES])
    y = _gelu(y + d_ref[...] * u)
    y = y * _sigmoid(_bdot(y, gw_ref[...]) + gb_ref[...])
    o_ref[0] = _rms(y, og_ref[...])


def _s5_weights(a_re, a_im, log_dt, b_re, b_im, c_re, c_im):
    lam_re = jnp.minimum(a_re.astype(F32), -1e-4)
    lam_im = a_im.astype(F32)
    dt = jnp.exp(log_dt.astype(F32))[:, None]
    mag = jnp.exp(lam_re * dt)
    lb_re = mag * jnp.cos(lam_im * dt)
    lb_im = mag * jnp.sin(lam_im * dt)
    den = lam_re * lam_re + lam_im * lam_im
    c1_re = ((lb_re - 1.0) * lam_re + lb_im * lam_im) / den
    c1_im = (lb_im * lam_re - (lb_re - 1.0) * lam_im) / den
    br, bi = b_re.astype(F32), b_im.astype(F32)
    bb_re = c1_re[..., None] * br - c1_im[..., None] * bi
    bb_im = c1_re[..., None] * bi + c1_im[..., None] * br
    eye = jnp.eye(S5_GROUPS, dtype=F32)
    wb_re = jnp.einsum('gpc,gh->gchp', bb_re, eye).reshape(D_S5, S5_MODES)
    wb_im = jnp.einsum('gpc,gh->gchp', bb_im, eye).reshape(D_S5, S5_MODES)
    wb = jnp.concatenate([wb_re, wb_im], axis=1).astype(BF16)
    wc_re = jnp.einsum('gcp,gh->gphc', c_re.astype(F32), eye).reshape(S5_MODES, D_S5)
    wc_im = jnp.einsum('gcp,gh->gphc', c_im.astype(F32), eye).reshape(S5_MODES, D_S5)
    wc = jnp.concatenate([wc_re, -wc_im], axis=0).astype(BF16)
    pr, pi = [lb_re], [lb_im]
    for _ in range(SUBLANES - 1):
        pr, pi = pr + [pr[-1] * lb_re - pi[-1] * lb_im], pi + [pr[-1] * lb_im + pi[-1] * lb_re]
    lp_re = jnp.stack(pr).reshape(SUBLANES, S5_MODES)
    lp_im = jnp.stack(pi).reshape(SUBLANES, S5_MODES)
    return wb, wc, lp_re, lp_im


def _s5(u, prm, bsz, seq):
    tb = min(S5_BLOCK, seq)
    wb, wc, lp_re, lp_im = _s5_weights(prm['a_re'], prm['a_im'], prm['log_dt'], prm['b_re'], prm['b_im'],
                                       prm['c_re'], prm['c_im'])
    row = lambda a: a.reshape(1, -1).astype(F32)
    consts = [wb, wc, lp_re, lp_im, row(prm['d']), prm['glu_w'].astype(BF16), row(prm['glu_b']),
              row(prm['out_g'])]
    tok = pl.BlockSpec((1, tb, D_S5), lambda b, t: (b, t, 0))
    return pl.pallas_call(
        _s5_kernel, grid=(bsz, seq // tb),
        in_specs=[tok] + [_full(c.shape) for c in consts], out_specs=tok,
        out_shape=jax.ShapeDtypeStruct((bsz, seq, D_S5), F32),
        scratch_shapes=[pltpu.VMEM((1, S5_MODES), F32), pltpu.VMEM((1, S5_MODES), F32),
                        pltpu.VMEM((tb, S5_MODES), F32), pltpu.VMEM((tb, S5_MODES), F32)],
        compiler_params=_params(("parallel", "arbitrary")),
    )(u, *consts)


def _mix_xattn_kernel(x_ref, yr_ref, ys_ref, wo1_ref, wo2_ref, g_ref, wq_ref, k_ref, v_ref, wo_ref, o_ref):
    x1 = x_ref[0] + _bdot(yr_ref[0], wo1_ref[...]) + _bdot(ys_ref[0], wo2_ref[...])
    h = _rms(x1, g_ref[...])
    q = _bdot(h, wq_ref[...])
    km, vm = k_ref[0], v_ref[0]
    outs = []
    for hd in range(XA_HEADS):
        sl = slice(XA_HEAD * hd, XA_HEAD * (hd + 1))
        s = _bdot_nt(q[:, sl], km[:, sl]) * (XA_HEAD ** -0.5)
        s = s - jnp.max(s, axis=-1, keepdims=True)
        e = jnp.exp(s)
        p = e / jnp.sum(e, axis=-1, keepdims=True)
        outs.append(_bdot(p, vm[:, sl]))
    o = jnp.concatenate(outs, axis=1)
    o_ref[0] = x1 + _bdot(o, wo_ref[...])


def _mix_xattn(x, y_rw, y_s5, w_out, g, wq, kmem, vmem, wo, bsz, seq):
    tm = min(XA_BLOCK, seq)
    consts_a = [w_out[:D_RWKV].astype(BF16), w_out[D_RWKV:].astype(BF16), g.reshape(1, -1).astype(F32),
                wq.astype(BF16)]
    tok = lambda w: pl.BlockSpec((1, tm, w), lambda b, t: (b, t, 0))
    mem = pl.BlockSpec((1, N_MEM, D_MODEL), lambda b, t: (b, 0, 0))
    wo_b = wo.astype(BF16)
    return pl.pallas_call(
        _mix_xattn_kernel, grid=(bsz, seq // tm),
        in_specs=[tok(D_MODEL), tok(D_RWKV), tok(D_S5)] + [_full(c.shape) for c in consts_a]
                 + [mem, mem, _full(wo_b.shape)],
        out_specs=tok(D_MODEL), out_shape=jax.ShapeDtypeStruct((bsz, seq, D_MODEL), F32),
        compiler_params=_params(("parallel", "parallel")),
    )(x, y_rw, y_s5, *consts_a, kmem, vmem, wo_b)


def _top_rows(work, iota, n, aux, val_ref, idx_ref):
    for it in range(PEER_TOPK):
        m = jnp.max(work, axis=0, keepdims=True)
        pos = jnp.min(jnp.where(work == m, iota, float(n)), axis=0, keepdims=True)
        hit = iota == pos
        val_ref[it:it + 1, :] = m
        if aux is None:
            idx_ref[it:it + 1, :] = pos
        else:
            idx_ref[it:it + 1, :] = jnp.sum(jnp.where(hit, aux, 0.0), axis=0, keepdims=True)
        work = jnp.where(hit, -jnp.inf, work)


def _peer_select_kernel(x_ref, g_ref, wq_ref, keys_ref, h_ref, base_ref, shift_ref, gate_ref,
                        q3, s1, i1, s2, i2, cand, cidx, top, eid, idx_t, gate_t):
    tm = x_ref.shape[0]
    h = _rms(x_ref[...], g_ref[...])
    h_ref[...] = h
    q = _bdot(h, wq_ref[...])
    for j in range(2 * PEER_HEADS):
        q3[j] = q[:, LANES * j:LANES * (j + 1)]
    iota_k = lax.broadcasted_iota(jnp.int32, (PEER_NKEYS, tm), 0).astype(F32)
    n_cand = PEER_TOPK * PEER_TOPK
    iota_c = lax.broadcasted_iota(jnp.int32, (n_cand, tm), 0).astype(F32)

    def head(hd, _):
        sc1 = _bdot_nt(keys_ref[2 * hd], q3[2 * hd])
        sc2 = _bdot_nt(keys_ref[2 * hd + 1], q3[2 * hd + 1])
        _top_rows(sc1, iota_k, PEER_NKEYS, None, s1, i1)
        _top_rows(sc2, iota_k, PEER_NKEYS, None, s2, i2)
        s2v, i2v = s2[...], i2[...]
        for a in range(PEER_TOPK):
            rs = slice(PEER_TOPK * a, PEER_TOPK * (a + 1))
            cand[rs, :] = s1[a:a + 1, :] + s2v
            cidx[rs, :] = i1[a:a + 1, :] * float(PEER_NKEYS) + i2v
        _top_rows(cand[...], iota_c, n_cand, cidx[...], top, eid)
        tv = top[...]
        e = jnp.exp(tv - jnp.max(tv, axis=0, keepdims=True))
        gate = e / jnp.sum(e, axis=0, keepdims=True)
        rs = pl.ds(pl.multiple_of(hd * PEER_TOPK, PEER_TOPK), PEER_TOPK)
        idx_t[rs, :] = eid[...]
        gate_t[rs, :] = gate
        return 0

    lax.fori_loop(0, PEER_HEADS, head, 0)
    e_t = idx_t[...].T
    pair = jnp.floor(e_t * 0.5)
    base_ref[...] = (pair * float(SUBLANES)).astype(jnp.int32)
    shift_ref[...] = ((e_t - 2.0 * pair) * 16.0).astype(jnp.int32)
    gate_ref[...] = gate_t[...].T


def _peer_select(x2, g, wq, keys):
    t = x2.shape[0]
    tm = SEL_BLOCK
    keys_b = keys.reshape(2 * PEER_HEADS, PEER_NKEYS, LANES).astype(BF16)
    wq_b = wq.astype(BF16)
    tokspec = lambda w: pl.BlockSpec((tm, w), lambda i: (i, 0))
    vm = lambda r: pltpu.VMEM((r, tm), F32)
    return pl.pallas_call(
        _peer_select_kernel, grid=(t // tm,),
        in_specs=[tokspec(D_MODEL), _full((1, D_MODEL)), _full(wq_b.shape), _full(keys_b.shape)],
        out_specs=[tokspec(D_MODEL), tokspec(PEER_SEL), tokspec(PEER_SEL), tokspec(PEER_SEL)],
        out_shape=[jax.ShapeDtypeStruct((t, D_MODEL), F32), jax.ShapeDtypeStruct((t, PEER_SEL), jnp.int32),
                   jax.ShapeDtypeStruct((t, PEER_SEL), jnp.int32), jax.ShapeDtypeStruct((t, PEER_SEL), F32)],
        scratch_shapes=[pltpu.VMEM((2 * PEER_HEADS, tm, LANES), F32),
                        vm(PEER_TOPK), vm(PEER_TOPK), vm(PEER_TOPK), vm(PEER_TOPK),
                        vm(PEER_TOPK * PEER_TOPK), vm(PEER_TOPK * PEER_TOPK),
                        vm(PEER_TOPK), vm(PEER_TOPK), vm(PEER_SEL), vm(PEER_SEL)],
        compiler_params=_params(("parallel",)),
    )(x2, g.reshape(1, -1).astype(F32), wq_b, keys_b)


def _pack_table(tab):
    n, d = tab.shape
    bits = lax.bitcast_convert_type(tab.astype(BF16), jnp.uint16).astype(U32)
    bits = bits.reshape(n // 2, 2, d)
    packed = (bits[:, 0, :] << 16) | bits[:, 1, :]
    return lax.bitcast_convert_type(packed, jnp.int32).reshape(n // 2 * SUBLANES, LANES)


def _splat_into(src_ref, t, dst_ref, slot):
    tile = jnp.broadcast_to(src_ref[pl.ds(t, 1), :], (LANES, LANES)).T
    dst_ref[LANES * slot:LANES * (slot + 1), :] = tile


def _bcast_row(ref, row):
    return jnp.broadcast_to(ref[row:row + 1, :], (SUBLANES, LANES))


def _expert_tile(tab_ref, base, shift_splat, row):
    w = tab_ref[pl.ds(pl.multiple_of(base, SUBLANES), SUBLANES), :]
    return lax.bitcast_convert_type((w << _bcast_row(shift_splat, row)) & jnp.int32(-65536), F32)


def _token_rows(t):
    return pl.ds(pl.multiple_of(t * SUBLANES, SUBLANES), SUBLANES)


def _pipelined_tokens(tg, prep, work):
    prep(0, 0)

    def body(i, _):
        t0 = 2 * i
        prep(t0 + 1, 1)
        work(t0, 0)
        prep(jnp.minimum(t0 + 2, tg - 1), 0)
        work(t0 + 1, 1)
        return 0

    lax.fori_loop(0, tg // 2, body, 0)


def _table_spec(shape):
    return pl.BlockSpec(shape, lambda i: (0, 0), pipeline_mode=pl.Buffered(1))


N_CHUNK = D_MODEL // LANES
CHUNK_STRIDE = PEER_SEL + SUBLANES


def _peer_u_kernel(base_ref, shift_ref, h_ref, gate_ref, tab_ref, eye_ref, c_ref, shift_splat, *planes):
    tg = gate_ref.shape[0]
    ones = jnp.ones((LANES, LANES), BF16)

    half = N_CHUNK // 2

    def prep(t, slot):
        _splat_into(shift_ref, t, shift_splat, slot)

    def gather(t, slot):
        ht = h_ref[_token_rows(t), :]
        for kx in range(PEER_SEL):
            prod = _expert_tile(tab_ref, base_ref[t, kx], shift_splat, LANES * slot + kx) * ht
            fold = prod + pltpu.roll(prod, half, 0)
            planes[slot][pl.ds(kx, half, stride=CHUNK_STRIDE), :] = fold[0:half, :]

    def finish(t, slot):
        plane = planes[slot]
        acc = plane[0:PEER_SEL, :]
        for r in range(1, half):
            acc = acc + plane[CHUNK_STRIDE * r:CHUNK_STRIDE * r + PEER_SEL, :]
        hi = acc.astype(BF16)
        lo = (acc - hi.astype(F32)).astype(BF16)
        tot = jnp.dot(hi, ones, preferred_element_type=F32) + jnp.dot(lo, ones, preferred_element_type=F32)
        score = jnp.sum(tot * eye_ref[...], axis=0, keepdims=True)
        c_ref[pl.ds(t, 1), :] = gate_ref[pl.ds(t, 1), :] * _gelu(score)

    planes[1][...] = jnp.zeros_like(planes[1])
    prep(0, 0)

    def body(i, _):
        t0 = 2 * i
        prep(t0 + 1, 1)
        gather(t0, 0)
        finish(jnp.maximum(t0 - 1, 0), 1)
        prep(jnp.minimum(t0 + 2, tg - 1), 0)
        gather(t0 + 1, 1)
        finish(t0, 0)
        return 0

    lax.fori_loop(0, tg // 2, body, 0)
    finish(tg - 1, 1)


def _peer_u(base, shift, h2, gate, tab):
    t = base.shape[0]
    tg = GATHER_BLOCK
    h8 = h2.reshape(t * SUBLANES, LANES)
    eye = jnp.eye(LANES, dtype=F32)
    tokrow = pl.BlockSpec((tg, PEER_SEL), lambda i: (i, 0))
    return pl.pallas_call(
        _peer_u_kernel, grid=(t // tg,),
        in_specs=[pl.BlockSpec((tg, PEER_SEL), lambda i: (i, 0), memory_space=pltpu.SMEM), tokrow,
                  pl.BlockSpec((tg * SUBLANES, LANES), lambda i: (i, 0)), tokrow,
                  _table_spec(tab.shape), _full(eye.shape)],
        out_specs=tokrow, out_shape=jax.ShapeDtypeStruct((t, PEER_SEL), F32),
        scratch_shapes=[pltpu.VMEM((2 * LANES, LANES), jnp.int32)]
                       + [pltpu.VMEM((N_CHUNK // 2 * CHUNK_STRIDE, LANES), F32)] * 2,
        compiler_params=_params(("parallel",)),
    )(base, shift, h8, gate, tab, eye)


def _peer_v_kernel(base_ref, shift_ref, c_ref, x_ref, tab_ref, o_ref, shift_splat, c_splat):
    tg = c_ref.shape[0]
    n_acc = 4

    def prep(t, slot):
        _splat_into(shift_ref, t, shift_splat, slot)
        _splat_into(c_ref, t, c_splat, slot)

    def work(t, slot):
        accs = [jnp.zeros((SUBLANES, LANES), F32) for _ in range(n_acc)]
        for kx in range(PEER_SEL):
            row = LANES * slot + kx
            accs[kx % n_acc] = accs[kx % n_acc] + (_bcast_row(c_splat, row)
                                                   * _expert_tile(tab_ref, base_ref[t, kx], shift_splat, row))
        rs = _token_rows(t)
        o_ref[rs, :] = x_ref[rs, :] + ((accs[0] + accs[1]) + (accs[2] + accs[3]))

    _pipelined_tokens(tg, prep, work)


def _peer_v(base, shift, c, x2, tab):
    t = base.shape[0]
    tg = GATHER_BLOCK
    x8 = x2.reshape(t * SUBLANES, LANES)
    smem = pl.BlockSpec((tg, PEER_SEL), lambda i: (i, 0), memory_space=pltpu.SMEM)
    tokrow = pl.BlockSpec((tg, PEER_SEL), lambda i: (i, 0))
    tile = pl.BlockSpec((tg * SUBLANES, LANES), lambda i: (i, 0))
    out = pl.pallas_call(
        _peer_v_kernel, grid=(t // tg,),
        in_specs=[smem, tokrow, tokrow, tile, _table_spec(tab.shape)],
        out_specs=tile, out_shape=jax.ShapeDtypeStruct((t * SUBLANES, LANES), F32),
        scratch_shapes=[pltpu.VMEM((2 * LANES, LANES), jnp.int32), pltpu.VMEM((2 * LANES, LANES), F32)],
        compiler_params=_params(("parallel",)),
    )(base, shift, c, x8, tab)
    return out.reshape(t, D_MODEL)


def _final_norm_kernel(x_ref, g_ref, o_ref):
    o_ref[...] = _rms(x_ref[...], g_ref[...])


def _final_norm(x2, g):
    t, d = x2.shape
    tm = min(PROJ_BLOCK, t)
    spec = pl.BlockSpec((tm, d), lambda i: (i, 0))
    return pl.pallas_call(
        _final_norm_kernel, grid=(t // tm,), in_specs=[spec, _full((1, d))], out_specs=spec,
        out_shape=jax.ShapeDtypeStruct((t, d), F32), compiler_params=_params(("parallel",)),
    )(x2, g.reshape(1, d).astype(F32))


def kernel(x, mem, norm_mix, w_in, rw_mu, rw_w0, rw_w2, rw_a0, rw_a2, rw_g2, rw_kk, rw_ka, rw_rk, rw_v0, rw_v1, rw_v2, rw_lnx_g, rw_lnx_b, s5_a_re, s5_a_im, s5_log_dt, s5_b_re, s5_b_im, s5_c_re, s5_c_im, s5_d, s5_glu_w, s5_glu_b, s5_out_g, w_out, norm_xa, norm_mem, xa_wq, xa_wk, xa_wv, xa_wo, norm_ffn, peer_wq, peer_keys, peer_u, peer_v, norm_final):
    bsz, seq, d = x.shape
    t = bsz * seq
    depth = w_in.shape[0]
    mem2 = mem.reshape(bsz * N_MEM, d)
    v_first = None
    for l in range(depth):
        x2 = x.reshape(t, d)
        w_rw = w_in[l][:, :RW_COLS].astype(BF16)
        w_s5 = w_in[l][:, RW_COLS:].astype(BF16)
        ws = [w_rw, w_s5]
        if l > 0:
            ws.append(_pad_rows(rw_v1[l - 1].T, LANES, 0).T.astype(BF16))
        outs = _norm_proj(x2, norm_mix[l], ws, [F32] * len(ws), PROJ_BLOCK)
        z_rw = outs[0].reshape(bsz, seq, RW_COLS)
        u_s5 = outs[1].reshape(bsz, seq, D_S5)
        hv = outs[2].reshape(bsz, seq, LANES) if l > 0 else None
        rw_prm = dict(mu=rw_mu[l], w0=rw_w0[l], w2=rw_w2[l], a0=rw_a0[l], a2=rw_a2[l], g2=rw_g2[l],
                      kk=rw_kk[l], ka=rw_ka[l], rk=rw_rk[l], lng=rw_lnx_g[l], lnb=rw_lnx_b[l])
        if l > 0:
            rw_prm.update(v0=rw_v0[l - 1], v2=rw_v2[l - 1])
        y_rw, v_first = _rwkv(z_rw, hv, v_first, rw_prm, bsz, seq)
        s5_prm = dict(a_re=s5_a_re[l], a_im=s5_a_im[l], log_dt=s5_log_dt[l], b_re=s5_b_re[l], b_im=s5_b_im[l],
                      c_re=s5_c_re[l], c_im=s5_c_im[l], d=s5_d[l], glu_w=s5_glu_w[l], glu_b=s5_glu_b[l],
                      out_g=s5_out_g[l])
        y_s5 = _s5(u_s5, s5_prm, bsz, seq)
        kv = _norm_proj(mem2, norm_mem[l], [xa_wk[l].astype(BF16), xa_wv[l].astype(BF16)], [BF16, BF16],
                        PROJ_BLOCK)
        kmem = kv[0].reshape(bsz, N_MEM, d)
        vmem = kv[1].reshape(bsz, N_MEM, d)
        x = _mix_xattn(x, y_rw, y_s5, w_out[l], norm_xa[l], xa_wq[l], kmem, vmem, xa_wo[l], bsz, seq)
        x2 = x.reshape(t, d)
        h3, base, shift, gate = _peer_select(x2, norm_ffn[l], peer_wq[l], peer_keys[l])
        c = _peer_u(base, shift, h3, gate, _pack_table(peer_u[l]))
        x = _peer_v(base, shift, c, x2, _pack_table(peer_v[l])).reshape(bsz, seq, d)
    return _final_norm(x.reshape(t, d), norm_final).reshape(bsz, seq, d)
```

```python
import functools
import math

import jax
import jax.numpy as jnp
from jax import lax
from jax.experimental import pallas as pl
from jax.experimental.pallas import tpu as pltpu

F32 = jnp.float32
BF16 = jnp.bfloat16
U32 = jnp.uint32

LANES = 128
SUBLANES = 8
VMEM_LIMIT = 56 * 1024 * 1024

D_MODEL = 1024
D_RWKV = 512
RW_HEAD = 64
RW_COLS = 1792
D_S5 = 512
S5_GROUPS = 32
S5_CH = 16
S5_STATE = 64
S5_MODES = S5_GROUPS * S5_STATE
N_MEM = 256
XA_HEADS = 4
XA_HEAD = 256
PEER_HEADS = 8
PEER_NKEYS = 128
PEER_TOPK = 16
PEER_SEL = PEER_HEADS * PEER_TOPK
RMS_EPS = 1e-6
GN_EPS = 64e-5

RW_CHUNK = 128
S5_BLOCK = 256
PROJ_BLOCK = 512
XA_BLOCK = 256
SEL_BLOCK = 128
GATHER_BLOCK = 64


def _params(sem):
    return pltpu.CompilerParams(dimension_semantics=sem, vmem_limit_bytes=VMEM_LIMIT)


def _rms(x, g):
    ms = jnp.mean(x * x, axis=-1, keepdims=True)
    return x * lax.rsqrt(ms + RMS_EPS) * g


def _bdot(a, b):
    return jnp.dot(a.astype(BF16), b.astype(BF16), preferred_element_type=F32)


def _bdot_nt(a, b):
    return lax.dot_general(a.astype(BF16), b.astype(BF16), (((1,), (1,)), ((), ())),
                           preferred_element_type=F32)


def _sigmoid(x):
    return 1.0 / (1.0 + jnp.exp(-x))


def _softplus(x):
    return jnp.maximum(x, 0.0) + jnp.log(1.0 + jnp.exp(-jnp.abs(x)))


def _gelu(x):
    return 0.5 * x * (1.0 + jnp.tanh(math.sqrt(2.0 / math.pi) * (x + 0.044715 * (x * x * x))))


def _full(shape):
    n = len(shape)
    return pl.BlockSpec(shape, lambda *_: (0,) * n)


def _norm_proj_kernel(*refs, n_out):
    x_ref, g_ref = refs[0], refs[1]
    w_refs = refs[2:2 + n_out]
    o_refs = refs[2 + n_out:]
    h = _rms(x_ref[...], g_ref[...]).astype(BF16)
    for w_ref, o_ref in zip(w_refs, o_refs):
        o_ref[...] = jnp.dot(h, w_ref[...], preferred_element_type=F32).astype(o_ref.dtype)


def _norm_proj(x2, g, ws, out_dtypes, block):
    t, d = x2.shape
    block = min(block, t)
    in_specs = [pl.BlockSpec((block, d), lambda i: (i, 0)), _full((1, d))]
    in_specs += [_full(w.shape) for w in ws]
    out_specs = [pl.BlockSpec((block, w.shape[1]), lambda i: (i, 0)) for w in ws]
    out_shape = [jax.ShapeDtypeStruct((t, w.shape[1]), dt) for w, dt in zip(ws, out_dtypes)]
    return pl.pallas_call(
        functools.partial(_norm_proj_kernel, n_out=len(ws)),
        grid=(t // block,), in_specs=in_specs, out_specs=out_specs, out_shape=out_shape,
        compiler_params=_params(("parallel",)),
    )(x2, g.reshape(1, d), *ws)


def _seg_sum(x, mseg):
    hi = x.astype(BF16)
    lo = (x - hi.astype(F32)).astype(BF16)
    return (jnp.dot(hi, mseg, preferred_element_type=F32)
            + jnp.dot(lo, mseg, preferred_element_type=F32))


def _col_bcast(row):
    return jnp.broadcast_to(row, (LANES, LANES)).T


def _rwkv_kernel(*refs, first_layer):
    if first_layer:
        (z_ref, mu_ref, w0_ref, w2_ref, a0_ref, a2_ref, g2_ref, kk_ref, ka_ref, rk_ref,
         lng_ref, lnb_ref, mseg_ref, y_ref, vf_out_ref, zprev, hst) = refs
    else:
        (z_ref, hv_ref, vf_ref, v0_ref, v2_ref, mu_ref, w0_ref, w2_ref, a0_ref, a2_ref, g2_ref,
         kk_ref, ka_ref, rk_ref, lng_ref, lnb_ref, mseg_ref, y_ref, zprev, hst) = refs
    L = RW_CHUNK

    @pl.when(pl.program_id(1) == 0)
    def _():
        zprev[...] = jnp.zeros_like(zprev)
        hst[...] = jnp.zeros_like(hst)

    z = z_ref[0]
    row = lax.broadcasted_iota(jnp.int32, (L, 1), 0)
    zs = jnp.where(row == 0, zprev[...], pltpu.roll(z, 1, 0))
    zprev[...] = z[L - 1:L, :]
    z = z + (zs - z) * mu_ref[...]
    r = z[:, 0:512]
    k = z[:, 512:1024]
    v = z[:, 1024:1536]
    wa = z[:, 1536:1664]
    gd = z[:, 1664:1792]
    mseg = mseg_ref[...]

    wlin = w0_ref[...] + _bdot(jnp.tanh(wa), w2_ref[...])
    lw = -jnp.exp(-_softplus(-wlin) - 0.5)
    a = _sigmoid(a0_ref[...] + _bdot(wa, a2_ref[...]))
    g = _bdot(_sigmoid(gd), g2_ref[...])
    if first_layer:
        vf_out_ref[0] = v
    else:
        v = v + (vf_ref[0] - v) * _sigmoid(v0_ref[...] + _bdot(hv_ref[0], v2_ref[...]))
    kk = k * kk_ref[...]
    kk = kk / jnp.maximum(jnp.sqrt(_seg_sum(kk * kk, mseg)), 1e-12)
    k2 = k * (1.0 + (a - 1.0) * ka_ref[...])
    av = -kk
    bv = kk * a

    ti = lax.broadcasted_iota(jnp.int32, (L, L), 0)
    si = lax.broadcasted_iota(jnp.int32, (L, L), 1)
    tril = (ti >= si).astype(F32)
    cum = jnp.dot(tril, lw, preferred_element_type=F32, precision=lax.Precision.HIGHEST)
    mid = cum[L // 2 - 1:L // 2, :]
    cm = cum - mid
    ecum = jnp.exp(cm)
    einv = jnp.exp(-cm)
    rt = r * ecum
    kt = k2 * einv
    bt = bv * einv
    at = av * jnp.exp(cm - lw)
    p_end = jnp.exp(cum[L - 1:L, :])
    e_end = ecum[L - 1:L, :]
    e_mid = jnp.exp(mid)

    lane = lax.broadcasted_iota(jnp.int32, (1, LANES), 1)
    m0 = (lane < RW_HEAD).astype(F32)
    m1 = 1.0 - m0
    strict = ti > si
    incl = ti >= si
    bi = lax.broadcasted_iota(jnp.int32, (LANES, LANES), 0) < RW_HEAD
    bj = lax.broadcasted_iota(jnp.int32, (LANES, LANES), 1) < RW_HEAD
    bdmask = (bi == bj).astype(F32)
    zeros_ll = jnp.zeros((L, L), F32)

    ys = []
    for p in range(D_RWKV // LANES):
        sl = slice(LANES * p, LANES * (p + 1))
        A, B, K, R, V = at[:, sl], bt[:, sl], kt[:, sl], rt[:, sl], v[:, sl]
        btkt = jnp.concatenate([B.T, K.T], axis=1)
        h0 = hst[p]
        h0m = h0 * _col_bcast(e_mid[:, sl])
        sc = _bdot(jnp.concatenate([A * m0, A * m1, R * m0, R * m1], axis=0), btkt)
        aab = [jnp.where(strict, sc[e * L:(e + 1) * L, 0:L], 0.0) for e in range(2)]
        aak = [jnp.where(strict, sc[e * L:(e + 1) * L, L:2 * L], 0.0) for e in range(2)]
        arb = [jnp.where(incl, sc[(2 + e) * L:(3 + e) * L, 0:L], 0.0) for e in range(2)]
        ark = [jnp.where(incl, sc[(2 + e) * L:(3 + e) * L, L:2 * L], 0.0) for e in range(2)]
        arh = _bdot(jnp.concatenate([A, R], axis=0), h0m)
        v01 = jnp.concatenate([V * m0, V * m1], axis=0)
        x = arh[0:L] + _bdot(jnp.concatenate(aak, axis=1), v01)
        pm = jnp.concatenate(aab, axis=1)
        n_fac = int(math.log2(L))
        for it in range(n_fac):
            x = x + _bdot(pm, jnp.concatenate([x * m0, x * m1], axis=0))
            if it + 1 < n_fac:
                pd = jnp.concatenate(
                    [jnp.concatenate([pm[:, 0:L], zeros_ll], axis=1),
                     jnp.concatenate([zeros_ll, pm[:, L:2 * L]], axis=1)], axis=0)
                pm = _bdot(pm, pd)
        u = x
        yp = arh[L:2 * L] + _bdot(jnp.concatenate(arb + ark, axis=1),
                                  jnp.concatenate([u * m0, u * m1, v01], axis=0))
        upd = _bdot(btkt, jnp.concatenate([u, V], axis=0))
        hst[p] = (h0 * _col_bcast(p_end[:, sl]) + upd * _col_bcast(e_end[:, sl])) * bdmask
        ys.append(yp)
    y = jnp.concatenate(ys, axis=1)

    mean = _seg_sum(y, mseg) * (1.0 / RW_HEAD)
    d = y - mean
    var = _seg_sum(d * d, mseg) * (1.0 / RW_HEAD)
    yn = d * lax.rsqrt(var + GN_EPS) * lng_ref[...] + lnb_ref[...]
    bonus = _seg_sum(r * k2 * rk_ref[...], mseg) * v
    y_ref[0] = (yn + bonus) * g


def _pad_rows(w, rows, offset):
    out = jnp.zeros((rows, w.shape[1]), w.dtype)
    return out.at[offset:offset + w.shape[0]].set(w)


def _rwkv(z_rw, hv, v_first, prm, bsz, seq):
    L = RW_CHUNK
    first = v_first is None
    row = lambda a: a.reshape(1, -1).astype(F32)
    hid = jnp.arange(D_RWKV) // RW_HEAD
    mseg = (hid[:, None] == hid[None, :]).astype(BF16)
    w2p = _pad_rows(prm['w2'], LANES, 0).astype(BF16)
    a2p = _pad_rows(prm['a2'], LANES, 64).astype(BF16)
    common = [row(prm['mu']), row(prm['w0']), w2p, row(prm['a0']), a2p, prm['g2'].astype(BF16),
              row(prm['kk']), row(prm['ka']), row(prm['rk']), row(prm['lng']), row(prm['lnb']), mseg]
    tok = lambda w: pl.BlockSpec((1, L, w), lambda b, t: (b, t, 0))
    common_specs = [_full(c.shape) for c in common]
    y_shape = jax.ShapeDtypeStruct((bsz, seq, D_RWKV), F32)
    scratch = [pltpu.VMEM((1, RW_COLS), F32), pltpu.VMEM((D_RWKV // LANES, LANES, LANES), F32)]
    if first:
        args = [z_rw] + common
        in_specs = [tok(RW_COLS)] + common_specs
        out_shape = [y_shape, y_shape]
        out_specs = [tok(D_RWKV), tok(D_RWKV)]
    else:
        v2p = _pad_rows(prm['v2'], LANES, 0).astype(BF16)
        extra = [row(prm['v0']), v2p]
        args = [z_rw, hv, v_first] + extra + common
        in_specs = [tok(RW_COLS), tok(LANES), tok(D_RWKV)] + [_full(c.shape) for c in extra] + common_specs
        out_shape = [y_shape]
        out_specs = [tok(D_RWKV)]
    outs = pl.pallas_call(
        functools.partial(_rwkv_kernel, first_layer=first),
        grid=(bsz, seq // L), in_specs=in_specs, out_specs=out_specs, out_shape=out_shape,
        scratch_shapes=scratch, compiler_params=_params(("parallel", "arbitrary")),
    )(*args)
    return (outs[0], outs[1]) if first else (outs[0], v_first)


def _s5_kernel(u_ref, wb_ref, wc_ref, lpr_ref, lpi_ref, d_ref, gw_ref, gb_ref, og_ref, o_ref,
               car_re, car_im, xre, xim):
    tb = u_ref.shape[1]

    @pl.when(pl.program_id(1) == 0)
    def _():
        car_re[...] = jnp.zeros_like(car_re)
        car_im[...] = jnp.zeros_like(car_im)

    u = u_ref[0]
    bu = _bdot(u, wb_ref[...])
    xre[...] = bu[:, 0:S5_MODES]
    xim[...] = bu[:, S5_MODES:2 * S5_MODES]
    row = lax.broadcasted_iota(jnp.int32, (SUBLANES, 1), 0)

    def tile(i, carry):
        cr, ci = carry
        rs = pl.ds(pl.multiple_of(i * SUBLANES, SUBLANES), SUBLANES)
        br, bi = xre[rs, :], xim[rs, :]
        for dist in (1, 2, 4):
            keep = row >= dist
            sr = jnp.where(keep, pltpu.roll(br, dist, 0), 0.0)
            si = jnp.where(keep, pltpu.roll(bi, dist, 0), 0.0)
            lr = lpr_ref[dist - 1:dist, :]
            li = lpi_ref[dist - 1:dist, :]
            br, bi = br + lr * sr - li * si, bi + lr * si + li * sr
        pr, pi = lpr_ref[...], lpi_ref[...]
        xr = br + pr * cr - pi * ci
        xi = bi + pr * ci + pi * cr
        xre[rs, :] = xr
        xim[rs, :] = xi
        return xr[SUBLANES - 1:SUBLANES, :], xi[SUBLANES - 1:SUBLANES, :]

    cr, ci = lax.fori_loop(0, tb // SUBLANES, tile, (car_re[...], car_im[...]))
    car_re[...] = cr
    car_im[...] = ci
    wc = wc_ref[...]
    y = _bdot(xre[...], wc[0:S5_MODES]) + _bdot(xim[...], wc[S5_MODES:2 * S5_MODES])
    y = _gelu(y + d_ref[...] * u)
    y = y * _sigmoid(_bdot(y, gw_ref[...]) + gb_ref[...])
    o_ref[0] = _rms(y, og_ref[...])


def _s5_weights(a_re, a_im, log_dt, b_re, b_im, c_re, c_im):
    lam_re = jnp.minimum(a_re.astype(F32), -1e-4)
    lam_im = a_im.astype(F32)
    dt = jnp.exp(log_dt.astype(F32))[:, None]
    mag = jnp.exp(lam_re * dt)
    lb_re = mag * jnp.cos(lam_im * dt)
    lb_im = mag * jnp.sin(lam_im * dt)
    den = lam_re * lam_re + lam_im * lam_im
    c1_re = ((lb_re - 1.0) * lam_re + lb_im * lam_im) / den
    c1_im = (lb_im * lam_re - (lb_re - 1.0) * lam_im) / den
    br, bi = b_re.astype(F32), b_im.astype(F32)
    bb_re = c1_re[..., None] * br - c1_im[..., None] * bi
    bb_im = c1_re[..., None] * bi + c1_im[..., None] * br
    eye = jnp.eye(S5_GROUPS, dtype=F32)
    wb_re = jnp.einsum('gpc,gh->gchp', bb_re, eye).reshape(D_S5, S5_MODES)
    wb_im = jnp.einsum('gpc,gh->gchp', bb_im, eye).reshape(D_S5, S5_MODES)
    wb = jnp.concatenate([wb_re, wb_im], axis=1).astype(BF16)
    wc_re = jnp.einsum('gcp,gh->gphc', c_re.astype(F32), eye).reshape(S5_MODES, D_S5)
    wc_im = jnp.einsum('gcp,gh->gphc', c_im.astype(F32), eye).reshape(S5_MODES, D_S5)
    wc = jnp.concatenate([wc_re, -wc_im], axis=0).astype(BF16)
    pr, pi = [lb_re], [lb_im]
    for _ in range(SUBLANES - 1):
        pr, pi = pr + [pr[-1] * lb_re - pi[-1] * lb_im], pi + [pr[-1] * lb_im + pi[-1] * lb_re]
    lp_re = jnp.stack(pr).reshape(SUBLANES, S5_MODES)
    lp_im = jnp.stack(pi).reshape(SUBLANES, S5_MODES)
    return wb, wc, lp_re, lp_im


def _s5(u, prm, bsz, seq):
    tb = min(S5_BLOCK, seq)
    wb, wc, lp_re, lp_im = _s5_weights(prm['a_re'], prm['a_im'], prm['log_dt'], prm['b_re'], prm['b_im'],
                                       prm['c_re'], prm['c_im'])
    row = lambda a: a.reshape(1, -1).astype(F32)
    consts = [wb, wc, lp_re, lp_im, row(prm['d']), prm['glu_w'].astype(BF16), row(prm['glu_b']),
              row(prm['out_g'])]
    tok = pl.BlockSpec((1, tb, D_S5), lambda b, t: (b, t, 0))
    return pl.pallas_call(
        _s5_kernel, grid=(bsz, seq // tb),
        in_specs=[tok] + [_full(c.shape) for c in consts], out_specs=tok,
        out_shape=jax.ShapeDtypeStruct((bsz, seq, D_S5), F32),
        scratch_shapes=[pltpu.VMEM((1, S5_MODES), F32), pltpu.VMEM((1, S5_MODES), F32),
                        pltpu.VMEM((tb, S5_MODES), F32), pltpu.VMEM((tb, S5_MODES), F32)],
        compiler_params=_params(("parallel", "arbitrary")),
    )(u, *consts)


def _mix_xattn_kernel(x_ref, yr_ref, ys_ref, wo1_ref, wo2_ref, g_ref, wq_ref, k_ref, v_ref, wo_ref, o_ref):
    x1 = x_ref[0] + _bdot(yr_ref[0], wo1_ref[...]) + _bdot(ys_ref[0], wo2_ref[...])
    h = _rms(x1, g_ref[...])
    q = _bdot(h, wq_ref[...])
    km, vm = k_ref[0], v_ref[0]
    outs = []
    for hd in range(XA_HEADS):
        sl = slice(XA_HEAD * hd, XA_HEAD * (hd + 1))
        s = _bdot_nt(q[:, sl], km[:, sl]) * (XA_HEAD ** -0.5)
        s = s - jnp.max(s, axis=-1, keepdims=True)
        e = jnp.exp(s)
        p = e / jnp.sum(e, axis=-1, keepdims=True)
        outs.append(_bdot(p, vm[:, sl]))
    o = jnp.concatenate(outs, axis=1)
    o_ref[0] = x1 + _bdot(o, wo_ref[...])


def _mix_xattn(x, y_rw, y_s5, w_out, g, wq, kmem, vmem, wo, bsz, seq):
    tm = min(XA_BLOCK, seq)
    consts_a = [w_out[:D_RWKV].astype(BF16), w_out[D_RWKV:].astype(BF16), g.reshape(1, -1).astype(F32),
                wq.astype(BF16)]
    tok = lambda w: pl.BlockSpec((1, tm, w), lambda b, t: (b, t, 0))
    mem = pl.BlockSpec((1, N_MEM, D_MODEL), lambda b, t: (b, 0, 0))
    wo_b = wo.astype(BF16)
    return pl.pallas_call(
        _mix_xattn_kernel, grid=(bsz, seq // tm),
        in_specs=[tok(D_MODEL), tok(D_RWKV), tok(D_S5)] + [_full(c.shape) for c in consts_a]
                 + [mem, mem, _full(wo_b.shape)],
        out_specs=tok(D_MODEL), out_shape=jax.ShapeDtypeStruct((bsz, seq, D_MODEL), F32),
        compiler_params=_params(("parallel", "parallel")),
    )(x, y_rw, y_s5, *consts_a, kmem, vmem, wo_b)


def _top_rows(work, order, aux, val_ref, idx_ref):
    for it in range(PEER_TOPK):
        m = jnp.max(work, axis=0, keepdims=True)
        pos = jnp.min(jnp.where(work == m, order, jnp.inf), axis=0, keepdims=True)
        hit = order == pos
        val_ref[it:it + 1, :] = m
        if aux is None:
            idx_ref[it:it + 1, :] = pos
        else:
            idx_ref[it:it + 1, :] = jnp.sum(jnp.where(hit, aux, 0.0), axis=0, keepdims=True)
        work = jnp.where(hit, -jnp.inf, work)


_CAND_ROW_BLOCKS = [(0, PEER_TOPK), (1, SUBLANES), (2, SUBLANES), (3, SUBLANES)]
_CAND_COL_BLOCKS = [(0, PEER_TOPK, 4, 15), (1, SUBLANES, 4, 7), (2, SUBLANES, 4, 4)]
N_CAND = sum(n for _, n in _CAND_ROW_BLOCKS) + sum(n for _, n, _, _ in _CAND_COL_BLOCKS)


def _cand_consts(tm):
    flat, neg = [], []
    for a, nb in _CAND_ROW_BLOCKS:
        flat += [a * PEER_TOPK + b for b in range(nb)]
        neg += [0.0] * nb
    for b, na, lo, hi in _CAND_COL_BLOCKS:
        flat += [a * PEER_TOPK + b for a in range(na)]
        neg += [0.0 if lo <= a <= hi else -float('inf') for a in range(na)]
    flat = [f if n == 0.0 else 1000.0 + i for i, (f, n) in enumerate(zip(flat, neg))]
    col = lambda v: jnp.broadcast_to(jnp.asarray(v, F32)[:, None], (N_CAND, tm))
    return col(flat), col(neg)


def _cand_rows(row_vals, col_vals, combine):
    blocks = [combine(row_vals[a:a + 1, :], col_vals[0:nb, :]) for a, nb in _CAND_ROW_BLOCKS]
    blocks += [combine(row_vals[0:na, :], col_vals[b:b + 1, :]) for b, na, _, _ in _CAND_COL_BLOCKS]
    return jnp.concatenate(blocks, axis=0)


SEL_HEADS_PER_STEP = 2


def _peer_select_kernel(x_ref, g_ref, wq_ref, keys_ref, cflat_ref, cneg_ref, x8_ref, h8_ref, base_ref, shift_ref,
                        gate_ref, q3, idx_t, gate_t, *lists):
    tm = x_ref.shape[0]
    n = SEL_HEADS_PER_STEP
    s1, i1, s2, i2, top, eid = (lists[j * n:(j + 1) * n] for j in range(6))
    x = x_ref[...]
    h = _rms(x, g_ref[...])
    for r in range(D_MODEL // LANES):
        rows = pl.ds(r, tm, stride=D_MODEL // LANES)
        h8_ref[rows, :] = h[:, LANES * r:LANES * (r + 1)]
        x8_ref[rows, :] = x[:, LANES * r:LANES * (r + 1)]
    q = _bdot(h, wq_ref[...])
    for j in range(2 * PEER_HEADS):
        q3[j] = q[:, LANES * j:LANES * (j + 1)]
    iota_k = lax.broadcasted_iota(jnp.int32, (PEER_NKEYS, tm), 0).astype(F32)

    def heads(step, _):
        for u in range(SEL_HEADS_PER_STEP):
            hd = step * SEL_HEADS_PER_STEP + u
            sc1 = _bdot_nt(keys_ref[2 * hd], q3[2 * hd])
            sc2 = _bdot_nt(keys_ref[2 * hd + 1], q3[2 * hd + 1])
            _top_rows(sc1, iota_k, None, s1[u], i1[u])
            _top_rows(sc2, iota_k, None, s2[u], i2[u])
            cand = _cand_rows(s1[u][...], s2[u][...], lambda x, y: x + y) + cneg_ref[...]
            cidx = _cand_rows(i1[u][...], i2[u][...], lambda x, y: x * float(PEER_NKEYS) + y)
            _top_rows(cand, cflat_ref[...], cidx, top[u], eid[u])
            tv = top[u][...]
            e = jnp.exp(tv - jnp.max(tv, axis=0, keepdims=True))
            rs = pl.ds(pl.multiple_of(hd * PEER_TOPK, PEER_TOPK), PEER_TOPK)
            idx_t[rs, :] = eid[u][...]
            gate_t[rs, :] = e / jnp.sum(e, axis=0, keepdims=True)
        return 0

    lax.fori_loop(0, PEER_HEADS // SEL_HEADS_PER_STEP, heads, 0)
    e_t = idx_t[...].T
    pair = jnp.floor(e_t * 0.5)
    base_ref[...] = (pair * float(SUBLANES)).astype(jnp.int32)
    shift_ref[...] = ((e_t - 2.0 * pair) * 16.0).astype(jnp.int32)
    gate_ref[...] = gate_t[...].T


def _peer_select(x2, g, wq, keys):
    t = x2.shape[0]
    tm = SEL_BLOCK
    keys_b = keys.reshape(2 * PEER_HEADS, PEER_NKEYS, LANES).astype(BF16)
    wq_b = wq.astype(BF16)
    cflat, cneg = _cand_consts(tm)
    tokspec = lambda w: pl.BlockSpec((tm, w), lambda i: (i, 0))
    vm = lambda r: pltpu.VMEM((r, tm), F32)
    return pl.pallas_call(
        _peer_select_kernel, grid=(t // tm,),
        in_specs=[tokspec(D_MODEL), _full((1, D_MODEL)), _full(wq_b.shape), _full(keys_b.shape),
                  _full(cflat.shape), _full(cneg.shape)],
        out_specs=[pl.BlockSpec((tm * SUBLANES, LANES), lambda i: (i, 0))] * 2
                  + [tokspec(PEER_SEL), tokspec(PEER_SEL), tokspec(PEER_SEL)],
        out_shape=[jax.ShapeDtypeStruct((t * SUBLANES, LANES), F32)] * 2 + [
                   jax.ShapeDtypeStruct((t, PEER_SEL), jnp.int32),
                   jax.ShapeDtypeStruct((t, PEER_SEL), jnp.int32), jax.ShapeDtypeStruct((t, PEER_SEL), F32)],
        scratch_shapes=[pltpu.VMEM((2 * PEER_HEADS, tm, LANES), F32), vm(PEER_SEL), vm(PEER_SEL)]
                       + [vm(PEER_TOPK) for _ in range(6 * SEL_HEADS_PER_STEP)],
        compiler_params=_params(("parallel",)),
    )(x2, g.reshape(1, -1).astype(F32), wq_b, keys_b, cflat, cneg)


def _pack_table(tab):
    n, d = tab.shape
    bits = lax.bitcast_convert_type(tab.astype(BF16), jnp.uint16).astype(U32)
    bits = bits.reshape(n // 2, 2, d)
    packed = (bits[:, 0, :] << 16) | bits[:, 1, :]
    return lax.bitcast_convert_type(packed, jnp.int32).reshape(n // 2 * SUBLANES, LANES)


def _splat_into(src_ref, t, dst_ref, slot):
    tile = jnp.broadcast_to(src_ref[pl.ds(t, 1), :], (LANES, LANES)).T
    dst_ref[LANES * slot:LANES * (slot + 1), :] = tile


def _bcast_row(ref, row):
    return jnp.broadcast_to(ref[row:row + 1, :], (SUBLANES, LANES))


def _expert_tile(tab_ref, base, shift_splat, row):
    w = tab_ref[pl.ds(pl.multiple_of(base, SUBLANES), SUBLANES), :]
    return lax.bitcast_convert_type((w << _bcast_row(shift_splat, row)) & jnp.int32(-65536), F32)


def _token_rows(t):
    return pl.ds(pl.multiple_of(t * SUBLANES, SUBLANES), SUBLANES)


def _pipelined_tokens(tg, prep, work):
    prep(0, 0)

    def body(i, _):
        t0 = 2 * i
        prep(t0 + 1, 1)
        work(t0, 0)
        prep(jnp.minimum(t0 + 2, tg - 1), 0)
        work(t0 + 1, 1)
        return 0

    lax.fori_loop(0, tg // 2, body, 0)


def _table_spec(shape):
    return pl.BlockSpec(shape, lambda i: (0, 0), pipeline_mode=pl.Buffered(1))


N_CHUNK = D_MODEL // LANES
CHUNK_STRIDE = PEER_SEL + SUBLANES


def _peer_u_kernel(base_ref, shift_ref, h_ref, gate_ref, tab_ref, eye_ref, c_ref, shift_splat, *planes):
    tg = gate_ref.shape[0]
    ones = jnp.ones((LANES, LANES), BF16)

    half = N_CHUNK // 2

    def prep(t, slot):
        _splat_into(shift_ref, t, shift_splat, slot)

    def gather(t, slot):
        ht = h_ref[_token_rows(t), :]
        for kx in range(PEER_SEL):
            prod = _expert_tile(tab_ref, base_ref[t, kx], shift_splat, LANES * slot + kx) * ht
            fold = prod + pltpu.roll(prod, half, 0)
            planes[slot][pl.ds(kx, half, stride=CHUNK_STRIDE), :] = fold[0:half, :]

    def finish(t, slot):
        plane = planes[slot]
        acc = plane[0:PEER_SEL, :]
        for r in range(1, half):
            acc = acc + plane[CHUNK_STRIDE * r:CHUNK_STRIDE * r + PEER_SEL, :]
        hi = acc.astype(BF16)
        lo = (acc - hi.astype(F32)).astype(BF16)
        tot = jnp.dot(hi, ones, preferred_element_type=F32) + jnp.dot(lo, ones, preferred_element_type=F32)
        score = jnp.sum(tot * eye_ref[...], axis=0, keepdims=True)
        c_ref[pl.ds(t, 1), :] = gate_ref[pl.ds(t, 1), :] * _gelu(score)

    planes[1][...] = jnp.zeros_like(planes[1])
    prep(0, 0)

    def body(i, _):
        t0 = 2 * i
        prep(t0 + 1, 1)
        gather(t0, 0)
        finish(jnp.maximum(t0 - 1, 0), 1)
        prep(jnp.minimum(t0 + 2, tg - 1), 0)
        gather(t0 + 1, 1)
        finish(t0, 0)
        return 0

    lax.fori_loop(0, tg // 2, body, 0)
    finish(tg - 1, 1)


def _peer_u(base, shift, h8, gate, tab):
    t = base.shape[0]
    tg = GATHER_BLOCK
    eye = jnp.eye(LANES, dtype=F32)
    tokrow = pl.BlockSpec((tg, PEER_SEL), lambda i: (i, 0))
    return pl.pallas_call(
        _peer_u_kernel, grid=(t // tg,),
        in_specs=[pl.BlockSpec((tg, PEER_SEL), lambda i: (i, 0), memory_space=pltpu.SMEM), tokrow,
                  pl.BlockSpec((tg * SUBLANES, LANES), lambda i: (i, 0)), tokrow,
                  _table_spec(tab.shape), _full(eye.shape)],
        out_specs=tokrow, out_shape=jax.ShapeDtypeStruct((t, PEER_SEL), F32),
        scratch_shapes=[pltpu.VMEM((2 * LANES, LANES), jnp.int32)]
                       + [pltpu.VMEM((N_CHUNK // 2 * CHUNK_STRIDE, LANES), F32)] * 2,
        compiler_params=_params(("parallel",)),
    )(base, shift, h8, gate, tab, eye)


def _peer_v_kernel(base_ref, shift_ref, c_ref, x_ref, tab_ref, o_ref, shift_splat, c_splat, otile):
    tg = c_ref.shape[0]
    n_acc = 4

    def prep(t, slot):
        _splat_into(shift_ref, t, shift_splat, slot)
        _splat_into(c_ref, t, c_splat, slot)

    def work(t, slot):
        accs = [jnp.zeros((SUBLANES, LANES), F32) for _ in range(n_acc)]
        for kx in range(PEER_SEL):
            row = LANES * slot + kx
            accs[kx % n_acc] = accs[kx % n_acc] + (_bcast_row(c_splat, row)
                                                   * _expert_tile(tab_ref, base_ref[t, kx], shift_splat, row))
        rs = _token_rows(t)
        otile[rs, :] = x_ref[rs, :] + ((accs[0] + accs[1]) + (accs[2] + accs[3]))

    _pipelined_tokens(tg, prep, work)
    for r in range(N_CHUNK):
        o_ref[:, LANES * r:LANES * (r + 1)] = otile[pl.ds(r, tg, stride=N_CHUNK), :]


def _peer_v(base, shift, c, x8, tab):
    t = base.shape[0]
    tg = GATHER_BLOCK
    smem = pl.BlockSpec((tg, PEER_SEL), lambda i: (i, 0), memory_space=pltpu.SMEM)
    tokrow = pl.BlockSpec((tg, PEER_SEL), lambda i: (i, 0))
    tile = pl.BlockSpec((tg * SUBLANES, LANES), lambda i: (i, 0))
    return pl.pallas_call(
        _peer_v_kernel, grid=(t // tg,),
        in_specs=[smem, tokrow, tokrow, tile, _table_spec(tab.shape)],
        out_specs=pl.BlockSpec((tg, D_MODEL), lambda i: (i, 0)),
        out_shape=jax.ShapeDtypeStruct((t, D_MODEL), F32),
        scratch_shapes=[pltpu.VMEM((2 * LANES, LANES), jnp.int32), pltpu.VMEM((2 * LANES, LANES), F32),
                        pltpu.VMEM((tg * SUBLANES, LANES), F32)],
        compiler_params=_params(("parallel",)),
    )(base, shift, c, x8, tab)


def _final_norm_kernel(x_ref, g_ref, o_ref):
    o_ref[...] = _rms(x_ref[...], g_ref[...])


def _final_norm(x2, g):
    t, d = x2.shape
    tm = min(PROJ_BLOCK, t)
    spec = pl.BlockSpec((tm, d), lambda i: (i, 0))
    return pl.pallas_call(
        _final_norm_kernel, grid=(t // tm,), in_specs=[spec, _full((1, d))], out_specs=spec,
        out_shape=jax.ShapeDtypeStruct((t, d), F32), compiler_params=_params(("parallel",)),
    )(x2, g.reshape(1, d).astype(F32))


def kernel(x, mem, norm_mix, w_in, rw_mu, rw_w0, rw_w2, rw_a0, rw_a2, rw_g2, rw_kk, rw_ka, rw_rk, rw_v0, rw_v1, rw_v2, rw_lnx_g, rw_lnx_b, s5_a_re, s5_a_im, s5_log_dt, s5_b_re, s5_b_im, s5_c_re, s5_c_im, s5_d, s5_glu_w, s5_glu_b, s5_out_g, w_out, norm_xa, norm_mem, xa_wq, xa_wk, xa_wv, xa_wo, norm_ffn, peer_wq, peer_keys, peer_u, peer_v, norm_final):
    bsz, seq, d = x.shape
    t = bsz * seq
    depth = w_in.shape[0]
    mem2 = mem.reshape(bsz * N_MEM, d)
    v_first = None
    for l in range(depth):
        x2 = x.reshape(t, d)
        w_rw = w_in[l][:, :RW_COLS].astype(BF16)
        w_s5 = w_in[l][:, RW_COLS:].astype(BF16)
        ws = [w_rw, w_s5]
        if l > 0:
            ws.append(_pad_rows(rw_v1[l - 1].T, LANES, 0).T.astype(BF16))
        outs = _norm_proj(x2, norm_mix[l], ws, [F32] * len(ws), PROJ_BLOCK)
        z_rw = outs[0].reshape(bsz, seq, RW_COLS)
        u_s5 = outs[1].reshape(bsz, seq, D_S5)
        hv = outs[2].reshape(bsz, seq, LANES) if l > 0 else None
        rw_prm = dict(mu=rw_mu[l], w0=rw_w0[l], w2=rw_w2[l], a0=rw_a0[l], a2=rw_a2[l], g2=rw_g2[l],
                      kk=rw_kk[l], ka=rw_ka[l], rk=rw_rk[l], lng=rw_lnx_g[l], lnb=rw_lnx_b[l])
        if l > 0:
            rw_prm.update(v0=rw_v0[l - 1], v2=rw_v2[l - 1])
        y_rw, v_first = _rwkv(z_rw, hv, v_first, rw_prm, bsz, seq)
        s5_prm = dict(a_re=s5_a_re[l], a_im=s5_a_im[l], log_dt=s5_log_dt[l], b_re=s5_b_re[l], b_im=s5_b_im[l],
                      c_re=s5_c_re[l], c_im=s5_c_im[l], d=s5_d[l], glu_w=s5_glu_w[l], glu_b=s5_glu_b[l],
                      out_g=s5_out_g[l])
        y_s5 = _s5(u_s5, s5_prm, bsz, seq)
        kv = _norm_proj(mem2, norm_mem[l], [xa_wk[l].astype(BF16), xa_wv[l].astype(BF16)], [BF16, BF16],
                        PROJ_BLOCK)
        kmem = kv[0].reshape(bsz, N_MEM, d)
        vmem = kv[1].reshape(bsz, N_MEM, d)
        x = _mix_xattn(x, y_rw, y_s5, w_out[l], norm_xa[l], xa_wq[l], kmem, vmem, xa_wo[l], bsz, seq)
        x2 = x.reshape(t, d)
        x8, h8, base, shift, gate = _peer_select(x2, norm_ffn[l], peer_wq[l], peer_keys[l])
        c = _peer_u(base, shift, h8, gate, _pack_table(peer_u[l]))
        x = _peer_v(base, shift, c, x8, _pack_table(peer_v[l])).reshape(bsz, seq, d)
    return _final_norm(x.reshape(t, d), norm_final).reshape(bsz, seq, d)
```

```python
import functools
import math

import jax
import jax.numpy as jnp
from jax import lax
from jax.experimental import pallas as pl
from jax.experimental.pallas import tpu as pltpu
from jax.experimental.pallas import tpu_sc as plsc

F32 = jnp.float32
BF16 = jnp.bfloat16
U32 = jnp.uint32

LANES = 128
SUBLANES = 8
VMEM_LIMIT = 56 * 1024 * 1024

D_MODEL = 1024
D_RWKV = 512
RW_HEAD = 64
RW_COLS = 1792
D_S5 = 512
S5_GROUPS = 32
S5_CH = 16
S5_STATE = 64
S5_MODES = S5_GROUPS * S5_STATE
N_MEM = 256
XA_HEADS = 4
XA_HEAD = 256
PEER_HEADS = 8
PEER_NKEYS = 128
PEER_TOPK = 16
PEER_SEL = PEER_HEADS * PEER_TOPK
RMS_EPS = 1e-6
GN_EPS = 64e-5

RW_CHUNK = 128
S5_BLOCK = 256
PROJ_BLOCK = 512
XA_BLOCK = 256
SEL_BLOCK = 128
GATHER_BLOCK = 64


def _params(sem):
    return pltpu.CompilerParams(dimension_semantics=sem, vmem_limit_bytes=VMEM_LIMIT)


def _rms(x, g):
    ms = jnp.mean(x * x, axis=-1, keepdims=True)
    return x * lax.rsqrt(ms + RMS_EPS) * g


def _bdot(a, b):
    return jnp.dot(a.astype(BF16), b.astype(BF16), preferred_element_type=F32)


def _bdot_nt(a, b):
    return lax.dot_general(a.astype(BF16), b.astype(BF16), (((1,), (1,)), ((), ())),
                           preferred_element_type=F32)


def _sigmoid(x):
    return 1.0 / (1.0 + jnp.exp(-x))


def _softplus(x):
    return jnp.maximum(x, 0.0) + jnp.log(1.0 + jnp.exp(-jnp.abs(x)))


def _gelu(x):
    return 0.5 * x * (1.0 + jnp.tanh(math.sqrt(2.0 / math.pi) * (x + 0.044715 * (x * x * x))))


def _full(shape):
    n = len(shape)
    return pl.BlockSpec(shape, lambda *_: (0,) * n)


def _norm_proj_kernel(*refs, n_out):
    x_ref, g_ref = refs[0], refs[1]
    w_refs = refs[2:2 + n_out]
    o_refs = refs[2 + n_out:]
    h = _rms(x_ref[...], g_ref[...]).astype(BF16)
    for w_ref, o_ref in zip(w_refs, o_refs):
        o_ref[...] = jnp.dot(h, w_ref[...], preferred_element_type=F32).astype(o_ref.dtype)


def _norm_proj(x2, g, ws, out_dtypes, block):
    t, d = x2.shape
    block = min(block, t)
    in_specs = [pl.BlockSpec((block, d), lambda i: (i, 0)), _full((1, d))]
    in_specs += [_full(w.shape) for w in ws]
    out_specs = [pl.BlockSpec((block, w.shape[1]), lambda i: (i, 0)) for w in ws]
    out_shape = [jax.ShapeDtypeStruct((t, w.shape[1]), dt) for w, dt in zip(ws, out_dtypes)]
    return pl.pallas_call(
        functools.partial(_norm_proj_kernel, n_out=len(ws)),
        grid=(t // block,), in_specs=in_specs, out_specs=out_specs, out_shape=out_shape,
        compiler_params=_params(("parallel",)),
    )(x2, g.reshape(1, d), *ws)


def _seg_sum(x, mseg):
    hi = x.astype(BF16)
    lo = (x - hi.astype(F32)).astype(BF16)
    return (jnp.dot(hi, mseg, preferred_element_type=F32)
            + jnp.dot(lo, mseg, preferred_element_type=F32))


def _col_bcast(row):
    return jnp.broadcast_to(row, (LANES, LANES)).T


def _rwkv_kernel(*refs, first_layer):
    if first_layer:
        (z_ref, mu_ref, w0_ref, w2_ref, a0_ref, a2_ref, g2_ref, kk_ref, ka_ref, rk_ref,
         lng_ref, lnb_ref, mseg_ref, y_ref, vf_out_ref, zprev, hst) = refs
    else:
        (z_ref, hv_ref, vf_ref, v0_ref, v2_ref, mu_ref, w0_ref, w2_ref, a0_ref, a2_ref, g2_ref,
         kk_ref, ka_ref, rk_ref, lng_ref, lnb_ref, mseg_ref, y_ref, zprev, hst) = refs
    L = RW_CHUNK

    @pl.when(pl.program_id(1) == 0)
    def _():
        zprev[...] = jnp.zeros_like(zprev)
        hst[...] = jnp.zeros_like(hst)

    z = z_ref[0]
    row = lax.broadcasted_iota(jnp.int32, (L, 1), 0)
    zs = jnp.where(row == 0, zprev[...], pltpu.roll(z, 1, 0))
    zprev[...] = z[L - 1:L, :]
    z = z + (zs - z) * mu_ref[...]
    r = z[:, 0:512]
    k = z[:, 512:1024]
    v = z[:, 1024:1536]
    wa = z[:, 1536:1664]
    gd = z[:, 1664:1792]
    mseg = mseg_ref[...]

    wlin = w0_ref[...] + _bdot(jnp.tanh(wa), w2_ref[...])
    lw = -jnp.exp(-_softplus(-wlin) - 0.5)
    a = _sigmoid(a0_ref[...] + _bdot(wa, a2_ref[...]))
    g = _bdot(_sigmoid(gd), g2_ref[...])
    if first_layer:
        vf_out_ref[0] = v
    else:
        v = v + (vf_ref[0] - v) * _sigmoid(v0_ref[...] + _bdot(hv_ref[0], v2_ref[...]))
    kk = k * kk_ref[...]
    kk = kk / jnp.maximum(jnp.sqrt(_seg_sum(kk * kk, mseg)), 1e-12)
    k2 = k * (1.0 + (a - 1.0) * ka_ref[...])
    av = -kk
    bv = kk * a

    ti = lax.broadcasted_iota(jnp.int32, (L, L), 0)
    si = lax.broadcasted_iota(jnp.int32, (L, L), 1)
    tril = (ti >= si).astype(F32)
    cum = jnp.dot(tril, lw, preferred_element_type=F32, precision=lax.Precision.HIGHEST)
    mid = cum[L // 2 - 1:L // 2, :]
    cm = cum - mid
    ecum = jnp.exp(cm)
    einv = jnp.exp(-cm)
    rt = r * ecum
    kt = k2 * einv
    bt = bv * einv
    at = av * jnp.exp(cm - lw)
    p_end = jnp.exp(cum[L - 1:L, :])
    e_end = ecum[L - 1:L, :]
    e_mid = jnp.exp(mid)

    lane = lax.broadcasted_iota(jnp.int32, (1, LANES), 1)
    m0 = (lane < RW_HEAD).astype(F32)
    m1 = 1.0 - m0
    strict = ti > si
    incl = ti >= si
    bi = lax.broadcasted_iota(jnp.int32, (LANES, LANES), 0) < RW_HEAD
    bj = lax.broadcasted_iota(jnp.int32, (LANES, LANES), 1) < RW_HEAD
    bdmask = (bi == bj).astype(F32)
    zeros_ll = jnp.zeros((L, L), F32)

    ys = []
    for p in range(D_RWKV // LANES):
        sl = slice(LANES * p, LANES * (p + 1))
        A, B, K, R, V = at[:, sl], bt[:, sl], kt[:, sl], rt[:, sl], v[:, sl]
        btkt = jnp.concatenate([B.T, K.T], axis=1)
        h0 = hst[p]
        h0m = h0 * _col_bcast(e_mid[:, sl])
        sc = _bdot(jnp.concatenate([A * m0, A * m1, R * m0, R * m1], axis=0), btkt)
        aab = [jnp.where(strict, sc[e * L:(e + 1) * L, 0:L], 0.0) for e in range(2)]
        aak = [jnp.where(strict, sc[e * L:(e + 1) * L, L:2 * L], 0.0) for e in range(2)]
        arb = [jnp.where(incl, sc[(2 + e) * L:(3 + e) * L, 0:L], 0.0) for e in range(2)]
        ark = [jnp.where(incl, sc[(2 + e) * L:(3 + e) * L, L:2 * L], 0.0) for e in range(2)]
        arh = _bdot(jnp.concatenate([A, R], axis=0), h0m)
        v01 = jnp.concatenate([V * m0, V * m1], axis=0)
        x = arh[0:L] + _bdot(jnp.concatenate(aak, axis=1), v01)
        pm = jnp.concatenate(aab, axis=1)
        n_fac = int(math.log2(L))
        for it in range(n_fac):
            x = x + _bdot(pm, jnp.concatenate([x * m0, x * m1], axis=0))
            if it + 1 < n_fac:
                pd = jnp.concatenate(
                    [jnp.concatenate([pm[:, 0:L], zeros_ll], axis=1),
                     jnp.concatenate([zeros_ll, pm[:, L:2 * L]], axis=1)], axis=0)
                pm = _bdot(pm, pd)
        u = x
        yp = arh[L:2 * L] + _bdot(jnp.concatenate(arb + ark, axis=1),
                                  jnp.concatenate([u * m0, u * m1, v01], axis=0))
        upd = _bdot(btkt, jnp.concatenate([u, V], axis=0))
        hst[p] = (h0 * _col_bcast(p_end[:, sl]) + upd * _col_bcast(e_end[:, sl])) * bdmask
        ys.append(yp)
    y = jnp.concatenate(ys, axis=1)

    mean = _seg_sum(y, mseg) * (1.0 / RW_HEAD)
    d = y - mean
    var = _seg_sum(d * d, mseg) * (1.0 / RW_HEAD)
    yn = d * lax.rsqrt(var + GN_EPS) * lng_ref[...] + lnb_ref[...]
    bonus = _seg_sum(r * k2 * rk_ref[...], mseg) * v
    y_ref[0] = (yn + bonus) * g


def _pad_rows(w, rows, offset):
    out = jnp.zeros((rows, w.shape[1]), w.dtype)
    return out.at[offset:offset + w.shape[0]].set(w)


def _rwkv(z_rw, hv, v_first, prm, bsz, seq):
    L = RW_CHUNK
    first = v_first is None
    row = lambda a: a.reshape(1, -1).astype(F32)
    hid = jnp.arange(D_RWKV) // RW_HEAD
    mseg = (hid[:, None] == hid[None, :]).astype(BF16)
    w2p = _pad_rows(prm['w2'], LANES, 0).astype(BF16)
    a2p = _pad_rows(prm['a2'], LANES, 64).astype(BF16)
    common = [row(prm['mu']), row(prm['w0']), w2p, row(prm['a0']), a2p, prm['g2'].astype(BF16),
              row(prm['kk']), row(prm['ka']), row(prm['rk']), row(prm['lng']), row(prm['lnb']), mseg]
    tok = lambda w: pl.BlockSpec((1, L, w), lambda b, t: (b, t, 0))
    common_specs = [_full(c.shape) for c in common]
    y_shape = jax.ShapeDtypeStruct((bsz, seq, D_RWKV), F32)
    scratch = [pltpu.VMEM((1, RW_COLS), F32), pltpu.VMEM((D_RWKV // LANES, LANES, LANES), F32)]
    if first:
        args = [z_rw] + common
        in_specs = [tok(RW_COLS)] + common_specs
        out_shape = [y_shape, y_shape]
        out_specs = [tok(D_RWKV), tok(D_RWKV)]
    else:
        v2p = _pad_rows(prm['v2'], LANES, 0).astype(BF16)
        extra = [row(prm['v0']), v2p]
        args = [z_rw, hv, v_first] + extra + common
        in_specs = [tok(RW_COLS), tok(LANES), tok(D_RWKV)] + [_full(c.shape) for c in extra] + common_specs
        out_shape = [y_shape]
        out_specs = [tok(D_RWKV)]
    outs = pl.pallas_call(
        functools.partial(_rwkv_kernel, first_layer=first),
        grid=(bsz, seq // L), in_specs=in_specs, out_specs=out_specs, out_shape=out_shape,
        scratch_shapes=scratch, compiler_params=_params(("parallel", "arbitrary")),
    )(*args)
    return (outs[0], outs[1]) if first else (outs[0], v_first)


def _s5_kernel(u_ref, wb_ref, wc_ref, lpr_ref, lpi_ref, d_ref, gw_ref, gb_ref, og_ref, o_ref,
               car_re, car_im, xre, xim):
    tb = u_ref.shape[1]

    @pl.when(pl.program_id(1) == 0)
    def _():
        car_re[...] = jnp.zeros_like(car_re)
        car_im[...] = jnp.zeros_like(car_im)

    u = u_ref[0]
    bu = _bdot(u, wb_ref[...])
    xre[...] = bu[:, 0:S5_MODES]
    xim[...] = bu[:, S5_MODES:2 * S5_MODES]
    row = lax.broadcasted_iota(jnp.int32, (SUBLANES, 1), 0)

    def tile(i, carry):
        cr, ci = carry
        rs = pl.ds(pl.multiple_of(i * SUBLANES, SUBLANES), SUBLANES)
        br, bi = xre[rs, :], xim[rs, :]
        for dist in (1, 2, 4):
            keep = row >= dist
            sr = jnp.where(keep, pltpu.roll(br, dist, 0), 0.0)
            si = jnp.where(keep, pltpu.roll(bi, dist, 0), 0.0)
            lr = lpr_ref[dist - 1:dist, :]
            li = lpi_ref[dist - 1:dist, :]
            br, bi = br + lr * sr - li * si, bi + lr * si + li * sr
        pr, pi = lpr_ref[...], lpi_ref[...]
        xr = br + pr * cr - pi * ci
        xi = bi + pr * ci + pi * cr
        xre[rs, :] = xr
        xim[rs, :] = xi
        return xr[SUBLANES - 1:SUBLANES, :], xi[SUBLANES - 1:SUBLANES, :]

    cr, ci = lax.fori_loop(0, tb // SUBLANES, tile, (car_re[...], car_im[...]))
    car_re[...] = cr
    car_im[...] = ci
    wc = wc_ref[...]
    y = _bdot(xre[...], wc[0:S5_MODES]) + _bdot(xim[...], wc[S5_MODES:2 * S5_MODES])
    y = _gelu(y + d_ref[...] * u)
    y = y * _sigmoid(_bdot(y, gw_ref[...]) + gb_ref[...])
    o_ref[0] = _rms(y, og_ref[...])


def _s5_weights(a_re, a_im, log_dt, b_re, b_im, c_re, c_im):
    lam_re = jnp.minimum(a_re.astype(F32), -1e-4)
    lam_im = a_im.astype(F32)
    dt = jnp.exp(log_dt.astype(F32))[:, None]
    mag = jnp.exp(lam_re * dt)
    lb_re = mag * jnp.cos(lam_im * dt)
    lb_im = mag * jnp.sin(lam_im * dt)
    den = lam_re * lam_re + lam_im * lam_im
    c1_re = ((lb_re - 1.0) * lam_re + lb_im * lam_im) / den
    c1_im = (lb_im * lam_re - (lb_re - 1.0) * lam_im) / den
    br, bi = b_re.astype(F32), b_im.astype(F32)
    bb_re = c1_re[..., None] * br - c1_im[..., None] * bi
    bb_im = c1_re[..., None] * bi + c1_im[..., None] * br
    eye = jnp.eye(S5_GROUPS, dtype=F32)
    wb_re = jnp.einsum('gpc,gh->gchp', bb_re, eye).reshape(D_S5, S5_MODES)
    wb_im = jnp.einsum('gpc,gh->gchp', bb_im, eye).reshape(D_S5, S5_MODES)
    wb = jnp.concatenate([wb_re, wb_im], axis=1).astype(BF16)
    wc_re = jnp.einsum('gcp,gh->gphc', c_re.astype(F32), eye).reshape(S5_MODES, D_S5)
    wc_im = jnp.einsum('gcp,gh->gphc', c_im.astype(F32), eye).reshape(S5_MODES, D_S5)
    wc = jnp.concatenate([wc_re, -wc_im], axis=0).astype(BF16)
    pr, pi = [lb_re], [lb_im]
    for _ in range(SUBLANES - 1):
        pr, pi = pr + [pr[-1] * lb_re - pi[-1] * lb_im], pi + [pr[-1] * lb_im + pi[-1] * lb_re]
    lp_re = jnp.stack(pr).reshape(SUBLANES, S5_MODES)
    lp_im = jnp.stack(pi).reshape(SUBLANES, S5_MODES)
    return wb, wc, lp_re, lp_im


def _s5(u, prm, bsz, seq):
    tb = min(S5_BLOCK, seq)
    wb, wc, lp_re, lp_im = _s5_weights(prm['a_re'], prm['a_im'], prm['log_dt'], prm['b_re'], prm['b_im'],
                                       prm['c_re'], prm['c_im'])
    row = lambda a: a.reshape(1, -1).astype(F32)
    consts = [wb, wc, lp_re, lp_im, row(prm['d']), prm['glu_w'].astype(BF16), row(prm['glu_b']),
              row(prm['out_g'])]
    tok = pl.BlockSpec((1, tb, D_S5), lambda b, t: (b, t, 0))
    return pl.pallas_call(
        _s5_kernel, grid=(bsz, seq // tb),
        in_specs=[tok] + [_full(c.shape) for c in consts], out_specs=tok,
        out_shape=jax.ShapeDtypeStruct((bsz, seq, D_S5), F32),
        scratch_shapes=[pltpu.VMEM((1, S5_MODES), F32), pltpu.VMEM((1, S5_MODES), F32),
                        pltpu.VMEM((tb, S5_MODES), F32), pltpu.VMEM((tb, S5_MODES), F32)],
        compiler_params=_params(("parallel", "arbitrary")),
    )(u, *consts)


def _mix_xattn_kernel(x_ref, yr_ref, ys_ref, wo1_ref, wo2_ref, g_ref, wq_ref, k_ref, v_ref, wo_ref, o_ref):
    x1 = x_ref[0] + _bdot(yr_ref[0], wo1_ref[...]) + _bdot(ys_ref[0], wo2_ref[...])
    h = _rms(x1, g_ref[...])
    q = _bdot(h, wq_ref[...])
    km, vm = k_ref[0], v_ref[0]
    outs = []
    for hd in range(XA_HEADS):
        sl = slice(XA_HEAD * hd, XA_HEAD * (hd + 1))
        s = _bdot_nt(q[:, sl], km[:, sl]) * (XA_HEAD ** -0.5)
        s = s - jnp.max(s, axis=-1, keepdims=True)
        e = jnp.exp(s)
        p = e / jnp.sum(e, axis=-1, keepdims=True)
        outs.append(_bdot(p, vm[:, sl]))
    o = jnp.concatenate(outs, axis=1)
    o_ref[0] = x1 + _bdot(o, wo_ref[...])


def _mix_xattn(x, y_rw, y_s5, w_out, g, wq, kmem, vmem, wo, bsz, seq):
    tm = min(XA_BLOCK, seq)
    consts_a = [w_out[:D_RWKV].astype(BF16), w_out[D_RWKV:].astype(BF16), g.reshape(1, -1).astype(F32),
                wq.astype(BF16)]
    tok = lambda w: pl.BlockSpec((1, tm, w), lambda b, t: (b, t, 0))
    mem = pl.BlockSpec((1, N_MEM, D_MODEL), lambda b, t: (b, 0, 0))
    wo_b = wo.astype(BF16)
    return pl.pallas_call(
        _mix_xattn_kernel, grid=(bsz, seq // tm),
        in_specs=[tok(D_MODEL), tok(D_RWKV), tok(D_S5)] + [_full(c.shape) for c in consts_a]
                 + [mem, mem, _full(wo_b.shape)],
        out_specs=tok(D_MODEL), out_shape=jax.ShapeDtypeStruct((bsz, seq, D_MODEL), F32),
        compiler_params=_params(("parallel", "parallel")),
    )(x, y_rw, y_s5, *consts_a, kmem, vmem, wo_b)


def _top_rows(work, order, aux, val_ref, idx_ref):
    for it in range(PEER_TOPK):
        m = jnp.max(work, axis=0, keepdims=True)
        pos = jnp.min(jnp.where(work == m, order, jnp.inf), axis=0, keepdims=True)
        hit = order == pos
        val_ref[it:it + 1, :] = m
        if aux is None:
            idx_ref[it:it + 1, :] = pos
        else:
            idx_ref[it:it + 1, :] = jnp.sum(jnp.where(hit, aux, 0.0), axis=0, keepdims=True)
        work = jnp.where(hit, -jnp.inf, work)


_CAND_ROW_BLOCKS = [(0, PEER_TOPK), (1, SUBLANES), (2, SUBLANES), (3, SUBLANES)]
_CAND_COL_BLOCKS = [(0, PEER_TOPK, 4, 15), (1, SUBLANES, 4, 7), (2, SUBLANES, 4, 4)]
N_CAND = sum(n for _, n in _CAND_ROW_BLOCKS) + sum(n for _, n, _, _ in _CAND_COL_BLOCKS)


def _cand_consts(tm):
    flat, neg = [], []
    for a, nb in _CAND_ROW_BLOCKS:
        flat += [a * PEER_TOPK + b for b in range(nb)]
        neg += [0.0] * nb
    for b, na, lo, hi in _CAND_COL_BLOCKS:
        flat += [a * PEER_TOPK + b for a in range(na)]
        neg += [0.0 if lo <= a <= hi else -float('inf') for a in range(na)]
    flat = [f if n == 0.0 else 1000.0 + i for i, (f, n) in enumerate(zip(flat, neg))]
    col = lambda v: jnp.broadcast_to(jnp.asarray(v, F32)[:, None], (N_CAND, tm))
    return col(flat), col(neg)


def _cand_rows(row_vals, col_vals, combine):
    blocks = [combine(row_vals[a:a + 1, :], col_vals[0:nb, :]) for a, nb in _CAND_ROW_BLOCKS]
    blocks += [combine(row_vals[0:na, :], col_vals[b:b + 1, :]) for b, na, _, _ in _CAND_COL_BLOCKS]
    return jnp.concatenate(blocks, axis=0)


SEL_HEADS_PER_STEP = 2


def _peer_select_kernel(x_ref, g_ref, wq_ref, keys_ref, cflat_ref, cneg_ref, x8_ref, h8_ref, base_ref, shift_ref,
                        gate_ref, q3, idx_t, gate_t, *lists):
    tm = x_ref.shape[0]
    n = SEL_HEADS_PER_STEP
    s1, i1, s2, i2, top, eid = (lists[j * n:(j + 1) * n] for j in range(6))
    x = x_ref[...]
    h = _rms(x, g_ref[...])
    for r in range(D_MODEL // LANES):
        rows = pl.ds(r, tm, stride=D_MODEL // LANES)
        h8_ref[rows, :] = h[:, LANES * r:LANES * (r + 1)]
        x8_ref[rows, :] = x[:, LANES * r:LANES * (r + 1)]
    q = _bdot(h, wq_ref[...])
    for j in range(2 * PEER_HEADS):
        q3[j] = q[:, LANES * j:LANES * (j + 1)]
    iota_k = lax.broadcasted_iota(jnp.int32, (PEER_NKEYS, tm), 0).astype(F32)

    def heads(step, _):
        for u in range(SEL_HEADS_PER_STEP):
            hd = step * SEL_HEADS_PER_STEP + u
            sc1 = _bdot_nt(keys_ref[2 * hd], q3[2 * hd])
            sc2 = _bdot_nt(keys_ref[2 * hd + 1], q3[2 * hd + 1])
            _top_rows(sc1, iota_k, None, s1[u], i1[u])
            _top_rows(sc2, iota_k, None, s2[u], i2[u])
            cand = _cand_rows(s1[u][...], s2[u][...], lambda x, y: x + y) + cneg_ref[...]
            cidx = _cand_rows(i1[u][...], i2[u][...], lambda x, y: x * float(PEER_NKEYS) + y)
            _top_rows(cand, cflat_ref[...], cidx, top[u], eid[u])
            tv = top[u][...]
            e = jnp.exp(tv - jnp.max(tv, axis=0, keepdims=True))
            rs = pl.ds(pl.multiple_of(hd * PEER_TOPK, PEER_TOPK), PEER_TOPK)
            idx_t[rs, :] = eid[u][...]
            gate_t[rs, :] = e / jnp.sum(e, axis=0, keepdims=True)
        return 0

    lax.fori_loop(0, PEER_HEADS // SEL_HEADS_PER_STEP, heads, 0)
    e_t = idx_t[...].T
    pair = jnp.floor(e_t * 0.5)
    base_ref[...] = (pair * float(SUBLANES)).astype(jnp.int32)
    shift_ref[...] = ((e_t - 2.0 * pair) * 16.0).astype(jnp.int32)
    gate_ref[...] = gate_t[...].T


def _peer_select(x2, g, wq, keys):
    t = x2.shape[0]
    tm = SEL_BLOCK
    keys_b = keys.reshape(2 * PEER_HEADS, PEER_NKEYS, LANES).astype(BF16)
    wq_b = wq.astype(BF16)
    cflat, cneg = _cand_consts(tm)
    tokspec = lambda w: pl.BlockSpec((tm, w), lambda i: (i, 0))
    vm = lambda r: pltpu.VMEM((r, tm), F32)
    return pl.pallas_call(
        _peer_select_kernel, grid=(t // tm,),
        in_specs=[tokspec(D_MODEL), _full((1, D_MODEL)), _full(wq_b.shape), _full(keys_b.shape),
                  _full(cflat.shape), _full(cneg.shape)],
        out_specs=[pl.BlockSpec((tm * SUBLANES, LANES), lambda i: (i, 0))] * 2
                  + [tokspec(PEER_SEL), tokspec(PEER_SEL), tokspec(PEER_SEL)],
        out_shape=[jax.ShapeDtypeStruct((t * SUBLANES, LANES), F32)] * 2 + [
                   jax.ShapeDtypeStruct((t, PEER_SEL), jnp.int32),
                   jax.ShapeDtypeStruct((t, PEER_SEL), jnp.int32), jax.ShapeDtypeStruct((t, PEER_SEL), F32)],
        scratch_shapes=[pltpu.VMEM((2 * PEER_HEADS, tm, LANES), F32), vm(PEER_SEL), vm(PEER_SEL)]
                       + [vm(PEER_TOPK) for _ in range(6 * SEL_HEADS_PER_STEP)],
        compiler_params=_params(("parallel",)),
    )(x2, g.reshape(1, -1).astype(F32), wq_b, keys_b, cflat, cneg)


def _pack_table(tab):
    n, d = tab.shape
    bits = lax.bitcast_convert_type(tab.astype(BF16), jnp.uint16).astype(U32)
    bits = bits.reshape(n // 2, 2, d)
    packed = (bits[:, 0, :] << 16) | bits[:, 1, :]
    return lax.bitcast_convert_type(packed, jnp.int32).reshape(n // 2 * SUBLANES, LANES)


def _splat_into(src_ref, t, dst_ref, slot):
    tile = jnp.broadcast_to(src_ref[pl.ds(t, 1), :], (LANES, LANES)).T
    dst_ref[LANES * slot:LANES * (slot + 1), :] = tile


def _bcast_row(ref, row):
    return jnp.broadcast_to(ref[row:row + 1, :], (SUBLANES, LANES))


def _expert_tile(tab_ref, base, shift_splat, row):
    w = tab_ref[pl.ds(pl.multiple_of(base, SUBLANES), SUBLANES), :]
    return lax.bitcast_convert_type((w << _bcast_row(shift_splat, row)) & jnp.int32(-65536), F32)


def _token_rows(t):
    return pl.ds(pl.multiple_of(t * SUBLANES, SUBLANES), SUBLANES)


def _pipelined_tokens(tg, prep, work):
    prep(0, 0)

    def body(i, _):
        t0 = 2 * i
        prep(t0 + 1, 1)
        work(t0, 0)
        prep(jnp.minimum(t0 + 2, tg - 1), 0)
        work(t0 + 1, 1)
        return 0

    lax.fori_loop(0, tg // 2, body, 0)


def _table_spec(shape):
    return pl.BlockSpec(shape, lambda i: (0, 0), pipeline_mode=pl.Buffered(1))


N_CHUNK = D_MODEL // LANES
CHUNK_STRIDE = PEER_SEL + SUBLANES


def _peer_u_kernel(base_ref, shift_ref, h_ref, gate_ref, tab_ref, eye_ref, c_ref, shift_splat, *planes):
    tg = gate_ref.shape[0]
    ones = jnp.ones((LANES, LANES), BF16)

    half = N_CHUNK // 2

    def prep(t, slot):
        _splat_into(shift_ref, t, shift_splat, slot)

    def gather(t, slot):
        ht = h_ref[_token_rows(t), :]
        for kx in range(PEER_SEL):
            prod = _expert_tile(tab_ref, base_ref[t, kx], shift_splat, LANES * slot + kx) * ht
            fold = prod + pltpu.roll(prod, half, 0)
            planes[slot][pl.ds(kx, half, stride=CHUNK_STRIDE), :] = fold[0:half, :]

    def finish(t, slot):
        plane = planes[slot]
        acc = plane[0:PEER_SEL, :]
        for r in range(1, half):
            acc = acc + plane[CHUNK_STRIDE * r:CHUNK_STRIDE * r + PEER_SEL, :]
        hi = acc.astype(BF16)
        lo = (acc - hi.astype(F32)).astype(BF16)
        tot = jnp.dot(hi, ones, preferred_element_type=F32) + jnp.dot(lo, ones, preferred_element_type=F32)
        score = jnp.sum(tot * eye_ref[...], axis=0, keepdims=True)
        c_ref[pl.ds(t, 1), :] = gate_ref[pl.ds(t, 1), :] * _gelu(score)

    planes[1][...] = jnp.zeros_like(planes[1])
    prep(0, 0)

    def body(i, _):
        t0 = 2 * i
        prep(t0 + 1, 1)
        gather(t0, 0)
        finish(jnp.maximum(t0 - 1, 0), 1)
        prep(jnp.minimum(t0 + 2, tg - 1), 0)
        gather(t0 + 1, 1)
        finish(t0, 0)
        return 0

    lax.fori_loop(0, tg // 2, body, 0)
    finish(tg - 1, 1)


def _peer_u(base, shift, h8, gate, tab):
    t = base.shape[0]
    tg = GATHER_BLOCK
    eye = jnp.eye(LANES, dtype=F32)
    tokrow = pl.BlockSpec((tg, PEER_SEL), lambda i: (i, 0))
    return pl.pallas_call(
        _peer_u_kernel, grid=(t // tg,),
        in_specs=[pl.BlockSpec((tg, PEER_SEL), lambda i: (i, 0), memory_space=pltpu.SMEM), tokrow,
                  pl.BlockSpec((tg * SUBLANES, LANES), lambda i: (i, 0)), tokrow,
                  _table_spec(tab.shape), _full(eye.shape)],
        out_specs=tokrow, out_shape=jax.ShapeDtypeStruct((t, PEER_SEL), F32),
        scratch_shapes=[pltpu.VMEM((2 * LANES, LANES), jnp.int32)]
                       + [pltpu.VMEM((N_CHUNK // 2 * CHUNK_STRIDE, LANES), F32)] * 2,
        compiler_params=_params(("parallel",)),
    )(base, shift, h8, gate, tab, eye)


def _peer_v_kernel(base_ref, shift_ref, c_ref, x_ref, tab_ref, o_ref, shift_splat, c_splat, otile):
    tg = c_ref.shape[0]
    n_acc = 4

    def prep(t, slot):
        _splat_into(shift_ref, t, shift_splat, slot)
        _splat_into(c_ref, t, c_splat, slot)

    def work(t, slot):
        accs = [jnp.zeros((SUBLANES, LANES), F32) for _ in range(n_acc)]
        for kx in range(PEER_SEL):
            row = LANES * slot + kx
            accs[kx % n_acc] = accs[kx % n_acc] + (_bcast_row(c_splat, row)
                                                   * _expert_tile(tab_ref, base_ref[t, kx], shift_splat, row))
        rs = _token_rows(t)
        otile[rs, :] = x_ref[rs, :] + ((accs[0] + accs[1]) + (accs[2] + accs[3]))

    _pipelined_tokens(tg, prep, work)
    for r in range(N_CHUNK):
        o_ref[:, LANES * r:LANES * (r + 1)] = otile[pl.ds(r, tg, stride=N_CHUNK), :]


def _peer_v(base, shift, c, x8, tab):
    t = base.shape[0]
    tg = GATHER_BLOCK
    smem = pl.BlockSpec((tg, PEER_SEL), lambda i: (i, 0), memory_space=pltpu.SMEM)
    tokrow = pl.BlockSpec((tg, PEER_SEL), lambda i: (i, 0))
    tile = pl.BlockSpec((tg * SUBLANES, LANES), lambda i: (i, 0))
    return pl.pallas_call(
        _peer_v_kernel, grid=(t // tg,),
        in_specs=[smem, tokrow, tokrow, tile, _table_spec(tab.shape)],
        out_specs=pl.BlockSpec((tg, D_MODEL), lambda i: (i, 0)),
        out_shape=jax.ShapeDtypeStruct((t, D_MODEL), F32),
        scratch_shapes=[pltpu.VMEM((2 * LANES, LANES), jnp.int32), pltpu.VMEM((2 * LANES, LANES), F32),
                        pltpu.VMEM((tg * SUBLANES, LANES), F32)],
        compiler_params=_params(("parallel",)),
    )(base, shift, c, x8, tab)


SC_WORKERS = 32
SC_LANES = 16
SC_ROWS = 32
SC_TOKENS = 8192


def _peer_v_sc(eidx, c, x2, tab):
    ts = eidx.shape[0]
    per_w = ts // SC_WORKERS
    n_chunk = PEER_SEL // SC_ROWS
    mesh = plsc.VectorSubcoreMesh(core_axis_name="c", subcore_axis_name="s")

    def body(tab_hbm, idx_hbm, c_hbm, x_hbm, out_hbm, idx_v, c_v, acc_v, rows0, rows1, sem0, sem1):
        wid = lax.axis_index("s") * 2 + lax.axis_index("c")
        bufs, sems = (rows0, rows1), (sem0, sem1)

        def gather(ch):
            return pltpu.async_copy(tab_hbm.at[idx_v.at[pl.ds(ch * SC_ROWS, SC_ROWS)]], bufs[ch % 2], sems[ch % 2])

        @pl.loop(0, per_w)
        def _(i):
            t = wid * per_w + i
            pltpu.sync_copy(idx_hbm.at[t], idx_v)
            pltpu.sync_copy(c_hbm.at[t], c_v)
            pltpu.sync_copy(x_hbm.at[t], acc_v)
            copies = [gather(0), None]
            for ch in range(n_chunk):
                if ch + 1 < n_chunk:
                    copies[(ch + 1) % 2] = gather(ch + 1)
                copies[ch % 2].wait()
                rows = bufs[ch % 2]

                @pl.loop(0, SC_ROWS)
                def _(k):
                    ck = plsc.load_gather(c_v, [jnp.full((SC_LANES,), ch * SC_ROWS, jnp.int32) + k])
                    for j in range(D_MODEL // SC_LANES):
                        sl = pl.ds(SC_LANES * j, SC_LANES)
                        plsc.addupdate(acc_v.at[sl], ck * rows[k, sl])

            pltpu.sync_copy(acc_v, out_hbm.at[t])

    return pl.kernel(
        body, mesh=mesh, out_type=jax.ShapeDtypeStruct((ts, D_MODEL), F32),
        scratch_types=[pltpu.VMEM((PEER_SEL,), jnp.int32), pltpu.VMEM((PEER_SEL,), F32),
                       pltpu.VMEM((D_MODEL,), F32), pltpu.VMEM((SC_ROWS, D_MODEL), F32),
                       pltpu.VMEM((SC_ROWS, D_MODEL), F32), pltpu.SemaphoreType.DMA, pltpu.SemaphoreType.DMA],
        compiler_params=pltpu.CompilerParams(needs_layout_passes=False),
    )(tab, eidx, c, x2)


def _final_norm_kernel(x_ref, g_ref, o_ref):
    o_ref[...] = _rms(x_ref[...], g_ref[...])


def _final_norm(x2, g):
    t, d = x2.shape
    tm = min(PROJ_BLOCK, t)
    spec = pl.BlockSpec((tm, d), lambda i: (i, 0))
    return pl.pallas_call(
        _final_norm_kernel, grid=(t // tm,), in_specs=[spec, _full((1, d))], out_specs=spec,
        out_shape=jax.ShapeDtypeStruct((t, d), F32), compiler_params=_params(("parallel",)),
    )(x2, g.reshape(1, d).astype(F32))


def kernel(x, mem, norm_mix, w_in, rw_mu, rw_w0, rw_w2, rw_a0, rw_a2, rw_g2, rw_kk, rw_ka, rw_rk, rw_v0, rw_v1, rw_v2, rw_lnx_g, rw_lnx_b, s5_a_re, s5_a_im, s5_log_dt, s5_b_re, s5_b_im, s5_c_re, s5_c_im, s5_d, s5_glu_w, s5_glu_b, s5_out_g, w_out, norm_xa, norm_mem, xa_wq, xa_wk, xa_wv, xa_wo, norm_ffn, peer_wq, peer_keys, peer_u, peer_v, norm_final):
    bsz, seq, d = x.shape
    t = bsz * seq
    depth = w_in.shape[0]
    mem2 = mem.reshape(bsz * N_MEM, d)
    v_first = None
    for l in range(depth):
        x2 = x.reshape(t, d)
        w_rw = w_in[l][:, :RW_COLS].astype(BF16)
        w_s5 = w_in[l][:, RW_COLS:].astype(BF16)
        ws = [w_rw, w_s5]
        if l > 0:
            ws.append(_pad_rows(rw_v1[l - 1].T, LANES, 0).T.astype(BF16))
        outs = _norm_proj(x2, norm_mix[l], ws, [F32] * len(ws), PROJ_BLOCK)
        z_rw = outs[0].reshape(bsz, seq, RW_COLS)
        u_s5 = outs[1].reshape(bsz, seq, D_S5)
        hv = outs[2].reshape(bsz, seq, LANES) if l > 0 else None
        rw_prm = dict(mu=rw_mu[l], w0=rw_w0[l], w2=rw_w2[l], a0=rw_a0[l], a2=rw_a2[l], g2=rw_g2[l],
                      kk=rw_kk[l], ka=rw_ka[l], rk=rw_rk[l], lng=rw_lnx_g[l], lnb=rw_lnx_b[l])
        if l > 0:
            rw_prm.update(v0=rw_v0[l - 1], v2=rw_v2[l - 1])
        y_rw, v_first = _rwkv(z_rw, hv, v_first, rw_prm, bsz, seq)
        s5_prm = dict(a_re=s5_a_re[l], a_im=s5_a_im[l], log_dt=s5_log_dt[l], b_re=s5_b_re[l], b_im=s5_b_im[l],
                      c_re=s5_c_re[l], c_im=s5_c_im[l], d=s5_d[l], glu_w=s5_glu_w[l], glu_b=s5_glu_b[l],
                      out_g=s5_out_g[l])
        y_s5 = _s5(u_s5, s5_prm, bsz, seq)
        kv = _norm_proj(mem2, norm_mem[l], [xa_wk[l].astype(BF16), xa_wv[l].astype(BF16)], [BF16, BF16],
                        PROJ_BLOCK)
        kmem = kv[0].reshape(bsz, N_MEM, d)
        vmem = kv[1].reshape(bsz, N_MEM, d)
        x = _mix_xattn(x, y_rw, y_s5, w_out[l], norm_xa[l], xa_wq[l], kmem, vmem, xa_wo[l], bsz, seq)
        x2 = x.reshape(t, d)
        x8, h8, base, shift, gate = _peer_select(x2, norm_ffn[l], peer_wq[l], peer_keys[l])
        c = _peer_u(base, shift, h8, gate, _pack_table(peer_u[l]))
        t_tc = t - SC_TOKENS
        x_tc = _peer_v(base[:t_tc], shift[:t_tc], c[:t_tc], x8[:t_tc * SUBLANES], _pack_table(peer_v[l]))
        eidx_sc = (base[t_tc:] >> 2) + (shift[t_tc:] >> 4)
        x_sc = _peer_v_sc(eidx_sc, c[t_tc:], x2[t_tc:], peer_v[l])
        x = jnp.concatenate([x_tc, x_sc], axis=0).reshape(bsz, seq, d)
    return _final_norm(x.reshape(t, d), norm_final).reshape(bsz, seq, d)
```

```python
import functools
import math

import jax
import jax.numpy as jnp
from jax import lax
from jax.experimental import pallas as pl
from jax.experimental.pallas import tpu as pltpu
from jax.experimental.pallas import tpu_sc as plsc

F32 = jnp.float32
BF16 = jnp.bfloat16
U32 = jnp.uint32

LANES = 128
SUBLANES = 8
VMEM_LIMIT = 56 * 1024 * 1024

D_MODEL = 1024
D_RWKV = 512
RW_HEAD = 64
RW_COLS = 1792
D_S5 = 512
S5_GROUPS = 32
S5_CH = 16
S5_STATE = 64
S5_MODES = S5_GROUPS * S5_STATE
N_MEM = 256
XA_HEADS = 4
XA_HEAD = 256
PEER_HEADS = 8
PEER_NKEYS = 128
PEER_TOPK = 16
PEER_SEL = PEER_HEADS * PEER_TOPK
RMS_EPS = 1e-6
GN_EPS = 64e-5

RW_CHUNK = 128
S5_BLOCK = 256
PROJ_BLOCK = 512
XA_BLOCK = 256
SEL_BLOCK = 128
GATHER_BLOCK = 64


def _params(sem):
    return pltpu.CompilerParams(dimension_semantics=sem, vmem_limit_bytes=VMEM_LIMIT)


def _rms(x, g):
    ms = jnp.mean(x * x, axis=-1, keepdims=True)
    return x * lax.rsqrt(ms + RMS_EPS) * g


def _bdot(a, b):
    return jnp.dot(a.astype(BF16), b.astype(BF16), preferred_element_type=F32)


def _bdot_nt(a, b):
    return lax.dot_general(a.astype(BF16), b.astype(BF16), (((1,), (1,)), ((), ())),
                           preferred_element_type=F32)


def _sigmoid(x):
    return 1.0 / (1.0 + jnp.exp(-x))


def _softplus(x):
    return jnp.maximum(x, 0.0) + jnp.log(1.0 + jnp.exp(-jnp.abs(x)))


def _gelu(x):
    return 0.5 * x * (1.0 + jnp.tanh(math.sqrt(2.0 / math.pi) * (x + 0.044715 * (x * x * x))))


def _full(shape):
    n = len(shape)
    return pl.BlockSpec(shape, lambda *_: (0,) * n)


def _norm_proj_kernel(*refs, n_out):
    x_ref, g_ref = refs[0], refs[1]
    w_refs = refs[2:2 + n_out]
    o_refs = refs[2 + n_out:]
    h = _rms(x_ref[...], g_ref[...]).astype(BF16)
    for w_ref, o_ref in zip(w_refs, o_refs):
        o_ref[...] = jnp.dot(h, w_ref[...], preferred_element_type=F32).astype(o_ref.dtype)


def _norm_proj(x2, g, ws, out_dtypes, block):
    t, d = x2.shape
    block = min(block, t)
    in_specs = [pl.BlockSpec((block, d), lambda i: (i, 0)), _full((1, d))]
    in_specs += [_full(w.shape) for w in ws]
    out_specs = [pl.BlockSpec((block, w.shape[1]), lambda i: (i, 0)) for w in ws]
    out_shape = [jax.ShapeDtypeStruct((t, w.shape[1]), dt) for w, dt in zip(ws, out_dtypes)]
    return pl.pallas_call(
        functools.partial(_norm_proj_kernel, n_out=len(ws)),
        grid=(t // block,), in_specs=in_specs, out_specs=out_specs, out_shape=out_shape,
        compiler_params=_params(("parallel",)),
    )(x2, g.reshape(1, d), *ws)


def _seg_sum(x, mseg):
    hi = x.astype(BF16)
    lo = (x - hi.astype(F32)).astype(BF16)
    return (jnp.dot(hi, mseg, preferred_element_type=F32)
            + jnp.dot(lo, mseg, preferred_element_type=F32))


def _col_bcast(row):
    return jnp.broadcast_to(row, (LANES, LANES)).T


def _rwkv_kernel(*refs, first_layer):
    if first_layer:
        (z_ref, mu_ref, w0_ref, w2_ref, a0_ref, a2_ref, g2_ref, kk_ref, ka_ref, rk_ref,
         lng_ref, lnb_ref, mseg_ref, y_ref, vf_out_ref, zprev, hst) = refs
    else:
        (z_ref, hv_ref, vf_ref, v0_ref, v2_ref, mu_ref, w0_ref, w2_ref, a0_ref, a2_ref, g2_ref,
         kk_ref, ka_ref, rk_ref, lng_ref, lnb_ref, mseg_ref, y_ref, zprev, hst) = refs
    L = RW_CHUNK

    @pl.when(pl.program_id(1) == 0)
    def _():
        zprev[...] = jnp.zeros_like(zprev)
        hst[...] = jnp.zeros_like(hst)

    z = z_ref[0]
    row = lax.broadcasted_iota(jnp.int32, (L, 1), 0)
    zs = jnp.where(row == 0, zprev[...], pltpu.roll(z, 1, 0))
    zprev[...] = z[L - 1:L, :]
    z = z + (zs - z) * mu_ref[...]
    r = z[:, 0:512]
    k = z[:, 512:1024]
    v = z[:, 1024:1536]
    wa = z[:, 1536:1664]
    gd = z[:, 1664:1792]
    mseg = mseg_ref[...]

    wlin = w0_ref[...] + _bdot(jnp.tanh(wa), w2_ref[...])
    lw = -jnp.exp(-_softplus(-wlin) - 0.5)
    a = _sigmoid(a0_ref[...] + _bdot(wa, a2_ref[...]))
    g = _bdot(_sigmoid(gd), g2_ref[...])
    if first_layer:
        vf_out_ref[0] = v
    else:
        v = v + (vf_ref[0] - v) * _sigmoid(v0_ref[...] + _bdot(hv_ref[0], v2_ref[...]))
    kk = k * kk_ref[...]
    kk = kk / jnp.maximum(jnp.sqrt(_seg_sum(kk * kk, mseg)), 1e-12)
    k2 = k * (1.0 + (a - 1.0) * ka_ref[...])
    av = -kk
    bv = kk * a

    ti = lax.broadcasted_iota(jnp.int32, (L, L), 0)
    si = lax.broadcasted_iota(jnp.int32, (L, L), 1)
    tril = (ti >= si).astype(F32)
    cum = jnp.dot(tril, lw, preferred_element_type=F32, precision=lax.Precision.HIGHEST)
    mid = cum[L // 2 - 1:L // 2, :]
    cm = cum - mid
    ecum = jnp.exp(cm)
    einv = jnp.exp(-cm)
    rt = r * ecum
    kt = k2 * einv
    bt = bv * einv
    at = av * jnp.exp(cm - lw)
    p_end = jnp.exp(cum[L - 1:L, :])
    e_end = ecum[L - 1:L, :]
    e_mid = jnp.exp(mid)

    lane = lax.broadcasted_iota(jnp.int32, (1, LANES), 1)
    m0 = (lane < RW_HEAD).astype(F32)
    m1 = 1.0 - m0
    strict = ti > si
    incl = ti >= si
    bi = lax.broadcasted_iota(jnp.int32, (LANES, LANES), 0) < RW_HEAD
    bj = lax.broadcasted_iota(jnp.int32, (LANES, LANES), 1) < RW_HEAD
    bdmask = (bi == bj).astype(F32)
    zeros_ll = jnp.zeros((L, L), F32)

    ys = []
    for p in range(D_RWKV // LANES):
        sl = slice(LANES * p, LANES * (p + 1))
        A, B, K, R, V = at[:, sl], bt[:, sl], kt[:, sl], rt[:, sl], v[:, sl]
        btkt = jnp.concatenate([B.T, K.T], axis=1)
        h0 = hst[p]
        h0m = h0 * _col_bcast(e_mid[:, sl])
        sc = _bdot(jnp.concatenate([A * m0, A * m1, R * m0, R * m1], axis=0), btkt)
        aab = [jnp.where(strict, sc[e * L:(e + 1) * L, 0:L], 0.0) for e in range(2)]
        aak = [jnp.where(strict, sc[e * L:(e + 1) * L, L:2 * L], 0.0) for e in range(2)]
        arb = [jnp.where(incl, sc[(2 + e) * L:(3 + e) * L, 0:L], 0.0) for e in range(2)]
        ark = [jnp.where(incl, sc[(2 + e) * L:(3 + e) * L, L:2 * L], 0.0) for e in range(2)]
        arh = _bdot(jnp.concatenate([A, R], axis=0), h0m)
        v01 = jnp.concatenate([V * m0, V * m1], axis=0)
        x = arh[0:L] + _bdot(jnp.concatenate(aak, axis=1), v01)
        pm = jnp.concatenate(aab, axis=1)
        n_fac = int(math.log2(L))
        for it in range(n_fac):
            x = x + _bdot(pm, jnp.concatenate([x * m0, x * m1], axis=0))
            if it + 1 < n_fac:
                pd = jnp.concatenate(
                    [jnp.concatenate([pm[:, 0:L], zeros_ll], axis=1),
                     jnp.concatenate([zeros_ll, pm[:, L:2 * L]], axis=1)], axis=0)
                pm = _bdot(pm, pd)
        u = x
        yp = arh[L:2 * L] + _bdot(jnp.concatenate(arb + ark, axis=1),
                                  jnp.concatenate([u * m0, u * m1, v01], axis=0))
        upd = _bdot(btkt, jnp.concatenate([u, V], axis=0))
        hst[p] = (h0 * _col_bcast(p_end[:, sl]) + upd * _col_bcast(e_end[:, sl])) * bdmask
        ys.append(yp)
    y = jnp.concatenate(ys, axis=1)

    mean = _seg_sum(y, mseg) * (1.0 / RW_HEAD)
    d = y - mean
    var = _seg_sum(d * d, mseg) * (1.0 / RW_HEAD)
    yn = d * lax.rsqrt(var + GN_EPS) * lng_ref[...] + lnb_ref[...]
    bonus = _seg_sum(r * k2 * rk_ref[...], mseg) * v
    y_ref[0] = (yn + bonus) * g


def _pad_rows(w, rows, offset):
    out = jnp.zeros((rows, w.shape[1]), w.dtype)
    return out.at[offset:offset + w.shape[0]].set(w)


def _rwkv(z_rw, hv, v_first, prm, bsz, seq):
    L = RW_CHUNK
    first = v_first is None
    row = lambda a: a.reshape(1, -1).astype(F32)
    hid = jnp.arange(D_RWKV) // RW_HEAD
    mseg = (hid[:, None] == hid[None, :]).astype(BF16)
    w2p = _pad_rows(prm['w2'], LANES, 0).astype(BF16)
    a2p = _pad_rows(prm['a2'], LANES, 64).astype(BF16)
    common = [row(prm['mu']), row(prm['w0']), w2p, row(prm['a0']), a2p, prm['g2'].astype(BF16),
              row(prm['kk']), row(prm['ka']), row(prm['rk']), row(prm['lng']), row(prm['lnb']), mseg]
    tok = lambda w: pl.BlockSpec((1, L, w), lambda b, t: (b, t, 0))
    common_specs = [_full(c.shape) for c in common]
    y_shape = jax.ShapeDtypeStruct((bsz, seq, D_RWKV), F32)
    scratch = [pltpu.VMEM((1, RW_COLS), F32), pltpu.VMEM((D_RWKV // LANES, LANES, LANES), F32)]
    if first:
        args = [z_rw] + common
        in_specs = [tok(RW_COLS)] + common_specs
        out_shape = [y_shape, y_shape]
        out_specs = [tok(D_RWKV), tok(D_RWKV)]
    else:
        v2p = _pad_rows(prm['v2'], LANES, 0).astype(BF16)
        extra = [row(prm['v0']), v2p]
        args = [z_rw, hv, v_first] + extra + common
        in_specs = [tok(RW_COLS), tok(LANES), tok(D_RWKV)] + [_full(c.shape) for c in extra] + common_specs
        out_shape = [y_shape]
        out_specs = [tok(D_RWKV)]
    outs = pl.pallas_call(
        functools.partial(_rwkv_kernel, first_layer=first),
        grid=(bsz, seq // L), in_specs=in_specs, out_specs=out_specs, out_shape=out_shape,
        scratch_shapes=scratch, compiler_params=_params(("parallel", "arbitrary")),
    )(*args)
    return (outs[0], outs[1]) if first else (outs[0], v_first)


def _s5_kernel(u_ref, wb_ref, wc_ref, lpr_ref, lpi_ref, d_ref, gw_ref, gb_ref, og_ref, o_ref,
               car_re, car_im, xre, xim):
    tb = u_ref.shape[1]

    @pl.when(pl.program_id(1) == 0)
    def _():
        car_re[...] = jnp.zeros_like(car_re)
        car_im[...] = jnp.zeros_like(car_im)

    u = u_ref[0]
    bu = _bdot(u, wb_ref[...])
    xre[...] = bu[:, 0:S5_MODES]
    xim[...] = bu[:, S5_MODES:2 * S5_MODES]
    row = lax.broadcasted_iota(jnp.int32, (SUBLANES, 1), 0)

    def tile(i, carry):
        cr, ci = carry
        rs = pl.ds(pl.multiple_of(i * SUBLANES, SUBLANES), SUBLANES)
        br, bi = xre[rs, :], xim[rs, :]
        for dist in (1, 2, 4):
            keep = row >= dist
            sr = jnp.where(keep, pltpu.roll(br, dist, 0), 0.0)
            si = jnp.where(keep, pltpu.roll(bi, dist, 0), 0.0)
            lr = lpr_ref[dist - 1:dist, :]
            li = lpi_ref[dist - 1:dist, :]
            br, bi = br + lr * sr - li * si, bi + lr * si + li * sr
        pr, pi = lpr_ref[...], lpi_ref[...]
        xr = br + pr * cr - pi * ci
        xi = bi + pr * ci + pi * cr
        xre[rs, :] = xr
        xim[rs, :] = xi
        return xr[SUBLANES - 1:SUBLANES, :], xi[SUBLANES - 1:SUBLANES, :]

    cr, ci = lax.fori_loop(0, tb // SUBLANES, tile, (car_re[...], car_im[...]))
    car_re[...] = cr
    car_im[...] = ci
    wc = wc_ref[...]
    y = _bdot(xre[...], wc[0:S5_MODES]) + _bdot(xim[...], wc[S5_MODES:2 * S5_MODES])
    y = _gelu(y + d_ref[...] * u)
    y = y * _sigmoid(_bdot(y, gw_ref[...]) + gb_ref[...])
    o_ref[0] = _rms(y, og_ref[...])


def _s5_weights(a_re, a_im, log_dt, b_re, b_im, c_re, c_im):
    lam_re = jnp.minimum(a_re.astype(F32), -1e-4)
    lam_im = a_im.astype(F32)
    dt = jnp.exp(log_dt.astype(F32))[:, None]
    mag = jnp.exp(lam_re * dt)
    lb_re = mag * jnp.cos(lam_im * dt)
    lb_im = mag * jnp.sin(lam_im * dt)
    den = lam_re * lam_re + lam_im * lam_im
    c1_re = ((lb_re - 1.0) * lam_re + lb_im * lam_im) / den
    c1_im = (lb_im * lam_re - (lb_re - 1.0) * lam_im) / den
    br, bi = b_re.astype(F32), b_im.astype(F32)
    bb_re = c1_re[..., None] * br - c1_im[..., None] * bi
    bb_im = c1_re[..., None] * bi + c1_im[..., None] * br
    eye = jnp.eye(S5_GROUPS, dtype=F32)
    wb_re = jnp.einsum('gpc,gh->gchp', bb_re, eye).reshape(D_S5, S5_MODES)
    wb_im = jnp.einsum('gpc,gh->gchp', bb_im, eye).reshape(D_S5, S5_MODES)
    wb = jnp.concatenate([wb_re, wb_im], axis=1).astype(BF16)
    wc_re = jnp.einsum('gcp,gh->gphc', c_re.astype(F32), eye).reshape(S5_MODES, D_S5)
    wc_im = jnp.einsum('gcp,gh->gphc', c_im.astype(F32), eye).reshape(S5_MODES, D_S5)
    wc = jnp.concatenate([wc_re, -wc_im], axis=0).astype(BF16)
    pr, pi = [lb_re], [lb_im]
    for _ in range(SUBLANES - 1):
        pr, pi = pr + [pr[-1] * lb_re - pi[-1] * lb_im], pi + [pr[-1] * lb_im + pi[-1] * lb_re]
    lp_re = jnp.stack(pr).reshape(SUBLANES, S5_MODES)
    lp_im = jnp.stack(pi).reshape(SUBLANES, S5_MODES)
    return wb, wc, lp_re, lp_im


def _s5(u, prm, bsz, seq):
    tb = min(S5_BLOCK, seq)
    wb, wc, lp_re, lp_im = _s5_weights(prm['a_re'], prm['a_im'], prm['log_dt'], prm['b_re'], prm['b_im'],
                                       prm['c_re'], prm['c_im'])
    row = lambda a: a.reshape(1, -1).astype(F32)
    consts = [wb, wc, lp_re, lp_im, row(prm['d']), prm['glu_w'].astype(BF16), row(prm['glu_b']),
              row(prm['out_g'])]
    tok = pl.BlockSpec((1, tb, D_S5), lambda b, t: (b, t, 0))
    return pl.pallas_call(
        _s5_kernel, grid=(bsz, seq // tb),
        in_specs=[tok] + [_full(c.shape) for c in consts], out_specs=tok,
        out_shape=jax.ShapeDtypeStruct((bsz, seq, D_S5), F32),
        scratch_shapes=[pltpu.VMEM((1, S5_MODES), F32), pltpu.VMEM((1, S5_MODES), F32),
                        pltpu.VMEM((tb, S5_MODES), F32), pltpu.VMEM((tb, S5_MODES), F32)],
        compiler_params=_params(("parallel", "arbitrary")),
    )(u, *consts)


def _mix_xattn_kernel(x_ref, yr_ref, ys_ref, wo1_ref, wo2_ref, g_ref, wq_ref, k_ref, v_ref, wo_ref, o_ref):
    x1 = x_ref[0] + _bdot(yr_ref[0], wo1_ref[...]) + _bdot(ys_ref[0], wo2_ref[...])
    h = _rms(x1, g_ref[...])
    q = _bdot(h, wq_ref[...])
    km, vm = k_ref[0], v_ref[0]
    outs = []
    for hd in range(XA_HEADS):
        sl = slice(XA_HEAD * hd, XA_HEAD * (hd + 1))
        s = _bdot_nt(q[:, sl], km[:, sl]) * (XA_HEAD ** -0.5)
        s = s - jnp.max(s, axis=-1, keepdims=True)
        e = jnp.exp(s)
        p = e / jnp.sum(e, axis=-1, keepdims=True)
        outs.append(_bdot(p, vm[:, sl]))
    o = jnp.concatenate(outs, axis=1)
    o_ref[0] = x1 + _bdot(o, wo_ref[...])


def _mix_xattn(x, y_rw, y_s5, w_out, g, wq, kmem, vmem, wo, bsz, seq):
    tm = min(XA_BLOCK, seq)
    consts_a = [w_out[:D_RWKV].astype(BF16), w_out[D_RWKV:].astype(BF16), g.reshape(1, -1).astype(F32),
                wq.astype(BF16)]
    tok = lambda w: pl.BlockSpec((1, tm, w), lambda b, t: (b, t, 0))
    mem = pl.BlockSpec((1, N_MEM, D_MODEL), lambda b, t: (b, 0, 0))
    wo_b = wo.astype(BF16)
    return pl.pallas_call(
        _mix_xattn_kernel, grid=(bsz, seq // tm),
        in_specs=[tok(D_MODEL), tok(D_RWKV), tok(D_S5)] + [_full(c.shape) for c in consts_a]
                 + [mem, mem, _full(wo_b.shape)],
        out_specs=tok(D_MODEL), out_shape=jax.ShapeDtypeStruct((bsz, seq, D_MODEL), F32),
        compiler_params=_params(("parallel", "parallel")),
    )(x, y_rw, y_s5, *consts_a, kmem, vmem, wo_b)


def _top_rows(work, order, aux, val_ref, idx_ref):
    for it in range(PEER_TOPK):
        m = jnp.max(work, axis=0, keepdims=True)
        pos = jnp.min(jnp.where(work == m, order, jnp.inf), axis=0, keepdims=True)
        hit = order == pos
        val_ref[it:it + 1, :] = m
        if aux is None:
            idx_ref[it:it + 1, :] = pos
        else:
            idx_ref[it:it + 1, :] = jnp.sum(jnp.where(hit, aux, 0.0), axis=0, keepdims=True)
        work = jnp.where(hit, -jnp.inf, work)


_CAND_ROW_BLOCKS = [(0, PEER_TOPK), (1, SUBLANES), (2, SUBLANES), (3, SUBLANES)]
_CAND_COL_BLOCKS = [(0, PEER_TOPK, 4, 15), (1, SUBLANES, 4, 7), (2, SUBLANES, 4, 4)]
N_CAND = sum(n for _, n in _CAND_ROW_BLOCKS) + sum(n for _, n, _, _ in _CAND_COL_BLOCKS)


def _cand_consts(tm):
    flat, neg = [], []
    for a, nb in _CAND_ROW_BLOCKS:
        flat += [a * PEER_TOPK + b for b in range(nb)]
        neg += [0.0] * nb
    for b, na, lo, hi in _CAND_COL_BLOCKS:
        flat += [a * PEER_TOPK + b for a in range(na)]
        neg += [0.0 if lo <= a <= hi else -float('inf') for a in range(na)]
    flat = [f if n == 0.0 else 1000.0 + i for i, (f, n) in enumerate(zip(flat, neg))]
    col = lambda v: jnp.broadcast_to(jnp.asarray(v, F32)[:, None], (N_CAND, tm))
    return col(flat), col(neg)


def _cand_rows(row_vals, col_vals, combine):
    blocks = [combine(row_vals[a:a + 1, :], col_vals[0:nb, :]) for a, nb in _CAND_ROW_BLOCKS]
    blocks += [combine(row_vals[0:na, :], col_vals[b:b + 1, :]) for b, na, _, _ in _CAND_COL_BLOCKS]
    return jnp.concatenate(blocks, axis=0)


SEL_HEADS_PER_STEP = 2


def _peer_select_kernel(x_ref, g_ref, wq_ref, keys_ref, cflat_ref, cneg_ref, x8_ref, h8_ref, base_ref, shift_ref,
                        gate_ref, q3, idx_t, gate_t, *lists):
    tm = x_ref.shape[0]
    n = SEL_HEADS_PER_STEP
    s1, i1, s2, i2, top, eid = (lists[j * n:(j + 1) * n] for j in range(6))
    x = x_ref[...]
    h = _rms(x, g_ref[...])
    for r in range(D_MODEL // LANES):
        rows = pl.ds(r, tm, stride=D_MODEL // LANES)
        h8_ref[rows, :] = h[:, LANES * r:LANES * (r + 1)]
        x8_ref[rows, :] = x[:, LANES * r:LANES * (r + 1)]
    q = _bdot(h, wq_ref[...])
    for j in range(2 * PEER_HEADS):
        q3[j] = q[:, LANES * j:LANES * (j + 1)]
    iota_k = lax.broadcasted_iota(jnp.int32, (PEER_NKEYS, tm), 0).astype(F32)

    def heads(step, _):
        for u in range(SEL_HEADS_PER_STEP):
            hd = step * SEL_HEADS_PER_STEP + u
            sc1 = _bdot_nt(keys_ref[2 * hd], q3[2 * hd])
            sc2 = _bdot_nt(keys_ref[2 * hd + 1], q3[2 * hd + 1])
            _top_rows(sc1, iota_k, None, s1[u], i1[u])
            _top_rows(sc2, iota_k, None, s2[u], i2[u])
            cand = _cand_rows(s1[u][...], s2[u][...], lambda x, y: x + y) + cneg_ref[...]
            cidx = _cand_rows(i1[u][...], i2[u][...], lambda x, y: x * float(PEER_NKEYS) + y)
            _top_rows(cand, cflat_ref[...], cidx, top[u], eid[u])
            tv = top[u][...]
            e = jnp.exp(tv - jnp.max(tv, axis=0, keepdims=True))
            rs = pl.ds(pl.multiple_of(hd * PEER_TOPK, PEER_TOPK), PEER_TOPK)
            idx_t[rs, :] = eid[u][...]
            gate_t[rs, :] = e / jnp.sum(e, axis=0, keepdims=True)
        return 0

    lax.fori_loop(0, PEER_HEADS // SEL_HEADS_PER_STEP, heads, 0)
    e_t = idx_t[...].T
    pair = jnp.floor(e_t * 0.5)
    base_ref[...] = (pair * float(SUBLANES)).astype(jnp.int32)
    shift_ref[...] = ((e_t - 2.0 * pair) * 16.0).astype(jnp.int32)
    gate_ref[...] = gate_t[...].T


def _peer_select(x2, g, wq, keys):
    t = x2.shape[0]
    tm = SEL_BLOCK
    keys_b = keys.reshape(2 * PEER_HEADS, PEER_NKEYS, LANES).astype(BF16)
    wq_b = wq.astype(BF16)
    cflat, cneg = _cand_consts(tm)
    tokspec = lambda w: pl.BlockSpec((tm, w), lambda i: (i, 0))
    vm = lambda r: pltpu.VMEM((r, tm), F32)
    return pl.pallas_call(
        _peer_select_kernel, grid=(t // tm,),
        in_specs=[tokspec(D_MODEL), _full((1, D_MODEL)), _full(wq_b.shape), _full(keys_b.shape),
                  _full(cflat.shape), _full(cneg.shape)],
        out_specs=[pl.BlockSpec((tm * SUBLANES, LANES), lambda i: (i, 0))] * 2
                  + [tokspec(PEER_SEL), tokspec(PEER_SEL), tokspec(PEER_SEL)],
        out_shape=[jax.ShapeDtypeStruct((t * SUBLANES, LANES), F32)] * 2 + [
                   jax.ShapeDtypeStruct((t, PEER_SEL), jnp.int32),
                   jax.ShapeDtypeStruct((t, PEER_SEL), jnp.int32), jax.ShapeDtypeStruct((t, PEER_SEL), F32)],
        scratch_shapes=[pltpu.VMEM((2 * PEER_HEADS, tm, LANES), F32), vm(PEER_SEL), vm(PEER_SEL)]
                       + [vm(PEER_TOPK) for _ in range(6 * SEL_HEADS_PER_STEP)],
        compiler_params=_params(("parallel",)),
    )(x2, g.reshape(1, -1).astype(F32), wq_b, keys_b, cflat, cneg)


def _pack_table(tab):
    n, d = tab.shape
    bits = lax.bitcast_convert_type(tab.astype(BF16), jnp.uint16).astype(U32)
    bits = bits.reshape(n // 2, 2, d)
    packed = (bits[:, 0, :] << 16) | bits[:, 1, :]
    return lax.bitcast_convert_type(packed, jnp.int32).reshape(n // 2 * SUBLANES, LANES)


def _splat_into(src_ref, t, dst_ref, slot):
    tile = jnp.broadcast_to(src_ref[pl.ds(t, 1), :], (LANES, LANES)).T
    dst_ref[LANES * slot:LANES * (slot + 1), :] = tile


def _bcast_row(ref, row):
    return jnp.broadcast_to(ref[row:row + 1, :], (SUBLANES, LANES))


def _expert_tile(tab_ref, base, shift_splat, row):
    w = tab_ref[pl.ds(pl.multiple_of(base, SUBLANES), SUBLANES), :]
    return lax.bitcast_convert_type((w << _bcast_row(shift_splat, row)) & jnp.int32(-65536), F32)


def _token_rows(t):
    return pl.ds(pl.multiple_of(t * SUBLANES, SUBLANES), SUBLANES)


def _pipelined_tokens(tg, prep, work):
    prep(0, 0)

    def body(i, _):
        t0 = 2 * i
        prep(t0 + 1, 1)
        work(t0, 0)
        prep(jnp.minimum(t0 + 2, tg - 1), 0)
        work(t0 + 1, 1)
        return 0

    lax.fori_loop(0, tg // 2, body, 0)


def _table_spec(shape):
    return pl.BlockSpec(shape, lambda i: (0, 0), pipeline_mode=pl.Buffered(1))


N_CHUNK = D_MODEL // LANES
CHUNK_STRIDE = PEER_SEL + SUBLANES


def _peer_u_kernel(base_ref, shift_ref, h_ref, gate_ref, tab_ref, eye_ref, c_ref, shift_splat, *planes):
    tg = gate_ref.shape[0]
    ones = jnp.ones((LANES, LANES), BF16)

    half = N_CHUNK // 2

    def prep(t, slot):
        _splat_into(shift_ref, t, shift_splat, slot)

    def gather(t, slot):
        ht = h_ref[_token_rows(t), :]
        for kx in range(PEER_SEL):
            prod = _expert_tile(tab_ref, base_ref[t, kx], shift_splat, LANES * slot + kx) * ht
            fold = prod + pltpu.roll(prod, half, 0)
            planes[slot][pl.ds(kx, half, stride=CHUNK_STRIDE), :] = fold[0:half, :]

    def finish(t, slot):
        plane = planes[slot]
        acc = plane[0:PEER_SEL, :]
        for r in range(1, half):
            acc = acc + plane[CHUNK_STRIDE * r:CHUNK_STRIDE * r + PEER_SEL, :]
        hi = acc.astype(BF16)
        lo = (acc - hi.astype(F32)).astype(BF16)
        tot = jnp.dot(hi, ones, preferred_element_type=F32) + jnp.dot(lo, ones, preferred_element_type=F32)
        score = jnp.sum(tot * eye_ref[...], axis=0, keepdims=True)
        c_ref[pl.ds(t, 1), :] = gate_ref[pl.ds(t, 1), :] * _gelu(score)

    planes[1][...] = jnp.zeros_like(planes[1])
    prep(0, 0)

    def body(i, _):
        t0 = 2 * i
        prep(t0 + 1, 1)
        gather(t0, 0)
        finish(jnp.maximum(t0 - 1, 0), 1)
        prep(jnp.minimum(t0 + 2, tg - 1), 0)
        gather(t0 + 1, 1)
        finish(t0, 0)
        return 0

    lax.fori_loop(0, tg // 2, body, 0)
    finish(tg - 1, 1)


def _peer_u(base, shift, h8, gate, tab):
    t = base.shape[0]
    tg = GATHER_BLOCK
    eye = jnp.eye(LANES, dtype=F32)
    tokrow = pl.BlockSpec((tg, PEER_SEL), lambda i: (i, 0))
    return pl.pallas_call(
        _peer_u_kernel, grid=(t // tg,),
        in_specs=[pl.BlockSpec((tg, PEER_SEL), lambda i: (i, 0), memory_space=pltpu.SMEM), tokrow,
                  pl.BlockSpec((tg * SUBLANES, LANES), lambda i: (i, 0)), tokrow,
                  _table_spec(tab.shape), _full(eye.shape)],
        out_specs=tokrow, out_shape=jax.ShapeDtypeStruct((t, PEER_SEL), F32),
        scratch_shapes=[pltpu.VMEM((2 * LANES, LANES), jnp.int32)]
                       + [pltpu.VMEM((N_CHUNK // 2 * CHUNK_STRIDE, LANES), F32)] * 2,
        compiler_params=_params(("parallel",)),
    )(base, shift, h8, gate, tab, eye)


def _peer_v_kernel(base_ref, shift_ref, c_ref, x_ref, tab_ref, o_ref, shift_splat, c_splat, otile):
    tg = c_ref.shape[0]
    n_acc = 4

    def prep(t, slot):
        _splat_into(shift_ref, t, shift_splat, slot)
        _splat_into(c_ref, t, c_splat, slot)

    def work(t, slot):
        accs = [jnp.zeros((SUBLANES, LANES), F32) for _ in range(n_acc)]
        for kx in range(PEER_SEL):
            row = LANES * slot + kx
            accs[kx % n_acc] = accs[kx % n_acc] + (_bcast_row(c_splat, row)
                                                   * _expert_tile(tab_ref, base_ref[t, kx], shift_splat, row))
        rs = _token_rows(t)
        otile[rs, :] = x_ref[rs, :] + ((accs[0] + accs[1]) + (accs[2] + accs[3]))

    _pipelined_tokens(tg, prep, work)
    for r in range(N_CHUNK):
        o_ref[:, LANES * r:LANES * (r + 1)] = otile[pl.ds(r, tg, stride=N_CHUNK), :]


def _peer_v(base, shift, c, x8, tab):
    t = base.shape[0]
    tg = GATHER_BLOCK
    smem = pl.BlockSpec((tg, PEER_SEL), lambda i: (i, 0), memory_space=pltpu.SMEM)
    tokrow = pl.BlockSpec((tg, PEER_SEL), lambda i: (i, 0))
    tile = pl.BlockSpec((tg * SUBLANES, LANES), lambda i: (i, 0))
    return pl.pallas_call(
        _peer_v_kernel, grid=(t // tg,),
        in_specs=[smem, tokrow, tokrow, tile, _table_spec(tab.shape)],
        out_specs=pl.BlockSpec((tg, D_MODEL), lambda i: (i, 0)),
        out_shape=jax.ShapeDtypeStruct((t, D_MODEL), F32),
        scratch_shapes=[pltpu.VMEM((2 * LANES, LANES), jnp.int32), pltpu.VMEM((2 * LANES, LANES), F32),
                        pltpu.VMEM((tg * SUBLANES, LANES), F32)],
        compiler_params=_params(("parallel",)),
    )(base, shift, c, x8, tab)


SC_WORKERS = 32
SC_LANES = 16
SC_ROWS = 32
SC_TOKENS = 8192
SC_GROUP = 8


def _peer_v_sc(eidx, c, x2, tab):
    ts = eidx.shape[0]
    per_w = ts // SC_WORKERS
    n_chunk = PEER_SEL // SC_ROWS
    mesh = plsc.VectorSubcoreMesh(core_axis_name="c", subcore_axis_name="s")

    grp = SC_GROUP
    sel = grp * PEER_SEL
    n_pairs = grp * n_chunk // 2

    def body(tab_hbm, idx_hbm, c_hbm, x_hbm, out_hbm, idx_v, c_v, acc_v, rows0, rows1, sem0, sem1):
        wid = lax.axis_index("s") * 2 + lax.axis_index("c")
        bufs, sems = (rows0, rows1), (sem0, sem1)

        def gather(chunk, slot):
            start = pl.multiple_of(chunk * SC_ROWS, SC_ROWS)
            return pltpu.make_async_copy(tab_hbm.at[idx_v.at[pl.ds(start, SC_ROWS)]], bufs[slot], sems[slot])

        def accumulate(chunk, slot):
            tok = chunk // n_chunk

            @pl.loop(0, SC_ROWS)
            def _(k):
                ck = plsc.load_gather(c_v, [jnp.full((SC_LANES,), chunk * SC_ROWS, jnp.int32) + k])
                for r in range(N_CHUNK):
                    for j in range(LANES // SC_LANES):
                        sl = pl.ds(SC_LANES * j, SC_LANES)
                        plsc.addupdate(acc_v.at[tok, r, sl], ck * bufs[slot][k, r, sl])

        @pl.loop(0, per_w // grp)
        def _(g):
            t0 = wid * per_w + g * grp
            flat = pl.ds(pl.multiple_of(t0 * PEER_SEL, sel), sel)
            pltpu.sync_copy(idx_hbm.at[flat], idx_v)
            pltpu.sync_copy(c_hbm.at[flat], c_v)
            pltpu.sync_copy(x_hbm.at[pl.ds(t0, grp)], acc_v)
            gather(0, 0).start()

            @pl.loop(0, n_pairs)
            def _(p):
                c0 = 2 * p
                gather(c0 + 1, 1).start()
                gather(c0, 0).wait()
                accumulate(c0, 0)
                gather(jnp.minimum(c0 + 2, 2 * n_pairs - 1), 0).start()
                gather(c0 + 1, 1).wait()
                accumulate(c0 + 1, 1)

            gather(0, 0).wait()
            pltpu.sync_copy(acc_v, out_hbm.at[pl.ds(t0, grp)])

    return pl.kernel(
        body, mesh=mesh, out_type=jax.ShapeDtypeStruct((ts, N_CHUNK, LANES), F32),
        scratch_types=[pltpu.VMEM((sel,), jnp.int32), pltpu.VMEM((sel,), F32),
                       pltpu.VMEM((grp, N_CHUNK, LANES), F32), pltpu.VMEM((SC_ROWS, N_CHUNK, LANES), F32),
                       pltpu.VMEM((SC_ROWS, N_CHUNK, LANES), F32), pltpu.SemaphoreType.DMA,
                       pltpu.SemaphoreType.DMA],
        compiler_params=pltpu.CompilerParams(needs_layout_passes=False),
    )(tab.reshape(-1, N_CHUNK, LANES), eidx.reshape(-1), c.reshape(-1), x2.reshape(ts, N_CHUNK, LANES)
      ).reshape(ts, D_MODEL)


def _final_norm_kernel(x_ref, g_ref, o_ref):
    o_ref[...] = _rms(x_ref[...], g_ref[...])


def _final_norm(x2, g):
    t, d = x2.shape
    tm = min(PROJ_BLOCK, t)
    spec = pl.BlockSpec((tm, d), lambda i: (i, 0))
    return pl.pallas_call(
        _final_norm_kernel, grid=(t // tm,), in_specs=[spec, _full((1, d))], out_specs=spec,
        out_shape=jax.ShapeDtypeStruct((t, d), F32), compiler_params=_params(("parallel",)),
    )(x2, g.reshape(1, d).astype(F32))


def kernel(x, mem, norm_mix, w_in, rw_mu, rw_w0, rw_w2, rw_a0, rw_a2, rw_g2, rw_kk, rw_ka, rw_rk, rw_v0, rw_v1, rw_v2, rw_lnx_g, rw_lnx_b, s5_a_re, s5_a_im, s5_log_dt, s5_b_re, s5_b_im, s5_c_re, s5_c_im, s5_d, s5_glu_w, s5_glu_b, s5_out_g, w_out, norm_xa, norm_mem, xa_wq, xa_wk, xa_wv, xa_wo, norm_ffn, peer_wq, peer_keys, peer_u, peer_v, norm_final):
    bsz, seq, d = x.shape
    t = bsz * seq
    depth = w_in.shape[0]
    mem2 = mem.reshape(bsz * N_MEM, d)
    v_first = None
    for l in range(depth):
        x2 = x.reshape(t, d)
        w_rw = w_in[l][:, :RW_COLS].astype(BF16)
        w_s5 = w_in[l][:, RW_COLS:].astype(BF16)
        ws = [w_rw, w_s5]
        if l > 0:
            ws.append(_pad_rows(rw_v1[l - 1].T, LANES, 0).T.astype(BF16))
        outs = _norm_proj(x2, norm_mix[l], ws, [F32] * len(ws), PROJ_BLOCK)
        z_rw = outs[0].reshape(bsz, seq, RW_COLS)
        u_s5 = outs[1].reshape(bsz, seq, D_S5)
        hv = outs[2].reshape(bsz, seq, LANES) if l > 0 else None
        rw_prm = dict(mu=rw_mu[l], w0=rw_w0[l], w2=rw_w2[l], a0=rw_a0[l], a2=rw_a2[l], g2=rw_g2[l],
                      kk=rw_kk[l], ka=rw_ka[l], rk=rw_rk[l], lng=rw_lnx_g[l], lnb=rw_lnx_b[l])
        if l > 0:
            rw_prm.update(v0=rw_v0[l - 1], v2=rw_v2[l - 1])
        y_rw, v_first = _rwkv(z_rw, hv, v_first, rw_prm, bsz, seq)
        s5_prm = dict(a_re=s5_a_re[l], a_im=s5_a_im[l], log_dt=s5_log_dt[l], b_re=s5_b_re[l], b_im=s5_b_im[l],
                      c_re=s5_c_re[l], c_im=s5_c_im[l], d=s5_d[l], glu_w=s5_glu_w[l], glu_b=s5_glu_b[l],
                      out_g=s5_out_g[l])
        y_s5 = _s5(u_s5, s5_prm, bsz, seq)
        kv = _norm_proj(mem2, norm_mem[l], [xa_wk[l].astype(BF16), xa_wv[l].astype(BF16)], [BF16, BF16],
                        PROJ_BLOCK)
        kmem = kv[0].reshape(bsz, N_MEM, d)
        vmem = kv[1].reshape(bsz, N_MEM, d)
        x = _mix_xattn(x, y_rw, y_s5, w_out[l], norm_xa[l], xa_wq[l], kmem, vmem, xa_wo[l], bsz, seq)
        x2 = x.reshape(t, d)
        x8, h8, base, shift, gate = _peer_select(x2, norm_ffn[l], peer_wq[l], peer_keys[l])
        c = _peer_u(base, shift, h8, gate, _pack_table(peer_u[l]))
        t_tc = t - SC_TOKENS
        x_tc = _peer_v(base[:t_tc], shift[:t_tc], c[:t_tc], x8[:t_tc * SUBLANES], _pack_table(peer_v[l]))
        eidx_sc = (base[t_tc:] >> 2) + (shift[t_tc:] >> 4)
        x_sc = _peer_v_sc(eidx_sc, c[t_tc:], x2[t_tc:], peer_v[l])
        x = jnp.concatenate([x_tc, x_sc], axis=0).reshape(bsz, seq, d)
    return _final_norm(x.reshape(t, d), norm_final).reshape(bsz, seq, d)
```

```python
import functools
import math

import jax
import jax.numpy as jnp
from jax import lax
from jax.experimental import pallas as pl
from jax.experimental.pallas import tpu as pltpu
from jax.experimental.pallas import tpu_sc as plsc

F32 = jnp.float32
BF16 = jnp.bfloat16
U32 = jnp.uint32

LANES = 128
SUBLANES = 8
VMEM_LIMIT = 56 * 1024 * 1024

D_MODEL = 1024
D_RWKV = 512
RW_HEAD = 64
RW_COLS = 1792
D_S5 = 512
S5_GROUPS = 32
S5_CH = 16
S5_STATE = 64
S5_MODES = S5_GROUPS * S5_STATE
N_MEM = 256
XA_HEADS = 4
XA_HEAD = 256
PEER_HEADS = 8
PEER_NKEYS = 128
PEER_TOPK = 16
PEER_SEL = PEER_HEADS * PEER_TOPK
RMS_EPS = 1e-6
GN_EPS = 64e-5

RW_CHUNK = 128
S5_BLOCK = 256
PROJ_BLOCK = 512
XA_BLOCK = 256
SEL_BLOCK = 128
GATHER_BLOCK = 64


def _params(sem):
    return pltpu.CompilerParams(dimension_semantics=sem, vmem_limit_bytes=VMEM_LIMIT)


def _rms(x, g):
    ms = jnp.mean(x * x, axis=-1, keepdims=True)
    return x * lax.rsqrt(ms + RMS_EPS) * g


def _bdot(a, b):
    return jnp.dot(a.astype(BF16), b.astype(BF16), preferred_element_type=F32)


def _bdot_nt(a, b):
    return lax.dot_general(a.astype(BF16), b.astype(BF16), (((1,), (1,)), ((), ())),
                           preferred_element_type=F32)


def _sigmoid(x):
    return 1.0 / (1.0 + jnp.exp(-x))


def _softplus(x):
    return jnp.maximum(x, 0.0) + jnp.log(1.0 + jnp.exp(-jnp.abs(x)))


def _gelu(x):
    return 0.5 * x * (1.0 + jnp.tanh(math.sqrt(2.0 / math.pi) * (x + 0.044715 * (x * x * x))))


def _full(shape):
    n = len(shape)
    return pl.BlockSpec(shape, lambda *_: (0,) * n)


def _norm_proj_kernel(*refs, n_out):
    x_ref, g_ref = refs[0], refs[1]
    w_refs = refs[2:2 + n_out]
    o_refs = refs[2 + n_out:]
    h = _rms(x_ref[...], g_ref[...]).astype(BF16)
    for w_ref, o_ref in zip(w_refs, o_refs):
        o_ref[...] = jnp.dot(h, w_ref[...], preferred_element_type=F32).astype(o_ref.dtype)


def _norm_proj(x2, g, ws, out_dtypes, block):
    t, d = x2.shape
    block = min(block, t)
    in_specs = [pl.BlockSpec((block, d), lambda i: (i, 0)), _full((1, d))]
    in_specs += [_full(w.shape) for w in ws]
    out_specs = [pl.BlockSpec((block, w.shape[1]), lambda i: (i, 0)) for w in ws]
    out_shape = [jax.ShapeDtypeStruct((t, w.shape[1]), dt) for w, dt in zip(ws, out_dtypes)]
    return pl.pallas_call(
        functools.partial(_norm_proj_kernel, n_out=len(ws)),
        grid=(t // block,), in_specs=in_specs, out_specs=out_specs, out_shape=out_shape,
        compiler_params=_params(("parallel",)),
    )(x2, g.reshape(1, d), *ws)


def _seg_sum(x, mseg):
    hi = x.astype(BF16)
    lo = (x - hi.astype(F32)).astype(BF16)
    return (jnp.dot(hi, mseg, preferred_element_type=F32)
            + jnp.dot(lo, mseg, preferred_element_type=F32))


def _col_bcast(row):
    return jnp.broadcast_to(row, (LANES, LANES)).T


def _rwkv_kernel(*refs, first_layer):
    if first_layer:
        (z_ref, mu_ref, w0_ref, w2_ref, a0_ref, a2_ref, g2_ref, kk_ref, ka_ref, rk_ref,
         lng_ref, lnb_ref, mseg_ref, y_ref, vf_out_ref, zprev, hst) = refs
    else:
        (z_ref, hv_ref, vf_ref, v0_ref, v2_ref, mu_ref, w0_ref, w2_ref, a0_ref, a2_ref, g2_ref,
         kk_ref, ka_ref, rk_ref, lng_ref, lnb_ref, mseg_ref, y_ref, zprev, hst) = refs
    L = RW_CHUNK

    @pl.when(pl.program_id(1) == 0)
    def _():
        zprev[...] = jnp.zeros_like(zprev)
        hst[...] = jnp.zeros_like(hst)

    z = z_ref[0]
    row = lax.broadcasted_iota(jnp.int32, (L, 1), 0)
    zs = jnp.where(row == 0, zprev[...], pltpu.roll(z, 1, 0))
    zprev[...] = z[L - 1:L, :]
    z = z + (zs - z) * mu_ref[...]
    r = z[:, 0:512]
    k = z[:, 512:1024]
    v = z[:, 1024:1536]
    wa = z[:, 1536:1664]
    gd = z[:, 1664:1792]
    mseg = mseg_ref[...]

    wlin = w0_ref[...] + _bdot(jnp.tanh(wa), w2_ref[...])
    lw = -jnp.exp(-_softplus(-wlin) - 0.5)
    a = _sigmoid(a0_ref[...] + _bdot(wa, a2_ref[...]))
    g = _bdot(_sigmoid(gd), g2_ref[...])
    if first_layer:
        vf_out_ref[0] = v
    else:
        v = v + (vf_ref[0] - v) * _sigmoid(v0_ref[...] + _bdot(hv_ref[0], v2_ref[...]))
    kk = k * kk_ref[...]
    kk = kk / jnp.maximum(jnp.sqrt(_seg_sum(kk * kk, mseg)), 1e-12)
    k2 = k * (1.0 + (a - 1.0) * ka_ref[...])
    av = -kk
    bv = kk * a

    ti = lax.broadcasted_iota(jnp.int32, (L, L), 0)
    si = lax.broadcasted_iota(jnp.int32, (L, L), 1)
    tril = (ti >= si).astype(F32)
    cum = jnp.dot(tril, lw, preferred_element_type=F32, precision=lax.Precision.HIGHEST)
    mid = cum[L // 2 - 1:L // 2, :]
    cm = cum - mid
    ecum = jnp.exp(cm)
    einv = jnp.exp(-cm)
    rt = r * ecum
    kt = k2 * einv
    bt = bv * einv
    at = av * jnp.exp(cm - lw)
    p_end = jnp.exp(cum[L - 1:L, :])
    e_end = ecum[L - 1:L, :]
    e_mid = jnp.exp(mid)

    lane = lax.broadcasted_iota(jnp.int32, (1, LANES), 1)
    m0 = (lane < RW_HEAD).astype(F32)
    m1 = 1.0 - m0
    strict = ti > si
    incl = ti >= si
    bi = lax.broadcasted_iota(jnp.int32, (LANES, LANES), 0) < RW_HEAD
    bj = lax.broadcasted_iota(jnp.int32, (LANES, LANES), 1) < RW_HEAD
    bdmask = (bi == bj).astype(F32)
    zeros_ll = jnp.zeros((L, L), F32)

    ys = []
    for p in range(D_RWKV // LANES):
        sl = slice(LANES * p, LANES * (p + 1))
        A, B, K, R, V = at[:, sl], bt[:, sl], kt[:, sl], rt[:, sl], v[:, sl]
        btkt = jnp.concatenate([B.T, K.T], axis=1)
        h0 = hst[p]
        h0m = h0 * _col_bcast(e_mid[:, sl])
        sc = _bdot(jnp.concatenate([A * m0, A * m1, R * m0, R * m1], axis=0), btkt)
        aab = [jnp.where(strict, sc[e * L:(e + 1) * L, 0:L], 0.0) for e in range(2)]
        aak = [jnp.where(strict, sc[e * L:(e + 1) * L, L:2 * L], 0.0) for e in range(2)]
        arb = [jnp.where(incl, sc[(2 + e) * L:(3 + e) * L, 0:L], 0.0) for e in range(2)]
        ark = [jnp.where(incl, sc[(2 + e) * L:(3 + e) * L, L:2 * L], 0.0) for e in range(2)]
        arh = _bdot(jnp.concatenate([A, R], axis=0), h0m)
        v01 = jnp.concatenate([V * m0, V * m1], axis=0)
        x = arh[0:L] + _bdot(jnp.concatenate(aak, axis=1), v01)
        pm = jnp.concatenate(aab, axis=1)
        n_fac = int(math.log2(L))
        for it in range(n_fac):
            x = x + _bdot(pm, jnp.concatenate([x * m0, x * m1], axis=0))
            if it + 1 < n_fac:
                pd = jnp.concatenate(
                    [jnp.concatenate([pm[:, 0:L], zeros_ll], axis=1),
                     jnp.concatenate([zeros_ll, pm[:, L:2 * L]], axis=1)], axis=0)
                pm = _bdot(pm, pd)
        u = x
        yp = arh[L:2 * L] + _bdot(jnp.concatenate(arb + ark, axis=1),
                                  jnp.concatenate([u * m0, u * m1, v01], axis=0))
        upd = _bdot(btkt, jnp.concatenate([u, V], axis=0))
        hst[p] = (h0 * _col_bcast(p_end[:, sl]) + upd * _col_bcast(e_end[:, sl])) * bdmask
        ys.append(yp)
    y = jnp.concatenate(ys, axis=1)

    mean = _seg_sum(y, mseg) * (1.0 / RW_HEAD)
    d = y - mean
    var = _seg_sum(d * d, mseg) * (1.0 / RW_HEAD)
    yn = d * lax.rsqrt(var + GN_EPS) * lng_ref[...] + lnb_ref[...]
    bonus = _seg_sum(r * k2 * rk_ref[...], mseg) * v
    y_ref[0] = (yn + bonus) * g


def _pad_rows(w, rows, offset):
    out = jnp.zeros((rows, w.shape[1]), w.dtype)
    return out.at[offset:offset + w.shape[0]].set(w)


def _rwkv(z_rw, hv, v_first, prm, bsz, seq):
    L = RW_CHUNK
    first = v_first is None
    row = lambda a: a.reshape(1, -1).astype(F32)
    hid = jnp.arange(D_RWKV) // RW_HEAD
    mseg = (hid[:, None] == hid[None, :]).astype(BF16)
    w2p = _pad_rows(prm['w2'], LANES, 0).astype(BF16)
    a2p = _pad_rows(prm['a2'], LANES, 64).astype(BF16)
    common = [row(prm['mu']), row(prm['w0']), w2p, row(prm['a0']), a2p, prm['g2'].astype(BF16),
              row(prm['kk']), row(prm['ka']), row(prm['rk']), row(prm['lng']), row(prm['lnb']), mseg]
    tok = lambda w: pl.BlockSpec((1, L, w), lambda b, t: (b, t, 0))
    common_specs = [_full(c.shape) for c in common]
    y_shape = jax.ShapeDtypeStruct((bsz, seq, D_RWKV), F32)
    scratch = [pltpu.VMEM((1, RW_COLS), F32), pltpu.VMEM((D_RWKV // LANES, LANES, LANES), F32)]
    if first:
        args = [z_rw] + common
        in_specs = [tok(RW_COLS)] + common_specs
        out_shape = [y_shape, y_shape]
        out_specs = [tok(D_RWKV), tok(D_RWKV)]
    else:
        v2p = _pad_rows(prm['v2'], LANES, 0).astype(BF16)
        extra = [row(prm['v0']), v2p]
        args = [z_rw, hv, v_first] + extra + common
        in_specs = [tok(RW_COLS), tok(LANES), tok(D_RWKV)] + [_full(c.shape) for c in extra] + common_specs
        out_shape = [y_shape]
        out_specs = [tok(D_RWKV)]
    outs = pl.pallas_call(
        functools.partial(_rwkv_kernel, first_layer=first),
        grid=(bsz, seq // L), in_specs=in_specs, out_specs=out_specs, out_shape=out_shape,
        scratch_shapes=scratch, compiler_params=_params(("parallel", "arbitrary")),
    )(*args)
    return (outs[0], outs[1]) if first else (outs[0], v_first)


def _s5_kernel(u_ref, wb_ref, wc_ref, lpr_ref, lpi_ref, d_ref, gw_ref, gb_ref, og_ref, o_ref,
               car_re, car_im, xre, xim):
    tb = u_ref.shape[1]

    @pl.when(pl.program_id(1) == 0)
    def _():
        car_re[...] = jnp.zeros_like(car_re)
        car_im[...] = jnp.zeros_like(car_im)

    u = u_ref[0]
    bu = _bdot(u, wb_ref[...])
    xre[...] = bu[:, 0:S5_MODES]
    xim[...] = bu[:, S5_MODES:2 * S5_MODES]
    row = lax.broadcasted_iota(jnp.int32, (SUBLANES, 1), 0)

    def tile(i, carry):
        cr, ci = carry
        rs = pl.ds(pl.multiple_of(i * SUBLANES, SUBLANES), SUBLANES)
        br, bi = xre[rs, :], xim[rs, :]
        for dist in (1, 2, 4):
            keep = row >= dist
            sr = jnp.where(keep, pltpu.roll(br, dist, 0), 0.0)
            si = jnp.where(keep, pltpu.roll(bi, dist, 0), 0.0)
            lr = lpr_ref[dist - 1:dist, :]
            li = lpi_ref[dist - 1:dist, :]
            br, bi = br + lr * sr - li * si, bi + lr * si + li * sr
        pr, pi = lpr_ref[...], lpi_ref[...]
        xr = br + pr * cr - pi * ci
        xi = bi + pr * ci + pi * cr
        xre[rs, :] = xr
        xim[rs, :] = xi
        return xr[SUBLANES - 1:SUBLANES, :], xi[SUBLANES - 1:SUBLANES, :]

    cr, ci = lax.fori_loop(0, tb // SUBLANES, tile, (car_re[...], car_im[...]))
    car_re[...] = cr
    car_im[...] = ci
    wc = wc_ref[...]
    y = _bdot(xre[...], wc[0:S5_MODES]) + _bdot(xim[...], wc[S5_MODES:2 * S5_MODES])
    y = _gelu(y + d_ref[...] * u)
    y = y * _sigmoid(_bdot(y, gw_ref[...]) + gb_ref[...])
    o_ref[0] = _rms(y, og_ref[...])


def _s5_weights(a_re, a_im, log_dt, b_re, b_im, c_re, c_im):
    lam_re = jnp.minimum(a_re.astype(F32), -1e-4)
    lam_im = a_im.astype(F32)
    dt = jnp.exp(log_dt.astype(F32))[:, None]
    mag = jnp.exp(lam_re * dt)
    lb_re = mag * jnp.cos(lam_im * dt)
    lb_im = mag * jnp.sin(lam_im * dt)
    den = lam_re * lam_re + lam_im * lam_im
    c1_re = ((lb_re - 1.0) * lam_re + lb_im * lam_im) / den
    c1_im = (lb_im * lam_re - (lb_re - 1.0) * lam_im) / den
    br, bi = b_re.astype(F32), b_im.astype(F32)
    bb_re = c1_re[..., None] * br - c1_im[..., None] * bi
    bb_im = c1_re[..., None] * bi + c1_im[..., None] * br
    eye = jnp.eye(S5_GROUPS, dtype=F32)
    wb_re = jnp.einsum('gpc,gh->gchp', bb_re, eye).reshape(D_S5, S5_MODES)
    wb_im = jnp.einsum('gpc,gh->gchp', bb_im, eye).reshape(D_S5, S5_MODES)
    wb = jnp.concatenate([wb_re, wb_im], axis=1).astype(BF16)
    wc_re = jnp.einsum('gcp,gh->gphc', c_re.astype(F32), eye).reshape(S5_MODES, D_S5)
    wc_im = jnp.einsum('gcp,gh->gphc', c_im.astype(F32), eye).reshape(S5_MODES, D_S5)
    wc = jnp.concatenate([wc_re, -wc_im], axis=0).astype(BF16)
    pr, pi = [lb_re], [lb_im]
    for _ in range(SUBLANES - 1):
        pr, pi = pr + [pr[-1] * lb_re - pi[-1] * lb_im], pi + [pr[-1] * lb_im + pi[-1] * lb_re]
    lp_re = jnp.stack(pr).reshape(SUBLANES, S5_MODES)
    lp_im = jnp.stack(pi).reshape(SUBLANES, S5_MODES)
    return wb, wc, lp_re, lp_im


def _s5(u, prm, bsz, seq):
    tb = min(S5_BLOCK, seq)
    wb, wc, lp_re, lp_im = _s5_weights(prm['a_re'], prm['a_im'], prm['log_dt'], prm['b_re'], prm['b_im'],
                                       prm['c_re'], prm['c_im'])
    row = lambda a: a.reshape(1, -1).astype(F32)
    consts = [wb, wc, lp_re, lp_im, row(prm['d']), prm['glu_w'].astype(BF16), row(prm['glu_b']),
              row(prm['out_g'])]
    tok = pl.BlockSpec((1, tb, D_S5), lambda b, t: (b, t, 0))
    return pl.pallas_call(
        _s5_kernel, grid=(bsz, seq // tb),
        in_specs=[tok] + [_full(c.shape) for c in consts], out_specs=tok,
        out_shape=jax.ShapeDtypeStruct((bsz, seq, D_S5), F32),
        scratch_shapes=[pltpu.VMEM((1, S5_MODES), F32), pltpu.VMEM((1, S5_MODES), F32),
                        pltpu.VMEM((tb, S5_MODES), F32), pltpu.VMEM((tb, S5_MODES), F32)],
        compiler_params=_params(("parallel", "arbitrary")),
    )(u, *consts)


def _mix_xattn_kernel(x_ref, yr_ref, ys_ref, wo1_ref, wo2_ref, g_ref, wq_ref, k_ref, v_ref, wo_ref, o_ref):
    x1 = x_ref[0] + _bdot(yr_ref[0], wo1_ref[...]) + _bdot(ys_ref[0], wo2_ref[...])
    h = _rms(x1, g_ref[...])
    q = _bdot(h, wq_ref[...])
    km, vm = k_ref[0], v_ref[0]
    outs = []
    for hd in range(XA_HEADS):
        sl = slice(XA_HEAD * hd, XA_HEAD * (hd + 1))
        s = _bdot_nt(q[:, sl], km[:, sl]) * (XA_HEAD ** -0.5)
        s = s - jnp.max(s, axis=-1, keepdims=True)
        e = jnp.exp(s)
        p = e / jnp.sum(e, axis=-1, keepdims=True)
        outs.append(_bdot(p, vm[:, sl]))
    o = jnp.concatenate(outs, axis=1)
    o_ref[0] = x1 + _bdot(o, wo_ref[...])


def _mix_xattn(x, y_rw, y_s5, w_out, g, wq, kmem, vmem, wo, bsz, seq):
    tm = min(XA_BLOCK, seq)
    consts_a = [w_out[:D_RWKV].astype(BF16), w_out[D_RWKV:].astype(BF16), g.reshape(1, -1).astype(F32),
                wq.astype(BF16)]
    tok = lambda w: pl.BlockSpec((1, tm, w), lambda b, t: (b, t, 0))
    mem = pl.BlockSpec((1, N_MEM, D_MODEL), lambda b, t: (b, 0, 0))
    wo_b = wo.astype(BF16)
    return pl.pallas_call(
        _mix_xattn_kernel, grid=(bsz, seq // tm),
        in_specs=[tok(D_MODEL), tok(D_RWKV), tok(D_S5)] + [_full(c.shape) for c in consts_a]
                 + [mem, mem, _full(wo_b.shape)],
        out_specs=tok(D_MODEL), out_shape=jax.ShapeDtypeStruct((bsz, seq, D_MODEL), F32),
        compiler_params=_params(("parallel", "parallel")),
    )(x, y_rw, y_s5, *consts_a, kmem, vmem, wo_b)


def _top_rows(work, order, aux, val_ref, idx_ref):
    for it in range(PEER_TOPK):
        m = jnp.max(work, axis=0, keepdims=True)
        pos = jnp.min(jnp.where(work == m, order, jnp.inf), axis=0, keepdims=True)
        hit = order == pos
        val_ref[it:it + 1, :] = m
        if aux is None:
            idx_ref[it:it + 1, :] = pos
        else:
            idx_ref[it:it + 1, :] = jnp.sum(jnp.where(hit, aux, 0.0), axis=0, keepdims=True)
        work = jnp.where(hit, -jnp.inf, work)


_CAND_ROW_BLOCKS = [(0, PEER_TOPK), (1, SUBLANES), (2, SUBLANES), (3, SUBLANES)]
_CAND_COL_BLOCKS = [(0, PEER_TOPK, 4, 15), (1, SUBLANES, 4, 7), (2, SUBLANES, 4, 4)]
N_CAND = sum(n for _, n in _CAND_ROW_BLOCKS) + sum(n for _, n, _, _ in _CAND_COL_BLOCKS)


def _cand_consts(tm):
    flat, neg = [], []
    for a, nb in _CAND_ROW_BLOCKS:
        flat += [a * PEER_TOPK + b for b in range(nb)]
        neg += [0.0] * nb
    for b, na, lo, hi in _CAND_COL_BLOCKS:
        flat += [a * PEER_TOPK + b for a in range(na)]
        neg += [0.0 if lo <= a <= hi else -float('inf') for a in range(na)]
    flat = [f if n == 0.0 else 1000.0 + i for i, (f, n) in enumerate(zip(flat, neg))]
    col = lambda v: jnp.broadcast_to(jnp.asarray(v, F32)[:, None], (N_CAND, tm))
    return col(flat), col(neg)


def _cand_rows(row_vals, col_vals, combine):
    blocks = [combine(row_vals[a:a + 1, :], col_vals[0:nb, :]) for a, nb in _CAND_ROW_BLOCKS]
    blocks += [combine(row_vals[0:na, :], col_vals[b:b + 1, :]) for b, na, _, _ in _CAND_COL_BLOCKS]
    return jnp.concatenate(blocks, axis=0)


SEL_HEADS_PER_STEP = 2


def _peer_select_kernel(x_ref, g_ref, wq_ref, keys_ref, cflat_ref, cneg_ref, x8_ref, h8_ref, base_ref, shift_ref,
                        gate_ref, q3, idx_t, gate_t, *lists):
    tm = x_ref.shape[0]
    n = SEL_HEADS_PER_STEP
    s1, i1, s2, i2, top, eid = (lists[j * n:(j + 1) * n] for j in range(6))
    x = x_ref[...]
    h = _rms(x, g_ref[...])
    for r in range(D_MODEL // LANES):
        rows = pl.ds(r, tm, stride=D_MODEL // LANES)
        h8_ref[rows, :] = h[:, LANES * r:LANES * (r + 1)]
        x8_ref[rows, :] = x[:, LANES * r:LANES * (r + 1)]
    q = _bdot(h, wq_ref[...])
    for j in range(2 * PEER_HEADS):
        q3[j] = q[:, LANES * j:LANES * (j + 1)]
    iota_k = lax.broadcasted_iota(jnp.int32, (PEER_NKEYS, tm), 0).astype(F32)

    def heads(step, _):
        for u in range(SEL_HEADS_PER_STEP):
            hd = step * SEL_HEADS_PER_STEP + u
            sc1 = _bdot_nt(keys_ref[2 * hd], q3[2 * hd])
            sc2 = _bdot_nt(keys_ref[2 * hd + 1], q3[2 * hd + 1])
            _top_rows(sc1, iota_k, None, s1[u], i1[u])
            _top_rows(sc2, iota_k, None, s2[u], i2[u])
            cand = _cand_rows(s1[u][...], s2[u][...], lambda x, y: x + y) + cneg_ref[...]
            cidx = _cand_rows(i1[u][...], i2[u][...], lambda x, y: x * float(PEER_NKEYS) + y)
            _top_rows(cand, cflat_ref[...], cidx, top[u], eid[u])
            tv = top[u][...]
            e = jnp.exp(tv - jnp.max(tv, axis=0, keepdims=True))
            rs = pl.ds(pl.multiple_of(hd * PEER_TOPK, PEER_TOPK), PEER_TOPK)
            idx_t[rs, :] = eid[u][...]
            gate_t[rs, :] = e / jnp.sum(e, axis=0, keepdims=True)
        return 0

    lax.fori_loop(0, PEER_HEADS // SEL_HEADS_PER_STEP, heads, 0)
    e_t = idx_t[...].T
    pair = jnp.floor(e_t * 0.5)
    base_ref[...] = (pair * float(SUBLANES)).astype(jnp.int32)
    shift_ref[...] = ((e_t - 2.0 * pair) * 16.0).astype(jnp.int32)
    gate_ref[...] = gate_t[...].T


def _peer_select(x2, g, wq, keys):
    t = x2.shape[0]
    tm = SEL_BLOCK
    keys_b = keys.reshape(2 * PEER_HEADS, PEER_NKEYS, LANES).astype(BF16)
    wq_b = wq.astype(BF16)
    cflat, cneg = _cand_consts(tm)
    tokspec = lambda w: pl.BlockSpec((tm, w), lambda i: (i, 0))
    vm = lambda r: pltpu.VMEM((r, tm), F32)
    return pl.pallas_call(
        _peer_select_kernel, grid=(t // tm,),
        in_specs=[tokspec(D_MODEL), _full((1, D_MODEL)), _full(wq_b.shape), _full(keys_b.shape),
                  _full(cflat.shape), _full(cneg.shape)],
        out_specs=[pl.BlockSpec((tm * SUBLANES, LANES), lambda i: (i, 0))] * 2
                  + [tokspec(PEER_SEL), tokspec(PEER_SEL), tokspec(PEER_SEL)],
        out_shape=[jax.ShapeDtypeStruct((t * SUBLANES, LANES), F32)] * 2 + [
                   jax.ShapeDtypeStruct((t, PEER_SEL), jnp.int32),
                   jax.ShapeDtypeStruct((t, PEER_SEL), jnp.int32), jax.ShapeDtypeStruct((t, PEER_SEL), F32)],
        scratch_shapes=[pltpu.VMEM((2 * PEER_HEADS, tm, LANES), F32), vm(PEER_SEL), vm(PEER_SEL)]
                       + [vm(PEER_TOPK) for _ in range(6 * SEL_HEADS_PER_STEP)],
        compiler_params=_params(("parallel",)),
    )(x2, g.reshape(1, -1).astype(F32), wq_b, keys_b, cflat, cneg)


def _pack_table(tab):
    n, d = tab.shape
    bits = lax.bitcast_convert_type(tab.astype(BF16), jnp.uint16).astype(U32)
    bits = bits.reshape(n // 2, 2, d)
    packed = (bits[:, 0, :] << 16) | bits[:, 1, :]
    return lax.bitcast_convert_type(packed, jnp.int32).reshape(n // 2 * SUBLANES, LANES)


def _splat_into(src_ref, t, dst_ref, slot):
    tile = jnp.broadcast_to(src_ref[pl.ds(t, 1), :], (LANES, LANES)).T
    dst_ref[LANES * slot:LANES * (slot + 1), :] = tile


def _bcast_row(ref, row):
    return jnp.broadcast_to(ref[row:row + 1, :], (SUBLANES, LANES))


def _expert_tile(tab_ref, base, shift_splat, row):
    w = tab_ref[pl.ds(pl.multiple_of(base, SUBLANES), SUBLANES), :]
    return lax.bitcast_convert_type((w << _bcast_row(shift_splat, row)) & jnp.int32(-65536), F32)


def _token_rows(t):
    return pl.ds(pl.multiple_of(t * SUBLANES, SUBLANES), SUBLANES)


def _pipelined_tokens(tg, prep, work):
    prep(0, 0)

    def body(i, _):
        t0 = 2 * i
        prep(t0 + 1, 1)
        work(t0, 0)
        prep(jnp.minimum(t0 + 2, tg - 1), 0)
        work(t0 + 1, 1)
        return 0

    lax.fori_loop(0, tg // 2, body, 0)


def _table_spec(shape):
    return pl.BlockSpec(shape, lambda i: (0, 0), pipeline_mode=pl.Buffered(1))


N_CHUNK = D_MODEL // LANES
CHUNK_STRIDE = PEER_SEL + SUBLANES


def _peer_u_kernel(base_ref, shift_ref, h_ref, gate_ref, tab_ref, eye_ref, c_ref, shift_splat, *planes):
    tg = gate_ref.shape[0]
    ones = jnp.ones((LANES, LANES), BF16)

    half = N_CHUNK // 2

    def prep(t, slot):
        _splat_into(shift_ref, t, shift_splat, slot)

    def gather(t, slot):
        ht = h_ref[_token_rows(t), :]
        for kx in range(PEER_SEL):
            prod = _expert_tile(tab_ref, base_ref[t, kx], shift_splat, LANES * slot + kx) * ht
            fold = prod + pltpu.roll(prod, half, 0)
            planes[slot][pl.ds(kx, half, stride=CHUNK_STRIDE), :] = fold[0:half, :]

    def finish(t, slot):
        plane = planes[slot]
        acc = plane[0:PEER_SEL, :]
        for r in range(1, half):
            acc = acc + plane[CHUNK_STRIDE * r:CHUNK_STRIDE * r + PEER_SEL, :]
        hi = acc.astype(BF16)
        lo = (acc - hi.astype(F32)).astype(BF16)
        tot = jnp.dot(hi, ones, preferred_element_type=F32) + jnp.dot(lo, ones, preferred_element_type=F32)
        score = jnp.sum(tot * eye_ref[...], axis=0, keepdims=True)
        c_ref[pl.ds(t, 1), :] = gate_ref[pl.ds(t, 1), :] * _gelu(score)

    planes[1][...] = jnp.zeros_like(planes[1])
    prep(0, 0)

    def body(i, _):
        t0 = 2 * i
        prep(t0 + 1, 1)
        gather(t0, 0)
        finish(jnp.maximum(t0 - 1, 0), 1)
        prep(jnp.minimum(t0 + 2, tg - 1), 0)
        gather(t0 + 1, 1)
        finish(t0, 0)
        return 0

    lax.fori_loop(0, tg // 2, body, 0)
    finish(tg - 1, 1)


def _peer_u(base, shift, h8, gate, tab):
    t = base.shape[0]
    tg = GATHER_BLOCK
    eye = jnp.eye(LANES, dtype=F32)
    tokrow = pl.BlockSpec((tg, PEER_SEL), lambda i: (i, 0))
    return pl.pallas_call(
        _peer_u_kernel, grid=(t // tg,),
        in_specs=[pl.BlockSpec((tg, PEER_SEL), lambda i: (i, 0), memory_space=pltpu.SMEM), tokrow,
                  pl.BlockSpec((tg * SUBLANES, LANES), lambda i: (i, 0)), tokrow,
                  _table_spec(tab.shape), _full(eye.shape)],
        out_specs=tokrow, out_shape=jax.ShapeDtypeStruct((t, PEER_SEL), F32),
        scratch_shapes=[pltpu.VMEM((2 * LANES, LANES), jnp.int32)]
                       + [pltpu.VMEM((N_CHUNK // 2 * CHUNK_STRIDE, LANES), F32)] * 2,
        compiler_params=_params(("parallel",)),
    )(base, shift, h8, gate, tab, eye)


def _peer_v_kernel(base_ref, shift_ref, c_ref, x_ref, tab_ref, o_ref, shift_splat, c_splat, otile):
    tg = c_ref.shape[0]
    n_acc = 4

    def prep(t, slot):
        _splat_into(shift_ref, t, shift_splat, slot)
        _splat_into(c_ref, t, c_splat, slot)

    def work(t, slot):
        accs = [jnp.zeros((SUBLANES, LANES), F32) for _ in range(n_acc)]
        for kx in range(PEER_SEL):
            row = LANES * slot + kx
            accs[kx % n_acc] = accs[kx % n_acc] + (_bcast_row(c_splat, row)
                                                   * _expert_tile(tab_ref, base_ref[t, kx], shift_splat, row))
        rs = _token_rows(t)
        otile[rs, :] = x_ref[rs, :] + ((accs[0] + accs[1]) + (accs[2] + accs[3]))

    _pipelined_tokens(tg, prep, work)
    for r in range(N_CHUNK):
        o_ref[:, LANES * r:LANES * (r + 1)] = otile[pl.ds(r, tg, stride=N_CHUNK), :]


def _peer_v(base, shift, c, x8, tab):
    t = base.shape[0]
    tg = GATHER_BLOCK
    smem = pl.BlockSpec((tg, PEER_SEL), lambda i: (i, 0), memory_space=pltpu.SMEM)
    tokrow = pl.BlockSpec((tg, PEER_SEL), lambda i: (i, 0))
    tile = pl.BlockSpec((tg * SUBLANES, LANES), lambda i: (i, 0))
    return pl.pallas_call(
        _peer_v_kernel, grid=(t // tg,),
        in_specs=[smem, tokrow, tokrow, tile, _table_spec(tab.shape)],
        out_specs=pl.BlockSpec((tg, D_MODEL), lambda i: (i, 0)),
        out_shape=jax.ShapeDtypeStruct((t, D_MODEL), F32),
        scratch_shapes=[pltpu.VMEM((2 * LANES, LANES), jnp.int32), pltpu.VMEM((2 * LANES, LANES), F32),
                        pltpu.VMEM((tg * SUBLANES, LANES), F32)],
        compiler_params=_params(("parallel",)),
    )(base, shift, c, x8, tab)


SC_WORKERS = 32
SC_LANES = 16
SC_ROWS = 32
SC_TOKENS = 8192
SC_GROUP = 8
SC_ACC_REGS = 32


def _peer_v_sc(eidx, c, x2, tab):
    ts = eidx.shape[0]
    per_w = ts // SC_WORKERS
    n_chunk = PEER_SEL // SC_ROWS
    mesh = plsc.VectorSubcoreMesh(core_axis_name="c", subcore_axis_name="s")

    grp = SC_GROUP
    sel = grp * PEER_SEL
    n_pairs = grp * n_chunk // 2

    def body(tab_hbm, idx_hbm, c_hbm, x_hbm, out_hbm, idx_v, c_v, acc_v, rows0, rows1, sem0, sem1):
        wid = lax.axis_index("s") * 2 + lax.axis_index("c")
        bufs, sems = (rows0, rows1), (sem0, sem1)

        def gather(chunk, slot):
            start = pl.multiple_of(chunk * SC_ROWS, SC_ROWS)
            return pltpu.make_async_copy(tab_hbm.at[idx_v.at[pl.ds(start, SC_ROWS)]], bufs[slot], sems[slot])

        def accumulate(chunk, slot):
            tok = chunk // n_chunk
            per_tile = LANES // SC_LANES
            n_reg = SC_ACC_REGS
            for blk in range(D_MODEL // (n_reg * SC_LANES)):
                where = [(blk * n_reg + j) for j in range(n_reg)]
                piece = lambda ref, lead, q: ref[lead, q // per_tile, pl.ds(SC_LANES * (q % per_tile), SC_LANES)]

                def row(k, accs):
                    ck = plsc.load_gather(c_v, [jnp.full((SC_LANES,), chunk * SC_ROWS, jnp.int32) + k])
                    return tuple(a + ck * piece(bufs[slot], k, q) for a, q in zip(accs, where))

                accs = lax.fori_loop(0, SC_ROWS, row, tuple(piece(acc_v, tok, q) for q in where))
                for a, q in zip(accs, where):
                    acc_v[tok, q // per_tile, pl.ds(SC_LANES * (q % per_tile), SC_LANES)] = a

        @pl.loop(0, per_w // grp)
        def _(g):
            t0 = wid * per_w + g * grp
            flat = pl.ds(pl.multiple_of(t0 * PEER_SEL, sel), sel)
            pltpu.sync_copy(idx_hbm.at[flat], idx_v)
            pltpu.sync_copy(c_hbm.at[flat], c_v)
            pltpu.sync_copy(x_hbm.at[pl.ds(t0, grp)], acc_v)
            gather(0, 0).start()

            @pl.loop(0, n_pairs)
            def _(p):
                c0 = 2 * p
                gather(c0 + 1, 1).start()
                gather(c0, 0).wait()
                accumulate(c0, 0)
                gather(jnp.minimum(c0 + 2, 2 * n_pairs - 1), 0).start()
                gather(c0 + 1, 1).wait()
                accumulate(c0 + 1, 1)

            gather(0, 0).wait()
            pltpu.sync_copy(acc_v, out_hbm.at[pl.ds(t0, grp)])

    return pl.kernel(
        body, mesh=mesh, out_type=jax.ShapeDtypeStruct((ts, N_CHUNK, LANES), F32),
        scratch_types=[pltpu.VMEM((sel,), jnp.int32), pltpu.VMEM((sel,), F32),
                       pltpu.VMEM((grp, N_CHUNK, LANES), F32), pltpu.VMEM((SC_ROWS, N_CHUNK, LANES), F32),
                       pltpu.VMEM((SC_ROWS, N_CHUNK, LANES), F32), pltpu.SemaphoreType.DMA,
                       pltpu.SemaphoreType.DMA],
        compiler_params=pltpu.CompilerParams(needs_layout_passes=False),
    )(tab.reshape(-1, N_CHUNK, LANES), eidx.reshape(-1), c.reshape(-1), x2.reshape(ts, N_CHUNK, LANES)
      ).reshape(ts, D_MODEL)


def _final_norm_kernel(x_ref, g_ref, o_ref):
    o_ref[...] = _rms(x_ref[...], g_ref[...])


def _final_norm(x2, g):
    t, d = x2.shape
    tm = min(PROJ_BLOCK, t)
    spec = pl.BlockSpec((tm, d), lambda i: (i, 0))
    return pl.pallas_call(
        _final_norm_kernel, grid=(t // tm,), in_specs=[spec, _full((1, d))], out_specs=spec,
        out_shape=jax.ShapeDtypeStruct((t, d), F32), compiler_params=_params(("parallel",)),
    )(x2, g.reshape(1, d).astype(F32))


def kernel(x, mem, norm_mix, w_in, rw_mu, rw_w0, rw_w2, rw_a0, rw_a2, rw_g2, rw_kk, rw_ka, rw_rk, rw_v0, rw_v1, rw_v2, rw_lnx_g, rw_lnx_b, s5_a_re, s5_a_im, s5_log_dt, s5_b_re, s5_b_im, s5_c_re, s5_c_im, s5_d, s5_glu_w, s5_glu_b, s5_out_g, w_out, norm_xa, norm_mem, xa_wq, xa_wk, xa_wv, xa_wo, norm_ffn, peer_wq, peer_keys, peer_u, peer_v, norm_final):
    bsz, seq, d = x.shape
    t = bsz * seq
    depth = w_in.shape[0]
    mem2 = mem.reshape(bsz * N_MEM, d)
    v_first = None
    for l in range(depth):
        x2 = x.reshape(t, d)
        w_rw = w_in[l][:, :RW_COLS].astype(BF16)
        w_s5 = w_in[l][:, RW_COLS:].astype(BF16)
        ws = [w_rw, w_s5]
        if l > 0:
            ws.append(_pad_rows(rw_v1[l - 1].T, LANES, 0).T.astype(BF16))
        outs = _norm_proj(x2, norm_mix[l], ws, [F32] * len(ws), PROJ_BLOCK)
        z_rw = outs[0].reshape(bsz, seq, RW_COLS)
        u_s5 = outs[1].reshape(bsz, seq, D_S5)
        hv = outs[2].reshape(bsz, seq, LANES) if l > 0 else None
        rw_prm = dict(mu=rw_mu[l], w0=rw_w0[l], w2=rw_w2[l], a0=rw_a0[l], a2=rw_a2[l], g2=rw_g2[l],
                      kk=rw_kk[l], ka=rw_ka[l], rk=rw_rk[l], lng=rw_lnx_g[l], lnb=rw_lnx_b[l])
        if l > 0:
            rw_prm.update(v0=rw_v0[l - 1], v2=rw_v2[l - 1])
        y_rw, v_first = _rwkv(z_rw, hv, v_first, rw_prm, bsz, seq)
        s5_prm = dict(a_re=s5_a_re[l], a_im=s5_a_im[l], log_dt=s5_log_dt[l], b_re=s5_b_re[l], b_im=s5_b_im[l],
                      c_re=s5_c_re[l], c_im=s5_c_im[l], d=s5_d[l], glu_w=s5_glu_w[l], glu_b=s5_glu_b[l],
                      out_g=s5_out_g[l])
        y_s5 = _s5(u_s5, s5_prm, bsz, seq)
        kv = _norm_proj(mem2, norm_mem[l], [xa_wk[l].astype(BF16), xa_wv[l].astype(BF16)], [BF16, BF16],
                        PROJ_BLOCK)
        kmem = kv[0].reshape(bsz, N_MEM, d)
        vmem = kv[1].reshape(bsz, N_MEM, d)
        x = _mix_xattn(x, y_rw, y_s5, w_out[l], norm_xa[l], xa_wq[l], kmem, vmem, xa_wo[l], bsz, seq)
        x2 = x.reshape(t, d)
        x8, h8, base, shift, gate = _peer_select(x2, norm_ffn[l], peer_wq[l], peer_keys[l])
        c = _peer_u(base, shift, h8, gate, _pack_table(peer_u[l]))
        t_tc = t - SC_TOKENS
        x_tc = _peer_v(base[:t_tc], shift[:t_tc], c[:t_tc], x8[:t_tc * SUBLANES], _pack_table(peer_v[l]))
        eidx_sc = (base[t_tc:] >> 2) + (shift[t_tc:] >> 4)
        x_sc = _peer_v_sc(eidx_sc, c[t_tc:], x2[t_tc:], peer_v[l])
        x = jnp.concatenate([x_tc, x_sc], axis=0).reshape(bsz, seq, d)
    return _final_norm(x.reshape(t, d), norm_final).reshape(bsz, seq, d)
```

```python
import functools
import math

import jax
import jax.numpy as jnp
from jax import lax
from jax.experimental import pallas as pl
from jax.experimental.pallas import tpu as pltpu
from jax.experimental.pallas import tpu_sc as plsc

F32 = jnp.float32
BF16 = jnp.bfloat16
U32 = jnp.uint32

LANES = 128
SUBLANES = 8
VMEM_LIMIT = 56 * 1024 * 1024

D_MODEL = 1024
D_RWKV = 512
RW_HEAD = 64
RW_COLS = 1792
D_S5 = 512
S5_GROUPS = 32
S5_CH = 16
S5_STATE = 64
S5_MODES = S5_GROUPS * S5_STATE
N_MEM = 256
XA_HEADS = 4
XA_HEAD = 256
PEER_HEADS = 8
PEER_NKEYS = 128
PEER_TOPK = 16
PEER_SEL = PEER_HEADS * PEER_TOPK
RMS_EPS = 1e-6
GN_EPS = 64e-5

RW_CHUNK = 128
S5_BLOCK = 256
PROJ_BLOCK = 512
XA_BLOCK = 256
SEL_BLOCK = 128
GATHER_BLOCK = 64


def _params(sem):
    return pltpu.CompilerParams(dimension_semantics=sem, vmem_limit_bytes=VMEM_LIMIT)


def _rms(x, g):
    ms = jnp.mean(x * x, axis=-1, keepdims=True)
    return x * lax.rsqrt(ms + RMS_EPS) * g


def _bdot(a, b):
    return jnp.dot(a.astype(BF16), b.astype(BF16), preferred_element_type=F32)


def _bdot_nt(a, b):
    return lax.dot_general(a.astype(BF16), b.astype(BF16), (((1,), (1,)), ((), ())),
                           preferred_element_type=F32)


def _sigmoid(x):
    return 1.0 / (1.0 + jnp.exp(-x))


def _softplus(x):
    return jnp.maximum(x, 0.0) + jnp.log(1.0 + jnp.exp(-jnp.abs(x)))


def _gelu(x):
    return 0.5 * x * (1.0 + jnp.tanh(math.sqrt(2.0 / math.pi) * (x + 0.044715 * (x * x * x))))


def _full(shape):
    n = len(shape)
    return pl.BlockSpec(shape, lambda *_: (0,) * n)


def _norm_proj_kernel(*refs, n_out):
    x_ref, g_ref = refs[0], refs[1]
    w_refs = refs[2:2 + n_out]
    o_refs = refs[2 + n_out:]
    h = _rms(x_ref[...], g_ref[...]).astype(BF16)
    for w_ref, o_ref in zip(w_refs, o_refs):
        o_ref[...] = jnp.dot(h, w_ref[...], preferred_element_type=F32).astype(o_ref.dtype)


def _norm_proj(x2, g, ws, out_dtypes, block):
    t, d = x2.shape
    block = min(block, t)
    in_specs = [pl.BlockSpec((block, d), lambda i: (i, 0)), _full((1, d))]
    in_specs += [_full(w.shape) for w in ws]
    out_specs = [pl.BlockSpec((block, w.shape[1]), lambda i: (i, 0)) for w in ws]
    out_shape = [jax.ShapeDtypeStruct((t, w.shape[1]), dt) for w, dt in zip(ws, out_dtypes)]
    return pl.pallas_call(
        functools.partial(_norm_proj_kernel, n_out=len(ws)),
        grid=(t // block,), in_specs=in_specs, out_specs=out_specs, out_shape=out_shape,
        compiler_params=_params(("parallel",)),
    )(x2, g.reshape(1, d), *ws)


def _seg_sum(x, mseg):
    hi = x.astype(BF16)
    lo = (x - hi.astype(F32)).astype(BF16)
    return (jnp.dot(hi, mseg, preferred_element_type=F32)
            + jnp.dot(lo, mseg, preferred_element_type=F32))


def _col_bcast(row):
    return jnp.broadcast_to(row, (LANES, LANES)).T


def _rwkv_kernel(*refs, first_layer):
    if first_layer:
        (z_ref, mu_ref, w0_ref, w2_ref, a0_ref, a2_ref, g2_ref, kk_ref, ka_ref, rk_ref,
         lng_ref, lnb_ref, mseg_ref, y_ref, vf_out_ref, zprev, hst) = refs
    else:
        (z_ref, hv_ref, vf_ref, v0_ref, v2_ref, mu_ref, w0_ref, w2_ref, a0_ref, a2_ref, g2_ref,
         kk_ref, ka_ref, rk_ref, lng_ref, lnb_ref, mseg_ref, y_ref, zprev, hst) = refs
    L = RW_CHUNK

    @pl.when(pl.program_id(1) == 0)
    def _():
        zprev[...] = jnp.zeros_like(zprev)
        hst[...] = jnp.zeros_like(hst)

    z = z_ref[0]
    row = lax.broadcasted_iota(jnp.int32, (L, 1), 0)
    zs = jnp.where(row == 0, zprev[...], pltpu.roll(z, 1, 0))
    zprev[...] = z[L - 1:L, :]
    z = z + (zs - z) * mu_ref[...]
    r = z[:, 0:512]
    k = z[:, 512:1024]
    v = z[:, 1024:1536]
    wa = z[:, 1536:1664]
    gd = z[:, 1664:1792]
    mseg = mseg_ref[...]

    wlin = w0_ref[...] + _bdot(jnp.tanh(wa), w2_ref[...])
    lw = -jnp.exp(-_softplus(-wlin) - 0.5)
    a = _sigmoid(a0_ref[...] + _bdot(wa, a2_ref[...]))
    g = _bdot(_sigmoid(gd), g2_ref[...])
    if first_layer:
        vf_out_ref[0] = v
    else:
        v = v + (vf_ref[0] - v) * _sigmoid(v0_ref[...] + _bdot(hv_ref[0], v2_ref[...]))
    kk = k * kk_ref[...]
    kk = kk / jnp.maximum(jnp.sqrt(_seg_sum(kk * kk, mseg)), 1e-12)
    k2 = k * (1.0 + (a - 1.0) * ka_ref[...])
    av = -kk
    bv = kk * a

    ti = lax.broadcasted_iota(jnp.int32, (L, L), 0)
    si = lax.broadcasted_iota(jnp.int32, (L, L), 1)
    tril = (ti >= si).astype(F32)
    cum = jnp.dot(tril, lw, preferred_element_type=F32, precision=lax.Precision.HIGHEST)
    mid = cum[L // 2 - 1:L // 2, :]
    cm = cum - mid
    ecum = jnp.exp(cm)
    einv = jnp.exp(-cm)
    rt = r * ecum
    kt = k2 * einv
    bt = bv * einv
    at = av * jnp.exp(cm - lw)
    p_end = jnp.exp(cum[L - 1:L, :])
    e_end = ecum[L - 1:L, :]
    e_mid = jnp.exp(mid)

    lane = lax.broadcasted_iota(jnp.int32, (1, LANES), 1)
    m0 = (lane < RW_HEAD).astype(F32)
    m1 = 1.0 - m0
    strict = ti > si
    incl = ti >= si
    bi = lax.broadcasted_iota(jnp.int32, (LANES, LANES), 0) < RW_HEAD
    bj = lax.broadcasted_iota(jnp.int32, (LANES, LANES), 1) < RW_HEAD
    bdmask = (bi == bj).astype(F32)
    zeros_ll = jnp.zeros((L, L), F32)

    ys = []
    for p in range(D_RWKV // LANES):
        sl = slice(LANES * p, LANES * (p + 1))
        A, B, K, R, V = at[:, sl], bt[:, sl], kt[:, sl], rt[:, sl], v[:, sl]
        btkt = jnp.concatenate([B.T, K.T], axis=1)
        h0 = hst[p]
        h0m = h0 * _col_bcast(e_mid[:, sl])
        sc = _bdot(jnp.concatenate([A * m0, A * m1, R * m0, R * m1], axis=0), btkt)
        aab = [jnp.where(strict, sc[e * L:(e + 1) * L, 0:L], 0.0) for e in range(2)]
        aak = [jnp.where(strict, sc[e * L:(e + 1) * L, L:2 * L], 0.0) for e in range(2)]
        arb = [jnp.where(incl, sc[(2 + e) * L:(3 + e) * L, 0:L], 0.0) for e in range(2)]
        ark = [jnp.where(incl, sc[(2 + e) * L:(3 + e) * L, L:2 * L], 0.0) for e in range(2)]
        arh = _bdot(jnp.concatenate([A, R], axis=0), h0m)
        v01 = jnp.concatenate([V * m0, V * m1], axis=0)
        x = arh[0:L] + _bdot(jnp.concatenate(aak, axis=1), v01)
        pm = jnp.concatenate(aab, axis=1)
        n_fac = int(math.log2(L))
        for it in range(n_fac):
            x = x + _bdot(pm, jnp.concatenate([x * m0, x * m1], axis=0))
            if it + 1 < n_fac:
                pd = jnp.concatenate(
                    [jnp.concatenate([pm[:, 0:L], zeros_ll], axis=1),
                     jnp.concatenate([zeros_ll, pm[:, L:2 * L]], axis=1)], axis=0)
                pm = _bdot(pm, pd)
        u = x
        yp = arh[L:2 * L] + _bdot(jnp.concatenate(arb + ark, axis=1),
                                  jnp.concatenate([u * m0, u * m1, v01], axis=0))
        upd = _bdot(btkt, jnp.concatenate([u, V], axis=0))
        hst[p] = (h0 * _col_bcast(p_end[:, sl]) + upd * _col_bcast(e_end[:, sl])) * bdmask
        ys.append(yp)
    y = jnp.concatenate(ys, axis=1)

    mean = _seg_sum(y, mseg) * (1.0 / RW_HEAD)
    d = y - mean
    var = _seg_sum(d * d, mseg) * (1.0 / RW_HEAD)
    yn = d * lax.rsqrt(var + GN_EPS) * lng_ref[...] + lnb_ref[...]
    bonus = _seg_sum(r * k2 * rk_ref[...], mseg) * v
    y_ref[0] = (yn + bonus) * g


def _pad_rows(w, rows, offset):
    out = jnp.zeros((rows, w.shape[1]), w.dtype)
    return out.at[offset:offset + w.shape[0]].set(w)


def _rwkv(z_rw, hv, v_first, prm, bsz, seq):
    L = RW_CHUNK
    first = v_first is None
    row = lambda a: a.reshape(1, -1).astype(F32)
    hid = jnp.arange(D_RWKV) // RW_HEAD
    mseg = (hid[:, None] == hid[None, :]).astype(BF16)
    w2p = _pad_rows(prm['w2'], LANES, 0).astype(BF16)
    a2p = _pad_rows(prm['a2'], LANES, 64).astype(BF16)
    common = [row(prm['mu']), row(prm['w0']), w2p, row(prm['a0']), a2p, prm['g2'].astype(BF16),
              row(prm['kk']), row(prm['ka']), row(prm['rk']), row(prm['lng']), row(prm['lnb']), mseg]
    tok = lambda w: pl.BlockSpec((1, L, w), lambda b, t: (b, t, 0))
    common_specs = [_full(c.shape) for c in common]
    y_shape = jax.ShapeDtypeStruct((bsz, seq, D_RWKV), F32)
    scratch = [pltpu.VMEM((1, RW_COLS), F32), pltpu.VMEM((D_RWKV // LANES, LANES, LANES), F32)]
    if first:
        args = [z_rw] + common
        in_specs = [tok(RW_COLS)] + common_specs
        out_shape = [y_shape, y_shape]
        out_specs = [tok(D_RWKV), tok(D_RWKV)]
    else:
        v2p = _pad_rows(prm['v2'], LANES, 0).astype(BF16)
        extra = [row(prm['v0']), v2p]
        args = [z_rw, hv, v_first] + extra + common
        in_specs = [tok(RW_COLS), tok(LANES), tok(D_RWKV)] + [_full(c.shape) for c in extra] + common_specs
        out_shape = [y_shape]
        out_specs = [tok(D_RWKV)]
    outs = pl.pallas_call(
        functools.partial(_rwkv_kernel, first_layer=first),
        grid=(bsz, seq // L), in_specs=in_specs, out_specs=out_specs, out_shape=out_shape,
        scratch_shapes=scratch, compiler_params=_params(("parallel", "arbitrary")),
    )(*args)
    return (outs[0], outs[1]) if first else (outs[0], v_first)


def _s5_kernel(u_ref, wb_ref, wc_ref, lpr_ref, lpi_ref, d_ref, gw_ref, gb_ref, og_ref, o_ref,
               car_re, car_im, xre, xim):
    tb = u_ref.shape[1]

    @pl.when(pl.program_id(1) == 0)
    def _():
        car_re[...] = jnp.zeros_like(car_re)
        car_im[...] = jnp.zeros_like(car_im)

    u = u_ref[0]
    bu = _bdot(u, wb_ref[...])
    xre[...] = bu[:, 0:S5_MODES]
    xim[...] = bu[:, S5_MODES:2 * S5_MODES]
    row = lax.broadcasted_iota(jnp.int32, (SUBLANES, 1), 0)

    def tile(i, carry):
        cr, ci = carry
        rs = pl.ds(pl.multiple_of(i * SUBLANES, SUBLANES), SUBLANES)
        br, bi = xre[rs, :], xim[rs, :]
        for dist in (1, 2, 4):
            keep = row >= dist
            sr = jnp.where(keep, pltpu.roll(br, dist, 0), 0.0)
            si = jnp.where(keep, pltpu.roll(bi, dist, 0), 0.0)
            lr = lpr_ref[dist - 1:dist, :]
            li = lpi_ref[dist - 1:dist, :]
            br, bi = br + lr * sr - li * si, bi + lr * si + li * sr
        pr, pi = lpr_ref[...], lpi_ref[...]
        xr = br + pr * cr - pi * ci
        xi = bi + pr * ci + pi * cr
        xre[rs, :] = xr
        xim[rs, :] = xi
        return xr[SUBLANES - 1:SUBLANES, :], xi[SUBLANES - 1:SUBLANES, :]

    cr, ci = lax.fori_loop(0, tb // SUBLANES, tile, (car_re[...], car_im[...]))
    car_re[...] = cr
    car_im[...] = ci
    wc = wc_ref[...]
    y = _bdot(xre[...], wc[0:S5_MODES]) + _bdot(xim[...], wc[S5_MODES:2 * S5_MODES])
    y = _gelu(y + d_ref[...] * u)
    y = y * _sigmoid(_bdot(y, gw_ref[...]) + gb_ref[...])
    o_ref[0] = _rms(y, og_ref[...])


def _s5_weights(a_re, a_im, log_dt, b_re, b_im, c_re, c_im):
    lam_re = jnp.minimum(a_re.astype(F32), -1e-4)
    lam_im = a_im.astype(F32)
    dt = jnp.exp(log_dt.astype(F32))[:, None]
    mag = jnp.exp(lam_re * dt)
    lb_re = mag * jnp.cos(lam_im * dt)
    lb_im = mag * jnp.sin(lam_im * dt)
    den = lam_re * lam_re + lam_im * lam_im
    c1_re = ((lb_re - 1.0) * lam_re + lb_im * lam_im) / den
    c1_im = (lb_im * lam_re - (lb_re - 1.0) * lam_im) / den
    br, bi = b_re.astype(F32), b_im.astype(F32)
    bb_re = c1_re[..., None] * br - c1_im[..., None] * bi
    bb_im = c1_re[..., None] * bi + c1_im[..., None] * br
    eye = jnp.eye(S5_GROUPS, dtype=F32)
    wb_re = jnp.einsum('gpc,gh->gchp', bb_re, eye).reshape(D_S5, S5_MODES)
    wb_im = jnp.einsum('gpc,gh->gchp', bb_im, eye).reshape(D_S5, S5_MODES)
    wb = jnp.concatenate([wb_re, wb_im], axis=1).astype(BF16)
    wc_re = jnp.einsum('gcp,gh->gphc', c_re.astype(F32), eye).reshape(S5_MODES, D_S5)
    wc_im = jnp.einsum('gcp,gh->gphc', c_im.astype(F32), eye).reshape(S5_MODES, D_S5)
    wc = jnp.concatenate([wc_re, -wc_im], axis=0).astype(BF16)
    pr, pi = [lb_re], [lb_im]
    for _ in range(SUBLANES - 1):
        pr, pi = pr + [pr[-1] * lb_re - pi[-1] * lb_im], pi + [pr[-1] * lb_im + pi[-1] * lb_re]
    lp_re = jnp.stack(pr).reshape(SUBLANES, S5_MODES)
    lp_im = jnp.stack(pi).reshape(SUBLANES, S5_MODES)
    return wb, wc, lp_re, lp_im


def _s5(u, prm, bsz, seq):
    tb = min(S5_BLOCK, seq)
    wb, wc, lp_re, lp_im = _s5_weights(prm['a_re'], prm['a_im'], prm['log_dt'], prm['b_re'], prm['b_im'],
                                       prm['c_re'], prm['c_im'])
    row = lambda a: a.reshape(1, -1).astype(F32)
    consts = [wb, wc, lp_re, lp_im, row(prm['d']), prm['glu_w'].astype(BF16), row(prm['glu_b']),
              row(prm['out_g'])]
    tok = pl.BlockSpec((1, tb, D_S5), lambda b, t: (b, t, 0))
    return pl.pallas_call(
        _s5_kernel, grid=(bsz, seq // tb),
        in_specs=[tok] + [_full(c.shape) for c in consts], out_specs=tok,
        out_shape=jax.ShapeDtypeStruct((bsz, seq, D_S5), F32),
        scratch_shapes=[pltpu.VMEM((1, S5_MODES), F32), pltpu.VMEM((1, S5_MODES), F32),
                        pltpu.VMEM((tb, S5_MODES), F32), pltpu.VMEM((tb, S5_MODES), F32)],
        compiler_params=_params(("parallel", "arbitrary")),
    )(u, *consts)


def _mix_xattn_kernel(x_ref, yr_ref, ys_ref, wo1_ref, wo2_ref, g_ref, wq_ref, k_ref, v_ref, wo_ref, o_ref):
    x1 = x_ref[0] + _bdot(yr_ref[0], wo1_ref[...]) + _bdot(ys_ref[0], wo2_ref[...])
    h = _rms(x1, g_ref[...])
    q = _bdot(h, wq_ref[...])
    km, vm = k_ref[0], v_ref[0]
    outs = []
    for hd in range(XA_HEADS):
        sl = slice(XA_HEAD * hd, XA_HEAD * (hd + 1))
        s = _bdot_nt(q[:, sl], km[:, sl]) * (XA_HEAD ** -0.5)
        s = s - jnp.max(s, axis=-1, keepdims=True)
        e = jnp.exp(s)
        p = e / jnp.sum(e, axis=-1, keepdims=True)
        outs.append(_bdot(p, vm[:, sl]))
    o = jnp.concatenate(outs, axis=1)
    o_ref[0] = x1 + _bdot(o, wo_ref[...])


def _mix_xattn(x, y_rw, y_s5, w_out, g, wq, kmem, vmem, wo, bsz, seq):
    tm = min(XA_BLOCK, seq)
    consts_a = [w_out[:D_RWKV].astype(BF16), w_out[D_RWKV:].astype(BF16), g.reshape(1, -1).astype(F32),
                wq.astype(BF16)]
    tok = lambda w: pl.BlockSpec((1, tm, w), lambda b, t: (b, t, 0))
    mem = pl.BlockSpec((1, N_MEM, D_MODEL), lambda b, t: (b, 0, 0))
    wo_b = wo.astype(BF16)
    return pl.pallas_call(
        _mix_xattn_kernel, grid=(bsz, seq // tm),
        in_specs=[tok(D_MODEL), tok(D_RWKV), tok(D_S5)] + [_full(c.shape) for c in consts_a]
                 + [mem, mem, _full(wo_b.shape)],
        out_specs=tok(D_MODEL), out_shape=jax.ShapeDtypeStruct((bsz, seq, D_MODEL), F32),
        compiler_params=_params(("parallel", "parallel")),
    )(x, y_rw, y_s5, *consts_a, kmem, vmem, wo_b)


def _top_rows(work, order, aux, val_ref, idx_ref):
    for it in range(PEER_TOPK):
        m = jnp.max(work, axis=0, keepdims=True)
        pos = jnp.min(jnp.where(work == m, order, jnp.inf), axis=0, keepdims=True)
        hit = order == pos
        val_ref[it:it + 1, :] = m
        if aux is None:
            idx_ref[it:it + 1, :] = pos
        else:
            idx_ref[it:it + 1, :] = jnp.sum(jnp.where(hit, aux, 0.0), axis=0, keepdims=True)
        work = jnp.where(hit, -jnp.inf, work)


_CAND_ROW_BLOCKS = [(0, PEER_TOPK), (1, SUBLANES), (2, SUBLANES), (3, SUBLANES)]
_CAND_COL_BLOCKS = [(0, PEER_TOPK, 4, 15), (1, SUBLANES, 4, 7), (2, SUBLANES, 4, 4)]
N_CAND = sum(n for _, n in _CAND_ROW_BLOCKS) + sum(n for _, n, _, _ in _CAND_COL_BLOCKS)


def _cand_consts(tm):
    flat, neg = [], []
    for a, nb in _CAND_ROW_BLOCKS:
        flat += [a * PEER_TOPK + b for b in range(nb)]
        neg += [0.0] * nb
    for b, na, lo, hi in _CAND_COL_BLOCKS:
        flat += [a * PEER_TOPK + b for a in range(na)]
        neg += [0.0 if lo <= a <= hi else -float('inf') for a in range(na)]
    flat = [f if n == 0.0 else 1000.0 + i for i, (f, n) in enumerate(zip(flat, neg))]
    col = lambda v: jnp.broadcast_to(jnp.asarray(v, F32)[:, None], (N_CAND, tm))
    return col(flat), col(neg)


def _cand_rows(row_vals, col_vals, combine):
    blocks = [combine(row_vals[a:a + 1, :], col_vals[0:nb, :]) for a, nb in _CAND_ROW_BLOCKS]
    blocks += [combine(row_vals[0:na, :], col_vals[b:b + 1, :]) for b, na, _, _ in _CAND_COL_BLOCKS]
    return jnp.concatenate(blocks, axis=0)


SEL_HEADS_PER_STEP = 2


def _peer_select_kernel(x_ref, g_ref, wq_ref, keys_ref, cflat_ref, cneg_ref, x8_ref, h8_ref, base_ref, shift_ref,
                        gate_ref, q3, idx_t, gate_t, *lists):
    tm = x_ref.shape[0]
    n = SEL_HEADS_PER_STEP
    s1, i1, s2, i2, top, eid = (lists[j * n:(j + 1) * n] for j in range(6))
    x = x_ref[...]
    h = _rms(x, g_ref[...])
    for r in range(D_MODEL // LANES):
        rows = pl.ds(r, tm, stride=D_MODEL // LANES)
        h8_ref[rows, :] = h[:, LANES * r:LANES * (r + 1)]
        x8_ref[rows, :] = x[:, LANES * r:LANES * (r + 1)]
    q = _bdot(h, wq_ref[...])
    for j in range(2 * PEER_HEADS):
        q3[j] = q[:, LANES * j:LANES * (j + 1)]
    iota_k = lax.broadcasted_iota(jnp.int32, (PEER_NKEYS, tm), 0).astype(F32)

    def heads(step, _):
        for u in range(SEL_HEADS_PER_STEP):
            hd = step * SEL_HEADS_PER_STEP + u
            sc1 = _bdot_nt(keys_ref[2 * hd], q3[2 * hd])
            sc2 = _bdot_nt(keys_ref[2 * hd + 1], q3[2 * hd + 1])
            _top_rows(sc1, iota_k, None, s1[u], i1[u])
            _top_rows(sc2, iota_k, None, s2[u], i2[u])
            cand = _cand_rows(s1[u][...], s2[u][...], lambda x, y: x + y) + cneg_ref[...]
            cidx = _cand_rows(i1[u][...], i2[u][...], lambda x, y: x * float(PEER_NKEYS) + y)
            _top_rows(cand, cflat_ref[...], cidx, top[u], eid[u])
            tv = top[u][...]
            e = jnp.exp(tv - jnp.max(tv, axis=0, keepdims=True))
            rs = pl.ds(pl.multiple_of(hd * PEER_TOPK, PEER_TOPK), PEER_TOPK)
            idx_t[rs, :] = eid[u][...]
            gate_t[rs, :] = e / jnp.sum(e, axis=0, keepdims=True)
        return 0

    lax.fori_loop(0, PEER_HEADS // SEL_HEADS_PER_STEP, heads, 0)
    e_t = idx_t[...].T
    pair = jnp.floor(e_t * 0.5)
    base_ref[...] = (pair * float(SUBLANES)).astype(jnp.int32)
    shift_ref[...] = ((e_t - 2.0 * pair) * 16.0).astype(jnp.int32)
    gate_ref[...] = gate_t[...].T


def _peer_select(x2, g, wq, keys):
    t = x2.shape[0]
    tm = SEL_BLOCK
    keys_b = keys.reshape(2 * PEER_HEADS, PEER_NKEYS, LANES).astype(BF16)
    wq_b = wq.astype(BF16)
    cflat, cneg = _cand_consts(tm)
    tokspec = lambda w: pl.BlockSpec((tm, w), lambda i: (i, 0))
    vm = lambda r: pltpu.VMEM((r, tm), F32)
    return pl.pallas_call(
        _peer_select_kernel, grid=(t // tm,),
        in_specs=[tokspec(D_MODEL), _full((1, D_MODEL)), _full(wq_b.shape), _full(keys_b.shape),
                  _full(cflat.shape), _full(cneg.shape)],
        out_specs=[pl.BlockSpec((tm * SUBLANES, LANES), lambda i: (i, 0))] * 2
                  + [tokspec(PEER_SEL), tokspec(PEER_SEL), tokspec(PEER_SEL)],
        out_shape=[jax.ShapeDtypeStruct((t * SUBLANES, LANES), F32)] * 2 + [
                   jax.ShapeDtypeStruct((t, PEER_SEL), jnp.int32),
                   jax.ShapeDtypeStruct((t, PEER_SEL), jnp.int32), jax.ShapeDtypeStruct((t, PEER_SEL), F32)],
        scratch_shapes=[pltpu.VMEM((2 * PEER_HEADS, tm, LANES), F32), vm(PEER_SEL), vm(PEER_SEL)]
                       + [vm(PEER_TOPK) for _ in range(6 * SEL_HEADS_PER_STEP)],
        compiler_params=_params(("parallel",)),
    )(x2, g.reshape(1, -1).astype(F32), wq_b, keys_b, cflat, cneg)


def _pack_table(tab):
    n, d = tab.shape
    bits = lax.bitcast_convert_type(tab.astype(BF16), jnp.uint16).astype(U32)
    bits = bits.reshape(n // 2, 2, d)
    packed = (bits[:, 0, :] << 16) | bits[:, 1, :]
    return lax.bitcast_convert_type(packed, jnp.int32).reshape(n // 2 * SUBLANES, LANES)


def _splat_into(src_ref, t, dst_ref, slot):
    tile = jnp.broadcast_to(src_ref[pl.ds(t, 1), :], (LANES, LANES)).T
    dst_ref[LANES * slot:LANES * (slot + 1), :] = tile


def _bcast_row(ref, row):
    return jnp.broadcast_to(ref[row:row + 1, :], (SUBLANES, LANES))


def _expert_tile(tab_ref, base, shift_splat, row):
    w = tab_ref[pl.ds(pl.multiple_of(base, SUBLANES), SUBLANES), :]
    return lax.bitcast_convert_type((w << _bcast_row(shift_splat, row)) & jnp.int32(-65536), F32)


def _token_rows(t):
    return pl.ds(pl.multiple_of(t * SUBLANES, SUBLANES), SUBLANES)


def _pipelined_tokens(tg, prep, work):
    prep(0, 0)

    def body(i, _):
        t0 = 2 * i
        prep(t0 + 1, 1)
        work(t0, 0)
        prep(jnp.minimum(t0 + 2, tg - 1), 0)
        work(t0 + 1, 1)
        return 0

    lax.fori_loop(0, tg // 2, body, 0)


def _table_spec(shape):
    return pl.BlockSpec(shape, lambda i: (0, 0), pipeline_mode=pl.Buffered(1))


N_CHUNK = D_MODEL // LANES
CHUNK_STRIDE = PEER_SEL + SUBLANES


def _peer_u_kernel(base_ref, shift_ref, h_ref, gate_ref, tab_ref, eye_ref, c_ref, shift_splat, *planes):
    tg = gate_ref.shape[0]
    ones = jnp.ones((LANES, LANES), BF16)

    half = N_CHUNK // 2

    def prep(t, slot):
        _splat_into(shift_ref, t, shift_splat, slot)

    def gather(t, slot):
        ht = h_ref[_token_rows(t), :]
        for kx in range(PEER_SEL):
            prod = _expert_tile(tab_ref, base_ref[t, kx], shift_splat, LANES * slot + kx) * ht
            fold = prod + pltpu.roll(prod, half, 0)
            planes[slot][pl.ds(kx, half, stride=CHUNK_STRIDE), :] = fold[0:half, :]

    def finish(t, slot):
        plane = planes[slot]
        acc = plane[0:PEER_SEL, :]
        for r in range(1, half):
            acc = acc + plane[CHUNK_STRIDE * r:CHUNK_STRIDE * r + PEER_SEL, :]
        hi = acc.astype(BF16)
        lo = (acc - hi.astype(F32)).astype(BF16)
        tot = jnp.dot(hi, ones, preferred_element_type=F32) + jnp.dot(lo, ones, preferred_element_type=F32)
        score = jnp.sum(tot * eye_ref[...], axis=0, keepdims=True)
        c_ref[pl.ds(t, 1), :] = gate_ref[pl.ds(t, 1), :] * _gelu(score)

    planes[1][...] = jnp.zeros_like(planes[1])
    prep(0, 0)

    def body(i, _):
        t0 = 2 * i
        prep(t0 + 1, 1)
        gather(t0, 0)
        finish(jnp.maximum(t0 - 1, 0), 1)
        prep(jnp.minimum(t0 + 2, tg - 1), 0)
        gather(t0 + 1, 1)
        finish(t0, 0)
        return 0

    lax.fori_loop(0, tg // 2, body, 0)
    finish(tg - 1, 1)


def _peer_u(base, shift, h8, gate, tab):
    t = base.shape[0]
    tg = GATHER_BLOCK
    eye = jnp.eye(LANES, dtype=F32)
    tokrow = pl.BlockSpec((tg, PEER_SEL), lambda i: (i, 0))
    return pl.pallas_call(
        _peer_u_kernel, grid=(t // tg,),
        in_specs=[pl.BlockSpec((tg, PEER_SEL), lambda i: (i, 0), memory_space=pltpu.SMEM), tokrow,
                  pl.BlockSpec((tg * SUBLANES, LANES), lambda i: (i, 0)), tokrow,
                  _table_spec(tab.shape), _full(eye.shape)],
        out_specs=tokrow, out_shape=jax.ShapeDtypeStruct((t, PEER_SEL), F32),
        scratch_shapes=[pltpu.VMEM((2 * LANES, LANES), jnp.int32)]
                       + [pltpu.VMEM((N_CHUNK // 2 * CHUNK_STRIDE, LANES), F32)] * 2,
        compiler_params=_params(("parallel",)),
    )(base, shift, h8, gate, tab, eye)


def _peer_v_kernel(base_ref, shift_ref, c_ref, x_ref, tab_ref, o_ref, shift_splat, c_splat, otile):
    tg = c_ref.shape[0]
    n_acc = 4

    def prep(t, slot):
        _splat_into(shift_ref, t, shift_splat, slot)
        _splat_into(c_ref, t, c_splat, slot)

    def work(t, slot):
        accs = [jnp.zeros((SUBLANES, LANES), F32) for _ in range(n_acc)]
        for kx in range(PEER_SEL):
            row = LANES * slot + kx
            accs[kx % n_acc] = accs[kx % n_acc] + (_bcast_row(c_splat, row)
                                                   * _expert_tile(tab_ref, base_ref[t, kx], shift_splat, row))
        rs = _token_rows(t)
        otile[rs, :] = x_ref[rs, :] + ((accs[0] + accs[1]) + (accs[2] + accs[3]))

    _pipelined_tokens(tg, prep, work)
    for r in range(N_CHUNK):
        o_ref[:, LANES * r:LANES * (r + 1)] = otile[pl.ds(r, tg, stride=N_CHUNK), :]


def _peer_v(base, shift, c, x8, tab):
    t = base.shape[0]
    tg = GATHER_BLOCK
    smem = pl.BlockSpec((tg, PEER_SEL), lambda i: (i, 0), memory_space=pltpu.SMEM)
    tokrow = pl.BlockSpec((tg, PEER_SEL), lambda i: (i, 0))
    tile = pl.BlockSpec((tg * SUBLANES, LANES), lambda i: (i, 0))
    return pl.pallas_call(
        _peer_v_kernel, grid=(t // tg,),
        in_specs=[smem, tokrow, tokrow, tile, _table_spec(tab.shape)],
        out_specs=pl.BlockSpec((tg, D_MODEL), lambda i: (i, 0)),
        out_shape=jax.ShapeDtypeStruct((t, D_MODEL), F32),
        scratch_shapes=[pltpu.VMEM((2 * LANES, LANES), jnp.int32), pltpu.VMEM((2 * LANES, LANES), F32),
                        pltpu.VMEM((tg * SUBLANES, LANES), F32)],
        compiler_params=_params(("parallel",)),
    )(base, shift, c, x8, tab)


SC_WORKERS = 32
SC_LANES = 16
SC_ROWS = 32
SC_TOKENS = 8192
SC_GROUP = 8
SC_REGS = 32


def _peer_sc(eidx, gate, h8, x8, tab_u, tab_v):
    ts = eidx.shape[0]
    per_w = ts // SC_WORKERS
    n_chunk = PEER_SEL // SC_ROWS
    grp = SC_GROUP
    sel = grp * PEER_SEL
    n_pairs = grp * n_chunk // 2
    per_tile = LANES // SC_LANES
    n_blk = D_MODEL // (SC_REGS * SC_LANES)
    mesh = plsc.VectorSubcoreMesh(core_axis_name="c", subcore_axis_name="s")

    def piece(ref, lead, q):
        return ref[lead, q // per_tile, pl.ds(SC_LANES * (q % per_tile), SC_LANES)]

    def body(u_hbm, v_hbm, idx_hbm, gate_hbm, h_hbm, x_hbm, out_hbm,
             idx_v, c_v, h_v, acc_v, part_v, rows0, rows1, sem0, sem1):
        wid = lax.axis_index("s") * 2 + lax.axis_index("c")
        bufs, sems = (rows0, rows1), (sem0, sem1)
        lane = lax.iota(jnp.int32, SC_LANES)

        def stream(tab_hbm, compute):
            def gather(chunk, slot):
                start = pl.multiple_of(chunk * SC_ROWS, SC_ROWS)
                return pltpu.make_async_copy(tab_hbm.at[idx_v.at[pl.ds(start, SC_ROWS)]], bufs[slot], sems[slot])

            gather(0, 0).start()

            @pl.loop(0, n_pairs)
            def _(p):
                c0 = 2 * p
                gather(c0 + 1, 1).start()
                gather(c0, 0).wait()
                compute(c0, 0)
                gather(jnp.minimum(c0 + 2, 2 * n_pairs - 1), 0).start()
                gather(c0 + 1, 1).wait()
                compute(c0 + 1, 1)

            gather(0, 0).wait()

        def dots(chunk, slot):
            tok = chunk // n_chunk
            for blk in range(n_blk):
                where = [blk * SC_REGS + j for j in range(SC_REGS)]
                hs = [piece(h_v, tok, q) for q in where]

                def row(k, carry):
                    parts = [None] * 4
                    for i, q in enumerate(where):
                        term = piece(bufs[slot], k, q) * hs[i]
                        parts[i % 4] = term if parts[i % 4] is None else parts[i % 4] + term
                    tot = (parts[0] + parts[1]) + (parts[2] + parts[3])
                    at = pl.ds(pl.multiple_of((chunk * SC_ROWS + k) * SC_LANES, SC_LANES), SC_LANES)
                    if blk == 0:
                        part_v[at] = tot
                    else:
                        part_v[at] = part_v[at] + tot
                    return carry

                lax.fori_loop(0, SC_ROWS, row, 0)

        def weigh(chunk, slot):
            tok = chunk // n_chunk
            for blk in range(n_blk):
                where = [blk * SC_REGS + j for j in range(SC_REGS)]

                def row(k, accs):
                    ck = plsc.load_gather(c_v, [jnp.full((SC_LANES,), chunk * SC_ROWS, jnp.int32) + k])
                    return tuple(a + ck * piece(bufs[slot], k, q) for a, q in zip(accs, where))

                accs = lax.fori_loop(0, SC_ROWS, row, tuple(piece(acc_v, tok, q) for q in where))
                for a, q in zip(accs, where):
                    acc_v[tok, q // per_tile, pl.ds(SC_LANES * (q % per_tile), SC_LANES)] = a

        @pl.loop(0, per_w // grp)
        def _(g):
            t0 = wid * per_w + g * grp
            flat = pl.ds(pl.multiple_of(t0 * PEER_SEL, sel), sel)
            pltpu.sync_copy(idx_hbm.at[flat], idx_v)
            pltpu.sync_copy(gate_hbm.at[flat], c_v)
            pltpu.sync_copy(h_hbm.at[pl.ds(t0, grp)], h_v)
            pltpu.sync_copy(x_hbm.at[pl.ds(t0, grp)], acc_v)
            stream(u_hbm, dots)

            @pl.loop(0, sel // SC_LANES)
            def _(m):
                base = (m * SC_LANES + lane) * SC_LANES
                score = plsc.load_gather(part_v, [base])
                for l in range(1, SC_LANES):
                    score = score + plsc.load_gather(part_v, [base + l])
                z = math.sqrt(2.0 / math.pi) * (score + 0.044715 * (score * score * score))
                tanh_z = 1.0 - 2.0 / (jnp.exp(2.0 * z) + 1.0)
                at = pl.ds(pl.multiple_of(m * SC_LANES, SC_LANES), SC_LANES)
                c_v[at] = c_v[at] * (0.5 * score * (1.0 + tanh_z))

            stream(v_hbm, weigh)
            pltpu.sync_copy(acc_v, out_hbm.at[pl.ds(t0, grp)])

    tile = lambda n: pltpu.VMEM((n, N_CHUNK, LANES), F32)
    out = pl.kernel(
        body, mesh=mesh, out_type=jax.ShapeDtypeStruct((ts, N_CHUNK, LANES), F32),
        scratch_types=[pltpu.VMEM((sel,), jnp.int32), pltpu.VMEM((sel,), F32), tile(grp), tile(grp),
                       pltpu.VMEM((sel * SC_LANES,), F32), tile(SC_ROWS), tile(SC_ROWS),
                       pltpu.SemaphoreType.DMA, pltpu.SemaphoreType.DMA],
        compiler_params=pltpu.CompilerParams(needs_layout_passes=False),
    )(tab_u.reshape(-1, N_CHUNK, LANES), tab_v.reshape(-1, N_CHUNK, LANES), eidx.reshape(-1), gate.reshape(-1),
      h8.reshape(ts, N_CHUNK, LANES), x8.reshape(ts, N_CHUNK, LANES))
    return out.reshape(ts, D_MODEL)


def _final_norm_kernel(x_ref, g_ref, o_ref):
    o_ref[...] = _rms(x_ref[...], g_ref[...])


def _final_norm(x2, g):
    t, d = x2.shape
    tm = min(PROJ_BLOCK, t)
    spec = pl.BlockSpec((tm, d), lambda i: (i, 0))
    return pl.pallas_call(
        _final_norm_kernel, grid=(t // tm,), in_specs=[spec, _full((1, d))], out_specs=spec,
        out_shape=jax.ShapeDtypeStruct((t, d), F32), compiler_params=_params(("parallel",)),
    )(x2, g.reshape(1, d).astype(F32))


def kernel(x, mem, norm_mix, w_in, rw_mu, rw_w0, rw_w2, rw_a0, rw_a2, rw_g2, rw_kk, rw_ka, rw_rk, rw_v0, rw_v1, rw_v2, rw_lnx_g, rw_lnx_b, s5_a_re, s5_a_im, s5_log_dt, s5_b_re, s5_b_im, s5_c_re, s5_c_im, s5_d, s5_glu_w, s5_glu_b, s5_out_g, w_out, norm_xa, norm_mem, xa_wq, xa_wk, xa_wv, xa_wo, norm_ffn, peer_wq, peer_keys, peer_u, peer_v, norm_final):
    bsz, seq, d = x.shape
    t = bsz * seq
    depth = w_in.shape[0]
    mem2 = mem.reshape(bsz * N_MEM, d)
    v_first = None
    for l in range(depth):
        x2 = x.reshape(t, d)
        w_rw = w_in[l][:, :RW_COLS].astype(BF16)
        w_s5 = w_in[l][:, RW_COLS:].astype(BF16)
        ws = [w_rw, w_s5]
        if l > 0:
            ws.append(_pad_rows(rw_v1[l - 1].T, LANES, 0).T.astype(BF16))
        outs = _norm_proj(x2, norm_mix[l], ws, [F32] * len(ws), PROJ_BLOCK)
        z_rw = outs[0].reshape(bsz, seq, RW_COLS)
        u_s5 = outs[1].reshape(bsz, seq, D_S5)
        hv = outs[2].reshape(bsz, seq, LANES) if l > 0 else None
        rw_prm = dict(mu=rw_mu[l], w0=rw_w0[l], w2=rw_w2[l], a0=rw_a0[l], a2=rw_a2[l], g2=rw_g2[l],
                      kk=rw_kk[l], ka=rw_ka[l], rk=rw_rk[l], lng=rw_lnx_g[l], lnb=rw_lnx_b[l])
        if l > 0:
            rw_prm.update(v0=rw_v0[l - 1], v2=rw_v2[l - 1])
        y_rw, v_first = _rwkv(z_rw, hv, v_first, rw_prm, bsz, seq)
        s5_prm = dict(a_re=s5_a_re[l], a_im=s5_a_im[l], log_dt=s5_log_dt[l], b_re=s5_b_re[l], b_im=s5_b_im[l],
                      c_re=s5_c_re[l], c_im=s5_c_im[l], d=s5_d[l], glu_w=s5_glu_w[l], glu_b=s5_glu_b[l],
                      out_g=s5_out_g[l])
        y_s5 = _s5(u_s5, s5_prm, bsz, seq)
        kv = _norm_proj(mem2, norm_mem[l], [xa_wk[l].astype(BF16), xa_wv[l].astype(BF16)], [BF16, BF16],
                        PROJ_BLOCK)
        kmem = kv[0].reshape(bsz, N_MEM, d)
        vmem = kv[1].reshape(bsz, N_MEM, d)
        x = _mix_xattn(x, y_rw, y_s5, w_out[l], norm_xa[l], xa_wq[l], kmem, vmem, xa_wo[l], bsz, seq)
        x2 = x.reshape(t, d)
        x8, h8, base, shift, gate = _peer_select(x2, norm_ffn[l], peer_wq[l], peer_keys[l])
        t_tc = t - SC_TOKENS
        r_tc = t_tc * SUBLANES
        eidx_sc = (base[t_tc:] >> 2) + (shift[t_tc:] >> 4)
        x_sc = _peer_sc(eidx_sc, gate[t_tc:], h8[r_tc:], x8[r_tc:], peer_u[l], peer_v[l])
        c = _peer_u(base[:t_tc], shift[:t_tc], h8[:r_tc], gate[:t_tc], _pack_table(peer_u[l]))
        x_tc = _peer_v(base[:t_tc], shift[:t_tc], c, x8[:r_tc], _pack_table(peer_v[l]))
        x = jnp.concatenate([x_tc, x_sc], axis=0).reshape(bsz, seq, d)
    return _final_norm(x.reshape(t, d), norm_final).reshape(bsz, seq, d)
```

```python
import functools
import math

import jax
import jax.numpy as jnp
from jax import lax
from jax.experimental import pallas as pl
from jax.experimental.pallas import tpu as pltpu
from jax.experimental.pallas import tpu_sc as plsc

F32 = jnp.float32
BF16 = jnp.bfloat16
U32 = jnp.uint32

LANES = 128
SUBLANES = 8
VMEM_LIMIT = 56 * 1024 * 1024

D_MODEL = 1024
D_RWKV = 512
RW_HEAD = 64
RW_COLS = 1792
D_S5 = 512
S5_GROUPS = 32
S5_CH = 16
S5_STATE = 64
S5_MODES = S5_GROUPS * S5_STATE
N_MEM = 256
XA_HEADS = 4
XA_HEAD = 256
PEER_HEADS = 8
PEER_NKEYS = 128
PEER_TOPK = 16
PEER_SEL = PEER_HEADS * PEER_TOPK
RMS_EPS = 1e-6
GN_EPS = 64e-5

RW_CHUNK = 128
S5_BLOCK = 256
PROJ_BLOCK = 512
XA_BLOCK = 256
SEL_BLOCK = 128
GATHER_BLOCK = 64


def _params(sem):
    return pltpu.CompilerParams(dimension_semantics=sem, vmem_limit_bytes=VMEM_LIMIT)


def _rms(x, g):
    ms = jnp.mean(x * x, axis=-1, keepdims=True)
    return x * lax.rsqrt(ms + RMS_EPS) * g


def _bdot(a, b):
    return jnp.dot(a.astype(BF16), b.astype(BF16), preferred_element_type=F32)


def _bdot_nt(a, b):
    return lax.dot_general(a.astype(BF16), b.astype(BF16), (((1,), (1,)), ((), ())),
                           preferred_element_type=F32)


def _sigmoid(x):
    return 1.0 / (1.0 + jnp.exp(-x))


def _softplus(x):
    return jnp.maximum(x, 0.0) + jnp.log(1.0 + jnp.exp(-jnp.abs(x)))


def _gelu(x):
    return 0.5 * x * (1.0 + jnp.tanh(math.sqrt(2.0 / math.pi) * (x + 0.044715 * (x * x * x))))


def _full(shape):
    n = len(shape)
    return pl.BlockSpec(shape, lambda *_: (0,) * n)


def _norm_proj_kernel(*refs, n_out):
    x_ref, g_ref = refs[0], refs[1]
    w_refs = refs[2:2 + n_out]
    o_refs = refs[2 + n_out:]
    h = _rms(x_ref[...], g_ref[...]).astype(BF16)
    for w_ref, o_ref in zip(w_refs, o_refs):
        o_ref[...] = jnp.dot(h, w_ref[...], preferred_element_type=F32).astype(o_ref.dtype)


def _norm_proj(x2, g, ws, out_dtypes, block):
    t, d = x2.shape
    block = min(block, t)
    in_specs = [pl.BlockSpec((block, d), lambda i: (i, 0)), _full((1, d))]
    in_specs += [_full(w.shape) for w in ws]
    out_specs = [pl.BlockSpec((block, w.shape[1]), lambda i: (i, 0)) for w in ws]
    out_shape = [jax.ShapeDtypeStruct((t, w.shape[1]), dt) for w, dt in zip(ws, out_dtypes)]
    return pl.pallas_call(
        functools.partial(_norm_proj_kernel, n_out=len(ws)),
        grid=(t // block,), in_specs=in_specs, out_specs=out_specs, out_shape=out_shape,
        compiler_params=_params(("parallel",)),
    )(x2, g.reshape(1, d), *ws)


def _seg_sum(x, mseg):
    hi = x.astype(BF16)
    lo = (x - hi.astype(F32)).astype(BF16)
    return (jnp.dot(hi, mseg, preferred_element_type=F32)
            + jnp.dot(lo, mseg, preferred_element_type=F32))


def _col_bcast(row):
    return jnp.broadcast_to(row, (LANES, LANES)).T


def _rwkv_kernel(*refs, first_layer):
    if first_layer:
        (z_ref, mu_ref, w0_ref, w2_ref, a0_ref, a2_ref, g2_ref, kk_ref, ka_ref, rk_ref,
         lng_ref, lnb_ref, mseg_ref, y_ref, vf_out_ref, zprev, hst) = refs
    else:
        (z_ref, hv_ref, vf_ref, v0_ref, v2_ref, mu_ref, w0_ref, w2_ref, a0_ref, a2_ref, g2_ref,
         kk_ref, ka_ref, rk_ref, lng_ref, lnb_ref, mseg_ref, y_ref, zprev, hst) = refs
    L = RW_CHUNK

    @pl.when(pl.program_id(1) == 0)
    def _():
        zprev[...] = jnp.zeros_like(zprev)
        hst[...] = jnp.zeros_like(hst)

    z = z_ref[0]
    row = lax.broadcasted_iota(jnp.int32, (L, 1), 0)
    zs = jnp.where(row == 0, zprev[...], pltpu.roll(z, 1, 0))
    zprev[...] = z[L - 1:L, :]
    z = z + (zs - z) * mu_ref[...]
    r = z[:, 0:512]
    k = z[:, 512:1024]
    v = z[:, 1024:1536]
    wa = z[:, 1536:1664]
    gd = z[:, 1664:1792]
    mseg = mseg_ref[...]

    wlin = w0_ref[...] + _bdot(jnp.tanh(wa), w2_ref[...])
    lw = -jnp.exp(-_softplus(-wlin) - 0.5)
    a = _sigmoid(a0_ref[...] + _bdot(wa, a2_ref[...]))
    g = _bdot(_sigmoid(gd), g2_ref[...])
    if first_layer:
        vf_out_ref[0] = v
    else:
        v = v + (vf_ref[0] - v) * _sigmoid(v0_ref[...] + _bdot(hv_ref[0], v2_ref[...]))
    kk = k * kk_ref[...]
    kk = kk / jnp.maximum(jnp.sqrt(_seg_sum(kk * kk, mseg)), 1e-12)
    k2 = k * (1.0 + (a - 1.0) * ka_ref[...])
    av = -kk
    bv = kk * a

    ti = lax.broadcasted_iota(jnp.int32, (L, L), 0)
    si = lax.broadcasted_iota(jnp.int32, (L, L), 1)
    tril = (ti >= si).astype(F32)
    cum = jnp.dot(tril, lw, preferred_element_type=F32, precision=lax.Precision.HIGHEST)
    mid = cum[L // 2 - 1:L // 2, :]
    cm = cum - mid
    ecum = jnp.exp(cm)
    einv = jnp.exp(-cm)
    rt = r * ecum
    kt = k2 * einv
    bt = bv * einv
    at = av * jnp.exp(cm - lw)
    p_end = jnp.exp(cum[L - 1:L, :])
    e_end = ecum[L - 1:L, :]
    e_mid = jnp.exp(mid)

    lane = lax.broadcasted_iota(jnp.int32, (1, LANES), 1)
    m0 = (lane < RW_HEAD).astype(F32)
    m1 = 1.0 - m0
    strict = ti > si
    incl = ti >= si
    bi = lax.broadcasted_iota(jnp.int32, (LANES, LANES), 0) < RW_HEAD
    bj = lax.broadcasted_iota(jnp.int32, (LANES, LANES), 1) < RW_HEAD
    bdmask = (bi == bj).astype(F32)
    zeros_ll = jnp.zeros((L, L), F32)

    ys = []
    for p in range(D_RWKV // LANES):
        sl = slice(LANES * p, LANES * (p + 1))
        A, B, K, R, V = at[:, sl], bt[:, sl], kt[:, sl], rt[:, sl], v[:, sl]
        btkt = jnp.concatenate([B.T, K.T], axis=1)
        h0 = hst[p]
        h0m = h0 * _col_bcast(e_mid[:, sl])
        sc = _bdot(jnp.concatenate([A * m0, A * m1, R * m0, R * m1], axis=0), btkt)
        aab = [jnp.where(strict, sc[e * L:(e + 1) * L, 0:L], 0.0) for e in range(2)]
        aak = [jnp.where(strict, sc[e * L:(e + 1) * L, L:2 * L], 0.0) for e in range(2)]
        arb = [jnp.where(incl, sc[(2 + e) * L:(3 + e) * L, 0:L], 0.0) for e in range(2)]
        ark = [jnp.where(incl, sc[(2 + e) * L:(3 + e) * L, L:2 * L], 0.0) for e in range(2)]
        arh = _bdot(jnp.concatenate([A, R], axis=0), h0m)
        v01 = jnp.concatenate([V * m0, V * m1], axis=0)
        x = arh[0:L] + _bdot(jnp.concatenate(aak, axis=1), v01)
        pm = jnp.concatenate(aab, axis=1)
        n_fac = int(math.log2(L))
        for it in range(n_fac):
            x = x + _bdot(pm, jnp.concatenate([x * m0, x * m1], axis=0))
            if it + 1 < n_fac:
                pd = jnp.concatenate(
                    [jnp.concatenate([pm[:, 0:L], zeros_ll], axis=1),
                     jnp.concatenate([zeros_ll, pm[:, L:2 * L]], axis=1)], axis=0)
                pm = _bdot(pm, pd)
        u = x
        yp = arh[L:2 * L] + _bdot(jnp.concatenate(arb + ark, axis=1),
                                  jnp.concatenate([u * m0, u * m1, v01], axis=0))
        upd = _bdot(btkt, jnp.concatenate([u, V], axis=0))
        hst[p] = (h0 * _col_bcast(p_end[:, sl]) + upd * _col_bcast(e_end[:, sl])) * bdmask
        ys.append(yp)
    y = jnp.concatenate(ys, axis=1)

    mean = _seg_sum(y, mseg) * (1.0 / RW_HEAD)
    d = y - mean
    var = _seg_sum(d * d, mseg) * (1.0 / RW_HEAD)
    yn = d * lax.rsqrt(var + GN_EPS) * lng_ref[...] + lnb_ref[...]
    bonus = _seg_sum(r * k2 * rk_ref[...], mseg) * v
    y_ref[0] = (yn + bonus) * g


def _pad_rows(w, rows, offset):
    out = jnp.zeros((rows, w.shape[1]), w.dtype)
    return out.at[offset:offset + w.shape[0]].set(w)


def _rwkv(z_rw, hv, v_first, prm, bsz, seq):
    L = RW_CHUNK
    first = v_first is None
    row = lambda a: a.reshape(1, -1).astype(F32)
    hid = jnp.arange(D_RWKV) // RW_HEAD
    mseg = (hid[:, None] == hid[None, :]).astype(BF16)
    w2p = _pad_rows(prm['w2'], LANES, 0).astype(BF16)
    a2p = _pad_rows(prm['a2'], LANES, 64).astype(BF16)
    common = [row(prm['mu']), row(prm['w0']), w2p, row(prm['a0']), a2p, prm['g2'].astype(BF16),
              row(prm['kk']), row(prm['ka']), row(prm['rk']), row(prm['lng']), row(prm['lnb']), mseg]
    tok = lambda w: pl.BlockSpec((1, L, w), lambda b, t: (b, t, 0))
    common_specs = [_full(c.shape) for c in common]
    y_shape = jax.ShapeDtypeStruct((bsz, seq, D_RWKV), F32)
    scratch = [pltpu.VMEM((1, RW_COLS), F32), pltpu.VMEM((D_RWKV // LANES, LANES, LANES), F32)]
    if first:
        args = [z_rw] + common
        in_specs = [tok(RW_COLS)] + common_specs
        out_shape = [y_shape, y_shape]
        out_specs = [tok(D_RWKV), tok(D_RWKV)]
    else:
        v2p = _pad_rows(prm['v2'], LANES, 0).astype(BF16)
        extra = [row(prm['v0']), v2p]
        args = [z_rw, hv, v_first] + extra + common
        in_specs = [tok(RW_COLS), tok(LANES), tok(D_RWKV)] + [_full(c.shape) for c in extra] + common_specs
        out_shape = [y_shape]
        out_specs = [tok(D_RWKV)]
    outs = pl.pallas_call(
        functools.partial(_rwkv_kernel, first_layer=first),
        grid=(bsz, seq // L), in_specs=in_specs, out_specs=out_specs, out_shape=out_shape,
        scratch_shapes=scratch, compiler_params=_params(("parallel", "arbitrary")),
    )(*args)
    return (outs[0], outs[1]) if first else (outs[0], v_first)


def _s5_kernel(u_ref, wb_ref, wc_ref, lpr_ref, lpi_ref, d_ref, gw_ref, gb_ref, og_ref, o_ref,
               car_re, car_im, xre, xim):
    tb = u_ref.shape[1]

    @pl.when(pl.program_id(1) == 0)
    def _():
        car_re[...] = jnp.zeros_like(car_re)
        car_im[...] = jnp.zeros_like(car_im)

    u = u_ref[0]
    bu = _bdot(u, wb_ref[...])
    xre[...] = bu[:, 0:S5_MODES]
    xim[...] = bu[:, S5_MODES:2 * S5_MODES]
    row = lax.broadcasted_iota(jnp.int32, (SUBLANES, 1), 0)

    def tile(i, carry):
        cr, ci = carry
        rs = pl.ds(pl.multiple_of(i * SUBLANES, SUBLANES), SUBLANES)
        br, bi = xre[rs, :], xim[rs, :]
        for dist in (1, 2, 4):
            keep = row >= dist
            sr = jnp.where(keep, pltpu.roll(br, dist, 0), 0.0)
            si = jnp.where(keep, pltpu.roll(bi, dist, 0), 0.0)
            lr = lpr_ref[dist - 1:dist, :]
            li = lpi_ref[dist - 1:dist, :]
            br, bi = br + lr * sr - li * si, bi + lr * si + li * sr
        pr, pi = lpr_ref[...], lpi_ref[...]
        xr = br + pr * cr - pi * ci
        xi = bi + pr * ci + pi * cr
        xre[rs, :] = xr
        xim[rs, :] = xi
        return xr[SUBLANES - 1:SUBLANES, :], xi[SUBLANES - 1:SUBLANES, :]

    cr, ci = lax.fori_loop(0, tb // SUBLANES, tile, (car_re[...], car_im[...]))
    car_re[...] = cr
    car_im[...] = ci
    wc = wc_ref[...]
    y = _bdot(xre[...], wc[0:S5_MODES]) + _bdot(xim[...], wc[S5_MODES:2 * S5_MODES])
    y = _gelu(y + d_ref[...] * u)
    y = y * _sigmoid(_bdot(y, gw_ref[...]) + gb_ref[...])
    o_ref[0] = _rms(y, og_ref[...])


def _s5_weights(a_re, a_im, log_dt, b_re, b_im, c_re, c_im):
    lam_re = jnp.minimum(a_re.astype(F32), -1e-4)
    lam_im = a_im.astype(F32)
    dt = jnp.exp(log_dt.astype(F32))[:, None]
    mag = jnp.exp(lam_re * dt)
    lb_re = mag * jnp.cos(lam_im * dt)
    lb_im = mag * jnp.sin(lam_im * dt)
    den = lam_re * lam_re + lam_im * lam_im
    c1_re = ((lb_re - 1.0) * lam_re + lb_im * lam_im) / den
    c1_im = (lb_im * lam_re - (lb_re - 1.0) * lam_im) / den
    br, bi = b_re.astype(F32), b_im.astype(F32)
    bb_re = c1_re[..., None] * br - c1_im[..., None] * bi
    bb_im = c1_re[..., None] * bi + c1_im[..., None] * br
    eye = jnp.eye(S5_GROUPS, dtype=F32)
    wb_re = jnp.einsum('gpc,gh->gchp', bb_re, eye).reshape(D_S5, S5_MODES)
    wb_im = jnp.einsum('gpc,gh->gchp', bb_im, eye).reshape(D_S5, S5_MODES)
    wb = jnp.concatenate([wb_re, wb_im], axis=1).astype(BF16)
    wc_re = jnp.einsum('gcp,gh->gphc', c_re.astype(F32), eye).reshape(S5_MODES, D_S5)
    wc_im = jnp.einsum('gcp,gh->gphc', c_im.astype(F32), eye).reshape(S5_MODES, D_S5)
    wc = jnp.concatenate([wc_re, -wc_im], axis=0).astype(BF16)
    pr, pi = [lb_re], [lb_im]
    for _ in range(SUBLANES - 1):
        pr, pi = pr + [pr[-1] * lb_re - pi[-1] * lb_im], pi + [pr[-1] * lb_im + pi[-1] * lb_re]
    lp_re = jnp.stack(pr).reshape(SUBLANES, S5_MODES)
    lp_im = jnp.stack(pi).reshape(SUBLANES, S5_MODES)
    return wb, wc, lp_re, lp_im


def _s5(u, prm, bsz, seq):
    tb = min(S5_BLOCK, seq)
    wb, wc, lp_re, lp_im = _s5_weights(prm['a_re'], prm['a_im'], prm['log_dt'], prm['b_re'], prm['b_im'],
                                       prm['c_re'], prm['c_im'])
    row = lambda a: a.reshape(1, -1).astype(F32)
    consts = [wb, wc, lp_re, lp_im, row(prm['d']), prm['glu_w'].astype(BF16), row(prm['glu_b']),
              row(prm['out_g'])]
    tok = pl.BlockSpec((1, tb, D_S5), lambda b, t: (b, t, 0))
    return pl.pallas_call(
        _s5_kernel, grid=(bsz, seq // tb),
        in_specs=[tok] + [_full(c.shape) for c in consts], out_specs=tok,
        out_shape=jax.ShapeDtypeStruct((bsz, seq, D_S5), F32),
        scratch_shapes=[pltpu.VMEM((1, S5_MODES), F32), pltpu.VMEM((1, S5_MODES), F32),
                        pltpu.VMEM((tb, S5_MODES), F32), pltpu.VMEM((tb, S5_MODES), F32)],
        compiler_params=_params(("parallel", "arbitrary")),
    )(u, *consts)


def _mix_xattn_kernel(x_ref, yr_ref, ys_ref, wo1_ref, wo2_ref, g_ref, wq_ref, k_ref, v_ref, wo_ref, o_ref):
    x1 = x_ref[0] + _bdot(yr_ref[0], wo1_ref[...]) + _bdot(ys_ref[0], wo2_ref[...])
    h = _rms(x1, g_ref[...])
    q = _bdot(h, wq_ref[...])
    km, vm = k_ref[0], v_ref[0]
    outs = []
    for hd in range(XA_HEADS):
        sl = slice(XA_HEAD * hd, XA_HEAD * (hd + 1))
        s = _bdot_nt(q[:, sl], km[:, sl]) * (XA_HEAD ** -0.5)
        s = s - jnp.max(s, axis=-1, keepdims=True)
        e = jnp.exp(s)
        p = e / jnp.sum(e, axis=-1, keepdims=True)
        outs.append(_bdot(p, vm[:, sl]))
    o = jnp.concatenate(outs, axis=1)
    o_ref[0] = x1 + _bdot(o, wo_ref[...])


def _mix_xattn(x, y_rw, y_s5, w_out, g, wq, kmem, vmem, wo, bsz, seq):
    tm = min(XA_BLOCK, seq)
    consts_a = [w_out[:D_RWKV].astype(BF16), w_out[D_RWKV:].astype(BF16), g.reshape(1, -1).astype(F32),
                wq.astype(BF16)]
    tok = lambda w: pl.BlockSpec((1, tm, w), lambda b, t: (b, t, 0))
    mem = pl.BlockSpec((1, N_MEM, D_MODEL), lambda b, t: (b, 0, 0))
    wo_b = wo.astype(BF16)
    return pl.pallas_call(
        _mix_xattn_kernel, grid=(bsz, seq // tm),
        in_specs=[tok(D_MODEL), tok(D_RWKV), tok(D_S5)] + [_full(c.shape) for c in consts_a]
                 + [mem, mem, _full(wo_b.shape)],
        out_specs=tok(D_MODEL), out_shape=jax.ShapeDtypeStruct((bsz, seq, D_MODEL), F32),
        compiler_params=_params(("parallel", "parallel")),
    )(x, y_rw, y_s5, *consts_a, kmem, vmem, wo_b)


def _top_rows(work, order, aux, val_ref, idx_ref):
    for it in range(PEER_TOPK):
        m = jnp.max(work, axis=0, keepdims=True)
        pos = jnp.min(jnp.where(work == m, order, jnp.inf), axis=0, keepdims=True)
        hit = order == pos
        val_ref[it:it + 1, :] = m
        if aux is None:
            idx_ref[it:it + 1, :] = pos
        else:
            idx_ref[it:it + 1, :] = jnp.sum(jnp.where(hit, aux, 0.0), axis=0, keepdims=True)
        work = jnp.where(hit, -jnp.inf, work)


_CAND_ROW_BLOCKS = [(0, PEER_TOPK), (1, SUBLANES), (2, SUBLANES), (3, SUBLANES)]
_CAND_COL_BLOCKS = [(0, PEER_TOPK, 4, 15), (1, SUBLANES, 4, 7), (2, SUBLANES, 4, 4)]
N_CAND = sum(n for _, n in _CAND_ROW_BLOCKS) + sum(n for _, n, _, _ in _CAND_COL_BLOCKS)


def _cand_consts(tm):
    flat, neg = [], []
    for a, nb in _CAND_ROW_BLOCKS:
        flat += [a * PEER_TOPK + b for b in range(nb)]
        neg += [0.0] * nb
    for b, na, lo, hi in _CAND_COL_BLOCKS:
        flat += [a * PEER_TOPK + b for a in range(na)]
        neg += [0.0 if lo <= a <= hi else -float('inf') for a in range(na)]
    flat = [f if n == 0.0 else 1000.0 + i for i, (f, n) in enumerate(zip(flat, neg))]
    col = lambda v: jnp.broadcast_to(jnp.asarray(v, F32)[:, None], (N_CAND, tm))
    return col(flat), col(neg)


def _cand_rows(row_vals, col_vals, combine):
    blocks = [combine(row_vals[a:a + 1, :], col_vals[0:nb, :]) for a, nb in _CAND_ROW_BLOCKS]
    blocks += [combine(row_vals[0:na, :], col_vals[b:b + 1, :]) for b, na, _, _ in _CAND_COL_BLOCKS]
    return jnp.concatenate(blocks, axis=0)


SEL_HEADS_PER_STEP = 2


def _peer_select_kernel(x_ref, g_ref, wq_ref, keys_ref, cflat_ref, cneg_ref, x8_ref, h8_ref, base_ref, shift_ref,
                        gate_ref, q3, idx_t, gate_t, *lists):
    tm = x_ref.shape[0]
    n = SEL_HEADS_PER_STEP
    s1, i1, s2, i2, top, eid = (lists[j * n:(j + 1) * n] for j in range(6))
    x = x_ref[...]
    h = _rms(x, g_ref[...])
    for r in range(D_MODEL // LANES):
        rows = pl.ds(r, tm, stride=D_MODEL // LANES)
        h8_ref[rows, :] = h[:, LANES * r:LANES * (r + 1)]
        x8_ref[rows, :] = x[:, LANES * r:LANES * (r + 1)]
    q = _bdot(h, wq_ref[...])
    for j in range(2 * PEER_HEADS):
        q3[j] = q[:, LANES * j:LANES * (j + 1)]
    iota_k = lax.broadcasted_iota(jnp.int32, (PEER_NKEYS, tm), 0).astype(F32)

    def heads(step, _):
        for u in range(SEL_HEADS_PER_STEP):
            hd = step * SEL_HEADS_PER_STEP + u
            sc1 = _bdot_nt(keys_ref[2 * hd], q3[2 * hd])
            sc2 = _bdot_nt(keys_ref[2 * hd + 1], q3[2 * hd + 1])
            _top_rows(sc1, iota_k, None, s1[u], i1[u])
            _top_rows(sc2, iota_k, None, s2[u], i2[u])
            cand = _cand_rows(s1[u][...], s2[u][...], lambda x, y: x + y) + cneg_ref[...]
            cidx = _cand_rows(i1[u][...], i2[u][...], lambda x, y: x * float(PEER_NKEYS) + y)
            _top_rows(cand, cflat_ref[...], cidx, top[u], eid[u])
            tv = top[u][...]
            e = jnp.exp(tv - jnp.max(tv, axis=0, keepdims=True))
            rs = pl.ds(pl.multiple_of(hd * PEER_TOPK, PEER_TOPK), PEER_TOPK)
            idx_t[rs, :] = eid[u][...]
            gate_t[rs, :] = e / jnp.sum(e, axis=0, keepdims=True)
        return 0

    lax.fori_loop(0, PEER_HEADS // SEL_HEADS_PER_STEP, heads, 0)
    e_t = idx_t[...].T
    pair = jnp.floor(e_t * 0.5)
    base_ref[...] = (pair * float(SUBLANES)).astype(jnp.int32)
    shift_ref[...] = ((e_t - 2.0 * pair) * 16.0).astype(jnp.int32)
    gate_ref[...] = gate_t[...].T


def _peer_select(x2, g, wq, keys):
    t = x2.shape[0]
    tm = SEL_BLOCK
    keys_b = keys.reshape(2 * PEER_HEADS, PEER_NKEYS, LANES).astype(BF16)
    wq_b = wq.astype(BF16)
    cflat, cneg = _cand_consts(tm)
    tokspec = lambda w: pl.BlockSpec((tm, w), lambda i: (i, 0))
    vm = lambda r: pltpu.VMEM((r, tm), F32)
    return pl.pallas_call(
        _peer_select_kernel, grid=(t // tm,),
        in_specs=[tokspec(D_MODEL), _full((1, D_MODEL)), _full(wq_b.shape), _full(keys_b.shape),
                  _full(cflat.shape), _full(cneg.shape)],
        out_specs=[pl.BlockSpec((tm * SUBLANES, LANES), lambda i: (i, 0))] * 2
                  + [tokspec(PEER_SEL), tokspec(PEER_SEL), tokspec(PEER_SEL)],
        out_shape=[jax.ShapeDtypeStruct((t * SUBLANES, LANES), F32)] * 2 + [
                   jax.ShapeDtypeStruct((t, PEER_SEL), jnp.int32),
                   jax.ShapeDtypeStruct((t, PEER_SEL), jnp.int32), jax.ShapeDtypeStruct((t, PEER_SEL), F32)],
        scratch_shapes=[pltpu.VMEM((2 * PEER_HEADS, tm, LANES), F32), vm(PEER_SEL), vm(PEER_SEL)]
                       + [vm(PEER_TOPK) for _ in range(6 * SEL_HEADS_PER_STEP)],
        compiler_params=_params(("parallel",)),
    )(x2, g.reshape(1, -1).astype(F32), wq_b, keys_b, cflat, cneg)


def _pack_table(tab):
    n, d = tab.shape
    bits = lax.bitcast_convert_type(tab.astype(BF16), jnp.uint16).astype(U32)
    bits = bits.reshape(n // 2, 2, d)
    packed = (bits[:, 0, :] << 16) | bits[:, 1, :]
    return lax.bitcast_convert_type(packed, jnp.int32).reshape(n // 2 * SUBLANES, LANES)


def _splat_into(src_ref, t, dst_ref, slot):
    tile = jnp.broadcast_to(src_ref[pl.ds(t, 1), :], (LANES, LANES)).T
    dst_ref[LANES * slot:LANES * (slot + 1), :] = tile


def _bcast_row(ref, row):
    return jnp.broadcast_to(ref[row:row + 1, :], (SUBLANES, LANES))


def _expert_tile(tab_ref, base, shift_splat, row):
    w = tab_ref[pl.ds(pl.multiple_of(base, SUBLANES), SUBLANES), :]
    return lax.bitcast_convert_type((w << _bcast_row(shift_splat, row)) & jnp.int32(-65536), F32)


def _token_rows(t):
    return pl.ds(pl.multiple_of(t * SUBLANES, SUBLANES), SUBLANES)


def _pipelined_tokens(tg, prep, work):
    prep(0, 0)

    def body(i, _):
        t0 = 2 * i
        prep(t0 + 1, 1)
        work(t0, 0)
        prep(jnp.minimum(t0 + 2, tg - 1), 0)
        work(t0 + 1, 1)
        return 0

    lax.fori_loop(0, tg // 2, body, 0)


def _table_spec(shape):
    return pl.BlockSpec(shape, lambda i: (0, 0), pipeline_mode=pl.Buffered(1))


N_CHUNK = D_MODEL // LANES
CHUNK_STRIDE = PEER_SEL + SUBLANES


def _peer_u_kernel(base_ref, shift_ref, h_ref, gate_ref, tab_ref, eye_ref, c_ref, shift_splat, *planes):
    tg = gate_ref.shape[0]
    ones = jnp.ones((LANES, LANES), BF16)

    half = N_CHUNK // 2

    def prep(t, slot):
        _splat_into(shift_ref, t, shift_splat, slot)

    def gather(t, slot):
        ht = h_ref[_token_rows(t), :]
        for kx in range(PEER_SEL):
            prod = _expert_tile(tab_ref, base_ref[t, kx], shift_splat, LANES * slot + kx) * ht
            fold = prod + pltpu.roll(prod, half, 0)
            planes[slot][pl.ds(kx, half, stride=CHUNK_STRIDE), :] = fold[0:half, :]

    def finish(t, slot):
        plane = planes[slot]
        acc = plane[0:PEER_SEL, :]
        for r in range(1, half):
            acc = acc + plane[CHUNK_STRIDE * r:CHUNK_STRIDE * r + PEER_SEL, :]
        hi = acc.astype(BF16)
        lo = (acc - hi.astype(F32)).astype(BF16)
        tot = jnp.dot(hi, ones, preferred_element_type=F32) + jnp.dot(lo, ones, preferred_element_type=F32)
        score = jnp.sum(tot * eye_ref[...], axis=0, keepdims=True)
        c_ref[pl.ds(t, 1), :] = gate_ref[pl.ds(t, 1), :] * _gelu(score)

    planes[1][...] = jnp.zeros_like(planes[1])
    prep(0, 0)

    def body(i, _):
        t0 = 2 * i
        prep(t0 + 1, 1)
        gather(t0, 0)
        finish(jnp.maximum(t0 - 1, 0), 1)
        prep(jnp.minimum(t0 + 2, tg - 1), 0)
        gather(t0 + 1, 1)
        finish(t0, 0)
        return 0

    lax.fori_loop(0, tg // 2, body, 0)
    finish(tg - 1, 1)


def _peer_u(base, shift, h8, gate, tab):
    t = base.shape[0]
    tg = GATHER_BLOCK
    eye = jnp.eye(LANES, dtype=F32)
    tokrow = pl.BlockSpec((tg, PEER_SEL), lambda i: (i, 0))
    return pl.pallas_call(
        _peer_u_kernel, grid=(t // tg,),
        in_specs=[pl.BlockSpec((tg, PEER_SEL), lambda i: (i, 0), memory_space=pltpu.SMEM), tokrow,
                  pl.BlockSpec((tg * SUBLANES, LANES), lambda i: (i, 0)), tokrow,
                  _table_spec(tab.shape), _full(eye.shape)],
        out_specs=tokrow, out_shape=jax.ShapeDtypeStruct((t, PEER_SEL), F32),
        scratch_shapes=[pltpu.VMEM((2 * LANES, LANES), jnp.int32)]
                       + [pltpu.VMEM((N_CHUNK // 2 * CHUNK_STRIDE, LANES), F32)] * 2,
        compiler_params=_params(("parallel",)),
    )(base, shift, h8, gate, tab, eye)


def _peer_v_kernel(base_ref, shift_ref, c_ref, x_ref, tab_ref, o_ref, shift_splat, c_splat, otile):
    tg = c_ref.shape[0]
    n_acc = 4

    def prep(t, slot):
        _splat_into(shift_ref, t, shift_splat, slot)
        _splat_into(c_ref, t, c_splat, slot)

    def work(t, slot):
        accs = [jnp.zeros((SUBLANES, LANES), F32) for _ in range(n_acc)]
        for kx in range(PEER_SEL):
            row = LANES * slot + kx
            accs[kx % n_acc] = accs[kx % n_acc] + (_bcast_row(c_splat, row)
                                                   * _expert_tile(tab_ref, base_ref[t, kx], shift_splat, row))
        rs = _token_rows(t)
        otile[rs, :] = x_ref[rs, :] + ((accs[0] + accs[1]) + (accs[2] + accs[3]))

    _pipelined_tokens(tg, prep, work)
    for r in range(N_CHUNK):
        o_ref[:, LANES * r:LANES * (r + 1)] = otile[pl.ds(r, tg, stride=N_CHUNK), :]


def _peer_v(base, shift, c, x8, tab):
    t = base.shape[0]
    tg = GATHER_BLOCK
    smem = pl.BlockSpec((tg, PEER_SEL), lambda i: (i, 0), memory_space=pltpu.SMEM)
    tokrow = pl.BlockSpec((tg, PEER_SEL), lambda i: (i, 0))
    tile = pl.BlockSpec((tg * SUBLANES, LANES), lambda i: (i, 0))
    return pl.pallas_call(
        _peer_v_kernel, grid=(t // tg,),
        in_specs=[smem, tokrow, tokrow, tile, _table_spec(tab.shape)],
        out_specs=pl.BlockSpec((tg, D_MODEL), lambda i: (i, 0)),
        out_shape=jax.ShapeDtypeStruct((t, D_MODEL), F32),
        scratch_shapes=[pltpu.VMEM((2 * LANES, LANES), jnp.int32), pltpu.VMEM((2 * LANES, LANES), F32),
                        pltpu.VMEM((tg * SUBLANES, LANES), F32)],
        compiler_params=_params(("parallel",)),
    )(base, shift, c, x8, tab)


SC_WORKERS = 32
SC_LANES = 16
SC_ROWS = 32
SC_TOKENS = 16384
SC_GROUP = 8
SC_REGS = 32


def _peer_sc(eidx, gate, h8, x8, tab_u, tab_v):
    ts = eidx.shape[0]
    per_w = ts // SC_WORKERS
    n_chunk = PEER_SEL // SC_ROWS
    grp = SC_GROUP
    sel = grp * PEER_SEL
    n_pairs = grp * n_chunk // 2
    per_tile = LANES // SC_LANES
    n_blk = D_MODEL // (SC_REGS * SC_LANES)
    mesh = plsc.VectorSubcoreMesh(core_axis_name="c", subcore_axis_name="s")

    def piece(ref, lead, q):
        return ref[lead, q // per_tile, pl.ds(SC_LANES * (q % per_tile), SC_LANES)]

    def body(u_hbm, v_hbm, idx_hbm, gate_hbm, h_hbm, x_hbm, out_hbm,
             idx_v, c_v, h_v, acc_v, part_v, rows0, rows1, sem0, sem1):
        wid = lax.axis_index("s") * 2 + lax.axis_index("c")
        bufs, sems = (rows0, rows1), (sem0, sem1)
        lane = lax.iota(jnp.int32, SC_LANES)

        def stream(tab_hbm, compute):
            def gather(chunk, slot):
                start = pl.multiple_of(chunk * SC_ROWS, SC_ROWS)
                return pltpu.make_async_copy(tab_hbm.at[idx_v.at[pl.ds(start, SC_ROWS)]], bufs[slot], sems[slot])

            gather(0, 0).start()

            @pl.loop(0, n_pairs)
            def _(p):
                c0 = 2 * p
                gather(c0 + 1, 1).start()
                gather(c0, 0).wait()
                compute(c0, 0)
                gather(jnp.minimum(c0 + 2, 2 * n_pairs - 1), 0).start()
                gather(c0 + 1, 1).wait()
                compute(c0 + 1, 1)

            gather(0, 0).wait()

        def dots(chunk, slot):
            tok = chunk // n_chunk
            for blk in range(n_blk):
                where = [blk * SC_REGS + j for j in range(SC_REGS)]
                hs = [piece(h_v, tok, q) for q in where]

                def row(k, carry):
                    parts = [None] * 4
                    for i, q in enumerate(where):
                        term = piece(bufs[slot], k, q) * hs[i]
                        parts[i % 4] = term if parts[i % 4] is None else parts[i % 4] + term
                    tot = (parts[0] + parts[1]) + (parts[2] + parts[3])
                    at = pl.ds(pl.multiple_of((chunk * SC_ROWS + k) * SC_LANES, SC_LANES), SC_LANES)
                    if blk == 0:
                        part_v[at] = tot
                    else:
                        part_v[at] = part_v[at] + tot
                    return carry

                lax.fori_loop(0, SC_ROWS, row, 0)

        def weigh(chunk, slot):
            tok = chunk // n_chunk
            for blk in range(n_blk):
                where = [blk * SC_REGS + j for j in range(SC_REGS)]

                def row(k, accs):
                    ck = plsc.load_gather(c_v, [jnp.full((SC_LANES,), chunk * SC_ROWS, jnp.int32) + k])
                    return tuple(a + ck * piece(bufs[slot], k, q) for a, q in zip(accs, where))

                accs = lax.fori_loop(0, SC_ROWS, row, tuple(piece(acc_v, tok, q) for q in where))
                for a, q in zip(accs, where):
                    acc_v[tok, q // per_tile, pl.ds(SC_LANES * (q % per_tile), SC_LANES)] = a

        @pl.loop(0, per_w // grp)
        def _(g):
            t0 = wid * per_w + g * grp
            flat = pl.ds(pl.multiple_of(t0 * PEER_SEL, sel), sel)
            pltpu.sync_copy(idx_hbm.at[flat], idx_v)
            pltpu.sync_copy(gate_hbm.at[flat], c_v)
            pltpu.sync_copy(h_hbm.at[pl.ds(t0, grp)], h_v)
            pltpu.sync_copy(x_hbm.at[pl.ds(t0, grp)], acc_v)
            stream(u_hbm, dots)

            @pl.loop(0, sel // SC_LANES)
            def _(m):
                base = (m * SC_LANES + lane) * SC_LANES
                score = plsc.load_gather(part_v, [base])
                for l in range(1, SC_LANES):
                    score = score + plsc.load_gather(part_v, [base + l])
                z = math.sqrt(2.0 / math.pi) * (score + 0.044715 * (score * score * score))
                tanh_z = 1.0 - 2.0 / (jnp.exp(2.0 * z) + 1.0)
                at = pl.ds(pl.multiple_of(m * SC_LANES, SC_LANES), SC_LANES)
                c_v[at] = c_v[at] * (0.5 * score * (1.0 + tanh_z))

            stream(v_hbm, weigh)
            pltpu.sync_copy(acc_v, out_hbm.at[pl.ds(t0, grp)])

    tile = lambda n: pltpu.VMEM((n, N_CHUNK, LANES), F32)
    out = pl.kernel(
        body, mesh=mesh, out_type=jax.ShapeDtypeStruct((ts, N_CHUNK, LANES), F32),
        scratch_types=[pltpu.VMEM((sel,), jnp.int32), pltpu.VMEM((sel,), F32), tile(grp), tile(grp),
                       pltpu.VMEM((sel * SC_LANES,), F32), tile(SC_ROWS), tile(SC_ROWS),
                       pltpu.SemaphoreType.DMA, pltpu.SemaphoreType.DMA],
        compiler_params=pltpu.CompilerParams(needs_layout_passes=False),
    )(tab_u.reshape(-1, N_CHUNK, LANES), tab_v.reshape(-1, N_CHUNK, LANES), eidx.reshape(-1), gate.reshape(-1),
      h8.reshape(ts, N_CHUNK, LANES), x8.reshape(ts, N_CHUNK, LANES))
    return out.reshape(ts, D_MODEL)


def _final_norm_kernel(x_ref, g_ref, o_ref):
    o_ref[...] = _rms(x_ref[...], g_ref[...])


def _final_norm(x2, g):
    t, d = x2.shape
    tm = min(PROJ_BLOCK, t)
    spec = pl.BlockSpec((tm, d), lambda i: (i, 0))
    return pl.pallas_call(
        _final_norm_kernel, grid=(t // tm,), in_specs=[spec, _full((1, d))], out_specs=spec,
        out_shape=jax.ShapeDtypeStruct((t, d), F32), compiler_params=_params(("parallel",)),
    )(x2, g.reshape(1, d).astype(F32))


def kernel(x, mem, norm_mix, w_in, rw_mu, rw_w0, rw_w2, rw_a0, rw_a2, rw_g2, rw_kk, rw_ka, rw_rk, rw_v0, rw_v1, rw_v2, rw_lnx_g, rw_lnx_b, s5_a_re, s5_a_im, s5_log_dt, s5_b_re, s5_b_im, s5_c_re, s5_c_im, s5_d, s5_glu_w, s5_glu_b, s5_out_g, w_out, norm_xa, norm_mem, xa_wq, xa_wk, xa_wv, xa_wo, norm_ffn, peer_wq, peer_keys, peer_u, peer_v, norm_final):
    bsz, seq, d = x.shape
    t = bsz * seq
    depth = w_in.shape[0]
    mem2 = mem.reshape(bsz * N_MEM, d)
    v_first = None
    for l in range(depth):
        x2 = x.reshape(t, d)
        w_rw = w_in[l][:, :RW_COLS].astype(BF16)
        w_s5 = w_in[l][:, RW_COLS:].astype(BF16)
        ws = [w_rw, w_s5]
        if l > 0:
            ws.append(_pad_rows(rw_v1[l - 1].T, LANES, 0).T.astype(BF16))
        outs = _norm_proj(x2, norm_mix[l], ws, [F32] * len(ws), PROJ_BLOCK)
        z_rw = outs[0].reshape(bsz, seq, RW_COLS)
        u_s5 = outs[1].reshape(bsz, seq, D_S5)
        hv = outs[2].reshape(bsz, seq, LANES) if l > 0 else None
        rw_prm = dict(mu=rw_mu[l], w0=rw_w0[l], w2=rw_w2[l], a0=rw_a0[l], a2=rw_a2[l], g2=rw_g2[l],
                      kk=rw_kk[l], ka=rw_ka[l], rk=rw_rk[l], lng=rw_lnx_g[l], lnb=rw_lnx_b[l])
        if l > 0:
            rw_prm.update(v0=rw_v0[l - 1], v2=rw_v2[l - 1])
        y_rw, v_first = _rwkv(z_rw, hv, v_first, rw_prm, bsz, seq)
        s5_prm = dict(a_re=s5_a_re[l], a_im=s5_a_im[l], log_dt=s5_log_dt[l], b_re=s5_b_re[l], b_im=s5_b_im[l],
                      c_re=s5_c_re[l], c_im=s5_c_im[l], d=s5_d[l], glu_w=s5_glu_w[l], glu_b=s5_glu_b[l],
                      out_g=s5_out_g[l])
        y_s5 = _s5(u_s5, s5_prm, bsz, seq)
        kv = _norm_proj(mem2, norm_mem[l], [xa_wk[l].astype(BF16), xa_wv[l].astype(BF16)], [BF16, BF16],
                        PROJ_BLOCK)
        kmem = kv[0].reshape(bsz, N_MEM, d)
        vmem = kv[1].reshape(bsz, N_MEM, d)
        x = _mix_xattn(x, y_rw, y_s5, w_out[l], norm_xa[l], xa_wq[l], kmem, vmem, xa_wo[l], bsz, seq)
        x2 = x.reshape(t, d)
        x8, h8, base, shift, gate = _peer_select(x2, norm_ffn[l], peer_wq[l], peer_keys[l])
        t_tc = t - SC_TOKENS
        r_tc = t_tc * SUBLANES
        eidx_sc = (base[t_tc:] >> 2) + (shift[t_tc:] >> 4)
        x_sc = _peer_sc(eidx_sc, gate[t_tc:], h8[r_tc:], x8[r_tc:], peer_u[l], peer_v[l])
        c = _peer_u(base[:t_tc], shift[:t_tc], h8[:r_tc], gate[:t_tc], _pack_table(peer_u[l]))
        x_tc = _peer_v(base[:t_tc], shift[:t_tc], c, x8[:r_tc], _pack_table(peer_v[l]))
        x = jnp.concatenate([x_tc, x_sc], axis=0).reshape(bsz, seq, d)
    return _final_norm(x.reshape(t, d), norm_final).reshape(bsz, seq, d)
```

```python
import functools
import math

import jax
import jax.numpy as jnp
from jax import lax
from jax.experimental import pallas as pl
from jax.experimental.pallas import tpu as pltpu
from jax.experimental.pallas import tpu_sc as plsc

F32 = jnp.float32
BF16 = jnp.bfloat16
U32 = jnp.uint32

LANES = 128
SUBLANES = 8
VMEM_LIMIT = 56 * 1024 * 1024

D_MODEL = 1024
D_RWKV = 512
RW_HEAD = 64
RW_COLS = 1792
D_S5 = 512
S5_GROUPS = 32
S5_CH = 16
S5_STATE = 64
S5_MODES = S5_GROUPS * S5_STATE
N_MEM = 256
XA_HEADS = 4
XA_HEAD = 256
PEER_HEADS = 8
PEER_NKEYS = 128
PEER_TOPK = 16
PEER_SEL = PEER_HEADS * PEER_TOPK
RMS_EPS = 1e-6
GN_EPS = 64e-5

RW_CHUNK = 128
S5_BLOCK = 256
PROJ_BLOCK = 512
XA_BLOCK = 256
SEL_BLOCK = 128
GATHER_BLOCK = 64


def _params(sem):
    return pltpu.CompilerParams(dimension_semantics=sem, vmem_limit_bytes=VMEM_LIMIT)


def _rms(x, g):
    ms = jnp.mean(x * x, axis=-1, keepdims=True)
    return x * lax.rsqrt(ms + RMS_EPS) * g


def _bdot(a, b):
    return jnp.dot(a.astype(BF16), b.astype(BF16), preferred_element_type=F32)


def _bdot_nt(a, b):
    return lax.dot_general(a.astype(BF16), b.astype(BF16), (((1,), (1,)), ((), ())),
                           preferred_element_type=F32)


def _sigmoid(x):
    return 1.0 / (1.0 + jnp.exp(-x))


def _softplus(x):
    return jnp.maximum(x, 0.0) + jnp.log(1.0 + jnp.exp(-jnp.abs(x)))


def _gelu(x):
    return 0.5 * x * (1.0 + jnp.tanh(math.sqrt(2.0 / math.pi) * (x + 0.044715 * (x * x * x))))


def _full(shape):
    n = len(shape)
    return pl.BlockSpec(shape, lambda *_: (0,) * n)


def _norm_proj_kernel(*refs, n_out):
    x_ref, g_ref = refs[0], refs[1]
    w_refs = refs[2:2 + n_out]
    o_refs = refs[2 + n_out:]
    h = _rms(x_ref[...], g_ref[...]).astype(BF16)
    for w_ref, o_ref in zip(w_refs, o_refs):
        o_ref[...] = jnp.dot(h, w_ref[...], preferred_element_type=F32).astype(o_ref.dtype)


def _norm_proj(x2, g, ws, out_dtypes, block):
    t, d = x2.shape
    block = min(block, t)
    in_specs = [pl.BlockSpec((block, d), lambda i: (i, 0)), _full((1, d))]
    in_specs += [_full(w.shape) for w in ws]
    out_specs = [pl.BlockSpec((block, w.shape[1]), lambda i: (i, 0)) for w in ws]
    out_shape = [jax.ShapeDtypeStruct((t, w.shape[1]), dt) for w, dt in zip(ws, out_dtypes)]
    return pl.pallas_call(
        functools.partial(_norm_proj_kernel, n_out=len(ws)),
        grid=(t // block,), in_specs=in_specs, out_specs=out_specs, out_shape=out_shape,
        compiler_params=_params(("parallel",)),
    )(x2, g.reshape(1, d), *ws)


def _seg_sum(x, mseg):
    hi = x.astype(BF16)
    lo = (x - hi.astype(F32)).astype(BF16)
    return (jnp.dot(hi, mseg, preferred_element_type=F32)
            + jnp.dot(lo, mseg, preferred_element_type=F32))


def _col_bcast(row):
    return jnp.broadcast_to(row, (LANES, LANES)).T


def _rwkv_kernel(*refs, first_layer):
    if first_layer:
        (z_ref, mu_ref, w0_ref, w2_ref, a0_ref, a2_ref, g2_ref, kk_ref, ka_ref, rk_ref,
         lng_ref, lnb_ref, mseg_ref, y_ref, vf_out_ref, zprev, hst) = refs
    else:
        (z_ref, hv_ref, vf_ref, v0_ref, v2_ref, mu_ref, w0_ref, w2_ref, a0_ref, a2_ref, g2_ref,
         kk_ref, ka_ref, rk_ref, lng_ref, lnb_ref, mseg_ref, y_ref, zprev, hst) = refs
    L = RW_CHUNK

    @pl.when(pl.program_id(1) == 0)
    def _():
        zprev[...] = jnp.zeros_like(zprev)
        hst[...] = jnp.zeros_like(hst)

    z = z_ref[0]
    row = lax.broadcasted_iota(jnp.int32, (L, 1), 0)
    zs = jnp.where(row == 0, zprev[...], pltpu.roll(z, 1, 0))
    zprev[...] = z[L - 1:L, :]
    z = z + (zs - z) * mu_ref[...]
    r = z[:, 0:512]
    k = z[:, 512:1024]
    v = z[:, 1024:1536]
    wa = z[:, 1536:1664]
    gd = z[:, 1664:1792]
    mseg = mseg_ref[...]

    wlin = w0_ref[...] + _bdot(jnp.tanh(wa), w2_ref[...])
    lw = -jnp.exp(-_softplus(-wlin) - 0.5)
    a = _sigmoid(a0_ref[...] + _bdot(wa, a2_ref[...]))
    g = _bdot(_sigmoid(gd), g2_ref[...])
    if first_layer:
        vf_out_ref[0] = v
    else:
        v = v + (vf_ref[0] - v) * _sigmoid(v0_ref[...] + _bdot(hv_ref[0], v2_ref[...]))
    kk = k * kk_ref[...]
    kk = kk / jnp.maximum(jnp.sqrt(_seg_sum(kk * kk, mseg)), 1e-12)
    k2 = k * (1.0 + (a - 1.0) * ka_ref[...])
    av = -kk
    bv = kk * a

    ti = lax.broadcasted_iota(jnp.int32, (L, L), 0)
    si = lax.broadcasted_iota(jnp.int32, (L, L), 1)
    tril = (ti >= si).astype(F32)
    cum = jnp.dot(tril, lw, preferred_element_type=F32, precision=lax.Precision.HIGHEST)
    mid = cum[L // 2 - 1:L // 2, :]
    cm = cum - mid
    ecum = jnp.exp(cm)
    einv = jnp.exp(-cm)
    rt = r * ecum
    kt = k2 * einv
    bt = bv * einv
    at = av * jnp.exp(cm - lw)
    p_end = jnp.exp(cum[L - 1:L, :])
    e_end = ecum[L - 1:L, :]
    e_mid = jnp.exp(mid)

    lane = lax.broadcasted_iota(jnp.int32, (1, LANES), 1)
    m0 = (lane < RW_HEAD).astype(F32)
    m1 = 1.0 - m0
    strict = ti > si
    incl = ti >= si
    bi = lax.broadcasted_iota(jnp.int32, (LANES, LANES), 0) < RW_HEAD
    bj = lax.broadcasted_iota(jnp.int32, (LANES, LANES), 1) < RW_HEAD
    bdmask = (bi == bj).astype(F32)
    zeros_ll = jnp.zeros((L, L), F32)

    ys = []
    for p in range(D_RWKV // LANES):
        sl = slice(LANES * p, LANES * (p + 1))
        A, B, K, R, V = at[:, sl], bt[:, sl], kt[:, sl], rt[:, sl], v[:, sl]
        btkt = jnp.concatenate([B.T, K.T], axis=1)
        h0 = hst[p]
        h0m = h0 * _col_bcast(e_mid[:, sl])
        sc = _bdot(jnp.concatenate([A * m0, A * m1, R * m0, R * m1], axis=0), btkt)
        aab = [jnp.where(strict, sc[e * L:(e + 1) * L, 0:L], 0.0) for e in range(2)]
        aak = [jnp.where(strict, sc[e * L:(e + 1) * L, L:2 * L], 0.0) for e in range(2)]
        arb = [jnp.where(incl, sc[(2 + e) * L:(3 + e) * L, 0:L], 0.0) for e in range(2)]
        ark = [jnp.where(incl, sc[(2 + e) * L:(3 + e) * L, L:2 * L], 0.0) for e in range(2)]
        arh = _bdot(jnp.concatenate([A, R], axis=0), h0m)
        v01 = jnp.concatenate([V * m0, V * m1], axis=0)
        x = arh[0:L] + _bdot(jnp.concatenate(aak, axis=1), v01)
        pm = jnp.concatenate(aab, axis=1)
        n_fac = int(math.log2(L))
        for it in range(n_fac):
            x = x + _bdot(pm, jnp.concatenate([x * m0, x * m1], axis=0))
            if it + 1 < n_fac:
                pd = jnp.concatenate(
                    [jnp.concatenate([pm[:, 0:L], zeros_ll], axis=1),
                     jnp.concatenate([zeros_ll, pm[:, L:2 * L]], axis=1)], axis=0)
                pm = _bdot(pm, pd)
        u = x
        yp = arh[L:2 * L] + _bdot(jnp.concatenate(arb + ark, axis=1),
                                  jnp.concatenate([u * m0, u * m1, v01], axis=0))
        upd = _bdot(btkt, jnp.concatenate([u, V], axis=0))
        hst[p] = (h0 * _col_bcast(p_end[:, sl]) + upd * _col_bcast(e_end[:, sl])) * bdmask
        ys.append(yp)
    y = jnp.concatenate(ys, axis=1)

    mean = _seg_sum(y, mseg) * (1.0 / RW_HEAD)
    d = y - mean
    var = _seg_sum(d * d, mseg) * (1.0 / RW_HEAD)
    yn = d * lax.rsqrt(var + GN_EPS) * lng_ref[...] + lnb_ref[...]
    bonus = _seg_sum(r * k2 * rk_ref[...], mseg) * v
    y_ref[0] = (yn + bonus) * g


def _pad_rows(w, rows, offset):
    out = jnp.zeros((rows, w.shape[1]), w.dtype)
    return out.at[offset:offset + w.shape[0]].set(w)


def _rwkv(z_rw, hv, v_first, prm, bsz, seq):
    L = RW_CHUNK
    first = v_first is None
    row = lambda a: a.reshape(1, -1).astype(F32)
    hid = jnp.arange(D_RWKV) // RW_HEAD
    mseg = (hid[:, None] == hid[None, :]).astype(BF16)
    w2p = _pad_rows(prm['w2'], LANES, 0).astype(BF16)
    a2p = _pad_rows(prm['a2'], LANES, 64).astype(BF16)
    common = [row(prm['mu']), row(prm['w0']), w2p, row(prm['a0']), a2p, prm['g2'].astype(BF16),
              row(prm['kk']), row(prm['ka']), row(prm['rk']), row(prm['lng']), row(prm['lnb']), mseg]
    tok = lambda w: pl.BlockSpec((1, L, w), lambda b, t: (b, t, 0))
    common_specs = [_full(c.shape) for c in common]
    y_shape = jax.ShapeDtypeStruct((bsz, seq, D_RWKV), F32)
    scratch = [pltpu.VMEM((1, RW_COLS), F32), pltpu.VMEM((D_RWKV // LANES, LANES, LANES), F32)]
    if first:
        args = [z_rw] + common
        in_specs = [tok(RW_COLS)] + common_specs
        out_shape = [y_shape, y_shape]
        out_specs = [tok(D_RWKV), tok(D_RWKV)]
    else:
        v2p = _pad_rows(prm['v2'], LANES, 0).astype(BF16)
        extra = [row(prm['v0']), v2p]
        args = [z_rw, hv, v_first] + extra + common
        in_specs = [tok(RW_COLS), tok(LANES), tok(D_RWKV)] + [_full(c.shape) for c in extra] + common_specs
        out_shape = [y_shape]
        out_specs = [tok(D_RWKV)]
    outs = pl.pallas_call(
        functools.partial(_rwkv_kernel, first_layer=first),
        grid=(bsz, seq // L), in_specs=in_specs, out_specs=out_specs, out_shape=out_shape,
        scratch_shapes=scratch, compiler_params=_params(("parallel", "arbitrary")),
    )(*args)
    return (outs[0], outs[1]) if first else (outs[0], v_first)


def _s5_kernel(u_ref, wb_ref, wc_ref, lpr_ref, lpi_ref, d_ref, gw_ref, gb_ref, og_ref, o_ref,
               car_re, car_im, xre, xim):
    tb = u_ref.shape[1]

    @pl.when(pl.program_id(1) == 0)
    def _():
        car_re[...] = jnp.zeros_like(car_re)
        car_im[...] = jnp.zeros_like(car_im)

    u = u_ref[0]
    bu = _bdot(u, wb_ref[...])
    xre[...] = bu[:, 0:S5_MODES]
    xim[...] = bu[:, S5_MODES:2 * S5_MODES]
    row = lax.broadcasted_iota(jnp.int32, (SUBLANES, 1), 0)

    def tile(i, carry):
        cr, ci = carry
        rs = pl.ds(pl.multiple_of(i * SUBLANES, SUBLANES), SUBLANES)
        br, bi = xre[rs, :], xim[rs, :]
        for dist in (1, 2, 4):
            keep = row >= dist
            sr = jnp.where(keep, pltpu.roll(br, dist, 0), 0.0)
            si = jnp.where(keep, pltpu.roll(bi, dist, 0), 0.0)
            lr = lpr_ref[dist - 1:dist, :]
            li = lpi_ref[dist - 1:dist, :]
            br, bi = br + lr * sr - li * si, bi + lr * si + li * sr
        pr, pi = lpr_ref[...], lpi_ref[...]
        xr = br + pr * cr - pi * ci
        xi = bi + pr * ci + pi * cr
        xre[rs, :] = xr
        xim[rs, :] = xi
        return xr[SUBLANES - 1:SUBLANES, :], xi[SUBLANES - 1:SUBLANES, :]

    cr, ci = lax.fori_loop(0, tb // SUBLANES, tile, (car_re[...], car_im[...]))
    car_re[...] = cr
    car_im[...] = ci
    wc = wc_ref[...]
    y = _bdot(xre[...], wc[0:S5_MODES]) + _bdot(xim[...], wc[S5_MODES:2 * S5_MODES])
    y = _gelu(y + d_ref[...] * u)
    y = y * _sigmoid(_bdot(y, gw_ref[...]) + gb_ref[...])
    o_ref[0] = _rms(y, og_ref[...])


def _s5_weights(a_re, a_im, log_dt, b_re, b_im, c_re, c_im):
    lam_re = jnp.minimum(a_re.astype(F32), -1e-4)
    lam_im = a_im.astype(F32)
    dt = jnp.exp(log_dt.astype(F32))[:, None]
    mag = jnp.exp(lam_re * dt)
    lb_re = mag * jnp.cos(lam_im * dt)
    lb_im = mag * jnp.sin(lam_im * dt)
    den = lam_re * lam_re + lam_im * lam_im
    c1_re = ((lb_re - 1.0) * lam_re + lb_im * lam_im) / den
    c1_im = (lb_im * lam_re - (lb_re - 1.0) * lam_im) / den
    br, bi = b_re.astype(F32), b_im.astype(F32)
    bb_re = c1_re[..., None] * br - c1_im[..., None] * bi
    bb_im = c1_re[..., None] * bi + c1_im[..., None] * br
    eye = jnp.eye(S5_GROUPS, dtype=F32)
    wb_re = jnp.einsum('gpc,gh->gchp', bb_re, eye).reshape(D_S5, S5_MODES)
    wb_im = jnp.einsum('gpc,gh->gchp', bb_im, eye).reshape(D_S5, S5_MODES)
    wb = jnp.concatenate([wb_re, wb_im], axis=1).astype(BF16)
    wc_re = jnp.einsum('gcp,gh->gphc', c_re.astype(F32), eye).reshape(S5_MODES, D_S5)
    wc_im = jnp.einsum('gcp,gh->gphc', c_im.astype(F32), eye).reshape(S5_MODES, D_S5)
    wc = jnp.concatenate([wc_re, -wc_im], axis=0).astype(BF16)
    pr, pi = [lb_re], [lb_im]
    for _ in range(SUBLANES - 1):
        pr, pi = pr + [pr[-1] * lb_re - pi[-1] * lb_im], pi + [pr[-1] * lb_im + pi[-1] * lb_re]
    lp_re = jnp.stack(pr).reshape(SUBLANES, S5_MODES)
    lp_im = jnp.stack(pi).reshape(SUBLANES, S5_MODES)
    return wb, wc, lp_re, lp_im


def _s5(u, prm, bsz, seq):
    tb = min(S5_BLOCK, seq)
    wb, wc, lp_re, lp_im = _s5_weights(prm['a_re'], prm['a_im'], prm['log_dt'], prm['b_re'], prm['b_im'],
                                       prm['c_re'], prm['c_im'])
    row = lambda a: a.reshape(1, -1).astype(F32)
    consts = [wb, wc, lp_re, lp_im, row(prm['d']), prm['glu_w'].astype(BF16), row(prm['glu_b']),
              row(prm['out_g'])]
    tok = pl.BlockSpec((1, tb, D_S5), lambda b, t: (b, t, 0))
    return pl.pallas_call(
        _s5_kernel, grid=(bsz, seq // tb),
        in_specs=[tok] + [_full(c.shape) for c in consts], out_specs=tok,
        out_shape=jax.ShapeDtypeStruct((bsz, seq, D_S5), F32),
        scratch_shapes=[pltpu.VMEM((1, S5_MODES), F32), pltpu.VMEM((1, S5_MODES), F32),
                        pltpu.VMEM((tb, S5_MODES), F32), pltpu.VMEM((tb, S5_MODES), F32)],
        compiler_params=_params(("parallel", "arbitrary")),
    )(u, *consts)


def _mix_xattn_kernel(x_ref, yr_ref, ys_ref, wo1_ref, wo2_ref, g_ref, wq_ref, k_ref, v_ref, wo_ref, o_ref):
    x1 = x_ref[0] + _bdot(yr_ref[0], wo1_ref[...]) + _bdot(ys_ref[0], wo2_ref[...])
    h = _rms(x1, g_ref[...])
    q = _bdot(h, wq_ref[...])
    km, vm = k_ref[0], v_ref[0]
    outs = []
    for hd in range(XA_HEADS):
        sl = slice(XA_HEAD * hd, XA_HEAD * (hd + 1))
        s = _bdot_nt(q[:, sl], km[:, sl]) * (XA_HEAD ** -0.5)
        s = s - jnp.max(s, axis=-1, keepdims=True)
        e = jnp.exp(s)
        p = e / jnp.sum(e, axis=-1, keepdims=True)
        outs.append(_bdot(p, vm[:, sl]))
    o = jnp.concatenate(outs, axis=1)
    o_ref[0] = x1 + _bdot(o, wo_ref[...])


def _mix_xattn(x, y_rw, y_s5, w_out, g, wq, kmem, vmem, wo, bsz, seq):
    tm = min(XA_BLOCK, seq)
    consts_a = [w_out[:D_RWKV].astype(BF16), w_out[D_RWKV:].astype(BF16), g.reshape(1, -1).astype(F32),
                wq.astype(BF16)]
    tok = lambda w: pl.BlockSpec((1, tm, w), lambda b, t: (b, t, 0))
    mem = pl.BlockSpec((1, N_MEM, D_MODEL), lambda b, t: (b, 0, 0))
    wo_b = wo.astype(BF16)
    return pl.pallas_call(
        _mix_xattn_kernel, grid=(bsz, seq // tm),
        in_specs=[tok(D_MODEL), tok(D_RWKV), tok(D_S5)] + [_full(c.shape) for c in consts_a]
                 + [mem, mem, _full(wo_b.shape)],
        out_specs=tok(D_MODEL), out_shape=jax.ShapeDtypeStruct((bsz, seq, D_MODEL), F32),
        compiler_params=_params(("parallel", "parallel")),
    )(x, y_rw, y_s5, *consts_a, kmem, vmem, wo_b)


def _top_rows(work, order, aux, val_ref, idx_ref):
    for it in range(PEER_TOPK):
        m = jnp.max(work, axis=0, keepdims=True)
        pos = jnp.min(jnp.where(work == m, order, jnp.inf), axis=0, keepdims=True)
        hit = order == pos
        val_ref[it:it + 1, :] = m
        if aux is None:
            idx_ref[it:it + 1, :] = pos
        else:
            idx_ref[it:it + 1, :] = jnp.sum(jnp.where(hit, aux, 0.0), axis=0, keepdims=True)
        work = jnp.where(hit, -jnp.inf, work)


_CAND_ROW_BLOCKS = [(0, PEER_TOPK), (1, SUBLANES), (2, SUBLANES), (3, SUBLANES)]
_CAND_COL_BLOCKS = [(0, PEER_TOPK, 4, 15), (1, SUBLANES, 4, 7), (2, SUBLANES, 4, 4)]
N_CAND = sum(n for _, n in _CAND_ROW_BLOCKS) + sum(n for _, n, _, _ in _CAND_COL_BLOCKS)


def _cand_consts(tm):
    flat, neg = [], []
    for a, nb in _CAND_ROW_BLOCKS:
        flat += [a * PEER_TOPK + b for b in range(nb)]
        neg += [0.0] * nb
    for b, na, lo, hi in _CAND_COL_BLOCKS:
        flat += [a * PEER_TOPK + b for a in range(na)]
        neg += [0.0 if lo <= a <= hi else -float('inf') for a in range(na)]
    flat = [f if n == 0.0 else 1000.0 + i for i, (f, n) in enumerate(zip(flat, neg))]
    col = lambda v: jnp.broadcast_to(jnp.asarray(v, F32)[:, None], (N_CAND, tm))
    return col(flat), col(neg)


def _cand_rows(row_vals, col_vals, combine):
    blocks = [combine(row_vals[a:a + 1, :], col_vals[0:nb, :]) for a, nb in _CAND_ROW_BLOCKS]
    blocks += [combine(row_vals[0:na, :], col_vals[b:b + 1, :]) for b, na, _, _ in _CAND_COL_BLOCKS]
    return jnp.concatenate(blocks, axis=0)


SEL_HEADS_PER_STEP = 2


def _peer_select_kernel(x_ref, g_ref, wq_ref, keys_ref, cflat_ref, cneg_ref, x8_ref, h8_ref, base_ref, shift_ref,
                        gate_ref, q3, idx_t, gate_t, *lists):
    tm = x_ref.shape[0]
    n = SEL_HEADS_PER_STEP
    s1, i1, s2, i2, top, eid = (lists[j * n:(j + 1) * n] for j in range(6))
    x = x_ref[...]
    h = _rms(x, g_ref[...])
    for r in range(D_MODEL // LANES):
        rows = pl.ds(r, tm, stride=D_MODEL // LANES)
        h8_ref[rows, :] = h[:, LANES * r:LANES * (r + 1)]
        x8_ref[rows, :] = x[:, LANES * r:LANES * (r + 1)]
    q = _bdot(h, wq_ref[...])
    for j in range(2 * PEER_HEADS):
        q3[j] = q[:, LANES * j:LANES * (j + 1)]
    iota_k = lax.broadcasted_iota(jnp.int32, (PEER_NKEYS, tm), 0).astype(F32)

    def heads(step, _):
        for u in range(SEL_HEADS_PER_STEP):
            hd = step * SEL_HEADS_PER_STEP + u
            sc1 = _bdot_nt(keys_ref[2 * hd], q3[2 * hd])
            sc2 = _bdot_nt(keys_ref[2 * hd + 1], q3[2 * hd + 1])
            _top_rows(sc1, iota_k, None, s1[u], i1[u])
            _top_rows(sc2, iota_k, None, s2[u], i2[u])
            cand = _cand_rows(s1[u][...], s2[u][...], lambda x, y: x + y) + cneg_ref[...]
            cidx = _cand_rows(i1[u][...], i2[u][...], lambda x, y: x * float(PEER_NKEYS) + y)
            _top_rows(cand, cflat_ref[...], cidx, top[u], eid[u])
            tv = top[u][...]
            e = jnp.exp(tv - jnp.max(tv, axis=0, keepdims=True))
            rs = pl.ds(pl.multiple_of(hd * PEER_TOPK, PEER_TOPK), PEER_TOPK)
            idx_t[rs, :] = eid[u][...]
            gate_t[rs, :] = e / jnp.sum(e, axis=0, keepdims=True)
        return 0

    lax.fori_loop(0, PEER_HEADS // SEL_HEADS_PER_STEP, heads, 0)
    e_t = idx_t[...].T
    pair = jnp.floor(e_t * 0.5)
    base_ref[...] = (pair * float(SUBLANES)).astype(jnp.int32)
    shift_ref[...] = ((e_t - 2.0 * pair) * 16.0).astype(jnp.int32)
    gate_ref[...] = gate_t[...].T


def _peer_select(x2, g, wq, keys):
    t = x2.shape[0]
    tm = SEL_BLOCK
    keys_b = keys.reshape(2 * PEER_HEADS, PEER_NKEYS, LANES).astype(BF16)
    wq_b = wq.astype(BF16)
    cflat, cneg = _cand_consts(tm)
    tokspec = lambda w: pl.BlockSpec((tm, w), lambda i: (i, 0))
    vm = lambda r: pltpu.VMEM((r, tm), F32)
    return pl.pallas_call(
        _peer_select_kernel, grid=(t // tm,),
        in_specs=[tokspec(D_MODEL), _full((1, D_MODEL)), _full(wq_b.shape), _full(keys_b.shape),
                  _full(cflat.shape), _full(cneg.shape)],
        out_specs=[pl.BlockSpec((tm * SUBLANES, LANES), lambda i: (i, 0))] * 2
                  + [tokspec(PEER_SEL), tokspec(PEER_SEL), tokspec(PEER_SEL)],
        out_shape=[jax.ShapeDtypeStruct((t * SUBLANES, LANES), F32)] * 2 + [
                   jax.ShapeDtypeStruct((t, PEER_SEL), jnp.int32),
                   jax.ShapeDtypeStruct((t, PEER_SEL), jnp.int32), jax.ShapeDtypeStruct((t, PEER_SEL), F32)],
        scratch_shapes=[pltpu.VMEM((2 * PEER_HEADS, tm, LANES), F32), vm(PEER_SEL), vm(PEER_SEL)]
                       + [vm(PEER_TOPK) for _ in range(6 * SEL_HEADS_PER_STEP)],
        compiler_params=_params(("parallel",)),
    )(x2, g.reshape(1, -1).astype(F32), wq_b, keys_b, cflat, cneg)


def _pack_table(tab):
    n, d = tab.shape
    bits = lax.bitcast_convert_type(tab.astype(BF16), jnp.uint16).astype(U32)
    bits = bits.reshape(n // 2, 2, d)
    packed = (bits[:, 0, :] << 16) | bits[:, 1, :]
    return lax.bitcast_convert_type(packed, jnp.int32).reshape(n // 2 * SUBLANES, LANES)


def _splat_into(src_ref, t, dst_ref, slot):
    tile = jnp.broadcast_to(src_ref[pl.ds(t, 1), :], (LANES, LANES)).T
    dst_ref[LANES * slot:LANES * (slot + 1), :] = tile


def _bcast_row(ref, row):
    return jnp.broadcast_to(ref[row:row + 1, :], (SUBLANES, LANES))


def _expert_tile(tab_ref, base, shift_splat, row):
    w = tab_ref[pl.ds(pl.multiple_of(base, SUBLANES), SUBLANES), :]
    return lax.bitcast_convert_type((w << _bcast_row(shift_splat, row)) & jnp.int32(-65536), F32)


def _token_rows(t):
    return pl.ds(pl.multiple_of(t * SUBLANES, SUBLANES), SUBLANES)


def _pipelined_tokens(tg, prep, work):
    prep(0, 0)

    def body(i, _):
        t0 = 2 * i
        prep(t0 + 1, 1)
        work(t0, 0)
        prep(jnp.minimum(t0 + 2, tg - 1), 0)
        work(t0 + 1, 1)
        return 0

    lax.fori_loop(0, tg // 2, body, 0)


def _table_spec(shape):
    return pl.BlockSpec(shape, lambda i: (0, 0), pipeline_mode=pl.Buffered(1))


N_CHUNK = D_MODEL // LANES
CHUNK_STRIDE = PEER_SEL + SUBLANES


def _peer_u_kernel(base_ref, shift_ref, h_ref, gate_ref, tab_ref, eye_ref, c_ref, shift_splat, *planes):
    tg = gate_ref.shape[0]
    ones = jnp.ones((LANES, LANES), BF16)

    half = N_CHUNK // 2

    def prep(t, slot):
        _splat_into(shift_ref, t, shift_splat, slot)

    def gather(t, slot):
        ht = h_ref[_token_rows(t), :]
        for kx in range(PEER_SEL):
            prod = _expert_tile(tab_ref, base_ref[t, kx], shift_splat, LANES * slot + kx) * ht
            fold = prod + pltpu.roll(prod, half, 0)
            planes[slot][pl.ds(kx, half, stride=CHUNK_STRIDE), :] = fold[0:half, :]

    def finish(t, slot):
        plane = planes[slot]
        acc = plane[0:PEER_SEL, :]
        for r in range(1, half):
            acc = acc + plane[CHUNK_STRIDE * r:CHUNK_STRIDE * r + PEER_SEL, :]
        hi = acc.astype(BF16)
        lo = (acc - hi.astype(F32)).astype(BF16)
        tot = jnp.dot(hi, ones, preferred_element_type=F32) + jnp.dot(lo, ones, preferred_element_type=F32)
        score = jnp.sum(tot * eye_ref[...], axis=0, keepdims=True)
        c_ref[pl.ds(t, 1), :] = gate_ref[pl.ds(t, 1), :] * _gelu(score)

    planes[1][...] = jnp.zeros_like(planes[1])
    prep(0, 0)

    def body(i, _):
        t0 = 2 * i
        prep(t0 + 1, 1)
        gather(t0, 0)
        finish(jnp.maximum(t0 - 1, 0), 1)
        prep(jnp.minimum(t0 + 2, tg - 1), 0)
        gather(t0 + 1, 1)
        finish(t0, 0)
        return 0

    lax.fori_loop(0, tg // 2, body, 0)
    finish(tg - 1, 1)


def _peer_u(base, shift, h8, gate, tab):
    t = base.shape[0]
    tg = GATHER_BLOCK
    eye = jnp.eye(LANES, dtype=F32)
    tokrow = pl.BlockSpec((tg, PEER_SEL), lambda i: (i, 0))
    return pl.pallas_call(
        _peer_u_kernel, grid=(t // tg,),
        in_specs=[pl.BlockSpec((tg, PEER_SEL), lambda i: (i, 0), memory_space=pltpu.SMEM), tokrow,
                  pl.BlockSpec((tg * SUBLANES, LANES), lambda i: (i, 0)), tokrow,
                  _table_spec(tab.shape), _full(eye.shape)],
        out_specs=tokrow, out_shape=jax.ShapeDtypeStruct((t, PEER_SEL), F32),
        scratch_shapes=[pltpu.VMEM((2 * LANES, LANES), jnp.int32)]
                       + [pltpu.VMEM((N_CHUNK // 2 * CHUNK_STRIDE, LANES), F32)] * 2,
        compiler_params=_params(("parallel",)),
    )(base, shift, h8, gate, tab, eye)


def _peer_v_kernel(base_ref, shift_ref, c_ref, x_ref, tab_ref, o_ref, shift_splat, c_splat, otile):
    tg = c_ref.shape[0]
    n_acc = 4

    def prep(t, slot):
        _splat_into(shift_ref, t, shift_splat, slot)
        _splat_into(c_ref, t, c_splat, slot)

    def work(t, slot):
        accs = [jnp.zeros((SUBLANES, LANES), F32) for _ in range(n_acc)]
        for kx in range(PEER_SEL):
            row = LANES * slot + kx
            accs[kx % n_acc] = accs[kx % n_acc] + (_bcast_row(c_splat, row)
                                                   * _expert_tile(tab_ref, base_ref[t, kx], shift_splat, row))
        rs = _token_rows(t)
        otile[rs, :] = x_ref[rs, :] + ((accs[0] + accs[1]) + (accs[2] + accs[3]))

    _pipelined_tokens(tg, prep, work)
    for r in range(N_CHUNK):
        o_ref[:, LANES * r:LANES * (r + 1)] = otile[pl.ds(r, tg, stride=N_CHUNK), :]


def _peer_v(base, shift, c, x8, tab):
    t = base.shape[0]
    tg = GATHER_BLOCK
    smem = pl.BlockSpec((tg, PEER_SEL), lambda i: (i, 0), memory_space=pltpu.SMEM)
    tokrow = pl.BlockSpec((tg, PEER_SEL), lambda i: (i, 0))
    tile = pl.BlockSpec((tg * SUBLANES, LANES), lambda i: (i, 0))
    return pl.pallas_call(
        _peer_v_kernel, grid=(t // tg,),
        in_specs=[smem, tokrow, tokrow, tile, _table_spec(tab.shape)],
        out_specs=pl.BlockSpec((tg, D_MODEL), lambda i: (i, 0)),
        out_shape=jax.ShapeDtypeStruct((t, D_MODEL), F32),
        scratch_shapes=[pltpu.VMEM((2 * LANES, LANES), jnp.int32), pltpu.VMEM((2 * LANES, LANES), F32),
                        pltpu.VMEM((tg * SUBLANES, LANES), F32)],
        compiler_params=_params(("parallel",)),
    )(base, shift, c, x8, tab)


SC_WORKERS = 32
SC_LANES = 16
SC_ROWS = 32
SC_TOKENS = 13824
SC_GROUP = 8
SC_REGS = 32


def _peer_sc(eidx, gate, h8, x8, tab_u, tab_v):
    ts = eidx.shape[0]
    per_w = ts // SC_WORKERS
    n_chunk = PEER_SEL // SC_ROWS
    grp = SC_GROUP
    sel = grp * PEER_SEL
    n_pairs = grp * n_chunk // 2
    per_tile = LANES // SC_LANES
    n_blk = D_MODEL // (SC_REGS * SC_LANES)
    mesh = plsc.VectorSubcoreMesh(core_axis_name="c", subcore_axis_name="s")

    def piece(ref, lead, q):
        return ref[lead, q // per_tile, pl.ds(SC_LANES * (q % per_tile), SC_LANES)]

    def body(u_hbm, v_hbm, idx_hbm, gate_hbm, h_hbm, x_hbm, out_hbm,
             idx_v, c_v, h_v, acc_v, part_v, rows0, rows1, sem0, sem1):
        wid = lax.axis_index("s") * 2 + lax.axis_index("c")
        bufs, sems = (rows0, rows1), (sem0, sem1)
        lane = lax.iota(jnp.int32, SC_LANES)

        def stream(tab_hbm, compute):
            def gather(chunk, slot):
                start = pl.multiple_of(chunk * SC_ROWS, SC_ROWS)
                return pltpu.make_async_copy(tab_hbm.at[idx_v.at[pl.ds(start, SC_ROWS)]], bufs[slot], sems[slot])

            gather(0, 0).start()

            @pl.loop(0, n_pairs)
            def _(p):
                c0 = 2 * p
                gather(c0 + 1, 1).start()
                gather(c0, 0).wait()
                compute(c0, 0)
                gather(jnp.minimum(c0 + 2, 2 * n_pairs - 1), 0).start()
                gather(c0 + 1, 1).wait()
                compute(c0 + 1, 1)

            gather(0, 0).wait()

        def dots(chunk, slot):
            tok = chunk // n_chunk
            for blk in range(n_blk):
                where = [blk * SC_REGS + j for j in range(SC_REGS)]
                hs = [piece(h_v, tok, q) for q in where]

                def row(k, carry):
                    parts = [None] * 4
                    for i, q in enumerate(where):
                        term = piece(bufs[slot], k, q) * hs[i]
                        parts[i % 4] = term if parts[i % 4] is None else parts[i % 4] + term
                    tot = (parts[0] + parts[1]) + (parts[2] + parts[3])
                    at = pl.ds(pl.multiple_of((chunk * SC_ROWS + k) * SC_LANES, SC_LANES), SC_LANES)
                    if blk == 0:
                        part_v[at] = tot
                    else:
                        part_v[at] = part_v[at] + tot
                    return carry

                lax.fori_loop(0, SC_ROWS, row, 0)

        def weigh(chunk, slot):
            tok = chunk // n_chunk
            for blk in range(n_blk):
                where = [blk * SC_REGS + j for j in range(SC_REGS)]

                def row(k, accs):
                    ck = plsc.load_gather(c_v, [jnp.full((SC_LANES,), chunk * SC_ROWS, jnp.int32) + k])
                    return tuple(a + ck * piece(bufs[slot], k, q) for a, q in zip(accs, where))

                accs = lax.fori_loop(0, SC_ROWS, row, tuple(piece(acc_v, tok, q) for q in where))
                for a, q in zip(accs, where):
                    acc_v[tok, q // per_tile, pl.ds(SC_LANES * (q % per_tile), SC_LANES)] = a

        @pl.loop(0, per_w // grp)
        def _(g):
            t0 = wid * per_w + g * grp
            flat = pl.ds(pl.multiple_of(t0 * PEER_SEL, sel), sel)
            pltpu.sync_copy(idx_hbm.at[flat], idx_v)
            pltpu.sync_copy(gate_hbm.at[flat], c_v)
            pltpu.sync_copy(h_hbm.at[pl.ds(t0, grp)], h_v)
            pltpu.sync_copy(x_hbm.at[pl.ds(t0, grp)], acc_v)
            stream(u_hbm, dots)

            @pl.loop(0, sel // SC_LANES)
            def _(m):
                base = (m * SC_LANES + lane) * SC_LANES
                score = plsc.load_gather(part_v, [base])
                for l in range(1, SC_LANES):
                    score = score + plsc.load_gather(part_v, [base + l])
                z = math.sqrt(2.0 / math.pi) * (score + 0.044715 * (score * score * score))
                tanh_z = 1.0 - 2.0 / (jnp.exp(2.0 * z) + 1.0)
                at = pl.ds(pl.multiple_of(m * SC_LANES, SC_LANES), SC_LANES)
                c_v[at] = c_v[at] * (0.5 * score * (1.0 + tanh_z))

            stream(v_hbm, weigh)
            pltpu.sync_copy(acc_v, out_hbm.at[pl.ds(t0, grp)])

    tile = lambda n: pltpu.VMEM((n, N_CHUNK, LANES), F32)
    out = pl.kernel(
        body, mesh=mesh, out_type=jax.ShapeDtypeStruct((ts, N_CHUNK, LANES), F32),
        scratch_types=[pltpu.VMEM((sel,), jnp.int32), pltpu.VMEM((sel,), F32), tile(grp), tile(grp),
                       pltpu.VMEM((sel * SC_LANES,), F32), tile(SC_ROWS), tile(SC_ROWS),
                       pltpu.SemaphoreType.DMA, pltpu.SemaphoreType.DMA],
        compiler_params=pltpu.CompilerParams(needs_layout_passes=False),
    )(tab_u.reshape(-1, N_CHUNK, LANES), tab_v.reshape(-1, N_CHUNK, LANES), eidx.reshape(-1), gate.reshape(-1),
      h8.reshape(ts, N_CHUNK, LANES), x8.reshape(ts, N_CHUNK, LANES))
    return out.reshape(ts, D_MODEL)


def _final_norm_kernel(x_ref, g_ref, o_ref):
    o_ref[...] = _rms(x_ref[...], g_ref[...])


def _final_norm(x2, g):
    t, d = x2.shape
    tm = min(PROJ_BLOCK, t)
    spec = pl.BlockSpec((tm, d), lambda i: (i, 0))
    return pl.pallas_call(
        _final_norm_kernel, grid=(t // tm,), in_specs=[spec, _full((1, d))], out_specs=spec,
        out_shape=jax.ShapeDtypeStruct((t, d), F32), compiler_params=_params(("parallel",)),
    )(x2, g.reshape(1, d).astype(F32))


def kernel(x, mem, norm_mix, w_in, rw_mu, rw_w0, rw_w2, rw_a0, rw_a2, rw_g2, rw_kk, rw_ka, rw_rk, rw_v0, rw_v1, rw_v2, rw_lnx_g, rw_lnx_b, s5_a_re, s5_a_im, s5_log_dt, s5_b_re, s5_b_im, s5_c_re, s5_c_im, s5_d, s5_glu_w, s5_glu_b, s5_out_g, w_out, norm_xa, norm_mem, xa_wq, xa_wk, xa_wv, xa_wo, norm_ffn, peer_wq, peer_keys, peer_u, peer_v, norm_final):
    bsz, seq, d = x.shape
    t = bsz * seq
    depth = w_in.shape[0]
    mem2 = mem.reshape(bsz * N_MEM, d)
    v_first = None
    for l in range(depth):
        x2 = x.reshape(t, d)
        w_rw = w_in[l][:, :RW_COLS].astype(BF16)
        w_s5 = w_in[l][:, RW_COLS:].astype(BF16)
        ws = [w_rw, w_s5]
        if l > 0:
            ws.append(_pad_rows(rw_v1[l - 1].T, LANES, 0).T.astype(BF16))
        outs = _norm_proj(x2, norm_mix[l], ws, [F32] * len(ws), PROJ_BLOCK)
        z_rw = outs[0].reshape(bsz, seq, RW_COLS)
        u_s5 = outs[1].reshape(bsz, seq, D_S5)
        hv = outs[2].reshape(bsz, seq, LANES) if l > 0 else None
        rw_prm = dict(mu=rw_mu[l], w0=rw_w0[l], w2=rw_w2[l], a0=rw_a0[l], a2=rw_a2[l], g2=rw_g2[l],
                      kk=rw_kk[l], ka=rw_ka[l], rk=rw_rk[l], lng=rw_lnx_g[l], lnb=rw_lnx_b[l])
        if l > 0:
            rw_prm.update(v0=rw_v0[l - 1], v2=rw_v2[l - 1])
        y_rw, v_first = _rwkv(z_rw, hv, v_first, rw_prm, bsz, seq)
        s5_prm = dict(a_re=s5_a_re[l], a_im=s5_a_im[l], log_dt=s5_log_dt[l], b_re=s5_b_re[l], b_im=s5_b_im[l],
                      c_re=s5_c_re[l], c_im=s5_c_im[l], d=s5_d[l], glu_w=s5_glu_w[l], glu_b=s5_glu_b[l],
                      out_g=s5_out_g[l])
        y_s5 = _s5(u_s5, s5_prm, bsz, seq)
        kv = _norm_proj(mem2, norm_mem[l], [xa_wk[l].astype(BF16), xa_wv[l].astype(BF16)], [BF16, BF16],
                        PROJ_BLOCK)
        kmem = kv[0].reshape(bsz, N_MEM, d)
        vmem = kv[1].reshape(bsz, N_MEM, d)
        x = _mix_xattn(x, y_rw, y_s5, w_out[l], norm_xa[l], xa_wq[l], kmem, vmem, xa_wo[l], bsz, seq)
        x2 = x.reshape(t, d)
        x8, h8, base, shift, gate = _peer_select(x2, norm_ffn[l], peer_wq[l], peer_keys[l])
        t_tc = t - SC_TOKENS
        r_tc = t_tc * SUBLANES
        eidx_sc = (base[t_tc:] >> 2) + (shift[t_tc:] >> 4)
        x_sc = _peer_sc(eidx_sc, gate[t_tc:], h8[r_tc:], x8[r_tc:], peer_u[l], peer_v[l])
        c = _peer_u(base[:t_tc], shift[:t_tc], h8[:r_tc], gate[:t_tc], _pack_table(peer_u[l]))
        x_tc = _peer_v(base[:t_tc], shift[:t_tc], c, x8[:r_tc], _pack_table(peer_v[l]))
        x = jnp.concatenate([x_tc, x_sc], axis=0).reshape(bsz, seq, d)
    return _final_norm(x.reshape(t, d), norm_final).reshape(bsz, seq, d)
```

```python
import functools
import math

import jax
import jax.numpy as jnp
from jax import lax
from jax.experimental import pallas as pl
from jax.experimental.pallas import tpu as pltpu
from jax.experimental.pallas import tpu_sc as plsc

F32 = jnp.float32
BF16 = jnp.bfloat16
U32 = jnp.uint32

LANES = 128
SUBLANES = 8
VMEM_LIMIT = 56 * 1024 * 1024

D_MODEL = 1024
D_RWKV = 512
RW_HEAD = 64
RW_COLS = 1792
D_S5 = 512
S5_GROUPS = 32
S5_CH = 16
S5_STATE = 64
S5_MODES = S5_GROUPS * S5_STATE
N_MEM = 256
XA_HEADS = 4
XA_HEAD = 256
PEER_HEADS = 8
PEER_NKEYS = 128
PEER_TOPK = 16
PEER_SEL = PEER_HEADS * PEER_TOPK
RMS_EPS = 1e-6
GN_EPS = 64e-5

RW_CHUNK = 128
S5_BLOCK = 256
PROJ_BLOCK = 512
XA_BLOCK = 256
SEL_BLOCK = 128
GATHER_BLOCK = 64


def _params(sem):
    return pltpu.CompilerParams(dimension_semantics=sem, vmem_limit_bytes=VMEM_LIMIT)


def _rms(x, g):
    ms = jnp.mean(x * x, axis=-1, keepdims=True)
    return x * lax.rsqrt(ms + RMS_EPS) * g


def _bdot(a, b):
    return jnp.dot(a.astype(BF16), b.astype(BF16), preferred_element_type=F32)


def _bdot_nt(a, b):
    return lax.dot_general(a.astype(BF16), b.astype(BF16), (((1,), (1,)), ((), ())),
                           preferred_element_type=F32)


def _sigmoid(x):
    return 1.0 / (1.0 + jnp.exp(-x))


def _softplus(x):
    return jnp.maximum(x, 0.0) + jnp.log(1.0 + jnp.exp(-jnp.abs(x)))


def _gelu(x):
    return 0.5 * x * (1.0 + jnp.tanh(math.sqrt(2.0 / math.pi) * (x + 0.044715 * (x * x * x))))


def _full(shape):
    n = len(shape)
    return pl.BlockSpec(shape, lambda *_: (0,) * n)


def _norm_proj_kernel(*refs, n_out):
    x_ref, g_ref = refs[0], refs[1]
    w_refs = refs[2:2 + n_out]
    o_refs = refs[2 + n_out:]
    h = _rms(x_ref[...], g_ref[...]).astype(BF16)
    for w_ref, o_ref in zip(w_refs, o_refs):
        o_ref[...] = jnp.dot(h, w_ref[...], preferred_element_type=F32).astype(o_ref.dtype)


def _norm_proj(x2, g, ws, out_dtypes, block):
    t, d = x2.shape
    block = min(block, t)
    in_specs = [pl.BlockSpec((block, d), lambda i: (i, 0)), _full((1, d))]
    in_specs += [_full(w.shape) for w in ws]
    out_specs = [pl.BlockSpec((block, w.shape[1]), lambda i: (i, 0)) for w in ws]
    out_shape = [jax.ShapeDtypeStruct((t, w.shape[1]), dt) for w, dt in zip(ws, out_dtypes)]
    return pl.pallas_call(
        functools.partial(_norm_proj_kernel, n_out=len(ws)),
        grid=(t // block,), in_specs=in_specs, out_specs=out_specs, out_shape=out_shape,
        compiler_params=_params(("parallel",)),
    )(x2, g.reshape(1, d), *ws)


def _seg_sum(x, mseg):
    hi = x.astype(BF16)
    lo = (x - hi.astype(F32)).astype(BF16)
    return (jnp.dot(hi, mseg, preferred_element_type=F32)
            + jnp.dot(lo, mseg, preferred_element_type=F32))


def _col_bcast(row):
    return jnp.broadcast_to(row, (LANES, LANES)).T


def _rwkv_kernel(*refs, first_layer):
    if first_layer:
        (z_ref, mu_ref, w0_ref, w2_ref, a0_ref, a2_ref, g2_ref, kk_ref, ka_ref, rk_ref,
         lng_ref, lnb_ref, mseg_ref, y_ref, vf_out_ref, zprev, hst) = refs
    else:
        (z_ref, hv_ref, vf_ref, v0_ref, v2_ref, mu_ref, w0_ref, w2_ref, a0_ref, a2_ref, g2_ref,
         kk_ref, ka_ref, rk_ref, lng_ref, lnb_ref, mseg_ref, y_ref, zprev, hst) = refs
    L = RW_CHUNK

    @pl.when(pl.program_id(1) == 0)
    def _():
        zprev[...] = jnp.zeros_like(zprev)
        hst[...] = jnp.zeros_like(hst)

    z = z_ref[0]
    row = lax.broadcasted_iota(jnp.int32, (L, 1), 0)
    zs = jnp.where(row == 0, zprev[...], pltpu.roll(z, 1, 0))
    zprev[...] = z[L - 1:L, :]
    z = z + (zs - z) * mu_ref[...]
    r = z[:, 0:512]
    k = z[:, 512:1024]
    v = z[:, 1024:1536]
    wa = z[:, 1536:1664]
    gd = z[:, 1664:1792]
    mseg = mseg_ref[...]

    wlin = w0_ref[...] + _bdot(jnp.tanh(wa), w2_ref[...])
    lw = -jnp.exp(-_softplus(-wlin) - 0.5)
    a = _sigmoid(a0_ref[...] + _bdot(wa, a2_ref[...]))
    g = _bdot(_sigmoid(gd), g2_ref[...])
    if first_layer:
        vf_out_ref[0] = v
    else:
        v = v + (vf_ref[0] - v) * _sigmoid(v0_ref[...] + _bdot(hv_ref[0], v2_ref[...]))
    kk = k * kk_ref[...]
    kk = kk / jnp.maximum(jnp.sqrt(_seg_sum(kk * kk, mseg)), 1e-12)
    k2 = k * (1.0 + (a - 1.0) * ka_ref[...])
    av = -kk
    bv = kk * a

    ti = lax.broadcasted_iota(jnp.int32, (L, L), 0)
    si = lax.broadcasted_iota(jnp.int32, (L, L), 1)
    tril = (ti >= si).astype(F32)
    cum = jnp.dot(tril, lw, preferred_element_type=F32, precision=lax.Precision.HIGHEST)
    mid = cum[L // 2 - 1:L // 2, :]
    cm = cum - mid
    ecum = jnp.exp(cm)
    einv = jnp.exp(-cm)
    rt = r * ecum
    kt = k2 * einv
    bt = bv * einv
    at = av * jnp.exp(cm - lw)
    p_end = jnp.exp(cum[L - 1:L, :])
    e_end = ecum[L - 1:L, :]
    e_mid = jnp.exp(mid)

    lane = lax.broadcasted_iota(jnp.int32, (1, LANES), 1)
    m0 = (lane < RW_HEAD).astype(F32)
    m1 = 1.0 - m0
    strict = ti > si
    incl = ti >= si
    bi = lax.broadcasted_iota(jnp.int32, (LANES, LANES), 0) < RW_HEAD
    bj = lax.broadcasted_iota(jnp.int32, (LANES, LANES), 1) < RW_HEAD
    bdmask = (bi == bj).astype(F32)
    zeros_ll = jnp.zeros((L, L), F32)

    ys = []
    for p in range(D_RWKV // LANES):
        sl = slice(LANES * p, LANES * (p + 1))
        A, B, K, R, V = at[:, sl], bt[:, sl], kt[:, sl], rt[:, sl], v[:, sl]
        btkt = jnp.concatenate([B.T, K.T], axis=1)
        h0 = hst[p]
        h0m = h0 * _col_bcast(e_mid[:, sl])
        sc = _bdot(jnp.concatenate([A * m0, A * m1, R * m0, R * m1], axis=0), btkt)
        aab = [jnp.where(strict, sc[e * L:(e + 1) * L, 0:L], 0.0) for e in range(2)]
        aak = [jnp.where(strict, sc[e * L:(e + 1) * L, L:2 * L], 0.0) for e in range(2)]
        arb = [jnp.where(incl, sc[(2 + e) * L:(3 + e) * L, 0:L], 0.0) for e in range(2)]
        ark = [jnp.where(incl, sc[(2 + e) * L:(3 + e) * L, L:2 * L], 0.0) for e in range(2)]
        arh = _bdot(jnp.concatenate([A, R], axis=0), h0m)
        v01 = jnp.concatenate([V * m0, V * m1], axis=0)
        x = arh[0:L] + _bdot(jnp.concatenate(aak, axis=1), v01)
        pm = jnp.concatenate(aab, axis=1)
        n_fac = int(math.log2(L))
        for it in range(n_fac):
            x = x + _bdot(pm, jnp.concatenate([x * m0, x * m1], axis=0))
            if it + 1 < n_fac:
                pd = jnp.concatenate(
                    [jnp.concatenate([pm[:, 0:L], zeros_ll], axis=1),
                     jnp.concatenate([zeros_ll, pm[:, L:2 * L]], axis=1)], axis=0)
                pm = _bdot(pm, pd)
        u = x
        yp = arh[L:2 * L] + _bdot(jnp.concatenate(arb + ark, axis=1),
                                  jnp.concatenate([u * m0, u * m1, v01], axis=0))
        upd = _bdot(btkt, jnp.concatenate([u, V], axis=0))
        hst[p] = (h0 * _col_bcast(p_end[:, sl]) + upd * _col_bcast(e_end[:, sl])) * bdmask
        ys.append(yp)
    y = jnp.concatenate(ys, axis=1)

    mean = _seg_sum(y, mseg) * (1.0 / RW_HEAD)
    d = y - mean
    var = _seg_sum(d * d, mseg) * (1.0 / RW_HEAD)
    yn = d * lax.rsqrt(var + GN_EPS) * lng_ref[...] + lnb_ref[...]
    bonus = _seg_sum(r * k2 * rk_ref[...], mseg) * v
    y_ref[0] = (yn + bonus) * g


def _pad_rows(w, rows, offset):
    out = jnp.zeros((rows, w.shape[1]), w.dtype)
    return out.at[offset:offset + w.shape[0]].set(w)


def _rwkv(z_rw, hv, v_first, prm, bsz, seq):
    L = RW_CHUNK
    first = v_first is None
    row = lambda a: a.reshape(1, -1).astype(F32)
    hid = jnp.arange(D_RWKV) // RW_HEAD
    mseg = (hid[:, None] == hid[None, :]).astype(BF16)
    w2p = _pad_rows(prm['w2'], LANES, 0).astype(BF16)
    a2p = _pad_rows(prm['a2'], LANES, 64).astype(BF16)
    common = [row(prm['mu']), row(prm['w0']), w2p, row(prm['a0']), a2p, prm['g2'].astype(BF16),
              row(prm['kk']), row(prm['ka']), row(prm['rk']), row(prm['lng']), row(prm['lnb']), mseg]
    tok = lambda w: pl.BlockSpec((1, L, w), lambda b, t: (b, t, 0))
    common_specs = [_full(c.shape) for c in common]
    y_shape = jax.ShapeDtypeStruct((bsz, seq, D_RWKV), F32)
    scratch = [pltpu.VMEM((1, RW_COLS), F32), pltpu.VMEM((D_RWKV // LANES, LANES, LANES), F32)]
    if first:
        args = [z_rw] + common
        in_specs = [tok(RW_COLS)] + common_specs
        out_shape = [y_shape, y_shape]
        out_specs = [tok(D_RWKV), tok(D_RWKV)]
    else:
        v2p = _pad_rows(prm['v2'], LANES, 0).astype(BF16)
        extra = [row(prm['v0']), v2p]
        args = [z_rw, hv, v_first] + extra + common
        in_specs = [tok(RW_COLS), tok(LANES), tok(D_RWKV)] + [_full(c.shape) for c in extra] + common_specs
        out_shape = [y_shape]
        out_specs = [tok(D_RWKV)]
    outs = pl.pallas_call(
        functools.partial(_rwkv_kernel, first_layer=first),
        grid=(bsz, seq // L), in_specs=in_specs, out_specs=out_specs, out_shape=out_shape,
        scratch_shapes=scratch, compiler_params=_params(("parallel", "arbitrary")),
    )(*args)
    return (outs[0], outs[1]) if first else (outs[0], v_first)


def _s5_kernel(u_ref, wb_ref, wc_ref, lpr_ref, lpi_ref, d_ref, gw_ref, gb_ref, og_ref, o_ref,
               car_re, car_im, xre, xim):
    tb = u_ref.shape[1]

    @pl.when(pl.program_id(1) == 0)
    def _():
        car_re[...] = jnp.zeros_like(car_re)
        car_im[...] = jnp.zeros_like(car_im)

    u = u_ref[0]
    bu = _bdot(u, wb_ref[...])
    xre[...] = bu[:, 0:S5_MODES]
    xim[...] = bu[:, S5_MODES:2 * S5_MODES]
    row = lax.broadcasted_iota(jnp.int32, (SUBLANES, 1), 0)

    def tile(i, carry):
        cr, ci = carry
        rs = pl.ds(pl.multiple_of(i * SUBLANES, SUBLANES), SUBLANES)
        br, bi = xre[rs, :], xim[rs, :]
        for dist in (1, 2, 4):
            keep = row >= dist
            sr = jnp.where(keep, pltpu.roll(br, dist, 0), 0.0)
            si = jnp.where(keep, pltpu.roll(bi, dist, 0), 0.0)
            lr = lpr_ref[dist - 1:dist, :]
            li = lpi_ref[dist - 1:dist, :]
            br, bi = br + lr * sr - li * si, bi + lr * si + li * sr
        pr, pi = lpr_ref[...], lpi_ref[...]
        xr = br + pr * cr - pi * ci
        xi = bi + pr * ci + pi * cr
        xre[rs, :] = xr
        xim[rs, :] = xi
        return xr[SUBLANES - 1:SUBLANES, :], xi[SUBLANES - 1:SUBLANES, :]

    cr, ci = lax.fori_loop(0, tb // SUBLANES, tile, (car_re[...], car_im[...]))
    car_re[...] = cr
    car_im[...] = ci
    wc = wc_ref[...]
    y = _bdot(xre[...], wc[0:S5_MODES]) + _bdot(xim[...], wc[S5_MODES:2 * S5_MODES])
    y = _gelu(y + d_ref[...] * u)
    y = y * _sigmoid(_bdot(y, gw_ref[...]) + gb_ref[...])
    o_ref[0] = _rms(y, og_ref[...])


def _s5_weights(a_re, a_im, log_dt, b_re, b_im, c_re, c_im):
    lam_re = jnp.minimum(a_re.astype(F32), -1e-4)
    lam_im = a_im.astype(F32)
    dt = jnp.exp(log_dt.astype(F32))[:, None]
    mag = jnp.exp(lam_re * dt)
    lb_re = mag * jnp.cos(lam_im * dt)
    lb_im = mag * jnp.sin(lam_im * dt)
    den = lam_re * lam_re + lam_im * lam_im
    c1_re = ((lb_re - 1.0) * lam_re + lb_im * lam_im) / den
    c1_im = (lb_im * lam_re - (lb_re - 1.0) * lam_im) / den
    br, bi = b_re.astype(F32), b_im.astype(F32)
    bb_re = c1_re[..., None] * br - c1_im[..., None] * bi
    bb_im = c1_re[..., None] * bi + c1_im[..., None] * br
    eye = jnp.eye(S5_GROUPS, dtype=F32)
    wb_re = jnp.einsum('gpc,gh->gchp', bb_re, eye).reshape(D_S5, S5_MODES)
    wb_im = jnp.einsum('gpc,gh->gchp', bb_im, eye).reshape(D_S5, S5_MODES)
    wb = jnp.concatenate([wb_re, wb_im], axis=1).astype(BF16)
    wc_re = jnp.einsum('gcp,gh->gphc', c_re.astype(F32), eye).reshape(S5_MODES, D_S5)
    wc_im = jnp.einsum('gcp,gh->gphc', c_im.astype(F32), eye).reshape(S5_MODES, D_S5)
    wc = jnp.concatenate([wc_re, -wc_im], axis=0).astype(BF16)
    pr, pi = [lb_re], [lb_im]
    for _ in range(SUBLANES - 1):
        pr, pi = pr + [pr[-1] * lb_re - pi[-1] * lb_im], pi + [pr[-1] * lb_im + pi[-1] * lb_re]
    lp_re = jnp.stack(pr).reshape(SUBLANES, S5_MODES)
    lp_im = jnp.stack(pi).reshape(SUBLANES, S5_MODES)
    return wb, wc, lp_re, lp_im


def _s5(u, prm, bsz, seq):
    tb = min(S5_BLOCK, seq)
    wb, wc, lp_re, lp_im = _s5_weights(prm['a_re'], prm['a_im'], prm['log_dt'], prm['b_re'], prm['b_im'],
                                       prm['c_re'], prm['c_im'])
    row = lambda a: a.reshape(1, -1).astype(F32)
    consts = [wb, wc, lp_re, lp_im, row(prm['d']), prm['glu_w'].astype(BF16), row(prm['glu_b']),
              row(prm['out_g'])]
    tok = pl.BlockSpec((1, tb, D_S5), lambda b, t: (b, t, 0))
    return pl.pallas_call(
        _s5_kernel, grid=(bsz, seq // tb),
        in_specs=[tok] + [_full(c.shape) for c in consts], out_specs=tok,
        out_shape=jax.ShapeDtypeStruct((bsz, seq, D_S5), F32),
        scratch_shapes=[pltpu.VMEM((1, S5_MODES), F32), pltpu.VMEM((1, S5_MODES), F32),
                        pltpu.VMEM((tb, S5_MODES), F32), pltpu.VMEM((tb, S5_MODES), F32)],
        compiler_params=_params(("parallel", "arbitrary")),
    )(u, *consts)


def _mix_xattn_kernel(x_ref, yr_ref, ys_ref, wo1_ref, wo2_ref, g_ref, wq_ref, k_ref, v_ref, wo_ref, o_ref):
    x1 = x_ref[0] + _bdot(yr_ref[0], wo1_ref[...]) + _bdot(ys_ref[0], wo2_ref[...])
    h = _rms(x1, g_ref[...])
    q = _bdot(h, wq_ref[...])
    km, vm = k_ref[0], v_ref[0]
    outs = []
    for hd in range(XA_HEADS):
        sl = slice(XA_HEAD * hd, XA_HEAD * (hd + 1))
        s = _bdot_nt(q[:, sl], km[:, sl]) * (XA_HEAD ** -0.5)
        s = s - jnp.max(s, axis=-1, keepdims=True)
        e = jnp.exp(s)
        p = e / jnp.sum(e, axis=-1, keepdims=True)
        outs.append(_bdot(p, vm[:, sl]))
    o = jnp.concatenate(outs, axis=1)
    o_ref[0] = x1 + _bdot(o, wo_ref[...])


def _mix_xattn(x, y_rw, y_s5, w_out, g, wq, kmem, vmem, wo, bsz, seq):
    tm = min(XA_BLOCK, seq)
    consts_a = [w_out[:D_RWKV].astype(BF16), w_out[D_RWKV:].astype(BF16), g.reshape(1, -1).astype(F32),
                wq.astype(BF16)]
    tok = lambda w: pl.BlockSpec((1, tm, w), lambda b, t: (b, t, 0))
    mem = pl.BlockSpec((1, N_MEM, D_MODEL), lambda b, t: (b, 0, 0))
    wo_b = wo.astype(BF16)
    return pl.pallas_call(
        _mix_xattn_kernel, grid=(bsz, seq // tm),
        in_specs=[tok(D_MODEL), tok(D_RWKV), tok(D_S5)] + [_full(c.shape) for c in consts_a]
                 + [mem, mem, _full(wo_b.shape)],
        out_specs=tok(D_MODEL), out_shape=jax.ShapeDtypeStruct((bsz, seq, D_MODEL), F32),
        compiler_params=_params(("parallel", "parallel")),
    )(x, y_rw, y_s5, *consts_a, kmem, vmem, wo_b)


def _top_rows(work, order, aux, val_ref, idx_ref):
    for it in range(PEER_TOPK):
        m = jnp.max(work, axis=0, keepdims=True)
        pos = jnp.min(jnp.where(work == m, order, jnp.inf), axis=0, keepdims=True)
        hit = order == pos
        val_ref[it:it + 1, :] = m
        if aux is None:
            idx_ref[it:it + 1, :] = pos
        else:
            idx_ref[it:it + 1, :] = jnp.sum(jnp.where(hit, aux, 0.0), axis=0, keepdims=True)
        work = jnp.where(hit, -jnp.inf, work)


_CAND_ROW_BLOCKS = [(0, PEER_TOPK), (1, SUBLANES), (2, SUBLANES), (3, SUBLANES)]
_CAND_COL_BLOCKS = [(0, PEER_TOPK, 4, 15), (1, SUBLANES, 4, 7), (2, SUBLANES, 4, 4)]
N_CAND = sum(n for _, n in _CAND_ROW_BLOCKS) + sum(n for _, n, _, _ in _CAND_COL_BLOCKS)


def _cand_consts(tm):
    flat, neg = [], []
    for a, nb in _CAND_ROW_BLOCKS:
        flat += [a * PEER_TOPK + b for b in range(nb)]
        neg += [0.0] * nb
    for b, na, lo, hi in _CAND_COL_BLOCKS:
        flat += [a * PEER_TOPK + b for a in range(na)]
        neg += [0.0 if lo <= a <= hi else -float('inf') for a in range(na)]
    flat = [f if n == 0.0 else 1000.0 + i for i, (f, n) in enumerate(zip(flat, neg))]
    col = lambda v: jnp.broadcast_to(jnp.asarray(v, F32)[:, None], (N_CAND, tm))
    return col(flat), col(neg)


def _cand_rows(row_vals, col_vals, combine):
    blocks = [combine(row_vals[a:a + 1, :], col_vals[0:nb, :]) for a, nb in _CAND_ROW_BLOCKS]
    blocks += [combine(row_vals[0:na, :], col_vals[b:b + 1, :]) for b, na, _, _ in _CAND_COL_BLOCKS]
    return jnp.concatenate(blocks, axis=0)


SEL_HEADS_PER_STEP = 2


def _peer_select_kernel(x_ref, g_ref, wq_ref, keys_ref, cflat_ref, cneg_ref, x8_ref, h8_ref, base_ref, shift_ref,
                        gate_ref, q3, idx_t, gate_t, *lists):
    tm = x_ref.shape[0]
    n = SEL_HEADS_PER_STEP
    s1, i1, s2, i2, top, eid = (lists[j * n:(j + 1) * n] for j in range(6))
    x = x_ref[...]
    h = _rms(x, g_ref[...])
    for r in range(D_MODEL // LANES):
        rows = pl.ds(r, tm, stride=D_MODEL // LANES)
        h8_ref[rows, :] = h[:, LANES * r:LANES * (r + 1)]
        x8_ref[rows, :] = x[:, LANES * r:LANES * (r + 1)]
    q = _bdot(h, wq_ref[...])
    for j in range(2 * PEER_HEADS):
        q3[j] = q[:, LANES * j:LANES * (j + 1)]
    iota_k = lax.broadcasted_iota(jnp.int32, (PEER_NKEYS, tm), 0).astype(F32)

    def heads(step, _):
        for u in range(SEL_HEADS_PER_STEP):
            hd = step * SEL_HEADS_PER_STEP + u
            sc1 = _bdot_nt(keys_ref[2 * hd], q3[2 * hd])
            sc2 = _bdot_nt(keys_ref[2 * hd + 1], q3[2 * hd + 1])
            _top_rows(sc1, iota_k, None, s1[u], i1[u])
            _top_rows(sc2, iota_k, None, s2[u], i2[u])
            cand = _cand_rows(s1[u][...], s2[u][...], lambda x, y: x + y) + cneg_ref[...]
            cidx = _cand_rows(i1[u][...], i2[u][...], lambda x, y: x * float(PEER_NKEYS) + y)
            _top_rows(cand, cflat_ref[...], cidx, top[u], eid[u])
            tv = top[u][...]
            e = jnp.exp(tv - jnp.max(tv, axis=0, keepdims=True))
            rs = pl.ds(pl.multiple_of(hd * PEER_TOPK, PEER_TOPK), PEER_TOPK)
            idx_t[rs, :] = eid[u][...]
            gate_t[rs, :] = e / jnp.sum(e, axis=0, keepdims=True)
        return 0

    lax.fori_loop(0, PEER_HEADS // SEL_HEADS_PER_STEP, heads, 0)
    e_t = idx_t[...].T
    pair = jnp.floor(e_t * 0.5)
    base_ref[...] = (pair * float(SUBLANES)).astype(jnp.int32)
    shift_ref[...] = ((e_t - 2.0 * pair) * 16.0).astype(jnp.int32)
    gate_ref[...] = gate_t[...].T


def _peer_select(x2, g, wq, keys):
    t = x2.shape[0]
    tm = SEL_BLOCK
    keys_b = keys.reshape(2 * PEER_HEADS, PEER_NKEYS, LANES).astype(BF16)
    wq_b = wq.astype(BF16)
    cflat, cneg = _cand_consts(tm)
    tokspec = lambda w: pl.BlockSpec((tm, w), lambda i: (i, 0))
    vm = lambda r: pltpu.VMEM((r, tm), F32)
    return pl.pallas_call(
        _peer_select_kernel, grid=(t // tm,),
        in_specs=[tokspec(D_MODEL), _full((1, D_MODEL)), _full(wq_b.shape), _full(keys_b.shape),
                  _full(cflat.shape), _full(cneg.shape)],
        out_specs=[pl.BlockSpec((tm * SUBLANES, LANES), lambda i: (i, 0))] * 2
                  + [tokspec(PEER_SEL), tokspec(PEER_SEL), tokspec(PEER_SEL)],
        out_shape=[jax.ShapeDtypeStruct((t * SUBLANES, LANES), F32)] * 2 + [
                   jax.ShapeDtypeStruct((t, PEER_SEL), jnp.int32),
                   jax.ShapeDtypeStruct((t, PEER_SEL), jnp.int32), jax.ShapeDtypeStruct((t, PEER_SEL), F32)],
        scratch_shapes=[pltpu.VMEM((2 * PEER_HEADS, tm, LANES), F32), vm(PEER_SEL), vm(PEER_SEL)]
                       + [vm(PEER_TOPK) for _ in range(6 * SEL_HEADS_PER_STEP)],
        compiler_params=_params(("parallel",)),
    )(x2, g.reshape(1, -1).astype(F32), wq_b, keys_b, cflat, cneg)


def _tile_table(tab):
    return tab.reshape(tab.shape[0], D_MODEL // LANES, LANES)


def _pack_table(tab3):
    n = tab3.shape[0]
    bits = lax.bitcast_convert_type(tab3.astype(BF16), jnp.uint16).astype(U32)
    bits = bits.reshape(n // 2, 2, SUBLANES, LANES)
    packed = (bits[:, 0] << 16) | bits[:, 1]
    return lax.bitcast_convert_type(packed, jnp.int32).reshape(n // 2 * SUBLANES, LANES)


def _splat_into(src_ref, t, dst_ref, slot):
    tile = jnp.broadcast_to(src_ref[pl.ds(t, 1), :], (LANES, LANES)).T
    dst_ref[LANES * slot:LANES * (slot + 1), :] = tile


def _bcast_row(ref, row):
    return jnp.broadcast_to(ref[row:row + 1, :], (SUBLANES, LANES))


def _expert_tile(tab_ref, base, shift_splat, row):
    w = tab_ref[pl.ds(pl.multiple_of(base, SUBLANES), SUBLANES), :]
    return lax.bitcast_convert_type((w << _bcast_row(shift_splat, row)) & jnp.int32(-65536), F32)


def _token_rows(t):
    return pl.ds(pl.multiple_of(t * SUBLANES, SUBLANES), SUBLANES)


def _pipelined_tokens(tg, prep, work):
    prep(0, 0)

    def body(i, _):
        t0 = 2 * i
        prep(t0 + 1, 1)
        work(t0, 0)
        prep(jnp.minimum(t0 + 2, tg - 1), 0)
        work(t0 + 1, 1)
        return 0

    lax.fori_loop(0, tg // 2, body, 0)


def _table_spec(shape):
    return pl.BlockSpec(shape, lambda i: (0, 0), pipeline_mode=pl.Buffered(1))


N_CHUNK = D_MODEL // LANES
CHUNK_STRIDE = PEER_SEL + SUBLANES


def _peer_u_kernel(base_ref, shift_ref, h_ref, gate_ref, tab_ref, eye_ref, c_ref, shift_splat, *planes):
    tg = gate_ref.shape[0]
    ones = jnp.ones((LANES, LANES), BF16)

    half = N_CHUNK // 2

    def prep(t, slot):
        _splat_into(shift_ref, t, shift_splat, slot)

    def gather(t, slot):
        ht = h_ref[_token_rows(t), :]
        for kx in range(PEER_SEL):
            prod = _expert_tile(tab_ref, base_ref[t, kx], shift_splat, LANES * slot + kx) * ht
            fold = prod + pltpu.roll(prod, half, 0)
            planes[slot][pl.ds(kx, half, stride=CHUNK_STRIDE), :] = fold[0:half, :]

    def finish(t, slot):
        plane = planes[slot]
        acc = plane[0:PEER_SEL, :]
        for r in range(1, half):
            acc = acc + plane[CHUNK_STRIDE * r:CHUNK_STRIDE * r + PEER_SEL, :]
        hi = acc.astype(BF16)
        lo = (acc - hi.astype(F32)).astype(BF16)
        tot = jnp.dot(hi, ones, preferred_element_type=F32) + jnp.dot(lo, ones, preferred_element_type=F32)
        score = jnp.sum(tot * eye_ref[...], axis=0, keepdims=True)
        c_ref[pl.ds(t, 1), :] = gate_ref[pl.ds(t, 1), :] * _gelu(score)

    planes[1][...] = jnp.zeros_like(planes[1])
    prep(0, 0)

    def body(i, _):
        t0 = 2 * i
        prep(t0 + 1, 1)
        gather(t0, 0)
        finish(jnp.maximum(t0 - 1, 0), 1)
        prep(jnp.minimum(t0 + 2, tg - 1), 0)
        gather(t0 + 1, 1)
        finish(t0, 0)
        return 0

    lax.fori_loop(0, tg // 2, body, 0)
    finish(tg - 1, 1)


def _peer_u(base, shift, h8, gate, tab, t):
    tg = GATHER_BLOCK
    eye = jnp.eye(LANES, dtype=F32)
    tokrow = pl.BlockSpec((tg, PEER_SEL), lambda i: (i, 0))
    return pl.pallas_call(
        _peer_u_kernel, grid=(t // tg,),
        in_specs=[pl.BlockSpec((tg, PEER_SEL), lambda i: (i, 0), memory_space=pltpu.SMEM), tokrow,
                  pl.BlockSpec((tg * SUBLANES, LANES), lambda i: (i, 0)), tokrow,
                  _table_spec(tab.shape), _full(eye.shape)],
        out_specs=tokrow, out_shape=jax.ShapeDtypeStruct((t, PEER_SEL), F32),
        scratch_shapes=[pltpu.VMEM((2 * LANES, LANES), jnp.int32)]
                       + [pltpu.VMEM((N_CHUNK // 2 * CHUNK_STRIDE, LANES), F32)] * 2,
        compiler_params=_params(("parallel",)),
    )(base, shift, h8, gate, tab, eye)


def _peer_v_kernel(base_ref, shift_ref, c_ref, x_ref, tab_ref, _aliased_out, o_ref, shift_splat, c_splat, otile):
    tg = c_ref.shape[0]
    n_acc = 4

    def prep(t, slot):
        _splat_into(shift_ref, t, shift_splat, slot)
        _splat_into(c_ref, t, c_splat, slot)

    def work(t, slot):
        accs = [jnp.zeros((SUBLANES, LANES), F32) for _ in range(n_acc)]
        for kx in range(PEER_SEL):
            row = LANES * slot + kx
            accs[kx % n_acc] = accs[kx % n_acc] + (_bcast_row(c_splat, row)
                                                   * _expert_tile(tab_ref, base_ref[t, kx], shift_splat, row))
        rs = _token_rows(t)
        otile[rs, :] = x_ref[rs, :] + ((accs[0] + accs[1]) + (accs[2] + accs[3]))

    _pipelined_tokens(tg, prep, work)
    for r in range(N_CHUNK):
        o_ref[:, LANES * r:LANES * (r + 1)] = otile[pl.ds(r, tg, stride=N_CHUNK), :]


def _peer_v(base, shift, c, x8, x2, tab, t):
    tg = GATHER_BLOCK
    smem = pl.BlockSpec((tg, PEER_SEL), lambda i: (i, 0), memory_space=pltpu.SMEM)
    tokrow = pl.BlockSpec((tg, PEER_SEL), lambda i: (i, 0))
    tile = pl.BlockSpec((tg * SUBLANES, LANES), lambda i: (i, 0))
    return pl.pallas_call(
        _peer_v_kernel, grid=(t // tg,),
        in_specs=[smem, tokrow, tokrow, tile, _table_spec(tab.shape), pl.BlockSpec(memory_space=pl.ANY)],
        out_specs=pl.BlockSpec((tg, D_MODEL), lambda i: (i, 0)),
        out_shape=jax.ShapeDtypeStruct(x2.shape, F32),
        input_output_aliases={5: 0},
        scratch_shapes=[pltpu.VMEM((2 * LANES, LANES), jnp.int32), pltpu.VMEM((2 * LANES, LANES), F32),
                        pltpu.VMEM((tg * SUBLANES, LANES), F32)],
        compiler_params=_params(("parallel",)),
    )(base, shift, c, x8, tab, x2)


SC_WORKERS = 32
SC_LANES = 16
SC_ROWS = 32
SC_TOKENS = 13824
SC_GROUP = 8
SC_REGS = 32


def _peer_sc(eidx, gate, h8, x8, tab_u, tab_v, t_off):
    ts = eidx.shape[0]
    per_w = ts // SC_WORKERS
    n_chunk = PEER_SEL // SC_ROWS
    grp = SC_GROUP
    sel = grp * PEER_SEL
    n_pairs = grp * n_chunk // 2
    per_tile = LANES // SC_LANES
    n_blk = D_MODEL // (SC_REGS * SC_LANES)
    mesh = plsc.VectorSubcoreMesh(core_axis_name="c", subcore_axis_name="s")

    def piece(ref, lead, q):
        return ref[lead, q // per_tile, pl.ds(SC_LANES * (q % per_tile), SC_LANES)]

    def body(u_hbm, v_hbm, idx_hbm, gate_hbm, h_hbm, x_hbm, out_hbm,
             idx_v, c_v, h_v, acc_v, part_v, rows0, rows1, sem0, sem1):
        wid = lax.axis_index("s") * 2 + lax.axis_index("c")
        bufs, sems = (rows0, rows1), (sem0, sem1)
        lane = lax.iota(jnp.int32, SC_LANES)

        def stream(tab_hbm, compute):
            def gather(chunk, slot):
                start = pl.multiple_of(chunk * SC_ROWS, SC_ROWS)
                return pltpu.make_async_copy(tab_hbm.at[idx_v.at[pl.ds(start, SC_ROWS)]], bufs[slot], sems[slot])

            gather(0, 0).start()

            @pl.loop(0, n_pairs)
            def _(p):
                c0 = 2 * p
                gather(c0 + 1, 1).start()
                gather(c0, 0).wait()
                compute(c0, 0)
                gather(jnp.minimum(c0 + 2, 2 * n_pairs - 1), 0).start()
                gather(c0 + 1, 1).wait()
                compute(c0 + 1, 1)

            gather(0, 0).wait()

        def dots(chunk, slot):
            tok = chunk // n_chunk
            for blk in range(n_blk):
                where = [blk * SC_REGS + j for j in range(SC_REGS)]
                hs = [piece(h_v, tok, q) for q in where]

                def row(k, carry):
                    parts = [None] * 4
                    for i, q in enumerate(where):
                        term = piece(bufs[slot], k, q) * hs[i]
                        parts[i % 4] = term if parts[i % 4] is None else parts[i % 4] + term
                    tot = (parts[0] + parts[1]) + (parts[2] + parts[3])
                    at = pl.ds(pl.multiple_of((chunk * SC_ROWS + k) * SC_LANES, SC_LANES), SC_LANES)
                    if blk == 0:
                        part_v[at] = tot
                    else:
                        part_v[at] = part_v[at] + tot
                    return carry

                lax.fori_loop(0, SC_ROWS, row, 0)

        def weigh(chunk, slot):
            tok = chunk // n_chunk
            for blk in range(n_blk):
                where = [blk * SC_REGS + j for j in range(SC_REGS)]

                def row(k, accs):
                    ck = plsc.load_gather(c_v, [jnp.full((SC_LANES,), chunk * SC_ROWS, jnp.int32) + k])
                    return tuple(a + ck * piece(bufs[slot], k, q) for a, q in zip(accs, where))

                accs = lax.fori_loop(0, SC_ROWS, row, tuple(piece(acc_v, tok, q) for q in where))
                for a, q in zip(accs, where):
                    acc_v[tok, q // per_tile, pl.ds(SC_LANES * (q % per_tile), SC_LANES)] = a

        @pl.loop(0, per_w // grp)
        def _(g):
            t0 = wid * per_w + g * grp
            flat = pl.ds(pl.multiple_of(t0 * PEER_SEL, sel), sel)
            pltpu.sync_copy(idx_hbm.at[flat], idx_v)
            pltpu.sync_copy(gate_hbm.at[flat], c_v)
            pltpu.sync_copy(h_hbm.at[pl.ds(t_off + t0, grp)], h_v)
            pltpu.sync_copy(x_hbm.at[pl.ds(t_off + t0, grp)], acc_v)
            stream(u_hbm, dots)

            @pl.loop(0, sel // SC_LANES)
            def _(m):
                base = (m * SC_LANES + lane) * SC_LANES
                score = plsc.load_gather(part_v, [base])
                for l in range(1, SC_LANES):
                    score = score + plsc.load_gather(part_v, [base + l])
                z = math.sqrt(2.0 / math.pi) * (score + 0.044715 * (score * score * score))
                tanh_z = 1.0 - 2.0 / (jnp.exp(2.0 * z) + 1.0)
                at = pl.ds(pl.multiple_of(m * SC_LANES, SC_LANES), SC_LANES)
                c_v[at] = c_v[at] * (0.5 * score * (1.0 + tanh_z))

            stream(v_hbm, weigh)
            pltpu.sync_copy(acc_v, out_hbm.at[pl.ds(t0, grp)])

    tile = lambda n: pltpu.VMEM((n, N_CHUNK, LANES), F32)
    out = pl.kernel(
        body, mesh=mesh, out_type=jax.ShapeDtypeStruct((ts, N_CHUNK, LANES), F32),
        scratch_types=[pltpu.VMEM((sel,), jnp.int32), pltpu.VMEM((sel,), F32), tile(grp), tile(grp),
                       pltpu.VMEM((sel * SC_LANES,), F32), tile(SC_ROWS), tile(SC_ROWS),
                       pltpu.SemaphoreType.DMA, pltpu.SemaphoreType.DMA],
        compiler_params=pltpu.CompilerParams(needs_layout_passes=False),
    )(tab_u, tab_v, eidx.reshape(-1), gate.reshape(-1),
      h8.reshape(-1, N_CHUNK, LANES), x8.reshape(-1, N_CHUNK, LANES))
    return out


def _final_norm_kernel(x_ref, g_ref, o_ref):
    o_ref[...] = _rms(x_ref[...], g_ref[...])


def _final_norm(x2, g):
    t, d = x2.shape
    tm = min(PROJ_BLOCK, t)
    spec = pl.BlockSpec((tm, d), lambda i: (i, 0))
    return pl.pallas_call(
        _final_norm_kernel, grid=(t // tm,), in_specs=[spec, _full((1, d))], out_specs=spec,
        out_shape=jax.ShapeDtypeStruct((t, d), F32), compiler_params=_params(("parallel",)),
    )(x2, g.reshape(1, d).astype(F32))


def kernel(x, mem, norm_mix, w_in, rw_mu, rw_w0, rw_w2, rw_a0, rw_a2, rw_g2, rw_kk, rw_ka, rw_rk, rw_v0, rw_v1, rw_v2, rw_lnx_g, rw_lnx_b, s5_a_re, s5_a_im, s5_log_dt, s5_b_re, s5_b_im, s5_c_re, s5_c_im, s5_d, s5_glu_w, s5_glu_b, s5_out_g, w_out, norm_xa, norm_mem, xa_wq, xa_wk, xa_wv, xa_wo, norm_ffn, peer_wq, peer_keys, peer_u, peer_v, norm_final):
    bsz, seq, d = x.shape
    t = bsz * seq
    depth = w_in.shape[0]
    mem2 = mem.reshape(bsz * N_MEM, d)
    v_first = None
    for l in range(depth):
        x2 = x.reshape(t, d)
        w_rw = w_in[l][:, :RW_COLS].astype(BF16)
        w_s5 = w_in[l][:, RW_COLS:].astype(BF16)
        ws = [w_rw, w_s5]
        if l > 0:
            ws.append(_pad_rows(rw_v1[l - 1].T, LANES, 0).T.astype(BF16))
        outs = _norm_proj(x2, norm_mix[l], ws, [F32] * len(ws), PROJ_BLOCK)
        z_rw = outs[0].reshape(bsz, seq, RW_COLS)
        u_s5 = outs[1].reshape(bsz, seq, D_S5)
        hv = outs[2].reshape(bsz, seq, LANES) if l > 0 else None
        rw_prm = dict(mu=rw_mu[l], w0=rw_w0[l], w2=rw_w2[l], a0=rw_a0[l], a2=rw_a2[l], g2=rw_g2[l],
                      kk=rw_kk[l], ka=rw_ka[l], rk=rw_rk[l], lng=rw_lnx_g[l], lnb=rw_lnx_b[l])
        if l > 0:
            rw_prm.update(v0=rw_v0[l - 1], v2=rw_v2[l - 1])
        y_rw, v_first = _rwkv(z_rw, hv, v_first, rw_prm, bsz, seq)
        s5_prm = dict(a_re=s5_a_re[l], a_im=s5_a_im[l], log_dt=s5_log_dt[l], b_re=s5_b_re[l], b_im=s5_b_im[l],
                      c_re=s5_c_re[l], c_im=s5_c_im[l], d=s5_d[l], glu_w=s5_glu_w[l], glu_b=s5_glu_b[l],
                      out_g=s5_out_g[l])
        y_s5 = _s5(u_s5, s5_prm, bsz, seq)
        kv = _norm_proj(mem2, norm_mem[l], [xa_wk[l].astype(BF16), xa_wv[l].astype(BF16)], [BF16, BF16],
                        PROJ_BLOCK)
        kmem = kv[0].reshape(bsz, N_MEM, d)
        vmem = kv[1].reshape(bsz, N_MEM, d)
        x = _mix_xattn(x, y_rw, y_s5, w_out[l], norm_xa[l], xa_wq[l], kmem, vmem, xa_wo[l], bsz, seq)
        x2 = x.reshape(t, d)
        x8, h8, base, shift, gate = _peer_select(x2, norm_ffn[l], peer_wq[l], peer_keys[l])
        t_tc = t - SC_TOKENS
        u3, v3 = _tile_table(peer_u[l]), _tile_table(peer_v[l])
        eidx_sc = (base[t_tc:] >> 2) + (shift[t_tc:] >> 4)
        x_sc = _peer_sc(eidx_sc, gate[t_tc:], h8, x8, u3, v3, t_tc)
        c = _peer_u(base, shift, h8, gate, _pack_table(u3), t_tc)
        x_new = _peer_v(base, shift, c, x8, x2, _pack_table(v3), t_tc)
        x = lax.dynamic_update_slice(x_new, x_sc.reshape(SC_TOKENS, d), (t_tc, 0)).reshape(bsz, seq, d)
    return _final_norm(x.reshape(t, d), norm_final).reshape(bsz, seq, d)
```

```python
import functools
import math

import jax
import jax.numpy as jnp
from jax import lax
from jax.experimental import pallas as pl
from jax.experimental.pallas import tpu as pltpu
from jax.experimental.pallas import tpu_sc as plsc

F32 = jnp.float32
BF16 = jnp.bfloat16
U32 = jnp.uint32

LANES = 128
SUBLANES = 8
VMEM_LIMIT = 56 * 1024 * 1024

D_MODEL = 1024
D_RWKV = 512
RW_HEAD = 64
RW_COLS = 1792
D_S5 = 512
S5_GROUPS = 32
S5_CH = 16
S5_STATE = 64
S5_MODES = S5_GROUPS * S5_STATE
N_MEM = 256
XA_HEADS = 4
XA_HEAD = 256
PEER_HEADS = 8
PEER_NKEYS = 128
PEER_TOPK = 16
PEER_SEL = PEER_HEADS * PEER_TOPK
RMS_EPS = 1e-6
GN_EPS = 64e-5

RW_CHUNK = 128
S5_BLOCK = 256
PROJ_BLOCK = 512
XA_BLOCK = 256
SEL_BLOCK = 128
GATHER_BLOCK = 64


def _params(sem):
    return pltpu.CompilerParams(dimension_semantics=sem, vmem_limit_bytes=VMEM_LIMIT)


def _rms(x, g):
    ms = jnp.mean(x * x, axis=-1, keepdims=True)
    return x * lax.rsqrt(ms + RMS_EPS) * g


def _bdot(a, b):
    return jnp.dot(a.astype(BF16), b.astype(BF16), preferred_element_type=F32)


def _bdot_nt(a, b):
    return lax.dot_general(a.astype(BF16), b.astype(BF16), (((1,), (1,)), ((), ())),
                           preferred_element_type=F32)


def _sigmoid(x):
    return 1.0 / (1.0 + jnp.exp(-x))


def _softplus(x):
    return jnp.maximum(x, 0.0) + jnp.log(1.0 + jnp.exp(-jnp.abs(x)))


def _gelu(x):
    return 0.5 * x * (1.0 + jnp.tanh(math.sqrt(2.0 / math.pi) * (x + 0.044715 * (x * x * x))))


def _full(shape):
    n = len(shape)
    return pl.BlockSpec(shape, lambda *_: (0,) * n)


def _norm_proj_kernel(*refs, n_out):
    x_ref, g_ref = refs[0], refs[1]
    w_refs = refs[2:2 + n_out]
    o_refs = refs[2 + n_out:]
    h = _rms(x_ref[...], g_ref[...]).astype(BF16)
    for w_ref, o_ref in zip(w_refs, o_refs):
        o_ref[...] = jnp.dot(h, w_ref[...], preferred_element_type=F32).astype(o_ref.dtype)


def _norm_proj(x2, g, ws, out_dtypes, block):
    t, d = x2.shape
    block = min(block, t)
    in_specs = [pl.BlockSpec((block, d), lambda i: (i, 0)), _full((1, d))]
    in_specs += [_full(w.shape) for w in ws]
    out_specs = [pl.BlockSpec((block, w.shape[1]), lambda i: (i, 0)) for w in ws]
    out_shape = [jax.ShapeDtypeStruct((t, w.shape[1]), dt) for w, dt in zip(ws, out_dtypes)]
    return pl.pallas_call(
        functools.partial(_norm_proj_kernel, n_out=len(ws)),
        grid=(t // block,), in_specs=in_specs, out_specs=out_specs, out_shape=out_shape,
        compiler_params=_params(("parallel",)),
    )(x2, g.reshape(1, d), *ws)


def _seg_sum(x, mseg):
    hi = x.astype(BF16)
    lo = (x - hi.astype(F32)).astype(BF16)
    return (jnp.dot(hi, mseg, preferred_element_type=F32)
            + jnp.dot(lo, mseg, preferred_element_type=F32))


def _col_bcast(row):
    return jnp.broadcast_to(row, (LANES, LANES)).T


def _rwkv_kernel(*refs, first_layer):
    if first_layer:
        (z_ref, mu_ref, w0_ref, w2_ref, a0_ref, a2_ref, g2_ref, kk_ref, ka_ref, rk_ref,
         lng_ref, lnb_ref, mseg_ref, y_ref, vf_out_ref, zprev, hst) = refs
    else:
        (z_ref, hv_ref, vf_ref, v0_ref, v2_ref, mu_ref, w0_ref, w2_ref, a0_ref, a2_ref, g2_ref,
         kk_ref, ka_ref, rk_ref, lng_ref, lnb_ref, mseg_ref, y_ref, zprev, hst) = refs
    L = RW_CHUNK

    @pl.when(pl.program_id(1) == 0)
    def _():
        zprev[...] = jnp.zeros_like(zprev)
        hst[...] = jnp.zeros_like(hst)

    z = z_ref[0]
    row = lax.broadcasted_iota(jnp.int32, (L, 1), 0)
    zs = jnp.where(row == 0, zprev[...], pltpu.roll(z, 1, 0))
    zprev[...] = z[L - 1:L, :]
    z = z + (zs - z) * mu_ref[...]
    r = z[:, 0:512]
    k = z[:, 512:1024]
    v = z[:, 1024:1536]
    wa = z[:, 1536:1664]
    gd = z[:, 1664:1792]
    mseg = mseg_ref[...]

    wlin = w0_ref[...] + _bdot(jnp.tanh(wa), w2_ref[...])
    lw = -jnp.exp(-_softplus(-wlin) - 0.5)
    a = _sigmoid(a0_ref[...] + _bdot(wa, a2_ref[...]))
    g = _bdot(_sigmoid(gd), g2_ref[...])
    if first_layer:
        vf_out_ref[0] = v
    else:
        v = v + (vf_ref[0] - v) * _sigmoid(v0_ref[...] + _bdot(hv_ref[0], v2_ref[...]))
    kk = k * kk_ref[...]
    kk = kk / jnp.maximum(jnp.sqrt(_seg_sum(kk * kk, mseg)), 1e-12)
    k2 = k * (1.0 + (a - 1.0) * ka_ref[...])
    av = -kk
    bv = kk * a

    ti = lax.broadcasted_iota(jnp.int32, (L, L), 0)
    si = lax.broadcasted_iota(jnp.int32, (L, L), 1)
    tril = (ti >= si).astype(F32)
    cum = jnp.dot(tril, lw, preferred_element_type=F32, precision=lax.Precision.HIGHEST)
    mid = cum[L // 2 - 1:L // 2, :]
    cm = cum - mid
    ecum = jnp.exp(cm)
    einv = jnp.exp(-cm)
    rt = r * ecum
    kt = k2 * einv
    bt = bv * einv
    at = av * jnp.exp(cm - lw)
    p_end = jnp.exp(cum[L - 1:L, :])
    e_end = ecum[L - 1:L, :]
    e_mid = jnp.exp(mid)

    lane = lax.broadcasted_iota(jnp.int32, (1, LANES), 1)
    m0 = (lane < RW_HEAD).astype(F32)
    m1 = 1.0 - m0
    strict = ti > si
    incl = ti >= si
    bi = lax.broadcasted_iota(jnp.int32, (LANES, LANES), 0) < RW_HEAD
    bj = lax.broadcasted_iota(jnp.int32, (LANES, LANES), 1) < RW_HEAD
    bdmask = (bi == bj).astype(F32)
    zeros_ll = jnp.zeros((L, L), F32)

    ys = []
    for p in range(D_RWKV // LANES):
        sl = slice(LANES * p, LANES * (p + 1))
        A, B, K, R, V = at[:, sl], bt[:, sl], kt[:, sl], rt[:, sl], v[:, sl]
        btkt = jnp.concatenate([B.T, K.T], axis=1)
        h0 = hst[p]
        h0m = h0 * _col_bcast(e_mid[:, sl])
        sc = _bdot(jnp.concatenate([A * m0, A * m1, R * m0, R * m1], axis=0), btkt)
        aab = [jnp.where(strict, sc[e * L:(e + 1) * L, 0:L], 0.0) for e in range(2)]
        aak = [jnp.where(strict, sc[e * L:(e + 1) * L, L:2 * L], 0.0) for e in range(2)]
        arb = [jnp.where(incl, sc[(2 + e) * L:(3 + e) * L, 0:L], 0.0) for e in range(2)]
        ark = [jnp.where(incl, sc[(2 + e) * L:(3 + e) * L, L:2 * L], 0.0) for e in range(2)]
        arh = _bdot(jnp.concatenate([A, R], axis=0), h0m)
        v01 = jnp.concatenate([V * m0, V * m1], axis=0)
        x = arh[0:L] + _bdot(jnp.concatenate(aak, axis=1), v01)
        pm = jnp.concatenate(aab, axis=1)
        n_fac = int(math.log2(L))
        for it in range(n_fac):
            x = x + _bdot(pm, jnp.concatenate([x * m0, x * m1], axis=0))
            if it + 1 < n_fac:
                pd = jnp.concatenate(
                    [jnp.concatenate([pm[:, 0:L], zeros_ll], axis=1),
                     jnp.concatenate([zeros_ll, pm[:, L:2 * L]], axis=1)], axis=0)
                pm = _bdot(pm, pd)
        u = x
        yp = arh[L:2 * L] + _bdot(jnp.concatenate(arb + ark, axis=1),
                                  jnp.concatenate([u * m0, u * m1, v01], axis=0))
        upd = _bdot(btkt, jnp.concatenate([u, V], axis=0))
        hst[p] = (h0 * _col_bcast(p_end[:, sl]) + upd * _col_bcast(e_end[:, sl])) * bdmask
        ys.append(yp)
    y = jnp.concatenate(ys, axis=1)

    mean = _seg_sum(y, mseg) * (1.0 / RW_HEAD)
    d = y - mean
    var = _seg_sum(d * d, mseg) * (1.0 / RW_HEAD)
    yn = d * lax.rsqrt(var + GN_EPS) * lng_ref[...] + lnb_ref[...]
    bonus = _seg_sum(r * k2 * rk_ref[...], mseg) * v
    y_ref[0] = (yn + bonus) * g


def _pad_rows(w, rows, offset):
    out = jnp.zeros((rows, w.shape[1]), w.dtype)
    return out.at[offset:offset + w.shape[0]].set(w)


def _rwkv(z_rw, hv, v_first, prm, bsz, seq):
    L = RW_CHUNK
    first = v_first is None
    row = lambda a: a.reshape(1, -1).astype(F32)
    hid = jnp.arange(D_RWKV) // RW_HEAD
    mseg = (hid[:, None] == hid[None, :]).astype(BF16)
    w2p = _pad_rows(prm['w2'], LANES, 0).astype(BF16)
    a2p = _pad_rows(prm['a2'], LANES, 64).astype(BF16)
    common = [row(prm['mu']), row(prm['w0']), w2p, row(prm['a0']), a2p, prm['g2'].astype(BF16),
              row(prm['kk']), row(prm['ka']), row(prm['rk']), row(prm['lng']), row(prm['lnb']), mseg]
    tok = lambda w: pl.BlockSpec((1, L, w), lambda b, t: (b, t, 0))
    common_specs = [_full(c.shape) for c in common]
    y_shape = jax.ShapeDtypeStruct((bsz, seq, D_RWKV), F32)
    scratch = [pltpu.VMEM((1, RW_COLS), F32), pltpu.VMEM((D_RWKV // LANES, LANES, LANES), F32)]
    if first:
        args = [z_rw] + common
        in_specs = [tok(RW_COLS)] + common_specs
        out_shape = [y_shape, y_shape]
        out_specs = [tok(D_RWKV), tok(D_RWKV)]
    else:
        v2p = _pad_rows(prm['v2'], LANES, 0).astype(BF16)
        extra = [row(prm['v0']), v2p]
        args = [z_rw, hv, v_first] + extra + common
        in_specs = [tok(RW_COLS), tok(LANES), tok(D_RWKV)] + [_full(c.shape) for c in extra] + common_specs
        out_shape = [y_shape]
        out_specs = [tok(D_RWKV)]
    outs = pl.pallas_call(
        functools.partial(_rwkv_kernel, first_layer=first),
        grid=(bsz, seq // L), in_specs=in_specs, out_specs=out_specs, out_shape=out_shape,
        scratch_shapes=scratch, compiler_params=_params(("parallel", "arbitrary")),
    )(*args)
    return (outs[0], outs[1]) if first else (outs[0], v_first)


def _s5_kernel(u_ref, wb_ref, wc_ref, lpr_ref, lpi_ref, d_ref, gw_ref, gb_ref, og_ref, o_ref,
               car_re, car_im, xre, xim):
    tb = u_ref.shape[1]

    @pl.when(pl.program_id(1) == 0)
    def _():
        car_re[...] = jnp.zeros_like(car_re)
        car_im[...] = jnp.zeros_like(car_im)

    u = u_ref[0]
    bu = _bdot(u, wb_ref[...])
    xre[...] = bu[:, 0:S5_MODES]
    xim[...] = bu[:, S5_MODES:2 * S5_MODES]
    row = lax.broadcasted_iota(jnp.int32, (SUBLANES, 1), 0)

    def tile(i, carry):
        cr, ci = carry
        rs = pl.ds(pl.multiple_of(i * SUBLANES, SUBLANES), SUBLANES)
        br, bi = xre[rs, :], xim[rs, :]
        for dist in (1, 2, 4):
            keep = row >= dist
            sr = jnp.where(keep, pltpu.roll(br, dist, 0), 0.0)
            si = jnp.where(keep, pltpu.roll(bi, dist, 0), 0.0)
            lr = lpr_ref[dist - 1:dist, :]
            li = lpi_ref[dist - 1:dist, :]
            br, bi = br + lr * sr - li * si, bi + lr * si + li * sr
        pr, pi = lpr_ref[...], lpi_ref[...]
        xr = br + pr * cr - pi * ci
        xi = bi + pr * ci + pi * cr
        xre[rs, :] = xr
        xim[rs, :] = xi
        return xr[SUBLANES - 1:SUBLANES, :], xi[SUBLANES - 1:SUBLANES, :]

    cr, ci = lax.fori_loop(0, tb // SUBLANES, tile, (car_re[...], car_im[...]))
    car_re[...] = cr
    car_im[...] = ci
    wc = wc_ref[...]
    y = _bdot(xre[...], wc[0:S5_MODES]) + _bdot(xim[...], wc[S5_MODES:2 * S5_MODES])
    y = _gelu(y + d_ref[...] * u)
    y = y * _sigmoid(_bdot(y, gw_ref[...]) + gb_ref[...])
    o_ref[0] = _rms(y, og_ref[...])


def _s5_weights(a_re, a_im, log_dt, b_re, b_im, c_re, c_im):
    lam_re = jnp.minimum(a_re.astype(F32), -1e-4)
    lam_im = a_im.astype(F32)
    dt = jnp.exp(log_dt.astype(F32))[:, None]
    mag = jnp.exp(lam_re * dt)
    lb_re = mag * jnp.cos(lam_im * dt)
    lb_im = mag * jnp.sin(lam_im * dt)
    den = lam_re * lam_re + lam_im * lam_im
    c1_re = ((lb_re - 1.0) * lam_re + lb_im * lam_im) / den
    c1_im = (lb_im * lam_re - (lb_re - 1.0) * lam_im) / den
    br, bi = b_re.astype(F32), b_im.astype(F32)
    bb_re = c1_re[..., None] * br - c1_im[..., None] * bi
    bb_im = c1_re[..., None] * bi + c1_im[..., None] * br
    eye = jnp.eye(S5_GROUPS, dtype=F32)
    wb_re = jnp.einsum('gpc,gh->gchp', bb_re, eye).reshape(D_S5, S5_MODES)
    wb_im = jnp.einsum('gpc,gh->gchp', bb_im, eye).reshape(D_S5, S5_MODES)
    wb = jnp.concatenate([wb_re, wb_im], axis=1).astype(BF16)
    wc_re = jnp.einsum('gcp,gh->gphc', c_re.astype(F32), eye).reshape(S5_MODES, D_S5)
    wc_im = jnp.einsum('gcp,gh->gphc', c_im.astype(F32), eye).reshape(S5_MODES, D_S5)
    wc = jnp.concatenate([wc_re, -wc_im], axis=0).astype(BF16)
    pr, pi = [lb_re], [lb_im]
    for _ in range(SUBLANES - 1):
        pr, pi = pr + [pr[-1] * lb_re - pi[-1] * lb_im], pi + [pr[-1] * lb_im + pi[-1] * lb_re]
    lp_re = jnp.stack(pr).reshape(SUBLANES, S5_MODES)
    lp_im = jnp.stack(pi).reshape(SUBLANES, S5_MODES)
    return wb, wc, lp_re, lp_im


def _s5(u, prm, bsz, seq):
    tb = min(S5_BLOCK, seq)
    wb, wc, lp_re, lp_im = _s5_weights(prm['a_re'], prm['a_im'], prm['log_dt'], prm['b_re'], prm['b_im'],
                                       prm['c_re'], prm['c_im'])
    row = lambda a: a.reshape(1, -1).astype(F32)
    consts = [wb, wc, lp_re, lp_im, row(prm['d']), prm['glu_w'].astype(BF16), row(prm['glu_b']),
              row(prm['out_g'])]
    tok = pl.BlockSpec((1, tb, D_S5), lambda b, t: (b, t, 0))
    return pl.pallas_call(
        _s5_kernel, grid=(bsz, seq // tb),
        in_specs=[tok] + [_full(c.shape) for c in consts], out_specs=tok,
        out_shape=jax.ShapeDtypeStruct((bsz, seq, D_S5), F32),
        scratch_shapes=[pltpu.VMEM((1, S5_MODES), F32), pltpu.VMEM((1, S5_MODES), F32),
                        pltpu.VMEM((tb, S5_MODES), F32), pltpu.VMEM((tb, S5_MODES), F32)],
        compiler_params=_params(("parallel", "arbitrary")),
    )(u, *consts)


def _mix_xattn_kernel(x_ref, yr_ref, ys_ref, wo1_ref, wo2_ref, g_ref, wq_ref, k_ref, v_ref, wo_ref, o_ref):
    x1 = x_ref[0] + _bdot(yr_ref[0], wo1_ref[...]) + _bdot(ys_ref[0], wo2_ref[...])
    h = _rms(x1, g_ref[...])
    q = _bdot(h, wq_ref[...])
    km, vm = k_ref[0], v_ref[0]
    outs = []
    for hd in range(XA_HEADS):
        sl = slice(XA_HEAD * hd, XA_HEAD * (hd + 1))
        s = _bdot_nt(q[:, sl], km[:, sl]) * (XA_HEAD ** -0.5)
        s = s - jnp.max(s, axis=-1, keepdims=True)
        e = jnp.exp(s)
        p = e / jnp.sum(e, axis=-1, keepdims=True)
        outs.append(_bdot(p, vm[:, sl]))
    o = jnp.concatenate(outs, axis=1)
    o_ref[0] = x1 + _bdot(o, wo_ref[...])


def _mix_xattn(x, y_rw, y_s5, w_out, g, wq, kmem, vmem, wo, bsz, seq):
    tm = min(XA_BLOCK, seq)
    consts_a = [w_out[:D_RWKV].astype(BF16), w_out[D_RWKV:].astype(BF16), g.reshape(1, -1).astype(F32),
                wq.astype(BF16)]
    tok = lambda w: pl.BlockSpec((1, tm, w), lambda b, t: (b, t, 0))
    mem = pl.BlockSpec((1, N_MEM, D_MODEL), lambda b, t: (b, 0, 0))
    wo_b = wo.astype(BF16)
    return pl.pallas_call(
        _mix_xattn_kernel, grid=(bsz, seq // tm),
        in_specs=[tok(D_MODEL), tok(D_RWKV), tok(D_S5)] + [_full(c.shape) for c in consts_a]
                 + [mem, mem, _full(wo_b.shape)],
        out_specs=tok(D_MODEL), out_shape=jax.ShapeDtypeStruct((bsz, seq, D_MODEL), F32),
        compiler_params=_params(("parallel", "parallel")),
    )(x, y_rw, y_s5, *consts_a, kmem, vmem, wo_b)


def _top_rows(work, order, aux, val_ref, idx_ref):
    for it in range(PEER_TOPK):
        m = jnp.max(work, axis=0, keepdims=True)
        pos = jnp.min(jnp.where(work == m, order, jnp.inf), axis=0, keepdims=True)
        hit = order == pos
        val_ref[it:it + 1, :] = m
        if aux is None:
            idx_ref[it:it + 1, :] = pos
        else:
            idx_ref[it:it + 1, :] = jnp.sum(jnp.where(hit, aux, 0.0), axis=0, keepdims=True)
        work = jnp.where(hit, -jnp.inf, work)


_CAND_ROW_BLOCKS = [(0, PEER_TOPK), (1, SUBLANES), (2, SUBLANES), (3, SUBLANES)]
_CAND_COL_BLOCKS = [(0, PEER_TOPK, 4, 15), (1, SUBLANES, 4, 7), (2, SUBLANES, 4, 4)]
N_CAND = sum(n for _, n in _CAND_ROW_BLOCKS) + sum(n for _, n, _, _ in _CAND_COL_BLOCKS)


def _cand_consts(tm):
    flat, neg = [], []
    for a, nb in _CAND_ROW_BLOCKS:
        flat += [a * PEER_TOPK + b for b in range(nb)]
        neg += [0.0] * nb
    for b, na, lo, hi in _CAND_COL_BLOCKS:
        flat += [a * PEER_TOPK + b for a in range(na)]
        neg += [0.0 if lo <= a <= hi else -float('inf') for a in range(na)]
    flat = [f if n == 0.0 else 1000.0 + i for i, (f, n) in enumerate(zip(flat, neg))]
    col = lambda v: jnp.broadcast_to(jnp.asarray(v, F32)[:, None], (N_CAND, tm))
    return col(flat), col(neg)


def _cand_rows(row_vals, col_vals, combine):
    blocks = [combine(row_vals[a:a + 1, :], col_vals[0:nb, :]) for a, nb in _CAND_ROW_BLOCKS]
    blocks += [combine(row_vals[0:na, :], col_vals[b:b + 1, :]) for b, na, _, _ in _CAND_COL_BLOCKS]
    return jnp.concatenate(blocks, axis=0)


SEL_HEADS_PER_STEP = 2


def _peer_select_kernel(x_ref, g_ref, wq_ref, keys_ref, cflat_ref, cneg_ref, x8_ref, h8_ref, base_ref, shift_ref,
                        gate_ref, q3, idx_t, gate_t, *lists):
    tm = x_ref.shape[0]
    n = SEL_HEADS_PER_STEP
    s1, i1, s2, i2, top, eid = (lists[j * n:(j + 1) * n] for j in range(6))
    x = x_ref[...]
    h = _rms(x, g_ref[...])
    for r in range(D_MODEL // LANES):
        rows = pl.ds(r, tm, stride=D_MODEL // LANES)
        h8_ref[rows, :] = h[:, LANES * r:LANES * (r + 1)]
        x8_ref[rows, :] = x[:, LANES * r:LANES * (r + 1)]
    q = _bdot(h, wq_ref[...])
    for j in range(2 * PEER_HEADS):
        q3[j] = q[:, LANES * j:LANES * (j + 1)]
    iota_k = lax.broadcasted_iota(jnp.int32, (PEER_NKEYS, tm), 0).astype(F32)

    def heads(step, _):
        for u in range(SEL_HEADS_PER_STEP):
            hd = step * SEL_HEADS_PER_STEP + u
            sc1 = _bdot_nt(keys_ref[2 * hd], q3[2 * hd])
            sc2 = _bdot_nt(keys_ref[2 * hd + 1], q3[2 * hd + 1])
            _top_rows(sc1, iota_k, None, s1[u], i1[u])
            _top_rows(sc2, iota_k, None, s2[u], i2[u])
            cand = _cand_rows(s1[u][...], s2[u][...], lambda x, y: x + y) + cneg_ref[...]
            cidx = _cand_rows(i1[u][...], i2[u][...], lambda x, y: x * float(PEER_NKEYS) + y)
            _top_rows(cand, cflat_ref[...], cidx, top[u], eid[u])
            tv = top[u][...]
            e = jnp.exp(tv - jnp.max(tv, axis=0, keepdims=True))
            rs = pl.ds(pl.multiple_of(hd * PEER_TOPK, PEER_TOPK), PEER_TOPK)
            idx_t[rs, :] = eid[u][...]
            gate_t[rs, :] = e / jnp.sum(e, axis=0, keepdims=True)
        return 0

    lax.fori_loop(0, PEER_HEADS // SEL_HEADS_PER_STEP, heads, 0)
    e_t = idx_t[...].T
    pair = jnp.floor(e_t * 0.5)
    base_ref[...] = (pair * float(SUBLANES)).astype(jnp.int32)
    shift_ref[...] = ((e_t - 2.0 * pair) * 16.0).astype(jnp.int32)
    gate_ref[...] = gate_t[...].T


def _peer_select(x2, g, wq, keys):
    t = x2.shape[0]
    tm = SEL_BLOCK
    keys_b = keys.reshape(2 * PEER_HEADS, PEER_NKEYS, LANES).astype(BF16)
    wq_b = wq.astype(BF16)
    cflat, cneg = _cand_consts(tm)
    tokspec = lambda w: pl.BlockSpec((tm, w), lambda i: (i, 0))
    vm = lambda r: pltpu.VMEM((r, tm), F32)
    return pl.pallas_call(
        _peer_select_kernel, grid=(t // tm,),
        in_specs=[tokspec(D_MODEL), _full((1, D_MODEL)), _full(wq_b.shape), _full(keys_b.shape),
                  _full(cflat.shape), _full(cneg.shape)],
        out_specs=[pl.BlockSpec((tm * SUBLANES, LANES), lambda i: (i, 0))] * 2
                  + [tokspec(PEER_SEL), tokspec(PEER_SEL), tokspec(PEER_SEL)],
        out_shape=[jax.ShapeDtypeStruct((t * SUBLANES, LANES), F32)] * 2 + [
                   jax.ShapeDtypeStruct((t, PEER_SEL), jnp.int32),
                   jax.ShapeDtypeStruct((t, PEER_SEL), jnp.int32), jax.ShapeDtypeStruct((t, PEER_SEL), F32)],
        scratch_shapes=[pltpu.VMEM((2 * PEER_HEADS, tm, LANES), F32), vm(PEER_SEL), vm(PEER_SEL)]
                       + [vm(PEER_TOPK) for _ in range(6 * SEL_HEADS_PER_STEP)],
        compiler_params=_params(("parallel",)),
    )(x2, g.reshape(1, -1).astype(F32), wq_b, keys_b, cflat, cneg)


def _tile_table(tab):
    return tab.reshape(tab.shape[0], D_MODEL // LANES, LANES)


def _pack_table(tab3):
    n = tab3.shape[0]
    bits = lax.bitcast_convert_type(tab3.astype(BF16), jnp.uint16).astype(U32)
    bits = bits.reshape(n // 2, 2, SUBLANES, LANES)
    packed = (bits[:, 0] << 16) | bits[:, 1]
    return lax.bitcast_convert_type(packed, jnp.int32).reshape(n // 2 * SUBLANES, LANES)


def _splat_into(src_ref, t, dst_ref, slot):
    tile = jnp.broadcast_to(src_ref[pl.ds(t, 1), :], (LANES, LANES)).T
    dst_ref[LANES * slot:LANES * (slot + 1), :] = tile


def _bcast_row(ref, row):
    return jnp.broadcast_to(ref[row:row + 1, :], (SUBLANES, LANES))


def _expert_tile(tab_ref, base, shift_splat, row):
    w = tab_ref[pl.ds(pl.multiple_of(base, SUBLANES), SUBLANES), :]
    return lax.bitcast_convert_type((w << _bcast_row(shift_splat, row)) & jnp.int32(-65536), F32)


def _token_rows(t):
    return pl.ds(pl.multiple_of(t * SUBLANES, SUBLANES), SUBLANES)


def _pipelined_tokens(tg, prep, work):
    prep(0, 0)

    def body(i, _):
        t0 = 2 * i
        prep(t0 + 1, 1)
        work(t0, 0)
        prep(jnp.minimum(t0 + 2, tg - 1), 0)
        work(t0 + 1, 1)
        return 0

    lax.fori_loop(0, tg // 2, body, 0)


def _table_spec(shape):
    return pl.BlockSpec(shape, lambda i: (0, 0), pipeline_mode=pl.Buffered(1))


N_CHUNK = D_MODEL // LANES
CHUNK_STRIDE = PEER_SEL + SUBLANES


def _peer_u_kernel(base_ref, shift_ref, h_ref, gate_ref, tab_ref, eye_ref, c_ref, shift_splat, *planes):
    tg = gate_ref.shape[0]
    ones = jnp.ones((LANES, LANES), BF16)

    half = N_CHUNK // 2

    def prep(t, slot):
        _splat_into(shift_ref, t, shift_splat, slot)

    def gather(t, slot):
        ht = h_ref[_token_rows(t), :]
        for kx in range(PEER_SEL):
            prod = _expert_tile(tab_ref, base_ref[t, kx], shift_splat, LANES * slot + kx) * ht
            fold = prod + pltpu.roll(prod, half, 0)
            planes[slot][pl.ds(kx, half, stride=CHUNK_STRIDE), :] = fold[0:half, :]

    def finish(t, slot):
        plane = planes[slot]
        acc = plane[0:PEER_SEL, :]
        for r in range(1, half):
            acc = acc + plane[CHUNK_STRIDE * r:CHUNK_STRIDE * r + PEER_SEL, :]
        hi = acc.astype(BF16)
        lo = (acc - hi.astype(F32)).astype(BF16)
        tot = jnp.dot(hi, ones, preferred_element_type=F32) + jnp.dot(lo, ones, preferred_element_type=F32)
        score = jnp.sum(tot * eye_ref[...], axis=0, keepdims=True)
        c_ref[pl.ds(t, 1), :] = gate_ref[pl.ds(t, 1), :] * _gelu(score)

    planes[1][...] = jnp.zeros_like(planes[1])
    prep(0, 0)

    def body(i, _):
        t0 = 2 * i
        prep(t0 + 1, 1)
        gather(t0, 0)
        finish(jnp.maximum(t0 - 1, 0), 1)
        prep(jnp.minimum(t0 + 2, tg - 1), 0)
        gather(t0 + 1, 1)
        finish(t0, 0)
        return 0

    lax.fori_loop(0, tg // 2, body, 0)
    finish(tg - 1, 1)


def _peer_u(base, shift, h8, gate, tab, t):
    tg = GATHER_BLOCK
    eye = jnp.eye(LANES, dtype=F32)
    tokrow = pl.BlockSpec((tg, PEER_SEL), lambda i: (i, 0))
    return pl.pallas_call(
        _peer_u_kernel, grid=(t // tg,),
        in_specs=[pl.BlockSpec((tg, PEER_SEL), lambda i: (i, 0), memory_space=pltpu.SMEM), tokrow,
                  pl.BlockSpec((tg * SUBLANES, LANES), lambda i: (i, 0)), tokrow,
                  _table_spec(tab.shape), _full(eye.shape)],
        out_specs=tokrow, out_shape=jax.ShapeDtypeStruct((t, PEER_SEL), F32),
        scratch_shapes=[pltpu.VMEM((2 * LANES, LANES), jnp.int32)]
                       + [pltpu.VMEM((N_CHUNK // 2 * CHUNK_STRIDE, LANES), F32)] * 2,
        compiler_params=_params(("parallel",)),
    )(base, shift, h8, gate, tab, eye)


def _peer_v_kernel(base_ref, shift_ref, c_ref, x_ref, tab_ref, _aliased_out, o_ref, shift_splat, c_splat, otile):
    tg = c_ref.shape[0]
    n_acc = 4

    def prep(t, slot):
        _splat_into(shift_ref, t, shift_splat, slot)
        _splat_into(c_ref, t, c_splat, slot)

    def work(t, slot):
        accs = [jnp.zeros((SUBLANES, LANES), F32) for _ in range(n_acc)]
        for kx in range(PEER_SEL):
            row = LANES * slot + kx
            accs[kx % n_acc] = accs[kx % n_acc] + (_bcast_row(c_splat, row)
                                                   * _expert_tile(tab_ref, base_ref[t, kx], shift_splat, row))
        rs = _token_rows(t)
        otile[rs, :] = x_ref[rs, :] + ((accs[0] + accs[1]) + (accs[2] + accs[3]))

    _pipelined_tokens(tg, prep, work)
    for r in range(N_CHUNK):
        o_ref[:, LANES * r:LANES * (r + 1)] = otile[pl.ds(r, tg, stride=N_CHUNK), :]


def _peer_v(base, shift, c, x8, x2, tab, t):
    tg = GATHER_BLOCK
    smem = pl.BlockSpec((tg, PEER_SEL), lambda i: (i, 0), memory_space=pltpu.SMEM)
    tokrow = pl.BlockSpec((tg, PEER_SEL), lambda i: (i, 0))
    tile = pl.BlockSpec((tg * SUBLANES, LANES), lambda i: (i, 0))
    return pl.pallas_call(
        _peer_v_kernel, grid=(t // tg,),
        in_specs=[smem, tokrow, tokrow, tile, _table_spec(tab.shape), pl.BlockSpec(memory_space=pl.ANY)],
        out_specs=pl.BlockSpec((tg, D_MODEL), lambda i: (i, 0)),
        out_shape=jax.ShapeDtypeStruct(x2.shape, F32),
        input_output_aliases={5: 0},
        scratch_shapes=[pltpu.VMEM((2 * LANES, LANES), jnp.int32), pltpu.VMEM((2 * LANES, LANES), F32),
                        pltpu.VMEM((tg * SUBLANES, LANES), F32)],
        compiler_params=_params(("parallel",)),
    )(base, shift, c, x8, tab, x2)


SC_WORKERS = 32
SC_LANES = 16
SC_ROWS = 32
SC_TOKENS = 13824
SC_GROUP = 8
SC_REGS = 32


def _peer_sc(eidx, gate, h8, x8, tab_u, tab_v, t_off):
    ts = eidx.shape[0]
    per_w = ts // SC_WORKERS
    n_chunk = PEER_SEL // SC_ROWS
    grp = SC_GROUP
    sel = grp * PEER_SEL
    n_pairs = grp * n_chunk // 2
    per_tile = LANES // SC_LANES
    n_blk = D_MODEL // (SC_REGS * SC_LANES)
    mesh = plsc.VectorSubcoreMesh(core_axis_name="c", subcore_axis_name="s")

    def piece(ref, lead, q):
        return ref[lead, q // per_tile, pl.ds(SC_LANES * (q % per_tile), SC_LANES)]

    def body(u_hbm, v_hbm, idx_hbm, gate_hbm, h_hbm, x_hbm, out_hbm,
             idx_v, c_v, h_v, acc_v, part_v, rows0, rows1, sem0, sem1):
        wid = lax.axis_index("s") * 2 + lax.axis_index("c")
        bufs, sems = (rows0, rows1), (sem0, sem1)
        lane = lax.iota(jnp.int32, SC_LANES)

        def stream(tab_hbm, compute):
            def gather(chunk, slot):
                start = pl.multiple_of(chunk * SC_ROWS, SC_ROWS)
                return pltpu.make_async_copy(tab_hbm.at[idx_v.at[pl.ds(start, SC_ROWS)]], bufs[slot], sems[slot])

            gather(0, 0).start()

            @pl.loop(0, n_pairs)
            def _(p):
                c0 = 2 * p
                gather(c0 + 1, 1).start()
                gather(c0, 0).wait()
                compute(c0, 0)
                gather(jnp.minimum(c0 + 2, 2 * n_pairs - 1), 0).start()
                gather(c0 + 1, 1).wait()
                compute(c0 + 1, 1)

            gather(0, 0).wait()

        def dots(chunk, slot):
            tok = chunk // n_chunk
            for blk in range(n_blk):
                where = [blk * SC_REGS + j for j in range(SC_REGS)]
                hs = [piece(h_v, tok, q) for q in where]

                def row(k, carry):
                    parts = [None] * 4
                    for i, q in enumerate(where):
                        term = piece(bufs[slot], k, q) * hs[i]
                        parts[i % 4] = term if parts[i % 4] is None else parts[i % 4] + term
                    tot = (parts[0] + parts[1]) + (parts[2] + parts[3])
                    at = pl.ds(pl.multiple_of((chunk * SC_ROWS + k) * SC_LANES, SC_LANES), SC_LANES)
                    if blk == 0:
                        part_v[at] = tot
                    else:
                        part_v[at] = part_v[at] + tot
                    return carry

                lax.fori_loop(0, SC_ROWS, row, 0)

        def weigh(chunk, slot):
            tok = chunk // n_chunk
            for blk in range(n_blk):
                where = [blk * SC_REGS + j for j in range(SC_REGS)]

                def row(k, accs):
                    ck = plsc.load_gather(c_v, [jnp.full((SC_LANES,), chunk * SC_ROWS, jnp.int32) + k])
                    return tuple(a + ck * piece(bufs[slot], k, q) for a, q in zip(accs, where))

                accs = lax.fori_loop(0, SC_ROWS, row, tuple(piece(acc_v, tok, q) for q in where))
                for a, q in zip(accs, where):
                    acc_v[tok, q // per_tile, pl.ds(SC_LANES * (q % per_tile), SC_LANES)] = a

        @pl.loop(0, per_w // grp)
        def _(g):
            t0 = wid * per_w + g * grp
            flat = pl.ds(pl.multiple_of(t0 * PEER_SEL, sel), sel)
            pltpu.sync_copy(idx_hbm.at[flat], idx_v)
            pltpu.sync_copy(gate_hbm.at[flat], c_v)
            pltpu.sync_copy(h_hbm.at[pl.ds(t_off + t0, grp)], h_v)
            pltpu.sync_copy(x_hbm.at[pl.ds(t_off + t0, grp)], acc_v)
            stream(u_hbm, dots)

            @pl.loop(0, sel // SC_LANES)
            def _(m):
                base = (m * SC_LANES + lane) * SC_LANES
                score = plsc.load_gather(part_v, [base])
                for l in range(1, SC_LANES):
                    score = score + plsc.load_gather(part_v, [base + l])
                z = math.sqrt(2.0 / math.pi) * (score + 0.044715 * (score * score * score))
                tanh_z = 1.0 - 2.0 / (jnp.exp(2.0 * z) + 1.0)
                at = pl.ds(pl.multiple_of(m * SC_LANES, SC_LANES), SC_LANES)
                c_v[at] = c_v[at] * (0.5 * score * (1.0 + tanh_z))

            stream(v_hbm, weigh)
            pltpu.sync_copy(acc_v, out_hbm.at[pl.ds(t0, grp)])

    tile = lambda n: pltpu.VMEM((n, N_CHUNK, LANES), F32)
    out = pl.kernel(
        body, mesh=mesh, out_type=jax.ShapeDtypeStruct((ts, N_CHUNK, LANES), F32),
        scratch_types=[pltpu.VMEM((sel,), jnp.int32), pltpu.VMEM((sel,), F32), tile(grp), tile(grp),
                       pltpu.VMEM((sel * SC_LANES,), F32), tile(SC_ROWS), tile(SC_ROWS),
                       pltpu.SemaphoreType.DMA, pltpu.SemaphoreType.DMA],
        compiler_params=pltpu.CompilerParams(needs_layout_passes=False, use_tc_tiling_on_sc=True),
    )(tab_u, tab_v, eidx.reshape(-1), gate.reshape(-1),
      h8.reshape(-1, N_CHUNK, LANES), x8.reshape(-1, N_CHUNK, LANES))
    return out


def _final_norm_kernel(x_ref, g_ref, o_ref):
    o_ref[...] = _rms(x_ref[...], g_ref[...])


def _final_norm(x2, g):
    t, d = x2.shape
    tm = min(PROJ_BLOCK, t)
    spec = pl.BlockSpec((tm, d), lambda i: (i, 0))
    return pl.pallas_call(
        _final_norm_kernel, grid=(t // tm,), in_specs=[spec, _full((1, d))], out_specs=spec,
        out_shape=jax.ShapeDtypeStruct((t, d), F32), compiler_params=_params(("parallel",)),
    )(x2, g.reshape(1, d).astype(F32))


def kernel(x, mem, norm_mix, w_in, rw_mu, rw_w0, rw_w2, rw_a0, rw_a2, rw_g2, rw_kk, rw_ka, rw_rk, rw_v0, rw_v1, rw_v2, rw_lnx_g, rw_lnx_b, s5_a_re, s5_a_im, s5_log_dt, s5_b_re, s5_b_im, s5_c_re, s5_c_im, s5_d, s5_glu_w, s5_glu_b, s5_out_g, w_out, norm_xa, norm_mem, xa_wq, xa_wk, xa_wv, xa_wo, norm_ffn, peer_wq, peer_keys, peer_u, peer_v, norm_final):
    bsz, seq, d = x.shape
    t = bsz * seq
    depth = w_in.shape[0]
    mem2 = mem.reshape(bsz * N_MEM, d)
    v_first = None
    for l in range(depth):
        x2 = x.reshape(t, d)
        w_rw = w_in[l][:, :RW_COLS].astype(BF16)
        w_s5 = w_in[l][:, RW_COLS:].astype(BF16)
        ws = [w_rw, w_s5]
        if l > 0:
            ws.append(_pad_rows(rw_v1[l - 1].T, LANES, 0).T.astype(BF16))
        outs = _norm_proj(x2, norm_mix[l], ws, [F32] * len(ws), PROJ_BLOCK)
        z_rw = outs[0].reshape(bsz, seq, RW_COLS)
        u_s5 = outs[1].reshape(bsz, seq, D_S5)
        hv = outs[2].reshape(bsz, seq, LANES) if l > 0 else None
        rw_prm = dict(mu=rw_mu[l], w0=rw_w0[l], w2=rw_w2[l], a0=rw_a0[l], a2=rw_a2[l], g2=rw_g2[l],
                      kk=rw_kk[l], ka=rw_ka[l], rk=rw_rk[l], lng=rw_lnx_g[l], lnb=rw_lnx_b[l])
        if l > 0:
            rw_prm.update(v0=rw_v0[l - 1], v2=rw_v2[l - 1])
        y_rw, v_first = _rwkv(z_rw, hv, v_first, rw_prm, bsz, seq)
        s5_prm = dict(a_re=s5_a_re[l], a_im=s5_a_im[l], log_dt=s5_log_dt[l], b_re=s5_b_re[l], b_im=s5_b_im[l],
                      c_re=s5_c_re[l], c_im=s5_c_im[l], d=s5_d[l], glu_w=s5_glu_w[l], glu_b=s5_glu_b[l],
                      out_g=s5_out_g[l])
        y_s5 = _s5(u_s5, s5_prm, bsz, seq)
        kv = _norm_proj(mem2, norm_mem[l], [xa_wk[l].astype(BF16), xa_wv[l].astype(BF16)], [BF16, BF16],
                        PROJ_BLOCK)
        kmem = kv[0].reshape(bsz, N_MEM, d)
        vmem = kv[1].reshape(bsz, N_MEM, d)
        x = _mix_xattn(x, y_rw, y_s5, w_out[l], norm_xa[l], xa_wq[l], kmem, vmem, xa_wo[l], bsz, seq)
        x2 = x.reshape(t, d)
        x8, h8, base, shift, gate = _peer_select(x2, norm_ffn[l], peer_wq[l], peer_keys[l])
        t_tc = t - SC_TOKENS
        u3, v3 = _tile_table(peer_u[l]), _tile_table(peer_v[l])
        eidx_sc = (base[t_tc:] >> 2) + (shift[t_tc:] >> 4)
        x_sc = _peer_sc(eidx_sc, gate[t_tc:], h8, x8, u3, v3, t_tc)
        c = _peer_u(base, shift, h8, gate, _pack_table(u3), t_tc)
        x_new = _peer_v(base, shift, c, x8, x2, _pack_table(v3), t_tc)
        x = lax.dynamic_update_slice(x_new, x_sc.reshape(SC_TOKENS, d), (t_tc, 0)).reshape(bsz, seq, d)
    return _final_norm(x.reshape(t, d), norm_final).reshape(bsz, seq, d)
```

```python
import functools
import math

import jax
import jax.numpy as jnp
from jax import lax
from jax.experimental import pallas as pl
from jax.experimental.pallas import tpu as pltpu
from jax.experimental.pallas import tpu_sc as plsc

F32 = jnp.float32
BF16 = jnp.bfloat16
U32 = jnp.uint32

LANES = 128
SUBLANES = 8
VMEM_LIMIT = 56 * 1024 * 1024

D_MODEL = 1024
D_RWKV = 512
RW_HEAD = 64
RW_COLS = 1792
D_S5 = 512
S5_GROUPS = 32
S5_CH = 16
S5_STATE = 64
S5_MODES = S5_GROUPS * S5_STATE
N_MEM = 256
XA_HEADS = 4
XA_HEAD = 256
PEER_HEADS = 8
PEER_NKEYS = 128
PEER_TOPK = 16
PEER_SEL = PEER_HEADS * PEER_TOPK
RMS_EPS = 1e-6
GN_EPS = 64e-5

RW_CHUNK = 128
S5_BLOCK = 256
PROJ_BLOCK = 512
XA_BLOCK = 256
SEL_BLOCK = 128
GATHER_BLOCK = 64


def _params(sem):
    return pltpu.CompilerParams(dimension_semantics=sem, vmem_limit_bytes=VMEM_LIMIT)


def _rms(x, g):
    ms = jnp.mean(x * x, axis=-1, keepdims=True)
    return x * lax.rsqrt(ms + RMS_EPS) * g


def _bdot(a, b):
    return jnp.dot(a.astype(BF16), b.astype(BF16), preferred_element_type=F32)


def _bdot_nt(a, b):
    return lax.dot_general(a.astype(BF16), b.astype(BF16), (((1,), (1,)), ((), ())),
                           preferred_element_type=F32)


def _sigmoid(x):
    return 1.0 / (1.0 + jnp.exp(-x))


def _softplus(x):
    return jnp.maximum(x, 0.0) + jnp.log(1.0 + jnp.exp(-jnp.abs(x)))


def _gelu(x):
    return 0.5 * x * (1.0 + jnp.tanh(math.sqrt(2.0 / math.pi) * (x + 0.044715 * (x * x * x))))


def _full(shape):
    n = len(shape)
    return pl.BlockSpec(shape, lambda *_: (0,) * n)


def _norm_proj_kernel(*refs, n_out):
    x_ref, g_ref = refs[0], refs[1]
    w_refs = refs[2:2 + n_out]
    o_refs = refs[2 + n_out:]
    h = _rms(x_ref[...], g_ref[...]).astype(BF16)
    for w_ref, o_ref in zip(w_refs, o_refs):
        o_ref[...] = jnp.dot(h, w_ref[...], preferred_element_type=F32).astype(o_ref.dtype)


def _norm_proj(x2, g, ws, out_dtypes, block):
    t, d = x2.shape
    block = min(block, t)
    in_specs = [pl.BlockSpec((block, d), lambda i: (i, 0)), _full((1, d))]
    in_specs += [_full(w.shape) for w in ws]
    out_specs = [pl.BlockSpec((block, w.shape[1]), lambda i: (i, 0)) for w in ws]
    out_shape = [jax.ShapeDtypeStruct((t, w.shape[1]), dt) for w, dt in zip(ws, out_dtypes)]
    return pl.pallas_call(
        functools.partial(_norm_proj_kernel, n_out=len(ws)),
        grid=(t // block,), in_specs=in_specs, out_specs=out_specs, out_shape=out_shape,
        compiler_params=_params(("parallel",)),
    )(x2, g.reshape(1, d), *ws)


def _seg_sum(x, mseg):
    hi = x.astype(BF16)
    lo = (x - hi.astype(F32)).astype(BF16)
    return (jnp.dot(hi, mseg, preferred_element_type=F32)
            + jnp.dot(lo, mseg, preferred_element_type=F32))


def _col_bcast(row):
    return jnp.broadcast_to(row, (LANES, LANES)).T


def _rwkv_kernel(*refs, first_layer):
    if first_layer:
        (z_ref, mu_ref, w0_ref, w2_ref, a0_ref, a2_ref, g2_ref, kk_ref, ka_ref, rk_ref,
         lng_ref, lnb_ref, mseg_ref, y_ref, vf_out_ref, zprev, hst) = refs
    else:
        (z_ref, hv_ref, vf_ref, v0_ref, v2_ref, mu_ref, w0_ref, w2_ref, a0_ref, a2_ref, g2_ref,
         kk_ref, ka_ref, rk_ref, lng_ref, lnb_ref, mseg_ref, y_ref, zprev, hst) = refs
    L = RW_CHUNK

    @pl.when(pl.program_id(1) == 0)
    def _():
        zprev[...] = jnp.zeros_like(zprev)
        hst[...] = jnp.zeros_like(hst)

    z = z_ref[0]
    row = lax.broadcasted_iota(jnp.int32, (L, 1), 0)
    zs = jnp.where(row == 0, zprev[...], pltpu.roll(z, 1, 0))
    zprev[...] = z[L - 1:L, :]
    z = z + (zs - z) * mu_ref[...]
    r = z[:, 0:512]
    k = z[:, 512:1024]
    v = z[:, 1024:1536]
    wa = z[:, 1536:1664]
    gd = z[:, 1664:1792]
    mseg = mseg_ref[...]

    wlin = w0_ref[...] + _bdot(jnp.tanh(wa), w2_ref[...])
    lw = -jnp.exp(-_softplus(-wlin) - 0.5)
    a = _sigmoid(a0_ref[...] + _bdot(wa, a2_ref[...]))
    g = _bdot(_sigmoid(gd), g2_ref[...])
    if first_layer:
        vf_out_ref[0] = v
    else:
        v = v + (vf_ref[0] - v) * _sigmoid(v0_ref[...] + _bdot(hv_ref[0], v2_ref[...]))
    kk = k * kk_ref[...]
    kk = kk / jnp.maximum(jnp.sqrt(_seg_sum(kk * kk, mseg)), 1e-12)
    k2 = k * (1.0 + (a - 1.0) * ka_ref[...])
    av = -kk
    bv = kk * a

    ti = lax.broadcasted_iota(jnp.int32, (L, L), 0)
    si = lax.broadcasted_iota(jnp.int32, (L, L), 1)
    tril = (ti >= si).astype(F32)
    cum = jnp.dot(tril, lw, preferred_element_type=F32, precision=lax.Precision.HIGHEST)
    mid = cum[L // 2 - 1:L // 2, :]
    cm = cum - mid
    ecum = jnp.exp(cm)
    einv = jnp.exp(-cm)
    rt = r * ecum
    kt = k2 * einv
    bt = bv * einv
    at = av * jnp.exp(cm - lw)
    p_end = jnp.exp(cum[L - 1:L, :])
    e_end = ecum[L - 1:L, :]
    e_mid = jnp.exp(mid)

    lane = lax.broadcasted_iota(jnp.int32, (1, LANES), 1)
    m0 = (lane < RW_HEAD).astype(F32)
    m1 = 1.0 - m0
    strict = ti > si
    incl = ti >= si
    bi = lax.broadcasted_iota(jnp.int32, (LANES, LANES), 0) < RW_HEAD
    bj = lax.broadcasted_iota(jnp.int32, (LANES, LANES), 1) < RW_HEAD
    bdmask = (bi == bj).astype(F32)
    zeros_ll = jnp.zeros((L, L), F32)

    ys = []
    for p in range(D_RWKV // LANES):
        sl = slice(LANES * p, LANES * (p + 1))
        A, B, K, R, V = at[:, sl], bt[:, sl], kt[:, sl], rt[:, sl], v[:, sl]
        btkt = jnp.concatenate([B.T, K.T], axis=1)
        h0 = hst[p]
        h0m = h0 * _col_bcast(e_mid[:, sl])
        sc = _bdot(jnp.concatenate([A * m0, A * m1, R * m0, R * m1], axis=0), btkt)
        aab = [jnp.where(strict, sc[e * L:(e + 1) * L, 0:L], 0.0) for e in range(2)]
        aak = [jnp.where(strict, sc[e * L:(e + 1) * L, L:2 * L], 0.0) for e in range(2)]
        arb = [jnp.where(incl, sc[(2 + e) * L:(3 + e) * L, 0:L], 0.0) for e in range(2)]
        ark = [jnp.where(incl, sc[(2 + e) * L:(3 + e) * L, L:2 * L], 0.0) for e in range(2)]
        arh = _bdot(jnp.concatenate([A, R], axis=0), h0m)
        v01 = jnp.concatenate([V * m0, V * m1], axis=0)
        x = arh[0:L] + _bdot(jnp.concatenate(aak, axis=1), v01)
        pm = jnp.concatenate(aab, axis=1)
        n_fac = int(math.log2(L))
        for it in range(n_fac):
            x = x + _bdot(pm, jnp.concatenate([x * m0, x * m1], axis=0))
            if it + 1 < n_fac:
                pd = jnp.concatenate(
                    [jnp.concatenate([pm[:, 0:L], zeros_ll], axis=1),
                     jnp.concatenate([zeros_ll, pm[:, L:2 * L]], axis=1)], axis=0)
                pm = _bdot(pm, pd)
        u = x
        yp = arh[L:2 * L] + _bdot(jnp.concatenate(arb + ark, axis=1),
                                  jnp.concatenate([u * m0, u * m1, v01], axis=0))
        upd = _bdot(btkt, jnp.concatenate([u, V], axis=0))
        hst[p] = (h0 * _col_bcast(p_end[:, sl]) + upd * _col_bcast(e_end[:, sl])) * bdmask
        ys.append(yp)
    y = jnp.concatenate(ys, axis=1)

    mean = _seg_sum(y, mseg) * (1.0 / RW_HEAD)
    d = y - mean
    var = _seg_sum(d * d, mseg) * (1.0 / RW_HEAD)
    yn = d * lax.rsqrt(var + GN_EPS) * lng_ref[...] + lnb_ref[...]
    bonus = _seg_sum(r * k2 * rk_ref[...], mseg) * v
    y_ref[0] = (yn + bonus) * g


def _pad_rows(w, rows, offset):
    out = jnp.zeros((rows, w.shape[1]), w.dtype)
    return out.at[offset:offset + w.shape[0]].set(w)


def _rwkv(z_rw, hv, v_first, prm, bsz, seq):
    L = RW_CHUNK
    first = v_first is None
    row = lambda a: a.reshape(1, -1).astype(F32)
    hid = jnp.arange(D_RWKV) // RW_HEAD
    mseg = (hid[:, None] == hid[None, :]).astype(BF16)
    w2p = _pad_rows(prm['w2'], LANES, 0).astype(BF16)
    a2p = _pad_rows(prm['a2'], LANES, 64).astype(BF16)
    common = [row(prm['mu']), row(prm['w0']), w2p, row(prm['a0']), a2p, prm['g2'].astype(BF16),
              row(prm['kk']), row(prm['ka']), row(prm['rk']), row(prm['lng']), row(prm['lnb']), mseg]
    tok = lambda w: pl.BlockSpec((1, L, w), lambda b, t: (b, t, 0))
    common_specs = [_full(c.shape) for c in common]
    y_shape = jax.ShapeDtypeStruct((bsz, seq, D_RWKV), F32)
    scratch = [pltpu.VMEM((1, RW_COLS), F32), pltpu.VMEM((D_RWKV // LANES, LANES, LANES), F32)]
    if first:
        args = [z_rw] + common
        in_specs = [tok(RW_COLS)] + common_specs
        out_shape = [y_shape, y_shape]
        out_specs = [tok(D_RWKV), tok(D_RWKV)]
    else:
        v2p = _pad_rows(prm['v2'], LANES, 0).astype(BF16)
        extra = [row(prm['v0']), v2p]
        args = [z_rw, hv, v_first] + extra + common
        in_specs = [tok(RW_COLS), tok(LANES), tok(D_RWKV)] + [_full(c.shape) for c in extra] + common_specs
        out_shape = [y_shape]
        out_specs = [tok(D_RWKV)]
    outs = pl.pallas_call(
        functools.partial(_rwkv_kernel, first_layer=first),
        grid=(bsz, seq // L), in_specs=in_specs, out_specs=out_specs, out_shape=out_shape,
        scratch_shapes=scratch, compiler_params=_params(("parallel", "arbitrary")),
    )(*args)
    return (outs[0], outs[1]) if first else (outs[0], v_first)


def _s5_kernel(u_ref, wb_ref, wc_ref, lpr_ref, lpi_ref, d_ref, gw_ref, gb_ref, og_ref, o_ref,
               car_re, car_im, xre, xim):
    tb = u_ref.shape[1]

    @pl.when(pl.program_id(1) == 0)
    def _():
        car_re[...] = jnp.zeros_like(car_re)
        car_im[...] = jnp.zeros_like(car_im)

    u = u_ref[0]
    bu = _bdot(u, wb_ref[...])
    xre[...] = bu[:, 0:S5_MODES]
    xim[...] = bu[:, S5_MODES:2 * S5_MODES]
    row = lax.broadcasted_iota(jnp.int32, (SUBLANES, 1), 0)

    def tile(i, carry):
        cr, ci = carry
        rs = pl.ds(pl.multiple_of(i * SUBLANES, SUBLANES), SUBLANES)
        br, bi = xre[rs, :], xim[rs, :]
        for dist in (1, 2, 4):
            keep = row >= dist
            sr = jnp.where(keep, pltpu.roll(br, dist, 0), 0.0)
            si = jnp.where(keep, pltpu.roll(bi, dist, 0), 0.0)
            lr = lpr_ref[dist - 1:dist, :]
            li = lpi_ref[dist - 1:dist, :]
            br, bi = br + lr * sr - li * si, bi + lr * si + li * sr
        pr, pi = lpr_ref[...], lpi_ref[...]
        xr = br + pr * cr - pi * ci
        xi = bi + pr * ci + pi * cr
        xre[rs, :] = xr
        xim[rs, :] = xi
        return xr[SUBLANES - 1:SUBLANES, :], xi[SUBLANES - 1:SUBLANES, :]

    cr, ci = lax.fori_loop(0, tb // SUBLANES, tile, (car_re[...], car_im[...]))
    car_re[...] = cr
    car_im[...] = ci
    wc = wc_ref[...]
    y = _bdot(xre[...], wc[0:S5_MODES]) + _bdot(xim[...], wc[S5_MODES:2 * S5_MODES])
    y = _gelu(y + d_ref[...] * u)
    y = y * _sigmoid(_bdot(y, gw_ref[...]) + gb_ref[...])
    o_ref[0] = _rms(y, og_ref[...])


def _s5_weights(a_re, a_im, log_dt, b_re, b_im, c_re, c_im):
    lam_re = jnp.minimum(a_re.astype(F32), -1e-4)
    lam_im = a_im.astype(F32)
    dt = jnp.exp(log_dt.astype(F32))[:, None]
    mag = jnp.exp(lam_re * dt)
    lb_re = mag * jnp.cos(lam_im * dt)
    lb_im = mag * jnp.sin(lam_im * dt)
    den = lam_re * lam_re + lam_im * lam_im
    c1_re = ((lb_re - 1.0) * lam_re + lb_im * lam_im) / den
    c1_im = (lb_im * lam_re - (lb_re - 1.0) * lam_im) / den
    br, bi = b_re.astype(F32), b_im.astype(F32)
    bb_re = c1_re[..., None] * br - c1_im[..., None] * bi
    bb_im = c1_re[..., None] * bi + c1_im[..., None] * br
    eye = jnp.eye(S5_GROUPS, dtype=F32)
    wb_re = jnp.einsum('gpc,gh->gchp', bb_re, eye).reshape(D_S5, S5_MODES)
    wb_im = jnp.einsum('gpc,gh->gchp', bb_im, eye).reshape(D_S5, S5_MODES)
    wb = jnp.concatenate([wb_re, wb_im], axis=1).astype(BF16)
    wc_re = jnp.einsum('gcp,gh->gphc', c_re.astype(F32), eye).reshape(S5_MODES, D_S5)
    wc_im = jnp.einsum('gcp,gh->gphc', c_im.astype(F32), eye).reshape(S5_MODES, D_S5)
    wc = jnp.concatenate([wc_re, -wc_im], axis=0).astype(BF16)
    pr, pi = [lb_re], [lb_im]
    for _ in range(SUBLANES - 1):
        pr, pi = pr + [pr[-1] * lb_re - pi[-1] * lb_im], pi + [pr[-1] * lb_im + pi[-1] * lb_re]
    lp_re = jnp.stack(pr).reshape(SUBLANES, S5_MODES)
    lp_im = jnp.stack(pi).reshape(SUBLANES, S5_MODES)
    return wb, wc, lp_re, lp_im


def _s5(u, prm, bsz, seq):
    tb = min(S5_BLOCK, seq)
    wb, wc, lp_re, lp_im = _s5_weights(prm['a_re'], prm['a_im'], prm['log_dt'], prm['b_re'], prm['b_im'],
                                       prm['c_re'], prm['c_im'])
    row = lambda a: a.reshape(1, -1).astype(F32)
    consts = [wb, wc, lp_re, lp_im, row(prm['d']), prm['glu_w'].astype(BF16), row(prm['glu_b']),
              row(prm['out_g'])]
    tok = pl.BlockSpec((1, tb, D_S5), lambda b, t: (b, t, 0))
    return pl.pallas_call(
        _s5_kernel, grid=(bsz, seq // tb),
        in_specs=[tok] + [_full(c.shape) for c in consts], out_specs=tok,
        out_shape=jax.ShapeDtypeStruct((bsz, seq, D_S5), F32),
        scratch_shapes=[pltpu.VMEM((1, S5_MODES), F32), pltpu.VMEM((1, S5_MODES), F32),
                        pltpu.VMEM((tb, S5_MODES), F32), pltpu.VMEM((tb, S5_MODES), F32)],
        compiler_params=_params(("parallel", "arbitrary")),
    )(u, *consts)


def _mix_xattn_kernel(x_ref, yr_ref, ys_ref, wo1_ref, wo2_ref, g_ref, wq_ref, k_ref, v_ref, wo_ref, o_ref):
    x1 = x_ref[0] + _bdot(yr_ref[0], wo1_ref[...]) + _bdot(ys_ref[0], wo2_ref[...])
    h = _rms(x1, g_ref[...])
    q = _bdot(h, wq_ref[...])
    km, vm = k_ref[0], v_ref[0]
    outs = []
    for hd in range(XA_HEADS):
        sl = slice(XA_HEAD * hd, XA_HEAD * (hd + 1))
        s = _bdot_nt(q[:, sl], km[:, sl]) * (XA_HEAD ** -0.5)
        s = s - jnp.max(s, axis=-1, keepdims=True)
        e = jnp.exp(s)
        p = e / jnp.sum(e, axis=-1, keepdims=True)
        outs.append(_bdot(p, vm[:, sl]))
    o = jnp.concatenate(outs, axis=1)
    o_ref[0] = x1 + _bdot(o, wo_ref[...])


def _mix_xattn(x, y_rw, y_s5, w_out, g, wq, kmem, vmem, wo, bsz, seq):
    tm = min(XA_BLOCK, seq)
    consts_a = [w_out[:D_RWKV].astype(BF16), w_out[D_RWKV:].astype(BF16), g.reshape(1, -1).astype(F32),
                wq.astype(BF16)]
    tok = lambda w: pl.BlockSpec((1, tm, w), lambda b, t: (b, t, 0))
    mem = pl.BlockSpec((1, N_MEM, D_MODEL), lambda b, t: (b, 0, 0))
    wo_b = wo.astype(BF16)
    return pl.pallas_call(
        _mix_xattn_kernel, grid=(bsz, seq // tm),
        in_specs=[tok(D_MODEL), tok(D_RWKV), tok(D_S5)] + [_full(c.shape) for c in consts_a]
                 + [mem, mem, _full(wo_b.shape)],
        out_specs=tok(D_MODEL), out_shape=jax.ShapeDtypeStruct((bsz, seq, D_MODEL), F32),
        compiler_params=_params(("parallel", "parallel")),
    )(x, y_rw, y_s5, *consts_a, kmem, vmem, wo_b)


def _top_rows(work, order, aux, val_ref, idx_ref):
    for it in range(PEER_TOPK):
        m = jnp.max(work, axis=0, keepdims=True)
        pos = jnp.min(jnp.where(work == m, order, jnp.inf), axis=0, keepdims=True)
        hit = order == pos
        val_ref[it:it + 1, :] = m
        if aux is None:
            idx_ref[it:it + 1, :] = pos
        else:
            idx_ref[it:it + 1, :] = jnp.sum(jnp.where(hit, aux, 0.0), axis=0, keepdims=True)
        work = jnp.where(hit, -jnp.inf, work)


_CAND_ROW_BLOCKS = [(0, PEER_TOPK), (1, SUBLANES), (2, SUBLANES), (3, SUBLANES)]
_CAND_COL_BLOCKS = [(0, PEER_TOPK, 4, 15), (1, SUBLANES, 4, 7), (2, SUBLANES, 4, 4)]
N_CAND = sum(n for _, n in _CAND_ROW_BLOCKS) + sum(n for _, n, _, _ in _CAND_COL_BLOCKS)


def _cand_consts(tm):
    flat, neg = [], []
    for a, nb in _CAND_ROW_BLOCKS:
        flat += [a * PEER_TOPK + b for b in range(nb)]
        neg += [0.0] * nb
    for b, na, lo, hi in _CAND_COL_BLOCKS:
        flat += [a * PEER_TOPK + b for a in range(na)]
        neg += [0.0 if lo <= a <= hi else -float('inf') for a in range(na)]
    flat = [f if n == 0.0 else 1000.0 + i for i, (f, n) in enumerate(zip(flat, neg))]
    col = lambda v: jnp.broadcast_to(jnp.asarray(v, F32)[:, None], (N_CAND, tm))
    return col(flat), col(neg)


def _cand_rows(row_vals, col_vals, combine):
    blocks = [combine(row_vals[a:a + 1, :], col_vals[0:nb, :]) for a, nb in _CAND_ROW_BLOCKS]
    blocks += [combine(row_vals[0:na, :], col_vals[b:b + 1, :]) for b, na, _, _ in _CAND_COL_BLOCKS]
    return jnp.concatenate(blocks, axis=0)


SEL_HEADS_PER_STEP = 2


def _peer_select_kernel(x_ref, g_ref, wq_ref, keys_ref, cflat_ref, cneg_ref, x8_ref, h8_ref, base_ref, shift_ref,
                        gate_ref, q3, idx_t, gate_t, *lists):
    tm = x_ref.shape[0]
    n = SEL_HEADS_PER_STEP
    s1, i1, s2, i2, top, eid = (lists[j * n:(j + 1) * n] for j in range(6))
    x = x_ref[...]
    h = _rms(x, g_ref[...])
    for r in range(D_MODEL // LANES):
        rows = pl.ds(r, tm, stride=D_MODEL // LANES)
        h8_ref[rows, :] = h[:, LANES * r:LANES * (r + 1)]
        x8_ref[rows, :] = x[:, LANES * r:LANES * (r + 1)]
    q = _bdot(h, wq_ref[...])
    for j in range(2 * PEER_HEADS):
        q3[j] = q[:, LANES * j:LANES * (j + 1)]
    iota_k = lax.broadcasted_iota(jnp.int32, (PEER_NKEYS, tm), 0).astype(F32)

    def heads(step, _):
        for u in range(SEL_HEADS_PER_STEP):
            hd = step * SEL_HEADS_PER_STEP + u
            sc1 = _bdot_nt(keys_ref[2 * hd], q3[2 * hd])
            sc2 = _bdot_nt(keys_ref[2 * hd + 1], q3[2 * hd + 1])
            _top_rows(sc1, iota_k, None, s1[u], i1[u])
            _top_rows(sc2, iota_k, None, s2[u], i2[u])
            cand = _cand_rows(s1[u][...], s2[u][...], lambda x, y: x + y) + cneg_ref[...]
            cidx = _cand_rows(i1[u][...], i2[u][...], lambda x, y: x * float(PEER_NKEYS) + y)
            _top_rows(cand, cflat_ref[...], cidx, top[u], eid[u])
            tv = top[u][...]
            e = jnp.exp(tv - jnp.max(tv, axis=0, keepdims=True))
            rs = pl.ds(pl.multiple_of(hd * PEER_TOPK, PEER_TOPK), PEER_TOPK)
            idx_t[rs, :] = eid[u][...]
            gate_t[rs, :] = e / jnp.sum(e, axis=0, keepdims=True)
        return 0

    lax.fori_loop(0, PEER_HEADS // SEL_HEADS_PER_STEP, heads, 0)
    e_t = idx_t[...].T
    pair = jnp.floor(e_t * 0.5)
    base_ref[...] = (pair * float(SUBLANES)).astype(jnp.int32)
    shift_ref[...] = ((e_t - 2.0 * pair) * 16.0).astype(jnp.int32)
    gate_ref[...] = gate_t[...].T


def _peer_select(x2, g, wq, keys):
    t = x2.shape[0]
    tm = SEL_BLOCK
    keys_b = keys.reshape(2 * PEER_HEADS, PEER_NKEYS, LANES).astype(BF16)
    wq_b = wq.astype(BF16)
    cflat, cneg = _cand_consts(tm)
    tokspec = lambda w: pl.BlockSpec((tm, w), lambda i: (i, 0))
    vm = lambda r: pltpu.VMEM((r, tm), F32)
    return pl.pallas_call(
        _peer_select_kernel, grid=(t // tm,),
        in_specs=[tokspec(D_MODEL), _full((1, D_MODEL)), _full(wq_b.shape), _full(keys_b.shape),
                  _full(cflat.shape), _full(cneg.shape)],
        out_specs=[pl.BlockSpec((tm * SUBLANES, LANES), lambda i: (i, 0))] * 2
                  + [tokspec(PEER_SEL), tokspec(PEER_SEL), tokspec(PEER_SEL)],
        out_shape=[jax.ShapeDtypeStruct((t * SUBLANES, LANES), F32)] * 2 + [
                   jax.ShapeDtypeStruct((t, PEER_SEL), jnp.int32),
                   jax.ShapeDtypeStruct((t, PEER_SEL), jnp.int32), jax.ShapeDtypeStruct((t, PEER_SEL), F32)],
        scratch_shapes=[pltpu.VMEM((2 * PEER_HEADS, tm, LANES), F32), vm(PEER_SEL), vm(PEER_SEL)]
                       + [vm(PEER_TOPK) for _ in range(6 * SEL_HEADS_PER_STEP)],
        compiler_params=_params(("parallel",)),
    )(x2, g.reshape(1, -1).astype(F32), wq_b, keys_b, cflat, cneg)


def _tile_table(tab):
    return tab.reshape(tab.shape[0], D_MODEL // LANES, LANES)


def _pack_table(tab3):
    n = tab3.shape[0]
    bits = lax.bitcast_convert_type(tab3.astype(BF16), jnp.uint16).astype(U32)
    bits = bits.reshape(n // 2, 2, SUBLANES, LANES)
    packed = (bits[:, 0] << 16) | bits[:, 1]
    return lax.bitcast_convert_type(packed, jnp.int32).reshape(n // 2 * SUBLANES, LANES)


def _splat_into(src_ref, t, dst_ref, slot):
    tile = jnp.broadcast_to(src_ref[pl.ds(t, 1), :], (LANES, LANES)).T
    dst_ref[LANES * slot:LANES * (slot + 1), :] = tile


def _bcast_row(ref, row):
    return jnp.broadcast_to(ref[row:row + 1, :], (SUBLANES, LANES))


def _expert_tile(tab_ref, base, shift_splat, row):
    w = tab_ref[pl.ds(pl.multiple_of(base, SUBLANES), SUBLANES), :]
    return lax.bitcast_convert_type((w << _bcast_row(shift_splat, row)) & jnp.int32(-65536), F32)


def _token_rows(t):
    return pl.ds(pl.multiple_of(t * SUBLANES, SUBLANES), SUBLANES)


def _pipelined_tokens(tg, prep, work):
    prep(0, 0)

    def body(i, _):
        t0 = 2 * i
        prep(t0 + 1, 1)
        work(t0, 0)
        prep(jnp.minimum(t0 + 2, tg - 1), 0)
        work(t0 + 1, 1)
        return 0

    lax.fori_loop(0, tg // 2, body, 0)


def _table_spec(shape):
    return pl.BlockSpec(shape, lambda i: (0, 0), pipeline_mode=pl.Buffered(1))


N_CHUNK = D_MODEL // LANES
CHUNK_STRIDE = PEER_SEL + SUBLANES


def _peer_u_kernel(base_ref, shift_ref, h_ref, gate_ref, tab_ref, eye_ref, c_ref, shift_splat, *planes):
    tg = gate_ref.shape[0]
    ones = jnp.ones((LANES, LANES), BF16)

    half = N_CHUNK // 2

    def prep(t, slot):
        _splat_into(shift_ref, t, shift_splat, slot)

    def gather(t, slot):
        ht = h_ref[_token_rows(t), :]
        for kx in range(PEER_SEL):
            prod = _expert_tile(tab_ref, base_ref[t, kx], shift_splat, LANES * slot + kx) * ht
            fold = prod + pltpu.roll(prod, half, 0)
            planes[slot][pl.ds(kx, half, stride=CHUNK_STRIDE), :] = fold[0:half, :]

    def finish(t, slot):
        plane = planes[slot]
        acc = plane[0:PEER_SEL, :]
        for r in range(1, half):
            acc = acc + plane[CHUNK_STRIDE * r:CHUNK_STRIDE * r + PEER_SEL, :]
        hi = acc.astype(BF16)
        lo = (acc - hi.astype(F32)).astype(BF16)
        tot = jnp.dot(hi, ones, preferred_element_type=F32) + jnp.dot(lo, ones, preferred_element_type=F32)
        score = jnp.sum(tot * eye_ref[...], axis=0, keepdims=True)
        c_ref[pl.ds(t, 1), :] = gate_ref[pl.ds(t, 1), :] * _gelu(score)

    planes[1][...] = jnp.zeros_like(planes[1])
    prep(0, 0)

    def body(i, _):
        t0 = 2 * i
        prep(t0 + 1, 1)
        gather(t0, 0)
        finish(jnp.maximum(t0 - 1, 0), 1)
        prep(jnp.minimum(t0 + 2, tg - 1), 0)
        gather(t0 + 1, 1)
        finish(t0, 0)
        return 0

    lax.fori_loop(0, tg // 2, body, 0)
    finish(tg - 1, 1)


def _peer_u(base, shift, h8, gate, tab, t):
    tg = GATHER_BLOCK
    eye = jnp.eye(LANES, dtype=F32)
    tokrow = pl.BlockSpec((tg, PEER_SEL), lambda i: (i, 0))
    return pl.pallas_call(
        _peer_u_kernel, grid=(t // tg,),
        in_specs=[pl.BlockSpec((tg, PEER_SEL), lambda i: (i, 0), memory_space=pltpu.SMEM), tokrow,
                  pl.BlockSpec((tg * SUBLANES, LANES), lambda i: (i, 0)), tokrow,
                  _table_spec(tab.shape), _full(eye.shape)],
        out_specs=tokrow, out_shape=jax.ShapeDtypeStruct((t, PEER_SEL), F32),
        scratch_shapes=[pltpu.VMEM((2 * LANES, LANES), jnp.int32)]
                       + [pltpu.VMEM((N_CHUNK // 2 * CHUNK_STRIDE, LANES), F32)] * 2,
        compiler_params=_params(("parallel",)),
    )(base, shift, h8, gate, tab, eye)


def _peer_v_kernel(base_ref, shift_ref, c_ref, x_ref, tab_ref, _aliased_out, o_ref, shift_splat, c_splat, otile):
    tg = c_ref.shape[0]
    n_acc = 4

    def prep(t, slot):
        _splat_into(shift_ref, t, shift_splat, slot)
        _splat_into(c_ref, t, c_splat, slot)

    def work(t, slot):
        accs = [jnp.zeros((SUBLANES, LANES), F32) for _ in range(n_acc)]
        for kx in range(PEER_SEL):
            row = LANES * slot + kx
            accs[kx % n_acc] = accs[kx % n_acc] + (_bcast_row(c_splat, row)
                                                   * _expert_tile(tab_ref, base_ref[t, kx], shift_splat, row))
        rs = _token_rows(t)
        otile[rs, :] = x_ref[rs, :] + ((accs[0] + accs[1]) + (accs[2] + accs[3]))

    _pipelined_tokens(tg, prep, work)
    for r in range(N_CHUNK):
        o_ref[:, LANES * r:LANES * (r + 1)] = otile[pl.ds(r, tg, stride=N_CHUNK), :]


def _peer_v(base, shift, c, x8, x2, tab, t):
    tg = GATHER_BLOCK
    smem = pl.BlockSpec((tg, PEER_SEL), lambda i: (i, 0), memory_space=pltpu.SMEM)
    tokrow = pl.BlockSpec((tg, PEER_SEL), lambda i: (i, 0))
    tile = pl.BlockSpec((tg * SUBLANES, LANES), lambda i: (i, 0))
    return pl.pallas_call(
        _peer_v_kernel, grid=(t // tg,),
        in_specs=[smem, tokrow, tokrow, tile, _table_spec(tab.shape), pl.BlockSpec(memory_space=pl.ANY)],
        out_specs=pl.BlockSpec((tg, D_MODEL), lambda i: (i, 0)),
        out_shape=jax.ShapeDtypeStruct(x2.shape, F32),
        input_output_aliases={5: 0},
        scratch_shapes=[pltpu.VMEM((2 * LANES, LANES), jnp.int32), pltpu.VMEM((2 * LANES, LANES), F32),
                        pltpu.VMEM((tg * SUBLANES, LANES), F32)],
        compiler_params=_params(("parallel",)),
    )(base, shift, c, x8, tab, x2)


SC_WORKERS = 32
SC_LANES = 16
SC_ROWS = 32
SC_TOKENS = 22016
SEQ_GROUPS = 2
SC_GROUP = 8
SC_REGS = 32


def _peer_sc(eidx, gate, h8, x8, tab_u, tab_v, t_off):
    ts = eidx.shape[0]
    per_w = ts // SC_WORKERS
    n_chunk = PEER_SEL // SC_ROWS
    grp = SC_GROUP
    sel = grp * PEER_SEL
    n_pairs = grp * n_chunk // 2
    per_tile = LANES // SC_LANES
    n_blk = D_MODEL // (SC_REGS * SC_LANES)
    mesh = plsc.VectorSubcoreMesh(core_axis_name="c", subcore_axis_name="s")

    def piece(ref, lead, q):
        return ref[lead, q // per_tile, pl.ds(SC_LANES * (q % per_tile), SC_LANES)]

    def body(u_hbm, v_hbm, idx_hbm, gate_hbm, h_hbm, x_hbm, out_hbm,
             idx_v, c_v, h_v, acc_v, part_v, rows0, rows1, sem0, sem1):
        wid = lax.axis_index("s") * 2 + lax.axis_index("c")
        bufs, sems = (rows0, rows1), (sem0, sem1)
        lane = lax.iota(jnp.int32, SC_LANES)

        def stream(tab_hbm, compute):
            def gather(chunk, slot):
                start = pl.multiple_of(chunk * SC_ROWS, SC_ROWS)
                return pltpu.make_async_copy(tab_hbm.at[idx_v.at[pl.ds(start, SC_ROWS)]], bufs[slot], sems[slot])

            gather(0, 0).start()

            @pl.loop(0, n_pairs)
            def _(p):
                c0 = 2 * p
                gather(c0 + 1, 1).start()
                gather(c0, 0).wait()
                compute(c0, 0)
                gather(jnp.minimum(c0 + 2, 2 * n_pairs - 1), 0).start()
                gather(c0 + 1, 1).wait()
                compute(c0 + 1, 1)

            gather(0, 0).wait()

        def dots(chunk, slot):
            tok = chunk // n_chunk
            for blk in range(n_blk):
                where = [blk * SC_REGS + j for j in range(SC_REGS)]
                hs = [piece(h_v, tok, q) for q in where]

                def row(k, carry):
                    parts = [None] * 4
                    for i, q in enumerate(where):
                        term = piece(bufs[slot], k, q) * hs[i]
                        parts[i % 4] = term if parts[i % 4] is None else parts[i % 4] + term
                    tot = (parts[0] + parts[1]) + (parts[2] + parts[3])
                    at = pl.ds(pl.multiple_of((chunk * SC_ROWS + k) * SC_LANES, SC_LANES), SC_LANES)
                    if blk == 0:
                        part_v[at] = tot
                    else:
                        part_v[at] = part_v[at] + tot
                    return carry

                lax.fori_loop(0, SC_ROWS, row, 0)

        def weigh(chunk, slot):
            tok = chunk // n_chunk
            for blk in range(n_blk):
                where = [blk * SC_REGS + j for j in range(SC_REGS)]

                def row(k, accs):
                    ck = plsc.load_gather(c_v, [jnp.full((SC_LANES,), chunk * SC_ROWS, jnp.int32) + k])
                    return tuple(a + ck * piece(bufs[slot], k, q) for a, q in zip(accs, where))

                accs = lax.fori_loop(0, SC_ROWS, row, tuple(piece(acc_v, tok, q) for q in where))
                for a, q in zip(accs, where):
                    acc_v[tok, q // per_tile, pl.ds(SC_LANES * (q % per_tile), SC_LANES)] = a

        @pl.loop(0, per_w // grp)
        def _(g):
            t0 = wid * per_w + g * grp
            flat = pl.ds(pl.multiple_of(t0 * PEER_SEL, sel), sel)
            pltpu.sync_copy(idx_hbm.at[flat], idx_v)
            pltpu.sync_copy(gate_hbm.at[flat], c_v)
            pltpu.sync_copy(h_hbm.at[pl.ds(t_off + t0, grp)], h_v)
            pltpu.sync_copy(x_hbm.at[pl.ds(t_off + t0, grp)], acc_v)
            stream(u_hbm, dots)

            @pl.loop(0, sel // SC_LANES)
            def _(m):
                base = (m * SC_LANES + lane) * SC_LANES
                score = plsc.load_gather(part_v, [base])
                for l in range(1, SC_LANES):
                    score = score + plsc.load_gather(part_v, [base + l])
                z = math.sqrt(2.0 / math.pi) * (score + 0.044715 * (score * score * score))
                tanh_z = 1.0 - 2.0 / (jnp.exp(2.0 * z) + 1.0)
                at = pl.ds(pl.multiple_of(m * SC_LANES, SC_LANES), SC_LANES)
                c_v[at] = c_v[at] * (0.5 * score * (1.0 + tanh_z))

            stream(v_hbm, weigh)
            pltpu.sync_copy(acc_v, out_hbm.at[pl.ds(t0, grp)])

    tile = lambda n: pltpu.VMEM((n, N_CHUNK, LANES), F32)
    out = pl.kernel(
        body, mesh=mesh, out_type=jax.ShapeDtypeStruct((ts, N_CHUNK, LANES), F32),
        scratch_types=[pltpu.VMEM((sel,), jnp.int32), pltpu.VMEM((sel,), F32), tile(grp), tile(grp),
                       pltpu.VMEM((sel * SC_LANES,), F32), tile(SC_ROWS), tile(SC_ROWS),
                       pltpu.SemaphoreType.DMA, pltpu.SemaphoreType.DMA],
        compiler_params=pltpu.CompilerParams(needs_layout_passes=False, use_tc_tiling_on_sc=True),
    )(tab_u, tab_v, eidx.reshape(-1), gate.reshape(-1),
      h8.reshape(-1, N_CHUNK, LANES), x8.reshape(-1, N_CHUNK, LANES))
    return out


def _final_norm_kernel(x_ref, g_ref, o_ref):
    o_ref[...] = _rms(x_ref[...], g_ref[...])


def _final_norm(x2, g):
    t, d = x2.shape
    tm = min(PROJ_BLOCK, t)
    spec = pl.BlockSpec((tm, d), lambda i: (i, 0))
    return pl.pallas_call(
        _final_norm_kernel, grid=(t // tm,), in_specs=[spec, _full((1, d))], out_specs=spec,
        out_shape=jax.ShapeDtypeStruct((t, d), F32), compiler_params=_params(("parallel",)),
    )(x2, g.reshape(1, d).astype(F32))


def kernel(x, mem, norm_mix, w_in, rw_mu, rw_w0, rw_w2, rw_a0, rw_a2, rw_g2, rw_kk, rw_ka, rw_rk, rw_v0, rw_v1, rw_v2, rw_lnx_g, rw_lnx_b, s5_a_re, s5_a_im, s5_log_dt, s5_b_re, s5_b_im, s5_c_re, s5_c_im, s5_d, s5_glu_w, s5_glu_b, s5_out_g, w_out, norm_xa, norm_mem, xa_wq, xa_wk, xa_wv, xa_wo, norm_ffn, peer_wq, peer_keys, peer_u, peer_v, norm_final):
    bsz, seq, d = x.shape
    depth = w_in.shape[0]

    def layer(l, xg, memg, v_first, tabs):
        nb = xg.shape[0]
        t = nb * seq
        x2 = xg.reshape(t, d)
        ws = [w_in[l][:, :RW_COLS].astype(BF16), w_in[l][:, RW_COLS:].astype(BF16)]
        if l > 0:
            ws.append(_pad_rows(rw_v1[l - 1].T, LANES, 0).T.astype(BF16))
        outs = _norm_proj(x2, norm_mix[l], ws, [F32] * len(ws), PROJ_BLOCK)
        z_rw = outs[0].reshape(nb, seq, RW_COLS)
        u_s5 = outs[1].reshape(nb, seq, D_S5)
        hv = outs[2].reshape(nb, seq, LANES) if l > 0 else None
        rw_prm = dict(mu=rw_mu[l], w0=rw_w0[l], w2=rw_w2[l], a0=rw_a0[l], a2=rw_a2[l], g2=rw_g2[l],
                      kk=rw_kk[l], ka=rw_ka[l], rk=rw_rk[l], lng=rw_lnx_g[l], lnb=rw_lnx_b[l])
        if l > 0:
            rw_prm.update(v0=rw_v0[l - 1], v2=rw_v2[l - 1])
        y_rw, v_first = _rwkv(z_rw, hv, v_first, rw_prm, nb, seq)
        s5_prm = dict(a_re=s5_a_re[l], a_im=s5_a_im[l], log_dt=s5_log_dt[l], b_re=s5_b_re[l], b_im=s5_b_im[l],
                      c_re=s5_c_re[l], c_im=s5_c_im[l], d=s5_d[l], glu_w=s5_glu_w[l], glu_b=s5_glu_b[l],
                      out_g=s5_out_g[l])
        y_s5 = _s5(u_s5, s5_prm, nb, seq)
        kv = _norm_proj(memg.reshape(nb * N_MEM, d), norm_mem[l],
                        [xa_wk[l].astype(BF16), xa_wv[l].astype(BF16)], [BF16, BF16], PROJ_BLOCK)
        kmem = kv[0].reshape(nb, N_MEM, d)
        vmem = kv[1].reshape(nb, N_MEM, d)
        xg = _mix_xattn(xg, y_rw, y_s5, w_out[l], norm_xa[l], xa_wq[l], kmem, vmem, xa_wo[l], nb, seq)
        x2 = xg.reshape(t, d)
        x8, h8, base, shift, gate = _peer_select(x2, norm_ffn[l], peer_wq[l], peer_keys[l])
        u3, v3, u_packed, v_packed = tabs
        n_sc = SC_TOKENS * nb // bsz
        t_tc = t - n_sc
        eidx_sc = (base[t_tc:] >> 2) + (shift[t_tc:] >> 4)
        x_sc = _peer_sc(eidx_sc, gate[t_tc:], h8, x8, u3, v3, t_tc)
        c = _peer_u(base, shift, h8, gate, u_packed, t_tc)
        x_new = _peer_v(base, shift, c, x8, x2, v_packed, t_tc)
        xg = lax.dynamic_update_slice(x_new, x_sc.reshape(n_sc, d), (t_tc, 0)).reshape(nb, seq, d)
        return xg, v_first

    nb = bsz // SEQ_GROUPS
    xs = [x[g * nb:(g + 1) * nb] for g in range(SEQ_GROUPS)]
    mems = [mem[g * nb:(g + 1) * nb] for g in range(SEQ_GROUPS)]
    v_firsts = [None] * SEQ_GROUPS
    for l in range(depth):
        u3, v3 = _tile_table(peer_u[l]), _tile_table(peer_v[l])
        tabs = (u3, v3, _pack_table(u3), _pack_table(v3))
        for g in range(SEQ_GROUPS):
            xs[g], v_firsts[g] = layer(l, xs[g], mems[g], v_firsts[g], tabs)
    outs = [_final_norm(xg.reshape(nb * seq, d), norm_final).reshape(nb, seq, d) for xg in xs]
    return jnp.concatenate(outs, axis=0)
```

```python
import functools
import math

import jax
import jax.numpy as jnp
from jax import lax
from jax.experimental import pallas as pl
from jax.experimental.pallas import tpu as pltpu
from jax.experimental.pallas import tpu_sc as plsc

F32 = jnp.float32
BF16 = jnp.bfloat16
U32 = jnp.uint32

LANES = 128
SUBLANES = 8
VMEM_LIMIT = 56 * 1024 * 1024

D_MODEL = 1024
D_RWKV = 512
RW_HEAD = 64
RW_COLS = 1792
D_S5 = 512
S5_GROUPS = 32
S5_CH = 16
S5_STATE = 64
S5_MODES = S5_GROUPS * S5_STATE
N_MEM = 256
XA_HEADS = 4
XA_HEAD = 256
PEER_HEADS = 8
PEER_NKEYS = 128
PEER_TOPK = 16
PEER_SEL = PEER_HEADS * PEER_TOPK
RMS_EPS = 1e-6
GN_EPS = 64e-5

RW_CHUNK = 128
S5_BLOCK = 256
PROJ_BLOCK = 512
XA_BLOCK = 256
SEL_BLOCK = 128
GATHER_BLOCK = 64


def _params(sem):
    return pltpu.CompilerParams(dimension_semantics=sem, vmem_limit_bytes=VMEM_LIMIT)


def _rms(x, g):
    ms = jnp.mean(x * x, axis=-1, keepdims=True)
    return x * lax.rsqrt(ms + RMS_EPS) * g


def _bdot(a, b):
    return jnp.dot(a.astype(BF16), b.astype(BF16), preferred_element_type=F32)


def _bdot_nt(a, b):
    return lax.dot_general(a.astype(BF16), b.astype(BF16), (((1,), (1,)), ((), ())),
                           preferred_element_type=F32)


def _sigmoid(x):
    return 1.0 / (1.0 + jnp.exp(-x))


def _softplus(x):
    return jnp.maximum(x, 0.0) + jnp.log(1.0 + jnp.exp(-jnp.abs(x)))


def _gelu(x):
    return 0.5 * x * (1.0 + jnp.tanh(math.sqrt(2.0 / math.pi) * (x + 0.044715 * (x * x * x))))


def _full(shape):
    n = len(shape)
    return pl.BlockSpec(shape, lambda *_: (0,) * n)


def _norm_proj_kernel(*refs, n_out):
    x_ref, g_ref = refs[0], refs[1]
    w_refs = refs[2:2 + n_out]
    o_refs = refs[2 + n_out:]
    h = _rms(x_ref[...], g_ref[...]).astype(BF16)
    for w_ref, o_ref in zip(w_refs, o_refs):
        o_ref[...] = jnp.dot(h, w_ref[...], preferred_element_type=F32).astype(o_ref.dtype)


def _norm_proj(x2, g, ws, out_dtypes, block):
    t, d = x2.shape
    block = min(block, t)
    in_specs = [pl.BlockSpec((block, d), lambda i: (i, 0)), _full((1, d))]
    in_specs += [_full(w.shape) for w in ws]
    out_specs = [pl.BlockSpec((block, w.shape[1]), lambda i: (i, 0)) for w in ws]
    out_shape = [jax.ShapeDtypeStruct((t, w.shape[1]), dt) for w, dt in zip(ws, out_dtypes)]
    return pl.pallas_call(
        functools.partial(_norm_proj_kernel, n_out=len(ws)),
        grid=(t // block,), in_specs=in_specs, out_specs=out_specs, out_shape=out_shape,
        compiler_params=_params(("parallel",)),
    )(x2, g.reshape(1, d), *ws)


def _seg_sum(x, mseg):
    hi = x.astype(BF16)
    lo = (x - hi.astype(F32)).astype(BF16)
    return (jnp.dot(hi, mseg, preferred_element_type=F32)
            + jnp.dot(lo, mseg, preferred_element_type=F32))


def _col_bcast(row):
    return jnp.broadcast_to(row, (LANES, LANES)).T


def _rwkv_kernel(*refs, first_layer):
    if first_layer:
        (z_ref, mu_ref, w0_ref, w2_ref, a0_ref, a2_ref, g2_ref, kk_ref, ka_ref, rk_ref,
         lng_ref, lnb_ref, mseg_ref, y_ref, vf_out_ref, zprev, hst) = refs
    else:
        (z_ref, hv_ref, vf_ref, v0_ref, v2_ref, mu_ref, w0_ref, w2_ref, a0_ref, a2_ref, g2_ref,
         kk_ref, ka_ref, rk_ref, lng_ref, lnb_ref, mseg_ref, y_ref, zprev, hst) = refs
    L = RW_CHUNK

    @pl.when(pl.program_id(1) == 0)
    def _():
        zprev[...] = jnp.zeros_like(zprev)
        hst[...] = jnp.zeros_like(hst)

    z = z_ref[0]
    row = lax.broadcasted_iota(jnp.int32, (L, 1), 0)
    zs = jnp.where(row == 0, zprev[...], pltpu.roll(z, 1, 0))
    zprev[...] = z[L - 1:L, :]
    z = z + (zs - z) * mu_ref[...]
    r = z[:, 0:512]
    k = z[:, 512:1024]
    v = z[:, 1024:1536]
    wa = z[:, 1536:1664]
    gd = z[:, 1664:1792]
    mseg = mseg_ref[...]

    wlin = w0_ref[...] + _bdot(jnp.tanh(wa), w2_ref[...])
    lw = -jnp.exp(-_softplus(-wlin) - 0.5)
    a = _sigmoid(a0_ref[...] + _bdot(wa, a2_ref[...]))
    g = _bdot(_sigmoid(gd), g2_ref[...])
    if first_layer:
        vf_out_ref[0] = v
    else:
        v = v + (vf_ref[0] - v) * _sigmoid(v0_ref[...] + _bdot(hv_ref[0], v2_ref[...]))
    kk = k * kk_ref[...]
    kk = kk / jnp.maximum(jnp.sqrt(_seg_sum(kk * kk, mseg)), 1e-12)
    k2 = k * (1.0 + (a - 1.0) * ka_ref[...])
    av = -kk
    bv = kk * a

    ti = lax.broadcasted_iota(jnp.int32, (L, L), 0)
    si = lax.broadcasted_iota(jnp.int32, (L, L), 1)
    tril = (ti >= si).astype(F32)
    cum = jnp.dot(tril, lw, preferred_element_type=F32, precision=lax.Precision.HIGHEST)
    mid = cum[L // 2 - 1:L // 2, :]
    cm = cum - mid
    ecum = jnp.exp(cm)
    einv = jnp.exp(-cm)
    rt = r * ecum
    kt = k2 * einv
    bt = bv * einv
    at = av * jnp.exp(cm - lw)
    p_end = jnp.exp(cum[L - 1:L, :])
    e_end = ecum[L - 1:L, :]
    e_mid = jnp.exp(mid)

    lane = lax.broadcasted_iota(jnp.int32, (1, LANES), 1)
    m0 = (lane < RW_HEAD).astype(F32)
    m1 = 1.0 - m0
    strict = ti > si
    incl = ti >= si
    bi = lax.broadcasted_iota(jnp.int32, (LANES, LANES), 0) < RW_HEAD
    bj = lax.broadcasted_iota(jnp.int32, (LANES, LANES), 1) < RW_HEAD
    bdmask = (bi == bj).astype(F32)
    zeros_ll = jnp.zeros((L, L), F32)

    ys = []
    for p in range(D_RWKV // LANES):
        sl = slice(LANES * p, LANES * (p + 1))
        A, B, K, R, V = at[:, sl], bt[:, sl], kt[:, sl], rt[:, sl], v[:, sl]
        btkt = jnp.concatenate([B.T, K.T], axis=1)
        h0 = hst[p]
        h0m = h0 * _col_bcast(e_mid[:, sl])
        sc = _bdot(jnp.concatenate([A * m0, A * m1, R * m0, R * m1], axis=0), btkt)
        aab = [jnp.where(strict, sc[e * L:(e + 1) * L, 0:L], 0.0) for e in range(2)]
        aak = [jnp.where(strict, sc[e * L:(e + 1) * L, L:2 * L], 0.0) for e in range(2)]
        arb = [jnp.where(incl, sc[(2 + e) * L:(3 + e) * L, 0:L], 0.0) for e in range(2)]
        ark = [jnp.where(incl, sc[(2 + e) * L:(3 + e) * L, L:2 * L], 0.0) for e in range(2)]
        arh = _bdot(jnp.concatenate([A, R], axis=0), h0m)
        v01 = jnp.concatenate([V * m0, V * m1], axis=0)
        x = arh[0:L] + _bdot(jnp.concatenate(aak, axis=1), v01)
        pm = jnp.concatenate(aab, axis=1)
        n_fac = int(math.log2(L))
        for it in range(n_fac):
            x = x + _bdot(pm, jnp.concatenate([x * m0, x * m1], axis=0))
            if it + 1 < n_fac:
                pd = jnp.concatenate(
                    [jnp.concatenate([pm[:, 0:L], zeros_ll], axis=1),
                     jnp.concatenate([zeros_ll, pm[:, L:2 * L]], axis=1)], axis=0)
                pm = _bdot(pm, pd)
        u = x
        yp = arh[L:2 * L] + _bdot(jnp.concatenate(arb + ark, axis=1),
                                  jnp.concatenate([u * m0, u * m1, v01], axis=0))
        upd = _bdot(btkt, jnp.concatenate([u, V], axis=0))
        hst[p] = (h0 * _col_bcast(p_end[:, sl]) + upd * _col_bcast(e_end[:, sl])) * bdmask
        ys.append(yp)
    y = jnp.concatenate(ys, axis=1)

    mean = _seg_sum(y, mseg) * (1.0 / RW_HEAD)
    d = y - mean
    var = _seg_sum(d * d, mseg) * (1.0 / RW_HEAD)
    yn = d * lax.rsqrt(var + GN_EPS) * lng_ref[...] + lnb_ref[...]
    bonus = _seg_sum(r * k2 * rk_ref[...], mseg) * v
    y_ref[0] = (yn + bonus) * g


def _pad_rows(w, rows, offset):
    out = jnp.zeros((rows, w.shape[1]), w.dtype)
    return out.at[offset:offset + w.shape[0]].set(w)


def _rwkv(z_rw, hv, v_first, prm, bsz, seq):
    L = RW_CHUNK
    first = v_first is None
    row = lambda a: a.reshape(1, -1).astype(F32)
    hid = jnp.arange(D_RWKV) // RW_HEAD
    mseg = (hid[:, None] == hid[None, :]).astype(BF16)
    w2p = _pad_rows(prm['w2'], LANES, 0).astype(BF16)
    a2p = _pad_rows(prm['a2'], LANES, 64).astype(BF16)
    common = [row(prm['mu']), row(prm['w0']), w2p, row(prm['a0']), a2p, prm['g2'].astype(BF16),
              row(prm['kk']), row(prm['ka']), row(prm['rk']), row(prm['lng']), row(prm['lnb']), mseg]
    tok = lambda w: pl.BlockSpec((1, L, w), lambda b, t: (b, t, 0))
    common_specs = [_full(c.shape) for c in common]
    y_shape = jax.ShapeDtypeStruct((bsz, seq, D_RWKV), F32)
    scratch = [pltpu.VMEM((1, RW_COLS), F32), pltpu.VMEM((D_RWKV // LANES, LANES, LANES), F32)]
    if first:
        args = [z_rw] + common
        in_specs = [tok(RW_COLS)] + common_specs
        out_shape = [y_shape, y_shape]
        out_specs = [tok(D_RWKV), tok(D_RWKV)]
    else:
        v2p = _pad_rows(prm['v2'], LANES, 0).astype(BF16)
        extra = [row(prm['v0']), v2p]
        args = [z_rw, hv, v_first] + extra + common
        in_specs = [tok(RW_COLS), tok(LANES), tok(D_RWKV)] + [_full(c.shape) for c in extra] + common_specs
        out_shape = [y_shape]
        out_specs = [tok(D_RWKV)]
    outs = pl.pallas_call(
        functools.partial(_rwkv_kernel, first_layer=first),
        grid=(bsz, seq // L), in_specs=in_specs, out_specs=out_specs, out_shape=out_shape,
        scratch_shapes=scratch, compiler_params=_params(("parallel", "arbitrary")),
    )(*args)
    return (outs[0], outs[1]) if first else (outs[0], v_first)


def _s5_kernel(u_ref, wb_ref, wc_ref, lpr_ref, lpi_ref, d_ref, gw_ref, gb_ref, og_ref, o_ref,
               car_re, car_im, xre, xim):
    tb = u_ref.shape[1]

    @pl.when(pl.program_id(1) == 0)
    def _():
        car_re[...] = jnp.zeros_like(car_re)
        car_im[...] = jnp.zeros_like(car_im)

    u = u_ref[0]
    bu = _bdot(u, wb_ref[...])
    xre[...] = bu[:, 0:S5_MODES]
    xim[...] = bu[:, S5_MODES:2 * S5_MODES]
    row = lax.broadcasted_iota(jnp.int32, (SUBLANES, 1), 0)

    def tile(i, carry):
        cr, ci = carry
        rs = pl.ds(pl.multiple_of(i * SUBLANES, SUBLANES), SUBLANES)
        br, bi = xre[rs, :], xim[rs, :]
        for dist in (1, 2, 4):
            keep = row >= dist
            sr = jnp.where(keep, pltpu.roll(br, dist, 0), 0.0)
            si = jnp.where(keep, pltpu.roll(bi, dist, 0), 0.0)
            lr = lpr_ref[dist - 1:dist, :]
            li = lpi_ref[dist - 1:dist, :]
            br, bi = br + lr * sr - li * si, bi + lr * si + li * sr
        pr, pi = lpr_ref[...], lpi_ref[...]
        xr = br + pr * cr - pi * ci
        xi = bi + pr * ci + pi * cr
        xre[rs, :] = xr
        xim[rs, :] = xi
        return xr[SUBLANES - 1:SUBLANES, :], xi[SUBLANES - 1:SUBLANES, :]

    cr, ci = lax.fori_loop(0, tb // SUBLANES, tile, (car_re[...], car_im[...]))
    car_re[...] = cr
    car_im[...] = ci
    wc = wc_ref[...]
    y = _bdot(xre[...], wc[0:S5_MODES]) + _bdot(xim[...], wc[S5_MODES:2 * S5_MODES])
    y = _gelu(y + d_ref[...] * u)
    y = y * _sigmoid(_bdot(y, gw_ref[...]) + gb_ref[...])
    o_ref[0] = _rms(y, og_ref[...])


def _s5_weights(a_re, a_im, log_dt, b_re, b_im, c_re, c_im):
    lam_re = jnp.minimum(a_re.astype(F32), -1e-4)
    lam_im = a_im.astype(F32)
    dt = jnp.exp(log_dt.astype(F32))[:, None]
    mag = jnp.exp(lam_re * dt)
    lb_re = mag * jnp.cos(lam_im * dt)
    lb_im = mag * jnp.sin(lam_im * dt)
    den = lam_re * lam_re + lam_im * lam_im
    c1_re = ((lb_re - 1.0) * lam_re + lb_im * lam_im) / den
    c1_im = (lb_im * lam_re - (lb_re - 1.0) * lam_im) / den
    br, bi = b_re.astype(F32), b_im.astype(F32)
    bb_re = c1_re[..., None] * br - c1_im[..., None] * bi
    bb_im = c1_re[..., None] * bi + c1_im[..., None] * br
    eye = jnp.eye(S5_GROUPS, dtype=F32)
    wb_re = jnp.einsum('gpc,gh->gchp', bb_re, eye).reshape(D_S5, S5_MODES)
    wb_im = jnp.einsum('gpc,gh->gchp', bb_im, eye).reshape(D_S5, S5_MODES)
    wb = jnp.concatenate([wb_re, wb_im], axis=1).astype(BF16)
    wc_re = jnp.einsum('gcp,gh->gphc', c_re.astype(F32), eye).reshape(S5_MODES, D_S5)
    wc_im = jnp.einsum('gcp,gh->gphc', c_im.astype(F32), eye).reshape(S5_MODES, D_S5)
    wc = jnp.concatenate([wc_re, -wc_im], axis=0).astype(BF16)
    pr, pi = [lb_re], [lb_im]
    for _ in range(SUBLANES - 1):
        pr, pi = pr + [pr[-1] * lb_re - pi[-1] * lb_im], pi + [pr[-1] * lb_im + pi[-1] * lb_re]
    lp_re = jnp.stack(pr).reshape(SUBLANES, S5_MODES)
    lp_im = jnp.stack(pi).reshape(SUBLANES, S5_MODES)
    return wb, wc, lp_re, lp_im


def _s5(u, prm, bsz, seq):
    tb = min(S5_BLOCK, seq)
    wb, wc, lp_re, lp_im = _s5_weights(prm['a_re'], prm['a_im'], prm['log_dt'], prm['b_re'], prm['b_im'],
                                       prm['c_re'], prm['c_im'])
    row = lambda a: a.reshape(1, -1).astype(F32)
    consts = [wb, wc, lp_re, lp_im, row(prm['d']), prm['glu_w'].astype(BF16), row(prm['glu_b']),
              row(prm['out_g'])]
    tok = pl.BlockSpec((1, tb, D_S5), lambda b, t: (b, t, 0))
    return pl.pallas_call(
        _s5_kernel, grid=(bsz, seq // tb),
        in_specs=[tok] + [_full(c.shape) for c in consts], out_specs=tok,
        out_shape=jax.ShapeDtypeStruct((bsz, seq, D_S5), F32),
        scratch_shapes=[pltpu.VMEM((1, S5_MODES), F32), pltpu.VMEM((1, S5_MODES), F32),
                        pltpu.VMEM((tb, S5_MODES), F32), pltpu.VMEM((tb, S5_MODES), F32)],
        compiler_params=_params(("parallel", "arbitrary")),
    )(u, *consts)


def _mix_xattn_kernel(x_ref, yr_ref, ys_ref, wo1_ref, wo2_ref, g_ref, wq_ref, k_ref, v_ref, wo_ref, o_ref):
    x1 = x_ref[0] + _bdot(yr_ref[0], wo1_ref[...]) + _bdot(ys_ref[0], wo2_ref[...])
    h = _rms(x1, g_ref[...])
    q = _bdot(h, wq_ref[...])
    km, vm = k_ref[0], v_ref[0]
    outs = []
    for hd in range(XA_HEADS):
        sl = slice(XA_HEAD * hd, XA_HEAD * (hd + 1))
        s = _bdot_nt(q[:, sl], km[:, sl]) * (XA_HEAD ** -0.5)
        s = s - jnp.max(s, axis=-1, keepdims=True)
        e = jnp.exp(s)
        p = e / jnp.sum(e, axis=-1, keepdims=True)
        outs.append(_bdot(p, vm[:, sl]))
    o = jnp.concatenate(outs, axis=1)
    o_ref[0] = x1 + _bdot(o, wo_ref[...])


def _mix_xattn(x, y_rw, y_s5, w_out, g, wq, kmem, vmem, wo, bsz, seq):
    tm = min(XA_BLOCK, seq)
    consts_a = [w_out[:D_RWKV].astype(BF16), w_out[D_RWKV:].astype(BF16), g.reshape(1, -1).astype(F32),
                wq.astype(BF16)]
    tok = lambda w: pl.BlockSpec((1, tm, w), lambda b, t: (b, t, 0))
    mem = pl.BlockSpec((1, N_MEM, D_MODEL), lambda b, t: (b, 0, 0))
    wo_b = wo.astype(BF16)
    return pl.pallas_call(
        _mix_xattn_kernel, grid=(bsz, seq // tm),
        in_specs=[tok(D_MODEL), tok(D_RWKV), tok(D_S5)] + [_full(c.shape) for c in consts_a]
                 + [mem, mem, _full(wo_b.shape)],
        out_specs=tok(D_MODEL), out_shape=jax.ShapeDtypeStruct((bsz, seq, D_MODEL), F32),
        compiler_params=_params(("parallel", "parallel")),
    )(x, y_rw, y_s5, *consts_a, kmem, vmem, wo_b)


def _top_rows(work, order, aux, val_ref, idx_ref):
    for it in range(PEER_TOPK):
        m = jnp.max(work, axis=0, keepdims=True)
        pos = jnp.min(jnp.where(work == m, order, jnp.inf), axis=0, keepdims=True)
        hit = order == pos
        val_ref[it:it + 1, :] = m
        if aux is None:
            idx_ref[it:it + 1, :] = pos
        else:
            idx_ref[it:it + 1, :] = jnp.sum(jnp.where(hit, aux, 0.0), axis=0, keepdims=True)
        work = jnp.where(hit, -jnp.inf, work)


_CAND_ROW_BLOCKS = [(0, PEER_TOPK), (1, SUBLANES), (2, SUBLANES), (3, SUBLANES)]
_CAND_COL_BLOCKS = [(0, PEER_TOPK, 4, 15), (1, SUBLANES, 4, 7), (2, SUBLANES, 4, 4)]
N_CAND = sum(n for _, n in _CAND_ROW_BLOCKS) + sum(n for _, n, _, _ in _CAND_COL_BLOCKS)


def _cand_consts(tm):
    flat, neg = [], []
    for a, nb in _CAND_ROW_BLOCKS:
        flat += [a * PEER_TOPK + b for b in range(nb)]
        neg += [0.0] * nb
    for b, na, lo, hi in _CAND_COL_BLOCKS:
        flat += [a * PEER_TOPK + b for a in range(na)]
        neg += [0.0 if lo <= a <= hi else -float('inf') for a in range(na)]
    flat = [f if n == 0.0 else 1000.0 + i for i, (f, n) in enumerate(zip(flat, neg))]
    col = lambda v: jnp.broadcast_to(jnp.asarray(v, F32)[:, None], (N_CAND, tm))
    return col(flat), col(neg)


def _cand_rows(row_vals, col_vals, combine):
    blocks = [combine(row_vals[a:a + 1, :], col_vals[0:nb, :]) for a, nb in _CAND_ROW_BLOCKS]
    blocks += [combine(row_vals[0:na, :], col_vals[b:b + 1, :]) for b, na, _, _ in _CAND_COL_BLOCKS]
    return jnp.concatenate(blocks, axis=0)


SEL_HEADS_PER_STEP = 2


def _peer_select_kernel(x_ref, g_ref, wq_ref, keys_ref, cflat_ref, cneg_ref, x8_ref, h8_ref, base_ref, shift_ref,
                        gate_ref, q3, idx_t, gate_t, *lists):
    tm = x_ref.shape[0]
    n = SEL_HEADS_PER_STEP
    s1, i1, s2, i2, top, eid = (lists[j * n:(j + 1) * n] for j in range(6))
    x = x_ref[...]
    h = _rms(x, g_ref[...])
    for r in range(D_MODEL // LANES):
        rows = pl.ds(r, tm, stride=D_MODEL // LANES)
        h8_ref[rows, :] = h[:, LANES * r:LANES * (r + 1)]
        x8_ref[rows, :] = x[:, LANES * r:LANES * (r + 1)]
    q = _bdot(h, wq_ref[...])
    for j in range(2 * PEER_HEADS):
        q3[j] = q[:, LANES * j:LANES * (j + 1)]
    iota_k = lax.broadcasted_iota(jnp.int32, (PEER_NKEYS, tm), 0).astype(F32)

    def heads(step, _):
        for u in range(SEL_HEADS_PER_STEP):
            hd = step * SEL_HEADS_PER_STEP + u
            sc1 = _bdot_nt(keys_ref[2 * hd], q3[2 * hd])
            sc2 = _bdot_nt(keys_ref[2 * hd + 1], q3[2 * hd + 1])
            _top_rows(sc1, iota_k, None, s1[u], i1[u])
            _top_rows(sc2, iota_k, None, s2[u], i2[u])
            cand = _cand_rows(s1[u][...], s2[u][...], lambda x, y: x + y) + cneg_ref[...]
            cidx = _cand_rows(i1[u][...], i2[u][...], lambda x, y: x * float(PEER_NKEYS) + y)
            _top_rows(cand, cflat_ref[...], cidx, top[u], eid[u])
            tv = top[u][...]
            e = jnp.exp(tv - jnp.max(tv, axis=0, keepdims=True))
            rs = pl.ds(pl.multiple_of(hd * PEER_TOPK, PEER_TOPK), PEER_TOPK)
            idx_t[rs, :] = eid[u][...]
            gate_t[rs, :] = e / jnp.sum(e, axis=0, keepdims=True)
        return 0

    lax.fori_loop(0, PEER_HEADS // SEL_HEADS_PER_STEP, heads, 0)
    e_t = idx_t[...].T
    pair = jnp.floor(e_t * 0.5)
    base_ref[...] = (pair * float(SUBLANES)).astype(jnp.int32)
    shift_ref[...] = ((e_t - 2.0 * pair) * 16.0).astype(jnp.int32)
    gate_ref[...] = gate_t[...].T


def _peer_select(x2, g, wq, keys):
    t = x2.shape[0]
    tm = SEL_BLOCK
    keys_b = keys.reshape(2 * PEER_HEADS, PEER_NKEYS, LANES).astype(BF16)
    wq_b = wq.astype(BF16)
    cflat, cneg = _cand_consts(tm)
    tokspec = lambda w: pl.BlockSpec((tm, w), lambda i: (i, 0))
    vm = lambda r: pltpu.VMEM((r, tm), F32)
    return pl.pallas_call(
        _peer_select_kernel, grid=(t // tm,),
        in_specs=[tokspec(D_MODEL), _full((1, D_MODEL)), _full(wq_b.shape), _full(keys_b.shape),
                  _full(cflat.shape), _full(cneg.shape)],
        out_specs=[pl.BlockSpec((tm * SUBLANES, LANES), lambda i: (i, 0))] * 2
                  + [tokspec(PEER_SEL), tokspec(PEER_SEL), tokspec(PEER_SEL)],
        out_shape=[jax.ShapeDtypeStruct((t * SUBLANES, LANES), F32)] * 2 + [
                   jax.ShapeDtypeStruct((t, PEER_SEL), jnp.int32),
                   jax.ShapeDtypeStruct((t, PEER_SEL), jnp.int32), jax.ShapeDtypeStruct((t, PEER_SEL), F32)],
        scratch_shapes=[pltpu.VMEM((2 * PEER_HEADS, tm, LANES), F32), vm(PEER_SEL), vm(PEER_SEL)]
                       + [vm(PEER_TOPK) for _ in range(6 * SEL_HEADS_PER_STEP)],
        compiler_params=_params(("parallel",)),
    )(x2, g.reshape(1, -1).astype(F32), wq_b, keys_b, cflat, cneg)


def _tile_table(tab):
    return tab.reshape(tab.shape[0], D_MODEL // LANES, LANES)


def _pack_table(tab3):
    n = tab3.shape[0]
    bits = lax.bitcast_convert_type(tab3.astype(BF16), jnp.uint16).astype(U32)
    bits = bits.reshape(n // 2, 2, SUBLANES, LANES)
    packed = (bits[:, 0] << 16) | bits[:, 1]
    return lax.bitcast_convert_type(packed, jnp.int32).reshape(n // 2 * SUBLANES, LANES)


def _splat_into(src_ref, t, dst_ref, slot):
    tile = jnp.broadcast_to(src_ref[pl.ds(t, 1), :], (LANES, LANES)).T
    dst_ref[LANES * slot:LANES * (slot + 1), :] = tile


def _bcast_row(ref, row):
    return jnp.broadcast_to(ref[row:row + 1, :], (SUBLANES, LANES))


def _expert_tile(tab_ref, base, shift_splat, row):
    w = tab_ref[pl.ds(pl.multiple_of(base, SUBLANES), SUBLANES), :]
    return lax.bitcast_convert_type((w << _bcast_row(shift_splat, row)) & jnp.int32(-65536), F32)


def _token_rows(t):
    return pl.ds(pl.multiple_of(t * SUBLANES, SUBLANES), SUBLANES)


def _pipelined_tokens(tg, prep, work):
    prep(0, 0)

    def body(i, _):
        t0 = 2 * i
        prep(t0 + 1, 1)
        work(t0, 0)
        prep(jnp.minimum(t0 + 2, tg - 1), 0)
        work(t0 + 1, 1)
        return 0

    lax.fori_loop(0, tg // 2, body, 0)


def _table_spec(shape):
    return pl.BlockSpec(shape, lambda i: (0, 0), pipeline_mode=pl.Buffered(1))


N_CHUNK = D_MODEL // LANES
CHUNK_STRIDE = PEER_SEL + SUBLANES


def _peer_u_kernel(base_ref, shift_ref, h_ref, gate_ref, tab_ref, eye_ref, c_ref, shift_splat, *planes):
    tg = gate_ref.shape[0]
    ones = jnp.ones((LANES, LANES), BF16)

    half = N_CHUNK // 2

    def prep(t, slot):
        _splat_into(shift_ref, t, shift_splat, slot)

    def gather(t, slot):
        ht = h_ref[_token_rows(t), :]
        for kx in range(PEER_SEL):
            prod = _expert_tile(tab_ref, base_ref[t, kx], shift_splat, LANES * slot + kx) * ht
            fold = prod + pltpu.roll(prod, half, 0)
            planes[slot][pl.ds(kx, half, stride=CHUNK_STRIDE), :] = fold[0:half, :]

    def finish(t, slot):
        plane = planes[slot]
        acc = plane[0:PEER_SEL, :]
        for r in range(1, half):
            acc = acc + plane[CHUNK_STRIDE * r:CHUNK_STRIDE * r + PEER_SEL, :]
        hi = acc.astype(BF16)
        lo = (acc - hi.astype(F32)).astype(BF16)
        tot = jnp.dot(hi, ones, preferred_element_type=F32) + jnp.dot(lo, ones, preferred_element_type=F32)
        score = jnp.sum(tot * eye_ref[...], axis=0, keepdims=True)
        c_ref[pl.ds(t, 1), :] = gate_ref[pl.ds(t, 1), :] * _gelu(score)

    planes[1][...] = jnp.zeros_like(planes[1])
    prep(0, 0)

    def body(i, _):
        t0 = 2 * i
        prep(t0 + 1, 1)
        gather(t0, 0)
        finish(jnp.maximum(t0 - 1, 0), 1)
        prep(jnp.minimum(t0 + 2, tg - 1), 0)
        gather(t0 + 1, 1)
        finish(t0, 0)
        return 0

    lax.fori_loop(0, tg // 2, body, 0)
    finish(tg - 1, 1)


def _peer_u(base, shift, h8, gate, tab, t):
    tg = GATHER_BLOCK
    eye = jnp.eye(LANES, dtype=F32)
    tokrow = pl.BlockSpec((tg, PEER_SEL), lambda i: (i, 0))
    return pl.pallas_call(
        _peer_u_kernel, grid=(t // tg,),
        in_specs=[pl.BlockSpec((tg, PEER_SEL), lambda i: (i, 0), memory_space=pltpu.SMEM), tokrow,
                  pl.BlockSpec((tg * SUBLANES, LANES), lambda i: (i, 0)), tokrow,
                  _table_spec(tab.shape), _full(eye.shape)],
        out_specs=tokrow, out_shape=jax.ShapeDtypeStruct((t, PEER_SEL), F32),
        scratch_shapes=[pltpu.VMEM((2 * LANES, LANES), jnp.int32)]
                       + [pltpu.VMEM((N_CHUNK // 2 * CHUNK_STRIDE, LANES), F32)] * 2,
        compiler_params=_params(("parallel",)),
    )(base, shift, h8, gate, tab, eye)


def _peer_v_kernel(base_ref, shift_ref, c_ref, x_ref, tab_ref, _aliased_out, o_ref, shift_splat, c_splat, otile):
    tg = c_ref.shape[0]
    n_acc = 4

    def prep(t, slot):
        _splat_into(shift_ref, t, shift_splat, slot)
        _splat_into(c_ref, t, c_splat, slot)

    def work(t, slot):
        accs = [jnp.zeros((SUBLANES, LANES), F32) for _ in range(n_acc)]
        for kx in range(PEER_SEL):
            row = LANES * slot + kx
            accs[kx % n_acc] = accs[kx % n_acc] + (_bcast_row(c_splat, row)
                                                   * _expert_tile(tab_ref, base_ref[t, kx], shift_splat, row))
        rs = _token_rows(t)
        otile[rs, :] = x_ref[rs, :] + ((accs[0] + accs[1]) + (accs[2] + accs[3]))

    _pipelined_tokens(tg, prep, work)
    for r in range(N_CHUNK):
        o_ref[:, LANES * r:LANES * (r + 1)] = otile[pl.ds(r, tg, stride=N_CHUNK), :]


def _peer_v(base, shift, c, x8, x2, tab, t):
    tg = GATHER_BLOCK
    smem = pl.BlockSpec((tg, PEER_SEL), lambda i: (i, 0), memory_space=pltpu.SMEM)
    tokrow = pl.BlockSpec((tg, PEER_SEL), lambda i: (i, 0))
    tile = pl.BlockSpec((tg * SUBLANES, LANES), lambda i: (i, 0))
    return pl.pallas_call(
        _peer_v_kernel, grid=(t // tg,),
        in_specs=[smem, tokrow, tokrow, tile, _table_spec(tab.shape), pl.BlockSpec(memory_space=pl.ANY)],
        out_specs=pl.BlockSpec((tg, D_MODEL), lambda i: (i, 0)),
        out_shape=jax.ShapeDtypeStruct(x2.shape, F32),
        input_output_aliases={5: 0},
        scratch_shapes=[pltpu.VMEM((2 * LANES, LANES), jnp.int32), pltpu.VMEM((2 * LANES, LANES), F32),
                        pltpu.VMEM((tg * SUBLANES, LANES), F32)],
        compiler_params=_params(("parallel",)),
    )(base, shift, c, x8, tab, x2)


SC_WORKERS = 32
SC_LANES = 16
SC_ROWS = 32
SC_SHARE_PIPELINED = 41 / 64
SC_SHARE_LAST = 27 / 64
SEQ_GROUPS = 2
SC_GROUP = 8
SC_REGS = 32


def _peer_sc(eidx, gate, h8, x8, tab_u, tab_v, t_off):
    ts = eidx.shape[0]
    per_w = ts // SC_WORKERS
    n_chunk = PEER_SEL // SC_ROWS
    grp = SC_GROUP
    sel = grp * PEER_SEL
    n_pairs = grp * n_chunk // 2
    per_tile = LANES // SC_LANES
    n_blk = D_MODEL // (SC_REGS * SC_LANES)
    mesh = plsc.VectorSubcoreMesh(core_axis_name="c", subcore_axis_name="s")

    def piece(ref, lead, q):
        return ref[lead, q // per_tile, pl.ds(SC_LANES * (q % per_tile), SC_LANES)]

    def body(u_hbm, v_hbm, idx_hbm, gate_hbm, h_hbm, x_hbm, out_hbm,
             idx_v, c_v, h_v, acc_v, part_v, rows0, rows1, sem0, sem1):
        wid = lax.axis_index("s") * 2 + lax.axis_index("c")
        bufs, sems = (rows0, rows1), (sem0, sem1)
        lane = lax.iota(jnp.int32, SC_LANES)

        def stream(tab_hbm, compute):
            def gather(chunk, slot):
                start = pl.multiple_of(chunk * SC_ROWS, SC_ROWS)
                return pltpu.make_async_copy(tab_hbm.at[idx_v.at[pl.ds(start, SC_ROWS)]], bufs[slot], sems[slot])

            gather(0, 0).start()

            @pl.loop(0, n_pairs)
            def _(p):
                c0 = 2 * p
                gather(c0 + 1, 1).start()
                gather(c0, 0).wait()
                compute(c0, 0)
                gather(jnp.minimum(c0 + 2, 2 * n_pairs - 1), 0).start()
                gather(c0 + 1, 1).wait()
                compute(c0 + 1, 1)

            gather(0, 0).wait()

        def dots(chunk, slot):
            tok = chunk // n_chunk
            for blk in range(n_blk):
                where = [blk * SC_REGS + j for j in range(SC_REGS)]
                hs = [piece(h_v, tok, q) for q in where]

                def row(k, carry):
                    parts = [None] * 4
                    for i, q in enumerate(where):
                        term = piece(bufs[slot], k, q) * hs[i]
                        parts[i % 4] = term if parts[i % 4] is None else parts[i % 4] + term
                    tot = (parts[0] + parts[1]) + (parts[2] + parts[3])
                    at = pl.ds(pl.multiple_of((chunk * SC_ROWS + k) * SC_LANES, SC_LANES), SC_LANES)
                    if blk == 0:
                        part_v[at] = tot
                    else:
                        part_v[at] = part_v[at] + tot
                    return carry

                lax.fori_loop(0, SC_ROWS, row, 0)

        def weigh(chunk, slot):
            tok = chunk // n_chunk
            for blk in range(n_blk):
                where = [blk * SC_REGS + j for j in range(SC_REGS)]

                def row(k, accs):
                    ck = plsc.load_gather(c_v, [jnp.full((SC_LANES,), chunk * SC_ROWS, jnp.int32) + k])
                    return tuple(a + ck * piece(bufs[slot], k, q) for a, q in zip(accs, where))

                accs = lax.fori_loop(0, SC_ROWS, row, tuple(piece(acc_v, tok, q) for q in where))
                for a, q in zip(accs, where):
                    acc_v[tok, q // per_tile, pl.ds(SC_LANES * (q % per_tile), SC_LANES)] = a

        @pl.loop(0, per_w // grp)
        def _(g):
            t0 = wid * per_w + g * grp
            flat = pl.ds(pl.multiple_of(t0 * PEER_SEL, sel), sel)
            pltpu.sync_copy(idx_hbm.at[flat], idx_v)
            pltpu.sync_copy(gate_hbm.at[flat], c_v)
            pltpu.sync_copy(h_hbm.at[pl.ds(t_off + t0, grp)], h_v)
            pltpu.sync_copy(x_hbm.at[pl.ds(t_off + t0, grp)], acc_v)
            stream(u_hbm, dots)

            @pl.loop(0, sel // SC_LANES)
            def _(m):
                base = (m * SC_LANES + lane) * SC_LANES
                score = plsc.load_gather(part_v, [base])
                for l in range(1, SC_LANES):
                    score = score + plsc.load_gather(part_v, [base + l])
                z = math.sqrt(2.0 / math.pi) * (score + 0.044715 * (score * score * score))
                tanh_z = 1.0 - 2.0 / (jnp.exp(2.0 * z) + 1.0)
                at = pl.ds(pl.multiple_of(m * SC_LANES, SC_LANES), SC_LANES)
                c_v[at] = c_v[at] * (0.5 * score * (1.0 + tanh_z))

            stream(v_hbm, weigh)
            pltpu.sync_copy(acc_v, out_hbm.at[pl.ds(t0, grp)])

    tile = lambda n: pltpu.VMEM((n, N_CHUNK, LANES), F32)
    out = pl.kernel(
        body, mesh=mesh, out_type=jax.ShapeDtypeStruct((ts, N_CHUNK, LANES), F32),
        scratch_types=[pltpu.VMEM((sel,), jnp.int32), pltpu.VMEM((sel,), F32), tile(grp), tile(grp),
                       pltpu.VMEM((sel * SC_LANES,), F32), tile(SC_ROWS), tile(SC_ROWS),
                       pltpu.SemaphoreType.DMA, pltpu.SemaphoreType.DMA],
        compiler_params=pltpu.CompilerParams(needs_layout_passes=False, use_tc_tiling_on_sc=True),
    )(tab_u, tab_v, eidx.reshape(-1), gate.reshape(-1),
      h8.reshape(-1, N_CHUNK, LANES), x8.reshape(-1, N_CHUNK, LANES))
    return out


def _final_norm_kernel(x_ref, g_ref, o_ref):
    o_ref[...] = _rms(x_ref[...], g_ref[...])


def _final_norm(x2, g):
    t, d = x2.shape
    tm = min(PROJ_BLOCK, t)
    spec = pl.BlockSpec((tm, d), lambda i: (i, 0))
    return pl.pallas_call(
        _final_norm_kernel, grid=(t // tm,), in_specs=[spec, _full((1, d))], out_specs=spec,
        out_shape=jax.ShapeDtypeStruct((t, d), F32), compiler_params=_params(("parallel",)),
    )(x2, g.reshape(1, d).astype(F32))


def kernel(x, mem, norm_mix, w_in, rw_mu, rw_w0, rw_w2, rw_a0, rw_a2, rw_g2, rw_kk, rw_ka, rw_rk, rw_v0, rw_v1, rw_v2, rw_lnx_g, rw_lnx_b, s5_a_re, s5_a_im, s5_log_dt, s5_b_re, s5_b_im, s5_c_re, s5_c_im, s5_d, s5_glu_w, s5_glu_b, s5_out_g, w_out, norm_xa, norm_mem, xa_wq, xa_wk, xa_wv, xa_wo, norm_ffn, peer_wq, peer_keys, peer_u, peer_v, norm_final):
    bsz, seq, d = x.shape
    depth = w_in.shape[0]

    def layer(l, xg, memg, v_first, tabs, sc_share):
        nb = xg.shape[0]
        t = nb * seq
        x2 = xg.reshape(t, d)
        ws = [w_in[l][:, :RW_COLS].astype(BF16), w_in[l][:, RW_COLS:].astype(BF16)]
        if l > 0:
            ws.append(_pad_rows(rw_v1[l - 1].T, LANES, 0).T.astype(BF16))
        outs = _norm_proj(x2, norm_mix[l], ws, [F32] * len(ws), PROJ_BLOCK)
        z_rw = outs[0].reshape(nb, seq, RW_COLS)
        u_s5 = outs[1].reshape(nb, seq, D_S5)
        hv = outs[2].reshape(nb, seq, LANES) if l > 0 else None
        rw_prm = dict(mu=rw_mu[l], w0=rw_w0[l], w2=rw_w2[l], a0=rw_a0[l], a2=rw_a2[l], g2=rw_g2[l],
                      kk=rw_kk[l], ka=rw_ka[l], rk=rw_rk[l], lng=rw_lnx_g[l], lnb=rw_lnx_b[l])
        if l > 0:
            rw_prm.update(v0=rw_v0[l - 1], v2=rw_v2[l - 1])
        y_rw, v_first = _rwkv(z_rw, hv, v_first, rw_prm, nb, seq)
        s5_prm = dict(a_re=s5_a_re[l], a_im=s5_a_im[l], log_dt=s5_log_dt[l], b_re=s5_b_re[l], b_im=s5_b_im[l],
                      c_re=s5_c_re[l], c_im=s5_c_im[l], d=s5_d[l], glu_w=s5_glu_w[l], glu_b=s5_glu_b[l],
                      out_g=s5_out_g[l])
        y_s5 = _s5(u_s5, s5_prm, nb, seq)
        kv = _norm_proj(memg.reshape(nb * N_MEM, d), norm_mem[l],
                        [xa_wk[l].astype(BF16), xa_wv[l].astype(BF16)], [BF16, BF16], PROJ_BLOCK)
        kmem = kv[0].reshape(nb, N_MEM, d)
        vmem = kv[1].reshape(nb, N_MEM, d)
        xg = _mix_xattn(xg, y_rw, y_s5, w_out[l], norm_xa[l], xa_wq[l], kmem, vmem, xa_wo[l], nb, seq)
        x2 = xg.reshape(t, d)
        x8, h8, base, shift, gate = _peer_select(x2, norm_ffn[l], peer_wq[l], peer_keys[l])
        u3, v3, u_packed, v_packed = tabs
        n_sc = int(t * sc_share)
        assert n_sc % (SC_WORKERS * SC_GROUP) == 0 and (t - n_sc) % GATHER_BLOCK == 0
        t_tc = t - n_sc
        eidx_sc = (base[t_tc:] >> 2) + (shift[t_tc:] >> 4)
        x_sc = _peer_sc(eidx_sc, gate[t_tc:], h8, x8, u3, v3, t_tc)
        c = _peer_u(base, shift, h8, gate, u_packed, t_tc)
        x_new = _peer_v(base, shift, c, x8, x2, v_packed, t_tc)
        xg = lax.dynamic_update_slice(x_new, x_sc.reshape(n_sc, d), (t_tc, 0)).reshape(nb, seq, d)
        return xg, v_first

    nb = bsz // SEQ_GROUPS
    xs = [x[g * nb:(g + 1) * nb] for g in range(SEQ_GROUPS)]
    mems = [mem[g * nb:(g + 1) * nb] for g in range(SEQ_GROUPS)]
    v_firsts = [None] * SEQ_GROUPS
    for l in range(depth):
        u3, v3 = _tile_table(peer_u[l]), _tile_table(peer_v[l])
        tabs = (u3, v3, _pack_table(u3), _pack_table(v3))
        for g in range(SEQ_GROUPS):
            last = l == depth - 1 and g == SEQ_GROUPS - 1
            share = SC_SHARE_LAST if last else SC_SHARE_PIPELINED
            xs[g], v_firsts[g] = layer(l, xs[g], mems[g], v_firsts[g], tabs, share)
    outs = [_final_norm(xg.reshape(nb * seq, d), norm_final).reshape(nb, seq, d) for xg in xs]
    return jnp.concatenate(outs, axis=0)
```

```python
import functools
import math

import jax
import jax.numpy as jnp
from jax import lax
from jax.experimental import pallas as pl
from jax.experimental.pallas import tpu as pltpu
from jax.experimental.pallas import tpu_sc as plsc

F32 = jnp.float32
BF16 = jnp.bfloat16
U32 = jnp.uint32

LANES = 128
SUBLANES = 8
VMEM_LIMIT = 56 * 1024 * 1024

D_MODEL = 1024
D_RWKV = 512
RW_HEAD = 64
RW_COLS = 1792
D_S5 = 512
S5_GROUPS = 32
S5_CH = 16
S5_STATE = 64
S5_MODES = S5_GROUPS * S5_STATE
N_MEM = 256
XA_HEADS = 4
XA_HEAD = 256
PEER_HEADS = 8
PEER_NKEYS = 128
PEER_TOPK = 16
PEER_SEL = PEER_HEADS * PEER_TOPK
RMS_EPS = 1e-6
GN_EPS = 64e-5

RW_CHUNK = 128
S5_BLOCK = 256
PROJ_BLOCK = 512
XA_BLOCK = 256
SEL_BLOCK = 128
GATHER_BLOCK = 64


def _params(sem):
    return pltpu.CompilerParams(dimension_semantics=sem, vmem_limit_bytes=VMEM_LIMIT)


def _rms(x, g):
    ms = jnp.mean(x * x, axis=-1, keepdims=True)
    return x * lax.rsqrt(ms + RMS_EPS) * g


def _bdot(a, b):
    return jnp.dot(a.astype(BF16), b.astype(BF16), preferred_element_type=F32)


def _bdot_nt(a, b):
    return lax.dot_general(a.astype(BF16), b.astype(BF16), (((1,), (1,)), ((), ())),
                           preferred_element_type=F32)


def _sigmoid(x):
    return 1.0 / (1.0 + jnp.exp(-x))


def _softplus(x):
    return jnp.maximum(x, 0.0) + jnp.log(1.0 + jnp.exp(-jnp.abs(x)))


def _gelu(x):
    return 0.5 * x * (1.0 + jnp.tanh(math.sqrt(2.0 / math.pi) * (x + 0.044715 * (x * x * x))))


def _full(shape):
    n = len(shape)
    return pl.BlockSpec(shape, lambda *_: (0,) * n)


def _norm_proj_kernel(*refs, n_out):
    x_ref, g_ref = refs[0], refs[1]
    w_refs = refs[2:2 + n_out]
    o_refs = refs[2 + n_out:]
    h = _rms(x_ref[...], g_ref[...]).astype(BF16)
    for w_ref, o_ref in zip(w_refs, o_refs):
        o_ref[...] = jnp.dot(h, w_ref[...], preferred_element_type=F32).astype(o_ref.dtype)


def _norm_proj(x2, g, ws, out_dtypes, block):
    t, d = x2.shape
    block = min(block, t)
    in_specs = [pl.BlockSpec((block, d), lambda i: (i, 0)), _full((1, d))]
    in_specs += [_full(w.shape) for w in ws]
    out_specs = [pl.BlockSpec((block, w.shape[1]), lambda i: (i, 0)) for w in ws]
    out_shape = [jax.ShapeDtypeStruct((t, w.shape[1]), dt) for w, dt in zip(ws, out_dtypes)]
    return pl.pallas_call(
        functools.partial(_norm_proj_kernel, n_out=len(ws)),
        grid=(t // block,), in_specs=in_specs, out_specs=out_specs, out_shape=out_shape,
        compiler_params=_params(("parallel",)),
    )(x2, g.reshape(1, d), *ws)


def _seg_sum(x, mseg):
    hi = x.astype(BF16)
    lo = (x - hi.astype(F32)).astype(BF16)
    return (jnp.dot(hi, mseg, preferred_element_type=F32)
            + jnp.dot(lo, mseg, preferred_element_type=F32))


def _col_bcast(row):
    return jnp.broadcast_to(row, (LANES, LANES)).T


def _rwkv_kernel(*refs, first_layer):
    if first_layer:
        (z_ref, mu_ref, w0_ref, w2_ref, a0_ref, a2_ref, g2_ref, kk_ref, ka_ref, rk_ref,
         lng_ref, lnb_ref, mseg_ref, y_ref, vf_out_ref, zprev, hst) = refs
    else:
        (z_ref, hv_ref, vf_ref, v0_ref, v2_ref, mu_ref, w0_ref, w2_ref, a0_ref, a2_ref, g2_ref,
         kk_ref, ka_ref, rk_ref, lng_ref, lnb_ref, mseg_ref, y_ref, zprev, hst) = refs
    L = RW_CHUNK

    @pl.when(pl.program_id(1) == 0)
    def _():
        zprev[...] = jnp.zeros_like(zprev)
        hst[...] = jnp.zeros_like(hst)

    z = z_ref[0]
    row = lax.broadcasted_iota(jnp.int32, (L, 1), 0)
    zs = jnp.where(row == 0, zprev[...], pltpu.roll(z, 1, 0))
    zprev[...] = z[L - 1:L, :]
    z = z + (zs - z) * mu_ref[...]
    r = z[:, 0:512]
    k = z[:, 512:1024]
    v = z[:, 1024:1536]
    wa = z[:, 1536:1664]
    gd = z[:, 1664:1792]
    mseg = mseg_ref[...]

    wlin = w0_ref[...] + _bdot(jnp.tanh(wa), w2_ref[...])
    lw = -jnp.exp(-_softplus(-wlin) - 0.5)
    a = _sigmoid(a0_ref[...] + _bdot(wa, a2_ref[...]))
    g = _bdot(_sigmoid(gd), g2_ref[...])
    if first_layer:
        vf_out_ref[0] = v
    else:
        v = v + (vf_ref[0] - v) * _sigmoid(v0_ref[...] + _bdot(hv_ref[0], v2_ref[...]))
    kk = k * kk_ref[...]
    kk = kk / jnp.maximum(jnp.sqrt(_seg_sum(kk * kk, mseg)), 1e-12)
    k2 = k * (1.0 + (a - 1.0) * ka_ref[...])
    av = -kk
    bv = kk * a

    ti = lax.broadcasted_iota(jnp.int32, (L, L), 0)
    si = lax.broadcasted_iota(jnp.int32, (L, L), 1)
    tril = (ti >= si).astype(F32)
    cum = jnp.dot(tril, lw, preferred_element_type=F32, precision=lax.Precision.HIGHEST)
    mid = cum[L // 2 - 1:L // 2, :]
    cm = cum - mid
    ecum = jnp.exp(cm)
    einv = jnp.exp(-cm)
    rt = r * ecum
    kt = k2 * einv
    bt = bv * einv
    at = av * jnp.exp(cm - lw)
    p_end = jnp.exp(cum[L - 1:L, :])
    e_end = ecum[L - 1:L, :]
    e_mid = jnp.exp(mid)

    lane = lax.broadcasted_iota(jnp.int32, (1, LANES), 1)
    m0 = (lane < RW_HEAD).astype(F32)
    m1 = 1.0 - m0
    strict = ti > si
    incl = ti >= si
    bi = lax.broadcasted_iota(jnp.int32, (LANES, LANES), 0) < RW_HEAD
    bj = lax.broadcasted_iota(jnp.int32, (LANES, LANES), 1) < RW_HEAD
    bdmask = (bi == bj).astype(F32)
    zeros_ll = jnp.zeros((L, L), F32)

    ys = []
    for p in range(D_RWKV // LANES):
        sl = slice(LANES * p, LANES * (p + 1))
        A, B, K, R, V = at[:, sl], bt[:, sl], kt[:, sl], rt[:, sl], v[:, sl]
        btkt = jnp.concatenate([B.T, K.T], axis=1)
        h0 = hst[p]
        h0m = h0 * _col_bcast(e_mid[:, sl])
        sc = _bdot(jnp.concatenate([A * m0, A * m1, R * m0, R * m1], axis=0), btkt)
        aab = [jnp.where(strict, sc[e * L:(e + 1) * L, 0:L], 0.0) for e in range(2)]
        aak = [jnp.where(strict, sc[e * L:(e + 1) * L, L:2 * L], 0.0) for e in range(2)]
        arb = [jnp.where(incl, sc[(2 + e) * L:(3 + e) * L, 0:L], 0.0) for e in range(2)]
        ark = [jnp.where(incl, sc[(2 + e) * L:(3 + e) * L, L:2 * L], 0.0) for e in range(2)]
        arh = _bdot(jnp.concatenate([A, R], axis=0), h0m)
        v01 = jnp.concatenate([V * m0, V * m1], axis=0)
        x = arh[0:L] + _bdot(jnp.concatenate(aak, axis=1), v01)
        pm = jnp.concatenate(aab, axis=1)
        n_fac = int(math.log2(L))
        for it in range(n_fac):
            x = x + _bdot(pm, jnp.concatenate([x * m0, x * m1], axis=0))
            if it + 1 < n_fac:
                pd = jnp.concatenate(
                    [jnp.concatenate([pm[:, 0:L], zeros_ll], axis=1),
                     jnp.concatenate([zeros_ll, pm[:, L:2 * L]], axis=1)], axis=0)
                pm = _bdot(pm, pd)
        u = x
        yp = arh[L:2 * L] + _bdot(jnp.concatenate(arb + ark, axis=1),
                                  jnp.concatenate([u * m0, u * m1, v01], axis=0))
        upd = _bdot(btkt, jnp.concatenate([u, V], axis=0))
        hst[p] = (h0 * _col_bcast(p_end[:, sl]) + upd * _col_bcast(e_end[:, sl])) * bdmask
        ys.append(yp)
    y = jnp.concatenate(ys, axis=1)

    mean = _seg_sum(y, mseg) * (1.0 / RW_HEAD)
    d = y - mean
    var = _seg_sum(d * d, mseg) * (1.0 / RW_HEAD)
    yn = d * lax.rsqrt(var + GN_EPS) * lng_ref[...] + lnb_ref[...]
    bonus = _seg_sum(r * k2 * rk_ref[...], mseg) * v
    y_ref[0] = (yn + bonus) * g


def _pad_rows(w, rows, offset):
    out = jnp.zeros((rows, w.shape[1]), w.dtype)
    return out.at[offset:offset + w.shape[0]].set(w)


def _rwkv(z_rw, hv, v_first, prm, bsz, seq):
    L = RW_CHUNK
    first = v_first is None
    row = lambda a: a.reshape(1, -1).astype(F32)
    hid = jnp.arange(D_RWKV) // RW_HEAD
    mseg = (hid[:, None] == hid[None, :]).astype(BF16)
    w2p = _pad_rows(prm['w2'], LANES, 0).astype(BF16)
    a2p = _pad_rows(prm['a2'], LANES, 64).astype(BF16)
    common = [row(prm['mu']), row(prm['w0']), w2p, row(prm['a0']), a2p, prm['g2'].astype(BF16),
              row(prm['kk']), row(prm['ka']), row(prm['rk']), row(prm['lng']), row(prm['lnb']), mseg]
    tok = lambda w: pl.BlockSpec((1, L, w), lambda b, t: (b, t, 0))
    common_specs = [_full(c.shape) for c in common]
    y_shape = jax.ShapeDtypeStruct((bsz, seq, D_RWKV), F32)
    scratch = [pltpu.VMEM((1, RW_COLS), F32), pltpu.VMEM((D_RWKV // LANES, LANES, LANES), F32)]
    if first:
        args = [z_rw] + common
        in_specs = [tok(RW_COLS)] + common_specs
        out_shape = [y_shape, y_shape]
        out_specs = [tok(D_RWKV), tok(D_RWKV)]
    else:
        v2p = _pad_rows(prm['v2'], LANES, 0).astype(BF16)
        extra = [row(prm['v0']), v2p]
        args = [z_rw, hv, v_first] + extra + common
        in_specs = [tok(RW_COLS), tok(LANES), tok(D_RWKV)] + [_full(c.shape) for c in extra] + common_specs
        out_shape = [y_shape]
        out_specs = [tok(D_RWKV)]
    outs = pl.pallas_call(
        functools.partial(_rwkv_kernel, first_layer=first),
        grid=(bsz, seq // L), in_specs=in_specs, out_specs=out_specs, out_shape=out_shape,
        scratch_shapes=scratch, compiler_params=_params(("parallel", "arbitrary")),
    )(*args)
    return (outs[0], outs[1]) if first else (outs[0], v_first)


def _s5_kernel(u_ref, wb_ref, wc_ref, lpr_ref, lpi_ref, d_ref, gw_ref, gb_ref, og_ref, o_ref,
               car_re, car_im, xre, xim):
    tb = u_ref.shape[1]

    @pl.when(pl.program_id(1) == 0)
    def _():
        car_re[...] = jnp.zeros_like(car_re)
        car_im[...] = jnp.zeros_like(car_im)

    u = u_ref[0]
    bu = _bdot(u, wb_ref[...])
    xre[...] = bu[:, 0:S5_MODES]
    xim[...] = bu[:, S5_MODES:2 * S5_MODES]
    row = lax.broadcasted_iota(jnp.int32, (SUBLANES, 1), 0)

    def tile(i, carry):
        cr, ci = carry
        rs = pl.ds(pl.multiple_of(i * SUBLANES, SUBLANES), SUBLANES)
        br, bi = xre[rs, :], xim[rs, :]
        for dist in (1, 2, 4):
            keep = row >= dist
            sr = jnp.where(keep, pltpu.roll(br, dist, 0), 0.0)
            si = jnp.where(keep, pltpu.roll(bi, dist, 0), 0.0)
            lr = lpr_ref[dist - 1:dist, :]
            li = lpi_ref[dist - 1:dist, :]
            br, bi = br + lr * sr - li * si, bi + lr * si + li * sr
        pr, pi = lpr_ref[...], lpi_ref[...]
        xr = br + pr * cr - pi * ci
        xi = bi + pr * ci + pi * cr
        xre[rs, :] = xr
        xim[rs, :] = xi
        return xr[SUBLANES - 1:SUBLANES, :], xi[SUBLANES - 1:SUBLANES, :]

    cr, ci = lax.fori_loop(0, tb // SUBLANES, tile, (car_re[...], car_im[...]))
    car_re[...] = cr
    car_im[...] = ci
    wc = wc_ref[...]
    y = _bdot(xre[...], wc[0:S5_MODES]) + _bdot(xim[...], wc[S5_MODES:2 * S5_MODES])
    y = _gelu(y + d_ref[...] * u)
    y = y * _sigmoid(_bdot(y, gw_ref[...]) + gb_ref[...])
    o_ref[0] = _rms(y, og_ref[...])


def _s5_weights(a_re, a_im, log_dt, b_re, b_im, c_re, c_im):
    lam_re = jnp.minimum(a_re.astype(F32), -1e-4)
    lam_im = a_im.astype(F32)
    dt = jnp.exp(log_dt.astype(F32))[:, None]
    mag = jnp.exp(lam_re * dt)
    lb_re = mag * jnp.cos(lam_im * dt)
    lb_im = mag * jnp.sin(lam_im * dt)
    den = lam_re * lam_re + lam_im * lam_im
    c1_re = ((lb_re - 1.0) * lam_re + lb_im * lam_im) / den
    c1_im = (lb_im * lam_re - (lb_re - 1.0) * lam_im) / den
    br, bi = b_re.astype(F32), b_im.astype(F32)
    bb_re = c1_re[..., None] * br - c1_im[..., None] * bi
    bb_im = c1_re[..., None] * bi + c1_im[..., None] * br
    eye = jnp.eye(S5_GROUPS, dtype=F32)
    wb_re = jnp.einsum('gpc,gh->gchp', bb_re, eye).reshape(D_S5, S5_MODES)
    wb_im = jnp.einsum('gpc,gh->gchp', bb_im, eye).reshape(D_S5, S5_MODES)
    wb = jnp.concatenate([wb_re, wb_im], axis=1).astype(BF16)
    wc_re = jnp.einsum('gcp,gh->gphc', c_re.astype(F32), eye).reshape(S5_MODES, D_S5)
    wc_im = jnp.einsum('gcp,gh->gphc', c_im.astype(F32), eye).reshape(S5_MODES, D_S5)
    wc = jnp.concatenate([wc_re, -wc_im], axis=0).astype(BF16)
    pr, pi = [lb_re], [lb_im]
    for _ in range(SUBLANES - 1):
        pr, pi = pr + [pr[-1] * lb_re - pi[-1] * lb_im], pi + [pr[-1] * lb_im + pi[-1] * lb_re]
    lp_re = jnp.stack(pr).reshape(SUBLANES, S5_MODES)
    lp_im = jnp.stack(pi).reshape(SUBLANES, S5_MODES)
    return wb, wc, lp_re, lp_im


def _s5(u, prm, bsz, seq):
    tb = min(S5_BLOCK, seq)
    wb, wc, lp_re, lp_im = _s5_weights(prm['a_re'], prm['a_im'], prm['log_dt'], prm['b_re'], prm['b_im'],
                                       prm['c_re'], prm['c_im'])
    row = lambda a: a.reshape(1, -1).astype(F32)
    consts = [wb, wc, lp_re, lp_im, row(prm['d']), prm['glu_w'].astype(BF16), row(prm['glu_b']),
              row(prm['out_g'])]
    tok = pl.BlockSpec((1, tb, D_S5), lambda b, t: (b, t, 0))
    return pl.pallas_call(
        _s5_kernel, grid=(bsz, seq // tb),
        in_specs=[tok] + [_full(c.shape) for c in consts], out_specs=tok,
        out_shape=jax.ShapeDtypeStruct((bsz, seq, D_S5), F32),
        scratch_shapes=[pltpu.VMEM((1, S5_MODES), F32), pltpu.VMEM((1, S5_MODES), F32),
                        pltpu.VMEM((tb, S5_MODES), F32), pltpu.VMEM((tb, S5_MODES), F32)],
        compiler_params=_params(("parallel", "arbitrary")),
    )(u, *consts)


def _mix_xattn_kernel(x_ref, yr_ref, ys_ref, wo1_ref, wo2_ref, g_ref, wq_ref, k_ref, v_ref, wo_ref, o_ref):
    x1 = x_ref[0] + _bdot(yr_ref[0], wo1_ref[...]) + _bdot(ys_ref[0], wo2_ref[...])
    h = _rms(x1, g_ref[...])
    q = _bdot(h, wq_ref[...])
    km, vm = k_ref[0], v_ref[0]
    outs = []
    for hd in range(XA_HEADS):
        sl = slice(XA_HEAD * hd, XA_HEAD * (hd + 1))
        s = _bdot_nt(q[:, sl], km[:, sl]) * (XA_HEAD ** -0.5)
        s = s - jnp.max(s, axis=-1, keepdims=True)
        e = jnp.exp(s)
        p = e / jnp.sum(e, axis=-1, keepdims=True)
        outs.append(_bdot(p, vm[:, sl]))
    o = jnp.concatenate(outs, axis=1)
    o_ref[0] = x1 + _bdot(o, wo_ref[...])


def _mix_xattn(x, y_rw, y_s5, w_out, g, wq, kmem, vmem, wo, bsz, seq):
    tm = min(XA_BLOCK, seq)
    consts_a = [w_out[:D_RWKV].astype(BF16), w_out[D_RWKV:].astype(BF16), g.reshape(1, -1).astype(F32),
                wq.astype(BF16)]
    tok = lambda w: pl.BlockSpec((1, tm, w), lambda b, t: (b, t, 0))
    mem = pl.BlockSpec((1, N_MEM, D_MODEL), lambda b, t: (b, 0, 0))
    wo_b = wo.astype(BF16)
    return pl.pallas_call(
        _mix_xattn_kernel, grid=(bsz, seq // tm),
        in_specs=[tok(D_MODEL), tok(D_RWKV), tok(D_S5)] + [_full(c.shape) for c in consts_a]
                 + [mem, mem, _full(wo_b.shape)],
        out_specs=tok(D_MODEL), out_shape=jax.ShapeDtypeStruct((bsz, seq, D_MODEL), F32),
        compiler_params=_params(("parallel", "parallel")),
    )(x, y_rw, y_s5, *consts_a, kmem, vmem, wo_b)


def _top_rows(work, order, aux, val_ref, idx_ref):
    for it in range(PEER_TOPK):
        m = jnp.max(work, axis=0, keepdims=True)
        pos = jnp.min(jnp.where(work == m, order, jnp.inf), axis=0, keepdims=True)
        hit = order == pos
        val_ref[it:it + 1, :] = m
        if aux is None:
            idx_ref[it:it + 1, :] = pos
        else:
            idx_ref[it:it + 1, :] = jnp.sum(jnp.where(hit, aux, 0.0), axis=0, keepdims=True)
        work = jnp.where(hit, -jnp.inf, work)


_CAND_ROW_BLOCKS = [(0, PEER_TOPK), (1, SUBLANES), (2, SUBLANES), (3, SUBLANES)]
_CAND_COL_BLOCKS = [(0, PEER_TOPK, 4, 15), (1, SUBLANES, 4, 7), (2, SUBLANES, 4, 4)]
N_CAND = sum(n for _, n in _CAND_ROW_BLOCKS) + sum(n for _, n, _, _ in _CAND_COL_BLOCKS)


def _cand_consts(tm):
    flat, neg = [], []
    for a, nb in _CAND_ROW_BLOCKS:
        flat += [a * PEER_TOPK + b for b in range(nb)]
        neg += [0.0] * nb
    for b, na, lo, hi in _CAND_COL_BLOCKS:
        flat += [a * PEER_TOPK + b for a in range(na)]
        neg += [0.0 if lo <= a <= hi else -float('inf') for a in range(na)]
    flat = [f if n == 0.0 else 1000.0 + i for i, (f, n) in enumerate(zip(flat, neg))]
    col = lambda v: jnp.broadcast_to(jnp.asarray(v, F32)[:, None], (N_CAND, tm))
    return col(flat), col(neg)


def _cand_rows(row_vals, col_vals, combine):
    blocks = [combine(row_vals[a:a + 1, :], col_vals[0:nb, :]) for a, nb in _CAND_ROW_BLOCKS]
    blocks += [combine(row_vals[0:na, :], col_vals[b:b + 1, :]) for b, na, _, _ in _CAND_COL_BLOCKS]
    return jnp.concatenate(blocks, axis=0)


SEL_HEADS_PER_STEP = 2


def _peer_select_kernel(x_ref, g_ref, wq_ref, keys_ref, cflat_ref, cneg_ref, x8_ref, h8_ref, base_ref, shift_ref,
                        gate_ref, q3, idx_t, gate_t, *lists):
    tm = x_ref.shape[0]
    n = SEL_HEADS_PER_STEP
    s1, i1, s2, i2, top, eid = (lists[j * n:(j + 1) * n] for j in range(6))
    x = x_ref[...]
    h = _rms(x, g_ref[...])
    for r in range(D_MODEL // LANES):
        rows = pl.ds(r, tm, stride=D_MODEL // LANES)
        h8_ref[rows, :] = h[:, LANES * r:LANES * (r + 1)]
        x8_ref[rows, :] = x[:, LANES * r:LANES * (r + 1)]
    q = _bdot(h, wq_ref[...])
    for j in range(2 * PEER_HEADS):
        q3[j] = q[:, LANES * j:LANES * (j + 1)]
    iota_k = lax.broadcasted_iota(jnp.int32, (PEER_NKEYS, tm), 0).astype(F32)

    def heads(step, _):
        for u in range(SEL_HEADS_PER_STEP):
            hd = step * SEL_HEADS_PER_STEP + u
            sc1 = _bdot_nt(keys_ref[2 * hd], q3[2 * hd])
            sc2 = _bdot_nt(keys_ref[2 * hd + 1], q3[2 * hd + 1])
            _top_rows(sc1, iota_k, None, s1[u], i1[u])
            _top_rows(sc2, iota_k, None, s2[u], i2[u])
            cand = _cand_rows(s1[u][...], s2[u][...], lambda x, y: x + y) + cneg_ref[...]
            cidx = _cand_rows(i1[u][...], i2[u][...], lambda x, y: x * float(PEER_NKEYS) + y)
            _top_rows(cand, cflat_ref[...], cidx, top[u], eid[u])
            tv = top[u][...]
            e = jnp.exp(tv - jnp.max(tv, axis=0, keepdims=True))
            rs = pl.ds(pl.multiple_of(hd * PEER_TOPK, PEER_TOPK), PEER_TOPK)
            idx_t[rs, :] = eid[u][...]
            gate_t[rs, :] = e / jnp.sum(e, axis=0, keepdims=True)
        return 0

    lax.fori_loop(0, PEER_HEADS // SEL_HEADS_PER_STEP, heads, 0)
    e_t = idx_t[...].T
    pair = jnp.floor(e_t * 0.5)
    base_ref[...] = (pair * float(SUBLANES)).astype(jnp.int32)
    shift_ref[...] = ((e_t - 2.0 * pair) * 16.0).astype(jnp.int32)
    gate_ref[...] = gate_t[...].T


def _peer_select(x2, g, wq, keys):
    t = x2.shape[0]
    tm = SEL_BLOCK
    keys_b = keys.reshape(2 * PEER_HEADS, PEER_NKEYS, LANES).astype(BF16)
    wq_b = wq.astype(BF16)
    cflat, cneg = _cand_consts(tm)
    tokspec = lambda w: pl.BlockSpec((tm, w), lambda i: (i, 0))
    vm = lambda r: pltpu.VMEM((r, tm), F32)
    return pl.pallas_call(
        _peer_select_kernel, grid=(t // tm,),
        in_specs=[tokspec(D_MODEL), _full((1, D_MODEL)), _full(wq_b.shape), _full(keys_b.shape),
                  _full(cflat.shape), _full(cneg.shape)],
        out_specs=[pl.BlockSpec((tm * SUBLANES, LANES), lambda i: (i, 0))] * 2
                  + [tokspec(PEER_SEL), tokspec(PEER_SEL), tokspec(PEER_SEL)],
        out_shape=[jax.ShapeDtypeStruct((t * SUBLANES, LANES), F32)] * 2 + [
                   jax.ShapeDtypeStruct((t, PEER_SEL), jnp.int32),
                   jax.ShapeDtypeStruct((t, PEER_SEL), jnp.int32), jax.ShapeDtypeStruct((t, PEER_SEL), F32)],
        scratch_shapes=[pltpu.VMEM((2 * PEER_HEADS, tm, LANES), F32), vm(PEER_SEL), vm(PEER_SEL)]
                       + [vm(PEER_TOPK) for _ in range(6 * SEL_HEADS_PER_STEP)],
        compiler_params=_params(("parallel",)),
    )(x2, g.reshape(1, -1).astype(F32), wq_b, keys_b, cflat, cneg)


def _tile_table(tab):
    return tab.reshape(tab.shape[0], D_MODEL // LANES, LANES)


def _pack_table(tab3):
    n = tab3.shape[0]
    bits = lax.bitcast_convert_type(tab3.astype(BF16), jnp.uint16).astype(U32)
    bits = bits.reshape(n // 2, 2, SUBLANES, LANES)
    packed = (bits[:, 0] << 16) | bits[:, 1]
    return lax.bitcast_convert_type(packed, jnp.int32).reshape(n // 2 * SUBLANES, LANES)


def _splat_into(src_ref, t, dst_ref, slot):
    tile = jnp.broadcast_to(src_ref[pl.ds(t, 1), :], (LANES, LANES)).T
    dst_ref[LANES * slot:LANES * (slot + 1), :] = tile


def _bcast_row(ref, row):
    return jnp.broadcast_to(ref[row:row + 1, :], (SUBLANES, LANES))


def _expert_tile(tab_ref, base, shift_splat, row):
    w = tab_ref[pl.ds(pl.multiple_of(base, SUBLANES), SUBLANES), :]
    return lax.bitcast_convert_type((w << _bcast_row(shift_splat, row)) & jnp.int32(-65536), F32)


def _token_rows(t):
    return pl.ds(pl.multiple_of(t * SUBLANES, SUBLANES), SUBLANES)


def _pipelined_tokens(tg, prep, work):
    prep(0, 0)

    def body(i, _):
        t0 = 2 * i
        prep(t0 + 1, 1)
        work(t0, 0)
        prep(jnp.minimum(t0 + 2, tg - 1), 0)
        work(t0 + 1, 1)
        return 0

    lax.fori_loop(0, tg // 2, body, 0)


def _table_spec(shape):
    return pl.BlockSpec(shape, lambda i: (0, 0), pipeline_mode=pl.Buffered(1))


N_CHUNK = D_MODEL // LANES
CHUNK_STRIDE = PEER_SEL + SUBLANES


def _peer_u_kernel(base_ref, shift_ref, h_ref, gate_ref, tab_ref, eye_ref, c_ref, shift_splat, *planes):
    tg = gate_ref.shape[0]
    ones = jnp.ones((LANES, LANES), BF16)

    half = N_CHUNK // 2

    def prep(t, slot):
        _splat_into(shift_ref, t, shift_splat, slot)

    def gather(t, slot):
        ht = h_ref[_token_rows(t), :]
        for kx in range(PEER_SEL):
            prod = _expert_tile(tab_ref, base_ref[t, kx], shift_splat, LANES * slot + kx) * ht
            fold = prod + pltpu.roll(prod, half, 0)
            planes[slot][pl.ds(kx, half, stride=CHUNK_STRIDE), :] = fold[0:half, :]

    def finish(t, slot):
        plane = planes[slot]
        acc = plane[0:PEER_SEL, :]
        for r in range(1, half):
            acc = acc + plane[CHUNK_STRIDE * r:CHUNK_STRIDE * r + PEER_SEL, :]
        hi = acc.astype(BF16)
        lo = (acc - hi.astype(F32)).astype(BF16)
        tot = jnp.dot(hi, ones, preferred_element_type=F32) + jnp.dot(lo, ones, preferred_element_type=F32)
        score = jnp.sum(tot * eye_ref[...], axis=0, keepdims=True)
        c_ref[pl.ds(t, 1), :] = gate_ref[pl.ds(t, 1), :] * _gelu(score)

    planes[1][...] = jnp.zeros_like(planes[1])
    prep(0, 0)

    def body(i, _):
        t0 = 2 * i
        prep(t0 + 1, 1)
        gather(t0, 0)
        finish(jnp.maximum(t0 - 1, 0), 1)
        prep(jnp.minimum(t0 + 2, tg - 1), 0)
        gather(t0 + 1, 1)
        finish(t0, 0)
        return 0

    lax.fori_loop(0, tg // 2, body, 0)
    finish(tg - 1, 1)


def _peer_u(base, shift, h8, gate, tab, t):
    tg = GATHER_BLOCK
    eye = jnp.eye(LANES, dtype=F32)
    tokrow = pl.BlockSpec((tg, PEER_SEL), lambda i: (i, 0))
    return pl.pallas_call(
        _peer_u_kernel, grid=(t // tg,),
        in_specs=[pl.BlockSpec((tg, PEER_SEL), lambda i: (i, 0), memory_space=pltpu.SMEM), tokrow,
                  pl.BlockSpec((tg * SUBLANES, LANES), lambda i: (i, 0)), tokrow,
                  _table_spec(tab.shape), _full(eye.shape)],
        out_specs=tokrow, out_shape=jax.ShapeDtypeStruct((t, PEER_SEL), F32),
        scratch_shapes=[pltpu.VMEM((2 * LANES, LANES), jnp.int32)]
                       + [pltpu.VMEM((N_CHUNK // 2 * CHUNK_STRIDE, LANES), F32)] * 2,
        compiler_params=_params(("parallel",)),
    )(base, shift, h8, gate, tab, eye)


def _peer_v_kernel(base_ref, shift_ref, c_ref, x_ref, tab_ref, _aliased_out, o_ref, shift_splat, c_splat, otile):
    tg = c_ref.shape[0]
    n_acc = 4

    def prep(t, slot):
        _splat_into(shift_ref, t, shift_splat, slot)
        _splat_into(c_ref, t, c_splat, slot)

    def work(t, slot):
        accs = [jnp.zeros((SUBLANES, LANES), F32) for _ in range(n_acc)]
        for kx in range(PEER_SEL):
            row = LANES * slot + kx
            accs[kx % n_acc] = accs[kx % n_acc] + (_bcast_row(c_splat, row)
                                                   * _expert_tile(tab_ref, base_ref[t, kx], shift_splat, row))
        rs = _token_rows(t)
        otile[rs, :] = x_ref[rs, :] + ((accs[0] + accs[1]) + (accs[2] + accs[3]))

    _pipelined_tokens(tg, prep, work)
    for r in range(N_CHUNK):
        o_ref[:, LANES * r:LANES * (r + 1)] = otile[pl.ds(r, tg, stride=N_CHUNK), :]


def _peer_v(base, shift, c, x8, x2, tab, t):
    tg = GATHER_BLOCK
    smem = pl.BlockSpec((tg, PEER_SEL), lambda i: (i, 0), memory_space=pltpu.SMEM)
    tokrow = pl.BlockSpec((tg, PEER_SEL), lambda i: (i, 0))
    tile = pl.BlockSpec((tg * SUBLANES, LANES), lambda i: (i, 0))
    return pl.pallas_call(
        _peer_v_kernel, grid=(t // tg,),
        in_specs=[smem, tokrow, tokrow, tile, _table_spec(tab.shape), pl.BlockSpec(memory_space=pl.ANY)],
        out_specs=pl.BlockSpec((tg, D_MODEL), lambda i: (i, 0)),
        out_shape=jax.ShapeDtypeStruct(x2.shape, F32),
        input_output_aliases={5: 0},
        scratch_shapes=[pltpu.VMEM((2 * LANES, LANES), jnp.int32), pltpu.VMEM((2 * LANES, LANES), F32),
                        pltpu.VMEM((tg * SUBLANES, LANES), F32)],
        compiler_params=_params(("parallel",)),
    )(base, shift, c, x8, tab, x2)


SC_WORKERS = 32
SC_LANES = 16
SC_ROWS = 32
SC_SHARE_PIPELINED = 46 / 64
SC_SHARE_LAST = 28 / 64
SEQ_GROUPS = 4
SC_GROUP = 8
SC_REGS = 32


def _peer_sc(eidx, gate, h8, x8, tab_u, tab_v, t_off):
    ts = eidx.shape[0]
    per_w = ts // SC_WORKERS
    n_chunk = PEER_SEL // SC_ROWS
    grp = SC_GROUP
    sel = grp * PEER_SEL
    n_pairs = grp * n_chunk // 2
    per_tile = LANES // SC_LANES
    n_blk = D_MODEL // (SC_REGS * SC_LANES)
    mesh = plsc.VectorSubcoreMesh(core_axis_name="c", subcore_axis_name="s")

    def piece(ref, lead, q):
        return ref[lead, q // per_tile, pl.ds(SC_LANES * (q % per_tile), SC_LANES)]

    def body(u_hbm, v_hbm, idx_hbm, gate_hbm, h_hbm, x_hbm, out_hbm,
             idx_v, c_v, h_v, acc_v, part_v, rows0, rows1, sem0, sem1):
        wid = lax.axis_index("s") * 2 + lax.axis_index("c")
        bufs, sems = (rows0, rows1), (sem0, sem1)
        lane = lax.iota(jnp.int32, SC_LANES)

        def stream(tab_hbm, compute):
            def gather(chunk, slot):
                start = pl.multiple_of(chunk * SC_ROWS, SC_ROWS)
                return pltpu.make_async_copy(tab_hbm.at[idx_v.at[pl.ds(start, SC_ROWS)]], bufs[slot], sems[slot])

            gather(0, 0).start()

            @pl.loop(0, n_pairs)
            def _(p):
                c0 = 2 * p
                gather(c0 + 1, 1).start()
                gather(c0, 0).wait()
                compute(c0, 0)
                gather(jnp.minimum(c0 + 2, 2 * n_pairs - 1), 0).start()
                gather(c0 + 1, 1).wait()
                compute(c0 + 1, 1)

            gather(0, 0).wait()

        def dots(chunk, slot):
            tok = chunk // n_chunk
            for blk in range(n_blk):
                where = [blk * SC_REGS + j for j in range(SC_REGS)]
                hs = [piece(h_v, tok, q) for q in where]

                def row(k, carry):
                    parts = [None] * 4
                    for i, q in enumerate(where):
                        term = piece(bufs[slot], k, q) * hs[i]
                        parts[i % 4] = term if parts[i % 4] is None else parts[i % 4] + term
                    tot = (parts[0] + parts[1]) + (parts[2] + parts[3])
                    at = pl.ds(pl.multiple_of((chunk * SC_ROWS + k) * SC_LANES, SC_LANES), SC_LANES)
                    if blk == 0:
                        part_v[at] = tot
                    else:
                        part_v[at] = part_v[at] + tot
                    return carry

                lax.fori_loop(0, SC_ROWS, row, 0)

        def weigh(chunk, slot):
            tok = chunk // n_chunk
            for blk in range(n_blk):
                where = [blk * SC_REGS + j for j in range(SC_REGS)]

                def row(k, accs):
                    ck = plsc.load_gather(c_v, [jnp.full((SC_LANES,), chunk * SC_ROWS, jnp.int32) + k])
                    return tuple(a + ck * piece(bufs[slot], k, q) for a, q in zip(accs, where))

                accs = lax.fori_loop(0, SC_ROWS, row, tuple(piece(acc_v, tok, q) for q in where))
                for a, q in zip(accs, where):
                    acc_v[tok, q // per_tile, pl.ds(SC_LANES * (q % per_tile), SC_LANES)] = a

        @pl.loop(0, per_w // grp)
        def _(g):
            t0 = wid * per_w + g * grp
            flat = pl.ds(pl.multiple_of(t0 * PEER_SEL, sel), sel)
            pltpu.sync_copy(idx_hbm.at[flat], idx_v)
            pltpu.sync_copy(gate_hbm.at[flat], c_v)
            pltpu.sync_copy(h_hbm.at[pl.ds(t_off + t0, grp)], h_v)
            pltpu.sync_copy(x_hbm.at[pl.ds(t_off + t0, grp)], acc_v)
            stream(u_hbm, dots)

            @pl.loop(0, sel // SC_LANES)
            def _(m):
                base = (m * SC_LANES + lane) * SC_LANES
                score = plsc.load_gather(part_v, [base])
                for l in range(1, SC_LANES):
                    score = score + plsc.load_gather(part_v, [base + l])
                z = math.sqrt(2.0 / math.pi) * (score + 0.044715 * (score * score * score))
                tanh_z = 1.0 - 2.0 / (jnp.exp(2.0 * z) + 1.0)
                at = pl.ds(pl.multiple_of(m * SC_LANES, SC_LANES), SC_LANES)
                c_v[at] = c_v[at] * (0.5 * score * (1.0 + tanh_z))

            stream(v_hbm, weigh)
            pltpu.sync_copy(acc_v, out_hbm.at[pl.ds(t0, grp)])

    tile = lambda n: pltpu.VMEM((n, N_CHUNK, LANES), F32)
    out = pl.kernel(
        body, mesh=mesh, out_type=jax.ShapeDtypeStruct((ts, N_CHUNK, LANES), F32),
        scratch_types=[pltpu.VMEM((sel,), jnp.int32), pltpu.VMEM((sel,), F32), tile(grp), tile(grp),
                       pltpu.VMEM((sel * SC_LANES,), F32), tile(SC_ROWS), tile(SC_ROWS),
                       pltpu.SemaphoreType.DMA, pltpu.SemaphoreType.DMA],
        compiler_params=pltpu.CompilerParams(needs_layout_passes=False, use_tc_tiling_on_sc=True),
    )(tab_u, tab_v, eidx.reshape(-1), gate.reshape(-1),
      h8.reshape(-1, N_CHUNK, LANES), x8.reshape(-1, N_CHUNK, LANES))
    return out


def _final_norm_kernel(x_ref, g_ref, o_ref):
    o_ref[...] = _rms(x_ref[...], g_ref[...])


def _final_norm(x2, g):
    t, d = x2.shape
    tm = min(PROJ_BLOCK, t)
    spec = pl.BlockSpec((tm, d), lambda i: (i, 0))
    return pl.pallas_call(
        _final_norm_kernel, grid=(t // tm,), in_specs=[spec, _full((1, d))], out_specs=spec,
        out_shape=jax.ShapeDtypeStruct((t, d), F32), compiler_params=_params(("parallel",)),
    )(x2, g.reshape(1, d).astype(F32))


def kernel(x, mem, norm_mix, w_in, rw_mu, rw_w0, rw_w2, rw_a0, rw_a2, rw_g2, rw_kk, rw_ka, rw_rk, rw_v0, rw_v1, rw_v2, rw_lnx_g, rw_lnx_b, s5_a_re, s5_a_im, s5_log_dt, s5_b_re, s5_b_im, s5_c_re, s5_c_im, s5_d, s5_glu_w, s5_glu_b, s5_out_g, w_out, norm_xa, norm_mem, xa_wq, xa_wk, xa_wv, xa_wo, norm_ffn, peer_wq, peer_keys, peer_u, peer_v, norm_final):
    bsz, seq, d = x.shape
    depth = w_in.shape[0]

    def layer(l, xg, memg, v_first, tabs, sc_share):
        nb = xg.shape[0]
        t = nb * seq
        x2 = xg.reshape(t, d)
        ws = [w_in[l][:, :RW_COLS].astype(BF16), w_in[l][:, RW_COLS:].astype(BF16)]
        if l > 0:
            ws.append(_pad_rows(rw_v1[l - 1].T, LANES, 0).T.astype(BF16))
        outs = _norm_proj(x2, norm_mix[l], ws, [F32] * len(ws), PROJ_BLOCK)
        z_rw = outs[0].reshape(nb, seq, RW_COLS)
        u_s5 = outs[1].reshape(nb, seq, D_S5)
        hv = outs[2].reshape(nb, seq, LANES) if l > 0 else None
        rw_prm = dict(mu=rw_mu[l], w0=rw_w0[l], w2=rw_w2[l], a0=rw_a0[l], a2=rw_a2[l], g2=rw_g2[l],
                      kk=rw_kk[l], ka=rw_ka[l], rk=rw_rk[l], lng=rw_lnx_g[l], lnb=rw_lnx_b[l])
        if l > 0:
            rw_prm.update(v0=rw_v0[l - 1], v2=rw_v2[l - 1])
        y_rw, v_first = _rwkv(z_rw, hv, v_first, rw_prm, nb, seq)
        s5_prm = dict(a_re=s5_a_re[l], a_im=s5_a_im[l], log_dt=s5_log_dt[l], b_re=s5_b_re[l], b_im=s5_b_im[l],
                      c_re=s5_c_re[l], c_im=s5_c_im[l], d=s5_d[l], glu_w=s5_glu_w[l], glu_b=s5_glu_b[l],
                      out_g=s5_out_g[l])
        y_s5 = _s5(u_s5, s5_prm, nb, seq)
        kv = _norm_proj(memg.reshape(nb * N_MEM, d), norm_mem[l],
                        [xa_wk[l].astype(BF16), xa_wv[l].astype(BF16)], [BF16, BF16], PROJ_BLOCK)
        kmem = kv[0].reshape(nb, N_MEM, d)
        vmem = kv[1].reshape(nb, N_MEM, d)
        xg = _mix_xattn(xg, y_rw, y_s5, w_out[l], norm_xa[l], xa_wq[l], kmem, vmem, xa_wo[l], nb, seq)
        x2 = xg.reshape(t, d)
        x8, h8, base, shift, gate = _peer_select(x2, norm_ffn[l], peer_wq[l], peer_keys[l])
        u3, v3, u_packed, v_packed = tabs
        n_sc = int(t * sc_share)
        assert n_sc % (SC_WORKERS * SC_GROUP) == 0 and (t - n_sc) % GATHER_BLOCK == 0
        t_tc = t - n_sc
        eidx_sc = (base[t_tc:] >> 2) + (shift[t_tc:] >> 4)
        x_sc = _peer_sc(eidx_sc, gate[t_tc:], h8, x8, u3, v3, t_tc)
        c = _peer_u(base, shift, h8, gate, u_packed, t_tc)
        x_new = _peer_v(base, shift, c, x8, x2, v_packed, t_tc)
        xg = lax.dynamic_update_slice(x_new, x_sc.reshape(n_sc, d), (t_tc, 0)).reshape(nb, seq, d)
        return xg, v_first

    nb = bsz // SEQ_GROUPS
    xs = [x[g * nb:(g + 1) * nb] for g in range(SEQ_GROUPS)]
    mems = [mem[g * nb:(g + 1) * nb] for g in range(SEQ_GROUPS)]
    v_firsts = [None] * SEQ_GROUPS
    for l in range(depth):
        u3, v3 = _tile_table(peer_u[l]), _tile_table(peer_v[l])
        tabs = (u3, v3, _pack_table(u3), _pack_table(v3))
        for g in range(SEQ_GROUPS):
            last = l == depth - 1 and g == SEQ_GROUPS - 1
            share = SC_SHARE_LAST if last else SC_SHARE_PIPELINED
            xs[g], v_firsts[g] = layer(l, xs[g], mems[g], v_firsts[g], tabs, share)
    outs = [_final_norm(xg.reshape(nb * seq, d), norm_final).reshape(nb, seq, d) for xg in xs]
    return jnp.concatenate(outs, axis=0)
```

```python
import functools
import math

import jax
import jax.numpy as jnp
from jax import lax
from jax.experimental import pallas as pl
from jax.experimental.pallas import tpu as pltpu
from jax.experimental.pallas import tpu_sc as plsc

F32 = jnp.float32
BF16 = jnp.bfloat16
U32 = jnp.uint32

LANES = 128
SUBLANES = 8
VMEM_LIMIT = 56 * 1024 * 1024

D_MODEL = 1024
D_RWKV = 512
RW_HEAD = 64
RW_COLS = 1792
D_S5 = 512
S5_GROUPS = 32
S5_CH = 16
S5_STATE = 64
S5_MODES = S5_GROUPS * S5_STATE
N_MEM = 256
XA_HEADS = 4
XA_HEAD = 256
PEER_HEADS = 8
PEER_NKEYS = 128
PEER_TOPK = 16
PEER_SEL = PEER_HEADS * PEER_TOPK
RMS_EPS = 1e-6
GN_EPS = 64e-5

RW_CHUNK = 128
S5_BLOCK = 256
PROJ_BLOCK = 512
XA_BLOCK = 256
SEL_BLOCK = 128
GATHER_BLOCK = 64


def _params(sem):
    return pltpu.CompilerParams(dimension_semantics=sem, vmem_limit_bytes=VMEM_LIMIT)


def _rms(x, g):
    ms = jnp.mean(x * x, axis=-1, keepdims=True)
    return x * lax.rsqrt(ms + RMS_EPS) * g


def _bdot(a, b):
    return jnp.dot(a.astype(BF16), b.astype(BF16), preferred_element_type=F32)


def _bdot_nt(a, b):
    return lax.dot_general(a.astype(BF16), b.astype(BF16), (((1,), (1,)), ((), ())),
                           preferred_element_type=F32)


def _sigmoid(x):
    return 1.0 / (1.0 + jnp.exp(-x))


def _softplus(x):
    return jnp.maximum(x, 0.0) + jnp.log(1.0 + jnp.exp(-jnp.abs(x)))


def _gelu(x):
    return 0.5 * x * (1.0 + jnp.tanh(math.sqrt(2.0 / math.pi) * (x + 0.044715 * (x * x * x))))


def _full(shape):
    n = len(shape)
    return pl.BlockSpec(shape, lambda *_: (0,) * n)


def _norm_proj_kernel(*refs, n_out):
    x_ref, g_ref = refs[0], refs[1]
    w_refs = refs[2:2 + n_out]
    o_refs = refs[2 + n_out:]
    h = _rms(x_ref[...], g_ref[...]).astype(BF16)
    for w_ref, o_ref in zip(w_refs, o_refs):
        o_ref[...] = jnp.dot(h, w_ref[...], preferred_element_type=F32).astype(o_ref.dtype)


def _norm_proj(x2, g, ws, out_dtypes, block):
    t, d = x2.shape
    block = min(block, t)
    in_specs = [pl.BlockSpec((block, d), lambda i: (i, 0)), _full((1, d))]
    in_specs += [_full(w.shape) for w in ws]
    out_specs = [pl.BlockSpec((block, w.shape[1]), lambda i: (i, 0)) for w in ws]
    out_shape = [jax.ShapeDtypeStruct((t, w.shape[1]), dt) for w, dt in zip(ws, out_dtypes)]
    return pl.pallas_call(
        functools.partial(_norm_proj_kernel, n_out=len(ws)),
        grid=(t // block,), in_specs=in_specs, out_specs=out_specs, out_shape=out_shape,
        compiler_params=_params(("parallel",)),
    )(x2, g.reshape(1, d), *ws)


def _seg_sum(x, mseg):
    hi = x.astype(BF16)
    lo = (x - hi.astype(F32)).astype(BF16)
    return (jnp.dot(hi, mseg, preferred_element_type=F32)
            + jnp.dot(lo, mseg, preferred_element_type=F32))


def _col_bcast(row):
    return jnp.broadcast_to(row, (LANES, LANES)).T


def _rwkv_kernel(*refs, first_layer):
    if first_layer:
        (z_ref, mu_ref, w0_ref, w2_ref, a0_ref, a2_ref, g2_ref, kk_ref, ka_ref, rk_ref,
         lng_ref, lnb_ref, mseg_ref, y_ref, vf_out_ref, zprev, hst) = refs
    else:
        (z_ref, hv_ref, vf_ref, v0_ref, v2_ref, mu_ref, w0_ref, w2_ref, a0_ref, a2_ref, g2_ref,
         kk_ref, ka_ref, rk_ref, lng_ref, lnb_ref, mseg_ref, y_ref, zprev, hst) = refs
    L = RW_CHUNK

    @pl.when(pl.program_id(1) == 0)
    def _():
        zprev[...] = jnp.zeros_like(zprev)
        hst[...] = jnp.zeros_like(hst)

    z = z_ref[0]
    row = lax.broadcasted_iota(jnp.int32, (L, 1), 0)
    zs = jnp.where(row == 0, zprev[...], pltpu.roll(z, 1, 0))
    zprev[...] = z[L - 1:L, :]
    z = z + (zs - z) * mu_ref[...]
    r = z[:, 0:512]
    k = z[:, 512:1024]
    v = z[:, 1024:1536]
    wa = z[:, 1536:1664]
    gd = z[:, 1664:1792]
    mseg = mseg_ref[...]

    wlin = w0_ref[...] + _bdot(jnp.tanh(wa), w2_ref[...])
    lw = -jnp.exp(-_softplus(-wlin) - 0.5)
    a = _sigmoid(a0_ref[...] + _bdot(wa, a2_ref[...]))
    g = _bdot(_sigmoid(gd), g2_ref[...])
    if first_layer:
        vf_out_ref[0] = v
    else:
        v = v + (vf_ref[0] - v) * _sigmoid(v0_ref[...] + _bdot(hv_ref[0], v2_ref[...]))
    kk = k * kk_ref[...]
    kk = kk / jnp.maximum(jnp.sqrt(_seg_sum(kk * kk, mseg)), 1e-12)
    k2 = k * (1.0 + (a - 1.0) * ka_ref[...])
    av = -kk
    bv = kk * a

    ti = lax.broadcasted_iota(jnp.int32, (L, L), 0)
    si = lax.broadcasted_iota(jnp.int32, (L, L), 1)
    tril = (ti >= si).astype(F32)
    cum = jnp.dot(tril, lw, preferred_element_type=F32, precision=lax.Precision.HIGHEST)
    mid = cum[L // 2 - 1:L // 2, :]
    cm = cum - mid
    ecum = jnp.exp(cm)
    einv = jnp.exp(-cm)
    rt = r * ecum
    kt = k2 * einv
    bt = bv * einv
    at = av * jnp.exp(cm - lw)
    p_end = jnp.exp(cum[L - 1:L, :])
    e_end = ecum[L - 1:L, :]
    e_mid = jnp.exp(mid)

    lane = lax.broadcasted_iota(jnp.int32, (1, LANES), 1)
    m0 = (lane < RW_HEAD).astype(F32)
    m1 = 1.0 - m0
    strict = ti > si
    incl = ti >= si
    bi = lax.broadcasted_iota(jnp.int32, (LANES, LANES), 0) < RW_HEAD
    bj = lax.broadcasted_iota(jnp.int32, (LANES, LANES), 1) < RW_HEAD
    bdmask = (bi == bj).astype(F32)
    zeros_ll = jnp.zeros((L, L), F32)

    ys = []
    for p in range(D_RWKV // LANES):
        sl = slice(LANES * p, LANES * (p + 1))
        A, B, K, R, V = at[:, sl], bt[:, sl], kt[:, sl], rt[:, sl], v[:, sl]
        btkt = jnp.concatenate([B.T, K.T], axis=1)
        h0 = hst[p]
        h0m = h0 * _col_bcast(e_mid[:, sl])
        sc = _bdot(jnp.concatenate([A * m0, A * m1, R * m0, R * m1], axis=0), btkt)
        aab = [jnp.where(strict, sc[e * L:(e + 1) * L, 0:L], 0.0) for e in range(2)]
        aak = [jnp.where(strict, sc[e * L:(e + 1) * L, L:2 * L], 0.0) for e in range(2)]
        arb = [jnp.where(incl, sc[(2 + e) * L:(3 + e) * L, 0:L], 0.0) for e in range(2)]
        ark = [jnp.where(incl, sc[(2 + e) * L:(3 + e) * L, L:2 * L], 0.0) for e in range(2)]
        arh = _bdot(jnp.concatenate([A, R], axis=0), h0m)
        v01 = jnp.concatenate([V * m0, V * m1], axis=0)
        x = arh[0:L] + _bdot(jnp.concatenate(aak, axis=1), v01)
        pm = jnp.concatenate(aab, axis=1)
        n_fac = int(math.log2(L))
        for it in range(n_fac):
            x = x + _bdot(pm, jnp.concatenate([x * m0, x * m1], axis=0))
            if it + 1 < n_fac:
                pd = jnp.concatenate(
                    [jnp.concatenate([pm[:, 0:L], zeros_ll], axis=1),
                     jnp.concatenate([zeros_ll, pm[:, L:2 * L]], axis=1)], axis=0)
                pm = _bdot(pm, pd)
        u = x
        yp = arh[L:2 * L] + _bdot(jnp.concatenate(arb + ark, axis=1),
                                  jnp.concatenate([u * m0, u * m1, v01], axis=0))
        upd = _bdot(btkt, jnp.concatenate([u, V], axis=0))
        hst[p] = (h0 * _col_bcast(p_end[:, sl]) + upd * _col_bcast(e_end[:, sl])) * bdmask
        ys.append(yp)
    y = jnp.concatenate(ys, axis=1)

    mean = _seg_sum(y, mseg) * (1.0 / RW_HEAD)
    d = y - mean
    var = _seg_sum(d * d, mseg) * (1.0 / RW_HEAD)
    yn = d * lax.rsqrt(var + GN_EPS) * lng_ref[...] + lnb_ref[...]
    bonus = _seg_sum(r * k2 * rk_ref[...], mseg) * v
    y_ref[0] = (yn + bonus) * g


def _pad_rows(w, rows, offset):
    out = jnp.zeros((rows, w.shape[1]), w.dtype)
    return out.at[offset:offset + w.shape[0]].set(w)


def _rwkv(z_rw, hv, v_first, prm, bsz, seq):
    L = RW_CHUNK
    first = v_first is None
    row = lambda a: a.reshape(1, -1).astype(F32)
    hid = jnp.arange(D_RWKV) // RW_HEAD
    mseg = (hid[:, None] == hid[None, :]).astype(BF16)
    w2p = _pad_rows(prm['w2'], LANES, 0).astype(BF16)
    a2p = _pad_rows(prm['a2'], LANES, 64).astype(BF16)
    common = [row(prm['mu']), row(prm['w0']), w2p, row(prm['a0']), a2p, prm['g2'].astype(BF16),
              row(prm['kk']), row(prm['ka']), row(prm['rk']), row(prm['lng']), row(prm['lnb']), mseg]
    tok = lambda w: pl.BlockSpec((1, L, w), lambda b, t: (b, t, 0))
    common_specs = [_full(c.shape) for c in common]
    y_shape = jax.ShapeDtypeStruct((bsz, seq, D_RWKV), F32)
    scratch = [pltpu.VMEM((1, RW_COLS), F32), pltpu.VMEM((D_RWKV // LANES, LANES, LANES), F32)]
    if first:
        args = [z_rw] + common
        in_specs = [tok(RW_COLS)] + common_specs
        out_shape = [y_shape, y_shape]
        out_specs = [tok(D_RWKV), tok(D_RWKV)]
    else:
        v2p = _pad_rows(prm['v2'], LANES, 0).astype(BF16)
        extra = [row(prm['v0']), v2p]
        args = [z_rw, hv, v_first] + extra + common
        in_specs = [tok(RW_COLS), tok(LANES), tok(D_RWKV)] + [_full(c.shape) for c in extra] + common_specs
        out_shape = [y_shape]
        out_specs = [tok(D_RWKV)]
    outs = pl.pallas_call(
        functools.partial(_rwkv_kernel, first_layer=first),
        grid=(bsz, seq // L), in_specs=in_specs, out_specs=out_specs, out_shape=out_shape,
        scratch_shapes=scratch, compiler_params=_params(("parallel", "arbitrary")),
    )(*args)
    return (outs[0], outs[1]) if first else (outs[0], v_first)


def _s5_kernel(u_ref, wb_ref, wc_ref, lpr_ref, lpi_ref, d_ref, gw_ref, gb_ref, og_ref, o_ref,
               car_re, car_im, xre, xim):
    tb = u_ref.shape[1]

    @pl.when(pl.program_id(1) == 0)
    def _():
        car_re[...] = jnp.zeros_like(car_re)
        car_im[...] = jnp.zeros_like(car_im)

    u = u_ref[0]
    bu = _bdot(u, wb_ref[...])
    xre[...] = bu[:, 0:S5_MODES]
    xim[...] = bu[:, S5_MODES:2 * S5_MODES]
    row = lax.broadcasted_iota(jnp.int32, (SUBLANES, 1), 0)

    def tile(i, carry):
        cr, ci = carry
        rs = pl.ds(pl.multiple_of(i * SUBLANES, SUBLANES), SUBLANES)
        br, bi = xre[rs, :], xim[rs, :]
        for dist in (1, 2, 4):
            keep = row >= dist
            sr = jnp.where(keep, pltpu.roll(br, dist, 0), 0.0)
            si = jnp.where(keep, pltpu.roll(bi, dist, 0), 0.0)
            lr = lpr_ref[dist - 1:dist, :]
            li = lpi_ref[dist - 1:dist, :]
            br, bi = br + lr * sr - li * si, bi + lr * si + li * sr
        pr, pi = lpr_ref[...], lpi_ref[...]
        xr = br + pr * cr - pi * ci
        xi = bi + pr * ci + pi * cr
        xre[rs, :] = xr
        xim[rs, :] = xi
        return xr[SUBLANES - 1:SUBLANES, :], xi[SUBLANES - 1:SUBLANES, :]

    cr, ci = lax.fori_loop(0, tb // SUBLANES, tile, (car_re[...], car_im[...]))
    car_re[...] = cr
    car_im[...] = ci
    wc = wc_ref[...]
    y = _bdot(xre[...], wc[0:S5_MODES]) + _bdot(xim[...], wc[S5_MODES:2 * S5_MODES])
    y = _gelu(y + d_ref[...] * u)
    y = y * _sigmoid(_bdot(y, gw_ref[...]) + gb_ref[...])
    o_ref[0] = _rms(y, og_ref[...])


def _s5_weights(a_re, a_im, log_dt, b_re, b_im, c_re, c_im):
    lam_re = jnp.minimum(a_re.astype(F32), -1e-4)
    lam_im = a_im.astype(F32)
    dt = jnp.exp(log_dt.astype(F32))[:, None]
    mag = jnp.exp(lam_re * dt)
    lb_re = mag * jnp.cos(lam_im * dt)
    lb_im = mag * jnp.sin(lam_im * dt)
    den = lam_re * lam_re + lam_im * lam_im
    c1_re = ((lb_re - 1.0) * lam_re + lb_im * lam_im) / den
    c1_im = (lb_im * lam_re - (lb_re - 1.0) * lam_im) / den
    br, bi = b_re.astype(F32), b_im.astype(F32)
    bb_re = c1_re[..., None] * br - c1_im[..., None] * bi
    bb_im = c1_re[..., None] * bi + c1_im[..., None] * br
    eye = jnp.eye(S5_GROUPS, dtype=F32)
    wb_re = jnp.einsum('gpc,gh->gchp', bb_re, eye).reshape(D_S5, S5_MODES)
    wb_im = jnp.einsum('gpc,gh->gchp', bb_im, eye).reshape(D_S5, S5_MODES)
    wb = jnp.concatenate([wb_re, wb_im], axis=1).astype(BF16)
    wc_re = jnp.einsum('gcp,gh->gphc', c_re.astype(F32), eye).reshape(S5_MODES, D_S5)
    wc_im = jnp.einsum('gcp,gh->gphc', c_im.astype(F32), eye).reshape(S5_MODES, D_S5)
    wc = jnp.concatenate([wc_re, -wc_im], axis=0).astype(BF16)
    pr, pi = [lb_re], [lb_im]
    for _ in range(SUBLANES - 1):
        pr, pi = pr + [pr[-1] * lb_re - pi[-1] * lb_im], pi + [pr[-1] * lb_im + pi[-1] * lb_re]
    lp_re = jnp.stack(pr).reshape(SUBLANES, S5_MODES)
    lp_im = jnp.stack(pi).reshape(SUBLANES, S5_MODES)
    return wb, wc, lp_re, lp_im


def _s5(u, prm, bsz, seq):
    tb = min(S5_BLOCK, seq)
    wb, wc, lp_re, lp_im = _s5_weights(prm['a_re'], prm['a_im'], prm['log_dt'], prm['b_re'], prm['b_im'],
                                       prm['c_re'], prm['c_im'])
    row = lambda a: a.reshape(1, -1).astype(F32)
    consts = [wb, wc, lp_re, lp_im, row(prm['d']), prm['glu_w'].astype(BF16), row(prm['glu_b']),
              row(prm['out_g'])]
    tok = pl.BlockSpec((1, tb, D_S5), lambda b, t: (b, t, 0))
    return pl.pallas_call(
        _s5_kernel, grid=(bsz, seq // tb),
        in_specs=[tok] + [_full(c.shape) for c in consts], out_specs=tok,
        out_shape=jax.ShapeDtypeStruct((bsz, seq, D_S5), F32),
        scratch_shapes=[pltpu.VMEM((1, S5_MODES), F32), pltpu.VMEM((1, S5_MODES), F32),
                        pltpu.VMEM((tb, S5_MODES), F32), pltpu.VMEM((tb, S5_MODES), F32)],
        compiler_params=_params(("parallel", "arbitrary")),
    )(u, *consts)


def _mix_xattn_kernel(x_ref, yr_ref, ys_ref, wo1_ref, wo2_ref, g_ref, wq_ref, k_ref, v_ref, wo_ref, o_ref):
    x1 = x_ref[0] + _bdot(yr_ref[0], wo1_ref[...]) + _bdot(ys_ref[0], wo2_ref[...])
    h = _rms(x1, g_ref[...])
    q = _bdot(h, wq_ref[...])
    km, vm = k_ref[0], v_ref[0]
    outs = []
    for hd in range(XA_HEADS):
        sl = slice(XA_HEAD * hd, XA_HEAD * (hd + 1))
        s = _bdot_nt(q[:, sl], km[:, sl]) * (XA_HEAD ** -0.5)
        s = s - jnp.max(s, axis=-1, keepdims=True)
        e = jnp.exp(s)
        p = e / jnp.sum(e, axis=-1, keepdims=True)
        outs.append(_bdot(p, vm[:, sl]))
    o = jnp.concatenate(outs, axis=1)
    o_ref[0] = x1 + _bdot(o, wo_ref[...])


def _mix_xattn(x, y_rw, y_s5, w_out, g, wq, kmem, vmem, wo, bsz, seq):
    tm = min(XA_BLOCK, seq)
    consts_a = [w_out[:D_RWKV].astype(BF16), w_out[D_RWKV:].astype(BF16), g.reshape(1, -1).astype(F32),
                wq.astype(BF16)]
    tok = lambda w: pl.BlockSpec((1, tm, w), lambda b, t: (b, t, 0))
    mem = pl.BlockSpec((1, N_MEM, D_MODEL), lambda b, t: (b, 0, 0))
    wo_b = wo.astype(BF16)
    return pl.pallas_call(
        _mix_xattn_kernel, grid=(bsz, seq // tm),
        in_specs=[tok(D_MODEL), tok(D_RWKV), tok(D_S5)] + [_full(c.shape) for c in consts_a]
                 + [mem, mem, _full(wo_b.shape)],
        out_specs=tok(D_MODEL), out_shape=jax.ShapeDtypeStruct((bsz, seq, D_MODEL), F32),
        compiler_params=_params(("parallel", "parallel")),
    )(x, y_rw, y_s5, *consts_a, kmem, vmem, wo_b)


def _top_rows(work, order, aux, val_ref, idx_ref):
    for it in range(PEER_TOPK):
        m = jnp.max(work, axis=0, keepdims=True)
        pos = jnp.min(jnp.where(work == m, order, jnp.inf), axis=0, keepdims=True)
        hit = order == pos
        val_ref[it:it + 1, :] = m
        if aux is None:
            idx_ref[it:it + 1, :] = pos
        else:
            idx_ref[it:it + 1, :] = jnp.sum(jnp.where(hit, aux, 0.0), axis=0, keepdims=True)
        work = jnp.where(hit, -jnp.inf, work)


_CAND_ROW_BLOCKS = [(0, PEER_TOPK), (1, SUBLANES), (2, SUBLANES), (3, SUBLANES)]
_CAND_COL_BLOCKS = [(0, PEER_TOPK, 4, 15), (1, SUBLANES, 4, 7), (2, SUBLANES, 4, 4)]
N_CAND = sum(n for _, n in _CAND_ROW_BLOCKS) + sum(n for _, n, _, _ in _CAND_COL_BLOCKS)


def _cand_consts(tm):
    flat, neg = [], []
    for a, nb in _CAND_ROW_BLOCKS:
        flat += [a * PEER_TOPK + b for b in range(nb)]
        neg += [0.0] * nb
    for b, na, lo, hi in _CAND_COL_BLOCKS:
        flat += [a * PEER_TOPK + b for a in range(na)]
        neg += [0.0 if lo <= a <= hi else -float('inf') for a in range(na)]
    flat = [f if n == 0.0 else 1000.0 + i for i, (f, n) in enumerate(zip(flat, neg))]
    col = lambda v: jnp.broadcast_to(jnp.asarray(v, F32)[:, None], (N_CAND, tm))
    return col(flat), col(neg)


def _cand_rows(row_vals, col_vals, combine):
    blocks = [combine(row_vals[a:a + 1, :], col_vals[0:nb, :]) for a, nb in _CAND_ROW_BLOCKS]
    blocks += [combine(row_vals[0:na, :], col_vals[b:b + 1, :]) for b, na, _, _ in _CAND_COL_BLOCKS]
    return jnp.concatenate(blocks, axis=0)


SEL_HEADS_PER_STEP = 2


def _peer_select_kernel(x_ref, g_ref, wq_ref, keys_ref, cflat_ref, cneg_ref, x8_ref, h8_ref, base_ref, shift_ref,
                        gate_ref, q3, idx_t, gate_t, *lists):
    tm = x_ref.shape[0]
    n = SEL_HEADS_PER_STEP
    s1, i1, s2, i2, top, eid = (lists[j * n:(j + 1) * n] for j in range(6))
    x = x_ref[...]
    h = _rms(x, g_ref[...])
    for r in range(D_MODEL // LANES):
        rows = pl.ds(r, tm, stride=D_MODEL // LANES)
        h8_ref[rows, :] = h[:, LANES * r:LANES * (r + 1)]
        x8_ref[rows, :] = x[:, LANES * r:LANES * (r + 1)]
    q = _bdot(h, wq_ref[...])
    for j in range(2 * PEER_HEADS):
        q3[j] = q[:, LANES * j:LANES * (j + 1)]
    iota_k = lax.broadcasted_iota(jnp.int32, (PEER_NKEYS, tm), 0).astype(F32)

    def heads(step, _):
        for u in range(SEL_HEADS_PER_STEP):
            hd = step * SEL_HEADS_PER_STEP + u
            sc1 = _bdot_nt(keys_ref[2 * hd], q3[2 * hd])
            sc2 = _bdot_nt(keys_ref[2 * hd + 1], q3[2 * hd + 1])
            _top_rows(sc1, iota_k, None, s1[u], i1[u])
            _top_rows(sc2, iota_k, None, s2[u], i2[u])
            cand = _cand_rows(s1[u][...], s2[u][...], lambda x, y: x + y) + cneg_ref[...]
            cidx = _cand_rows(i1[u][...], i2[u][...], lambda x, y: x * float(PEER_NKEYS) + y)
            _top_rows(cand, cflat_ref[...], cidx, top[u], eid[u])
            tv = top[u][...]
            e = jnp.exp(tv - jnp.max(tv, axis=0, keepdims=True))
            rs = pl.ds(pl.multiple_of(hd * PEER_TOPK, PEER_TOPK), PEER_TOPK)
            idx_t[rs, :] = eid[u][...]
            gate_t[rs, :] = e / jnp.sum(e, axis=0, keepdims=True)
        return 0

    lax.fori_loop(0, PEER_HEADS // SEL_HEADS_PER_STEP, heads, 0)
    e_t = idx_t[...].T
    pair = jnp.floor(e_t * 0.5)
    base_ref[...] = (pair * float(SUBLANES)).astype(jnp.int32)
    shift_ref[...] = ((e_t - 2.0 * pair) * 16.0).astype(jnp.int32)
    gate_ref[...] = gate_t[...].T


def _peer_select(x2, g, wq, keys):
    t = x2.shape[0]
    tm = SEL_BLOCK
    keys_b = keys.reshape(2 * PEER_HEADS, PEER_NKEYS, LANES).astype(BF16)
    wq_b = wq.astype(BF16)
    cflat, cneg = _cand_consts(tm)
    tokspec = lambda w: pl.BlockSpec((tm, w), lambda i: (i, 0))
    vm = lambda r: pltpu.VMEM((r, tm), F32)
    return pl.pallas_call(
        _peer_select_kernel, grid=(t // tm,),
        in_specs=[tokspec(D_MODEL), _full((1, D_MODEL)), _full(wq_b.shape), _full(keys_b.shape),
                  _full(cflat.shape), _full(cneg.shape)],
        out_specs=[pl.BlockSpec((tm * SUBLANES, LANES), lambda i: (i, 0))] * 2
                  + [tokspec(PEER_SEL), tokspec(PEER_SEL), tokspec(PEER_SEL)],
        out_shape=[jax.ShapeDtypeStruct((t * SUBLANES, LANES), F32)] * 2 + [
                   jax.ShapeDtypeStruct((t, PEER_SEL), jnp.int32),
                   jax.ShapeDtypeStruct((t, PEER_SEL), jnp.int32), jax.ShapeDtypeStruct((t, PEER_SEL), F32)],
        scratch_shapes=[pltpu.VMEM((2 * PEER_HEADS, tm, LANES), F32), vm(PEER_SEL), vm(PEER_SEL)]
                       + [vm(PEER_TOPK) for _ in range(6 * SEL_HEADS_PER_STEP)],
        compiler_params=_params(("parallel",)),
    )(x2, g.reshape(1, -1).astype(F32), wq_b, keys_b, cflat, cneg)


def _tile_table(tab):
    return tab.reshape(tab.shape[0], D_MODEL // LANES, LANES)


def _pack_table(tab3):
    n = tab3.shape[0]
    bits = lax.bitcast_convert_type(tab3.astype(BF16), jnp.uint16).astype(U32)
    bits = bits.reshape(n // 2, 2, SUBLANES, LANES)
    packed = (bits[:, 0] << 16) | bits[:, 1]
    return lax.bitcast_convert_type(packed, jnp.int32).reshape(n // 2 * SUBLANES, LANES)


def _splat_into(src_ref, t, dst_ref, slot):
    tile = jnp.broadcast_to(src_ref[pl.ds(t, 1), :], (LANES, LANES)).T
    dst_ref[LANES * slot:LANES * (slot + 1), :] = tile


def _bcast_row(ref, row):
    return jnp.broadcast_to(ref[row:row + 1, :], (SUBLANES, LANES))


def _expert_tile(tab_ref, base, shift_splat, row):
    w = tab_ref[pl.ds(pl.multiple_of(base, SUBLANES), SUBLANES), :]
    return lax.bitcast_convert_type((w << _bcast_row(shift_splat, row)) & jnp.int32(-65536), F32)


def _token_rows(t):
    return pl.ds(pl.multiple_of(t * SUBLANES, SUBLANES), SUBLANES)


def _pipelined_tokens(tg, prep, work):
    prep(0, 0)

    def body(i, _):
        t0 = 2 * i
        prep(t0 + 1, 1)
        work(t0, 0)
        prep(jnp.minimum(t0 + 2, tg - 1), 0)
        work(t0 + 1, 1)
        return 0

    lax.fori_loop(0, tg // 2, body, 0)


def _table_spec(shape):
    return pl.BlockSpec(shape, lambda i: (0, 0), pipeline_mode=pl.Buffered(1))


N_CHUNK = D_MODEL // LANES
CHUNK_STRIDE = PEER_SEL + SUBLANES


def _peer_u_kernel(base_ref, shift_ref, h_ref, gate_ref, tab_ref, eye_ref, c_ref, shift_splat, *planes):
    tg = gate_ref.shape[0]
    ones = jnp.ones((LANES, LANES), BF16)

    half = N_CHUNK // 2

    def prep(t, slot):
        _splat_into(shift_ref, t, shift_splat, slot)

    def gather(t, slot):
        ht = h_ref[_token_rows(t), :]
        for kx in range(PEER_SEL):
            prod = _expert_tile(tab_ref, base_ref[t, kx], shift_splat, LANES * slot + kx) * ht
            fold = prod + pltpu.roll(prod, half, 0)
            planes[slot][pl.ds(kx, half, stride=CHUNK_STRIDE), :] = fold[0:half, :]

    def finish(t, slot):
        plane = planes[slot]
        acc = plane[0:PEER_SEL, :]
        for r in range(1, half):
            acc = acc + plane[CHUNK_STRIDE * r:CHUNK_STRIDE * r + PEER_SEL, :]
        hi = acc.astype(BF16)
        lo = (acc - hi.astype(F32)).astype(BF16)
        tot = jnp.dot(hi, ones, preferred_element_type=F32) + jnp.dot(lo, ones, preferred_element_type=F32)
        score = jnp.sum(tot * eye_ref[...], axis=0, keepdims=True)
        c_ref[pl.ds(t, 1), :] = gate_ref[pl.ds(t, 1), :] * _gelu(score)

    planes[1][...] = jnp.zeros_like(planes[1])
    prep(0, 0)

    def body(i, _):
        t0 = 2 * i
        prep(t0 + 1, 1)
        gather(t0, 0)
        finish(jnp.maximum(t0 - 1, 0), 1)
        prep(jnp.minimum(t0 + 2, tg - 1), 0)
        gather(t0 + 1, 1)
        finish(t0, 0)
        return 0

    lax.fori_loop(0, tg // 2, body, 0)
    finish(tg - 1, 1)


def _peer_u(base, shift, h8, gate, tab, t):
    tg = GATHER_BLOCK
    eye = jnp.eye(LANES, dtype=F32)
    tokrow = pl.BlockSpec((tg, PEER_SEL), lambda i: (i, 0))
    return pl.pallas_call(
        _peer_u_kernel, grid=(t // tg,),
        in_specs=[pl.BlockSpec((tg, PEER_SEL), lambda i: (i, 0), memory_space=pltpu.SMEM), tokrow,
                  pl.BlockSpec((tg * SUBLANES, LANES), lambda i: (i, 0)), tokrow,
                  _table_spec(tab.shape), _full(eye.shape)],
        out_specs=tokrow, out_shape=jax.ShapeDtypeStruct((t, PEER_SEL), F32),
        scratch_shapes=[pltpu.VMEM((2 * LANES, LANES), jnp.int32)]
                       + [pltpu.VMEM((N_CHUNK // 2 * CHUNK_STRIDE, LANES), F32)] * 2,
        compiler_params=_params(("parallel",)),
    )(base, shift, h8, gate, tab, eye)


def _peer_v_kernel(base_ref, shift_ref, c_ref, x_ref, tab_ref, _aliased_out, o_ref, shift_splat, c_splat, otile):
    tg = c_ref.shape[0]
    n_acc = 4

    def prep(t, slot):
        _splat_into(shift_ref, t, shift_splat, slot)
        _splat_into(c_ref, t, c_splat, slot)

    def work(t, slot):
        accs = [jnp.zeros((SUBLANES, LANES), F32) for _ in range(n_acc)]
        for kx in range(PEER_SEL):
            row = LANES * slot + kx
            accs[kx % n_acc] = accs[kx % n_acc] + (_bcast_row(c_splat, row)
                                                   * _expert_tile(tab_ref, base_ref[t, kx], shift_splat, row))
        rs = _token_rows(t)
        otile[rs, :] = x_ref[rs, :] + ((accs[0] + accs[1]) + (accs[2] + accs[3]))

    _pipelined_tokens(tg, prep, work)
    for r in range(N_CHUNK):
        o_ref[:, LANES * r:LANES * (r + 1)] = otile[pl.ds(r, tg, stride=N_CHUNK), :]


def _peer_v(base, shift, c, x8, x2, tab, t):
    tg = GATHER_BLOCK
    smem = pl.BlockSpec((tg, PEER_SEL), lambda i: (i, 0), memory_space=pltpu.SMEM)
    tokrow = pl.BlockSpec((tg, PEER_SEL), lambda i: (i, 0))
    tile = pl.BlockSpec((tg * SUBLANES, LANES), lambda i: (i, 0))
    return pl.pallas_call(
        _peer_v_kernel, grid=(t // tg,),
        in_specs=[smem, tokrow, tokrow, tile, _table_spec(tab.shape), pl.BlockSpec(memory_space=pl.ANY)],
        out_specs=pl.BlockSpec((tg, D_MODEL), lambda i: (i, 0)),
        out_shape=jax.ShapeDtypeStruct(x2.shape, F32),
        input_output_aliases={5: 0},
        scratch_shapes=[pltpu.VMEM((2 * LANES, LANES), jnp.int32), pltpu.VMEM((2 * LANES, LANES), F32),
                        pltpu.VMEM((tg * SUBLANES, LANES), F32)],
        compiler_params=_params(("parallel",)),
    )(base, shift, c, x8, tab, x2)


SC_WORKERS = 32
SC_LANES = 16
SC_ROWS = 32
SC_SHARE_PIPELINED = 42 / 64
SC_SHARE_LAST = 26 / 64
SEQ_GROUPS = 4
SC_GROUP = 8
SC_REGS = 32


def _peer_sc(eidx, gate, h8, x8, tab_u, tab_v, t_off):
    ts = eidx.shape[0]
    per_w = ts // SC_WORKERS
    n_chunk = PEER_SEL // SC_ROWS
    grp = SC_GROUP
    sel = grp * PEER_SEL
    n_pairs = grp * n_chunk // 2
    per_tile = LANES // SC_LANES
    n_blk = D_MODEL // (SC_REGS * SC_LANES)
    mesh = plsc.VectorSubcoreMesh(core_axis_name="c", subcore_axis_name="s")

    def piece(ref, lead, q):
        return ref[lead, q // per_tile, pl.ds(SC_LANES * (q % per_tile), SC_LANES)]

    def body(u_hbm, v_hbm, idx_hbm, gate_hbm, h_hbm, x_hbm, out_hbm,
             idx_v, c_v, h_v, acc_v, part_v, rows0, rows1, sem0, sem1):
        wid = lax.axis_index("s") * 2 + lax.axis_index("c")
        bufs, sems = (rows0, rows1), (sem0, sem1)
        lane = lax.iota(jnp.int32, SC_LANES)

        def stream(tab_hbm, compute):
            def gather(chunk, slot):
                start = pl.multiple_of(chunk * SC_ROWS, SC_ROWS)
                return pltpu.make_async_copy(tab_hbm.at[idx_v.at[pl.ds(start, SC_ROWS)]], bufs[slot], sems[slot])

            gather(0, 0).start()

            @pl.loop(0, n_pairs)
            def _(p):
                c0 = 2 * p
                gather(c0 + 1, 1).start()
                gather(c0, 0).wait()
                compute(c0, 0)
                gather(jnp.minimum(c0 + 2, 2 * n_pairs - 1), 0).start()
                gather(c0 + 1, 1).wait()
                compute(c0 + 1, 1)

            gather(0, 0).wait()

        def dots(chunk, slot):
            tok = chunk // n_chunk
            for blk in range(n_blk):
                where = [blk * SC_REGS + j for j in range(SC_REGS)]
                hs = [piece(h_v, tok, q) for q in where]

                def row(k, carry):
                    parts = [None] * 4
                    for i, q in enumerate(where):
                        term = piece(bufs[slot], k, q) * hs[i]
                        parts[i % 4] = term if parts[i % 4] is None else parts[i % 4] + term
                    tot = (parts[0] + parts[1]) + (parts[2] + parts[3])
                    at = pl.ds(pl.multiple_of((chunk * SC_ROWS + k) * SC_LANES, SC_LANES), SC_LANES)
                    if blk == 0:
                        part_v[at] = tot
                    else:
                        part_v[at] = part_v[at] + tot
                    return carry

                lax.fori_loop(0, SC_ROWS, row, 0)

        def weigh(chunk, slot):
            tok = chunk // n_chunk
            for blk in range(n_blk):
                where = [blk * SC_REGS + j for j in range(SC_REGS)]

                def row(k, accs):
                    ck = plsc.load_gather(c_v, [jnp.full((SC_LANES,), chunk * SC_ROWS, jnp.int32) + k])
                    return tuple(a + ck * piece(bufs[slot], k, q) for a, q in zip(accs, where))

                accs = lax.fori_loop(0, SC_ROWS, row, tuple(piece(acc_v, tok, q) for q in where))
                for a, q in zip(accs, where):
                    acc_v[tok, q // per_tile, pl.ds(SC_LANES * (q % per_tile), SC_LANES)] = a

        @pl.loop(0, per_w // grp)
        def _(g):
            t0 = wid * per_w + g * grp
            flat = pl.ds(pl.multiple_of(t0 * PEER_SEL, sel), sel)
            pltpu.sync_copy(idx_hbm.at[flat], idx_v)
            pltpu.sync_copy(gate_hbm.at[flat], c_v)
            pltpu.sync_copy(h_hbm.at[pl.ds(t_off + t0, grp)], h_v)
            pltpu.sync_copy(x_hbm.at[pl.ds(t_off + t0, grp)], acc_v)
            stream(u_hbm, dots)

            @pl.loop(0, sel // SC_LANES)
            def _(m):
                base = (m * SC_LANES + lane) * SC_LANES
                score = plsc.load_gather(part_v, [base])
                for l in range(1, SC_LANES):
                    score = score + plsc.load_gather(part_v, [base + l])
                z = math.sqrt(2.0 / math.pi) * (score + 0.044715 * (score * score * score))
                tanh_z = 1.0 - 2.0 / (jnp.exp(2.0 * z) + 1.0)
                at = pl.ds(pl.multiple_of(m * SC_LANES, SC_LANES), SC_LANES)
                c_v[at] = c_v[at] * (0.5 * score * (1.0 + tanh_z))

            stream(v_hbm, weigh)
            pltpu.sync_copy(acc_v, out_hbm.at[pl.ds(t0, grp)])

    tile = lambda n: pltpu.VMEM((n, N_CHUNK, LANES), F32)
    out = pl.kernel(
        body, mesh=mesh, out_type=jax.ShapeDtypeStruct((ts, N_CHUNK, LANES), F32),
        scratch_types=[pltpu.VMEM((sel,), jnp.int32), pltpu.VMEM((sel,), F32), tile(grp), tile(grp),
                       pltpu.VMEM((sel * SC_LANES,), F32), tile(SC_ROWS), tile(SC_ROWS),
                       pltpu.SemaphoreType.DMA, pltpu.SemaphoreType.DMA],
        compiler_params=pltpu.CompilerParams(needs_layout_passes=False, use_tc_tiling_on_sc=True),
    )(tab_u, tab_v, eidx.reshape(-1), gate.reshape(-1),
      h8.reshape(-1, N_CHUNK, LANES), x8.reshape(-1, N_CHUNK, LANES))
    return out


def _final_norm_kernel(x_ref, g_ref, o_ref):
    o_ref[...] = _rms(x_ref[...], g_ref[...])


def _final_norm(x2, g):
    t, d = x2.shape
    tm = min(PROJ_BLOCK, t)
    spec = pl.BlockSpec((tm, d), lambda i: (i, 0))
    return pl.pallas_call(
        _final_norm_kernel, grid=(t // tm,), in_specs=[spec, _full((1, d))], out_specs=spec,
        out_shape=jax.ShapeDtypeStruct((t, d), F32), compiler_params=_params(("parallel",)),
    )(x2, g.reshape(1, d).astype(F32))


def kernel(x, mem, norm_mix, w_in, rw_mu, rw_w0, rw_w2, rw_a0, rw_a2, rw_g2, rw_kk, rw_ka, rw_rk, rw_v0, rw_v1, rw_v2, rw_lnx_g, rw_lnx_b, s5_a_re, s5_a_im, s5_log_dt, s5_b_re, s5_b_im, s5_c_re, s5_c_im, s5_d, s5_glu_w, s5_glu_b, s5_out_g, w_out, norm_xa, norm_mem, xa_wq, xa_wk, xa_wv, xa_wo, norm_ffn, peer_wq, peer_keys, peer_u, peer_v, norm_final):
    bsz, seq, d = x.shape
    depth = w_in.shape[0]

    def layer(l, xg, memg, v_first, tabs, sc_share):
        nb = xg.shape[0]
        t = nb * seq
        x2 = xg.reshape(t, d)
        ws = [w_in[l][:, :RW_COLS].astype(BF16), w_in[l][:, RW_COLS:].astype(BF16)]
        if l > 0:
            ws.append(_pad_rows(rw_v1[l - 1].T, LANES, 0).T.astype(BF16))
        outs = _norm_proj(x2, norm_mix[l], ws, [F32] * len(ws), PROJ_BLOCK)
        z_rw = outs[0].reshape(nb, seq, RW_COLS)
        u_s5 = outs[1].reshape(nb, seq, D_S5)
        hv = outs[2].reshape(nb, seq, LANES) if l > 0 else None
        rw_prm = dict(mu=rw_mu[l], w0=rw_w0[l], w2=rw_w2[l], a0=rw_a0[l], a2=rw_a2[l], g2=rw_g2[l],
                      kk=rw_kk[l], ka=rw_ka[l], rk=rw_rk[l], lng=rw_lnx_g[l], lnb=rw_lnx_b[l])
        if l > 0:
            rw_prm.update(v0=rw_v0[l - 1], v2=rw_v2[l - 1])
        y_rw, v_first = _rwkv(z_rw, hv, v_first, rw_prm, nb, seq)
        s5_prm = dict(a_re=s5_a_re[l], a_im=s5_a_im[l], log_dt=s5_log_dt[l], b_re=s5_b_re[l], b_im=s5_b_im[l],
                      c_re=s5_c_re[l], c_im=s5_c_im[l], d=s5_d[l], glu_w=s5_glu_w[l], glu_b=s5_glu_b[l],
                      out_g=s5_out_g[l])
        y_s5 = _s5(u_s5, s5_prm, nb, seq)
        kv = _norm_proj(memg.reshape(nb * N_MEM, d), norm_mem[l],
                        [xa_wk[l].astype(BF16), xa_wv[l].astype(BF16)], [BF16, BF16], PROJ_BLOCK)
        kmem = kv[0].reshape(nb, N_MEM, d)
        vmem = kv[1].reshape(nb, N_MEM, d)
        xg = _mix_xattn(xg, y_rw, y_s5, w_out[l], norm_xa[l], xa_wq[l], kmem, vmem, xa_wo[l], nb, seq)
        x2 = xg.reshape(t, d)
        x8, h8, base, shift, gate = _peer_select(x2, norm_ffn[l], peer_wq[l], peer_keys[l])
        u3, v3, u_packed, v_packed = tabs
        n_sc = int(t * sc_share)
        assert n_sc % (SC_WORKERS * SC_GROUP) == 0 and (t - n_sc) % GATHER_BLOCK == 0
        t_tc = t - n_sc
        eidx_sc = (base[t_tc:] >> 2) + (shift[t_tc:] >> 4)
        x_sc = _peer_sc(eidx_sc, gate[t_tc:], h8, x8, u3, v3, t_tc)
        c = _peer_u(base, shift, h8, gate, u_packed, t_tc)
        x_new = _peer_v(base, shift, c, x8, x2, v_packed, t_tc)
        xg = lax.dynamic_update_slice(x_new, x_sc.reshape(n_sc, d), (t_tc, 0)).reshape(nb, seq, d)
        return xg, v_first

    nb = bsz // SEQ_GROUPS
    xs = [x[g * nb:(g + 1) * nb] for g in range(SEQ_GROUPS)]
    mems = [mem[g * nb:(g + 1) * nb] for g in range(SEQ_GROUPS)]
    v_firsts = [None] * SEQ_GROUPS
    for l in range(depth):
        u3, v3 = _tile_table(peer_u[l]), _tile_table(peer_v[l])
        tabs = (u3, v3, _pack_table(u3), _pack_table(v3))
        for g in range(SEQ_GROUPS):
            last = l == depth - 1 and g == SEQ_GROUPS - 1
            share = SC_SHARE_LAST if last else SC_SHARE_PIPELINED
            xs[g], v_firsts[g] = layer(l, xs[g], mems[g], v_firsts[g], tabs, share)
    outs = [_final_norm(xg.reshape(nb * seq, d), norm_final).reshape(nb, seq, d) for xg in xs]
    return jnp.concatenate(outs, axis=0)
```

```python
import functools
import math

import jax
import jax.numpy as jnp
from jax import lax
from jax.experimental import pallas as pl
from jax.experimental.pallas import tpu as pltpu
from jax.experimental.pallas import tpu_sc as plsc

F32 = jnp.float32
BF16 = jnp.bfloat16
U32 = jnp.uint32

LANES = 128
SUBLANES = 8
VMEM_LIMIT = 56 * 1024 * 1024

D_MODEL = 1024
D_RWKV = 512
RW_HEAD = 64
RW_COLS = 1792
D_S5 = 512
S5_GROUPS = 32
S5_CH = 16
S5_STATE = 64
S5_MODES = S5_GROUPS * S5_STATE
N_MEM = 256
XA_HEADS = 4
XA_HEAD = 256
PEER_HEADS = 8
PEER_NKEYS = 128
PEER_TOPK = 16
PEER_SEL = PEER_HEADS * PEER_TOPK
RMS_EPS = 1e-6
GN_EPS = 64e-5

RW_CHUNK = 128
S5_BLOCK = 256
PROJ_BLOCK = 512
XA_BLOCK = 256
SEL_BLOCK = 128
GATHER_BLOCK = 64


def _params(sem):
    return pltpu.CompilerParams(dimension_semantics=sem, vmem_limit_bytes=VMEM_LIMIT)


def _rms(x, g):
    ms = jnp.mean(x * x, axis=-1, keepdims=True)
    return x * lax.rsqrt(ms + RMS_EPS) * g


def _bdot(a, b):
    return jnp.dot(a.astype(BF16), b.astype(BF16), preferred_element_type=F32)


def _bdot_nt(a, b):
    return lax.dot_general(a.astype(BF16), b.astype(BF16), (((1,), (1,)), ((), ())),
                           preferred_element_type=F32)


def _sigmoid(x):
    return 1.0 / (1.0 + jnp.exp(-x))


def _softplus(x):
    return jnp.maximum(x, 0.0) + jnp.log(1.0 + jnp.exp(-jnp.abs(x)))


def _gelu(x):
    return 0.5 * x * (1.0 + jnp.tanh(math.sqrt(2.0 / math.pi) * (x + 0.044715 * (x * x * x))))


def _full(shape):
    n = len(shape)
    return pl.BlockSpec(shape, lambda *_: (0,) * n)


def _norm_proj_kernel(*refs, n_out):
    x_ref, g_ref = refs[0], refs[1]
    w_refs = refs[2:2 + n_out]
    o_refs = refs[2 + n_out:]
    h = _rms(x_ref[...], g_ref[...]).astype(BF16)
    for w_ref, o_ref in zip(w_refs, o_refs):
        o_ref[...] = jnp.dot(h, w_ref[...], preferred_element_type=F32).astype(o_ref.dtype)


def _norm_proj(x2, g, ws, out_dtypes, block):
    t, d = x2.shape
    block = min(block, t)
    in_specs = [pl.BlockSpec((block, d), lambda i: (i, 0)), _full((1, d))]
    in_specs += [_full(w.shape) for w in ws]
    out_specs = [pl.BlockSpec((block, w.shape[1]), lambda i: (i, 0)) for w in ws]
    out_shape = [jax.ShapeDtypeStruct((t, w.shape[1]), dt) for w, dt in zip(ws, out_dtypes)]
    return pl.pallas_call(
        functools.partial(_norm_proj_kernel, n_out=len(ws)),
        grid=(t // block,), in_specs=in_specs, out_specs=out_specs, out_shape=out_shape,
        compiler_params=_params(("parallel",)),
    )(x2, g.reshape(1, d), *ws)


def _seg_sum(x, mseg):
    hi = x.astype(BF16)
    lo = (x - hi.astype(F32)).astype(BF16)
    return (jnp.dot(hi, mseg, preferred_element_type=F32)
            + jnp.dot(lo, mseg, preferred_element_type=F32))


def _col_bcast(row):
    return jnp.broadcast_to(row, (LANES, LANES)).T


def _rwkv_kernel(*refs, first_layer):
    if first_layer:
        (z_ref, mu_ref, w0_ref, w2_ref, a0_ref, a2_ref, g2_ref, kk_ref, ka_ref, rk_ref,
         lng_ref, lnb_ref, mseg_ref, y_ref, vf_out_ref, zprev, hst) = refs
    else:
        (z_ref, hv_ref, vf_ref, v0_ref, v2_ref, mu_ref, w0_ref, w2_ref, a0_ref, a2_ref, g2_ref,
         kk_ref, ka_ref, rk_ref, lng_ref, lnb_ref, mseg_ref, y_ref, zprev, hst) = refs
    L = RW_CHUNK

    @pl.when(pl.program_id(1) == 0)
    def _():
        zprev[...] = jnp.zeros_like(zprev)
        hst[...] = jnp.zeros_like(hst)

    z = z_ref[0]
    row = lax.broadcasted_iota(jnp.int32, (L, 1), 0)
    zs = jnp.where(row == 0, zprev[...], pltpu.roll(z, 1, 0))
    zprev[...] = z[L - 1:L, :]
    z = z + (zs - z) * mu_ref[...]
    r = z[:, 0:512]
    k = z[:, 512:1024]
    v = z[:, 1024:1536]
    wa = z[:, 1536:1664]
    gd = z[:, 1664:1792]
    mseg = mseg_ref[...]

    wlin = w0_ref[...] + _bdot(jnp.tanh(wa), w2_ref[...])
    lw = -jnp.exp(-_softplus(-wlin) - 0.5)
    a = _sigmoid(a0_ref[...] + _bdot(wa, a2_ref[...]))
    g = _bdot(_sigmoid(gd), g2_ref[...])
    if first_layer:
        vf_out_ref[0] = v
    else:
        v = v + (vf_ref[0] - v) * _sigmoid(v0_ref[...] + _bdot(hv_ref[0], v2_ref[...]))
    kk = k * kk_ref[...]
    kk = kk / jnp.maximum(jnp.sqrt(_seg_sum(kk * kk, mseg)), 1e-12)
    k2 = k * (1.0 + (a - 1.0) * ka_ref[...])
    av = -kk
    bv = kk * a

    ti = lax.broadcasted_iota(jnp.int32, (L, L), 0)
    si = lax.broadcasted_iota(jnp.int32, (L, L), 1)
    tril = (ti >= si).astype(F32)
    cum = jnp.dot(tril, lw, preferred_element_type=F32, precision=lax.Precision.HIGHEST)
    mid = cum[L // 2 - 1:L // 2, :]
    cm = cum - mid
    ecum = jnp.exp(cm)
    einv = jnp.exp(-cm)
    rt = r * ecum
    kt = k2 * einv
    bt = bv * einv
    at = av * jnp.exp(cm - lw)
    p_end = jnp.exp(cum[L - 1:L, :])
    e_end = ecum[L - 1:L, :]
    e_mid = jnp.exp(mid)

    lane = lax.broadcasted_iota(jnp.int32, (1, LANES), 1)
    m0 = (lane < RW_HEAD).astype(F32)
    m1 = 1.0 - m0
    strict = ti > si
    incl = ti >= si
    bi = lax.broadcasted_iota(jnp.int32, (LANES, LANES), 0) < RW_HEAD
    bj = lax.broadcasted_iota(jnp.int32, (LANES, LANES), 1) < RW_HEAD
    bdmask = (bi == bj).astype(F32)
    zeros_ll = jnp.zeros((L, L), F32)

    ys = []
    for p in range(D_RWKV // LANES):
        sl = slice(LANES * p, LANES * (p + 1))
        A, B, K, R, V = at[:, sl], bt[:, sl], kt[:, sl], rt[:, sl], v[:, sl]
        btkt = jnp.concatenate([B.T, K.T], axis=1)
        h0 = hst[p]
        h0m = h0 * _col_bcast(e_mid[:, sl])
        sc = _bdot(jnp.concatenate([A * m0, A * m1, R * m0, R * m1], axis=0), btkt)
        aab = [jnp.where(strict, sc[e * L:(e + 1) * L, 0:L], 0.0) for e in range(2)]
        aak = [jnp.where(strict, sc[e * L:(e + 1) * L, L:2 * L], 0.0) for e in range(2)]
        arb = [jnp.where(incl, sc[(2 + e) * L:(3 + e) * L, 0:L], 0.0) for e in range(2)]
        ark = [jnp.where(incl, sc[(2 + e) * L:(3 + e) * L, L:2 * L], 0.0) for e in range(2)]
        arh = _bdot(jnp.concatenate([A, R], axis=0), h0m)
        v01 = jnp.concatenate([V * m0, V * m1], axis=0)
        x = arh[0:L] + _bdot(jnp.concatenate(aak, axis=1), v01)
        pm = jnp.concatenate(aab, axis=1)
        n_fac = int(math.log2(L))
        for it in range(n_fac):
            x = x + _bdot(pm, jnp.concatenate([x * m0, x * m1], axis=0))
            if it + 1 < n_fac:
                pd = jnp.concatenate(
                    [jnp.concatenate([pm[:, 0:L], zeros_ll], axis=1),
                     jnp.concatenate([zeros_ll, pm[:, L:2 * L]], axis=1)], axis=0)
                pm = _bdot(pm, pd)
        u = x
        yp = arh[L:2 * L] + _bdot(jnp.concatenate(arb + ark, axis=1),
                                  jnp.concatenate([u * m0, u * m1, v01], axis=0))
        upd = _bdot(btkt, jnp.concatenate([u, V], axis=0))
        hst[p] = (h0 * _col_bcast(p_end[:, sl]) + upd * _col_bcast(e_end[:, sl])) * bdmask
        ys.append(yp)
    y = jnp.concatenate(ys, axis=1)

    mean = _seg_sum(y, mseg) * (1.0 / RW_HEAD)
    d = y - mean
    var = _seg_sum(d * d, mseg) * (1.0 / RW_HEAD)
    yn = d * lax.rsqrt(var + GN_EPS) * lng_ref[...] + lnb_ref[...]
    bonus = _seg_sum(r * k2 * rk_ref[...], mseg) * v
    y_ref[0] = (yn + bonus) * g


def _pad_rows(w, rows, offset):
    out = jnp.zeros((rows, w.shape[1]), w.dtype)
    return out.at[offset:offset + w.shape[0]].set(w)


def _rwkv(z_rw, hv, v_first, prm, bsz, seq):
    L = RW_CHUNK
    first = v_first is None
    row = lambda a: a.reshape(1, -1).astype(F32)
    hid = jnp.arange(D_RWKV) // RW_HEAD
    mseg = (hid[:, None] == hid[None, :]).astype(BF16)
    w2p = _pad_rows(prm['w2'], LANES, 0).astype(BF16)
    a2p = _pad_rows(prm['a2'], LANES, 64).astype(BF16)
    common = [row(prm['mu']), row(prm['w0']), w2p, row(prm['a0']), a2p, prm['g2'].astype(BF16),
              row(prm['kk']), row(prm['ka']), row(prm['rk']), row(prm['lng']), row(prm['lnb']), mseg]
    tok = lambda w: pl.BlockSpec((1, L, w), lambda b, t: (b, t, 0))
    common_specs = [_full(c.shape) for c in common]
    y_shape = jax.ShapeDtypeStruct((bsz, seq, D_RWKV), F32)
    scratch = [pltpu.VMEM((1, RW_COLS), F32), pltpu.VMEM((D_RWKV // LANES, LANES, LANES), F32)]
    if first:
        args = [z_rw] + common
        in_specs = [tok(RW_COLS)] + common_specs
        out_shape = [y_shape, y_shape]
        out_specs = [tok(D_RWKV), tok(D_RWKV)]
    else:
        v2p = _pad_rows(prm['v2'], LANES, 0).astype(BF16)
        extra = [row(prm['v0']), v2p]
        args = [z_rw, hv, v_first] + extra + common
        in_specs = [tok(RW_COLS), tok(LANES), tok(D_RWKV)] + [_full(c.shape) for c in extra] + common_specs
        out_shape = [y_shape]
        out_specs = [tok(D_RWKV)]
    outs = pl.pallas_call(
        functools.partial(_rwkv_kernel, first_layer=first),
        grid=(bsz, seq // L), in_specs=in_specs, out_specs=out_specs, out_shape=out_shape,
        scratch_shapes=scratch, compiler_params=_params(("parallel", "arbitrary")),
    )(*args)
    return (outs[0], outs[1]) if first else (outs[0], v_first)


def _s5_kernel(u_ref, wb_ref, wc_ref, lpr_ref, lpi_ref, d_ref, gw_ref, gb_ref, og_ref, o_ref,
               car_re, car_im, xre, xim):
    tb = u_ref.shape[1]

    @pl.when(pl.program_id(1) == 0)
    def _():
        car_re[...] = jnp.zeros_like(car_re)
        car_im[...] = jnp.zeros_like(car_im)

    u = u_ref[0]
    bu = _bdot(u, wb_ref[...])
    xre[...] = bu[:, 0:S5_MODES]
    xim[...] = bu[:, S5_MODES:2 * S5_MODES]
    row = lax.broadcasted_iota(jnp.int32, (SUBLANES, 1), 0)

    def tile(i, carry):
        cr, ci = carry
        rs = pl.ds(pl.multiple_of(i * SUBLANES, SUBLANES), SUBLANES)
        br, bi = xre[rs, :], xim[rs, :]
        for dist in (1, 2, 4):
            keep = row >= dist
            sr = jnp.where(keep, pltpu.roll(br, dist, 0), 0.0)
            si = jnp.where(keep, pltpu.roll(bi, dist, 0), 0.0)
            lr = lpr_ref[dist - 1:dist, :]
            li = lpi_ref[dist - 1:dist, :]
            br, bi = br + lr * sr - li * si, bi + lr * si + li * sr
        pr, pi = lpr_ref[...], lpi_ref[...]
        xr = br + pr * cr - pi * ci
        xi = bi + pr * ci + pi * cr
        xre[rs, :] = xr
        xim[rs, :] = xi
        return xr[SUBLANES - 1:SUBLANES, :], xi[SUBLANES - 1:SUBLANES, :]

    cr, ci = lax.fori_loop(0, tb // SUBLANES, tile, (car_re[...], car_im[...]))
    car_re[...] = cr
    car_im[...] = ci
    wc = wc_ref[...]
    y = _bdot(xre[...], wc[0:S5_MODES]) + _bdot(xim[...], wc[S5_MODES:2 * S5_MODES])
    y = _gelu(y + d_ref[...] * u)
    y = y * _sigmoid(_bdot(y, gw_ref[...]) + gb_ref[...])
    o_ref[0] = _rms(y, og_ref[...])


def _s5_weights(a_re, a_im, log_dt, b_re, b_im, c_re, c_im):
    lam_re = jnp.minimum(a_re.astype(F32), -1e-4)
    lam_im = a_im.astype(F32)
    dt = jnp.exp(log_dt.astype(F32))[:, None]
    mag = jnp.exp(lam_re * dt)
    lb_re = mag * jnp.cos(lam_im * dt)
    lb_im = mag * jnp.sin(lam_im * dt)
    den = lam_re * lam_re + lam_im * lam_im
    c1_re = ((lb_re - 1.0) * lam_re + lb_im * lam_im) / den
    c1_im = (lb_im * lam_re - (lb_re - 1.0) * lam_im) / den
    br, bi = b_re.astype(F32), b_im.astype(F32)
    bb_re = c1_re[..., None] * br - c1_im[..., None] * bi
    bb_im = c1_re[..., None] * bi + c1_im[..., None] * br
    eye = jnp.eye(S5_GROUPS, dtype=F32)
    wb_re = jnp.einsum('gpc,gh->gchp', bb_re, eye).reshape(D_S5, S5_MODES)
    wb_im = jnp.einsum('gpc,gh->gchp', bb_im, eye).reshape(D_S5, S5_MODES)
    wb = jnp.concatenate([wb_re, wb_im], axis=1).astype(BF16)
    wc_re = jnp.einsum('gcp,gh->gphc', c_re.astype(F32), eye).reshape(S5_MODES, D_S5)
    wc_im = jnp.einsum('gcp,gh->gphc', c_im.astype(F32), eye).reshape(S5_MODES, D_S5)
    wc = jnp.concatenate([wc_re, -wc_im], axis=0).astype(BF16)
    pr, pi = [lb_re], [lb_im]
    for _ in range(SUBLANES - 1):
        pr, pi = pr + [pr[-1] * lb_re - pi[-1] * lb_im], pi + [pr[-1] * lb_im + pi[-1] * lb_re]
    lp_re = jnp.stack(pr).reshape(SUBLANES, S5_MODES)
    lp_im = jnp.stack(pi).reshape(SUBLANES, S5_MODES)
    return wb, wc, lp_re, lp_im


def _s5(u, prm, bsz, seq):
    tb = min(S5_BLOCK, seq)
    wb, wc, lp_re, lp_im = _s5_weights(prm['a_re'], prm['a_im'], prm['log_dt'], prm['b_re'], prm['b_im'],
                                       prm['c_re'], prm['c_im'])
    row = lambda a: a.reshape(1, -1).astype(F32)
    consts = [wb, wc, lp_re, lp_im, row(prm['d']), prm['glu_w'].astype(BF16), row(prm['glu_b']),
              row(prm['out_g'])]
    tok = pl.BlockSpec((1, tb, D_S5), lambda b, t: (b, t, 0))
    return pl.pallas_call(
        _s5_kernel, grid=(bsz, seq // tb),
        in_specs=[tok] + [_full(c.shape) for c in consts], out_specs=tok,
        out_shape=jax.ShapeDtypeStruct((bsz, seq, D_S5), F32),
        scratch_shapes=[pltpu.VMEM((1, S5_MODES), F32), pltpu.VMEM((1, S5_MODES), F32),
                        pltpu.VMEM((tb, S5_MODES), F32), pltpu.VMEM((tb, S5_MODES), F32)],
        compiler_params=_params(("parallel", "arbitrary")),
    )(u, *consts)


def _mix_xattn_kernel(x_ref, yr_ref, ys_ref, wo1_ref, wo2_ref, g_ref, wq_ref, k_ref, v_ref, wo_ref, o_ref):
    x1 = x_ref[0] + _bdot(yr_ref[0], wo1_ref[...]) + _bdot(ys_ref[0], wo2_ref[...])
    h = _rms(x1, g_ref[...])
    q = _bdot(h, wq_ref[...])
    km, vm = k_ref[0], v_ref[0]
    outs = []
    for hd in range(XA_HEADS):
        sl = slice(XA_HEAD * hd, XA_HEAD * (hd + 1))
        s = _bdot_nt(q[:, sl], km[:, sl]) * (XA_HEAD ** -0.5)
        s = s - jnp.max(s, axis=-1, keepdims=True)
        e = jnp.exp(s)
        p = e / jnp.sum(e, axis=-1, keepdims=True)
        outs.append(_bdot(p, vm[:, sl]))
    o = jnp.concatenate(outs, axis=1)
    o_ref[0] = x1 + _bdot(o, wo_ref[...])


def _mix_xattn(x, y_rw, y_s5, w_out, g, wq, kmem, vmem, wo, bsz, seq):
    tm = min(XA_BLOCK, seq)
    consts_a = [w_out[:D_RWKV].astype(BF16), w_out[D_RWKV:].astype(BF16), g.reshape(1, -1).astype(F32),
                wq.astype(BF16)]
    tok = lambda w: pl.BlockSpec((1, tm, w), lambda b, t: (b, t, 0))
    mem = pl.BlockSpec((1, N_MEM, D_MODEL), lambda b, t: (b, 0, 0))
    wo_b = wo.astype(BF16)
    return pl.pallas_call(
        _mix_xattn_kernel, grid=(bsz, seq // tm),
        in_specs=[tok(D_MODEL), tok(D_RWKV), tok(D_S5)] + [_full(c.shape) for c in consts_a]
                 + [mem, mem, _full(wo_b.shape)],
        out_specs=tok(D_MODEL), out_shape=jax.ShapeDtypeStruct((bsz, seq, D_MODEL), F32),
        compiler_params=_params(("parallel", "parallel")),
    )(x, y_rw, y_s5, *consts_a, kmem, vmem, wo_b)


def _top_rows(work, order, aux, val_ref, idx_ref):
    for it in range(PEER_TOPK):
        m = jnp.max(work, axis=0, keepdims=True)
        pos = jnp.min(jnp.where(work == m, order, jnp.inf), axis=0, keepdims=True)
        hit = order == pos
        val_ref[it:it + 1, :] = m
        if aux is None:
            idx_ref[it:it + 1, :] = pos
        else:
            idx_ref[it:it + 1, :] = jnp.sum(jnp.where(hit, aux, 0.0), axis=0, keepdims=True)
        work = jnp.where(hit, -jnp.inf, work)


_CAND_ROW_BLOCKS = [(0, PEER_TOPK), (1, SUBLANES), (2, SUBLANES), (3, SUBLANES)]
_CAND_COL_BLOCKS = [(0, PEER_TOPK, 4, 15), (1, SUBLANES, 4, 7), (2, SUBLANES, 4, 4)]
N_CAND = sum(n for _, n in _CAND_ROW_BLOCKS) + sum(n for _, n, _, _ in _CAND_COL_BLOCKS)


def _cand_consts(tm):
    flat, neg = [], []
    for a, nb in _CAND_ROW_BLOCKS:
        flat += [a * PEER_TOPK + b for b in range(nb)]
        neg += [0.0] * nb
    for b, na, lo, hi in _CAND_COL_BLOCKS:
        flat += [a * PEER_TOPK + b for a in range(na)]
        neg += [0.0 if lo <= a <= hi else -float('inf') for a in range(na)]
    flat = [f if n == 0.0 else 1000.0 + i for i, (f, n) in enumerate(zip(flat, neg))]
    col = lambda v: jnp.broadcast_to(jnp.asarray(v, F32)[:, None], (N_CAND, tm))
    return col(flat), col(neg)


def _cand_rows(row_vals, col_vals, combine):
    blocks = [combine(row_vals[a:a + 1, :], col_vals[0:nb, :]) for a, nb in _CAND_ROW_BLOCKS]
    blocks += [combine(row_vals[0:na, :], col_vals[b:b + 1, :]) for b, na, _, _ in _CAND_COL_BLOCKS]
    return jnp.concatenate(blocks, axis=0)


SEL_HEADS_PER_STEP = 2


def _peer_select_kernel(x_ref, g_ref, wq_ref, keys_ref, cflat_ref, cneg_ref, x8_ref, h8_ref, base_ref, shift_ref,
                        gate_ref, q3, idx_t, gate_t, *lists):
    tm = x_ref.shape[0]
    n = SEL_HEADS_PER_STEP
    s1, i1, s2, i2, top, eid = (lists[j * n:(j + 1) * n] for j in range(6))
    x = x_ref[...]
    h = _rms(x, g_ref[...])
    for r in range(D_MODEL // LANES):
        rows = pl.ds(r, tm, stride=D_MODEL // LANES)
        h8_ref[rows, :] = h[:, LANES * r:LANES * (r + 1)]
        x8_ref[rows, :] = x[:, LANES * r:LANES * (r + 1)]
    q = _bdot(h, wq_ref[...])
    for j in range(2 * PEER_HEADS):
        q3[j] = q[:, LANES * j:LANES * (j + 1)]
    iota_k = lax.broadcasted_iota(jnp.int32, (PEER_NKEYS, tm), 0).astype(F32)

    def heads(step, _):
        for u in range(SEL_HEADS_PER_STEP):
            hd = step * SEL_HEADS_PER_STEP + u
            sc1 = _bdot_nt(keys_ref[2 * hd], q3[2 * hd])
            sc2 = _bdot_nt(keys_ref[2 * hd + 1], q3[2 * hd + 1])
            _top_rows(sc1, iota_k, None, s1[u], i1[u])
            _top_rows(sc2, iota_k, None, s2[u], i2[u])
            cand = _cand_rows(s1[u][...], s2[u][...], lambda x, y: x + y) + cneg_ref[...]
            cidx = _cand_rows(i1[u][...], i2[u][...], lambda x, y: x * float(PEER_NKEYS) + y)
            _top_rows(cand, cflat_ref[...], cidx, top[u], eid[u])
            tv = top[u][...]
            e = jnp.exp(tv - jnp.max(tv, axis=0, keepdims=True))
            rs = pl.ds(pl.multiple_of(hd * PEER_TOPK, PEER_TOPK), PEER_TOPK)
            idx_t[rs, :] = eid[u][...]
            gate_t[rs, :] = e / jnp.sum(e, axis=0, keepdims=True)
        return 0

    lax.fori_loop(0, PEER_HEADS // SEL_HEADS_PER_STEP, heads, 0)
    e_t = idx_t[...].T
    pair = jnp.floor(e_t * 0.5)
    base_ref[...] = (pair * float(SUBLANES)).astype(jnp.int32)
    shift_ref[...] = ((e_t - 2.0 * pair) * 16.0).astype(jnp.int32)
    gate_ref[...] = gate_t[...].T


def _peer_select(x2, g, wq, keys):
    t = x2.shape[0]
    tm = SEL_BLOCK
    keys_b = keys.reshape(2 * PEER_HEADS, PEER_NKEYS, LANES).astype(BF16)
    wq_b = wq.astype(BF16)
    cflat, cneg = _cand_consts(tm)
    tokspec = lambda w: pl.BlockSpec((tm, w), lambda i: (i, 0))
    vm = lambda r: pltpu.VMEM((r, tm), F32)
    return pl.pallas_call(
        _peer_select_kernel, grid=(t // tm,),
        in_specs=[tokspec(D_MODEL), _full((1, D_MODEL)), _full(wq_b.shape), _full(keys_b.shape),
                  _full(cflat.shape), _full(cneg.shape)],
        out_specs=[pl.BlockSpec((tm * SUBLANES, LANES), lambda i: (i, 0))] * 2
                  + [tokspec(PEER_SEL), tokspec(PEER_SEL), tokspec(PEER_SEL)],
        out_shape=[jax.ShapeDtypeStruct((t * SUBLANES, LANES), F32)] * 2 + [
                   jax.ShapeDtypeStruct((t, PEER_SEL), jnp.int32),
                   jax.ShapeDtypeStruct((t, PEER_SEL), jnp.int32), jax.ShapeDtypeStruct((t, PEER_SEL), F32)],
        scratch_shapes=[pltpu.VMEM((2 * PEER_HEADS, tm, LANES), F32), vm(PEER_SEL), vm(PEER_SEL)]
                       + [vm(PEER_TOPK) for _ in range(6 * SEL_HEADS_PER_STEP)],
        compiler_params=_params(("parallel",)),
    )(x2, g.reshape(1, -1).astype(F32), wq_b, keys_b, cflat, cneg)


def _tile_table(tab):
    return tab.reshape(tab.shape[0], D_MODEL // LANES, LANES)


def _pack_table(tab3):
    n = tab3.shape[0]
    bits = lax.bitcast_convert_type(tab3.astype(BF16), jnp.uint16).astype(U32)
    bits = bits.reshape(n // 2, 2, SUBLANES, LANES)
    packed = (bits[:, 0] << 16) | bits[:, 1]
    return lax.bitcast_convert_type(packed, jnp.int32).reshape(n // 2 * SUBLANES, LANES)


def _splat_into(src_ref, t, dst_ref, slot):
    tile = jnp.broadcast_to(src_ref[pl.ds(t, 1), :], (LANES, LANES)).T
    dst_ref[LANES * slot:LANES * (slot + 1), :] = tile


def _bcast_row(ref, row):
    return jnp.broadcast_to(ref[row:row + 1, :], (SUBLANES, LANES))


def _expert_tile(tab_ref, base, shift_splat, row):
    w = tab_ref[pl.ds(pl.multiple_of(base, SUBLANES), SUBLANES), :]
    return lax.bitcast_convert_type((w << _bcast_row(shift_splat, row)) & jnp.int32(-65536), F32)


def _token_rows(t):
    return pl.ds(pl.multiple_of(t * SUBLANES, SUBLANES), SUBLANES)


def _pipelined_tokens(tg, prep, work):
    prep(0, 0)

    def body(i, _):
        t0 = 2 * i
        prep(t0 + 1, 1)
        work(t0, 0)
        prep(jnp.minimum(t0 + 2, tg - 1), 0)
        work(t0 + 1, 1)
        return 0

    lax.fori_loop(0, tg // 2, body, 0)


def _table_spec(shape):
    return pl.BlockSpec(shape, lambda i: (0, 0), pipeline_mode=pl.Buffered(1))


N_CHUNK = D_MODEL // LANES
CHUNK_STRIDE = PEER_SEL + SUBLANES


def _peer_u_kernel(base_ref, shift_ref, h_ref, gate_ref, tab_ref, eye_ref, c_ref, shift_splat, *planes):
    tg = gate_ref.shape[0]
    ones = jnp.ones((LANES, LANES), BF16)

    half = N_CHUNK // 2

    def prep(t, slot):
        _splat_into(shift_ref, t, shift_splat, slot)

    def gather(t, slot):
        ht = h_ref[_token_rows(t), :]
        for kx in range(PEER_SEL):
            prod = _expert_tile(tab_ref, base_ref[t, kx], shift_splat, LANES * slot + kx) * ht
            fold = prod + pltpu.roll(prod, half, 0)
            planes[slot][pl.ds(kx, half, stride=CHUNK_STRIDE), :] = fold[0:half, :]

    def finish(t, slot):
        plane = planes[slot]
        acc = plane[0:PEER_SEL, :]
        for r in range(1, half):
            acc = acc + plane[CHUNK_STRIDE * r:CHUNK_STRIDE * r + PEER_SEL, :]
        hi = acc.astype(BF16)
        lo = (acc - hi.astype(F32)).astype(BF16)
        tot = jnp.dot(hi, ones, preferred_element_type=F32) + jnp.dot(lo, ones, preferred_element_type=F32)
        score = jnp.sum(tot * eye_ref[...], axis=0, keepdims=True)
        c_ref[pl.ds(t, 1), :] = gate_ref[pl.ds(t, 1), :] * _gelu(score)

    planes[1][...] = jnp.zeros_like(planes[1])
    prep(0, 0)

    def body(i, _):
        t0 = 2 * i
        prep(t0 + 1, 1)
        gather(t0, 0)
        finish(jnp.maximum(t0 - 1, 0), 1)
        prep(jnp.minimum(t0 + 2, tg - 1), 0)
        gather(t0 + 1, 1)
        finish(t0, 0)
        return 0

    lax.fori_loop(0, tg // 2, body, 0)
    finish(tg - 1, 1)


def _peer_u(base, shift, h8, gate, tab, t):
    tg = GATHER_BLOCK
    eye = jnp.eye(LANES, dtype=F32)
    tokrow = pl.BlockSpec((tg, PEER_SEL), lambda i: (i, 0))
    return pl.pallas_call(
        _peer_u_kernel, grid=(t // tg,),
        in_specs=[pl.BlockSpec((tg, PEER_SEL), lambda i: (i, 0), memory_space=pltpu.SMEM), tokrow,
                  pl.BlockSpec((tg * SUBLANES, LANES), lambda i: (i, 0)), tokrow,
                  _table_spec(tab.shape), _full(eye.shape)],
        out_specs=tokrow, out_shape=jax.ShapeDtypeStruct((t, PEER_SEL), F32),
        scratch_shapes=[pltpu.VMEM((2 * LANES, LANES), jnp.int32)]
                       + [pltpu.VMEM((N_CHUNK // 2 * CHUNK_STRIDE, LANES), F32)] * 2,
        compiler_params=_params(("parallel",)),
    )(base, shift, h8, gate, tab, eye)


def _peer_v_kernel(base_ref, shift_ref, c_ref, x_ref, tab_ref, _aliased_out, o_ref, shift_splat, c_splat, otile):
    tg = c_ref.shape[0]
    n_acc = 4

    def prep(t, slot):
        _splat_into(shift_ref, t, shift_splat, slot)
        _splat_into(c_ref, t, c_splat, slot)

    def work(t, slot):
        accs = [jnp.zeros((SUBLANES, LANES), F32) for _ in range(n_acc)]
        for kx in range(PEER_SEL):
            row = LANES * slot + kx
            accs[kx % n_acc] = accs[kx % n_acc] + (_bcast_row(c_splat, row)
                                                   * _expert_tile(tab_ref, base_ref[t, kx], shift_splat, row))
        rs = _token_rows(t)
        otile[rs, :] = x_ref[rs, :] + ((accs[0] + accs[1]) + (accs[2] + accs[3]))

    _pipelined_tokens(tg, prep, work)
    for r in range(N_CHUNK):
        o_ref[:, LANES * r:LANES * (r + 1)] = otile[pl.ds(r, tg, stride=N_CHUNK), :]


def _peer_v(base, shift, c, x8, x2, tab, t):
    tg = GATHER_BLOCK
    smem = pl.BlockSpec((tg, PEER_SEL), lambda i: (i, 0), memory_space=pltpu.SMEM)
    tokrow = pl.BlockSpec((tg, PEER_SEL), lambda i: (i, 0))
    tile = pl.BlockSpec((tg * SUBLANES, LANES), lambda i: (i, 0))
    return pl.pallas_call(
        _peer_v_kernel, grid=(t // tg,),
        in_specs=[smem, tokrow, tokrow, tile, _table_spec(tab.shape), pl.BlockSpec(memory_space=pl.ANY)],
        out_specs=pl.BlockSpec((tg, D_MODEL), lambda i: (i, 0)),
        out_shape=jax.ShapeDtypeStruct(x2.shape, F32),
        input_output_aliases={5: 0},
        scratch_shapes=[pltpu.VMEM((2 * LANES, LANES), jnp.int32), pltpu.VMEM((2 * LANES, LANES), F32),
                        pltpu.VMEM((tg * SUBLANES, LANES), F32)],
        compiler_params=_params(("parallel",)),
    )(base, shift, c, x8, tab, x2)


SC_WORKERS = 32
SC_LANES = 16
SC_ROWS = 32
SC_SHARE_PIPELINED = 42 / 64
SC_SHARE_LAST = 26 / 64
SEQ_GROUPS = 4
SC_GROUP = 8
SC_REGS = 32


def _peer_sc(eidx, gate, h8, x8, tab_u, tab_v, t_off):
    ts = eidx.shape[0]
    per_w = ts // SC_WORKERS
    n_chunk = PEER_SEL // SC_ROWS
    grp = SC_GROUP
    sel = grp * PEER_SEL
    n_pairs = grp * n_chunk // 2
    per_tile = LANES // SC_LANES
    n_blk = D_MODEL // (SC_REGS * SC_LANES)
    mesh = plsc.VectorSubcoreMesh(core_axis_name="c", subcore_axis_name="s")

    def piece(ref, lead, q):
        return ref[lead, q // per_tile, pl.ds(SC_LANES * (q % per_tile), SC_LANES)]

    def tok_piece(ref, tok, q):
        return ref[tok * N_CHUNK + q // per_tile, pl.ds(SC_LANES * (q % per_tile), SC_LANES)]

    def body(u_hbm, v_hbm, idx_hbm, gate_hbm, h_hbm, x_hbm, out_hbm,
             idx_v, c_v, h_v, acc_v, part_v, rows0, rows1, sem0, sem1):
        wid = lax.axis_index("s") * 2 + lax.axis_index("c")
        bufs, sems = (rows0, rows1), (sem0, sem1)
        lane = lax.iota(jnp.int32, SC_LANES)

        def stream(tab_hbm, compute):
            def gather(chunk, slot):
                start = pl.multiple_of(chunk * SC_ROWS, SC_ROWS)
                return pltpu.make_async_copy(tab_hbm.at[idx_v.at[pl.ds(start, SC_ROWS)]], bufs[slot], sems[slot])

            gather(0, 0).start()

            @pl.loop(0, n_pairs)
            def _(p):
                c0 = 2 * p
                gather(c0 + 1, 1).start()
                gather(c0, 0).wait()
                compute(c0, 0)
                gather(jnp.minimum(c0 + 2, 2 * n_pairs - 1), 0).start()
                gather(c0 + 1, 1).wait()
                compute(c0 + 1, 1)

            gather(0, 0).wait()

        def dots(chunk, slot):
            tok = chunk // n_chunk
            for blk in range(n_blk):
                where = [blk * SC_REGS + j for j in range(SC_REGS)]
                hs = [tok_piece(h_v, tok, q) for q in where]

                def row(k, carry):
                    parts = [None] * 4
                    for i, q in enumerate(where):
                        term = piece(bufs[slot], k, q) * hs[i]
                        parts[i % 4] = term if parts[i % 4] is None else parts[i % 4] + term
                    tot = (parts[0] + parts[1]) + (parts[2] + parts[3])
                    at = pl.ds(pl.multiple_of((chunk * SC_ROWS + k) * SC_LANES, SC_LANES), SC_LANES)
                    if blk == 0:
                        part_v[at] = tot
                    else:
                        part_v[at] = part_v[at] + tot
                    return carry

                lax.fori_loop(0, SC_ROWS, row, 0)

        def weigh(chunk, slot):
            tok = chunk // n_chunk
            for blk in range(n_blk):
                where = [blk * SC_REGS + j for j in range(SC_REGS)]

                def row(k, accs):
                    ck = plsc.load_gather(c_v, [jnp.full((SC_LANES,), chunk * SC_ROWS, jnp.int32) + k])
                    return tuple(a + ck * piece(bufs[slot], k, q) for a, q in zip(accs, where))

                accs = lax.fori_loop(0, SC_ROWS, row, tuple(tok_piece(acc_v, tok, q) for q in where))
                for a, q in zip(accs, where):
                    acc_v[tok * N_CHUNK + q // per_tile, pl.ds(SC_LANES * (q % per_tile), SC_LANES)] = a

        @pl.loop(0, per_w // grp)
        def _(g):
            t0 = wid * per_w + g * grp
            flat = pl.ds(pl.multiple_of(t0 * PEER_SEL, sel), sel)
            pltpu.sync_copy(idx_hbm.at[flat], idx_v)
            pltpu.sync_copy(gate_hbm.at[flat], c_v)
            tok_rows = grp * N_CHUNK
            src_rows = pl.ds(pl.multiple_of((t_off + t0) * N_CHUNK, tok_rows), tok_rows)
            pltpu.sync_copy(h_hbm.at[src_rows], h_v)
            pltpu.sync_copy(x_hbm.at[src_rows], acc_v)
            stream(u_hbm, dots)

            @pl.loop(0, sel // SC_LANES)
            def _(m):
                base = (m * SC_LANES + lane) * SC_LANES
                score = plsc.load_gather(part_v, [base])
                for l in range(1, SC_LANES):
                    score = score + plsc.load_gather(part_v, [base + l])
                z = math.sqrt(2.0 / math.pi) * (score + 0.044715 * (score * score * score))
                tanh_z = 1.0 - 2.0 / (jnp.exp(2.0 * z) + 1.0)
                at = pl.ds(pl.multiple_of(m * SC_LANES, SC_LANES), SC_LANES)
                c_v[at] = c_v[at] * (0.5 * score * (1.0 + tanh_z))

            stream(v_hbm, weigh)
            pltpu.sync_copy(acc_v, out_hbm.at[pl.ds(pl.multiple_of(t0 * N_CHUNK, tok_rows), tok_rows)])

    tile = lambda n: pltpu.VMEM((n, N_CHUNK, LANES), F32)
    tok_tiles = pltpu.VMEM((grp * N_CHUNK, LANES), F32)
    return pl.kernel(
        body, mesh=mesh, out_type=jax.ShapeDtypeStruct((ts * N_CHUNK, LANES), F32),
        scratch_types=[pltpu.VMEM((sel,), jnp.int32), pltpu.VMEM((sel,), F32), tok_tiles, tok_tiles,
                       pltpu.VMEM((sel * SC_LANES,), F32), tile(SC_ROWS), tile(SC_ROWS),
                       pltpu.SemaphoreType.DMA, pltpu.SemaphoreType.DMA],
        compiler_params=pltpu.CompilerParams(needs_layout_passes=False, use_tc_tiling_on_sc=True),
    )(tab_u, tab_v, eidx.reshape(-1), gate.reshape(-1), h8, x8)


def _final_norm_kernel(x_ref, g_ref, o_ref):
    o_ref[...] = _rms(x_ref[...], g_ref[...])


def _final_norm(x2, g):
    t, d = x2.shape
    tm = min(PROJ_BLOCK, t)
    spec = pl.BlockSpec((tm, d), lambda i: (i, 0))
    return pl.pallas_call(
        _final_norm_kernel, grid=(t // tm,), in_specs=[spec, _full((1, d))], out_specs=spec,
        out_shape=jax.ShapeDtypeStruct((t, d), F32), compiler_params=_params(("parallel",)),
    )(x2, g.reshape(1, d).astype(F32))


def kernel(x, mem, norm_mix, w_in, rw_mu, rw_w0, rw_w2, rw_a0, rw_a2, rw_g2, rw_kk, rw_ka, rw_rk, rw_v0, rw_v1, rw_v2, rw_lnx_g, rw_lnx_b, s5_a_re, s5_a_im, s5_log_dt, s5_b_re, s5_b_im, s5_c_re, s5_c_im, s5_d, s5_glu_w, s5_glu_b, s5_out_g, w_out, norm_xa, norm_mem, xa_wq, xa_wk, xa_wv, xa_wo, norm_ffn, peer_wq, peer_keys, peer_u, peer_v, norm_final):
    bsz, seq, d = x.shape
    depth = w_in.shape[0]

    def layer(l, xg, memg, v_first, tabs, sc_share):
        nb = xg.shape[0]
        t = nb * seq
        x2 = xg.reshape(t, d)
        ws = [w_in[l][:, :RW_COLS].astype(BF16), w_in[l][:, RW_COLS:].astype(BF16)]
        if l > 0:
            ws.append(_pad_rows(rw_v1[l - 1].T, LANES, 0).T.astype(BF16))
        outs = _norm_proj(x2, norm_mix[l], ws, [F32] * len(ws), PROJ_BLOCK)
        z_rw = outs[0].reshape(nb, seq, RW_COLS)
        u_s5 = outs[1].reshape(nb, seq, D_S5)
        hv = outs[2].reshape(nb, seq, LANES) if l > 0 else None
        rw_prm = dict(mu=rw_mu[l], w0=rw_w0[l], w2=rw_w2[l], a0=rw_a0[l], a2=rw_a2[l], g2=rw_g2[l],
                      kk=rw_kk[l], ka=rw_ka[l], rk=rw_rk[l], lng=rw_lnx_g[l], lnb=rw_lnx_b[l])
        if l > 0:
            rw_prm.update(v0=rw_v0[l - 1], v2=rw_v2[l - 1])
        y_rw, v_first = _rwkv(z_rw, hv, v_first, rw_prm, nb, seq)
        s5_prm = dict(a_re=s5_a_re[l], a_im=s5_a_im[l], log_dt=s5_log_dt[l], b_re=s5_b_re[l], b_im=s5_b_im[l],
                      c_re=s5_c_re[l], c_im=s5_c_im[l], d=s5_d[l], glu_w=s5_glu_w[l], glu_b=s5_glu_b[l],
                      out_g=s5_out_g[l])
        y_s5 = _s5(u_s5, s5_prm, nb, seq)
        kv = _norm_proj(memg.reshape(nb * N_MEM, d), norm_mem[l],
                        [xa_wk[l].astype(BF16), xa_wv[l].astype(BF16)], [BF16, BF16], PROJ_BLOCK)
        kmem = kv[0].reshape(nb, N_MEM, d)
        vmem = kv[1].reshape(nb, N_MEM, d)
        xg = _mix_xattn(xg, y_rw, y_s5, w_out[l], norm_xa[l], xa_wq[l], kmem, vmem, xa_wo[l], nb, seq)
        x2 = xg.reshape(t, d)
        x8, h8, base, shift, gate = _peer_select(x2, norm_ffn[l], peer_wq[l], peer_keys[l])
        u3, v3, u_packed, v_packed = tabs
        n_sc = int(t * sc_share)
        assert n_sc % (SC_WORKERS * SC_GROUP) == 0 and (t - n_sc) % GATHER_BLOCK == 0
        t_tc = t - n_sc
        eidx_sc = (base[t_tc:] >> 2) + (shift[t_tc:] >> 4)
        x_sc = _peer_sc(eidx_sc, gate[t_tc:], h8, x8, u3, v3, t_tc)
        c = _peer_u(base, shift, h8, gate, u_packed, t_tc)
        x_new = _peer_v(base, shift, c, x8, x2, v_packed, t_tc)
        xg = lax.dynamic_update_slice(x_new, x_sc.reshape(n_sc, d), (t_tc, 0)).reshape(nb, seq, d)
        return xg, v_first

    nb = bsz // SEQ_GROUPS
    xs = [x[g * nb:(g + 1) * nb] for g in range(SEQ_GROUPS)]
    mems = [mem[g * nb:(g + 1) * nb] for g in range(SEQ_GROUPS)]
    v_firsts = [None] * SEQ_GROUPS
    for l in range(depth):
        u3, v3 = _tile_table(peer_u[l]), _tile_table(peer_v[l])
        tabs = (u3, v3, _pack_table(u3), _pack_table(v3))
        for g in range(SEQ_GROUPS):
            last = l == depth - 1 and g == SEQ_GROUPS - 1
            share = SC_SHARE_LAST if last else SC_SHARE_PIPELINED
            xs[g], v_firsts[g] = layer(l, xs[g], mems[g], v_firsts[g], tabs, share)
    outs = [_final_norm(xg.reshape(nb * seq, d), norm_final).reshape(nb, seq, d) for xg in xs]
    return jnp.concatenate(outs, axis=0)
```

```python
import functools
import math

import jax
import jax.numpy as jnp
from jax import lax
from jax.experimental import pallas as pl
from jax.experimental.pallas import tpu as pltpu
from jax.experimental.pallas import tpu_sc as plsc

F32 = jnp.float32
BF16 = jnp.bfloat16
U32 = jnp.uint32

LANES = 128
SUBLANES = 8
VMEM_LIMIT = 56 * 1024 * 1024

D_MODEL = 1024
D_RWKV = 512
RW_HEAD = 64
RW_COLS = 1792
D_S5 = 512
S5_GROUPS = 32
S5_CH = 16
S5_STATE = 64
S5_MODES = S5_GROUPS * S5_STATE
N_MEM = 256
XA_HEADS = 4
XA_HEAD = 256
PEER_HEADS = 8
PEER_NKEYS = 128
PEER_TOPK = 16
PEER_SEL = PEER_HEADS * PEER_TOPK
RMS_EPS = 1e-6
GN_EPS = 64e-5

RW_CHUNK = 128
S5_BLOCK = 256
PROJ_BLOCK = 512
XA_BLOCK = 256
SEL_BLOCK = 128
GATHER_BLOCK = 64


def _params(sem):
    return pltpu.CompilerParams(dimension_semantics=sem, vmem_limit_bytes=VMEM_LIMIT)


def _rms(x, g):
    ms = jnp.mean(x * x, axis=-1, keepdims=True)
    return x * lax.rsqrt(ms + RMS_EPS) * g


def _bdot(a, b):
    return jnp.dot(a.astype(BF16), b.astype(BF16), preferred_element_type=F32)


def _bdot_nt(a, b):
    return lax.dot_general(a.astype(BF16), b.astype(BF16), (((1,), (1,)), ((), ())),
                           preferred_element_type=F32)


def _sigmoid(x):
    return 1.0 / (1.0 + jnp.exp(-x))


def _softplus(x):
    return jnp.maximum(x, 0.0) + jnp.log(1.0 + jnp.exp(-jnp.abs(x)))


def _gelu(x):
    return 0.5 * x * (1.0 + jnp.tanh(math.sqrt(2.0 / math.pi) * (x + 0.044715 * (x * x * x))))


def _full(shape):
    n = len(shape)
    return pl.BlockSpec(shape, lambda *_: (0,) * n)


def _norm_proj_kernel(*refs, n_out):
    x_ref, g_ref = refs[0], refs[1]
    w_refs = refs[2:2 + n_out]
    o_refs = refs[2 + n_out:]
    h = _rms(x_ref[...], g_ref[...]).astype(BF16)
    for w_ref, o_ref in zip(w_refs, o_refs):
        o_ref[...] = jnp.dot(h, w_ref[...], preferred_element_type=F32).astype(o_ref.dtype)


def _norm_proj(x2, g, ws, out_dtypes, block):
    t, d = x2.shape
    block = min(block, t)
    in_specs = [pl.BlockSpec((block, d), lambda i: (i, 0)), _full((1, d))]
    in_specs += [_full(w.shape) for w in ws]
    out_specs = [pl.BlockSpec((block, w.shape[1]), lambda i: (i, 0)) for w in ws]
    out_shape = [jax.ShapeDtypeStruct((t, w.shape[1]), dt) for w, dt in zip(ws, out_dtypes)]
    return pl.pallas_call(
        functools.partial(_norm_proj_kernel, n_out=len(ws)),
        grid=(t // block,), in_specs=in_specs, out_specs=out_specs, out_shape=out_shape,
        compiler_params=_params(("parallel",)),
    )(x2, g.reshape(1, d), *ws)


def _seg_sum(x, mseg):
    hi = x.astype(BF16)
    lo = (x - hi.astype(F32)).astype(BF16)
    return (jnp.dot(hi, mseg, preferred_element_type=F32)
            + jnp.dot(lo, mseg, preferred_element_type=F32))


def _col_bcast(row):
    return jnp.broadcast_to(row, (LANES, LANES)).T


def _rwkv_kernel(*refs, first_layer):
    if first_layer:
        (z_ref, mu_ref, w0_ref, w2_ref, a0_ref, a2_ref, g2_ref, kk_ref, ka_ref, rk_ref,
         lng_ref, lnb_ref, mseg_ref, y_ref, vf_out_ref, zprev, hst) = refs
    else:
        (z_ref, hv_ref, vf_ref, v0_ref, v2_ref, mu_ref, w0_ref, w2_ref, a0_ref, a2_ref, g2_ref,
         kk_ref, ka_ref, rk_ref, lng_ref, lnb_ref, mseg_ref, y_ref, zprev, hst) = refs
    L = RW_CHUNK

    @pl.when(pl.program_id(1) == 0)
    def _():
        zprev[...] = jnp.zeros_like(zprev)
        hst[...] = jnp.zeros_like(hst)

    z = z_ref[0]
    row = lax.broadcasted_iota(jnp.int32, (L, 1), 0)
    zs = jnp.where(row == 0, zprev[...], pltpu.roll(z, 1, 0))
    zprev[...] = z[L - 1:L, :]
    z = z + (zs - z) * mu_ref[...]
    r = z[:, 0:512]
    k = z[:, 512:1024]
    v = z[:, 1024:1536]
    wa = z[:, 1536:1664]
    gd = z[:, 1664:1792]
    mseg = mseg_ref[...]

    wlin = w0_ref[...] + _bdot(jnp.tanh(wa), w2_ref[...])
    lw = -jnp.exp(-_softplus(-wlin) - 0.5)
    a = _sigmoid(a0_ref[...] + _bdot(wa, a2_ref[...]))
    g = _bdot(_sigmoid(gd), g2_ref[...])
    if first_layer:
        vf_out_ref[0] = v
    else:
        v = v + (vf_ref[0] - v) * _sigmoid(v0_ref[...] + _bdot(hv_ref[0], v2_ref[...]))
    kk = k * kk_ref[...]
    kk = kk / jnp.maximum(jnp.sqrt(_seg_sum(kk * kk, mseg)), 1e-12)
    k2 = k * (1.0 + (a - 1.0) * ka_ref[...])
    av = -kk
    bv = kk * a

    ti = lax.broadcasted_iota(jnp.int32, (L, L), 0)
    si = lax.broadcasted_iota(jnp.int32, (L, L), 1)
    tril = (ti >= si).astype(F32)
    cum = jnp.dot(tril, lw, preferred_element_type=F32, precision=lax.Precision.HIGHEST)
    mid = cum[L // 2 - 1:L // 2, :]
    cm = cum - mid
    ecum = jnp.exp(cm)
    einv = jnp.exp(-cm)
    rt = r * ecum
    kt = k2 * einv
    bt = bv * einv
    at = av * jnp.exp(cm - lw)
    p_end = jnp.exp(cum[L - 1:L, :])
    e_end = ecum[L - 1:L, :]
    e_mid = jnp.exp(mid)

    lane = lax.broadcasted_iota(jnp.int32, (1, LANES), 1)
    m0 = (lane < RW_HEAD).astype(F32)
    m1 = 1.0 - m0
    strict = ti > si
    incl = ti >= si
    bi = lax.broadcasted_iota(jnp.int32, (LANES, LANES), 0) < RW_HEAD
    bj = lax.broadcasted_iota(jnp.int32, (LANES, LANES), 1) < RW_HEAD
    bdmask = (bi == bj).astype(F32)
    zeros_ll = jnp.zeros((L, L), F32)

    ys = []
    for p in range(D_RWKV // LANES):
        sl = slice(LANES * p, LANES * (p + 1))
        A, B, K, R, V = at[:, sl], bt[:, sl], kt[:, sl], rt[:, sl], v[:, sl]
        btkt = jnp.concatenate([B.T, K.T], axis=1)
        h0 = hst[p]
        h0m = h0 * _col_bcast(e_mid[:, sl])
        sc = _bdot(jnp.concatenate([A * m0, A * m1, R * m0, R * m1], axis=0), btkt)
        aab = [jnp.where(strict, sc[e * L:(e + 1) * L, 0:L], 0.0) for e in range(2)]
        aak = [jnp.where(strict, sc[e * L:(e + 1) * L, L:2 * L], 0.0) for e in range(2)]
        arb = [jnp.where(incl, sc[(2 + e) * L:(3 + e) * L, 0:L], 0.0) for e in range(2)]
        ark = [jnp.where(incl, sc[(2 + e) * L:(3 + e) * L, L:2 * L], 0.0) for e in range(2)]
        arh = _bdot(jnp.concatenate([A, R], axis=0), h0m)
        v01 = jnp.concatenate([V * m0, V * m1], axis=0)
        x = arh[0:L] + _bdot(jnp.concatenate(aak, axis=1), v01)
        pm = jnp.concatenate(aab, axis=1)
        n_fac = int(math.log2(L))
        for it in range(n_fac):
            x = x + _bdot(pm, jnp.concatenate([x * m0, x * m1], axis=0))
            if it + 1 < n_fac:
                pd = jnp.concatenate(
                    [jnp.concatenate([pm[:, 0:L], zeros_ll], axis=1),
                     jnp.concatenate([zeros_ll, pm[:, L:2 * L]], axis=1)], axis=0)
                pm = _bdot(pm, pd)
        u = x
        yp = arh[L:2 * L] + _bdot(jnp.concatenate(arb + ark, axis=1),
                                  jnp.concatenate([u * m0, u * m1, v01], axis=0))
        upd = _bdot(btkt, jnp.concatenate([u, V], axis=0))
        hst[p] = (h0 * _col_bcast(p_end[:, sl]) + upd * _col_bcast(e_end[:, sl])) * bdmask
        ys.append(yp)
    y = jnp.concatenate(ys, axis=1)

    mean = _seg_sum(y, mseg) * (1.0 / RW_HEAD)
    d = y - mean
    var = _seg_sum(d * d, mseg) * (1.0 / RW_HEAD)
    yn = d * lax.rsqrt(var + GN_EPS) * lng_ref[...] + lnb_ref[...]
    bonus = _seg_sum(r * k2 * rk_ref[...], mseg) * v
    y_ref[0] = (yn + bonus) * g


def _pad_rows(w, rows, offset):
    out = jnp.zeros((rows, w.shape[1]), w.dtype)
    return out.at[offset:offset + w.shape[0]].set(w)


def _rwkv(z_rw, hv, v_first, prm, bsz, seq):
    L = RW_CHUNK
    first = v_first is None
    row = lambda a: a.reshape(1, -1).astype(F32)
    hid = jnp.arange(D_RWKV) // RW_HEAD
    mseg = (hid[:, None] == hid[None, :]).astype(BF16)
    w2p = _pad_rows(prm['w2'], LANES, 0).astype(BF16)
    a2p = _pad_rows(prm['a2'], LANES, 64).astype(BF16)
    common = [row(prm['mu']), row(prm['w0']), w2p, row(prm['a0']), a2p, prm['g2'].astype(BF16),
              row(prm['kk']), row(prm['ka']), row(prm['rk']), row(prm['lng']), row(prm['lnb']), mseg]
    tok = lambda w: pl.BlockSpec((1, L, w), lambda b, t: (b, t, 0))
    common_specs = [_full(c.shape) for c in common]
    y_shape = jax.ShapeDtypeStruct((bsz, seq, D_RWKV), F32)
    scratch = [pltpu.VMEM((1, RW_COLS), F32), pltpu.VMEM((D_RWKV // LANES, LANES, LANES), F32)]
    if first:
        args = [z_rw] + common
        in_specs = [tok(RW_COLS)] + common_specs
        out_shape = [y_shape, y_shape]
        out_specs = [tok(D_RWKV), tok(D_RWKV)]
    else:
        v2p = _pad_rows(prm['v2'], LANES, 0).astype(BF16)
        extra = [row(prm['v0']), v2p]
        args = [z_rw, hv, v_first] + extra + common
        in_specs = [tok(RW_COLS), tok(LANES), tok(D_RWKV)] + [_full(c.shape) for c in extra] + common_specs
        out_shape = [y_shape]
        out_specs = [tok(D_RWKV)]
    outs = pl.pallas_call(
        functools.partial(_rwkv_kernel, first_layer=first),
        grid=(bsz, seq // L), in_specs=in_specs, out_specs=out_specs, out_shape=out_shape,
        scratch_shapes=scratch, compiler_params=_params(("parallel", "arbitrary")),
    )(*args)
    return (outs[0], outs[1]) if first else (outs[0], v_first)


def _s5_kernel(u_ref, wb_ref, wc_ref, lpr_ref, lpi_ref, d_ref, gw_ref, gb_ref, og_ref, o_ref,
               car_re, car_im, xre, xim):
    tb = u_ref.shape[1]

    @pl.when(pl.program_id(1) == 0)
    def _():
        car_re[...] = jnp.zeros_like(car_re)
        car_im[...] = jnp.zeros_like(car_im)

    u = u_ref[0]
    bu = _bdot(u, wb_ref[...])
    xre[...] = bu[:, 0:S5_MODES]
    xim[...] = bu[:, S5_MODES:2 * S5_MODES]
    row = lax.broadcasted_iota(jnp.int32, (SUBLANES, 1), 0)

    def tile(i, carry):
        cr, ci = carry
        rs = pl.ds(pl.multiple_of(i * SUBLANES, SUBLANES), SUBLANES)
        br, bi = xre[rs, :], xim[rs, :]
        for dist in (1, 2, 4):
            keep = row >= dist
            sr = jnp.where(keep, pltpu.roll(br, dist, 0), 0.0)
            si = jnp.where(keep, pltpu.roll(bi, dist, 0), 0.0)
            lr = lpr_ref[dist - 1:dist, :]
            li = lpi_ref[dist - 1:dist, :]
            br, bi = br + lr * sr - li * si, bi + lr * si + li * sr
        pr, pi = lpr_ref[...], lpi_ref[...]
        xr = br + pr * cr - pi * ci
        xi = bi + pr * ci + pi * cr
        xre[rs, :] = xr
        xim[rs, :] = xi
        return xr[SUBLANES - 1:SUBLANES, :], xi[SUBLANES - 1:SUBLANES, :]

    cr, ci = lax.fori_loop(0, tb // SUBLANES, tile, (car_re[...], car_im[...]))
    car_re[...] = cr
    car_im[...] = ci
    wc = wc_ref[...]
    y = _bdot(xre[...], wc[0:S5_MODES]) + _bdot(xim[...], wc[S5_MODES:2 * S5_MODES])
    y = _gelu(y + d_ref[...] * u)
    y = y * _sigmoid(_bdot(y, gw_ref[...]) + gb_ref[...])
    o_ref[0] = _rms(y, og_ref[...])


def _s5_weights(a_re, a_im, log_dt, b_re, b_im, c_re, c_im):
    lam_re = jnp.minimum(a_re.astype(F32), -1e-4)
    lam_im = a_im.astype(F32)
    dt = jnp.exp(log_dt.astype(F32))[:, None]
    mag = jnp.exp(lam_re * dt)
    lb_re = mag * jnp.cos(lam_im * dt)
    lb_im = mag * jnp.sin(lam_im * dt)
    den = lam_re * lam_re + lam_im * lam_im
    c1_re = ((lb_re - 1.0) * lam_re + lb_im * lam_im) / den
    c1_im = (lb_im * lam_re - (lb_re - 1.0) * lam_im) / den
    br, bi = b_re.astype(F32), b_im.astype(F32)
    bb_re = c1_re[..., None] * br - c1_im[..., None] * bi
    bb_im = c1_re[..., None] * bi + c1_im[..., None] * br
    eye = jnp.eye(S5_GROUPS, dtype=F32)
    wb_re = jnp.einsum('gpc,gh->gchp', bb_re, eye).reshape(D_S5, S5_MODES)
    wb_im = jnp.einsum('gpc,gh->gchp', bb_im, eye).reshape(D_S5, S5_MODES)
    wb = jnp.concatenate([wb_re, wb_im], axis=1).astype(BF16)
    wc_re = jnp.einsum('gcp,gh->gphc', c_re.astype(F32), eye).reshape(S5_MODES, D_S5)
    wc_im = jnp.einsum('gcp,gh->gphc', c_im.astype(F32), eye).reshape(S5_MODES, D_S5)
    wc = jnp.concatenate([wc_re, -wc_im], axis=0).astype(BF16)
    pr, pi = [lb_re], [lb_im]
    for _ in range(SUBLANES - 1):
        pr, pi = pr + [pr[-1] * lb_re - pi[-1] * lb_im], pi + [pr[-1] * lb_im + pi[-1] * lb_re]
    lp_re = jnp.stack(pr).reshape(SUBLANES, S5_MODES)
    lp_im = jnp.stack(pi).reshape(SUBLANES, S5_MODES)
    return wb, wc, lp_re, lp_im


def _s5(u, prm, bsz, seq):
    tb = min(S5_BLOCK, seq)
    wb, wc, lp_re, lp_im = _s5_weights(prm['a_re'], prm['a_im'], prm['log_dt'], prm['b_re'], prm['b_im'],
                                       prm['c_re'], prm['c_im'])
    row = lambda a: a.reshape(1, -1).astype(F32)
    consts = [wb, wc, lp_re, lp_im, row(prm['d']), prm['glu_w'].astype(BF16), row(prm['glu_b']),
              row(prm['out_g'])]
    tok = pl.BlockSpec((1, tb, D_S5), lambda b, t: (b, t, 0))
    return pl.pallas_call(
        _s5_kernel, grid=(bsz, seq // tb),
        in_specs=[tok] + [_full(c.shape) for c in consts], out_specs=tok,
        out_shape=jax.ShapeDtypeStruct((bsz, seq, D_S5), F32),
        scratch_shapes=[pltpu.VMEM((1, S5_MODES), F32), pltpu.VMEM((1, S5_MODES), F32),
                        pltpu.VMEM((tb, S5_MODES), F32), pltpu.VMEM((tb, S5_MODES), F32)],
        compiler_params=_params(("parallel", "arbitrary")),
    )(u, *consts)


def _mix_xattn_kernel(x_ref, yr_ref, ys_ref, wo1_ref, wo2_ref, g_ref, wq_ref, k_ref, v_ref, wo_ref, o_ref):
    x1 = x_ref[0] + _bdot(yr_ref[0], wo1_ref[...]) + _bdot(ys_ref[0], wo2_ref[...])
    h = _rms(x1, g_ref[...])
    q = _bdot(h, wq_ref[...])
    km, vm = k_ref[0], v_ref[0]
    outs = []
    for hd in range(XA_HEADS):
        sl = slice(XA_HEAD * hd, XA_HEAD * (hd + 1))
        s = _bdot_nt(q[:, sl], km[:, sl]) * (XA_HEAD ** -0.5)
        s = s - jnp.max(s, axis=-1, keepdims=True)
        e = jnp.exp(s)
        p = e / jnp.sum(e, axis=-1, keepdims=True)
        outs.append(_bdot(p, vm[:, sl]))
    o = jnp.concatenate(outs, axis=1)
    o_ref[0] = x1 + _bdot(o, wo_ref[...])


def _mix_xattn(x, y_rw, y_s5, w_out, g, wq, kmem, vmem, wo, bsz, seq):
    tm = min(XA_BLOCK, seq)
    consts_a = [w_out[:D_RWKV].astype(BF16), w_out[D_RWKV:].astype(BF16), g.reshape(1, -1).astype(F32),
                wq.astype(BF16)]
    tok = lambda w: pl.BlockSpec((1, tm, w), lambda b, t: (b, t, 0))
    mem = pl.BlockSpec((1, N_MEM, D_MODEL), lambda b, t: (b, 0, 0))
    wo_b = wo.astype(BF16)
    return pl.pallas_call(
        _mix_xattn_kernel, grid=(bsz, seq // tm),
        in_specs=[tok(D_MODEL), tok(D_RWKV), tok(D_S5)] + [_full(c.shape) for c in consts_a]
                 + [mem, mem, _full(wo_b.shape)],
        out_specs=tok(D_MODEL), out_shape=jax.ShapeDtypeStruct((bsz, seq, D_MODEL), F32),
        compiler_params=_params(("parallel", "parallel")),
    )(x, y_rw, y_s5, *consts_a, kmem, vmem, wo_b)


def _top_rows(work, order, aux, val_ref, idx_ref):
    for it in range(PEER_TOPK):
        m = jnp.max(work, axis=0, keepdims=True)
        pos = jnp.min(jnp.where(work == m, order, jnp.inf), axis=0, keepdims=True)
        hit = order == pos
        val_ref[it:it + 1, :] = m
        if aux is None:
            idx_ref[it:it + 1, :] = pos
        else:
            idx_ref[it:it + 1, :] = jnp.sum(jnp.where(hit, aux, 0.0), axis=0, keepdims=True)
        work = jnp.where(hit, -jnp.inf, work)


_CAND_ROW_BLOCKS = [(0, PEER_TOPK), (1, SUBLANES), (2, SUBLANES), (3, SUBLANES)]
_CAND_COL_BLOCKS = [(0, PEER_TOPK, 4, 15), (1, SUBLANES, 4, 7), (2, SUBLANES, 4, 4)]
N_CAND = sum(n for _, n in _CAND_ROW_BLOCKS) + sum(n for _, n, _, _ in _CAND_COL_BLOCKS)


def _cand_consts(tm):
    flat, neg = [], []
    for a, nb in _CAND_ROW_BLOCKS:
        flat += [a * PEER_TOPK + b for b in range(nb)]
        neg += [0.0] * nb
    for b, na, lo, hi in _CAND_COL_BLOCKS:
        flat += [a * PEER_TOPK + b for a in range(na)]
        neg += [0.0 if lo <= a <= hi else -float('inf') for a in range(na)]
    flat = [f if n == 0.0 else 1000.0 + i for i, (f, n) in enumerate(zip(flat, neg))]
    col = lambda v: jnp.broadcast_to(jnp.asarray(v, F32)[:, None], (N_CAND, tm))
    return col(flat), col(neg)


def _cand_rows(row_vals, col_vals, combine):
    blocks = [combine(row_vals[a:a + 1, :], col_vals[0:nb, :]) for a, nb in _CAND_ROW_BLOCKS]
    blocks += [combine(row_vals[0:na, :], col_vals[b:b + 1, :]) for b, na, _, _ in _CAND_COL_BLOCKS]
    return jnp.concatenate(blocks, axis=0)


SEL_HEADS_PER_STEP = 2


def _peer_select_kernel(x_ref, g_ref, wq_ref, keys_ref, cflat_ref, cneg_ref, x8_ref, h8_ref, base_ref, shift_ref,
                        gate_ref, q3, idx_t, gate_t, *lists):
    tm = x_ref.shape[0]
    n = SEL_HEADS_PER_STEP
    s1, i1, s2, i2, top, eid = (lists[j * n:(j + 1) * n] for j in range(6))
    x = x_ref[...]
    h = _rms(x, g_ref[...])
    for r in range(D_MODEL // LANES):
        rows = pl.ds(r, tm, stride=D_MODEL // LANES)
        h8_ref[rows, :] = h[:, LANES * r:LANES * (r + 1)]
        x8_ref[rows, :] = x[:, LANES * r:LANES * (r + 1)]
    q = _bdot(h, wq_ref[...])
    for j in range(2 * PEER_HEADS):
        q3[j] = q[:, LANES * j:LANES * (j + 1)]
    iota_k = lax.broadcasted_iota(jnp.int32, (PEER_NKEYS, tm), 0).astype(F32)

    def heads(step, _):
        for u in range(SEL_HEADS_PER_STEP):
            hd = step * SEL_HEADS_PER_STEP + u
            sc1 = _bdot_nt(keys_ref[2 * hd], q3[2 * hd])
            sc2 = _bdot_nt(keys_ref[2 * hd + 1], q3[2 * hd + 1])
            _top_rows(sc1, iota_k, None, s1[u], i1[u])
            _top_rows(sc2, iota_k, None, s2[u], i2[u])
            cand = _cand_rows(s1[u][...], s2[u][...], lambda x, y: x + y) + cneg_ref[...]
            cidx = _cand_rows(i1[u][...], i2[u][...], lambda x, y: x * float(PEER_NKEYS) + y)
            _top_rows(cand, cflat_ref[...], cidx, top[u], eid[u])
            tv = top[u][...]
            e = jnp.exp(tv - jnp.max(tv, axis=0, keepdims=True))
            rs = pl.ds(pl.multiple_of(hd * PEER_TOPK, PEER_TOPK), PEER_TOPK)
            idx_t[rs, :] = eid[u][...]
            gate_t[rs, :] = e / jnp.sum(e, axis=0, keepdims=True)
        return 0

    lax.fori_loop(0, PEER_HEADS // SEL_HEADS_PER_STEP, heads, 0)
    e_t = idx_t[...].T
    pair = jnp.floor(e_t * 0.5)
    base_ref[...] = (pair * float(SUBLANES)).astype(jnp.int32)
    shift_ref[...] = ((e_t - 2.0 * pair) * 16.0).astype(jnp.int32)
    gate_ref[...] = gate_t[...].T


def _peer_select(x2, g, wq, keys):
    t = x2.shape[0]
    tm = SEL_BLOCK
    keys_b = keys.reshape(2 * PEER_HEADS, PEER_NKEYS, LANES).astype(BF16)
    wq_b = wq.astype(BF16)
    cflat, cneg = _cand_consts(tm)
    tokspec = lambda w: pl.BlockSpec((tm, w), lambda i: (i, 0))
    vm = lambda r: pltpu.VMEM((r, tm), F32)
    return pl.pallas_call(
        _peer_select_kernel, grid=(t // tm,),
        in_specs=[tokspec(D_MODEL), _full((1, D_MODEL)), _full(wq_b.shape), _full(keys_b.shape),
                  _full(cflat.shape), _full(cneg.shape)],
        out_specs=[pl.BlockSpec((tm * SUBLANES, LANES), lambda i: (i, 0))] * 2
                  + [tokspec(PEER_SEL), tokspec(PEER_SEL), tokspec(PEER_SEL)],
        out_shape=[jax.ShapeDtypeStruct((t * SUBLANES, LANES), F32)] * 2 + [
                   jax.ShapeDtypeStruct((t, PEER_SEL), jnp.int32),
                   jax.ShapeDtypeStruct((t, PEER_SEL), jnp.int32), jax.ShapeDtypeStruct((t, PEER_SEL), F32)],
        scratch_shapes=[pltpu.VMEM((2 * PEER_HEADS, tm, LANES), F32), vm(PEER_SEL), vm(PEER_SEL)]
                       + [vm(PEER_TOPK) for _ in range(6 * SEL_HEADS_PER_STEP)],
        compiler_params=_params(("parallel",)),
    )(x2, g.reshape(1, -1).astype(F32), wq_b, keys_b, cflat, cneg)


def _tile_table(tab):
    return tab.reshape(-1, D_MODEL // LANES, LANES)


def _pack_table(tab3):
    n = tab3.shape[0]
    bits = lax.bitcast_convert_type(tab3.astype(BF16), jnp.uint16).astype(U32)
    bits = bits.reshape(n // 2, 2, SUBLANES, LANES)
    packed = (bits[:, 0] << 16) | bits[:, 1]
    return lax.bitcast_convert_type(packed, jnp.int32).reshape(n // 2 * SUBLANES, LANES)


def _splat_into(src_ref, t, dst_ref, slot):
    tile = jnp.broadcast_to(src_ref[pl.ds(t, 1), :], (LANES, LANES)).T
    dst_ref[LANES * slot:LANES * (slot + 1), :] = tile


def _bcast_row(ref, row):
    return jnp.broadcast_to(ref[row:row + 1, :], (SUBLANES, LANES))


def _expert_tile(tab_ref, base, shift_splat, row):
    w = tab_ref[pl.ds(pl.multiple_of(base, SUBLANES), SUBLANES), :]
    return lax.bitcast_convert_type((w << _bcast_row(shift_splat, row)) & jnp.int32(-65536), F32)


def _token_rows(t):
    return pl.ds(pl.multiple_of(t * SUBLANES, SUBLANES), SUBLANES)


def _pipelined_tokens(tg, prep, work):
    prep(0, 0)

    def body(i, _):
        t0 = 2 * i
        prep(t0 + 1, 1)
        work(t0, 0)
        prep(jnp.minimum(t0 + 2, tg - 1), 0)
        work(t0 + 1, 1)
        return 0

    lax.fori_loop(0, tg // 2, body, 0)


def _table_spec(packed, layer, n_layers):
    rows = packed.shape[0] // n_layers
    return pl.BlockSpec((rows, LANES), lambda i: (layer, 0), pipeline_mode=pl.Buffered(1))


N_CHUNK = D_MODEL // LANES
CHUNK_STRIDE = PEER_SEL + SUBLANES


def _peer_u_kernel(base_ref, shift_ref, h_ref, gate_ref, tab_ref, eye_ref, c_ref, shift_splat, *planes):
    tg = gate_ref.shape[0]
    ones = jnp.ones((LANES, LANES), BF16)

    half = N_CHUNK // 2

    def prep(t, slot):
        _splat_into(shift_ref, t, shift_splat, slot)

    def gather(t, slot):
        ht = h_ref[_token_rows(t), :]
        for kx in range(PEER_SEL):
            prod = _expert_tile(tab_ref, base_ref[t, kx], shift_splat, LANES * slot + kx) * ht
            fold = prod + pltpu.roll(prod, half, 0)
            planes[slot][pl.ds(kx, half, stride=CHUNK_STRIDE), :] = fold[0:half, :]

    def finish(t, slot):
        plane = planes[slot]
        acc = plane[0:PEER_SEL, :]
        for r in range(1, half):
            acc = acc + plane[CHUNK_STRIDE * r:CHUNK_STRIDE * r + PEER_SEL, :]
        hi = acc.astype(BF16)
        lo = (acc - hi.astype(F32)).astype(BF16)
        tot = jnp.dot(hi, ones, preferred_element_type=F32) + jnp.dot(lo, ones, preferred_element_type=F32)
        score = jnp.sum(tot * eye_ref[...], axis=0, keepdims=True)
        c_ref[pl.ds(t, 1), :] = gate_ref[pl.ds(t, 1), :] * _gelu(score)

    planes[1][...] = jnp.zeros_like(planes[1])
    prep(0, 0)

    def body(i, _):
        t0 = 2 * i
        prep(t0 + 1, 1)
        gather(t0, 0)
        finish(jnp.maximum(t0 - 1, 0), 1)
        prep(jnp.minimum(t0 + 2, tg - 1), 0)
        gather(t0 + 1, 1)
        finish(t0, 0)
        return 0

    lax.fori_loop(0, tg // 2, body, 0)
    finish(tg - 1, 1)


def _peer_u(base, shift, h8, gate, tab, t, layer, n_layers):
    tg = GATHER_BLOCK
    eye = jnp.eye(LANES, dtype=F32)
    tokrow = pl.BlockSpec((tg, PEER_SEL), lambda i: (i, 0))
    return pl.pallas_call(
        _peer_u_kernel, grid=(t // tg,),
        in_specs=[pl.BlockSpec((tg, PEER_SEL), lambda i: (i, 0), memory_space=pltpu.SMEM), tokrow,
                  pl.BlockSpec((tg * SUBLANES, LANES), lambda i: (i, 0)), tokrow,
                  _table_spec(tab, layer, n_layers), _full(eye.shape)],
        out_specs=tokrow, out_shape=jax.ShapeDtypeStruct((t, PEER_SEL), F32),
        scratch_shapes=[pltpu.VMEM((2 * LANES, LANES), jnp.int32)]
                       + [pltpu.VMEM((N_CHUNK // 2 * CHUNK_STRIDE, LANES), F32)] * 2,
        compiler_params=_params(("parallel",)),
    )(base, shift, h8, gate, tab, eye)


def _peer_v_kernel(base_ref, shift_ref, c_ref, x_ref, tab_ref, _aliased_out, o_ref, shift_splat, c_splat, otile):
    tg = c_ref.shape[0]
    n_acc = 4

    def prep(t, slot):
        _splat_into(shift_ref, t, shift_splat, slot)
        _splat_into(c_ref, t, c_splat, slot)

    def work(t, slot):
        accs = [jnp.zeros((SUBLANES, LANES), F32) for _ in range(n_acc)]
        for kx in range(PEER_SEL):
            row = LANES * slot + kx
            accs[kx % n_acc] = accs[kx % n_acc] + (_bcast_row(c_splat, row)
                                                   * _expert_tile(tab_ref, base_ref[t, kx], shift_splat, row))
        rs = _token_rows(t)
        otile[rs, :] = x_ref[rs, :] + ((accs[0] + accs[1]) + (accs[2] + accs[3]))

    _pipelined_tokens(tg, prep, work)
    for r in range(N_CHUNK):
        o_ref[:, LANES * r:LANES * (r + 1)] = otile[pl.ds(r, tg, stride=N_CHUNK), :]


def _peer_v(base, shift, c, x8, x2, tab, t, layer, n_layers):
    tg = GATHER_BLOCK
    smem = pl.BlockSpec((tg, PEER_SEL), lambda i: (i, 0), memory_space=pltpu.SMEM)
    tokrow = pl.BlockSpec((tg, PEER_SEL), lambda i: (i, 0))
    tile = pl.BlockSpec((tg * SUBLANES, LANES), lambda i: (i, 0))
    return pl.pallas_call(
        _peer_v_kernel, grid=(t // tg,),
        in_specs=[smem, tokrow, tokrow, tile, _table_spec(tab, layer, n_layers),
                  pl.BlockSpec(memory_space=pl.ANY)],
        out_specs=pl.BlockSpec((tg, D_MODEL), lambda i: (i, 0)),
        out_shape=jax.ShapeDtypeStruct(x2.shape, F32),
        input_output_aliases={5: 0},
        scratch_shapes=[pltpu.VMEM((2 * LANES, LANES), jnp.int32), pltpu.VMEM((2 * LANES, LANES), F32),
                        pltpu.VMEM((tg * SUBLANES, LANES), F32)],
        compiler_params=_params(("parallel",)),
    )(base, shift, c, x8, tab, x2)


SC_WORKERS = 32
SC_LANES = 16
SC_ROWS = 32
SC_SHARE_PIPELINED = 42 / 64
SC_SHARE_LAST = 26 / 64
SEQ_GROUPS = 4
SC_GROUP = 8
SC_REGS = 32


def _peer_sc(eidx, gate, h8, x8, tab_u, tab_v, t_off):
    ts = eidx.shape[0]
    per_w = ts // SC_WORKERS
    n_chunk = PEER_SEL // SC_ROWS
    grp = SC_GROUP
    sel = grp * PEER_SEL
    n_pairs = grp * n_chunk // 2
    per_tile = LANES // SC_LANES
    n_blk = D_MODEL // (SC_REGS * SC_LANES)
    mesh = plsc.VectorSubcoreMesh(core_axis_name="c", subcore_axis_name="s")

    def piece(ref, lead, q):
        return ref[lead, q // per_tile, pl.ds(SC_LANES * (q % per_tile), SC_LANES)]

    def tok_piece(ref, tok, q):
        return ref[tok * N_CHUNK + q // per_tile, pl.ds(SC_LANES * (q % per_tile), SC_LANES)]

    def body(u_hbm, v_hbm, idx_hbm, gate_hbm, h_hbm, x_hbm, out_hbm,
             idx_v, c_v, h_v, acc_v, part_v, rows0, rows1, sem0, sem1):
        wid = lax.axis_index("s") * 2 + lax.axis_index("c")
        bufs, sems = (rows0, rows1), (sem0, sem1)
        lane = lax.iota(jnp.int32, SC_LANES)

        def stream(tab_hbm, compute):
            def gather(chunk, slot):
                start = pl.multiple_of(chunk * SC_ROWS, SC_ROWS)
                return pltpu.make_async_copy(tab_hbm.at[idx_v.at[pl.ds(start, SC_ROWS)]], bufs[slot], sems[slot])

            gather(0, 0).start()

            @pl.loop(0, n_pairs)
            def _(p):
                c0 = 2 * p
                gather(c0 + 1, 1).start()
                gather(c0, 0).wait()
                compute(c0, 0)
                gather(jnp.minimum(c0 + 2, 2 * n_pairs - 1), 0).start()
                gather(c0 + 1, 1).wait()
                compute(c0 + 1, 1)

            gather(0, 0).wait()

        def dots(chunk, slot):
            tok = chunk // n_chunk
            for blk in range(n_blk):
                where = [blk * SC_REGS + j for j in range(SC_REGS)]
                hs = [tok_piece(h_v, tok, q) for q in where]

                def row(k, carry):
                    parts = [None] * 4
                    for i, q in enumerate(where):
                        term = piece(bufs[slot], k, q) * hs[i]
                        parts[i % 4] = term if parts[i % 4] is None else parts[i % 4] + term
                    tot = (parts[0] + parts[1]) + (parts[2] + parts[3])
                    at = pl.ds(pl.multiple_of((chunk * SC_ROWS + k) * SC_LANES, SC_LANES), SC_LANES)
                    if blk == 0:
                        part_v[at] = tot
                    else:
                        part_v[at] = part_v[at] + tot
                    return carry

                lax.fori_loop(0, SC_ROWS, row, 0)

        def weigh(chunk, slot):
            tok = chunk // n_chunk
            for blk in range(n_blk):
                where = [blk * SC_REGS + j for j in range(SC_REGS)]

                def row(k, accs):
                    ck = plsc.load_gather(c_v, [jnp.full((SC_LANES,), chunk * SC_ROWS, jnp.int32) + k])
                    return tuple(a + ck * piece(bufs[slot], k, q) for a, q in zip(accs, where))

                accs = lax.fori_loop(0, SC_ROWS, row, tuple(tok_piece(acc_v, tok, q) for q in where))
                for a, q in zip(accs, where):
                    acc_v[tok * N_CHUNK + q // per_tile, pl.ds(SC_LANES * (q % per_tile), SC_LANES)] = a

        @pl.loop(0, per_w // grp)
        def _(g):
            t0 = wid * per_w + g * grp
            flat = pl.ds(pl.multiple_of(t0 * PEER_SEL, sel), sel)
            pltpu.sync_copy(idx_hbm.at[flat], idx_v)
            pltpu.sync_copy(gate_hbm.at[flat], c_v)
            tok_rows = grp * N_CHUNK
            src_rows = pl.ds(pl.multiple_of((t_off + t0) * N_CHUNK, tok_rows), tok_rows)
            pltpu.sync_copy(h_hbm.at[src_rows], h_v)
            pltpu.sync_copy(x_hbm.at[src_rows], acc_v)
            stream(u_hbm, dots)

            @pl.loop(0, sel // SC_LANES)
            def _(m):
                base = (m * SC_LANES + lane) * SC_LANES
                score = plsc.load_gather(part_v, [base])
                for l in range(1, SC_LANES):
                    score = score + plsc.load_gather(part_v, [base + l])
                z = math.sqrt(2.0 / math.pi) * (score + 0.044715 * (score * score * score))
                tanh_z = 1.0 - 2.0 / (jnp.exp(2.0 * z) + 1.0)
                at = pl.ds(pl.multiple_of(m * SC_LANES, SC_LANES), SC_LANES)
                c_v[at] = c_v[at] * (0.5 * score * (1.0 + tanh_z))

            stream(v_hbm, weigh)
            pltpu.sync_copy(acc_v, out_hbm.at[pl.ds(pl.multiple_of(t0 * N_CHUNK, tok_rows), tok_rows)])

    tile = lambda n: pltpu.VMEM((n, N_CHUNK, LANES), F32)
    tok_tiles = pltpu.VMEM((grp * N_CHUNK, LANES), F32)
    return pl.kernel(
        body, mesh=mesh, out_type=jax.ShapeDtypeStruct((ts * N_CHUNK, LANES), F32),
        scratch_types=[pltpu.VMEM((sel,), jnp.int32), pltpu.VMEM((sel,), F32), tok_tiles, tok_tiles,
                       pltpu.VMEM((sel * SC_LANES,), F32), tile(SC_ROWS), tile(SC_ROWS),
                       pltpu.SemaphoreType.DMA, pltpu.SemaphoreType.DMA],
        compiler_params=pltpu.CompilerParams(needs_layout_passes=False, use_tc_tiling_on_sc=True),
    )(tab_u, tab_v, eidx.reshape(-1), gate.reshape(-1), h8, x8)


def _final_norm_kernel(x_ref, g_ref, o_ref):
    o_ref[...] = _rms(x_ref[...], g_ref[...])


def _final_norm(x2, g):
    t, d = x2.shape
    tm = min(PROJ_BLOCK, t)
    spec = pl.BlockSpec((tm, d), lambda i: (i, 0))
    return pl.pallas_call(
        _final_norm_kernel, grid=(t // tm,), in_specs=[spec, _full((1, d))], out_specs=spec,
        out_shape=jax.ShapeDtypeStruct((t, d), F32), compiler_params=_params(("parallel",)),
    )(x2, g.reshape(1, d).astype(F32))


def kernel(x, mem, norm_mix, w_in, rw_mu, rw_w0, rw_w2, rw_a0, rw_a2, rw_g2, rw_kk, rw_ka, rw_rk, rw_v0, rw_v1, rw_v2, rw_lnx_g, rw_lnx_b, s5_a_re, s5_a_im, s5_log_dt, s5_b_re, s5_b_im, s5_c_re, s5_c_im, s5_d, s5_glu_w, s5_glu_b, s5_out_g, w_out, norm_xa, norm_mem, xa_wq, xa_wk, xa_wv, xa_wo, norm_ffn, peer_wq, peer_keys, peer_u, peer_v, norm_final):
    bsz, seq, d = x.shape
    depth = w_in.shape[0]

    def layer(l, xg, memg, v_first, tabs, sc_share):
        nb = xg.shape[0]
        t = nb * seq
        x2 = xg.reshape(t, d)
        ws = [w_in[l][:, :RW_COLS].astype(BF16), w_in[l][:, RW_COLS:].astype(BF16)]
        if l > 0:
            ws.append(_pad_rows(rw_v1[l - 1].T, LANES, 0).T.astype(BF16))
        outs = _norm_proj(x2, norm_mix[l], ws, [F32] * len(ws), PROJ_BLOCK)
        z_rw = outs[0].reshape(nb, seq, RW_COLS)
        u_s5 = outs[1].reshape(nb, seq, D_S5)
        hv = outs[2].reshape(nb, seq, LANES) if l > 0 else None
        rw_prm = dict(mu=rw_mu[l], w0=rw_w0[l], w2=rw_w2[l], a0=rw_a0[l], a2=rw_a2[l], g2=rw_g2[l],
                      kk=rw_kk[l], ka=rw_ka[l], rk=rw_rk[l], lng=rw_lnx_g[l], lnb=rw_lnx_b[l])
        if l > 0:
            rw_prm.update(v0=rw_v0[l - 1], v2=rw_v2[l - 1])
        y_rw, v_first = _rwkv(z_rw, hv, v_first, rw_prm, nb, seq)
        s5_prm = dict(a_re=s5_a_re[l], a_im=s5_a_im[l], log_dt=s5_log_dt[l], b_re=s5_b_re[l], b_im=s5_b_im[l],
                      c_re=s5_c_re[l], c_im=s5_c_im[l], d=s5_d[l], glu_w=s5_glu_w[l], glu_b=s5_glu_b[l],
                      out_g=s5_out_g[l])
        y_s5 = _s5(u_s5, s5_prm, nb, seq)
        kv = _norm_proj(memg.reshape(nb * N_MEM, d), norm_mem[l],
                        [xa_wk[l].astype(BF16), xa_wv[l].astype(BF16)], [BF16, BF16], PROJ_BLOCK)
        kmem = kv[0].reshape(nb, N_MEM, d)
        vmem = kv[1].reshape(nb, N_MEM, d)
        xg = _mix_xattn(xg, y_rw, y_s5, w_out[l], norm_xa[l], xa_wq[l], kmem, vmem, xa_wo[l], nb, seq)
        x2 = xg.reshape(t, d)
        x8, h8, base, shift, gate = _peer_select(x2, norm_ffn[l], peer_wq[l], peer_keys[l])
        u3, v3, u_packed, v_packed = tabs
        n_sc = int(t * sc_share)
        assert n_sc % (SC_WORKERS * SC_GROUP) == 0 and (t - n_sc) % GATHER_BLOCK == 0
        t_tc = t - n_sc
        n_exp = peer_u.shape[1]
        eidx_sc = (base[t_tc:] >> 2) + (shift[t_tc:] >> 4) + l * n_exp
        x_sc = _peer_sc(eidx_sc, gate[t_tc:], h8, x8, u3, v3, t_tc)
        c = _peer_u(base, shift, h8, gate, u_packed, t_tc, l, depth)
        x_new = _peer_v(base, shift, c, x8, x2, v_packed, t_tc, l, depth)
        xg = lax.dynamic_update_slice(x_new, x_sc.reshape(n_sc, d), (t_tc, 0)).reshape(nb, seq, d)
        return xg, v_first

    nb = bsz // SEQ_GROUPS
    xs = [x[g * nb:(g + 1) * nb] for g in range(SEQ_GROUPS)]
    mems = [mem[g * nb:(g + 1) * nb] for g in range(SEQ_GROUPS)]
    v_firsts = [None] * SEQ_GROUPS
    u3, v3 = _tile_table(peer_u), _tile_table(peer_v)
    tabs = (u3, v3, _pack_table(u3), _pack_table(v3))
    for l in range(depth):
        for g in range(SEQ_GROUPS):
            last = l == depth - 1 and g == SEQ_GROUPS - 1
            share = SC_SHARE_LAST if last else SC_SHARE_PIPELINED
            xs[g], v_firsts[g] = layer(l, xs[g], mems[g], v_firsts[g], tabs, share)
    outs = [_final_norm(xg.reshape(nb * seq, d), norm_final).reshape(nb, seq, d) for xg in xs]
    return jnp.concatenate(outs, axis=0)
```

```python
import functools
import math

import jax
import jax.numpy as jnp
from jax import lax
from jax.experimental import pallas as pl
from jax.experimental.pallas import tpu as pltpu
from jax.experimental.pallas import tpu_sc as plsc

F32 = jnp.float32
BF16 = jnp.bfloat16
U32 = jnp.uint32

LANES = 128
SUBLANES = 8
VMEM_LIMIT = 56 * 1024 * 1024

D_MODEL = 1024
D_RWKV = 512
RW_HEAD = 64
RW_COLS = 1792
D_S5 = 512
S5_GROUPS = 32
S5_CH = 16
S5_STATE = 64
S5_MODES = S5_GROUPS * S5_STATE
N_MEM = 256
XA_HEADS = 4
XA_HEAD = 256
PEER_HEADS = 8
PEER_NKEYS = 128
PEER_TOPK = 16
PEER_SEL = PEER_HEADS * PEER_TOPK
RMS_EPS = 1e-6
GN_EPS = 64e-5

RW_CHUNK = 128
RW_SEQS_PER_STEP = 1
S5_BLOCK = 256
PROJ_BLOCK = 512
XA_BLOCK = 256
SEL_BLOCK = 128
GATHER_BLOCK = 64


def _params(sem):
    return pltpu.CompilerParams(dimension_semantics=sem, vmem_limit_bytes=VMEM_LIMIT)


def _rms(x, g):
    ms = jnp.mean(x * x, axis=-1, keepdims=True)
    return x * lax.rsqrt(ms + RMS_EPS) * g


def _bdot(a, b):
    return jnp.dot(a.astype(BF16), b.astype(BF16), preferred_element_type=F32)


def _bdot_nt(a, b):
    return lax.dot_general(a.astype(BF16), b.astype(BF16), (((1,), (1,)), ((), ())),
                           preferred_element_type=F32)


def _sigmoid(x):
    return 1.0 / (1.0 + jnp.exp(-x))


def _softplus(x):
    return jnp.maximum(x, 0.0) + jnp.log(1.0 + jnp.exp(-jnp.abs(x)))


def _gelu(x):
    return 0.5 * x * (1.0 + jnp.tanh(math.sqrt(2.0 / math.pi) * (x + 0.044715 * (x * x * x))))


def _full(shape):
    n = len(shape)
    return pl.BlockSpec(shape, lambda *_: (0,) * n)


def _norm_proj_kernel(*refs, n_out):
    x_ref, g_ref = refs[0], refs[1]
    w_refs = refs[2:2 + n_out]
    o_refs = refs[2 + n_out:]
    h = _rms(x_ref[...], g_ref[...]).astype(BF16)
    for w_ref, o_ref in zip(w_refs, o_refs):
        o_ref[...] = jnp.dot(h, w_ref[...], preferred_element_type=F32).astype(o_ref.dtype)


def _norm_proj(x2, g, ws, out_dtypes, block):
    t, d = x2.shape
    block = min(block, t)
    in_specs = [pl.BlockSpec((block, d), lambda i: (i, 0)), _full((1, d))]
    in_specs += [_full(w.shape) for w in ws]
    out_specs = [pl.BlockSpec((block, w.shape[1]), lambda i: (i, 0)) for w in ws]
    out_shape = [jax.ShapeDtypeStruct((t, w.shape[1]), dt) for w, dt in zip(ws, out_dtypes)]
    return pl.pallas_call(
        functools.partial(_norm_proj_kernel, n_out=len(ws)),
        grid=(t // block,), in_specs=in_specs, out_specs=out_specs, out_shape=out_shape,
        compiler_params=_params(("parallel",)),
    )(x2, g.reshape(1, d), *ws)


def _seg_sum(x, mseg):
    hi = x.astype(BF16)
    lo = (x - hi.astype(F32)).astype(BF16)
    return (jnp.dot(hi, mseg, preferred_element_type=F32)
            + jnp.dot(lo, mseg, preferred_element_type=F32))


def _col_bcast(row):
    return jnp.broadcast_to(row, (LANES, LANES)).T


def _rwkv_kernel(*refs, first_layer):
    zprev, hst = refs[-2], refs[-1]

    @pl.when(pl.program_id(1) == 0)
    def _():
        zprev[...] = jnp.zeros_like(zprev)
        hst[...] = jnp.zeros_like(hst)

    for b in range(refs[0].shape[0]):
        _rwkv_chunk(refs, first_layer, b)


def _rwkv_chunk(refs, first_layer, b):
    if first_layer:
        (z_ref, mu_ref, w0_ref, w2_ref, a0_ref, a2_ref, g2_ref, kk_ref, ka_ref, rk_ref,
         lng_ref, lnb_ref, mseg_ref, y_ref, vf_out_ref, zprev, hst) = refs
    else:
        (z_ref, hv_ref, vf_ref, v0_ref, v2_ref, mu_ref, w0_ref, w2_ref, a0_ref, a2_ref, g2_ref,
         kk_ref, ka_ref, rk_ref, lng_ref, lnb_ref, mseg_ref, y_ref, zprev, hst) = refs
    L = RW_CHUNK
    n_pair = D_RWKV // LANES

    z = z_ref[b]
    row = lax.broadcasted_iota(jnp.int32, (L, 1), 0)
    zs = jnp.where(row == 0, zprev[b:b + 1, :], pltpu.roll(z, 1, 0))
    zprev[b:b + 1, :] = z[L - 1:L, :]
    z = z + (zs - z) * mu_ref[...]
    r = z[:, 0:512]
    k = z[:, 512:1024]
    v = z[:, 1024:1536]
    wa = z[:, 1536:1664]
    gd = z[:, 1664:1792]
    mseg = mseg_ref[...]

    wlin = w0_ref[...] + _bdot(jnp.tanh(wa), w2_ref[...])
    lw = -jnp.exp(-_softplus(-wlin) - 0.5)
    a = _sigmoid(a0_ref[...] + _bdot(wa, a2_ref[...]))
    g = _bdot(_sigmoid(gd), g2_ref[...])
    if first_layer:
        vf_out_ref[b] = v
    else:
        v = v + (vf_ref[b] - v) * _sigmoid(v0_ref[...] + _bdot(hv_ref[b], v2_ref[...]))
    kk = k * kk_ref[...]
    kk = kk / jnp.maximum(jnp.sqrt(_seg_sum(kk * kk, mseg)), 1e-12)
    k2 = k * (1.0 + (a - 1.0) * ka_ref[...])
    av = -kk
    bv = kk * a

    ti = lax.broadcasted_iota(jnp.int32, (L, L), 0)
    si = lax.broadcasted_iota(jnp.int32, (L, L), 1)
    tril = (ti >= si).astype(F32)
    cum = jnp.dot(tril, lw, preferred_element_type=F32, precision=lax.Precision.HIGHEST)
    mid = cum[L // 2 - 1:L // 2, :]
    cm = cum - mid
    ecum = jnp.exp(cm)
    einv = jnp.exp(-cm)
    rt = r * ecum
    kt = k2 * einv
    bt = bv * einv
    at = av * jnp.exp(cm - lw)
    p_end = jnp.exp(cum[L - 1:L, :])
    e_end = ecum[L - 1:L, :]
    e_mid = jnp.exp(mid)

    lane = lax.broadcasted_iota(jnp.int32, (1, LANES), 1)
    m0 = (lane < RW_HEAD).astype(F32)
    m1 = 1.0 - m0
    strict = ti > si
    incl = ti >= si
    bi = lax.broadcasted_iota(jnp.int32, (LANES, LANES), 0) < RW_HEAD
    bj = lax.broadcasted_iota(jnp.int32, (LANES, LANES), 1) < RW_HEAD
    bdmask = (bi == bj).astype(F32)
    zeros_ll = jnp.zeros((L, L), F32)

    ys = []
    for p in range(D_RWKV // LANES):
        sl = slice(LANES * p, LANES * (p + 1))
        A, B, K, R, V = at[:, sl], bt[:, sl], kt[:, sl], rt[:, sl], v[:, sl]
        btkt = jnp.concatenate([B.T, K.T], axis=1)
        h0 = hst[b * n_pair + p]
        h0m = h0 * _col_bcast(e_mid[:, sl])
        sc = _bdot(jnp.concatenate([A * m0, A * m1, R * m0, R * m1], axis=0), btkt)
        aab = [jnp.where(strict, sc[e * L:(e + 1) * L, 0:L], 0.0) for e in range(2)]
        aak = [jnp.where(strict, sc[e * L:(e + 1) * L, L:2 * L], 0.0) for e in range(2)]
        arb = [jnp.where(incl, sc[(2 + e) * L:(3 + e) * L, 0:L], 0.0) for e in range(2)]
        ark = [jnp.where(incl, sc[(2 + e) * L:(3 + e) * L, L:2 * L], 0.0) for e in range(2)]
        arh = _bdot(jnp.concatenate([A, R], axis=0), h0m)
        v01 = jnp.concatenate([V * m0, V * m1], axis=0)
        x = arh[0:L] + _bdot(jnp.concatenate(aak, axis=1), v01)
        pm = jnp.concatenate(aab, axis=1)
        n_fac = int(math.log2(L))
        for it in range(n_fac):
            x = x + _bdot(pm, jnp.concatenate([x * m0, x * m1], axis=0))
            if it + 1 < n_fac:
                pd = jnp.concatenate(
                    [jnp.concatenate([pm[:, 0:L], zeros_ll], axis=1),
                     jnp.concatenate([zeros_ll, pm[:, L:2 * L]], axis=1)], axis=0)
                pm = _bdot(pm, pd)
        u = x
        yp = arh[L:2 * L] + _bdot(jnp.concatenate(arb + ark, axis=1),
                                  jnp.concatenate([u * m0, u * m1, v01], axis=0))
        upd = _bdot(btkt, jnp.concatenate([u, V], axis=0))
        hst[b * n_pair + p] = (h0 * _col_bcast(p_end[:, sl]) + upd * _col_bcast(e_end[:, sl])) * bdmask
        ys.append(yp)
    y = jnp.concatenate(ys, axis=1)

    mean = _seg_sum(y, mseg) * (1.0 / RW_HEAD)
    d = y - mean
    var = _seg_sum(d * d, mseg) * (1.0 / RW_HEAD)
    yn = d * lax.rsqrt(var + GN_EPS) * lng_ref[...] + lnb_ref[...]
    bonus = _seg_sum(r * k2 * rk_ref[...], mseg) * v
    y_ref[b] = (yn + bonus) * g


def _pad_rows(w, rows, offset):
    out = jnp.zeros((rows, w.shape[1]), w.dtype)
    return out.at[offset:offset + w.shape[0]].set(w)


def _rwkv(z_rw, hv, v_first, prm, bsz, seq):
    L = RW_CHUNK
    first = v_first is None
    row = lambda a: a.reshape(1, -1).astype(F32)
    hid = jnp.arange(D_RWKV) // RW_HEAD
    mseg = (hid[:, None] == hid[None, :]).astype(BF16)
    w2p = _pad_rows(prm['w2'], LANES, 0).astype(BF16)
    a2p = _pad_rows(prm['a2'], LANES, 64).astype(BF16)
    common = [row(prm['mu']), row(prm['w0']), w2p, row(prm['a0']), a2p, prm['g2'].astype(BF16),
              row(prm['kk']), row(prm['ka']), row(prm['rk']), row(prm['lng']), row(prm['lnb']), mseg]
    per = RW_SEQS_PER_STEP if bsz % RW_SEQS_PER_STEP == 0 else 1
    tok = lambda w: pl.BlockSpec((per, L, w), lambda b, t: (b, t, 0))
    common_specs = [_full(c.shape) for c in common]
    y_shape = jax.ShapeDtypeStruct((bsz, seq, D_RWKV), F32)
    scratch = [pltpu.VMEM((per, RW_COLS), F32), pltpu.VMEM((per * D_RWKV // LANES, LANES, LANES), F32)]
    if first:
        args = [z_rw] + common
        in_specs = [tok(RW_COLS)] + common_specs
        out_shape = [y_shape, y_shape]
        out_specs = [tok(D_RWKV), tok(D_RWKV)]
    else:
        v2p = _pad_rows(prm['v2'], LANES, 0).astype(BF16)
        extra = [row(prm['v0']), v2p]
        args = [z_rw, hv, v_first] + extra + common
        in_specs = [tok(RW_COLS), tok(LANES), tok(D_RWKV)] + [_full(c.shape) for c in extra] + common_specs
        out_shape = [y_shape]
        out_specs = [tok(D_RWKV)]
    outs = pl.pallas_call(
        functools.partial(_rwkv_kernel, first_layer=first),
        grid=(bsz // per, seq // L), in_specs=in_specs, out_specs=out_specs, out_shape=out_shape,
        scratch_shapes=scratch, compiler_params=_params(("parallel", "arbitrary")),
    )(*args)
    return (outs[0], outs[1]) if first else (outs[0], v_first)


def _s5_kernel(u_ref, wb_ref, wc_ref, lpr_ref, lpi_ref, d_ref, gw_ref, gb_ref, og_ref, o_ref,
               car_re, car_im, xre, xim):
    tb = u_ref.shape[1]

    @pl.when(pl.program_id(1) == 0)
    def _():
        car_re[...] = jnp.zeros_like(car_re)
        car_im[...] = jnp.zeros_like(car_im)

    u = u_ref[0]
    bu = _bdot(u, wb_ref[...])
    xre[...] = bu[:, 0:S5_MODES]
    xim[...] = bu[:, S5_MODES:2 * S5_MODES]
    row = lax.broadcasted_iota(jnp.int32, (SUBLANES, 1), 0)

    def tile(i, carry):
        cr, ci = carry
        rs = pl.ds(pl.multiple_of(i * SUBLANES, SUBLANES), SUBLANES)
        br, bi = xre[rs, :], xim[rs, :]
        for dist in (1, 2, 4):
            keep = row >= dist
            sr = jnp.where(keep, pltpu.roll(br, dist, 0), 0.0)
            si = jnp.where(keep, pltpu.roll(bi, dist, 0), 0.0)
            lr = lpr_ref[dist - 1:dist, :]
            li = lpi_ref[dist - 1:dist, :]
            br, bi = br + lr * sr - li * si, bi + lr * si + li * sr
        pr, pi = lpr_ref[...], lpi_ref[...]
        xr = br + pr * cr - pi * ci
        xi = bi + pr * ci + pi * cr
        xre[rs, :] = xr
        xim[rs, :] = xi
        return xr[SUBLANES - 1:SUBLANES, :], xi[SUBLANES - 1:SUBLANES, :]

    cr, ci = lax.fori_loop(0, tb // SUBLANES, tile, (car_re[...], car_im[...]))
    car_re[...] = cr
    car_im[...] = ci
    wc = wc_ref[...]
    y = _bdot(xre[...], wc[0:S5_MODES]) + _bdot(xim[...], wc[S5_MODES:2 * S5_MODES])
    y = _gelu(y + d_ref[...] * u)
    y = y * _sigmoid(_bdot(y, gw_ref[...]) + gb_ref[...])
    o_ref[0] = _rms(y, og_ref[...])


def _s5_weights(a_re, a_im, log_dt, b_re, b_im, c_re, c_im):
    lam_re = jnp.minimum(a_re.astype(F32), -1e-4)
    lam_im = a_im.astype(F32)
    dt = jnp.exp(log_dt.astype(F32))[:, None]
    mag = jnp.exp(lam_re * dt)
    lb_re = mag * jnp.cos(lam_im * dt)
    lb_im = mag * jnp.sin(lam_im * dt)
    den = lam_re * lam_re + lam_im * lam_im
    c1_re = ((lb_re - 1.0) * lam_re + lb_im * lam_im) / den
    c1_im = (lb_im * lam_re - (lb_re - 1.0) * lam_im) / den
    br, bi = b_re.astype(F32), b_im.astype(F32)
    bb_re = c1_re[..., None] * br - c1_im[..., None] * bi
    bb_im = c1_re[..., None] * bi + c1_im[..., None] * br
    eye = jnp.eye(S5_GROUPS, dtype=F32)
    wb_re = jnp.einsum('gpc,gh->gchp', bb_re, eye).reshape(D_S5, S5_MODES)
    wb_im = jnp.einsum('gpc,gh->gchp', bb_im, eye).reshape(D_S5, S5_MODES)
    wb = jnp.concatenate([wb_re, wb_im], axis=1).astype(BF16)
    wc_re = jnp.einsum('gcp,gh->gphc', c_re.astype(F32), eye).reshape(S5_MODES, D_S5)
    wc_im = jnp.einsum('gcp,gh->gphc', c_im.astype(F32), eye).reshape(S5_MODES, D_S5)
    wc = jnp.concatenate([wc_re, -wc_im], axis=0).astype(BF16)
    pr, pi = [lb_re], [lb_im]
    for _ in range(SUBLANES - 1):
        pr, pi = pr + [pr[-1] * lb_re - pi[-1] * lb_im], pi + [pr[-1] * lb_im + pi[-1] * lb_re]
    lp_re = jnp.stack(pr).reshape(SUBLANES, S5_MODES)
    lp_im = jnp.stack(pi).reshape(SUBLANES, S5_MODES)
    return wb, wc, lp_re, lp_im


def _s5(u, prm, bsz, seq):
    tb = min(S5_BLOCK, seq)
    wb, wc, lp_re, lp_im = _s5_weights(prm['a_re'], prm['a_im'], prm['log_dt'], prm['b_re'], prm['b_im'],
                                       prm['c_re'], prm['c_im'])
    row = lambda a: a.reshape(1, -1).astype(F32)
    consts = [wb, wc, lp_re, lp_im, row(prm['d']), prm['glu_w'].astype(BF16), row(prm['glu_b']),
              row(prm['out_g'])]
    tok = pl.BlockSpec((1, tb, D_S5), lambda b, t: (b, t, 0))
    return pl.pallas_call(
        _s5_kernel, grid=(bsz, seq // tb),
        in_specs=[tok] + [_full(c.shape) for c in consts], out_specs=tok,
        out_shape=jax.ShapeDtypeStruct((bsz, seq, D_S5), F32),
        scratch_shapes=[pltpu.VMEM((1, S5_MODES), F32), pltpu.VMEM((1, S5_MODES), F32),
                        pltpu.VMEM((tb, S5_MODES), F32), pltpu.VMEM((tb, S5_MODES), F32)],
        compiler_params=_params(("parallel", "arbitrary")),
    )(u, *consts)


def _mix_xattn_kernel(x_ref, yr_ref, ys_ref, wo1_ref, wo2_ref, g_ref, wq_ref, k_ref, v_ref, wo_ref, o_ref):
    x1 = x_ref[0] + _bdot(yr_ref[0], wo1_ref[...]) + _bdot(ys_ref[0], wo2_ref[...])
    h = _rms(x1, g_ref[...])
    q = _bdot(h, wq_ref[...])
    km, vm = k_ref[0], v_ref[0]
    outs = []
    for hd in range(XA_HEADS):
        sl = slice(XA_HEAD * hd, XA_HEAD * (hd + 1))
        s = _bdot_nt(q[:, sl], km[:, sl]) * (XA_HEAD ** -0.5)
        s = s - jnp.max(s, axis=-1, keepdims=True)
        e = jnp.exp(s)
        p = e / jnp.sum(e, axis=-1, keepdims=True)
        outs.append(_bdot(p, vm[:, sl]))
    o = jnp.concatenate(outs, axis=1)
    o_ref[0] = x1 + _bdot(o, wo_ref[...])


def _mix_xattn(x, y_rw, y_s5, w_out, g, wq, kmem, vmem, wo, bsz, seq):
    tm = min(XA_BLOCK, seq)
    consts_a = [w_out[:D_RWKV].astype(BF16), w_out[D_RWKV:].astype(BF16), g.reshape(1, -1).astype(F32),
                wq.astype(BF16)]
    tok = lambda w: pl.BlockSpec((1, tm, w), lambda b, t: (b, t, 0))
    mem = pl.BlockSpec((1, N_MEM, D_MODEL), lambda b, t: (b, 0, 0))
    wo_b = wo.astype(BF16)
    return pl.pallas_call(
        _mix_xattn_kernel, grid=(bsz, seq // tm),
        in_specs=[tok(D_MODEL), tok(D_RWKV), tok(D_S5)] + [_full(c.shape) for c in consts_a]
                 + [mem, mem, _full(wo_b.shape)],
        out_specs=tok(D_MODEL), out_shape=jax.ShapeDtypeStruct((bsz, seq, D_MODEL), F32),
        compiler_params=_params(("parallel", "parallel")),
    )(x, y_rw, y_s5, *consts_a, kmem, vmem, wo_b)


def _top_rows(work, order, aux, val_ref, idx_ref):
    for it in range(PEER_TOPK):
        m = jnp.max(work, axis=0, keepdims=True)
        pos = jnp.min(jnp.where(work == m, order, jnp.inf), axis=0, keepdims=True)
        hit = order == pos
        val_ref[it:it + 1, :] = m
        if aux is None:
            idx_ref[it:it + 1, :] = pos
        else:
            idx_ref[it:it + 1, :] = jnp.sum(jnp.where(hit, aux, 0.0), axis=0, keepdims=True)
        work = jnp.where(hit, -jnp.inf, work)


_CAND_ROW_BLOCKS = [(0, PEER_TOPK), (1, SUBLANES), (2, SUBLANES), (3, SUBLANES)]
_CAND_COL_BLOCKS = [(0, PEER_TOPK, 4, 15), (1, SUBLANES, 4, 7), (2, SUBLANES, 4, 4)]
N_CAND = sum(n for _, n in _CAND_ROW_BLOCKS) + sum(n for _, n, _, _ in _CAND_COL_BLOCKS)


def _cand_consts(tm):
    flat, neg = [], []
    for a, nb in _CAND_ROW_BLOCKS:
        flat += [a * PEER_TOPK + b for b in range(nb)]
        neg += [0.0] * nb
    for b, na, lo, hi in _CAND_COL_BLOCKS:
        flat += [a * PEER_TOPK + b for a in range(na)]
        neg += [0.0 if lo <= a <= hi else -float('inf') for a in range(na)]
    flat = [f if n == 0.0 else 1000.0 + i for i, (f, n) in enumerate(zip(flat, neg))]
    col = lambda v: jnp.broadcast_to(jnp.asarray(v, F32)[:, None], (N_CAND, tm))
    return col(flat), col(neg)


def _cand_rows(row_vals, col_vals, combine):
    blocks = [combine(row_vals[a:a + 1, :], col_vals[0:nb, :]) for a, nb in _CAND_ROW_BLOCKS]
    blocks += [combine(row_vals[0:na, :], col_vals[b:b + 1, :]) for b, na, _, _ in _CAND_COL_BLOCKS]
    return jnp.concatenate(blocks, axis=0)


SEL_HEADS_PER_STEP = 4


def _peer_select_kernel(x_ref, g_ref, wq_ref, keys_ref, cflat_ref, cneg_ref, x8_ref, h8_ref, base_ref, shift_ref,
                        gate_ref, q3, idx_t, gate_t, *lists):
    tm = x_ref.shape[0]
    n = SEL_HEADS_PER_STEP
    s1, i1, s2, i2, top, eid = (lists[j * n:(j + 1) * n] for j in range(6))
    x = x_ref[...]
    h = _rms(x, g_ref[...])
    for r in range(D_MODEL // LANES):
        rows = pl.ds(r, tm, stride=D_MODEL // LANES)
        h8_ref[rows, :] = h[:, LANES * r:LANES * (r + 1)]
        x8_ref[rows, :] = x[:, LANES * r:LANES * (r + 1)]
    q = _bdot(h, wq_ref[...])
    for j in range(2 * PEER_HEADS):
        q3[j] = q[:, LANES * j:LANES * (j + 1)]
    iota_k = lax.broadcasted_iota(jnp.int32, (PEER_NKEYS, tm), 0).astype(F32)

    def heads(step, _):
        for u in range(SEL_HEADS_PER_STEP):
            hd = step * SEL_HEADS_PER_STEP + u
            sc1 = _bdot_nt(keys_ref[2 * hd], q3[2 * hd])
            sc2 = _bdot_nt(keys_ref[2 * hd + 1], q3[2 * hd + 1])
            _top_rows(sc1, iota_k, None, s1[u], i1[u])
            _top_rows(sc2, iota_k, None, s2[u], i2[u])
            cand = _cand_rows(s1[u][...], s2[u][...], lambda x, y: x + y) + cneg_ref[...]
            cidx = _cand_rows(i1[u][...], i2[u][...], lambda x, y: x * float(PEER_NKEYS) + y)
            _top_rows(cand, cflat_ref[...], cidx, top[u], eid[u])
            tv = top[u][...]
            e = jnp.exp(tv - jnp.max(tv, axis=0, keepdims=True))
            rs = pl.ds(pl.multiple_of(hd * PEER_TOPK, PEER_TOPK), PEER_TOPK)
            idx_t[rs, :] = eid[u][...]
            gate_t[rs, :] = e / jnp.sum(e, axis=0, keepdims=True)
        return 0

    lax.fori_loop(0, PEER_HEADS // SEL_HEADS_PER_STEP, heads, 0)
    e_t = idx_t[...].T
    pair = jnp.floor(e_t * 0.5)
    base_ref[...] = (pair * float(SUBLANES)).astype(jnp.int32)
    shift_ref[...] = ((e_t - 2.0 * pair) * 16.0).astype(jnp.int32)
    gate_ref[...] = gate_t[...].T


def _peer_select(x2, g, wq, keys):
    t = x2.shape[0]
    tm = SEL_BLOCK
    keys_b = keys.reshape(2 * PEER_HEADS, PEER_NKEYS, LANES).astype(BF16)
    wq_b = wq.astype(BF16)
    cflat, cneg = _cand_consts(tm)
    tokspec = lambda w: pl.BlockSpec((tm, w), lambda i: (i, 0))
    vm = lambda r: pltpu.VMEM((r, tm), F32)
    return pl.pallas_call(
        _peer_select_kernel, grid=(t // tm,),
        in_specs=[tokspec(D_MODEL), _full((1, D_MODEL)), _full(wq_b.shape), _full(keys_b.shape),
                  _full(cflat.shape), _full(cneg.shape)],
        out_specs=[pl.BlockSpec((tm * SUBLANES, LANES), lambda i: (i, 0))] * 2
                  + [tokspec(PEER_SEL), tokspec(PEER_SEL), tokspec(PEER_SEL)],
        out_shape=[jax.ShapeDtypeStruct((t * SUBLANES, LANES), F32)] * 2 + [
                   jax.ShapeDtypeStruct((t, PEER_SEL), jnp.int32),
                   jax.ShapeDtypeStruct((t, PEER_SEL), jnp.int32), jax.ShapeDtypeStruct((t, PEER_SEL), F32)],
        scratch_shapes=[pltpu.VMEM((2 * PEER_HEADS, tm, LANES), F32), vm(PEER_SEL), vm(PEER_SEL)]
                       + [vm(PEER_TOPK) for _ in range(6 * SEL_HEADS_PER_STEP)],
        compiler_params=_params(("parallel",)),
    )(x2, g.reshape(1, -1).astype(F32), wq_b, keys_b, cflat, cneg)


def _tile_table(tab):
    return tab.reshape(-1, D_MODEL // LANES, LANES)


def _pack_table(tab3):
    n = tab3.shape[0]
    bits = lax.bitcast_convert_type(tab3.astype(BF16), jnp.uint16).astype(U32)
    bits = bits.reshape(n // 2, 2, SUBLANES, LANES)
    packed = (bits[:, 0] << 16) | bits[:, 1]
    return lax.bitcast_convert_type(packed, jnp.int32).reshape(n // 2 * SUBLANES, LANES)


def _splat_into(src_ref, t, dst_ref, slot):
    tile = jnp.broadcast_to(src_ref[pl.ds(t, 1), :], (LANES, LANES)).T
    dst_ref[LANES * slot:LANES * (slot + 1), :] = tile


def _bcast_row(ref, row):
    return jnp.broadcast_to(ref[row:row + 1, :], (SUBLANES, LANES))


def _expert_tile(tab_ref, base, shift_splat, row):
    w = tab_ref[pl.ds(pl.multiple_of(base, SUBLANES), SUBLANES), :]
    return lax.bitcast_convert_type((w << _bcast_row(shift_splat, row)) & jnp.int32(-65536), F32)


def _token_rows(t):
    return pl.ds(pl.multiple_of(t * SUBLANES, SUBLANES), SUBLANES)


def _pipelined_tokens(tg, prep, work):
    prep(0, 0)

    def body(i, _):
        t0 = 2 * i
        prep(t0 + 1, 1)
        work(t0, 0)
        prep(jnp.minimum(t0 + 2, tg - 1), 0)
        work(t0 + 1, 1)
        return 0

    lax.fori_loop(0, tg // 2, body, 0)


def _table_spec(packed, layer, n_layers):
    rows = packed.shape[0] // n_layers
    return pl.BlockSpec((rows, LANES), lambda i: (layer, 0), pipeline_mode=pl.Buffered(1))


N_CHUNK = D_MODEL // LANES
CHUNK_STRIDE = PEER_SEL + SUBLANES


def _peer_u_kernel(base_ref, shift_ref, h_ref, gate_ref, tab_ref, eye_ref, c_ref, shift_splat, *planes):
    tg = gate_ref.shape[0]
    ones = jnp.ones((LANES, LANES), BF16)

    half = N_CHUNK // 2

    def prep(t, slot):
        _splat_into(shift_ref, t, shift_splat, slot)

    def gather(t, slot):
        ht = h_ref[_token_rows(t), :]
        for kx in range(PEER_SEL):
            prod = _expert_tile(tab_ref, base_ref[t, kx], shift_splat, LANES * slot + kx) * ht
            fold = prod + pltpu.roll(prod, half, 0)
            planes[slot][pl.ds(kx, half, stride=CHUNK_STRIDE), :] = fold[0:half, :]

    def finish(t, slot):
        plane = planes[slot]
        acc = plane[0:PEER_SEL, :]
        for r in range(1, half):
            acc = acc + plane[CHUNK_STRIDE * r:CHUNK_STRIDE * r + PEER_SEL, :]
        hi = acc.astype(BF16)
        lo = (acc - hi.astype(F32)).astype(BF16)
        tot = jnp.dot(hi, ones, preferred_element_type=F32) + jnp.dot(lo, ones, preferred_element_type=F32)
        score = jnp.sum(tot * eye_ref[...], axis=0, keepdims=True)
        c_ref[pl.ds(t, 1), :] = gate_ref[pl.ds(t, 1), :] * _gelu(score)

    planes[1][...] = jnp.zeros_like(planes[1])
    prep(0, 0)

    def body(i, _):
        t0 = 2 * i
        prep(t0 + 1, 1)
        gather(t0, 0)
        finish(jnp.maximum(t0 - 1, 0), 1)
        prep(jnp.minimum(t0 + 2, tg - 1), 0)
        gather(t0 + 1, 1)
        finish(t0, 0)
        return 0

    lax.fori_loop(0, tg // 2, body, 0)
    finish(tg - 1, 1)


def _peer_u(base, shift, h8, gate, tab, t, layer, n_layers):
    tg = GATHER_BLOCK
    eye = jnp.eye(LANES, dtype=F32)
    tokrow = pl.BlockSpec((tg, PEER_SEL), lambda i: (i, 0))
    return pl.pallas_call(
        _peer_u_kernel, grid=(t // tg,),
        in_specs=[pl.BlockSpec((tg, PEER_SEL), lambda i: (i, 0), memory_space=pltpu.SMEM), tokrow,
                  pl.BlockSpec((tg * SUBLANES, LANES), lambda i: (i, 0)), tokrow,
                  _table_spec(tab, layer, n_layers), _full(eye.shape)],
        out_specs=tokrow, out_shape=jax.ShapeDtypeStruct((t, PEER_SEL), F32),
        scratch_shapes=[pltpu.VMEM((2 * LANES, LANES), jnp.int32)]
                       + [pltpu.VMEM((N_CHUNK // 2 * CHUNK_STRIDE, LANES), F32)] * 2,
        compiler_params=_params(("parallel",)),
    )(base, shift, h8, gate, tab, eye)


def _peer_v_kernel(base_ref, shift_ref, c_ref, x_ref, tab_ref, _aliased_out, o_ref, shift_splat, c_splat, otile):
    tg = c_ref.shape[0]
    n_acc = 4

    def prep(t, slot):
        _splat_into(shift_ref, t, shift_splat, slot)
        _splat_into(c_ref, t, c_splat, slot)

    def work(t, slot):
        accs = [jnp.zeros((SUBLANES, LANES), F32) for _ in range(n_acc)]
        for kx in range(PEER_SEL):
            row = LANES * slot + kx
            accs[kx % n_acc] = accs[kx % n_acc] + (_bcast_row(c_splat, row)
                                                   * _expert_tile(tab_ref, base_ref[t, kx], shift_splat, row))
        rs = _token_rows(t)
        otile[rs, :] = x_ref[rs, :] + ((accs[0] + accs[1]) + (accs[2] + accs[3]))

    _pipelined_tokens(tg, prep, work)
    for r in range(N_CHUNK):
        o_ref[:, LANES * r:LANES * (r + 1)] = otile[pl.ds(r, tg, stride=N_CHUNK), :]


def _peer_v(base, shift, c, x8, x2, tab, t, layer, n_layers):
    tg = GATHER_BLOCK
    smem = pl.BlockSpec((tg, PEER_SEL), lambda i: (i, 0), memory_space=pltpu.SMEM)
    tokrow = pl.BlockSpec((tg, PEER_SEL), lambda i: (i, 0))
    tile = pl.BlockSpec((tg * SUBLANES, LANES), lambda i: (i, 0))
    return pl.pallas_call(
        _peer_v_kernel, grid=(t // tg,),
        in_specs=[smem, tokrow, tokrow, tile, _table_spec(tab, layer, n_layers),
                  pl.BlockSpec(memory_space=pl.ANY)],
        out_specs=pl.BlockSpec((tg, D_MODEL), lambda i: (i, 0)),
        out_shape=jax.ShapeDtypeStruct(x2.shape, F32),
        input_output_aliases={5: 0},
        scratch_shapes=[pltpu.VMEM((2 * LANES, LANES), jnp.int32), pltpu.VMEM((2 * LANES, LANES), F32),
                        pltpu.VMEM((tg * SUBLANES, LANES), F32)],
        compiler_params=_params(("parallel",)),
    )(base, shift, c, x8, tab, x2)


SC_WORKERS = 32
SC_LANES = 16
SC_ROWS = 32
SC_SHARE_PIPELINED = 42 / 64
SC_SHARE_LAST = 27 / 64
SEQ_GROUPS = 2
SC_GROUP = 8
SC_REGS = 32


def _peer_sc(eidx, gate, h8, x8, tab_u, tab_v, t_off):
    ts = eidx.shape[0]
    per_w = ts // SC_WORKERS
    n_chunk = PEER_SEL // SC_ROWS
    grp = SC_GROUP
    sel = grp * PEER_SEL
    n_pairs = grp * n_chunk // 2
    per_tile = LANES // SC_LANES
    n_blk = D_MODEL // (SC_REGS * SC_LANES)
    mesh = plsc.VectorSubcoreMesh(core_axis_name="c", subcore_axis_name="s")

    def piece(ref, lead, q):
        return ref[lead, q // per_tile, pl.ds(SC_LANES * (q % per_tile), SC_LANES)]

    def tok_piece(ref, tok, q):
        return ref[tok * N_CHUNK + q // per_tile, pl.ds(SC_LANES * (q % per_tile), SC_LANES)]

    def body(u_hbm, v_hbm, idx_hbm, gate_hbm, h_hbm, x_hbm, out_hbm,
             idx_v, c_v, h_v, acc_v, part_v, rows0, rows1, sem0, sem1):
        wid = lax.axis_index("s") * 2 + lax.axis_index("c")
        bufs, sems = (rows0, rows1), (sem0, sem1)
        lane = lax.iota(jnp.int32, SC_LANES)

        def stream(tab_hbm, compute):
            def gather(chunk, slot):
                start = pl.multiple_of(chunk * SC_ROWS, SC_ROWS)
                return pltpu.make_async_copy(tab_hbm.at[idx_v.at[pl.ds(start, SC_ROWS)]], bufs[slot], sems[slot])

            gather(0, 0).start()

            @pl.loop(0, n_pairs)
            def _(p):
                c0 = 2 * p
                gather(c0 + 1, 1).start()
                gather(c0, 0).wait()
                compute(c0, 0)
                gather(jnp.minimum(c0 + 2, 2 * n_pairs - 1), 0).start()
                gather(c0 + 1, 1).wait()
                compute(c0 + 1, 1)

            gather(0, 0).wait()

        def dots(chunk, slot):
            tok = chunk // n_chunk
            for blk in range(n_blk):
                where = [blk * SC_REGS + j for j in range(SC_REGS)]
                hs = [tok_piece(h_v, tok, q) for q in where]

                def row(k, carry):
                    parts = [None] * 4
                    for i, q in enumerate(where):
                        term = piece(bufs[slot], k, q) * hs[i]
                        parts[i % 4] = term if parts[i % 4] is None else parts[i % 4] + term
                    tot = (parts[0] + parts[1]) + (parts[2] + parts[3])
                    at = pl.ds(pl.multiple_of((chunk * SC_ROWS + k) * SC_LANES, SC_LANES), SC_LANES)
                    if blk == 0:
                        part_v[at] = tot
                    else:
                        part_v[at] = part_v[at] + tot
                    return carry

                lax.fori_loop(0, SC_ROWS, row, 0)

        def weigh(chunk, slot):
            tok = chunk // n_chunk
            for blk in range(n_blk):
                where = [blk * SC_REGS + j for j in range(SC_REGS)]

                def row(k, accs):
                    ck = plsc.load_gather(c_v, [jnp.full((SC_LANES,), chunk * SC_ROWS, jnp.int32) + k])
                    return tuple(a + ck * piece(bufs[slot], k, q) for a, q in zip(accs, where))

                accs = lax.fori_loop(0, SC_ROWS, row, tuple(tok_piece(acc_v, tok, q) for q in where))
                for a, q in zip(accs, where):
                    acc_v[tok * N_CHUNK + q // per_tile, pl.ds(SC_LANES * (q % per_tile), SC_LANES)] = a

        @pl.loop(0, per_w // grp)
        def _(g):
            t0 = wid * per_w + g * grp
            flat = pl.ds(pl.multiple_of(t0 * PEER_SEL, sel), sel)
            pltpu.sync_copy(idx_hbm.at[flat], idx_v)
            pltpu.sync_copy(gate_hbm.at[flat], c_v)
            tok_rows = grp * N_CHUNK
            src_rows = pl.ds(pl.multiple_of((t_off + t0) * N_CHUNK, tok_rows), tok_rows)
            pltpu.sync_copy(h_hbm.at[src_rows], h_v)
            pltpu.sync_copy(x_hbm.at[src_rows], acc_v)
            stream(u_hbm, dots)

            @pl.loop(0, sel // SC_LANES)
            def _(m):
                base = (m * SC_LANES + lane) * SC_LANES
                score = plsc.load_gather(part_v, [base])
                for l in range(1, SC_LANES):
                    score = score + plsc.load_gather(part_v, [base + l])
                z = math.sqrt(2.0 / math.pi) * (score + 0.044715 * (score * score * score))
                tanh_z = 1.0 - 2.0 / (jnp.exp(2.0 * z) + 1.0)
                at = pl.ds(pl.multiple_of(m * SC_LANES, SC_LANES), SC_LANES)
                c_v[at] = c_v[at] * (0.5 * score * (1.0 + tanh_z))

            stream(v_hbm, weigh)
            pltpu.sync_copy(acc_v, out_hbm.at[pl.ds(pl.multiple_of(t0 * N_CHUNK, tok_rows), tok_rows)])

    tile = lambda n: pltpu.VMEM((n, N_CHUNK, LANES), F32)
    tok_tiles = pltpu.VMEM((grp * N_CHUNK, LANES), F32)
    return pl.kernel(
        body, mesh=mesh, out_type=jax.ShapeDtypeStruct((ts * N_CHUNK, LANES), F32),
        scratch_types=[pltpu.VMEM((sel,), jnp.int32), pltpu.VMEM((sel,), F32), tok_tiles, tok_tiles,
                       pltpu.VMEM((sel * SC_LANES,), F32), tile(SC_ROWS), tile(SC_ROWS),
                       pltpu.SemaphoreType.DMA, pltpu.SemaphoreType.DMA],
        compiler_params=pltpu.CompilerParams(needs_layout_passes=False, use_tc_tiling_on_sc=True),
    )(tab_u, tab_v, eidx.reshape(-1), gate.reshape(-1), h8, x8)


def _final_norm_kernel(x_ref, g_ref, o_ref):
    o_ref[...] = _rms(x_ref[...], g_ref[...])


def _final_norm(x2, g):
    t, d = x2.shape
    tm = min(PROJ_BLOCK, t)
    spec = pl.BlockSpec((tm, d), lambda i: (i, 0))
    return pl.pallas_call(
        _final_norm_kernel, grid=(t // tm,), in_specs=[spec, _full((1, d))], out_specs=spec,
        out_shape=jax.ShapeDtypeStruct((t, d), F32), compiler_params=_params(("parallel",)),
    )(x2, g.reshape(1, d).astype(F32))


def kernel(x, mem, norm_mix, w_in, rw_mu, rw_w0, rw_w2, rw_a0, rw_a2, rw_g2, rw_kk, rw_ka, rw_rk, rw_v0, rw_v1, rw_v2, rw_lnx_g, rw_lnx_b, s5_a_re, s5_a_im, s5_log_dt, s5_b_re, s5_b_im, s5_c_re, s5_c_im, s5_d, s5_glu_w, s5_glu_b, s5_out_g, w_out, norm_xa, norm_mem, xa_wq, xa_wk, xa_wv, xa_wo, norm_ffn, peer_wq, peer_keys, peer_u, peer_v, norm_final):
    bsz, seq, d = x.shape
    depth = w_in.shape[0]

    def layer(l, xg, memg, v_first, tabs, sc_share):
        nb = xg.shape[0]
        t = nb * seq
        x2 = xg.reshape(t, d)
        ws = [w_in[l][:, :RW_COLS].astype(BF16), w_in[l][:, RW_COLS:].astype(BF16)]
        if l > 0:
            ws.append(_pad_rows(rw_v1[l - 1].T, LANES, 0).T.astype(BF16))
        outs = _norm_proj(x2, norm_mix[l], ws, [F32] * len(ws), PROJ_BLOCK)
        z_rw = outs[0].reshape(nb, seq, RW_COLS)
        u_s5 = outs[1].reshape(nb, seq, D_S5)
        hv = outs[2].reshape(nb, seq, LANES) if l > 0 else None
        rw_prm = dict(mu=rw_mu[l], w0=rw_w0[l], w2=rw_w2[l], a0=rw_a0[l], a2=rw_a2[l], g2=rw_g2[l],
                      kk=rw_kk[l], ka=rw_ka[l], rk=rw_rk[l], lng=rw_lnx_g[l], lnb=rw_lnx_b[l])
        if l > 0:
            rw_prm.update(v0=rw_v0[l - 1], v2=rw_v2[l - 1])
        y_rw, v_first = _rwkv(z_rw, hv, v_first, rw_prm, nb, seq)
        s5_prm = dict(a_re=s5_a_re[l], a_im=s5_a_im[l], log_dt=s5_log_dt[l], b_re=s5_b_re[l], b_im=s5_b_im[l],
                      c_re=s5_c_re[l], c_im=s5_c_im[l], d=s5_d[l], glu_w=s5_glu_w[l], glu_b=s5_glu_b[l],
                      out_g=s5_out_g[l])
        y_s5 = _s5(u_s5, s5_prm, nb, seq)
        kv = _norm_proj(memg.reshape(nb * N_MEM, d), norm_mem[l],
                        [xa_wk[l].astype(BF16), xa_wv[l].astype(BF16)], [BF16, BF16], PROJ_BLOCK)
        kmem = kv[0].reshape(nb, N_MEM, d)
        vmem = kv[1].reshape(nb, N_MEM, d)
        xg = _mix_xattn(xg, y_rw, y_s5, w_out[l], norm_xa[l], xa_wq[l], kmem, vmem, xa_wo[l], nb, seq)
        x2 = xg.reshape(t, d)
        x8, h8, base, shift, gate = _peer_select(x2, norm_ffn[l], peer_wq[l], peer_keys[l])
        u3, v3, u_packed, v_packed = tabs
        n_sc = int(t * sc_share)
        assert n_sc % (SC_WORKERS * SC_GROUP) == 0 and (t - n_sc) % GATHER_BLOCK == 0
        t_tc = t - n_sc
        n_exp = peer_u.shape[1]
        eidx_sc = (base[t_tc:] >> 2) + (shift[t_tc:] >> 4) + l * n_exp
        x_sc = _peer_sc(eidx_sc, gate[t_tc:], h8, x8, u3, v3, t_tc)
        c = _peer_u(base, shift, h8, gate, u_packed, t_tc, l, depth)
        x_new = _peer_v(base, shift, c, x8, x2, v_packed, t_tc, l, depth)
        xg = lax.dynamic_update_slice(x_new, x_sc.reshape(n_sc, d), (t_tc, 0)).reshape(nb, seq, d)
        return xg, v_first

    nb = bsz // SEQ_GROUPS
    xs = [x[g * nb:(g + 1) * nb] for g in range(SEQ_GROUPS)]
    mems = [mem[g * nb:(g + 1) * nb] for g in range(SEQ_GROUPS)]
    v_firsts = [None] * SEQ_GROUPS
    u3, v3 = _tile_table(peer_u), _tile_table(peer_v)
    tabs = (u3, v3, _pack_table(u3), _pack_table(v3))
    for l in range(depth):
        for g in range(SEQ_GROUPS):
            last = l == depth - 1 and g == SEQ_GROUPS - 1
            share = SC_SHARE_LAST if last else SC_SHARE_PIPELINED
            xs[g], v_firsts[g] = layer(l, xs[g], mems[g], v_firsts[g], tabs, share)
    outs = [_final_norm(xg.reshape(nb * seq, d), norm_final).reshape(nb, seq, d) for xg in xs]
    return jnp.concatenate(outs, axis=0)
```

```python
import functools
import math

import jax
import jax.numpy as jnp
from jax import lax
from jax.experimental import pallas as pl
from jax.experimental.pallas import tpu as pltpu
from jax.experimental.pallas import tpu_sc as plsc

F32 = jnp.float32
BF16 = jnp.bfloat16
U32 = jnp.uint32

LANES = 128
SUBLANES = 8
VMEM_LIMIT = 56 * 1024 * 1024

D_MODEL = 1024
D_RWKV = 512
RW_HEAD = 64
RW_COLS = 1792
D_S5 = 512
S5_GROUPS = 32
S5_CH = 16
S5_STATE = 64
S5_MODES = S5_GROUPS * S5_STATE
N_MEM = 256
XA_HEADS = 4
XA_HEAD = 256
PEER_HEADS = 8
PEER_NKEYS = 128
PEER_TOPK = 16
PEER_SEL = PEER_HEADS * PEER_TOPK
RMS_EPS = 1e-6
GN_EPS = 64e-5

RW_CHUNK = 128
RW_SEQS_PER_STEP = 1
S5_BLOCK = 256
PROJ_BLOCK = 512
XA_BLOCK = 256
SEL_BLOCK = 128
GATHER_BLOCK = 64


def _params(sem):
    return pltpu.CompilerParams(dimension_semantics=sem, vmem_limit_bytes=VMEM_LIMIT)


def _rms(x, g):
    ms = jnp.mean(x * x, axis=-1, keepdims=True)
    return x * lax.rsqrt(ms + RMS_EPS) * g


def _bdot(a, b):
    return jnp.dot(a.astype(BF16), b.astype(BF16), preferred_element_type=F32)


def _bdot_nt(a, b):
    return lax.dot_general(a.astype(BF16), b.astype(BF16), (((1,), (1,)), ((), ())),
                           preferred_element_type=F32)


def _sigmoid(x):
    return 1.0 / (1.0 + jnp.exp(-x))


def _softplus(x):
    return jnp.maximum(x, 0.0) + jnp.log(1.0 + jnp.exp(-jnp.abs(x)))


def _gelu(x):
    return 0.5 * x * (1.0 + jnp.tanh(math.sqrt(2.0 / math.pi) * (x + 0.044715 * (x * x * x))))


def _full(shape):
    n = len(shape)
    return pl.BlockSpec(shape, lambda *_: (0,) * n)


def _norm_proj_kernel(*refs, n_out):
    x_ref, g_ref = refs[0], refs[1]
    w_refs = refs[2:2 + n_out]
    o_refs = refs[2 + n_out:]
    h = _rms(x_ref[...], g_ref[...]).astype(BF16)
    for w_ref, o_ref in zip(w_refs, o_refs):
        o_ref[...] = jnp.dot(h, w_ref[...], preferred_element_type=F32).astype(o_ref.dtype)


def _norm_proj(x2, g, ws, out_dtypes, block):
    t, d = x2.shape
    block = min(block, t)
    in_specs = [pl.BlockSpec((block, d), lambda i: (i, 0)), _full((1, d))]
    in_specs += [_full(w.shape) for w in ws]
    out_specs = [pl.BlockSpec((block, w.shape[1]), lambda i: (i, 0)) for w in ws]
    out_shape = [jax.ShapeDtypeStruct((t, w.shape[1]), dt) for w, dt in zip(ws, out_dtypes)]
    return pl.pallas_call(
        functools.partial(_norm_proj_kernel, n_out=len(ws)),
        grid=(t // block,), in_specs=in_specs, out_specs=out_specs, out_shape=out_shape,
        compiler_params=_params(("parallel",)),
    )(x2, g.reshape(1, d), *ws)


def _seg_sum(x, mseg):
    hi = x.astype(BF16)
    lo = (x - hi.astype(F32)).astype(BF16)
    return (jnp.dot(hi, mseg, preferred_element_type=F32)
            + jnp.dot(lo, mseg, preferred_element_type=F32))


def _col_bcast(row):
    return jnp.broadcast_to(row, (LANES, LANES)).T


def _rwkv_kernel(*refs, first_layer):
    zprev, hst = refs[-2], refs[-1]

    @pl.when(pl.program_id(1) == 0)
    def _():
        zprev[...] = jnp.zeros_like(zprev)
        hst[...] = jnp.zeros_like(hst)

    for b in range(refs[0].shape[0]):
        _rwkv_chunk(refs, first_layer, b)


def _rwkv_chunk(refs, first_layer, b):
    if first_layer:
        (z_ref, mu_ref, w0_ref, w2_ref, a0_ref, a2_ref, g2_ref, kk_ref, ka_ref, rk_ref,
         lng_ref, lnb_ref, mseg_ref, y_ref, vf_out_ref, zprev, hst) = refs
    else:
        (z_ref, hv_ref, vf_ref, v0_ref, v2_ref, mu_ref, w0_ref, w2_ref, a0_ref, a2_ref, g2_ref,
         kk_ref, ka_ref, rk_ref, lng_ref, lnb_ref, mseg_ref, y_ref, zprev, hst) = refs
    L = RW_CHUNK
    n_pair = D_RWKV // LANES

    z = z_ref[b]
    row = lax.broadcasted_iota(jnp.int32, (L, 1), 0)
    zs = jnp.where(row == 0, zprev[b:b + 1, :], pltpu.roll(z, 1, 0))
    zprev[b:b + 1, :] = z[L - 1:L, :]
    z = z + (zs - z) * mu_ref[...]
    r = z[:, 0:512]
    k = z[:, 512:1024]
    v = z[:, 1024:1536]
    wa = z[:, 1536:1664]
    gd = z[:, 1664:1792]
    mseg = mseg_ref[...]

    wlin = w0_ref[...] + _bdot(jnp.tanh(wa), w2_ref[...])
    lw = -jnp.exp(-_softplus(-wlin) - 0.5)
    a = _sigmoid(a0_ref[...] + _bdot(wa, a2_ref[...]))
    g = _bdot(_sigmoid(gd), g2_ref[...])
    if first_layer:
        vf_out_ref[b] = v
    else:
        v = v + (vf_ref[b] - v) * _sigmoid(v0_ref[...] + _bdot(hv_ref[b], v2_ref[...]))
    kk = k * kk_ref[...]
    kk = kk / jnp.maximum(jnp.sqrt(_seg_sum(kk * kk, mseg)), 1e-12)
    k2 = k * (1.0 + (a - 1.0) * ka_ref[...])
    av = -kk
    bv = kk * a

    ti = lax.broadcasted_iota(jnp.int32, (L, L), 0)
    si = lax.broadcasted_iota(jnp.int32, (L, L), 1)
    tril = (ti >= si).astype(F32)
    cum = jnp.dot(tril, lw, preferred_element_type=F32, precision=lax.Precision.HIGHEST)
    mid = cum[L // 2 - 1:L // 2, :]
    cm = cum - mid
    ecum = jnp.exp(cm)
    einv = jnp.exp(-cm)
    rt = r * ecum
    kt = k2 * einv
    bt = bv * einv
    at = av * jnp.exp(cm - lw)
    p_end = jnp.exp(cum[L - 1:L, :])
    e_end = ecum[L - 1:L, :]
    e_mid = jnp.exp(mid)

    lane = lax.broadcasted_iota(jnp.int32, (1, LANES), 1)
    m0 = (lane < RW_HEAD).astype(F32)
    m1 = 1.0 - m0
    strict = ti > si
    incl = ti >= si
    bi = lax.broadcasted_iota(jnp.int32, (LANES, LANES), 0) < RW_HEAD
    bj = lax.broadcasted_iota(jnp.int32, (LANES, LANES), 1) < RW_HEAD
    bdmask = (bi == bj).astype(F32)
    zeros_ll = jnp.zeros((L, L), F32)

    ys = []
    for p in range(D_RWKV // LANES):
        sl = slice(LANES * p, LANES * (p + 1))
        A, B, K, R, V = at[:, sl], bt[:, sl], kt[:, sl], rt[:, sl], v[:, sl]
        btkt = jnp.concatenate([B.T, K.T], axis=1)
        h0 = hst[b * n_pair + p]
        h0m = h0 * _col_bcast(e_mid[:, sl])
        sc = _bdot(jnp.concatenate([A * m0, A * m1, R * m0, R * m1], axis=0), btkt)
        aab = [jnp.where(strict, sc[e * L:(e + 1) * L, 0:L], 0.0) for e in range(2)]
        aak = [jnp.where(strict, sc[e * L:(e + 1) * L, L:2 * L], 0.0) for e in range(2)]
        arb = [jnp.where(incl, sc[(2 + e) * L:(3 + e) * L, 0:L], 0.0) for e in range(2)]
        ark = [jnp.where(incl, sc[(2 + e) * L:(3 + e) * L, L:2 * L], 0.0) for e in range(2)]
        arh = _bdot(jnp.concatenate([A, R], axis=0), h0m)
        v01 = jnp.concatenate([V * m0, V * m1], axis=0)
        x = arh[0:L] + _bdot(jnp.concatenate(aak, axis=1), v01)
        pm = jnp.concatenate(aab, axis=1)
        n_fac = int(math.log2(L))
        for it in range(n_fac):
            x = x + _bdot(pm, jnp.concatenate([x * m0, x * m1], axis=0))
            if it + 1 < n_fac:
                pd = jnp.concatenate(
                    [jnp.concatenate([pm[:, 0:L], zeros_ll], axis=1),
                     jnp.concatenate([zeros_ll, pm[:, L:2 * L]], axis=1)], axis=0)
                pm = _bdot(pm, pd)
        u = x
        yp = arh[L:2 * L] + _bdot(jnp.concatenate(arb + ark, axis=1),
                                  jnp.concatenate([u * m0, u * m1, v01], axis=0))
        upd = _bdot(btkt, jnp.concatenate([u, V], axis=0))
        hst[b * n_pair + p] = (h0 * _col_bcast(p_end[:, sl]) + upd * _col_bcast(e_end[:, sl])) * bdmask
        ys.append(yp)
    y = jnp.concatenate(ys, axis=1)

    mean = _seg_sum(y, mseg) * (1.0 / RW_HEAD)
    d = y - mean
    var = _seg_sum(d * d, mseg) * (1.0 / RW_HEAD)
    yn = d * lax.rsqrt(var + GN_EPS) * lng_ref[...] + lnb_ref[...]
    bonus = _seg_sum(r * k2 * rk_ref[...], mseg) * v
    y_ref[b] = (yn + bonus) * g


def _pad_rows(w, rows, offset):
    out = jnp.zeros((rows, w.shape[1]), w.dtype)
    return out.at[offset:offset + w.shape[0]].set(w)


def _rwkv(z_rw, hv, v_first, prm, bsz, seq):
    L = RW_CHUNK
    first = v_first is None
    row = lambda a: a.reshape(1, -1).astype(F32)
    hid = jnp.arange(D_RWKV) // RW_HEAD
    mseg = (hid[:, None] == hid[None, :]).astype(BF16)
    w2p = _pad_rows(prm['w2'], LANES, 0).astype(BF16)
    a2p = _pad_rows(prm['a2'], LANES, 64).astype(BF16)
    common = [row(prm['mu']), row(prm['w0']), w2p, row(prm['a0']), a2p, prm['g2'].astype(BF16),
              row(prm['kk']), row(prm['ka']), row(prm['rk']), row(prm['lng']), row(prm['lnb']), mseg]
    per = RW_SEQS_PER_STEP if bsz % RW_SEQS_PER_STEP == 0 else 1
    tok = lambda w: pl.BlockSpec((per, L, w), lambda b, t: (b, t, 0))
    common_specs = [_full(c.shape) for c in common]
    y_shape = jax.ShapeDtypeStruct((bsz, seq, D_RWKV), F32)
    scratch = [pltpu.VMEM((per, RW_COLS), F32), pltpu.VMEM((per * D_RWKV // LANES, LANES, LANES), F32)]
    if first:
        args = [z_rw] + common
        in_specs = [tok(RW_COLS)] + common_specs
        out_shape = [y_shape, y_shape]
        out_specs = [tok(D_RWKV), tok(D_RWKV)]
    else:
        v2p = _pad_rows(prm['v2'], LANES, 0).astype(BF16)
        extra = [row(prm['v0']), v2p]
        args = [z_rw, hv, v_first] + extra + common
        in_specs = [tok(RW_COLS), tok(LANES), tok(D_RWKV)] + [_full(c.shape) for c in extra] + common_specs
        out_shape = [y_shape]
        out_specs = [tok(D_RWKV)]
    outs = pl.pallas_call(
        functools.partial(_rwkv_kernel, first_layer=first),
        grid=(bsz // per, seq // L), in_specs=in_specs, out_specs=out_specs, out_shape=out_shape,
        scratch_shapes=scratch, compiler_params=_params(("parallel", "arbitrary")),
    )(*args)
    return (outs[0], outs[1]) if first else (outs[0], v_first)


def _s5_kernel(u_ref, wb_ref, wc_ref, lpr_ref, lpi_ref, d_ref, gw_ref, gb_ref, og_ref, o_ref,
               car_re, car_im, xre, xim):
    tb = u_ref.shape[1]

    @pl.when(pl.program_id(1) == 0)
    def _():
        car_re[...] = jnp.zeros_like(car_re)
        car_im[...] = jnp.zeros_like(car_im)

    u = u_ref[0]
    bu = _bdot(u, wb_ref[...])
    xre[...] = bu[:, 0:S5_MODES]
    xim[...] = bu[:, S5_MODES:2 * S5_MODES]
    row = lax.broadcasted_iota(jnp.int32, (SUBLANES, 1), 0)

    def tile(i, carry):
        cr, ci = carry
        rs = pl.ds(pl.multiple_of(i * SUBLANES, SUBLANES), SUBLANES)
        br, bi = xre[rs, :], xim[rs, :]
        for dist in (1, 2, 4):
            keep = row >= dist
            sr = jnp.where(keep, pltpu.roll(br, dist, 0), 0.0)
            si = jnp.where(keep, pltpu.roll(bi, dist, 0), 0.0)
            lr = lpr_ref[dist - 1:dist, :]
            li = lpi_ref[dist - 1:dist, :]
            br, bi = br + lr * sr - li * si, bi + lr * si + li * sr
        pr, pi = lpr_ref[...], lpi_ref[...]
        xr = br + pr * cr - pi * ci
        xi = bi + pr * ci + pi * cr
        xre[rs, :] = xr
        xim[rs, :] = xi
        return xr[SUBLANES - 1:SUBLANES, :], xi[SUBLANES - 1:SUBLANES, :]

    cr, ci = lax.fori_loop(0, tb // SUBLANES, tile, (car_re[...], car_im[...]))
    car_re[...] = cr
    car_im[...] = ci
    wc = wc_ref[...]
    y = _bdot(xre[...], wc[0:S5_MODES]) + _bdot(xim[...], wc[S5_MODES:2 * S5_MODES])
    y = _gelu(y + d_ref[...] * u)
    y = y * _sigmoid(_bdot(y, gw_ref[...]) + gb_ref[...])
    o_ref[0] = _rms(y, og_ref[...])


def _s5_weights(a_re, a_im, log_dt, b_re, b_im, c_re, c_im):
    lam_re = jnp.minimum(a_re.astype(F32), -1e-4)
    lam_im = a_im.astype(F32)
    dt = jnp.exp(log_dt.astype(F32))[:, None]
    mag = jnp.exp(lam_re * dt)
    lb_re = mag * jnp.cos(lam_im * dt)
    lb_im = mag * jnp.sin(lam_im * dt)
    den = lam_re * lam_re + lam_im * lam_im
    c1_re = ((lb_re - 1.0) * lam_re + lb_im * lam_im) / den
    c1_im = (lb_im * lam_re - (lb_re - 1.0) * lam_im) / den
    br, bi = b_re.astype(F32), b_im.astype(F32)
    bb_re = c1_re[..., None] * br - c1_im[..., None] * bi
    bb_im = c1_re[..., None] * bi + c1_im[..., None] * br
    eye = jnp.eye(S5_GROUPS, dtype=F32)
    wb_re = jnp.einsum('gpc,gh->gchp', bb_re, eye).reshape(D_S5, S5_MODES)
    wb_im = jnp.einsum('gpc,gh->gchp', bb_im, eye).reshape(D_S5, S5_MODES)
    wb = jnp.concatenate([wb_re, wb_im], axis=1).astype(BF16)
    wc_re = jnp.einsum('gcp,gh->gphc', c_re.astype(F32), eye).reshape(S5_MODES, D_S5)
    wc_im = jnp.einsum('gcp,gh->gphc', c_im.astype(F32), eye).reshape(S5_MODES, D_S5)
    wc = jnp.concatenate([wc_re, -wc_im], axis=0).astype(BF16)
    pr, pi = [lb_re], [lb_im]
    for _ in range(SUBLANES - 1):
        pr, pi = pr + [pr[-1] * lb_re - pi[-1] * lb_im], pi + [pr[-1] * lb_im + pi[-1] * lb_re]
    lp_re = jnp.stack(pr).reshape(SUBLANES, S5_MODES)
    lp_im = jnp.stack(pi).reshape(SUBLANES, S5_MODES)
    return wb, wc, lp_re, lp_im


def _s5(u, prm, bsz, seq):
    tb = min(S5_BLOCK, seq)
    wb, wc, lp_re, lp_im = _s5_weights(prm['a_re'], prm['a_im'], prm['log_dt'], prm['b_re'], prm['b_im'],
                                       prm['c_re'], prm['c_im'])
    row = lambda a: a.reshape(1, -1).astype(F32)
    consts = [wb, wc, lp_re, lp_im, row(prm['d']), prm['glu_w'].astype(BF16), row(prm['glu_b']),
              row(prm['out_g'])]
    tok = pl.BlockSpec((1, tb, D_S5), lambda b, t: (b, t, 0))
    return pl.pallas_call(
        _s5_kernel, grid=(bsz, seq // tb),
        in_specs=[tok] + [_full(c.shape) for c in consts], out_specs=tok,
        out_shape=jax.ShapeDtypeStruct((bsz, seq, D_S5), F32),
        scratch_shapes=[pltpu.VMEM((1, S5_MODES), F32), pltpu.VMEM((1, S5_MODES), F32),
                        pltpu.VMEM((tb, S5_MODES), F32), pltpu.VMEM((tb, S5_MODES), F32)],
        compiler_params=_params(("parallel", "arbitrary")),
    )(u, *consts)


def _mix_xattn_kernel(x_ref, yr_ref, ys_ref, wo1_ref, wo2_ref, g_ref, wq_ref, k_ref, v_ref, wo_ref, o_ref):
    x1 = x_ref[0] + _bdot(yr_ref[0], wo1_ref[...]) + _bdot(ys_ref[0], wo2_ref[...])
    h = _rms(x1, g_ref[...])
    q = _bdot(h, wq_ref[...])
    km, vm = k_ref[0], v_ref[0]
    outs = []
    for hd in range(XA_HEADS):
        sl = slice(XA_HEAD * hd, XA_HEAD * (hd + 1))
        s = _bdot_nt(q[:, sl], km[:, sl]) * (XA_HEAD ** -0.5)
        s = s - jnp.max(s, axis=-1, keepdims=True)
        e = jnp.exp(s)
        p = e / jnp.sum(e, axis=-1, keepdims=True)
        outs.append(_bdot(p, vm[:, sl]))
    o = jnp.concatenate(outs, axis=1)
    o_ref[0] = x1 + _bdot(o, wo_ref[...])


def _mix_xattn(x, y_rw, y_s5, w_out, g, wq, kmem, vmem, wo, bsz, seq):
    tm = min(XA_BLOCK, seq)
    consts_a = [w_out[:D_RWKV].astype(BF16), w_out[D_RWKV:].astype(BF16), g.reshape(1, -1).astype(F32),
                wq.astype(BF16)]
    tok = lambda w: pl.BlockSpec((1, tm, w), lambda b, t: (b, t, 0))
    mem = pl.BlockSpec((1, N_MEM, D_MODEL), lambda b, t: (b, 0, 0))
    wo_b = wo.astype(BF16)
    return pl.pallas_call(
        _mix_xattn_kernel, grid=(bsz, seq // tm),
        in_specs=[tok(D_MODEL), tok(D_RWKV), tok(D_S5)] + [_full(c.shape) for c in consts_a]
                 + [mem, mem, _full(wo_b.shape)],
        out_specs=tok(D_MODEL), out_shape=jax.ShapeDtypeStruct((bsz, seq, D_MODEL), F32),
        compiler_params=_params(("parallel", "parallel")),
    )(x, y_rw, y_s5, *consts_a, kmem, vmem, wo_b)


def _top_rows(work, order, aux, val_ref, idx_ref):
    for it in range(PEER_TOPK):
        m = jnp.max(work, axis=0, keepdims=True)
        pos = jnp.min(jnp.where(work == m, order, jnp.inf), axis=0, keepdims=True)
        hit = order == pos
        val_ref[it:it + 1, :] = m
        if aux is None:
            idx_ref[it:it + 1, :] = pos
        else:
            idx_ref[it:it + 1, :] = jnp.sum(jnp.where(hit, aux, 0.0), axis=0, keepdims=True)
        work = jnp.where(hit, -jnp.inf, work)


_CAND_ROW_BLOCKS = [(0, PEER_TOPK), (1, SUBLANES), (2, SUBLANES), (3, SUBLANES)]
_CAND_COL_BLOCKS = [(0, PEER_TOPK, 4, 15), (1, SUBLANES, 4, 7), (2, SUBLANES, 4, 4)]
N_CAND = sum(n for _, n in _CAND_ROW_BLOCKS) + sum(n for _, n, _, _ in _CAND_COL_BLOCKS)


def _cand_consts(tm):
    flat, neg = [], []
    for a, nb in _CAND_ROW_BLOCKS:
        flat += [a * PEER_TOPK + b for b in range(nb)]
        neg += [0.0] * nb
    for b, na, lo, hi in _CAND_COL_BLOCKS:
        flat += [a * PEER_TOPK + b for a in range(na)]
        neg += [0.0 if lo <= a <= hi else -float('inf') for a in range(na)]
    flat = [f if n == 0.0 else 1000.0 + i for i, (f, n) in enumerate(zip(flat, neg))]
    col = lambda v: jnp.broadcast_to(jnp.asarray(v, F32)[:, None], (N_CAND, tm))
    return col(flat), col(neg)


def _cand_rows(row_vals, col_vals, combine):
    blocks = [combine(row_vals[a:a + 1, :], col_vals[0:nb, :]) for a, nb in _CAND_ROW_BLOCKS]
    blocks += [combine(row_vals[0:na, :], col_vals[b:b + 1, :]) for b, na, _, _ in _CAND_COL_BLOCKS]
    return jnp.concatenate(blocks, axis=0)


SEL_HEADS_PER_STEP = 4


def _peer_select_kernel(x_ref, g_ref, wq_ref, keys_ref, cflat_ref, cneg_ref, x8_ref, h8_ref, base_ref, shift_ref,
                        gate_ref, q3, idx_t, gate_t, *lists):
    tm = x_ref.shape[0]
    n = SEL_HEADS_PER_STEP
    s1, i1, s2, i2, top, eid = (lists[j * n:(j + 1) * n] for j in range(6))
    x = x_ref[...]
    h = _rms(x, g_ref[...])
    for r in range(D_MODEL // LANES):
        rows = pl.ds(r, tm, stride=D_MODEL // LANES)
        h8_ref[rows, :] = h[:, LANES * r:LANES * (r + 1)]
        x8_ref[rows, :] = x[:, LANES * r:LANES * (r + 1)]
    q = _bdot(h, wq_ref[...])
    for j in range(2 * PEER_HEADS):
        q3[j] = q[:, LANES * j:LANES * (j + 1)]
    iota_k = lax.broadcasted_iota(jnp.int32, (PEER_NKEYS, tm), 0).astype(F32)

    def heads(step, _):
        for u in range(SEL_HEADS_PER_STEP):
            hd = step * SEL_HEADS_PER_STEP + u
            sc1 = _bdot_nt(keys_ref[2 * hd], q3[2 * hd])
            sc2 = _bdot_nt(keys_ref[2 * hd + 1], q3[2 * hd + 1])
            _top_rows(sc1, iota_k, None, s1[u], i1[u])
            _top_rows(sc2, iota_k, None, s2[u], i2[u])
            cand = _cand_rows(s1[u][...], s2[u][...], lambda x, y: x + y) + cneg_ref[...]
            cidx = _cand_rows(i1[u][...], i2[u][...], lambda x, y: x * float(PEER_NKEYS) + y)
            _top_rows(cand, cflat_ref[...], cidx, top[u], eid[u])
            tv = top[u][...]
            e = jnp.exp(tv - jnp.max(tv, axis=0, keepdims=True))
            rs = pl.ds(pl.multiple_of(hd * PEER_TOPK, PEER_TOPK), PEER_TOPK)
            idx_t[rs, :] = eid[u][...]
            gate_t[rs, :] = e / jnp.sum(e, axis=0, keepdims=True)
        return 0

    lax.fori_loop(0, PEER_HEADS // SEL_HEADS_PER_STEP, heads, 0)
    e_t = idx_t[...].T
    pair = jnp.floor(e_t * 0.5)
    base_ref[...] = (pair * float(SUBLANES)).astype(jnp.int32)
    shift_ref[...] = ((e_t - 2.0 * pair) * 16.0).astype(jnp.int32)
    gate_ref[...] = gate_t[...].T


def _peer_select(x2, g, wq, keys):
    t = x2.shape[0]
    tm = SEL_BLOCK
    keys_b = keys.reshape(2 * PEER_HEADS, PEER_NKEYS, LANES).astype(BF16)
    wq_b = wq.astype(BF16)
    cflat, cneg = _cand_consts(tm)
    tokspec = lambda w: pl.BlockSpec((tm, w), lambda i: (i, 0))
    vm = lambda r: pltpu.VMEM((r, tm), F32)
    return pl.pallas_call(
        _peer_select_kernel, grid=(t // tm,),
        in_specs=[tokspec(D_MODEL), _full((1, D_MODEL)), _full(wq_b.shape), _full(keys_b.shape),
                  _full(cflat.shape), _full(cneg.shape)],
        out_specs=[pl.BlockSpec((tm * SUBLANES, LANES), lambda i: (i, 0))] * 2
                  + [tokspec(PEER_SEL), tokspec(PEER_SEL), tokspec(PEER_SEL)],
        out_shape=[jax.ShapeDtypeStruct((t * SUBLANES, LANES), F32)] * 2 + [
                   jax.ShapeDtypeStruct((t, PEER_SEL), jnp.int32),
                   jax.ShapeDtypeStruct((t, PEER_SEL), jnp.int32), jax.ShapeDtypeStruct((t, PEER_SEL), F32)],
        scratch_shapes=[pltpu.VMEM((2 * PEER_HEADS, tm, LANES), F32), vm(PEER_SEL), vm(PEER_SEL)]
                       + [vm(PEER_TOPK) for _ in range(6 * SEL_HEADS_PER_STEP)],
        compiler_params=_params(("parallel",)),
    )(x2, g.reshape(1, -1).astype(F32), wq_b, keys_b, cflat, cneg)


def _tile_table(tab):
    return tab.reshape(-1, D_MODEL // LANES, LANES)


def _pack_table(tab3):
    n = tab3.shape[0]
    bits = lax.bitcast_convert_type(tab3.astype(BF16), jnp.uint16).astype(U32)
    bits = bits.reshape(n // 2, 2, SUBLANES, LANES)
    packed = (bits[:, 0] << 16) | bits[:, 1]
    return lax.bitcast_convert_type(packed, jnp.int32).reshape(n // 2 * SUBLANES, LANES)


def _splat_into(src_ref, t, dst_ref, slot):
    tile = jnp.broadcast_to(src_ref[pl.ds(t, 1), :], (LANES, LANES)).T
    dst_ref[LANES * slot:LANES * (slot + 1), :] = tile


def _bcast_row(ref, row):
    return jnp.broadcast_to(ref[row:row + 1, :], (SUBLANES, LANES))


def _expert_tile(tab_ref, base, shift_splat, row):
    w = tab_ref[pl.ds(pl.multiple_of(base, SUBLANES), SUBLANES), :]
    return lax.bitcast_convert_type((w << _bcast_row(shift_splat, row)) & jnp.int32(-65536), F32)


def _token_rows(t):
    return pl.ds(pl.multiple_of(t * SUBLANES, SUBLANES), SUBLANES)


def _pipelined_tokens(tg, prep, work):
    prep(0, 0)

    def body(i, _):
        t0 = 2 * i
        prep(t0 + 1, 1)
        work(t0, 0)
        prep(jnp.minimum(t0 + 2, tg - 1), 0)
        work(t0 + 1, 1)
        return 0

    lax.fori_loop(0, tg // 2, body, 0)


def _table_spec(packed, layer, n_layers):
    rows = packed.shape[0] // n_layers
    return pl.BlockSpec((rows, LANES), lambda i: (layer, 0), pipeline_mode=pl.Buffered(1))


N_CHUNK = D_MODEL // LANES
CHUNK_STRIDE = PEER_SEL + SUBLANES


def _peer_u_kernel(base_ref, shift_ref, h_ref, gate_ref, tab_ref, eye_ref, c_ref, shift_splat, *planes):
    tg = gate_ref.shape[0]
    ones = jnp.ones((LANES, LANES), BF16)

    half = N_CHUNK // 2

    def prep(t, slot):
        _splat_into(shift_ref, t, shift_splat, slot)

    def gather(t, slot):
        ht = h_ref[_token_rows(t), :]
        for kx in range(PEER_SEL):
            prod = _expert_tile(tab_ref, base_ref[t, kx], shift_splat, LANES * slot + kx) * ht
            fold = prod + pltpu.roll(prod, half, 0)
            planes[slot][pl.ds(kx, half, stride=CHUNK_STRIDE), :] = fold[0:half, :]

    def finish(t, slot):
        plane = planes[slot]
        acc = plane[0:PEER_SEL, :]
        for r in range(1, half):
            acc = acc + plane[CHUNK_STRIDE * r:CHUNK_STRIDE * r + PEER_SEL, :]
        hi = acc.astype(BF16)
        lo = (acc - hi.astype(F32)).astype(BF16)
        tot = jnp.dot(hi, ones, preferred_element_type=F32) + jnp.dot(lo, ones, preferred_element_type=F32)
        score = jnp.sum(tot * eye_ref[...], axis=0, keepdims=True)
        c_ref[pl.ds(t, 1), :] = gate_ref[pl.ds(t, 1), :] * _gelu(score)

    planes[1][...] = jnp.zeros_like(planes[1])
    prep(0, 0)

    def body(i, _):
        t0 = 2 * i
        prep(t0 + 1, 1)
        gather(t0, 0)
        finish(jnp.maximum(t0 - 1, 0), 1)
        prep(jnp.minimum(t0 + 2, tg - 1), 0)
        gather(t0 + 1, 1)
        finish(t0, 0)
        return 0

    lax.fori_loop(0, tg // 2, body, 0)
    finish(tg - 1, 1)


def _peer_u(base, shift, h8, gate, tab, t, layer, n_layers):
    tg = GATHER_BLOCK
    eye = jnp.eye(LANES, dtype=F32)
    tokrow = pl.BlockSpec((tg, PEER_SEL), lambda i: (i, 0))
    return pl.pallas_call(
        _peer_u_kernel, grid=(t // tg,),
        in_specs=[pl.BlockSpec((tg, PEER_SEL), lambda i: (i, 0), memory_space=pltpu.SMEM), tokrow,
                  pl.BlockSpec((tg * SUBLANES, LANES), lambda i: (i, 0)), tokrow,
                  _table_spec(tab, layer, n_layers), _full(eye.shape)],
        out_specs=tokrow, out_shape=jax.ShapeDtypeStruct((t, PEER_SEL), F32),
        scratch_shapes=[pltpu.VMEM((2 * LANES, LANES), jnp.int32)]
                       + [pltpu.VMEM((N_CHUNK // 2 * CHUNK_STRIDE, LANES), F32)] * 2,
        compiler_params=_params(("parallel",)),
    )(base, shift, h8, gate, tab, eye)


def _peer_v_kernel(base_ref, shift_ref, c_ref, x_ref, tab_ref, _aliased_out, o_ref, shift_splat, c_splat, otile):
    tg = c_ref.shape[0]
    n_acc = 4

    def prep(t, slot):
        _splat_into(shift_ref, t, shift_splat, slot)
        _splat_into(c_ref, t, c_splat, slot)

    def work(t, slot):
        accs = [jnp.zeros((SUBLANES, LANES), F32) for _ in range(n_acc)]
        for kx in range(PEER_SEL):
            row = LANES * slot + kx
            accs[kx % n_acc] = accs[kx % n_acc] + (_bcast_row(c_splat, row)
                                                   * _expert_tile(tab_ref, base_ref[t, kx], shift_splat, row))
        rs = _token_rows(t)
        otile[rs, :] = x_ref[rs, :] + ((accs[0] + accs[1]) + (accs[2] + accs[3]))

    _pipelined_tokens(tg, prep, work)
    for r in range(N_CHUNK):
        o_ref[:, LANES * r:LANES * (r + 1)] = otile[pl.ds(r, tg, stride=N_CHUNK), :]


def _peer_v(base, shift, c, x8, x2, tab, t, layer, n_layers):
    tg = GATHER_BLOCK
    smem = pl.BlockSpec((tg, PEER_SEL), lambda i: (i, 0), memory_space=pltpu.SMEM)
    tokrow = pl.BlockSpec((tg, PEER_SEL), lambda i: (i, 0))
    tile = pl.BlockSpec((tg * SUBLANES, LANES), lambda i: (i, 0))
    return pl.pallas_call(
        _peer_v_kernel, grid=(t // tg,),
        in_specs=[smem, tokrow, tokrow, tile, _table_spec(tab, layer, n_layers),
                  pl.BlockSpec(memory_space=pl.ANY)],
        out_specs=pl.BlockSpec((tg, D_MODEL), lambda i: (i, 0)),
        out_shape=jax.ShapeDtypeStruct(x2.shape, F32),
        input_output_aliases={5: 0},
        scratch_shapes=[pltpu.VMEM((2 * LANES, LANES), jnp.int32), pltpu.VMEM((2 * LANES, LANES), F32),
                        pltpu.VMEM((tg * SUBLANES, LANES), F32)],
        compiler_params=_params(("parallel",)),
    )(base, shift, c, x8, tab, x2)


SC_WORKERS = 32
SC_LANES = 16
SC_ROWS = 32
SC_SHARE_PIPELINED = 40 / 64
SC_SHARE_LAST = 26 / 64
SEQ_GROUPS = 4
SC_GROUP = 8
SC_REGS = 32


def _peer_sc(eidx, gate, h8, x8, tab_u, tab_v, t_off):
    ts = eidx.shape[0]
    per_w = ts // SC_WORKERS
    n_chunk = PEER_SEL // SC_ROWS
    grp = SC_GROUP
    sel = grp * PEER_SEL
    n_pairs = grp * n_chunk // 2
    per_tile = LANES // SC_LANES
    n_blk = D_MODEL // (SC_REGS * SC_LANES)
    mesh = plsc.VectorSubcoreMesh(core_axis_name="c", subcore_axis_name="s")

    def piece(ref, lead, q):
        return ref[lead, q // per_tile, pl.ds(SC_LANES * (q % per_tile), SC_LANES)]

    def tok_piece(ref, tok, q):
        return ref[tok * N_CHUNK + q // per_tile, pl.ds(SC_LANES * (q % per_tile), SC_LANES)]

    def body(u_hbm, v_hbm, idx_hbm, gate_hbm, h_hbm, x_hbm, out_hbm,
             idx_v, c_v, h_v, acc_v, part_v, rows0, rows1, sem0, sem1):
        wid = lax.axis_index("s") * 2 + lax.axis_index("c")
        bufs, sems = (rows0, rows1), (sem0, sem1)
        lane = lax.iota(jnp.int32, SC_LANES)

        def stream(tab_hbm, compute):
            def gather(chunk, slot):
                start = pl.multiple_of(chunk * SC_ROWS, SC_ROWS)
                return pltpu.make_async_copy(tab_hbm.at[idx_v.at[pl.ds(start, SC_ROWS)]], bufs[slot], sems[slot])

            gather(0, 0).start()

            @pl.loop(0, n_pairs)
            def _(p):
                c0 = 2 * p
                gather(c0 + 1, 1).start()
                gather(c0, 0).wait()
                compute(c0, 0)
                gather(jnp.minimum(c0 + 2, 2 * n_pairs - 1), 0).start()
                gather(c0 + 1, 1).wait()
                compute(c0 + 1, 1)

            gather(0, 0).wait()

        def dots(chunk, slot):
            tok = chunk // n_chunk
            for blk in range(n_blk):
                where = [blk * SC_REGS + j for j in range(SC_REGS)]
                hs = [tok_piece(h_v, tok, q) for q in where]

                def row(k, carry):
                    parts = [None] * 4
                    for i, q in enumerate(where):
                        term = piece(bufs[slot], k, q) * hs[i]
                        parts[i % 4] = term if parts[i % 4] is None else parts[i % 4] + term
                    tot = (parts[0] + parts[1]) + (parts[2] + parts[3])
                    at = pl.ds(pl.multiple_of((chunk * SC_ROWS + k) * SC_LANES, SC_LANES), SC_LANES)
                    if blk == 0:
                        part_v[at] = tot
                    else:
                        part_v[at] = part_v[at] + tot
                    return carry

                lax.fori_loop(0, SC_ROWS, row, 0)

        def weigh(chunk, slot):
            tok = chunk // n_chunk
            for blk in range(n_blk):
                where = [blk * SC_REGS + j for j in range(SC_REGS)]

                def row(k, accs):
                    ck = plsc.load_gather(c_v, [jnp.full((SC_LANES,), chunk * SC_ROWS, jnp.int32) + k])
                    return tuple(a + ck * piece(bufs[slot], k, q) for a, q in zip(accs, where))

                accs = lax.fori_loop(0, SC_ROWS, row, tuple(tok_piece(acc_v, tok, q) for q in where))
                for a, q in zip(accs, where):
                    acc_v[tok * N_CHUNK + q // per_tile, pl.ds(SC_LANES * (q % per_tile), SC_LANES)] = a

        @pl.loop(0, per_w // grp)
        def _(g):
            t0 = wid * per_w + g * grp
            flat = pl.ds(pl.multiple_of(t0 * PEER_SEL, sel), sel)
            pltpu.sync_copy(idx_hbm.at[flat], idx_v)
            pltpu.sync_copy(gate_hbm.at[flat], c_v)
            tok_rows = grp * N_CHUNK
            src_rows = pl.ds(pl.multiple_of((t_off + t0) * N_CHUNK, tok_rows), tok_rows)
            pltpu.sync_copy(h_hbm.at[src_rows], h_v)
            pltpu.sync_copy(x_hbm.at[src_rows], acc_v)
            stream(u_hbm, dots)

            @pl.loop(0, sel // SC_LANES)
            def _(m):
                base = (m * SC_LANES + lane) * SC_LANES
                score = plsc.load_gather(part_v, [base])
                for l in range(1, SC_LANES):
                    score = score + plsc.load_gather(part_v, [base + l])
                z = math.sqrt(2.0 / math.pi) * (score + 0.044715 * (score * score * score))
                tanh_z = 1.0 - 2.0 / (jnp.exp(2.0 * z) + 1.0)
                at = pl.ds(pl.multiple_of(m * SC_LANES, SC_LANES), SC_LANES)
                c_v[at] = c_v[at] * (0.5 * score * (1.0 + tanh_z))

            stream(v_hbm, weigh)
            pltpu.sync_copy(acc_v, out_hbm.at[pl.ds(pl.multiple_of(t0 * N_CHUNK, tok_rows), tok_rows)])

    tile = lambda n: pltpu.VMEM((n, N_CHUNK, LANES), F32)
    tok_tiles = pltpu.VMEM((grp * N_CHUNK, LANES), F32)
    return pl.kernel(
        body, mesh=mesh, out_type=jax.ShapeDtypeStruct((ts * N_CHUNK, LANES), F32),
        scratch_types=[pltpu.VMEM((sel,), jnp.int32), pltpu.VMEM((sel,), F32), tok_tiles, tok_tiles,
                       pltpu.VMEM((sel * SC_LANES,), F32), tile(SC_ROWS), tile(SC_ROWS),
                       pltpu.SemaphoreType.DMA, pltpu.SemaphoreType.DMA],
        compiler_params=pltpu.CompilerParams(needs_layout_passes=False, use_tc_tiling_on_sc=True),
    )(tab_u, tab_v, eidx.reshape(-1), gate.reshape(-1), h8, x8)


def _final_norm_kernel(x_ref, g_ref, o_ref):
    o_ref[...] = _rms(x_ref[...], g_ref[...])


def _final_norm(x2, g):
    t, d = x2.shape
    tm = min(PROJ_BLOCK, t)
    spec = pl.BlockSpec((tm, d), lambda i: (i, 0))
    return pl.pallas_call(
        _final_norm_kernel, grid=(t // tm,), in_specs=[spec, _full((1, d))], out_specs=spec,
        out_shape=jax.ShapeDtypeStruct((t, d), F32), compiler_params=_params(("parallel",)),
    )(x2, g.reshape(1, d).astype(F32))


def kernel(x, mem, norm_mix, w_in, rw_mu, rw_w0, rw_w2, rw_a0, rw_a2, rw_g2, rw_kk, rw_ka, rw_rk, rw_v0, rw_v1, rw_v2, rw_lnx_g, rw_lnx_b, s5_a_re, s5_a_im, s5_log_dt, s5_b_re, s5_b_im, s5_c_re, s5_c_im, s5_d, s5_glu_w, s5_glu_b, s5_out_g, w_out, norm_xa, norm_mem, xa_wq, xa_wk, xa_wv, xa_wo, norm_ffn, peer_wq, peer_keys, peer_u, peer_v, norm_final):
    bsz, seq, d = x.shape
    depth = w_in.shape[0]

    def layer(l, xg, memg, v_first, tabs, sc_share):
        nb = xg.shape[0]
        t = nb * seq
        x2 = xg.reshape(t, d)
        ws = [w_in[l][:, :RW_COLS].astype(BF16), w_in[l][:, RW_COLS:].astype(BF16)]
        if l > 0:
            ws.append(_pad_rows(rw_v1[l - 1].T, LANES, 0).T.astype(BF16))
        outs = _norm_proj(x2, norm_mix[l], ws, [F32] * len(ws), PROJ_BLOCK)
        z_rw = outs[0].reshape(nb, seq, RW_COLS)
        u_s5 = outs[1].reshape(nb, seq, D_S5)
        hv = outs[2].reshape(nb, seq, LANES) if l > 0 else None
        rw_prm = dict(mu=rw_mu[l], w0=rw_w0[l], w2=rw_w2[l], a0=rw_a0[l], a2=rw_a2[l], g2=rw_g2[l],
                      kk=rw_kk[l], ka=rw_ka[l], rk=rw_rk[l], lng=rw_lnx_g[l], lnb=rw_lnx_b[l])
        if l > 0:
            rw_prm.update(v0=rw_v0[l - 1], v2=rw_v2[l - 1])
        y_rw, v_first = _rwkv(z_rw, hv, v_first, rw_prm, nb, seq)
        s5_prm = dict(a_re=s5_a_re[l], a_im=s5_a_im[l], log_dt=s5_log_dt[l], b_re=s5_b_re[l], b_im=s5_b_im[l],
                      c_re=s5_c_re[l], c_im=s5_c_im[l], d=s5_d[l], glu_w=s5_glu_w[l], glu_b=s5_glu_b[l],
                      out_g=s5_out_g[l])
        y_s5 = _s5(u_s5, s5_prm, nb, seq)
        kv = _norm_proj(memg.reshape(nb * N_MEM, d), norm_mem[l],
                        [xa_wk[l].astype(BF16), xa_wv[l].astype(BF16)], [BF16, BF16], PROJ_BLOCK)
        kmem = kv[0].reshape(nb, N_MEM, d)
        vmem = kv[1].reshape(nb, N_MEM, d)
        xg = _mix_xattn(xg, y_rw, y_s5, w_out[l], norm_xa[l], xa_wq[l], kmem, vmem, xa_wo[l], nb, seq)
        x2 = xg.reshape(t, d)
        x8, h8, base, shift, gate = _peer_select(x2, norm_ffn[l], peer_wq[l], peer_keys[l])
        u3, v3, u_packed, v_packed = tabs
        n_sc = int(t * sc_share)
        assert n_sc % (SC_WORKERS * SC_GROUP) == 0 and (t - n_sc) % GATHER_BLOCK == 0
        t_tc = t - n_sc
        n_exp = peer_u.shape[1]
        eidx_sc = (base[t_tc:] >> 2) + (shift[t_tc:] >> 4) + l * n_exp
        x_sc = _peer_sc(eidx_sc, gate[t_tc:], h8, x8, u3, v3, t_tc)
        c = _peer_u(base, shift, h8, gate, u_packed, t_tc, l, depth)
        x_new = _peer_v(base, shift, c, x8, x2, v_packed, t_tc, l, depth)
        xg = lax.dynamic_update_slice(x_new, x_sc.reshape(n_sc, d), (t_tc, 0)).reshape(nb, seq, d)
        return xg, v_first

    nb = bsz // SEQ_GROUPS
    xs = [x[g * nb:(g + 1) * nb] for g in range(SEQ_GROUPS)]
    mems = [mem[g * nb:(g + 1) * nb] for g in range(SEQ_GROUPS)]
    v_firsts = [None] * SEQ_GROUPS
    u3, v3 = _tile_table(peer_u), _tile_table(peer_v)
    tabs = (u3, v3, _pack_table(u3), _pack_table(v3))
    for l in range(depth):
        for g in range(SEQ_GROUPS):
            last = l == depth - 1 and g == SEQ_GROUPS - 1
            share = SC_SHARE_LAST if last else SC_SHARE_PIPELINED
            xs[g], v_firsts[g] = layer(l, xs[g], mems[g], v_firsts[g], tabs, share)
    outs = [_final_norm(xg.reshape(nb * seq, d), norm_final).reshape(nb, seq, d) for xg in xs]
    return jnp.concatenate(outs, axis=0)
```

```python
import functools
import math

import jax
import jax.numpy as jnp
from jax import lax
from jax.experimental import pallas as pl
from jax.experimental.pallas import tpu as pltpu
from jax.experimental.pallas import tpu_sc as plsc

F32 = jnp.float32
BF16 = jnp.bfloat16
U32 = jnp.uint32

LANES = 128
SUBLANES = 8
VMEM_LIMIT = 56 * 1024 * 1024

D_MODEL = 1024
D_RWKV = 512
RW_HEAD = 64
RW_COLS = 1792
D_S5 = 512
S5_GROUPS = 32
S5_CH = 16
S5_STATE = 64
S5_MODES = S5_GROUPS * S5_STATE
N_MEM = 256
XA_HEADS = 4
XA_HEAD = 256
PEER_HEADS = 8
PEER_NKEYS = 128
PEER_TOPK = 16
PEER_SEL = PEER_HEADS * PEER_TOPK
RMS_EPS = 1e-6
GN_EPS = 64e-5

RW_CHUNK = 128
RW_SEQS_PER_STEP = 1
S5_BLOCK = 256
PROJ_BLOCK = 512
XA_BLOCK = 256
SEL_BLOCK = 128
GATHER_BLOCK = 64


def _params(sem):
    return pltpu.CompilerParams(dimension_semantics=sem, vmem_limit_bytes=VMEM_LIMIT)


def _rms(x, g):
    ms = jnp.mean(x * x, axis=-1, keepdims=True)
    return x * lax.rsqrt(ms + RMS_EPS) * g


def _bdot(a, b):
    return jnp.dot(a.astype(BF16), b.astype(BF16), preferred_element_type=F32)


def _bdot_nt(a, b):
    return lax.dot_general(a.astype(BF16), b.astype(BF16), (((1,), (1,)), ((), ())),
                           preferred_element_type=F32)


def _sigmoid(x):
    return 1.0 / (1.0 + jnp.exp(-x))


def _softplus(x):
    return jnp.maximum(x, 0.0) + jnp.log(1.0 + jnp.exp(-jnp.abs(x)))


def _gelu(x):
    return 0.5 * x * (1.0 + jnp.tanh(math.sqrt(2.0 / math.pi) * (x + 0.044715 * (x * x * x))))


def _full(shape):
    n = len(shape)
    return pl.BlockSpec(shape, lambda *_: (0,) * n)


def _norm_proj_kernel(*refs, n_out):
    x_ref, g_ref = refs[0], refs[1]
    w_refs = refs[2:2 + n_out]
    o_refs = refs[2 + n_out:]
    h = _rms(x_ref[...], g_ref[...]).astype(BF16)
    for w_ref, o_ref in zip(w_refs, o_refs):
        o_ref[...] = jnp.dot(h, w_ref[...], preferred_element_type=F32).astype(o_ref.dtype)


def _norm_proj(x2, g, ws, out_dtypes, block):
    t, d = x2.shape
    block = min(block, t)
    in_specs = [pl.BlockSpec((block, d), lambda i: (i, 0)), _full((1, d))]
    in_specs += [_full(w.shape) for w in ws]
    out_specs = [pl.BlockSpec((block, w.shape[1]), lambda i: (i, 0)) for w in ws]
    out_shape = [jax.ShapeDtypeStruct((t, w.shape[1]), dt) for w, dt in zip(ws, out_dtypes)]
    return pl.pallas_call(
        functools.partial(_norm_proj_kernel, n_out=len(ws)),
        grid=(t // block,), in_specs=in_specs, out_specs=out_specs, out_shape=out_shape,
        compiler_params=_params(("parallel",)),
    )(x2, g.reshape(1, d), *ws)


def _seg_sum(x, mseg):
    hi = x.astype(BF16)
    lo = (x - hi.astype(F32)).astype(BF16)
    return (jnp.dot(hi, mseg, preferred_element_type=F32)
            + jnp.dot(lo, mseg, preferred_element_type=F32))


def _col_bcast(row):
    return jnp.broadcast_to(row, (LANES, LANES)).T


def _rwkv_kernel(*refs, first_layer):
    zprev, hst = refs[-2], refs[-1]

    @pl.when(pl.program_id(1) == 0)
    def _():
        zprev[...] = jnp.zeros_like(zprev)
        hst[...] = jnp.zeros_like(hst)

    for b in range(refs[0].shape[0]):
        _rwkv_chunk(refs, first_layer, b)


def _rwkv_chunk(refs, first_layer, b):
    if first_layer:
        (z_ref, mu_ref, w0_ref, w2_ref, a0_ref, a2_ref, g2_ref, kk_ref, ka_ref, rk_ref,
         lng_ref, lnb_ref, mseg_ref, y_ref, vf_out_ref, zprev, hst) = refs
    else:
        (z_ref, hv_ref, vf_ref, v0_ref, v2_ref, mu_ref, w0_ref, w2_ref, a0_ref, a2_ref, g2_ref,
         kk_ref, ka_ref, rk_ref, lng_ref, lnb_ref, mseg_ref, y_ref, zprev, hst) = refs
    L = RW_CHUNK
    n_pair = D_RWKV // LANES

    z = z_ref[b]
    row = lax.broadcasted_iota(jnp.int32, (L, 1), 0)
    zs = jnp.where(row == 0, zprev[b:b + 1, :], pltpu.roll(z, 1, 0))
    zprev[b:b + 1, :] = z[L - 1:L, :]
    z = z + (zs - z) * mu_ref[...]
    r = z[:, 0:512]
    k = z[:, 512:1024]
    v = z[:, 1024:1536]
    wa = z[:, 1536:1664]
    gd = z[:, 1664:1792]
    mseg = mseg_ref[...]

    wlin = w0_ref[...] + _bdot(jnp.tanh(wa), w2_ref[...])
    lw = -jnp.exp(-_softplus(-wlin) - 0.5)
    a = _sigmoid(a0_ref[...] + _bdot(wa, a2_ref[...]))
    g = _bdot(_sigmoid(gd), g2_ref[...])
    if first_layer:
        vf_out_ref[b] = v
    else:
        v = v + (vf_ref[b] - v) * _sigmoid(v0_ref[...] + _bdot(hv_ref[b], v2_ref[...]))
    kk = k * kk_ref[...]
    kk = kk / jnp.maximum(jnp.sqrt(_seg_sum(kk * kk, mseg)), 1e-12)
    k2 = k * (1.0 + (a - 1.0) * ka_ref[...])
    av = -kk
    bv = kk * a

    ti = lax.broadcasted_iota(jnp.int32, (L, L), 0)
    si = lax.broadcasted_iota(jnp.int32, (L, L), 1)
    tril = (ti >= si).astype(F32)
    cum = jnp.dot(tril, lw, preferred_element_type=F32, precision=lax.Precision.HIGHEST)
    mid = cum[L // 2 - 1:L // 2, :]
    cm = cum - mid
    ecum = jnp.exp(cm)
    einv = jnp.exp(-cm)
    rt = r * ecum
    kt = k2 * einv
    bt = bv * einv
    at = av * jnp.exp(cm - lw)
    p_end = jnp.exp(cum[L - 1:L, :])
    e_end = ecum[L - 1:L, :]
    e_mid = jnp.exp(mid)

    lane = lax.broadcasted_iota(jnp.int32, (1, LANES), 1)
    m0 = (lane < RW_HEAD).astype(F32)
    m1 = 1.0 - m0
    strict = ti > si
    incl = ti >= si
    bi = lax.broadcasted_iota(jnp.int32, (LANES, LANES), 0) < RW_HEAD
    bj = lax.broadcasted_iota(jnp.int32, (LANES, LANES), 1) < RW_HEAD
    bdmask = (bi == bj).astype(F32)
    zeros_ll = jnp.zeros((L, L), F32)

    ys = []
    for p in range(D_RWKV // LANES):
        sl = slice(LANES * p, LANES * (p + 1))
        A, B, K, R, V = at[:, sl], bt[:, sl], kt[:, sl], rt[:, sl], v[:, sl]
        btkt = jnp.concatenate([B.T, K.T], axis=1)
        h0 = hst[b * n_pair + p]
        h0m = h0 * _col_bcast(e_mid[:, sl])
        sc = _bdot(jnp.concatenate([A * m0, A * m1, R * m0, R * m1], axis=0), btkt)
        aab = [jnp.where(strict, sc[e * L:(e + 1) * L, 0:L], 0.0) for e in range(2)]
        aak = [jnp.where(strict, sc[e * L:(e + 1) * L, L:2 * L], 0.0) for e in range(2)]
        arb = [jnp.where(incl, sc[(2 + e) * L:(3 + e) * L, 0:L], 0.0) for e in range(2)]
        ark = [jnp.where(incl, sc[(2 + e) * L:(3 + e) * L, L:2 * L], 0.0) for e in range(2)]
        arh = _bdot(jnp.concatenate([A, R], axis=0), h0m)
        v01 = jnp.concatenate([V * m0, V * m1], axis=0)
        x = arh[0:L] + _bdot(jnp.concatenate(aak, axis=1), v01)
        pm = jnp.concatenate(aab, axis=1)
        n_fac = int(math.log2(L))
        for it in range(n_fac):
            x = x + _bdot(pm, jnp.concatenate([x * m0, x * m1], axis=0))
            if it + 1 < n_fac:
                pd = jnp.concatenate(
                    [jnp.concatenate([pm[:, 0:L], zeros_ll], axis=1),
                     jnp.concatenate([zeros_ll, pm[:, L:2 * L]], axis=1)], axis=0)
                pm = _bdot(pm, pd)
        u = x
        yp = arh[L:2 * L] + _bdot(jnp.concatenate(arb + ark, axis=1),
                                  jnp.concatenate([u * m0, u * m1, v01], axis=0))
        upd = _bdot(btkt, jnp.concatenate([u, V], axis=0))
        hst[b * n_pair + p] = (h0 * _col_bcast(p_end[:, sl]) + upd * _col_bcast(e_end[:, sl])) * bdmask
        ys.append(yp)
    y = jnp.concatenate(ys, axis=1)

    mean = _seg_sum(y, mseg) * (1.0 / RW_HEAD)
    d = y - mean
    var = _seg_sum(d * d, mseg) * (1.0 / RW_HEAD)
    yn = d * lax.rsqrt(var + GN_EPS) * lng_ref[...] + lnb_ref[...]
    bonus = _seg_sum(r * k2 * rk_ref[...], mseg) * v
    y_ref[b] = (yn + bonus) * g


def _pad_rows(w, rows, offset):
    out = jnp.zeros((rows, w.shape[1]), w.dtype)
    return out.at[offset:offset + w.shape[0]].set(w)


def _rwkv(z_rw, hv, v_first, prm, bsz, seq):
    L = RW_CHUNK
    first = v_first is None
    row = lambda a: a.reshape(1, -1).astype(F32)
    hid = jnp.arange(D_RWKV) // RW_HEAD
    mseg = (hid[:, None] == hid[None, :]).astype(BF16)
    w2p = _pad_rows(prm['w2'], LANES, 0).astype(BF16)
    a2p = _pad_rows(prm['a2'], LANES, 64).astype(BF16)
    common = [row(prm['mu']), row(prm['w0']), w2p, row(prm['a0']), a2p, prm['g2'].astype(BF16),
              row(prm['kk']), row(prm['ka']), row(prm['rk']), row(prm['lng']), row(prm['lnb']), mseg]
    per = RW_SEQS_PER_STEP if bsz % RW_SEQS_PER_STEP == 0 else 1
    tok = lambda w: pl.BlockSpec((per, L, w), lambda b, t: (b, t, 0))
    common_specs = [_full(c.shape) for c in common]
    y_shape = jax.ShapeDtypeStruct((bsz, seq, D_RWKV), F32)
    scratch = [pltpu.VMEM((per, RW_COLS), F32), pltpu.VMEM((per * D_RWKV // LANES, LANES, LANES), F32)]
    if first:
        args = [z_rw] + common
        in_specs = [tok(RW_COLS)] + common_specs
        out_shape = [y_shape, y_shape]
        out_specs = [tok(D_RWKV), tok(D_RWKV)]
    else:
        v2p = _pad_rows(prm['v2'], LANES, 0).astype(BF16)
        extra = [row(prm['v0']), v2p]
        args = [z_rw, hv, v_first] + extra + common
        in_specs = [tok(RW_COLS), tok(LANES), tok(D_RWKV)] + [_full(c.shape) for c in extra] + common_specs
        out_shape = [y_shape]
        out_specs = [tok(D_RWKV)]
    outs = pl.pallas_call(
        functools.partial(_rwkv_kernel, first_layer=first),
        grid=(bsz // per, seq // L), in_specs=in_specs, out_specs=out_specs, out_shape=out_shape,
        scratch_shapes=scratch, compiler_params=_params(("parallel", "arbitrary")),
    )(*args)
    return (outs[0], outs[1]) if first else (outs[0], v_first)


def _s5_kernel(u_ref, wb_ref, wc_ref, lpr_ref, lpi_ref, d_ref, gw_ref, gb_ref, og_ref, o_ref,
               car_re, car_im, xre, xim):
    tb = u_ref.shape[1]

    @pl.when(pl.program_id(1) == 0)
    def _():
        car_re[...] = jnp.zeros_like(car_re)
        car_im[...] = jnp.zeros_like(car_im)

    u = u_ref[0]
    bu = _bdot(u, wb_ref[...])
    xre[...] = bu[:, 0:S5_MODES]
    xim[...] = bu[:, S5_MODES:2 * S5_MODES]
    row = lax.broadcasted_iota(jnp.int32, (SUBLANES, 1), 0)

    def tile(i, carry):
        cr, ci = carry
        rs = pl.ds(pl.multiple_of(i * SUBLANES, SUBLANES), SUBLANES)
        br, bi = xre[rs, :], xim[rs, :]
        for dist in (1, 2, 4):
            keep = row >= dist
            sr = jnp.where(keep, pltpu.roll(br, dist, 0), 0.0)
            si = jnp.where(keep, pltpu.roll(bi, dist, 0), 0.0)
            lr = lpr_ref[dist - 1:dist, :]
            li = lpi_ref[dist - 1:dist, :]
            br, bi = br + lr * sr - li * si, bi + lr * si + li * sr
        pr, pi = lpr_ref[...], lpi_ref[...]
        xr = br + pr * cr - pi * ci
        xi = bi + pr * ci + pi * cr
        xre[rs, :] = xr
        xim[rs, :] = xi
        return xr[SUBLANES - 1:SUBLANES, :], xi[SUBLANES - 1:SUBLANES, :]

    cr, ci = lax.fori_loop(0, tb // SUBLANES, tile, (car_re[...], car_im[...]))
    car_re[...] = cr
    car_im[...] = ci
    wc = wc_ref[...]
    y = _bdot(xre[...], wc[0:S5_MODES]) + _bdot(xim[...], wc[S5_MODES:2 * S5_MODES])
    y = _gelu(y + d_ref[...] * u)
    y = y * _sigmoid(_bdot(y, gw_ref[...]) + gb_ref[...])
    o_ref[0] = _rms(y, og_ref[...])


def _s5_weights(a_re, a_im, log_dt, b_re, b_im, c_re, c_im):
    lam_re = jnp.minimum(a_re.astype(F32), -1e-4)
    lam_im = a_im.astype(F32)
    dt = jnp.exp(log_dt.astype(F32))[:, None]
    mag = jnp.exp(lam_re * dt)
    lb_re = mag * jnp.cos(lam_im * dt)
    lb_im = mag * jnp.sin(lam_im * dt)
    den = lam_re * lam_re + lam_im * lam_im
    c1_re = ((lb_re - 1.0) * lam_re + lb_im * lam_im) / den
    c1_im = (lb_im * lam_re - (lb_re - 1.0) * lam_im) / den
    br, bi = b_re.astype(F32), b_im.astype(F32)
    bb_re = c1_re[..., None] * br - c1_im[..., None] * bi
    bb_im = c1_re[..., None] * bi + c1_im[..., None] * br
    eye = jnp.eye(S5_GROUPS, dtype=F32)
    wb_re = jnp.einsum('gpc,gh->gchp', bb_re, eye).reshape(D_S5, S5_MODES)
    wb_im = jnp.einsum('gpc,gh->gchp', bb_im, eye).reshape(D_S5, S5_MODES)
    wb = jnp.concatenate([wb_re, wb_im], axis=1).astype(BF16)
    wc_re = jnp.einsum('gcp,gh->gphc', c_re.astype(F32), eye).reshape(S5_MODES, D_S5)
    wc_im = jnp.einsum('gcp,gh->gphc', c_im.astype(F32), eye).reshape(S5_MODES, D_S5)
    wc = jnp.concatenate([wc_re, -wc_im], axis=0).astype(BF16)
    pr, pi = [lb_re], [lb_im]
    for _ in range(SUBLANES - 1):
        pr, pi = pr + [pr[-1] * lb_re - pi[-1] * lb_im], pi + [pr[-1] * lb_im + pi[-1] * lb_re]
    lp_re = jnp.stack(pr).reshape(SUBLANES, S5_MODES)
    lp_im = jnp.stack(pi).reshape(SUBLANES, S5_MODES)
    return wb, wc, lp_re, lp_im


def _s5(u, prm, bsz, seq):
    tb = min(S5_BLOCK, seq)
    wb, wc, lp_re, lp_im = _s5_weights(prm['a_re'], prm['a_im'], prm['log_dt'], prm['b_re'], prm['b_im'],
                                       prm['c_re'], prm['c_im'])
    row = lambda a: a.reshape(1, -1).astype(F32)
    consts = [wb, wc, lp_re, lp_im, row(prm['d']), prm['glu_w'].astype(BF16), row(prm['glu_b']),
              row(prm['out_g'])]
    tok = pl.BlockSpec((1, tb, D_S5), lambda b, t: (b, t, 0))
    return pl.pallas_call(
        _s5_kernel, grid=(bsz, seq // tb),
        in_specs=[tok] + [_full(c.shape) for c in consts], out_specs=tok,
        out_shape=jax.ShapeDtypeStruct((bsz, seq, D_S5), F32),
        scratch_shapes=[pltpu.VMEM((1, S5_MODES), F32), pltpu.VMEM((1, S5_MODES), F32),
                        pltpu.VMEM((tb, S5_MODES), F32), pltpu.VMEM((tb, S5_MODES), F32)],
        compiler_params=_params(("parallel", "arbitrary")),
    )(u, *consts)


def _mix_xattn_kernel(x_ref, yr_ref, ys_ref, wo1_ref, wo2_ref, g_ref, wq_ref, k_ref, v_ref, wo_ref, o_ref):
    x1 = x_ref[0] + _bdot(yr_ref[0], wo1_ref[...]) + _bdot(ys_ref[0], wo2_ref[...])
    h = _rms(x1, g_ref[...])
    q = _bdot(h, wq_ref[...])
    km, vm = k_ref[0], v_ref[0]
    outs = []
    for hd in range(XA_HEADS):
        sl = slice(XA_HEAD * hd, XA_HEAD * (hd + 1))
        s = _bdot_nt(q[:, sl], km[:, sl]) * (XA_HEAD ** -0.5)
        s = s - jnp.max(s, axis=-1, keepdims=True)
        e = jnp.exp(s)
        p = e / jnp.sum(e, axis=-1, keepdims=True)
        outs.append(_bdot(p, vm[:, sl]))
    o = jnp.concatenate(outs, axis=1)
    o_ref[0] = x1 + _bdot(o, wo_ref[...])


def _mix_xattn(x, y_rw, y_s5, w_out, g, wq, kmem, vmem, wo, bsz, seq):
    tm = min(XA_BLOCK, seq)
    consts_a = [w_out[:D_RWKV].astype(BF16), w_out[D_RWKV:].astype(BF16), g.reshape(1, -1).astype(F32),
                wq.astype(BF16)]
    tok = lambda w: pl.BlockSpec((1, tm, w), lambda b, t: (b, t, 0))
    mem = pl.BlockSpec((1, N_MEM, D_MODEL), lambda b, t: (b, 0, 0))
    wo_b = wo.astype(BF16)
    return pl.pallas_call(
        _mix_xattn_kernel, grid=(bsz, seq // tm),
        in_specs=[tok(D_MODEL), tok(D_RWKV), tok(D_S5)] + [_full(c.shape) for c in consts_a]
                 + [mem, mem, _full(wo_b.shape)],
        out_specs=tok(D_MODEL), out_shape=jax.ShapeDtypeStruct((bsz, seq, D_MODEL), F32),
        compiler_params=_params(("parallel", "parallel")),
    )(x, y_rw, y_s5, *consts_a, kmem, vmem, wo_b)


def _top_rows(work, order, aux, val_ref, idx_ref):
    for it in range(PEER_TOPK):
        m = jnp.max(work, axis=0, keepdims=True)
        pos = jnp.min(jnp.where(work == m, order, jnp.inf), axis=0, keepdims=True)
        hit = order == pos
        val_ref[it:it + 1, :] = m
        if aux is None:
            idx_ref[it:it + 1, :] = pos
        else:
            idx_ref[it:it + 1, :] = jnp.sum(jnp.where(hit, aux, 0.0), axis=0, keepdims=True)
        work = jnp.where(hit, -jnp.inf, work)


_CAND_ROW_BLOCKS = [(0, PEER_TOPK), (1, SUBLANES), (2, SUBLANES), (3, SUBLANES)]
_CAND_COL_BLOCKS = [(0, PEER_TOPK, 4, 15), (1, SUBLANES, 4, 7), (2, SUBLANES, 4, 4)]
N_CAND = sum(n for _, n in _CAND_ROW_BLOCKS) + sum(n for _, n, _, _ in _CAND_COL_BLOCKS)


def _cand_consts(tm):
    flat, neg = [], []
    for a, nb in _CAND_ROW_BLOCKS:
        flat += [a * PEER_TOPK + b for b in range(nb)]
        neg += [0.0] * nb
    for b, na, lo, hi in _CAND_COL_BLOCKS:
        flat += [a * PEER_TOPK + b for a in range(na)]
        neg += [0.0 if lo <= a <= hi else -float('inf') for a in range(na)]
    flat = [f if n == 0.0 else 1000.0 + i for i, (f, n) in enumerate(zip(flat, neg))]
    col = lambda v: jnp.broadcast_to(jnp.asarray(v, F32)[:, None], (N_CAND, tm))
    return col(flat), col(neg)


def _cand_rows(row_vals, col_vals, combine):
    blocks = [combine(row_vals[a:a + 1, :], col_vals[0:nb, :]) for a, nb in _CAND_ROW_BLOCKS]
    blocks += [combine(row_vals[0:na, :], col_vals[b:b + 1, :]) for b, na, _, _ in _CAND_COL_BLOCKS]
    return jnp.concatenate(blocks, axis=0)


SEL_HEADS_PER_STEP = 4


def _peer_select_kernel(x_ref, g_ref, wq_ref, keys_ref, cflat_ref, cneg_ref, x8_ref, h8_ref, base_ref, shift_ref,
                        gate_ref, q3, idx_t, gate_t, *lists):
    tm = x_ref.shape[0]
    n = SEL_HEADS_PER_STEP
    s1, i1, s2, i2, top, eid = (lists[j * n:(j + 1) * n] for j in range(6))
    x = x_ref[...]
    h = _rms(x, g_ref[...])
    for r in range(D_MODEL // LANES):
        rows = pl.ds(r, tm, stride=D_MODEL // LANES)
        h8_ref[rows, :] = h[:, LANES * r:LANES * (r + 1)]
        x8_ref[rows, :] = x[:, LANES * r:LANES * (r + 1)]
    q = _bdot(h, wq_ref[...])
    for j in range(2 * PEER_HEADS):
        q3[j] = q[:, LANES * j:LANES * (j + 1)]
    iota_k = lax.broadcasted_iota(jnp.int32, (PEER_NKEYS, tm), 0).astype(F32)

    def heads(step, _):
        for u in range(SEL_HEADS_PER_STEP):
            hd = step * SEL_HEADS_PER_STEP + u
            sc1 = _bdot_nt(keys_ref[2 * hd], q3[2 * hd])
            sc2 = _bdot_nt(keys_ref[2 * hd + 1], q3[2 * hd + 1])
            _top_rows(sc1, iota_k, None, s1[u], i1[u])
            _top_rows(sc2, iota_k, None, s2[u], i2[u])
            cand = _cand_rows(s1[u][...], s2[u][...], lambda x, y: x + y) + cneg_ref[...]
            cidx = _cand_rows(i1[u][...], i2[u][...], lambda x, y: x * float(PEER_NKEYS) + y)
            _top_rows(cand, cflat_ref[...], cidx, top[u], eid[u])
            tv = top[u][...]
            e = jnp.exp(tv - jnp.max(tv, axis=0, keepdims=True))
            rs = pl.ds(pl.multiple_of(hd * PEER_TOPK, PEER_TOPK), PEER_TOPK)
            idx_t[rs, :] = eid[u][...]
            gate_t[rs, :] = e / jnp.sum(e, axis=0, keepdims=True)
        return 0

    lax.fori_loop(0, PEER_HEADS // SEL_HEADS_PER_STEP, heads, 0)
    e_t = idx_t[...].T
    pair = jnp.floor(e_t * 0.5)
    base_ref[...] = (pair * float(SUBLANES)).astype(jnp.int32)
    shift_ref[...] = ((e_t - 2.0 * pair) * 16.0).astype(jnp.int32)
    gate_ref[...] = gate_t[...].T


def _peer_select(x2, g, wq, keys):
    t = x2.shape[0]
    tm = SEL_BLOCK
    keys_b = keys.reshape(2 * PEER_HEADS, PEER_NKEYS, LANES).astype(BF16)
    wq_b = wq.astype(BF16)
    cflat, cneg = _cand_consts(tm)
    tokspec = lambda w: pl.BlockSpec((tm, w), lambda i: (i, 0))
    vm = lambda r: pltpu.VMEM((r, tm), F32)
    return pl.pallas_call(
        _peer_select_kernel, grid=(t // tm,),
        in_specs=[tokspec(D_MODEL), _full((1, D_MODEL)), _full(wq_b.shape), _full(keys_b.shape),
                  _full(cflat.shape), _full(cneg.shape)],
        out_specs=[pl.BlockSpec((tm * SUBLANES, LANES), lambda i: (i, 0))] * 2
                  + [tokspec(PEER_SEL), tokspec(PEER_SEL), tokspec(PEER_SEL)],
        out_shape=[jax.ShapeDtypeStruct((t * SUBLANES, LANES), F32)] * 2 + [
                   jax.ShapeDtypeStruct((t, PEER_SEL), jnp.int32),
                   jax.ShapeDtypeStruct((t, PEER_SEL), jnp.int32), jax.ShapeDtypeStruct((t, PEER_SEL), F32)],
        scratch_shapes=[pltpu.VMEM((2 * PEER_HEADS, tm, LANES), F32), vm(PEER_SEL), vm(PEER_SEL)]
                       + [vm(PEER_TOPK) for _ in range(6 * SEL_HEADS_PER_STEP)],
        compiler_params=_params(("parallel",)),
    )(x2, g.reshape(1, -1).astype(F32), wq_b, keys_b, cflat, cneg)


def _tile_table(tab):
    return tab.reshape(-1, D_MODEL // LANES, LANES)


def _pack_table(tab3):
    n = tab3.shape[0]
    bits = lax.bitcast_convert_type(tab3.astype(BF16), jnp.uint16).astype(U32)
    bits = bits.reshape(n // 2, 2, SUBLANES, LANES)
    packed = (bits[:, 0] << 16) | bits[:, 1]
    return lax.bitcast_convert_type(packed, jnp.int32).reshape(n // 2 * SUBLANES, LANES)


def _splat_into(src_ref, t, dst_ref, slot):
    tile = jnp.broadcast_to(src_ref[pl.ds(t, 1), :], (LANES, LANES)).T
    dst_ref[LANES * slot:LANES * (slot + 1), :] = tile


def _bcast_row(ref, row):
    return jnp.broadcast_to(ref[row:row + 1, :], (SUBLANES, LANES))


def _expert_tile(tab_ref, base, shift_splat, row):
    w = tab_ref[pl.ds(pl.multiple_of(base, SUBLANES), SUBLANES), :]
    return lax.bitcast_convert_type((w << _bcast_row(shift_splat, row)) & jnp.int32(-65536), F32)


def _token_rows(t):
    return pl.ds(pl.multiple_of(t * SUBLANES, SUBLANES), SUBLANES)


def _pipelined_tokens(tg, prep, work):
    prep(0, 0)

    def body(i, _):
        t0 = 2 * i
        prep(t0 + 1, 1)
        work(t0, 0)
        prep(jnp.minimum(t0 + 2, tg - 1), 0)
        work(t0 + 1, 1)
        return 0

    lax.fori_loop(0, tg // 2, body, 0)


def _table_spec(packed, layer, n_layers):
    rows = packed.shape[0] // n_layers
    return pl.BlockSpec((rows, LANES), lambda i: (layer, 0), pipeline_mode=pl.Buffered(1))


N_CHUNK = D_MODEL // LANES
CHUNK_STRIDE = PEER_SEL + SUBLANES


def _peer_u_kernel(base_ref, shift_ref, h_ref, gate_ref, tab_ref, eye_ref, c_ref, shift_splat, *planes):
    tg = gate_ref.shape[0]
    ones = jnp.ones((LANES, LANES), BF16)

    half = N_CHUNK // 2

    def prep(t, slot):
        _splat_into(shift_ref, t, shift_splat, slot)

    def gather(t, slot):
        ht = h_ref[_token_rows(t), :]
        for kx in range(PEER_SEL):
            prod = _expert_tile(tab_ref, base_ref[t, kx], shift_splat, LANES * slot + kx) * ht
            fold = prod + pltpu.roll(prod, half, 0)
            planes[slot][pl.ds(kx, half, stride=CHUNK_STRIDE), :] = fold[0:half, :]

    def finish(t, slot):
        plane = planes[slot]
        acc = plane[0:PEER_SEL, :]
        for r in range(1, half):
            acc = acc + plane[CHUNK_STRIDE * r:CHUNK_STRIDE * r + PEER_SEL, :]
        hi = acc.astype(BF16)
        lo = (acc - hi.astype(F32)).astype(BF16)
        tot = jnp.dot(hi, ones, preferred_element_type=F32) + jnp.dot(lo, ones, preferred_element_type=F32)
        score = jnp.sum(tot * eye_ref[...], axis=0, keepdims=True)
        c_ref[pl.ds(t, 1), :] = gate_ref[pl.ds(t, 1), :] * _gelu(score)

    planes[1][...] = jnp.zeros_like(planes[1])
    prep(0, 0)

    def body(i, _):
        t0 = 2 * i
        prep(t0 + 1, 1)
        gather(t0, 0)
        finish(jnp.maximum(t0 - 1, 0), 1)
        prep(jnp.minimum(t0 + 2, tg - 1), 0)
        gather(t0 + 1, 1)
        finish(t0, 0)
        return 0

    lax.fori_loop(0, tg // 2, body, 0)
    finish(tg - 1, 1)


def _peer_u(base, shift, h8, gate, tab, t, layer, n_layers):
    tg = GATHER_BLOCK
    eye = jnp.eye(LANES, dtype=F32)
    tokrow = pl.BlockSpec((tg, PEER_SEL), lambda i: (i, 0))
    return pl.pallas_call(
        _peer_u_kernel, grid=(t // tg,),
        in_specs=[pl.BlockSpec((tg, PEER_SEL), lambda i: (i, 0), memory_space=pltpu.SMEM), tokrow,
                  pl.BlockSpec((tg * SUBLANES, LANES), lambda i: (i, 0)), tokrow,
                  _table_spec(tab, layer, n_layers), _full(eye.shape)],
        out_specs=tokrow, out_shape=jax.ShapeDtypeStruct((t, PEER_SEL), F32),
        scratch_shapes=[pltpu.VMEM((2 * LANES, LANES), jnp.int32)]
                       + [pltpu.VMEM((N_CHUNK // 2 * CHUNK_STRIDE, LANES), F32)] * 2,
        compiler_params=_params(("parallel",)),
    )(base, shift, h8, gate, tab, eye)


def _peer_v_kernel(base_ref, shift_ref, c_ref, x_ref, tab_ref, _aliased_out, o_ref, shift_splat, c_splat, otile):
    tg = c_ref.shape[0]
    n_acc = 4

    def prep(t, slot):
        _splat_into(shift_ref, t, shift_splat, slot)
        _splat_into(c_ref, t, c_splat, slot)

    def work(t, slot):
        accs = [jnp.zeros((SUBLANES, LANES), F32) for _ in range(n_acc)]
        for kx in range(PEER_SEL):
            row = LANES * slot + kx
            accs[kx % n_acc] = accs[kx % n_acc] + (_bcast_row(c_splat, row)
                                                   * _expert_tile(tab_ref, base_ref[t, kx], shift_splat, row))
        rs = _token_rows(t)
        otile[rs, :] = x_ref[rs, :] + ((accs[0] + accs[1]) + (accs[2] + accs[3]))

    _pipelined_tokens(tg, prep, work)
    for r in range(N_CHUNK):
        o_ref[:, LANES * r:LANES * (r + 1)] = otile[pl.ds(r, tg, stride=N_CHUNK), :]


def _peer_v(base, shift, c, x8, x2, tab, t, layer, n_layers):
    tg = GATHER_BLOCK
    smem = pl.BlockSpec((tg, PEER_SEL), lambda i: (i, 0), memory_space=pltpu.SMEM)
    tokrow = pl.BlockSpec((tg, PEER_SEL), lambda i: (i, 0))
    tile = pl.BlockSpec((tg * SUBLANES, LANES), lambda i: (i, 0))
    return pl.pallas_call(
        _peer_v_kernel, grid=(t // tg,),
        in_specs=[smem, tokrow, tokrow, tile, _table_spec(tab, layer, n_layers),
                  pl.BlockSpec(memory_space=pl.ANY)],
        out_specs=pl.BlockSpec((tg, D_MODEL), lambda i: (i, 0)),
        out_shape=jax.ShapeDtypeStruct(x2.shape, F32),
        input_output_aliases={5: 0},
        scratch_shapes=[pltpu.VMEM((2 * LANES, LANES), jnp.int32), pltpu.VMEM((2 * LANES, LANES), F32),
                        pltpu.VMEM((tg * SUBLANES, LANES), F32)],
        compiler_params=_params(("parallel",)),
    )(base, shift, c, x8, tab, x2)


SC_WORKERS = 32
SC_LANES = 16
SC_ROWS = 32
SC_SHARE_PIPELINED = 42 / 64
SC_SHARE_TAIL = (26 / 64, 38 / 64)
SEQ_GROUPS = 4
SC_GROUP = 8
SC_REGS = 32


def _peer_sc(eidx, gate, h8, x8, tab_u, tab_v, t_off):
    ts = eidx.shape[0]
    per_w = ts // SC_WORKERS
    n_chunk = PEER_SEL // SC_ROWS
    grp = SC_GROUP
    sel = grp * PEER_SEL
    n_pairs = grp * n_chunk // 2
    per_tile = LANES // SC_LANES
    n_blk = D_MODEL // (SC_REGS * SC_LANES)
    mesh = plsc.VectorSubcoreMesh(core_axis_name="c", subcore_axis_name="s")

    def piece(ref, lead, q):
        return ref[lead, q // per_tile, pl.ds(SC_LANES * (q % per_tile), SC_LANES)]

    def tok_piece(ref, tok, q):
        return ref[tok * N_CHUNK + q // per_tile, pl.ds(SC_LANES * (q % per_tile), SC_LANES)]

    def body(u_hbm, v_hbm, idx_hbm, gate_hbm, h_hbm, x_hbm, out_hbm,
             idx_v, c_v, h_v, acc_v, part_v, rows0, rows1, sem0, sem1):
        wid = lax.axis_index("s") * 2 + lax.axis_index("c")
        bufs, sems = (rows0, rows1), (sem0, sem1)
        lane = lax.iota(jnp.int32, SC_LANES)

        def stream(tab_hbm, compute):
            def gather(chunk, slot):
                start = pl.multiple_of(chunk * SC_ROWS, SC_ROWS)
                return pltpu.make_async_copy(tab_hbm.at[idx_v.at[pl.ds(start, SC_ROWS)]], bufs[slot], sems[slot])

            gather(0, 0).start()

            @pl.loop(0, n_pairs)
            def _(p):
                c0 = 2 * p
                gather(c0 + 1, 1).start()
                gather(c0, 0).wait()
                compute(c0, 0)
                gather(jnp.minimum(c0 + 2, 2 * n_pairs - 1), 0).start()
                gather(c0 + 1, 1).wait()
                compute(c0 + 1, 1)

            gather(0, 0).wait()

        def dots(chunk, slot):
            tok = chunk // n_chunk
            for blk in range(n_blk):
                where = [blk * SC_REGS + j for j in range(SC_REGS)]
                hs = [tok_piece(h_v, tok, q) for q in where]

                def row(k, carry):
                    parts = [None] * 4
                    for i, q in enumerate(where):
                        term = piece(bufs[slot], k, q) * hs[i]
                        parts[i % 4] = term if parts[i % 4] is None else parts[i % 4] + term
                    tot = (parts[0] + parts[1]) + (parts[2] + parts[3])
                    at = pl.ds(pl.multiple_of((chunk * SC_ROWS + k) * SC_LANES, SC_LANES), SC_LANES)
                    if blk == 0:
                        part_v[at] = tot
                    else:
                        part_v[at] = part_v[at] + tot
                    return carry

                lax.fori_loop(0, SC_ROWS, row, 0)

        def weigh(chunk, slot):
            tok = chunk // n_chunk
            for blk in range(n_blk):
                where = [blk * SC_REGS + j for j in range(SC_REGS)]

                def row(k, accs):
                    ck = plsc.load_gather(c_v, [jnp.full((SC_LANES,), chunk * SC_ROWS, jnp.int32) + k])
                    return tuple(a + ck * piece(bufs[slot], k, q) for a, q in zip(accs, where))

                accs = lax.fori_loop(0, SC_ROWS, row, tuple(tok_piece(acc_v, tok, q) for q in where))
                for a, q in zip(accs, where):
                    acc_v[tok * N_CHUNK + q // per_tile, pl.ds(SC_LANES * (q % per_tile), SC_LANES)] = a

        @pl.loop(0, per_w // grp)
        def _(g):
            t0 = wid * per_w + g * grp
            flat = pl.ds(pl.multiple_of(t0 * PEER_SEL, sel), sel)
            pltpu.sync_copy(idx_hbm.at[flat], idx_v)
            pltpu.sync_copy(gate_hbm.at[flat], c_v)
            tok_rows = grp * N_CHUNK
            src_rows = pl.ds(pl.multiple_of((t_off + t0) * N_CHUNK, tok_rows), tok_rows)
            pltpu.sync_copy(h_hbm.at[src_rows], h_v)
            pltpu.sync_copy(x_hbm.at[src_rows], acc_v)
            stream(u_hbm, dots)

            @pl.loop(0, sel // SC_LANES)
            def _(m):
                base = (m * SC_LANES + lane) * SC_LANES
                score = plsc.load_gather(part_v, [base])
                for l in range(1, SC_LANES):
                    score = score + plsc.load_gather(part_v, [base + l])
                z = math.sqrt(2.0 / math.pi) * (score + 0.044715 * (score * score * score))
                tanh_z = 1.0 - 2.0 / (jnp.exp(2.0 * z) + 1.0)
                at = pl.ds(pl.multiple_of(m * SC_LANES, SC_LANES), SC_LANES)
                c_v[at] = c_v[at] * (0.5 * score * (1.0 + tanh_z))

            stream(v_hbm, weigh)
            pltpu.sync_copy(acc_v, out_hbm.at[pl.ds(pl.multiple_of(t0 * N_CHUNK, tok_rows), tok_rows)])

    tile = lambda n: pltpu.VMEM((n, N_CHUNK, LANES), F32)
    tok_tiles = pltpu.VMEM((grp * N_CHUNK, LANES), F32)
    return pl.kernel(
        body, mesh=mesh, out_type=jax.ShapeDtypeStruct((ts * N_CHUNK, LANES), F32),
        scratch_types=[pltpu.VMEM((sel,), jnp.int32), pltpu.VMEM((sel,), F32), tok_tiles, tok_tiles,
                       pltpu.VMEM((sel * SC_LANES,), F32), tile(SC_ROWS), tile(SC_ROWS),
                       pltpu.SemaphoreType.DMA, pltpu.SemaphoreType.DMA],
        compiler_params=pltpu.CompilerParams(needs_layout_passes=False, use_tc_tiling_on_sc=True),
    )(tab_u, tab_v, eidx.reshape(-1), gate.reshape(-1), h8, x8)


def _final_norm_kernel(x_ref, g_ref, o_ref):
    o_ref[...] = _rms(x_ref[...], g_ref[...])


def _final_norm(x2, g):
    t, d = x2.shape
    tm = min(PROJ_BLOCK, t)
    spec = pl.BlockSpec((tm, d), lambda i: (i, 0))
    return pl.pallas_call(
        _final_norm_kernel, grid=(t // tm,), in_specs=[spec, _full((1, d))], out_specs=spec,
        out_shape=jax.ShapeDtypeStruct((t, d), F32), compiler_params=_params(("parallel",)),
    )(x2, g.reshape(1, d).astype(F32))


def kernel(x, mem, norm_mix, w_in, rw_mu, rw_w0, rw_w2, rw_a0, rw_a2, rw_g2, rw_kk, rw_ka, rw_rk, rw_v0, rw_v1, rw_v2, rw_lnx_g, rw_lnx_b, s5_a_re, s5_a_im, s5_log_dt, s5_b_re, s5_b_im, s5_c_re, s5_c_im, s5_d, s5_glu_w, s5_glu_b, s5_out_g, w_out, norm_xa, norm_mem, xa_wq, xa_wk, xa_wv, xa_wo, norm_ffn, peer_wq, peer_keys, peer_u, peer_v, norm_final):
    bsz, seq, d = x.shape
    depth = w_in.shape[0]

    def layer(l, xg, memg, v_first, tabs, sc_share):
        nb = xg.shape[0]
        t = nb * seq
        x2 = xg.reshape(t, d)
        ws = [w_in[l][:, :RW_COLS].astype(BF16), w_in[l][:, RW_COLS:].astype(BF16)]
        if l > 0:
            ws.append(_pad_rows(rw_v1[l - 1].T, LANES, 0).T.astype(BF16))
        outs = _norm_proj(x2, norm_mix[l], ws, [F32] * len(ws), PROJ_BLOCK)
        z_rw = outs[0].reshape(nb, seq, RW_COLS)
        u_s5 = outs[1].reshape(nb, seq, D_S5)
        hv = outs[2].reshape(nb, seq, LANES) if l > 0 else None
        rw_prm = dict(mu=rw_mu[l], w0=rw_w0[l], w2=rw_w2[l], a0=rw_a0[l], a2=rw_a2[l], g2=rw_g2[l],
                      kk=rw_kk[l], ka=rw_ka[l], rk=rw_rk[l], lng=rw_lnx_g[l], lnb=rw_lnx_b[l])
        if l > 0:
            rw_prm.update(v0=rw_v0[l - 1], v2=rw_v2[l - 1])
        y_rw, v_first = _rwkv(z_rw, hv, v_first, rw_prm, nb, seq)
        s5_prm = dict(a_re=s5_a_re[l], a_im=s5_a_im[l], log_dt=s5_log_dt[l], b_re=s5_b_re[l], b_im=s5_b_im[l],
                      c_re=s5_c_re[l], c_im=s5_c_im[l], d=s5_d[l], glu_w=s5_glu_w[l], glu_b=s5_glu_b[l],
                      out_g=s5_out_g[l])
        y_s5 = _s5(u_s5, s5_prm, nb, seq)
        kv = _norm_proj(memg.reshape(nb * N_MEM, d), norm_mem[l],
                        [xa_wk[l].astype(BF16), xa_wv[l].astype(BF16)], [BF16, BF16], PROJ_BLOCK)
        kmem = kv[0].reshape(nb, N_MEM, d)
        vmem = kv[1].reshape(nb, N_MEM, d)
        xg = _mix_xattn(xg, y_rw, y_s5, w_out[l], norm_xa[l], xa_wq[l], kmem, vmem, xa_wo[l], nb, seq)
        x2 = xg.reshape(t, d)
        x8, h8, base, shift, gate = _peer_select(x2, norm_ffn[l], peer_wq[l], peer_keys[l])
        u3, v3, u_packed, v_packed = tabs
        n_sc = int(t * sc_share)
        assert n_sc % (SC_WORKERS * SC_GROUP) == 0 and (t - n_sc) % GATHER_BLOCK == 0
        t_tc = t - n_sc
        n_exp = peer_u.shape[1]
        eidx_sc = (base[t_tc:] >> 2) + (shift[t_tc:] >> 4) + l * n_exp
        x_sc = _peer_sc(eidx_sc, gate[t_tc:], h8, x8, u3, v3, t_tc)
        c = _peer_u(base, shift, h8, gate, u_packed, t_tc, l, depth)
        x_new = _peer_v(base, shift, c, x8, x2, v_packed, t_tc, l, depth)
        xg = lax.dynamic_update_slice(x_new, x_sc.reshape(n_sc, d), (t_tc, 0)).reshape(nb, seq, d)
        return xg, v_first

    nb = bsz // SEQ_GROUPS
    xs = [x[g * nb:(g + 1) * nb] for g in range(SEQ_GROUPS)]
    mems = [mem[g * nb:(g + 1) * nb] for g in range(SEQ_GROUPS)]
    v_firsts = [None] * SEQ_GROUPS
    u3, v3 = _tile_table(peer_u), _tile_table(peer_v)
    tabs = (u3, v3, _pack_table(u3), _pack_table(v3))
    for l in range(depth):
        for g in range(SEQ_GROUPS):
            calls_after = (depth - 1 - l) * SEQ_GROUPS + (SEQ_GROUPS - 1 - g)
            share = SC_SHARE_TAIL[calls_after] if calls_after < len(SC_SHARE_TAIL) else SC_SHARE_PIPELINED
            xs[g], v_firsts[g] = layer(l, xs[g], mems[g], v_firsts[g], tabs, share)
    outs = [_final_norm(xg.reshape(nb * seq, d), norm_final).reshape(nb, seq, d) for xg in xs]
    return jnp.concatenate(outs, axis=0)
```

```python
import functools
import math

import jax
import jax.numpy as jnp
from jax import lax
from jax.experimental import pallas as pl
from jax.experimental.pallas import tpu as pltpu
from jax.experimental.pallas import tpu_sc as plsc

F32 = jnp.float32
BF16 = jnp.bfloat16
U32 = jnp.uint32

LANES = 128
SUBLANES = 8
VMEM_LIMIT = 56 * 1024 * 1024

D_MODEL = 1024
D_RWKV = 512
RW_HEAD = 64
RW_COLS = 1792
D_S5 = 512
S5_GROUPS = 32
S5_CH = 16
S5_STATE = 64
S5_MODES = S5_GROUPS * S5_STATE
N_MEM = 256
XA_HEADS = 4
XA_HEAD = 256
PEER_HEADS = 8
PEER_NKEYS = 128
PEER_TOPK = 16
PEER_SEL = PEER_HEADS * PEER_TOPK
RMS_EPS = 1e-6
GN_EPS = 64e-5

RW_CHUNK = 128
RW_SEQS_PER_STEP = 1
S5_BLOCK = 256
PROJ_BLOCK = 512
XA_BLOCK = 256
SEL_BLOCK = 128
GATHER_BLOCK = 64


def _params(sem):
    return pltpu.CompilerParams(dimension_semantics=sem, vmem_limit_bytes=VMEM_LIMIT)


def _rms(x, g):
    ms = jnp.mean(x * x, axis=-1, keepdims=True)
    return x * lax.rsqrt(ms + RMS_EPS) * g


def _bdot(a, b):
    return jnp.dot(a.astype(BF16), b.astype(BF16), preferred_element_type=F32)


def _bdot_nt(a, b):
    return lax.dot_general(a.astype(BF16), b.astype(BF16), (((1,), (1,)), ((), ())),
                           preferred_element_type=F32)


def _sigmoid(x):
    return 1.0 / (1.0 + jnp.exp(-x))


def _softplus(x):
    return jnp.maximum(x, 0.0) + jnp.log(1.0 + jnp.exp(-jnp.abs(x)))


def _gelu(x):
    return 0.5 * x * (1.0 + jnp.tanh(math.sqrt(2.0 / math.pi) * (x + 0.044715 * (x * x * x))))


def _full(shape):
    n = len(shape)
    return pl.BlockSpec(shape, lambda *_: (0,) * n)


def _norm_proj_kernel(*refs, n_out):
    x_ref, g_ref = refs[0], refs[1]
    w_refs = refs[2:2 + n_out]
    o_refs = refs[2 + n_out:]
    h = _rms(x_ref[...], g_ref[...]).astype(BF16)
    for w_ref, o_ref in zip(w_refs, o_refs):
        o_ref[...] = jnp.dot(h, w_ref[...], preferred_element_type=F32).astype(o_ref.dtype)


def _norm_proj(x2, g, ws, out_dtypes, block):
    t, d = x2.shape
    block = min(block, t)
    in_specs = [pl.BlockSpec((block, d), lambda i: (i, 0)), _full((1, d))]
    in_specs += [_full(w.shape) for w in ws]
    out_specs = [pl.BlockSpec((block, w.shape[1]), lambda i: (i, 0)) for w in ws]
    out_shape = [jax.ShapeDtypeStruct((t, w.shape[1]), dt) for w, dt in zip(ws, out_dtypes)]
    return pl.pallas_call(
        functools.partial(_norm_proj_kernel, n_out=len(ws)),
        grid=(t // block,), in_specs=in_specs, out_specs=out_specs, out_shape=out_shape,
        compiler_params=_params(("parallel",)),
    )(x2, g.reshape(1, d), *ws)


def _seg_sum(x, mseg):
    hi = x.astype(BF16)
    lo = (x - hi.astype(F32)).astype(BF16)
    return (jnp.dot(hi, mseg, preferred_element_type=F32)
            + jnp.dot(lo, mseg, preferred_element_type=F32))


def _col_bcast(row):
    return jnp.broadcast_to(row, (LANES, LANES)).T


def _rwkv_kernel(*refs, first_layer):
    zprev, hst = refs[-2], refs[-1]

    @pl.when(pl.program_id(1) == 0)
    def _():
        zprev[...] = jnp.zeros_like(zprev)
        hst[...] = jnp.zeros_like(hst)

    for b in range(refs[0].shape[0]):
        _rwkv_chunk(refs, first_layer, b)


def _rwkv_chunk(refs, first_layer, b):
    if first_layer:
        (z_ref, mu_ref, w0_ref, w2_ref, a0_ref, a2_ref, g2_ref, kk_ref, ka_ref, rk_ref,
         lng_ref, lnb_ref, mseg_ref, y_ref, vf_out_ref, zprev, hst) = refs
    else:
        (z_ref, hv_ref, vf_ref, v0_ref, v2_ref, mu_ref, w0_ref, w2_ref, a0_ref, a2_ref, g2_ref,
         kk_ref, ka_ref, rk_ref, lng_ref, lnb_ref, mseg_ref, y_ref, zprev, hst) = refs
    L = RW_CHUNK
    n_pair = D_RWKV // LANES

    z = z_ref[b]
    row = lax.broadcasted_iota(jnp.int32, (L, 1), 0)
    zs = jnp.where(row == 0, zprev[b:b + 1, :], pltpu.roll(z, 1, 0))
    zprev[b:b + 1, :] = z[L - 1:L, :]
    z = z + (zs - z) * mu_ref[...]
    r = z[:, 0:512]
    k = z[:, 512:1024]
    v = z[:, 1024:1536]
    wa = z[:, 1536:1664]
    gd = z[:, 1664:1792]
    mseg = mseg_ref[...]

    wlin = w0_ref[...] + _bdot(jnp.tanh(wa), w2_ref[...])
    lw = -jnp.exp(-_softplus(-wlin) - 0.5)
    a = _sigmoid(a0_ref[...] + _bdot(wa, a2_ref[...]))
    g = _bdot(_sigmoid(gd), g2_ref[...])
    if first_layer:
        vf_out_ref[b] = v
    else:
        v = v + (vf_ref[b] - v) * _sigmoid(v0_ref[...] + _bdot(hv_ref[b], v2_ref[...]))
    kk = k * kk_ref[...]
    kk = kk / jnp.maximum(jnp.sqrt(_seg_sum(kk * kk, mseg)), 1e-12)
    k2 = k * (1.0 + (a - 1.0) * ka_ref[...])
    av = -kk
    bv = kk * a

    ti = lax.broadcasted_iota(jnp.int32, (L, L), 0)
    si = lax.broadcasted_iota(jnp.int32, (L, L), 1)
    tril = (ti >= si).astype(F32)
    cum = jnp.dot(tril, lw, preferred_element_type=F32, precision=lax.Precision.HIGHEST)
    mid = cum[L // 2 - 1:L // 2, :]
    cm = cum - mid
    ecum = jnp.exp(cm)
    einv = jnp.exp(-cm)
    rt = r * ecum
    kt = k2 * einv
    bt = bv * einv
    at = av * jnp.exp(cm - lw)
    p_end = jnp.exp(cum[L - 1:L, :])
    e_end = ecum[L - 1:L, :]
    e_mid = jnp.exp(mid)

    lane = lax.broadcasted_iota(jnp.int32, (1, LANES), 1)
    m0 = (lane < RW_HEAD).astype(F32)
    m1 = 1.0 - m0
    strict = ti > si
    incl = ti >= si
    bi = lax.broadcasted_iota(jnp.int32, (LANES, LANES), 0) < RW_HEAD
    bj = lax.broadcasted_iota(jnp.int32, (LANES, LANES), 1) < RW_HEAD
    bdmask = (bi == bj).astype(F32)
    zeros_ll = jnp.zeros((L, L), F32)

    ys = []
    for p in range(D_RWKV // LANES):
        sl = slice(LANES * p, LANES * (p + 1))
        A, B, K, R, V = at[:, sl], bt[:, sl], kt[:, sl], rt[:, sl], v[:, sl]
        btkt = jnp.concatenate([B.T, K.T], axis=1)
        h0 = hst[b * n_pair + p]
        h0m = h0 * _col_bcast(e_mid[:, sl])
        sc = _bdot(jnp.concatenate([A * m0, A * m1, R * m0, R * m1], axis=0), btkt)
        aab = [jnp.where(strict, sc[e * L:(e + 1) * L, 0:L], 0.0) for e in range(2)]
        aak = [jnp.where(strict, sc[e * L:(e + 1) * L, L:2 * L], 0.0) for e in range(2)]
        arb = [jnp.where(incl, sc[(2 + e) * L:(3 + e) * L, 0:L], 0.0) for e in range(2)]
        ark = [jnp.where(incl, sc[(2 + e) * L:(3 + e) * L, L:2 * L], 0.0) for e in range(2)]
        arh = _bdot(jnp.concatenate([A, R], axis=0), h0m)
        v01 = jnp.concatenate([V * m0, V * m1], axis=0)
        x = arh[0:L] + _bdot(jnp.concatenate(aak, axis=1), v01)
        pm = jnp.concatenate(aab, axis=1)
        n_fac = int(math.log2(L))
        for it in range(n_fac):
            x = x + _bdot(pm, jnp.concatenate([x * m0, x * m1], axis=0))
            if it + 1 < n_fac:
                pd = jnp.concatenate(
                    [jnp.concatenate([pm[:, 0:L], zeros_ll], axis=1),
                     jnp.concatenate([zeros_ll, pm[:, L:2 * L]], axis=1)], axis=0)
                pm = _bdot(pm, pd)
        u = x
        yp = arh[L:2 * L] + _bdot(jnp.concatenate(arb + ark, axis=1),
                                  jnp.concatenate([u * m0, u * m1, v01], axis=0))
        upd = _bdot(btkt, jnp.concatenate([u, V], axis=0))
        hst[b * n_pair + p] = (h0 * _col_bcast(p_end[:, sl]) + upd * _col_bcast(e_end[:, sl])) * bdmask
        ys.append(yp)
    y = jnp.concatenate(ys, axis=1)

    mean = _seg_sum(y, mseg) * (1.0 / RW_HEAD)
    d = y - mean
    var = _seg_sum(d * d, mseg) * (1.0 / RW_HEAD)
    yn = d * lax.rsqrt(var + GN_EPS) * lng_ref[...] + lnb_ref[...]
    bonus = _seg_sum(r * k2 * rk_ref[...], mseg) * v
    y_ref[b] = (yn + bonus) * g


def _pad_rows(w, rows, offset):
    out = jnp.zeros((rows, w.shape[1]), w.dtype)
    return out.at[offset:offset + w.shape[0]].set(w)


def _rwkv(z_rw, hv, v_first, prm, bsz, seq):
    L = RW_CHUNK
    first = v_first is None
    row = lambda a: a.reshape(1, -1).astype(F32)
    hid = jnp.arange(D_RWKV) // RW_HEAD
    mseg = (hid[:, None] == hid[None, :]).astype(BF16)
    w2p = _pad_rows(prm['w2'], LANES, 0).astype(BF16)
    a2p = _pad_rows(prm['a2'], LANES, 64).astype(BF16)
    common = [row(prm['mu']), row(prm['w0']), w2p, row(prm['a0']), a2p, prm['g2'].astype(BF16),
              row(prm['kk']), row(prm['ka']), row(prm['rk']), row(prm['lng']), row(prm['lnb']), mseg]
    per = RW_SEQS_PER_STEP if bsz % RW_SEQS_PER_STEP == 0 else 1
    tok = lambda w: pl.BlockSpec((per, L, w), lambda b, t: (b, t, 0))
    common_specs = [_full(c.shape) for c in common]
    y_shape = jax.ShapeDtypeStruct((bsz, seq, D_RWKV), F32)
    scratch = [pltpu.VMEM((per, RW_COLS), F32), pltpu.VMEM((per * D_RWKV // LANES, LANES, LANES), F32)]
    if first:
        args = [z_rw] + common
        in_specs = [tok(RW_COLS)] + common_specs
        out_shape = [y_shape, y_shape]
        out_specs = [tok(D_RWKV), tok(D_RWKV)]
    else:
        v2p = _pad_rows(prm['v2'], LANES, 0).astype(BF16)
        extra = [row(prm['v0']), v2p]
        args = [z_rw, hv, v_first] + extra + common
        in_specs = [tok(RW_COLS), tok(LANES), tok(D_RWKV)] + [_full(c.shape) for c in extra] + common_specs
        out_shape = [y_shape]
        out_specs = [tok(D_RWKV)]
    outs = pl.pallas_call(
        functools.partial(_rwkv_kernel, first_layer=first),
        grid=(bsz // per, seq // L), in_specs=in_specs, out_specs=out_specs, out_shape=out_shape,
        scratch_shapes=scratch, compiler_params=_params(("parallel", "arbitrary")),
    )(*args)
    return (outs[0], outs[1]) if first else (outs[0], v_first)


def _s5_kernel(u_ref, wb_ref, wc_ref, lpr_ref, lpi_ref, d_ref, gw_ref, gb_ref, og_ref, o_ref,
               car_re, car_im, xre, xim):
    tb = u_ref.shape[1]

    @pl.when(pl.program_id(1) == 0)
    def _():
        car_re[...] = jnp.zeros_like(car_re)
        car_im[...] = jnp.zeros_like(car_im)

    u = u_ref[0]
    bu = _bdot(u, wb_ref[...])
    xre[...] = bu[:, 0:S5_MODES]
    xim[...] = bu[:, S5_MODES:2 * S5_MODES]
    row = lax.broadcasted_iota(jnp.int32, (SUBLANES, 1), 0)

    def tile(i, carry):
        cr, ci = carry
        rs = pl.ds(pl.multiple_of(i * SUBLANES, SUBLANES), SUBLANES)
        br, bi = xre[rs, :], xim[rs, :]
        for dist in (1, 2, 4):
            keep = row >= dist
            sr = jnp.where(keep, pltpu.roll(br, dist, 0), 0.0)
            si = jnp.where(keep, pltpu.roll(bi, dist, 0), 0.0)
            lr = lpr_ref[dist - 1:dist, :]
            li = lpi_ref[dist - 1:dist, :]
            br, bi = br + lr * sr - li * si, bi + lr * si + li * sr
        pr, pi = lpr_ref[...], lpi_ref[...]
        xr = br + pr * cr - pi * ci
        xi = bi + pr * ci + pi * cr
        xre[rs, :] = xr
        xim[rs, :] = xi
        return xr[SUBLANES - 1:SUBLANES, :], xi[SUBLANES - 1:SUBLANES, :]

    cr, ci = lax.fori_loop(0, tb // SUBLANES, tile, (car_re[...], car_im[...]))
    car_re[...] = cr
    car_im[...] = ci
    wc = wc_ref[...]
    y = _bdot(xre[...], wc[0:S5_MODES]) + _bdot(xim[...], wc[S5_MODES:2 * S5_MODES])
    y = _gelu(y + d_ref[...] * u)
    y = y * _sigmoid(_bdot(y, gw_ref[...]) + gb_ref[...])
    o_ref[0] = _rms(y, og_ref[...])


def _s5_weights(a_re, a_im, log_dt, b_re, b_im, c_re, c_im):
    lam_re = jnp.minimum(a_re.astype(F32), -1e-4)
    lam_im = a_im.astype(F32)
    dt = jnp.exp(log_dt.astype(F32))[:, None]
    mag = jnp.exp(lam_re * dt)
    lb_re = mag * jnp.cos(lam_im * dt)
    lb_im = mag * jnp.sin(lam_im * dt)
    den = lam_re * lam_re + lam_im * lam_im
    c1_re = ((lb_re - 1.0) * lam_re + lb_im * lam_im) / den
    c1_im = (lb_im * lam_re - (lb_re - 1.0) * lam_im) / den
    br, bi = b_re.astype(F32), b_im.astype(F32)
    bb_re = c1_re[..., None] * br - c1_im[..., None] * bi
    bb_im = c1_re[..., None] * bi + c1_im[..., None] * br
    eye = jnp.eye(S5_GROUPS, dtype=F32)
    wb_re = jnp.einsum('gpc,gh->gchp', bb_re, eye).reshape(D_S5, S5_MODES)
    wb_im = jnp.einsum('gpc,gh->gchp', bb_im, eye).reshape(D_S5, S5_MODES)
    wb = jnp.concatenate([wb_re, wb_im], axis=1).astype(BF16)
    wc_re = jnp.einsum('gcp,gh->gphc', c_re.astype(F32), eye).reshape(S5_MODES, D_S5)
    wc_im = jnp.einsum('gcp,gh->gphc', c_im.astype(F32), eye).reshape(S5_MODES, D_S5)
    wc = jnp.concatenate([wc_re, -wc_im], axis=0).astype(BF16)
    pr, pi = [lb_re], [lb_im]
    for _ in range(SUBLANES - 1):
        pr, pi = pr + [pr[-1] * lb_re - pi[-1] * lb_im], pi + [pr[-1] * lb_im + pi[-1] * lb_re]
    lp_re = jnp.stack(pr).reshape(SUBLANES, S5_MODES)
    lp_im = jnp.stack(pi).reshape(SUBLANES, S5_MODES)
    return wb, wc, lp_re, lp_im


def _s5(u, prm, bsz, seq):
    tb = min(S5_BLOCK, seq)
    wb, wc, lp_re, lp_im = _s5_weights(prm['a_re'], prm['a_im'], prm['log_dt'], prm['b_re'], prm['b_im'],
                                       prm['c_re'], prm['c_im'])
    row = lambda a: a.reshape(1, -1).astype(F32)
    consts = [wb, wc, lp_re, lp_im, row(prm['d']), prm['glu_w'].astype(BF16), row(prm['glu_b']),
              row(prm['out_g'])]
    tok = pl.BlockSpec((1, tb, D_S5), lambda b, t: (b, t, 0))
    return pl.pallas_call(
        _s5_kernel, grid=(bsz, seq // tb),
        in_specs=[tok] + [_full(c.shape) for c in consts], out_specs=tok,
        out_shape=jax.ShapeDtypeStruct((bsz, seq, D_S5), F32),
        scratch_shapes=[pltpu.VMEM((1, S5_MODES), F32), pltpu.VMEM((1, S5_MODES), F32),
                        pltpu.VMEM((tb, S5_MODES), F32), pltpu.VMEM((tb, S5_MODES), F32)],
        compiler_params=_params(("parallel", "arbitrary")),
    )(u, *consts)


def _mix_xattn_kernel(x_ref, yr_ref, ys_ref, wo1_ref, wo2_ref, g_ref, wq_ref, k_ref, v_ref, wo_ref, o_ref):
    x1 = x_ref[0] + _bdot(yr_ref[0], wo1_ref[...]) + _bdot(ys_ref[0], wo2_ref[...])
    h = _rms(x1, g_ref[...])
    q = _bdot(h, wq_ref[...])
    km, vm = k_ref[0], v_ref[0]
    outs = []
    for hd in range(XA_HEADS):
        sl = slice(XA_HEAD * hd, XA_HEAD * (hd + 1))
        s = _bdot_nt(q[:, sl], km[:, sl]) * (XA_HEAD ** -0.5)
        s = s - jnp.max(s, axis=-1, keepdims=True)
        e = jnp.exp(s)
        p = e / jnp.sum(e, axis=-1, keepdims=True)
        outs.append(_bdot(p, vm[:, sl]))
    o = jnp.concatenate(outs, axis=1)
    o_ref[0] = x1 + _bdot(o, wo_ref[...])


def _mix_xattn(x, y_rw, y_s5, w_out, g, wq, kmem, vmem, wo, bsz, seq):
    tm = min(XA_BLOCK, seq)
    consts_a = [w_out[:D_RWKV].astype(BF16), w_out[D_RWKV:].astype(BF16), g.reshape(1, -1).astype(F32),
                wq.astype(BF16)]
    tok = lambda w: pl.BlockSpec((1, tm, w), lambda b, t: (b, t, 0))
    mem = pl.BlockSpec((1, N_MEM, D_MODEL), lambda b, t: (b, 0, 0))
    wo_b = wo.astype(BF16)
    return pl.pallas_call(
        _mix_xattn_kernel, grid=(bsz, seq // tm),
        in_specs=[tok(D_MODEL), tok(D_RWKV), tok(D_S5)] + [_full(c.shape) for c in consts_a]
                 + [mem, mem, _full(wo_b.shape)],
        out_specs=tok(D_MODEL), out_shape=jax.ShapeDtypeStruct((bsz, seq, D_MODEL), F32),
        compiler_params=_params(("parallel", "parallel")),
    )(x, y_rw, y_s5, *consts_a, kmem, vmem, wo_b)


def _top_rows(work, order, aux, val_ref, idx_ref):
    for it in range(PEER_TOPK):
        m = jnp.max(work, axis=0, keepdims=True)
        pos = jnp.min(jnp.where(work == m, order, jnp.inf), axis=0, keepdims=True)
        hit = order == pos
        val_ref[it:it + 1, :] = m
        if aux is None:
            idx_ref[it:it + 1, :] = pos
        else:
            idx_ref[it:it + 1, :] = jnp.sum(jnp.where(hit, aux, 0.0), axis=0, keepdims=True)
        work = jnp.where(hit, -jnp.inf, work)


_CAND_ROW_BLOCKS = [(0, PEER_TOPK), (1, SUBLANES), (2, SUBLANES), (3, SUBLANES)]
_CAND_COL_BLOCKS = [(0, PEER_TOPK, 4, 15), (1, SUBLANES, 4, 7), (2, SUBLANES, 4, 4)]
N_CAND = sum(n for _, n in _CAND_ROW_BLOCKS) + sum(n for _, n, _, _ in _CAND_COL_BLOCKS)


def _cand_consts(tm):
    flat, neg = [], []
    for a, nb in _CAND_ROW_BLOCKS:
        flat += [a * PEER_TOPK + b for b in range(nb)]
        neg += [0.0] * nb
    for b, na, lo, hi in _CAND_COL_BLOCKS:
        flat += [a * PEER_TOPK + b for a in range(na)]
        neg += [0.0 if lo <= a <= hi else -float('inf') for a in range(na)]
    flat = [f if n == 0.0 else 1000.0 + i for i, (f, n) in enumerate(zip(flat, neg))]
    col = lambda v: jnp.broadcast_to(jnp.asarray(v, F32)[:, None], (N_CAND, tm))
    return col(flat), col(neg)


def _cand_rows(row_vals, col_vals, combine):
    blocks = [combine(row_vals[a:a + 1, :], col_vals[0:nb, :]) for a, nb in _CAND_ROW_BLOCKS]
    blocks += [combine(row_vals[0:na, :], col_vals[b:b + 1, :]) for b, na, _, _ in _CAND_COL_BLOCKS]
    return jnp.concatenate(blocks, axis=0)


SEL_HEADS_PER_STEP = 4


def _peer_select_kernel(x_ref, g_ref, wq_ref, keys_ref, cflat_ref, cneg_ref, x8_ref, h8_ref, base_ref, shift_ref,
                        gate_ref, q3, idx_t, gate_t, *lists):
    tm = x_ref.shape[0]
    n = SEL_HEADS_PER_STEP
    s1, i1, s2, i2, top, eid = (lists[j * n:(j + 1) * n] for j in range(6))
    x = x_ref[...]
    h = _rms(x, g_ref[...])
    for r in range(D_MODEL // LANES):
        rows = pl.ds(r, tm, stride=D_MODEL // LANES)
        h8_ref[rows, :] = h[:, LANES * r:LANES * (r + 1)]
        x8_ref[rows, :] = x[:, LANES * r:LANES * (r + 1)]
    q = _bdot(h, wq_ref[...])
    for j in range(2 * PEER_HEADS):
        q3[j] = q[:, LANES * j:LANES * (j + 1)]
    iota_k = lax.broadcasted_iota(jnp.int32, (PEER_NKEYS, tm), 0).astype(F32)

    def heads(step, _):
        for u in range(SEL_HEADS_PER_STEP):
            hd = step * SEL_HEADS_PER_STEP + u
            sc1 = _bdot_nt(keys_ref[2 * hd], q3[2 * hd])
            sc2 = _bdot_nt(keys_ref[2 * hd + 1], q3[2 * hd + 1])
            _top_rows(sc1, iota_k, None, s1[u], i1[u])
            _top_rows(sc2, iota_k, None, s2[u], i2[u])
            cand = _cand_rows(s1[u][...], s2[u][...], lambda x, y: x + y) + cneg_ref[...]
            cidx = _cand_rows(i1[u][...], i2[u][...], lambda x, y: x * float(PEER_NKEYS) + y)
            _top_rows(cand, cflat_ref[...], cidx, top[u], eid[u])
            tv = top[u][...]
            e = jnp.exp(tv - jnp.max(tv, axis=0, keepdims=True))
            rs = pl.ds(pl.multiple_of(hd * PEER_TOPK, PEER_TOPK), PEER_TOPK)
            idx_t[rs, :] = eid[u][...]
            gate_t[rs, :] = e / jnp.sum(e, axis=0, keepdims=True)
        return 0

    lax.fori_loop(0, PEER_HEADS // SEL_HEADS_PER_STEP, heads, 0)
    e_t = idx_t[...].T
    pair = jnp.floor(e_t * 0.5)
    base_ref[...] = (pair * float(SUBLANES)).astype(jnp.int32)
    shift_ref[...] = ((e_t - 2.0 * pair) * 16.0).astype(jnp.int32)
    gate_ref[...] = gate_t[...].T


def _peer_select(x2, g, wq, keys):
    t = x2.shape[0]
    tm = SEL_BLOCK
    keys_b = keys.reshape(2 * PEER_HEADS, PEER_NKEYS, LANES).astype(BF16)
    wq_b = wq.astype(BF16)
    cflat, cneg = _cand_consts(tm)
    tokspec = lambda w: pl.BlockSpec((tm, w), lambda i: (i, 0))
    vm = lambda r: pltpu.VMEM((r, tm), F32)
    return pl.pallas_call(
        _peer_select_kernel, grid=(t // tm,),
        in_specs=[tokspec(D_MODEL), _full((1, D_MODEL)), _full(wq_b.shape), _full(keys_b.shape),
                  _full(cflat.shape), _full(cneg.shape)],
        out_specs=[pl.BlockSpec((tm * SUBLANES, LANES), lambda i: (i, 0))] * 2
                  + [tokspec(PEER_SEL), tokspec(PEER_SEL), tokspec(PEER_SEL)],
        out_shape=[jax.ShapeDtypeStruct((t * SUBLANES, LANES), F32)] * 2 + [
                   jax.ShapeDtypeStruct((t, PEER_SEL), jnp.int32),
                   jax.ShapeDtypeStruct((t, PEER_SEL), jnp.int32), jax.ShapeDtypeStruct((t, PEER_SEL), F32)],
        scratch_shapes=[pltpu.VMEM((2 * PEER_HEADS, tm, LANES), F32), vm(PEER_SEL), vm(PEER_SEL)]
                       + [vm(PEER_TOPK) for _ in range(6 * SEL_HEADS_PER_STEP)],
        compiler_params=_params(("parallel",)),
    )(x2, g.reshape(1, -1).astype(F32), wq_b, keys_b, cflat, cneg)


TABLE_PREP_ROWS = 256


def _table_prep_kernel(tab_ref, tiles_ref, packed_ref):
    n = tab_ref.shape[0]
    n_chunk = D_MODEL // LANES
    x = tab_ref[...]
    for r in range(n_chunk):
        tiles_ref[pl.ds(r, n, stride=n_chunk), :] = x[:, LANES * r:LANES * (r + 1)]
    pairs = tiles_ref[...].reshape(n // 2, 2 * n_chunk, LANES)
    as_bits = lambda t: lax.bitcast_convert_type(t.astype(BF16).astype(F32), jnp.int32)
    word = as_bits(pairs[:, 0:n_chunk, :]) | lax.shift_right_logical(as_bits(pairs[:, n_chunk:, :]), 16)
    packed_ref[...] = word.reshape(n // 2 * n_chunk, LANES)


def _table_prep(tab):
    rows = tab.shape[0] * tab.shape[1]
    n = TABLE_PREP_ROWS
    n_chunk = D_MODEL // LANES
    tiles, packed = pl.pallas_call(
        _table_prep_kernel, grid=(rows // n,),
        in_specs=[pl.BlockSpec((n, D_MODEL), lambda i: (i, 0))],
        out_specs=[pl.BlockSpec((n * n_chunk, LANES), lambda i: (i, 0)),
                   pl.BlockSpec((n // 2 * n_chunk, LANES), lambda i: (i, 0))],
        out_shape=[jax.ShapeDtypeStruct((rows * n_chunk, LANES), F32),
                   jax.ShapeDtypeStruct((rows // 2 * n_chunk, LANES), jnp.int32)],
        compiler_params=_params(("parallel",)),
    )(tab.reshape(rows, D_MODEL))
    return tiles.reshape(rows, n_chunk, LANES), packed


def _splat_into(src_ref, t, dst_ref, slot):
    tile = jnp.broadcast_to(src_ref[pl.ds(t, 1), :], (LANES, LANES)).T
    dst_ref[LANES * slot:LANES * (slot + 1), :] = tile


def _bcast_row(ref, row):
    return jnp.broadcast_to(ref[row:row + 1, :], (SUBLANES, LANES))


def _expert_tile(tab_ref, base, shift_splat, row):
    w = tab_ref[pl.ds(pl.multiple_of(base, SUBLANES), SUBLANES), :]
    return lax.bitcast_convert_type((w << _bcast_row(shift_splat, row)) & jnp.int32(-65536), F32)


def _token_rows(t):
    return pl.ds(pl.multiple_of(t * SUBLANES, SUBLANES), SUBLANES)


def _pipelined_tokens(tg, prep, work):
    prep(0, 0)

    def body(i, _):
        t0 = 2 * i
        prep(t0 + 1, 1)
        work(t0, 0)
        prep(jnp.minimum(t0 + 2, tg - 1), 0)
        work(t0 + 1, 1)
        return 0

    lax.fori_loop(0, tg // 2, body, 0)


def _table_spec(packed, layer, n_layers):
    rows = packed.shape[0] // n_layers
    return pl.BlockSpec((rows, LANES), lambda i: (layer, 0), pipeline_mode=pl.Buffered(1))


N_CHUNK = D_MODEL // LANES
CHUNK_STRIDE = PEER_SEL + SUBLANES


def _peer_u_kernel(base_ref, shift_ref, h_ref, gate_ref, tab_ref, eye_ref, c_ref, shift_splat, *planes):
    tg = gate_ref.shape[0]
    ones = jnp.ones((LANES, LANES), BF16)

    half = N_CHUNK // 2

    def prep(t, slot):
        _splat_into(shift_ref, t, shift_splat, slot)

    def gather(t, slot):
        ht = h_ref[_token_rows(t), :]
        for kx in range(PEER_SEL):
            prod = _expert_tile(tab_ref, base_ref[t, kx], shift_splat, LANES * slot + kx) * ht
            fold = prod + pltpu.roll(prod, half, 0)
            planes[slot][pl.ds(kx, half, stride=CHUNK_STRIDE), :] = fold[0:half, :]

    def finish(t, slot):
        plane = planes[slot]
        acc = plane[0:PEER_SEL, :]
        for r in range(1, half):
            acc = acc + plane[CHUNK_STRIDE * r:CHUNK_STRIDE * r + PEER_SEL, :]
        hi = acc.astype(BF16)
        lo = (acc - hi.astype(F32)).astype(BF16)
        tot = jnp.dot(hi, ones, preferred_element_type=F32) + jnp.dot(lo, ones, preferred_element_type=F32)
        score = jnp.sum(tot * eye_ref[...], axis=0, keepdims=True)
        c_ref[pl.ds(t, 1), :] = gate_ref[pl.ds(t, 1), :] * _gelu(score)

    planes[1][...] = jnp.zeros_like(planes[1])
    prep(0, 0)

    def body(i, _):
        t0 = 2 * i
        prep(t0 + 1, 1)
        gather(t0, 0)
        finish(jnp.maximum(t0 - 1, 0), 1)
        prep(jnp.minimum(t0 + 2, tg - 1), 0)
        gather(t0 + 1, 1)
        finish(t0, 0)
        return 0

    lax.fori_loop(0, tg // 2, body, 0)
    finish(tg - 1, 1)


def _peer_u(base, shift, h8, gate, tab, t, layer, n_layers):
    tg = GATHER_BLOCK
    eye = jnp.eye(LANES, dtype=F32)
    tokrow = pl.BlockSpec((tg, PEER_SEL), lambda i: (i, 0))
    return pl.pallas_call(
        _peer_u_kernel, grid=(t // tg,),
        in_specs=[pl.BlockSpec((tg, PEER_SEL), lambda i: (i, 0), memory_space=pltpu.SMEM), tokrow,
                  pl.BlockSpec((tg * SUBLANES, LANES), lambda i: (i, 0)), tokrow,
                  _table_spec(tab, layer, n_layers), _full(eye.shape)],
        out_specs=tokrow, out_shape=jax.ShapeDtypeStruct((t, PEER_SEL), F32),
        scratch_shapes=[pltpu.VMEM((2 * LANES, LANES), jnp.int32)]
                       + [pltpu.VMEM((N_CHUNK // 2 * CHUNK_STRIDE, LANES), F32)] * 2,
        compiler_params=_params(("parallel",)),
    )(base, shift, h8, gate, tab, eye)


def _peer_v_kernel(base_ref, shift_ref, c_ref, x_ref, tab_ref, _aliased_out, o_ref, shift_splat, c_splat, otile):
    tg = c_ref.shape[0]
    n_acc = 4

    def prep(t, slot):
        _splat_into(shift_ref, t, shift_splat, slot)
        _splat_into(c_ref, t, c_splat, slot)

    def work(t, slot):
        accs = [jnp.zeros((SUBLANES, LANES), F32) for _ in range(n_acc)]
        for kx in range(PEER_SEL):
            row = LANES * slot + kx
            accs[kx % n_acc] = accs[kx % n_acc] + (_bcast_row(c_splat, row)
                                                   * _expert_tile(tab_ref, base_ref[t, kx], shift_splat, row))
        rs = _token_rows(t)
        otile[rs, :] = x_ref[rs, :] + ((accs[0] + accs[1]) + (accs[2] + accs[3]))

    _pipelined_tokens(tg, prep, work)
    for r in range(N_CHUNK):
        o_ref[:, LANES * r:LANES * (r + 1)] = otile[pl.ds(r, tg, stride=N_CHUNK), :]


def _peer_v(base, shift, c, x8, x2, tab, t, layer, n_layers):
    tg = GATHER_BLOCK
    smem = pl.BlockSpec((tg, PEER_SEL), lambda i: (i, 0), memory_space=pltpu.SMEM)
    tokrow = pl.BlockSpec((tg, PEER_SEL), lambda i: (i, 0))
    tile = pl.BlockSpec((tg * SUBLANES, LANES), lambda i: (i, 0))
    return pl.pallas_call(
        _peer_v_kernel, grid=(t // tg,),
        in_specs=[smem, tokrow, tokrow, tile, _table_spec(tab, layer, n_layers),
                  pl.BlockSpec(memory_space=pl.ANY)],
        out_specs=pl.BlockSpec((tg, D_MODEL), lambda i: (i, 0)),
        out_shape=jax.ShapeDtypeStruct(x2.shape, F32),
        input_output_aliases={5: 0},
        scratch_shapes=[pltpu.VMEM((2 * LANES, LANES), jnp.int32), pltpu.VMEM((2 * LANES, LANES), F32),
                        pltpu.VMEM((tg * SUBLANES, LANES), F32)],
        compiler_params=_params(("parallel",)),
    )(base, shift, c, x8, tab, x2)


SC_WORKERS = 32
SC_LANES = 16
SC_ROWS = 32
SC_SHARE_PIPELINED = 40 / 64
SC_SHARE_TAIL = (26 / 64,)
SEQ_GROUPS = 4
SC_GROUP = 8
SC_REGS = 32


def _peer_sc(eidx, gate, h8, x8, tab_u, tab_v, t_off):
    ts = eidx.shape[0]
    per_w = ts // SC_WORKERS
    n_chunk = PEER_SEL // SC_ROWS
    grp = SC_GROUP
    sel = grp * PEER_SEL
    n_pairs = grp * n_chunk // 2
    per_tile = LANES // SC_LANES
    n_blk = D_MODEL // (SC_REGS * SC_LANES)
    mesh = plsc.VectorSubcoreMesh(core_axis_name="c", subcore_axis_name="s")

    def piece(ref, lead, q):
        return ref[lead, q // per_tile, pl.ds(SC_LANES * (q % per_tile), SC_LANES)]

    def tok_piece(ref, tok, q):
        return ref[tok * N_CHUNK + q // per_tile, pl.ds(SC_LANES * (q % per_tile), SC_LANES)]

    def body(u_hbm, v_hbm, idx_hbm, gate_hbm, h_hbm, x_hbm, out_hbm,
             idx_v, c_v, h_v, acc_v, part_v, rows0, rows1, sem0, sem1):
        wid = lax.axis_index("s") * 2 + lax.axis_index("c")
        bufs, sems = (rows0, rows1), (sem0, sem1)
        lane = lax.iota(jnp.int32, SC_LANES)

        def stream(tab_hbm, compute):
            def gather(chunk, slot):
                start = pl.multiple_of(chunk * SC_ROWS, SC_ROWS)
                return pltpu.make_async_copy(tab_hbm.at[idx_v.at[pl.ds(start, SC_ROWS)]], bufs[slot], sems[slot])

            gather(0, 0).start()

            @pl.loop(0, n_pairs)
            def _(p):
                c0 = 2 * p
                gather(c0 + 1, 1).start()
                gather(c0, 0).wait()
                compute(c0, 0)
                gather(jnp.minimum(c0 + 2, 2 * n_pairs - 1), 0).start()
                gather(c0 + 1, 1).wait()
                compute(c0 + 1, 1)

            gather(0, 0).wait()

        def dots(chunk, slot):
            tok = chunk // n_chunk
            for blk in range(n_blk):
                where = [blk * SC_REGS + j for j in range(SC_REGS)]
                hs = [tok_piece(h_v, tok, q) for q in where]

                def row(k, carry):
                    parts = [None] * 4
                    for i, q in enumerate(where):
                        term = piece(bufs[slot], k, q) * hs[i]
                        parts[i % 4] = term if parts[i % 4] is None else parts[i % 4] + term
                    tot = (parts[0] + parts[1]) + (parts[2] + parts[3])
                    at = pl.ds(pl.multiple_of((chunk * SC_ROWS + k) * SC_LANES, SC_LANES), SC_LANES)
                    if blk == 0:
                        part_v[at] = tot
                    else:
                        part_v[at] = part_v[at] + tot
                    return carry

                lax.fori_loop(0, SC_ROWS, row, 0)

        def weigh(chunk, slot):
            tok = chunk // n_chunk
            for blk in range(n_blk):
                where = [blk * SC_REGS + j for j in range(SC_REGS)]

                def row(k, accs):
                    ck = plsc.load_gather(c_v, [jnp.full((SC_LANES,), chunk * SC_ROWS, jnp.int32) + k])
                    return tuple(a + ck * piece(bufs[slot], k, q) for a, q in zip(accs, where))

                accs = lax.fori_loop(0, SC_ROWS, row, tuple(tok_piece(acc_v, tok, q) for q in where))
                for a, q in zip(accs, where):
                    acc_v[tok * N_CHUNK + q // per_tile, pl.ds(SC_LANES * (q % per_tile), SC_LANES)] = a

        @pl.loop(0, per_w // grp)
        def _(g):
            t0 = wid * per_w + g * grp
            flat = pl.ds(pl.multiple_of(t0 * PEER_SEL, sel), sel)
            pltpu.sync_copy(idx_hbm.at[flat], idx_v)
            pltpu.sync_copy(gate_hbm.at[flat], c_v)
            tok_rows = grp * N_CHUNK
            src_rows = pl.ds(pl.multiple_of((t_off + t0) * N_CHUNK, tok_rows), tok_rows)
            pltpu.sync_copy(h_hbm.at[src_rows], h_v)
            pltpu.sync_copy(x_hbm.at[src_rows], acc_v)
            stream(u_hbm, dots)

            @pl.loop(0, sel // SC_LANES)
            def _(m):
                base = (m * SC_LANES + lane) * SC_LANES
                score = plsc.load_gather(part_v, [base])
                for l in range(1, SC_LANES):
                    score = score + plsc.load_gather(part_v, [base + l])
                z = math.sqrt(2.0 / math.pi) * (score + 0.044715 * (score * score * score))
                tanh_z = 1.0 - 2.0 / (jnp.exp(2.0 * z) + 1.0)
                at = pl.ds(pl.multiple_of(m * SC_LANES, SC_LANES), SC_LANES)
                c_v[at] = c_v[at] * (0.5 * score * (1.0 + tanh_z))

            stream(v_hbm, weigh)
            pltpu.sync_copy(acc_v, out_hbm.at[pl.ds(pl.multiple_of(t0 * N_CHUNK, tok_rows), tok_rows)])

    tile = lambda n: pltpu.VMEM((n, N_CHUNK, LANES), F32)
    tok_tiles = pltpu.VMEM((grp * N_CHUNK, LANES), F32)
    return pl.kernel(
        body, mesh=mesh, out_type=jax.ShapeDtypeStruct((ts * N_CHUNK, LANES), F32),
        scratch_types=[pltpu.VMEM((sel,), jnp.int32), pltpu.VMEM((sel,), F32), tok_tiles, tok_tiles,
                       pltpu.VMEM((sel * SC_LANES,), F32), tile(SC_ROWS), tile(SC_ROWS),
                       pltpu.SemaphoreType.DMA, pltpu.SemaphoreType.DMA],
        compiler_params=pltpu.CompilerParams(needs_layout_passes=False, use_tc_tiling_on_sc=True),
    )(tab_u, tab_v, eidx.reshape(-1), gate.reshape(-1), h8, x8)


def _final_norm_kernel(x_ref, g_ref, o_ref):
    o_ref[...] = _rms(x_ref[...], g_ref[...])


def _final_norm(x2, g):
    t, d = x2.shape
    tm = min(PROJ_BLOCK, t)
    spec = pl.BlockSpec((tm, d), lambda i: (i, 0))
    return pl.pallas_call(
        _final_norm_kernel, grid=(t // tm,), in_specs=[spec, _full((1, d))], out_specs=spec,
        out_shape=jax.ShapeDtypeStruct((t, d), F32), compiler_params=_params(("parallel",)),
    )(x2, g.reshape(1, d).astype(F32))


def kernel(x, mem, norm_mix, w_in, rw_mu, rw_w0, rw_w2, rw_a0, rw_a2, rw_g2, rw_kk, rw_ka, rw_rk, rw_v0, rw_v1, rw_v2, rw_lnx_g, rw_lnx_b, s5_a_re, s5_a_im, s5_log_dt, s5_b_re, s5_b_im, s5_c_re, s5_c_im, s5_d, s5_glu_w, s5_glu_b, s5_out_g, w_out, norm_xa, norm_mem, xa_wq, xa_wk, xa_wv, xa_wo, norm_ffn, peer_wq, peer_keys, peer_u, peer_v, norm_final):
    bsz, seq, d = x.shape
    depth = w_in.shape[0]

    def layer(l, xg, memg, v_first, tabs, sc_share):
        nb = xg.shape[0]
        t = nb * seq
        x2 = xg.reshape(t, d)
        ws = [w_in[l][:, :RW_COLS].astype(BF16), w_in[l][:, RW_COLS:].astype(BF16)]
        if l > 0:
            ws.append(_pad_rows(rw_v1[l - 1].T, LANES, 0).T.astype(BF16))
        outs = _norm_proj(x2, norm_mix[l], ws, [F32] * len(ws), PROJ_BLOCK)
        z_rw = outs[0].reshape(nb, seq, RW_COLS)
        u_s5 = outs[1].reshape(nb, seq, D_S5)
        hv = outs[2].reshape(nb, seq, LANES) if l > 0 else None
        rw_prm = dict(mu=rw_mu[l], w0=rw_w0[l], w2=rw_w2[l], a0=rw_a0[l], a2=rw_a2[l], g2=rw_g2[l],
                      kk=rw_kk[l], ka=rw_ka[l], rk=rw_rk[l], lng=rw_lnx_g[l], lnb=rw_lnx_b[l])
        if l > 0:
            rw_prm.update(v0=rw_v0[l - 1], v2=rw_v2[l - 1])
        y_rw, v_first = _rwkv(z_rw, hv, v_first, rw_prm, nb, seq)
        s5_prm = dict(a_re=s5_a_re[l], a_im=s5_a_im[l], log_dt=s5_log_dt[l], b_re=s5_b_re[l], b_im=s5_b_im[l],
                      c_re=s5_c_re[l], c_im=s5_c_im[l], d=s5_d[l], glu_w=s5_glu_w[l], glu_b=s5_glu_b[l],
                      out_g=s5_out_g[l])
        y_s5 = _s5(u_s5, s5_prm, nb, seq)
        kv = _norm_proj(memg.reshape(nb * N_MEM, d), norm_mem[l],
                        [xa_wk[l].astype(BF16), xa_wv[l].astype(BF16)], [BF16, BF16], PROJ_BLOCK)
        kmem = kv[0].reshape(nb, N_MEM, d)
        vmem = kv[1].reshape(nb, N_MEM, d)
        xg = _mix_xattn(xg, y_rw, y_s5, w_out[l], norm_xa[l], xa_wq[l], kmem, vmem, xa_wo[l], nb, seq)
        x2 = xg.reshape(t, d)
        x8, h8, base, shift, gate = _peer_select(x2, norm_ffn[l], peer_wq[l], peer_keys[l])
        u3, v3, u_packed, v_packed = tabs
        n_sc = int(t * sc_share)
        assert n_sc % (SC_WORKERS * SC_GROUP) == 0 and (t - n_sc) % GATHER_BLOCK == 0
        t_tc = t - n_sc
        n_exp = peer_u.shape[1]
        eidx_sc = (base[t_tc:] >> 2) + (shift[t_tc:] >> 4) + l * n_exp
        x_sc = _peer_sc(eidx_sc, gate[t_tc:], h8, x8, u3, v3, t_tc)
        c = _peer_u(base, shift, h8, gate, u_packed, t_tc, l, depth)
        x_new = _peer_v(base, shift, c, x8, x2, v_packed, t_tc, l, depth)
        xg = lax.dynamic_update_slice(x_new, x_sc.reshape(n_sc, d), (t_tc, 0)).reshape(nb, seq, d)
        return xg, v_first

    nb = bsz // SEQ_GROUPS
    xs = [x[g * nb:(g + 1) * nb] for g in range(SEQ_GROUPS)]
    mems = [mem[g * nb:(g + 1) * nb] for g in range(SEQ_GROUPS)]
    v_firsts = [None] * SEQ_GROUPS
    (u3, u_packed), (v3, v_packed) = _table_prep(peer_u), _table_prep(peer_v)
    tabs = (u3, v3, u_packed, v_packed)
    for l in range(depth):
        for g in range(SEQ_GROUPS):
            calls_after = (depth - 1 - l) * SEQ_GROUPS + (SEQ_GROUPS - 1 - g)
            share = SC_SHARE_TAIL[calls_after] if calls_after < len(SC_SHARE_TAIL) else SC_SHARE_PIPELINED
            xs[g], v_firsts[g] = layer(l, xs[g], mems[g], v_firsts[g], tabs, share)
    outs = [_final_norm(xg.reshape(nb * seq, d), norm_final).reshape(nb, seq, d) for xg in xs]
    return jnp.concatenate(outs, axis=0)
```

```python
import functools
import math

import jax
import jax.numpy as jnp
from jax import lax
from jax.experimental import pallas as pl
from jax.experimental.pallas import tpu as pltpu
from jax.experimental.pallas import tpu_sc as plsc

F32 = jnp.float32
BF16 = jnp.bfloat16

LANES = 128
SUBLANES = 8
VMEM_LIMIT = 56 * 1024 * 1024

D_MODEL = 1024
D_RWKV = 512
RW_HEAD = 64
RW_COLS = 1792
D_S5 = 512
S5_GROUPS = 32
S5_CH = 16
S5_STATE = 64
S5_MODES = S5_GROUPS * S5_STATE
N_MEM = 256
XA_HEADS = 4
XA_HEAD = 256
PEER_HEADS = 8
PEER_NKEYS = 128
PEER_TOPK = 16
PEER_SEL = PEER_HEADS * PEER_TOPK
RMS_EPS = 1e-6
GN_EPS = 64e-5

RW_CHUNK = 128
RW_SEQS_PER_STEP = 1
S5_BLOCK = 256
PROJ_BLOCK = 512
XA_BLOCK = 256
SEL_BLOCK = 128
GATHER_BLOCK = 64


def _params(sem):
    return pltpu.CompilerParams(dimension_semantics=sem, vmem_limit_bytes=VMEM_LIMIT)


def _rms(x, g):
    ms = jnp.mean(x * x, axis=-1, keepdims=True)
    return x * lax.rsqrt(ms + RMS_EPS) * g


def _bdot(a, b):
    return jnp.dot(a.astype(BF16), b.astype(BF16), preferred_element_type=F32)


def _bdot_nt(a, b):
    return lax.dot_general(a.astype(BF16), b.astype(BF16), (((1,), (1,)), ((), ())),
                           preferred_element_type=F32)


def _sigmoid(x):
    return 1.0 / (1.0 + jnp.exp(-x))


def _softplus(x):
    return jnp.maximum(x, 0.0) + jnp.log(1.0 + jnp.exp(-jnp.abs(x)))


def _gelu(x):
    return 0.5 * x * (1.0 + jnp.tanh(math.sqrt(2.0 / math.pi) * (x + 0.044715 * (x * x * x))))


def _full(shape):
    n = len(shape)
    return pl.BlockSpec(shape, lambda *_: (0,) * n)


def _norm_proj_kernel(*refs, n_out):
    x_ref, g_ref = refs[0], refs[1]
    w_refs = refs[2:2 + n_out]
    o_refs = refs[2 + n_out:]
    h = _rms(x_ref[...], g_ref[...]).astype(BF16)
    for w_ref, o_ref in zip(w_refs, o_refs):
        o_ref[...] = jnp.dot(h, w_ref[...], preferred_element_type=F32).astype(o_ref.dtype)


def _norm_proj(x2, g, ws, out_dtypes, block):
    t, d = x2.shape
    block = min(block, t)
    in_specs = [pl.BlockSpec((block, d), lambda i: (i, 0)), _full((1, d))]
    in_specs += [_full(w.shape) for w in ws]
    out_specs = [pl.BlockSpec((block, w.shape[1]), lambda i: (i, 0)) for w in ws]
    out_shape = [jax.ShapeDtypeStruct((t, w.shape[1]), dt) for w, dt in zip(ws, out_dtypes)]
    return pl.pallas_call(
        functools.partial(_norm_proj_kernel, n_out=len(ws)),
        grid=(t // block,), in_specs=in_specs, out_specs=out_specs, out_shape=out_shape,
        compiler_params=_params(("parallel",)),
    )(x2, g.reshape(1, d), *ws)


def _seg_sum(x, mseg):
    hi = x.astype(BF16)
    lo = (x - hi.astype(F32)).astype(BF16)
    return (jnp.dot(hi, mseg, preferred_element_type=F32)
            + jnp.dot(lo, mseg, preferred_element_type=F32))


def _col_bcast(row):
    return jnp.broadcast_to(row, (LANES, LANES)).T


def _rwkv_kernel(*refs, first_layer):
    zprev, hst = refs[-2], refs[-1]

    @pl.when(pl.program_id(1) == 0)
    def _():
        zprev[...] = jnp.zeros_like(zprev)
        hst[...] = jnp.zeros_like(hst)

    for b in range(refs[0].shape[0]):
        _rwkv_chunk(refs, first_layer, b)


def _rwkv_chunk(refs, first_layer, b):
    if first_layer:
        (z_ref, mu_ref, w0_ref, w2_ref, a0_ref, a2_ref, g2_ref, kk_ref, ka_ref, rk_ref,
         lng_ref, lnb_ref, mseg_ref, y_ref, vf_out_ref, zprev, hst) = refs
    else:
        (z_ref, hv_ref, vf_ref, v0_ref, v2_ref, mu_ref, w0_ref, w2_ref, a0_ref, a2_ref, g2_ref,
         kk_ref, ka_ref, rk_ref, lng_ref, lnb_ref, mseg_ref, y_ref, zprev, hst) = refs
    L = RW_CHUNK
    n_pair = D_RWKV // LANES

    z = z_ref[b]
    row = lax.broadcasted_iota(jnp.int32, (L, 1), 0)
    zs = jnp.where(row == 0, zprev[b:b + 1, :], pltpu.roll(z, 1, 0))
    zprev[b:b + 1, :] = z[L - 1:L, :]
    z = z + (zs - z) * mu_ref[...]
    r = z[:, 0:512]
    k = z[:, 512:1024]
    v = z[:, 1024:1536]
    wa = z[:, 1536:1664]
    gd = z[:, 1664:1792]
    mseg = mseg_ref[...]

    wlin = w0_ref[...] + _bdot(jnp.tanh(wa), w2_ref[...])
    lw = -jnp.exp(-_softplus(-wlin) - 0.5)
    a = _sigmoid(a0_ref[...] + _bdot(wa, a2_ref[...]))
    g = _bdot(_sigmoid(gd), g2_ref[...])
    if first_layer:
        vf_out_ref[b] = v
    else:
        v = v + (vf_ref[b] - v) * _sigmoid(v0_ref[...] + _bdot(hv_ref[b], v2_ref[...]))
    kk = k * kk_ref[...]
    kk = kk / jnp.maximum(jnp.sqrt(_seg_sum(kk * kk, mseg)), 1e-12)
    k2 = k * (1.0 + (a - 1.0) * ka_ref[...])
    av = -kk
    bv = kk * a

    ti = lax.broadcasted_iota(jnp.int32, (L, L), 0)
    si = lax.broadcasted_iota(jnp.int32, (L, L), 1)
    tril = (ti >= si).astype(F32)
    cum = jnp.dot(tril, lw, preferred_element_type=F32, precision=lax.Precision.HIGHEST)
    mid = cum[L // 2 - 1:L // 2, :]
    cm = cum - mid
    ecum = jnp.exp(cm)
    einv = jnp.exp(-cm)
    rt = r * ecum
    kt = k2 * einv
    bt = bv * einv
    at = av * jnp.exp(cm - lw)
    p_end = jnp.exp(cum[L - 1:L, :])
    e_end = ecum[L - 1:L, :]
    e_mid = jnp.exp(mid)

    lane = lax.broadcasted_iota(jnp.int32, (1, LANES), 1)
    m0 = (lane < RW_HEAD).astype(F32)
    m1 = 1.0 - m0
    strict = ti > si
    incl = ti >= si
    bi = lax.broadcasted_iota(jnp.int32, (LANES, LANES), 0) < RW_HEAD
    bj = lax.broadcasted_iota(jnp.int32, (LANES, LANES), 1) < RW_HEAD
    bdmask = (bi == bj).astype(F32)
    zeros_ll = jnp.zeros((L, L), F32)

    ys = []
    for p in range(D_RWKV // LANES):
        sl = slice(LANES * p, LANES * (p + 1))
        A, B, K, R, V = at[:, sl], bt[:, sl], kt[:, sl], rt[:, sl], v[:, sl]
        btkt = jnp.concatenate([B.T, K.T], axis=1)
        h0 = hst[b * n_pair + p]
        h0m = h0 * _col_bcast(e_mid[:, sl])
        sc = _bdot(jnp.concatenate([A * m0, A * m1, R * m0, R * m1], axis=0), btkt)
        aab = [jnp.where(strict, sc[e * L:(e + 1) * L, 0:L], 0.0) for e in range(2)]
        aak = [jnp.where(strict, sc[e * L:(e + 1) * L, L:2 * L], 0.0) for e in range(2)]
        arb = [jnp.where(incl, sc[(2 + e) * L:(3 + e) * L, 0:L], 0.0) for e in range(2)]
        ark = [jnp.where(incl, sc[(2 + e) * L:(3 + e) * L, L:2 * L], 0.0) for e in range(2)]
        arh = _bdot(jnp.concatenate([A, R], axis=0), h0m)
        v01 = jnp.concatenate([V * m0, V * m1], axis=0)
        x = arh[0:L] + _bdot(jnp.concatenate(aak, axis=1), v01)
        pm = jnp.concatenate(aab, axis=1)
        n_fac = int(math.log2(L))
        for it in range(n_fac):
            x = x + _bdot(pm, jnp.concatenate([x * m0, x * m1], axis=0))
            if it + 1 < n_fac:
                pd = jnp.concatenate(
                    [jnp.concatenate([pm[:, 0:L], zeros_ll], axis=1),
                     jnp.concatenate([zeros_ll, pm[:, L:2 * L]], axis=1)], axis=0)
                pm = _bdot(pm, pd)
        u = x
        yp = arh[L:2 * L] + _bdot(jnp.concatenate(arb + ark, axis=1),
                                  jnp.concatenate([u * m0, u * m1, v01], axis=0))
        upd = _bdot(btkt, jnp.concatenate([u, V], axis=0))
        hst[b * n_pair + p] = (h0 * _col_bcast(p_end[:, sl]) + upd * _col_bcast(e_end[:, sl])) * bdmask
        ys.append(yp)
    y = jnp.concatenate(ys, axis=1)

    mean = _seg_sum(y, mseg) * (1.0 / RW_HEAD)
    d = y - mean
    var = _seg_sum(d * d, mseg) * (1.0 / RW_HEAD)
    yn = d * lax.rsqrt(var + GN_EPS) * lng_ref[...] + lnb_ref[...]
    bonus = _seg_sum(r * k2 * rk_ref[...], mseg) * v
    y_ref[b] = (yn + bonus) * g


def _pad_rows(w, rows, offset):
    out = jnp.zeros((rows, w.shape[1]), w.dtype)
    return out.at[offset:offset + w.shape[0]].set(w)


def _rwkv(z_rw, hv, v_first, prm, bsz, seq):
    L = RW_CHUNK
    first = v_first is None
    row = lambda a: a.reshape(1, -1).astype(F32)
    hid = jnp.arange(D_RWKV) // RW_HEAD
    mseg = (hid[:, None] == hid[None, :]).astype(BF16)
    w2p = _pad_rows(prm['w2'], LANES, 0).astype(BF16)
    a2p = _pad_rows(prm['a2'], LANES, 64).astype(BF16)
    common = [row(prm['mu']), row(prm['w0']), w2p, row(prm['a0']), a2p, prm['g2'].astype(BF16),
              row(prm['kk']), row(prm['ka']), row(prm['rk']), row(prm['lng']), row(prm['lnb']), mseg]
    per = RW_SEQS_PER_STEP if bsz % RW_SEQS_PER_STEP == 0 else 1
    tok = lambda w: pl.BlockSpec((per, L, w), lambda b, t: (b, t, 0))
    common_specs = [_full(c.shape) for c in common]
    y_shape = jax.ShapeDtypeStruct((bsz, seq, D_RWKV), F32)
    scratch = [pltpu.VMEM((per, RW_COLS), F32), pltpu.VMEM((per * D_RWKV // LANES, LANES, LANES), F32)]
    if first:
        args = [z_rw] + common
        in_specs = [tok(RW_COLS)] + common_specs
        out_shape = [y_shape, y_shape]
        out_specs = [tok(D_RWKV), tok(D_RWKV)]
    else:
        v2p = _pad_rows(prm['v2'], LANES, 0).astype(BF16)
        extra = [row(prm['v0']), v2p]
        args = [z_rw, hv, v_first] + extra + common
        in_specs = [tok(RW_COLS), tok(LANES), tok(D_RWKV)] + [_full(c.shape) for c in extra] + common_specs
        out_shape = [y_shape]
        out_specs = [tok(D_RWKV)]
    outs = pl.pallas_call(
        functools.partial(_rwkv_kernel, first_layer=first),
        grid=(bsz // per, seq // L), in_specs=in_specs, out_specs=out_specs, out_shape=out_shape,
        scratch_shapes=scratch, compiler_params=_params(("parallel", "arbitrary")),
    )(*args)
    return (outs[0], outs[1]) if first else (outs[0], v_first)


def _s5_kernel(u_ref, wb_ref, wc_ref, lpr_ref, lpi_ref, d_ref, gw_ref, gb_ref, og_ref, o_ref,
               car_re, car_im, xre, xim):
    tb = u_ref.shape[1]

    @pl.when(pl.program_id(1) == 0)
    def _():
        car_re[...] = jnp.zeros_like(car_re)
        car_im[...] = jnp.zeros_like(car_im)

    u = u_ref[0]
    bu = _bdot(u, wb_ref[...])
    xre[...] = bu[:, 0:S5_MODES]
    xim[...] = bu[:, S5_MODES:2 * S5_MODES]
    row = lax.broadcasted_iota(jnp.int32, (SUBLANES, 1), 0)

    def tile(i, carry):
        cr, ci = carry
        rs = pl.ds(pl.multiple_of(i * SUBLANES, SUBLANES), SUBLANES)
        br, bi = xre[rs, :], xim[rs, :]
        for dist in (1, 2, 4):
            keep = row >= dist
            sr = jnp.where(keep, pltpu.roll(br, dist, 0), 0.0)
            si = jnp.where(keep, pltpu.roll(bi, dist, 0), 0.0)
            lr = lpr_ref[dist - 1:dist, :]
            li = lpi_ref[dist - 1:dist, :]
            br, bi = br + lr * sr - li * si, bi + lr * si + li * sr
        pr, pi = lpr_ref[...], lpi_ref[...]
        xr = br + pr * cr - pi * ci
        xi = bi + pr * ci + pi * cr
        xre[rs, :] = xr
        xim[rs, :] = xi
        return xr[SUBLANES - 1:SUBLANES, :], xi[SUBLANES - 1:SUBLANES, :]

    cr, ci = lax.fori_loop(0, tb // SUBLANES, tile, (car_re[...], car_im[...]))
    car_re[...] = cr
    car_im[...] = ci
    wc = wc_ref[...]
    y = _bdot(xre[...], wc[0:S5_MODES]) + _bdot(xim[...], wc[S5_MODES:2 * S5_MODES])
    y = _gelu(y + d_ref[...] * u)
    y = y * _sigmoid(_bdot(y, gw_ref[...]) + gb_ref[...])
    o_ref[0] = _rms(y, og_ref[...])


def _s5_weights(a_re, a_im, log_dt, b_re, b_im, c_re, c_im):
    lam_re = jnp.minimum(a_re.astype(F32), -1e-4)
    lam_im = a_im.astype(F32)
    dt = jnp.exp(log_dt.astype(F32))[:, None]
    mag = jnp.exp(lam_re * dt)
    lb_re = mag * jnp.cos(lam_im * dt)
    lb_im = mag * jnp.sin(lam_im * dt)
    den = lam_re * lam_re + lam_im * lam_im
    c1_re = ((lb_re - 1.0) * lam_re + lb_im * lam_im) / den
    c1_im = (lb_im * lam_re - (lb_re - 1.0) * lam_im) / den
    br, bi = b_re.astype(F32), b_im.astype(F32)
    bb_re = c1_re[..., None] * br - c1_im[..., None] * bi
    bb_im = c1_re[..., None] * bi + c1_im[..., None] * br
    eye = jnp.eye(S5_GROUPS, dtype=F32)
    wb_re = jnp.einsum('gpc,gh->gchp', bb_re, eye).reshape(D_S5, S5_MODES)
    wb_im = jnp.einsum('gpc,gh->gchp', bb_im, eye).reshape(D_S5, S5_MODES)
    wb = jnp.concatenate([wb_re, wb_im], axis=1).astype(BF16)
    wc_re = jnp.einsum('gcp,gh->gphc', c_re.astype(F32), eye).reshape(S5_MODES, D_S5)
    wc_im = jnp.einsum('gcp,gh->gphc', c_im.astype(F32), eye).reshape(S5_MODES, D_S5)
    wc = jnp.concatenate([wc_re, -wc_im], axis=0).astype(BF16)
    pr, pi = [lb_re], [lb_im]
    for _ in range(SUBLANES - 1):
        pr, pi = pr + [pr[-1] * lb_re - pi[-1] * lb_im], pi + [pr[-1] * lb_im + pi[-1] * lb_re]
    lp_re = jnp.stack(pr).reshape(SUBLANES, S5_MODES)
    lp_im = jnp.stack(pi).reshape(SUBLANES, S5_MODES)
    return wb, wc, lp_re, lp_im


def _s5(u, prm, bsz, seq):
    tb = min(S5_BLOCK, seq)
    wb, wc, lp_re, lp_im = _s5_weights(prm['a_re'], prm['a_im'], prm['log_dt'], prm['b_re'], prm['b_im'],
                                       prm['c_re'], prm['c_im'])
    row = lambda a: a.reshape(1, -1).astype(F32)
    consts = [wb, wc, lp_re, lp_im, row(prm['d']), prm['glu_w'].astype(BF16), row(prm['glu_b']),
              row(prm['out_g'])]
    tok = pl.BlockSpec((1, tb, D_S5), lambda b, t: (b, t, 0))
    return pl.pallas_call(
        _s5_kernel, grid=(bsz, seq // tb),
        in_specs=[tok] + [_full(c.shape) for c in consts], out_specs=tok,
        out_shape=jax.ShapeDtypeStruct((bsz, seq, D_S5), F32),
        scratch_shapes=[pltpu.VMEM((1, S5_MODES), F32), pltpu.VMEM((1, S5_MODES), F32),
                        pltpu.VMEM((tb, S5_MODES), F32), pltpu.VMEM((tb, S5_MODES), F32)],
        compiler_params=_params(("parallel", "arbitrary")),
    )(u, *consts)


def _mix_xattn_kernel(x_ref, yr_ref, ys_ref, wo1_ref, wo2_ref, g_ref, wq_ref, k_ref, v_ref, wo_ref, o_ref):
    x1 = x_ref[0] + _bdot(yr_ref[0], wo1_ref[...]) + _bdot(ys_ref[0], wo2_ref[...])
    h = _rms(x1, g_ref[...])
    q = _bdot(h, wq_ref[...])
    km, vm = k_ref[0], v_ref[0]
    outs = []
    for hd in range(XA_HEADS):
        sl = slice(XA_HEAD * hd, XA_HEAD * (hd + 1))
        s = _bdot_nt(q[:, sl], km[:, sl]) * (XA_HEAD ** -0.5)
        s = s - jnp.max(s, axis=-1, keepdims=True)
        e = jnp.exp(s)
        p = e / jnp.sum(e, axis=-1, keepdims=True)
        outs.append(_bdot(p, vm[:, sl]))
    o = jnp.concatenate(outs, axis=1)
    o_ref[0] = x1 + _bdot(o, wo_ref[...])


def _mix_xattn(x, y_rw, y_s5, w_out, g, wq, kmem, vmem, wo, bsz, seq):
    tm = min(XA_BLOCK, seq)
    consts_a = [w_out[:D_RWKV].astype(BF16), w_out[D_RWKV:].astype(BF16), g.reshape(1, -1).astype(F32),
                wq.astype(BF16)]
    tok = lambda w: pl.BlockSpec((1, tm, w), lambda b, t: (b, t, 0))
    mem = pl.BlockSpec((1, N_MEM, D_MODEL), lambda b, t: (b, 0, 0))
    wo_b = wo.astype(BF16)
    return pl.pallas_call(
        _mix_xattn_kernel, grid=(bsz, seq // tm),
        in_specs=[tok(D_MODEL), tok(D_RWKV), tok(D_S5)] + [_full(c.shape) for c in consts_a]
                 + [mem, mem, _full(wo_b.shape)],
        out_specs=tok(D_MODEL), out_shape=jax.ShapeDtypeStruct((bsz, seq, D_MODEL), F32),
        compiler_params=_params(("parallel", "parallel")),
    )(x, y_rw, y_s5, *consts_a, kmem, vmem, wo_b)


def _top_rows(work, order, aux, val_ref, idx_ref):
    for it in range(PEER_TOPK):
        m = jnp.max(work, axis=0, keepdims=True)
        pos = jnp.min(jnp.where(work == m, order, jnp.inf), axis=0, keepdims=True)
        hit = order == pos
        val_ref[it:it + 1, :] = m
        if aux is None:
            idx_ref[it:it + 1, :] = pos
        else:
            idx_ref[it:it + 1, :] = jnp.sum(jnp.where(hit, aux, 0.0), axis=0, keepdims=True)
        work = jnp.where(hit, -jnp.inf, work)


_CAND_ROW_BLOCKS = [(0, PEER_TOPK), (1, SUBLANES), (2, SUBLANES), (3, SUBLANES)]
_CAND_COL_BLOCKS = [(0, PEER_TOPK, 4, 15), (1, SUBLANES, 4, 7), (2, SUBLANES, 4, 4)]
N_CAND = sum(n for _, n in _CAND_ROW_BLOCKS) + sum(n for _, n, _, _ in _CAND_COL_BLOCKS)


def _cand_consts(tm):
    flat, neg = [], []
    for a, nb in _CAND_ROW_BLOCKS:
        flat += [a * PEER_TOPK + b for b in range(nb)]
        neg += [0.0] * nb
    for b, na, lo, hi in _CAND_COL_BLOCKS:
        flat += [a * PEER_TOPK + b for a in range(na)]
        neg += [0.0 if lo <= a <= hi else -float('inf') for a in range(na)]
    flat = [f if n == 0.0 else 1000.0 + i for i, (f, n) in enumerate(zip(flat, neg))]
    col = lambda v: jnp.broadcast_to(jnp.asarray(v, F32)[:, None], (N_CAND, tm))
    return col(flat), col(neg)


def _cand_rows(row_vals, col_vals, combine):
    blocks = [combine(row_vals[a:a + 1, :], col_vals[0:nb, :]) for a, nb in _CAND_ROW_BLOCKS]
    blocks += [combine(row_vals[0:na, :], col_vals[b:b + 1, :]) for b, na, _, _ in _CAND_COL_BLOCKS]
    return jnp.concatenate(blocks, axis=0)


SEL_HEADS_PER_STEP = 4


def _peer_select_kernel(x_ref, g_ref, wq_ref, keys_ref, cflat_ref, cneg_ref, x8_ref, h8_ref, base_ref, shift_ref,
                        gate_ref, q3, idx_t, gate_t, *lists):
    tm = x_ref.shape[0]
    n = SEL_HEADS_PER_STEP
    s1, i1, s2, i2, top, eid = (lists[j * n:(j + 1) * n] for j in range(6))
    x = x_ref[...]
    h = _rms(x, g_ref[...])
    for r in range(D_MODEL // LANES):
        rows = pl.ds(r, tm, stride=D_MODEL // LANES)
        h8_ref[rows, :] = h[:, LANES * r:LANES * (r + 1)]
        x8_ref[rows, :] = x[:, LANES * r:LANES * (r + 1)]
    q = _bdot(h, wq_ref[...])
    for j in range(2 * PEER_HEADS):
        q3[j] = q[:, LANES * j:LANES * (j + 1)]
    iota_k = lax.broadcasted_iota(jnp.int32, (PEER_NKEYS, tm), 0).astype(F32)

    def heads(step, _):
        for u in range(SEL_HEADS_PER_STEP):
            hd = step * SEL_HEADS_PER_STEP + u
            sc1 = _bdot_nt(keys_ref[2 * hd], q3[2 * hd])
            sc2 = _bdot_nt(keys_ref[2 * hd + 1], q3[2 * hd + 1])
            _top_rows(sc1, iota_k, None, s1[u], i1[u])
            _top_rows(sc2, iota_k, None, s2[u], i2[u])
            cand = _cand_rows(s1[u][...], s2[u][...], lambda x, y: x + y) + cneg_ref[...]
            cidx = _cand_rows(i1[u][...], i2[u][...], lambda x, y: x * float(PEER_NKEYS) + y)
            _top_rows(cand, cflat_ref[...], cidx, top[u], eid[u])
            tv = top[u][...]
            e = jnp.exp(tv - jnp.max(tv, axis=0, keepdims=True))
            rs = pl.ds(pl.multiple_of(hd * PEER_TOPK, PEER_TOPK), PEER_TOPK)
            idx_t[rs, :] = eid[u][...]
            gate_t[rs, :] = e / jnp.sum(e, axis=0, keepdims=True)
        return 0

    lax.fori_loop(0, PEER_HEADS // SEL_HEADS_PER_STEP, heads, 0)
    e_t = idx_t[...].T
    pair = jnp.floor(e_t * 0.5)
    base_ref[...] = (pair * float(SUBLANES)).astype(jnp.int32)
    shift_ref[...] = ((e_t - 2.0 * pair) * 16.0).astype(jnp.int32)
    gate_ref[...] = gate_t[...].T


def _peer_select(x2, g, wq, keys):
    t = x2.shape[0]
    tm = SEL_BLOCK
    keys_b = keys.reshape(2 * PEER_HEADS, PEER_NKEYS, LANES).astype(BF16)
    wq_b = wq.astype(BF16)
    cflat, cneg = _cand_consts(tm)
    tokspec = lambda w: pl.BlockSpec((tm, w), lambda i: (i, 0))
    vm = lambda r: pltpu.VMEM((r, tm), F32)
    return pl.pallas_call(
        _peer_select_kernel, grid=(t // tm,),
        in_specs=[tokspec(D_MODEL), _full((1, D_MODEL)), _full(wq_b.shape), _full(keys_b.shape),
                  _full(cflat.shape), _full(cneg.shape)],
        out_specs=[pl.BlockSpec((tm * SUBLANES, LANES), lambda i: (i, 0))] * 2
                  + [tokspec(PEER_SEL), tokspec(PEER_SEL), tokspec(PEER_SEL)],
        out_shape=[jax.ShapeDtypeStruct((t * SUBLANES, LANES), F32)] * 2 + [
                   jax.ShapeDtypeStruct((t, PEER_SEL), jnp.int32),
                   jax.ShapeDtypeStruct((t, PEER_SEL), jnp.int32), jax.ShapeDtypeStruct((t, PEER_SEL), F32)],
        scratch_shapes=[pltpu.VMEM((2 * PEER_HEADS, tm, LANES), F32), vm(PEER_SEL), vm(PEER_SEL)]
                       + [vm(PEER_TOPK) for _ in range(6 * SEL_HEADS_PER_STEP)],
        compiler_params=_params(("parallel",)),
    )(x2, g.reshape(1, -1).astype(F32), wq_b, keys_b, cflat, cneg)


TABLE_PREP_ROWS = 256


def _table_prep_kernel(tab_ref, tiles_ref, packed_ref):
    n = tab_ref.shape[0]
    n_chunk = D_MODEL // LANES
    x = tab_ref[...]
    for r in range(n_chunk):
        tiles_ref[pl.ds(r, n, stride=n_chunk), :] = x[:, LANES * r:LANES * (r + 1)]
    pairs = tiles_ref[...].reshape(n // 2, 2 * n_chunk, LANES)
    as_bits = lambda t: lax.bitcast_convert_type(t.astype(BF16).astype(F32), jnp.int32)
    word = as_bits(pairs[:, 0:n_chunk, :]) | lax.shift_right_logical(as_bits(pairs[:, n_chunk:, :]), 16)
    packed_ref[...] = word.reshape(n // 2 * n_chunk, LANES)


def _table_prep(tab):
    rows = tab.shape[0] * tab.shape[1]
    n = TABLE_PREP_ROWS
    n_chunk = D_MODEL // LANES
    tiles, packed = pl.pallas_call(
        _table_prep_kernel, grid=(rows // n,),
        in_specs=[pl.BlockSpec((n, D_MODEL), lambda i: (i, 0))],
        out_specs=[pl.BlockSpec((n * n_chunk, LANES), lambda i: (i, 0)),
                   pl.BlockSpec((n // 2 * n_chunk, LANES), lambda i: (i, 0))],
        out_shape=[jax.ShapeDtypeStruct((rows * n_chunk, LANES), F32),
                   jax.ShapeDtypeStruct((rows // 2 * n_chunk, LANES), jnp.int32)],
        compiler_params=_params(("parallel",)),
    )(tab.reshape(rows, D_MODEL))
    return tiles.reshape(rows, n_chunk, LANES), packed


def _splat_into(src_ref, t, dst_ref, slot):
    tile = jnp.broadcast_to(src_ref[pl.ds(t, 1), :], (LANES, LANES)).T
    dst_ref[LANES * slot:LANES * (slot + 1), :] = tile


def _bcast_row(ref, row):
    return jnp.broadcast_to(ref[row:row + 1, :], (SUBLANES, LANES))


def _expert_tile(tab_ref, base, shift_splat, row):
    w = tab_ref[pl.ds(pl.multiple_of(base, SUBLANES), SUBLANES), :]
    return lax.bitcast_convert_type((w << _bcast_row(shift_splat, row)) & jnp.int32(-65536), F32)


def _token_rows(t):
    return pl.ds(pl.multiple_of(t * SUBLANES, SUBLANES), SUBLANES)


def _pipelined_tokens(tg, prep, work):
    prep(0, 0)

    def body(i, _):
        t0 = 2 * i
        prep(t0 + 1, 1)
        work(t0, 0)
        prep(jnp.minimum(t0 + 2, tg - 1), 0)
        work(t0 + 1, 1)
        return 0

    lax.fori_loop(0, tg // 2, body, 0)


def _table_spec(packed, layer, n_layers):
    rows = packed.shape[0] // n_layers
    return pl.BlockSpec((rows, LANES), lambda i: (layer, 0), pipeline_mode=pl.Buffered(1))


N_CHUNK = D_MODEL // LANES
CHUNK_STRIDE = PEER_SEL + SUBLANES


def _peer_u_kernel(base_ref, shift_ref, h_ref, gate_ref, tab_ref, eye_ref, c_ref, shift_splat, *planes):
    tg = gate_ref.shape[0]
    ones = jnp.ones((LANES, LANES), BF16)

    half = N_CHUNK // 2

    def prep(t, slot):
        _splat_into(shift_ref, t, shift_splat, slot)

    def gather(t, slot):
        ht = h_ref[_token_rows(t), :]
        for kx in range(PEER_SEL):
            prod = _expert_tile(tab_ref, base_ref[t, kx], shift_splat, LANES * slot + kx) * ht
            fold = prod + pltpu.roll(prod, half, 0)
            planes[slot][pl.ds(kx, half, stride=CHUNK_STRIDE), :] = fold[0:half, :]

    def finish(t, slot):
        plane = planes[slot]
        acc = plane[0:PEER_SEL, :]
        for r in range(1, half):
            acc = acc + plane[CHUNK_STRIDE * r:CHUNK_STRIDE * r + PEER_SEL, :]
        hi = acc.astype(BF16)
        lo = (acc - hi.astype(F32)).astype(BF16)
        tot = jnp.dot(hi, ones, preferred_element_type=F32) + jnp.dot(lo, ones, preferred_element_type=F32)
        score = jnp.sum(tot * eye_ref[...], axis=0, keepdims=True)
        c_ref[pl.ds(t, 1), :] = gate_ref[pl.ds(t, 1), :] * _gelu(score)

    planes[1][...] = jnp.zeros_like(planes[1])
    prep(0, 0)

    def body(i, _):
        t0 = 2 * i
        prep(t0 + 1, 1)
        gather(t0, 0)
        finish(jnp.maximum(t0 - 1, 0), 1)
        prep(jnp.minimum(t0 + 2, tg - 1), 0)
        gather(t0 + 1, 1)
        finish(t0, 0)
        return 0

    lax.fori_loop(0, tg // 2, body, 0)
    finish(tg - 1, 1)


def _peer_u(base, shift, h8, gate, tab, t, layer, n_layers):
    tg = GATHER_BLOCK
    eye = jnp.eye(LANES, dtype=F32)
    tokrow = pl.BlockSpec((tg, PEER_SEL), lambda i: (i, 0))
    return pl.pallas_call(
        _peer_u_kernel, grid=(t // tg,),
        in_specs=[pl.BlockSpec((tg, PEER_SEL), lambda i: (i, 0), memory_space=pltpu.SMEM), tokrow,
                  pl.BlockSpec((tg * SUBLANES, LANES), lambda i: (i, 0)), tokrow,
                  _table_spec(tab, layer, n_layers), _full(eye.shape)],
        out_specs=tokrow, out_shape=jax.ShapeDtypeStruct((t, PEER_SEL), F32),
        scratch_shapes=[pltpu.VMEM((2 * LANES, LANES), jnp.int32)]
                       + [pltpu.VMEM((N_CHUNK // 2 * CHUNK_STRIDE, LANES), F32)] * 2,
        compiler_params=_params(("parallel",)),
    )(base, shift, h8, gate, tab, eye)


def _peer_v_kernel(base_ref, shift_ref, c_ref, x_ref, tab_ref, _aliased_out, o_ref, shift_splat, c_splat, otile):
    tg = c_ref.shape[0]
    n_acc = 4

    def prep(t, slot):
        _splat_into(shift_ref, t, shift_splat, slot)
        _splat_into(c_ref, t, c_splat, slot)

    def work(t, slot):
        accs = [jnp.zeros((SUBLANES, LANES), F32) for _ in range(n_acc)]
        for kx in range(PEER_SEL):
            row = LANES * slot + kx
            accs[kx % n_acc] = accs[kx % n_acc] + (_bcast_row(c_splat, row)
                                                   * _expert_tile(tab_ref, base_ref[t, kx], shift_splat, row))
        rs = _token_rows(t)
        otile[rs, :] = x_ref[rs, :] + ((accs[0] + accs[1]) + (accs[2] + accs[3]))

    _pipelined_tokens(tg, prep, work)
    for r in range(N_CHUNK):
        o_ref[:, LANES * r:LANES * (r + 1)] = otile[pl.ds(r, tg, stride=N_CHUNK), :]


def _peer_v(base, shift, c, x8, x2, tab, t, layer, n_layers):
    tg = GATHER_BLOCK
    smem = pl.BlockSpec((tg, PEER_SEL), lambda i: (i, 0), memory_space=pltpu.SMEM)
    tokrow = pl.BlockSpec((tg, PEER_SEL), lambda i: (i, 0))
    tile = pl.BlockSpec((tg * SUBLANES, LANES), lambda i: (i, 0))
    return pl.pallas_call(
        _peer_v_kernel, grid=(t // tg,),
        in_specs=[smem, tokrow, tokrow, tile, _table_spec(tab, layer, n_layers),
                  pl.BlockSpec(memory_space=pl.ANY)],
        out_specs=pl.BlockSpec((tg, D_MODEL), lambda i: (i, 0)),
        out_shape=jax.ShapeDtypeStruct(x2.shape, F32),
        input_output_aliases={5: 0},
        scratch_shapes=[pltpu.VMEM((2 * LANES, LANES), jnp.int32), pltpu.VMEM((2 * LANES, LANES), F32),
                        pltpu.VMEM((tg * SUBLANES, LANES), F32)],
        compiler_params=_params(("parallel",)),
    )(base, shift, c, x8, tab, x2)


SC_WORKERS = 32
SC_LANES = 16
SC_ROWS = 32
SC_SHARE_PIPELINED = 40 / 64
SC_SHARE_TAIL = (26 / 64,)
SEQ_GROUPS = 4
SC_GROUP = 8
SC_REGS = 32


def _peer_sc(eidx, gate, h8, x8, tab_u, tab_v, t_off):
    ts = eidx.shape[0]
    per_w = ts // SC_WORKERS
    n_chunk = PEER_SEL // SC_ROWS
    grp = SC_GROUP
    sel = grp * PEER_SEL
    n_pairs = grp * n_chunk // 2
    per_tile = LANES // SC_LANES
    n_blk = D_MODEL // (SC_REGS * SC_LANES)
    mesh = plsc.VectorSubcoreMesh(core_axis_name="c", subcore_axis_name="s")

    def piece(ref, lead, q):
        return ref[lead, q // per_tile, pl.ds(SC_LANES * (q % per_tile), SC_LANES)]

    def tok_piece(ref, tok, q):
        return ref[tok * N_CHUNK + q // per_tile, pl.ds(SC_LANES * (q % per_tile), SC_LANES)]

    def body(u_hbm, v_hbm, idx_hbm, gate_hbm, h_hbm, x_hbm, out_hbm,
             idx_v, c_v, h_v, acc_v, part_v, rows0, rows1, sem0, sem1):
        wid = lax.axis_index("s") * 2 + lax.axis_index("c")
        bufs, sems = (rows0, rows1), (sem0, sem1)
        lane = lax.iota(jnp.int32, SC_LANES)

        def stream(tab_hbm, compute):
            def gather(chunk, slot):
                start = pl.multiple_of(chunk * SC_ROWS, SC_ROWS)
                return pltpu.make_async_copy(tab_hbm.at[idx_v.at[pl.ds(start, SC_ROWS)]], bufs[slot], sems[slot])

            gather(0, 0).start()

            @pl.loop(0, n_pairs)
            def _(p):
                c0 = 2 * p
                gather(c0 + 1, 1).start()
                gather(c0, 0).wait()
                compute(c0, 0)
                gather(jnp.minimum(c0 + 2, 2 * n_pairs - 1), 0).start()
                gather(c0 + 1, 1).wait()
                compute(c0 + 1, 1)

            gather(0, 0).wait()

        def dots(chunk, slot):
            tok = chunk // n_chunk
            for blk in range(n_blk):
                where = [blk * SC_REGS + j for j in range(SC_REGS)]
                hs = [tok_piece(h_v, tok, q) for q in where]

                def row(k, carry):
                    parts = [None] * 4
                    for i, q in enumerate(where):
                        term = piece(bufs[slot], k, q) * hs[i]
                        parts[i % 4] = term if parts[i % 4] is None else parts[i % 4] + term
                    tot = (parts[0] + parts[1]) + (parts[2] + parts[3])
                    at = pl.ds(pl.multiple_of((chunk * SC_ROWS + k) * SC_LANES, SC_LANES), SC_LANES)
                    if blk == 0:
                        part_v[at] = tot
                    else:
                        part_v[at] = part_v[at] + tot
                    return carry

                lax.fori_loop(0, SC_ROWS, row, 0)

        def weigh(chunk, slot):
            tok = chunk // n_chunk
            for blk in range(n_blk):
                where = [blk * SC_REGS + j for j in range(SC_REGS)]

                def row(k, accs):
                    ck = plsc.load_gather(c_v, [jnp.full((SC_LANES,), chunk * SC_ROWS, jnp.int32) + k])
                    return tuple(a + ck * piece(bufs[slot], k, q) for a, q in zip(accs, where))

                accs = lax.fori_loop(0, SC_ROWS, row, tuple(tok_piece(acc_v, tok, q) for q in where))
                for a, q in zip(accs, where):
                    acc_v[tok * N_CHUNK + q // per_tile, pl.ds(SC_LANES * (q % per_tile), SC_LANES)] = a

        @pl.loop(0, per_w // grp)
        def _(g):
            t0 = wid * per_w + g * grp
            flat = pl.ds(pl.multiple_of(t0 * PEER_SEL, sel), sel)
            pltpu.sync_copy(idx_hbm.at[flat], idx_v)
            pltpu.sync_copy(gate_hbm.at[flat], c_v)
            tok_rows = grp * N_CHUNK
            src_rows = pl.ds(pl.multiple_of((t_off + t0) * N_CHUNK, tok_rows), tok_rows)
            pltpu.sync_copy(h_hbm.at[src_rows], h_v)
            pltpu.sync_copy(x_hbm.at[src_rows], acc_v)
            stream(u_hbm, dots)

            @pl.loop(0, sel // SC_LANES)
            def _(m):
                base = (m * SC_LANES + lane) * SC_LANES
                score = plsc.load_gather(part_v, [base])
                for l in range(1, SC_LANES):
                    score = score + plsc.load_gather(part_v, [base + l])
                z = math.sqrt(2.0 / math.pi) * (score + 0.044715 * (score * score * score))
                tanh_z = 1.0 - 2.0 / (jnp.exp(2.0 * z) + 1.0)
                at = pl.ds(pl.multiple_of(m * SC_LANES, SC_LANES), SC_LANES)
                c_v[at] = c_v[at] * (0.5 * score * (1.0 + tanh_z))

            stream(v_hbm, weigh)
            pltpu.sync_copy(acc_v, out_hbm.at[pl.ds(pl.multiple_of(t0 * N_CHUNK, tok_rows), tok_rows)])

    tile = lambda n: pltpu.VMEM((n, N_CHUNK, LANES), F32)
    tok_tiles = pltpu.VMEM((grp * N_CHUNK, LANES), F32)
    return pl.kernel(
        body, mesh=mesh, out_type=jax.ShapeDtypeStruct((ts * N_CHUNK, LANES), F32),
        scratch_types=[pltpu.VMEM((sel,), jnp.int32), pltpu.VMEM((sel,), F32), tok_tiles, tok_tiles,
                       pltpu.VMEM((sel * SC_LANES,), F32), tile(SC_ROWS), tile(SC_ROWS),
                       pltpu.SemaphoreType.DMA, pltpu.SemaphoreType.DMA],
        compiler_params=pltpu.CompilerParams(needs_layout_passes=False, use_tc_tiling_on_sc=True),
    )(tab_u, tab_v, eidx.reshape(-1), gate.reshape(-1), h8, x8)


def _final_norm_kernel(x_ref, g_ref, o_ref):
    o_ref[...] = _rms(x_ref[...], g_ref[...])


def _final_norm(x2, g):
    t, d = x2.shape
    tm = min(PROJ_BLOCK, t)
    spec = pl.BlockSpec((tm, d), lambda i: (i, 0))
    return pl.pallas_call(
        _final_norm_kernel, grid=(t // tm,), in_specs=[spec, _full((1, d))], out_specs=spec,
        out_shape=jax.ShapeDtypeStruct((t, d), F32), compiler_params=_params(("parallel",)),
    )(x2, g.reshape(1, d).astype(F32))


def kernel(x, mem, norm_mix, w_in, rw_mu, rw_w0, rw_w2, rw_a0, rw_a2, rw_g2, rw_kk, rw_ka, rw_rk, rw_v0, rw_v1, rw_v2, rw_lnx_g, rw_lnx_b, s5_a_re, s5_a_im, s5_log_dt, s5_b_re, s5_b_im, s5_c_re, s5_c_im, s5_d, s5_glu_w, s5_glu_b, s5_out_g, w_out, norm_xa, norm_mem, xa_wq, xa_wk, xa_wv, xa_wo, norm_ffn, peer_wq, peer_keys, peer_u, peer_v, norm_final):
    bsz, seq, d = x.shape
    depth = w_in.shape[0]

    def layer(l, xg, memg, v_first, tabs, sc_share):
        nb = xg.shape[0]
        t = nb * seq
        x2 = xg.reshape(t, d)
        ws = [w_in[l][:, :RW_COLS].astype(BF16), w_in[l][:, RW_COLS:].astype(BF16)]
        if l > 0:
            ws.append(_pad_rows(rw_v1[l - 1].T, LANES, 0).T.astype(BF16))
        outs = _norm_proj(x2, norm_mix[l], ws, [F32] * len(ws), PROJ_BLOCK)
        z_rw = outs[0].reshape(nb, seq, RW_COLS)
        u_s5 = outs[1].reshape(nb, seq, D_S5)
        hv = outs[2].reshape(nb, seq, LANES) if l > 0 else None
        rw_prm = dict(mu=rw_mu[l], w0=rw_w0[l], w2=rw_w2[l], a0=rw_a0[l], a2=rw_a2[l], g2=rw_g2[l],
                      kk=rw_kk[l], ka=rw_ka[l], rk=rw_rk[l], lng=rw_lnx_g[l], lnb=rw_lnx_b[l])
        if l > 0:
            rw_prm.update(v0=rw_v0[l - 1], v2=rw_v2[l - 1])
        y_rw, v_first = _rwkv(z_rw, hv, v_first, rw_prm, nb, seq)
        s5_prm = dict(a_re=s5_a_re[l], a_im=s5_a_im[l], log_dt=s5_log_dt[l], b_re=s5_b_re[l], b_im=s5_b_im[l],
                      c_re=s5_c_re[l], c_im=s5_c_im[l], d=s5_d[l], glu_w=s5_glu_w[l], glu_b=s5_glu_b[l],
                      out_g=s5_out_g[l])
        y_s5 = _s5(u_s5, s5_prm, nb, seq)
        kv = _norm_proj(memg.reshape(nb * N_MEM, d), norm_mem[l],
                        [xa_wk[l].astype(BF16), xa_wv[l].astype(BF16)], [BF16, BF16], PROJ_BLOCK)
        kmem = kv[0].reshape(nb, N_MEM, d)
        vmem = kv[1].reshape(nb, N_MEM, d)
        xg = _mix_xattn(xg, y_rw, y_s5, w_out[l], norm_xa[l], xa_wq[l], kmem, vmem, xa_wo[l], nb, seq)
        x2 = xg.reshape(t, d)
        x8, h8, base, shift, gate = _peer_select(x2, norm_ffn[l], peer_wq[l], peer_keys[l])
        u3, v3, u_packed, v_packed = tabs
        n_sc = int(t * sc_share)
        assert n_sc % (SC_WORKERS * SC_GROUP) == 0 and (t - n_sc) % GATHER_BLOCK == 0
        t_tc = t - n_sc
        n_exp = peer_u.shape[1]
        eidx_sc = (base[t_tc:] >> 2) + (shift[t_tc:] >> 4) + l * n_exp
        x_sc = _peer_sc(eidx_sc, gate[t_tc:], h8, x8, u3, v3, t_tc)
        c = _peer_u(base, shift, h8, gate, u_packed, t_tc, l, depth)
        x_new = _peer_v(base, shift, c, x8, x2, v_packed, t_tc, l, depth)
        xg = lax.dynamic_update_slice(x_new, x_sc.reshape(n_sc, d), (t_tc, 0)).reshape(nb, seq, d)
        return xg, v_first

    nb = bsz // SEQ_GROUPS
    xs = [x[g * nb:(g + 1) * nb] for g in range(SEQ_GROUPS)]
    mems = [mem[g * nb:(g + 1) * nb] for g in range(SEQ_GROUPS)]
    v_firsts = [None] * SEQ_GROUPS
    (u3, u_packed), (v3, v_packed) = _table_prep(peer_u), _table_prep(peer_v)
    tabs = (u3, v3, u_packed, v_packed)
    for l in range(depth):
        for g in range(SEQ_GROUPS):
            calls_after = (depth - 1 - l) * SEQ_GROUPS + (SEQ_GROUPS - 1 - g)
            share = SC_SHARE_TAIL[calls_after] if calls_after < len(SC_SHARE_TAIL) else SC_SHARE_PIPELINED
            xs[g], v_firsts[g] = layer(l, xs[g], mems[g], v_firsts[g], tabs, share)
    outs = [_final_norm(xg.reshape(nb * seq, d), norm_final).reshape(nb, seq, d) for xg in xs]
    return jnp.concatenate(outs, axis=0)
```

```python
import functools
import math

import jax
import jax.numpy as jnp
from jax import lax
from jax.experimental import pallas as pl
from jax.experimental.pallas import tpu as pltpu
from jax.experimental.pallas import tpu_sc as plsc

F32 = jnp.float32
BF16 = jnp.bfloat16

LANES = 128
SUBLANES = 8
VMEM_LIMIT = 56 * 1024 * 1024

D_MODEL = 1024
D_RWKV = 512
RW_HEAD = 64
RW_COLS = 1792
D_S5 = 512
S5_GROUPS = 32
S5_CH = 16
S5_STATE = 64
S5_MODES = S5_GROUPS * S5_STATE
N_MEM = 256
XA_HEADS = 4
XA_HEAD = 256
PEER_HEADS = 8
PEER_NKEYS = 128
PEER_TOPK = 16
PEER_SEL = PEER_HEADS * PEER_TOPK
RMS_EPS = 1e-6
GN_EPS = 64e-5

RW_CHUNK = 128
RW_SEQS_PER_STEP = 1
S5_BLOCK = 256
PROJ_BLOCK = 512
XA_BLOCK = 256
SEL_BLOCK = 128
GATHER_BLOCK = 64


def _params(sem):
    return pltpu.CompilerParams(dimension_semantics=sem, vmem_limit_bytes=VMEM_LIMIT)


def _rms(x, g):
    ms = jnp.mean(x * x, axis=-1, keepdims=True)
    return x * lax.rsqrt(ms + RMS_EPS) * g


def _bdot(a, b):
    return jnp.dot(a.astype(BF16), b.astype(BF16), preferred_element_type=F32)


def _bdot_nt(a, b):
    return lax.dot_general(a.astype(BF16), b.astype(BF16), (((1,), (1,)), ((), ())),
                           preferred_element_type=F32)


def _sigmoid(x):
    return 1.0 / (1.0 + jnp.exp(-x))


def _softplus(x):
    return jnp.maximum(x, 0.0) + jnp.log(1.0 + jnp.exp(-jnp.abs(x)))


def _gelu(x):
    return 0.5 * x * (1.0 + jnp.tanh(math.sqrt(2.0 / math.pi) * (x + 0.044715 * (x * x * x))))


def _full(shape):
    n = len(shape)
    return pl.BlockSpec(shape, lambda *_: (0,) * n)


def _norm_proj_kernel(*refs, n_out):
    x_ref, g_ref = refs[0], refs[1]
    w_refs = refs[2:2 + n_out]
    o_refs = refs[2 + n_out:]
    h = _rms(x_ref[...], g_ref[...]).astype(BF16)
    for w_ref, o_ref in zip(w_refs, o_refs):
        o_ref[...] = jnp.dot(h, w_ref[...], preferred_element_type=F32).astype(o_ref.dtype)


def _norm_proj(x2, g, ws, out_dtypes, block):
    t, d = x2.shape
    block = min(block, t)
    in_specs = [pl.BlockSpec((block, d), lambda i: (i, 0)), _full((1, d))]
    in_specs += [_full(w.shape) for w in ws]
    out_specs = [pl.BlockSpec((block, w.shape[1]), lambda i: (i, 0)) for w in ws]
    out_shape = [jax.ShapeDtypeStruct((t, w.shape[1]), dt) for w, dt in zip(ws, out_dtypes)]
    return pl.pallas_call(
        functools.partial(_norm_proj_kernel, n_out=len(ws)),
        grid=(t // block,), in_specs=in_specs, out_specs=out_specs, out_shape=out_shape,
        compiler_params=_params(("parallel",)),
    )(x2, g.reshape(1, d), *ws)


def _seg_sum(x, mseg):
    hi = x.astype(BF16)
    lo = (x - hi.astype(F32)).astype(BF16)
    return (jnp.dot(hi, mseg, preferred_element_type=F32)
            + jnp.dot(lo, mseg, preferred_element_type=F32))


def _col_bcast(row):
    return jnp.broadcast_to(row, (LANES, LANES)).T


def _rwkv_kernel(*refs, first_layer):
    zprev, hst = refs[-2], refs[-1]

    @pl.when(pl.program_id(1) == 0)
    def _():
        zprev[...] = jnp.zeros_like(zprev)
        hst[...] = jnp.zeros_like(hst)

    for b in range(refs[0].shape[0]):
        _rwkv_chunk(refs, first_layer, b)


def _rwkv_chunk(refs, first_layer, b):
    if first_layer:
        (z_ref, mu_ref, w0_ref, w2_ref, a0_ref, a2_ref, g2_ref, kk_ref, ka_ref, rk_ref,
         lng_ref, lnb_ref, mseg_ref, y_ref, vf_out_ref, zprev, hst) = refs
    else:
        (z_ref, hv_ref, vf_ref, v0_ref, v2_ref, mu_ref, w0_ref, w2_ref, a0_ref, a2_ref, g2_ref,
         kk_ref, ka_ref, rk_ref, lng_ref, lnb_ref, mseg_ref, y_ref, zprev, hst) = refs
    L = RW_CHUNK
    n_pair = D_RWKV // LANES

    z = z_ref[b]
    row = lax.broadcasted_iota(jnp.int32, (L, 1), 0)
    zs = jnp.where(row == 0, zprev[b:b + 1, :], pltpu.roll(z, 1, 0))
    zprev[b:b + 1, :] = z[L - 1:L, :]
    z = z + (zs - z) * mu_ref[...]
    r = z[:, 0:512]
    k = z[:, 512:1024]
    v = z[:, 1024:1536]
    wa = z[:, 1536:1664]
    gd = z[:, 1664:1792]
    mseg = mseg_ref[...]

    wlin = w0_ref[...] + _bdot(jnp.tanh(wa), w2_ref[...])
    lw = -jnp.exp(-_softplus(-wlin) - 0.5)
    a = _sigmoid(a0_ref[...] + _bdot(wa, a2_ref[...]))
    g = _bdot(_sigmoid(gd), g2_ref[...])
    if first_layer:
        vf_out_ref[b] = v
    else:
        v = v + (vf_ref[b] - v) * _sigmoid(v0_ref[...] + _bdot(hv_ref[b], v2_ref[...]))
    kk = k * kk_ref[...]
    kk = kk / jnp.maximum(jnp.sqrt(_seg_sum(kk * kk, mseg)), 1e-12)
    k2 = k * (1.0 + (a - 1.0) * ka_ref[...])
    av = -kk
    bv = kk * a

    ti = lax.broadcasted_iota(jnp.int32, (L, L), 0)
    si = lax.broadcasted_iota(jnp.int32, (L, L), 1)
    tril = (ti >= si).astype(F32)
    cum = jnp.dot(tril, lw, preferred_element_type=F32, precision=lax.Precision.HIGHEST)
    mid = cum[L // 2 - 1:L // 2, :]
    cm = cum - mid
    ecum = jnp.exp(cm)
    einv = jnp.exp(-cm)
    rt = r * ecum
    kt = k2 * einv
    bt = bv * einv
    at = av * jnp.exp(cm - lw)
    p_end = jnp.exp(cum[L - 1:L, :])
    e_end = ecum[L - 1:L, :]
    e_mid = jnp.exp(mid)

    lane = lax.broadcasted_iota(jnp.int32, (1, LANES), 1)
    m0 = (lane < RW_HEAD).astype(F32)
    m1 = 1.0 - m0
    strict = ti > si
    incl = ti >= si
    bi = lax.broadcasted_iota(jnp.int32, (LANES, LANES), 0) < RW_HEAD
    bj = lax.broadcasted_iota(jnp.int32, (LANES, LANES), 1) < RW_HEAD
    bdmask = (bi == bj).astype(F32)
    zeros_ll = jnp.zeros((L, L), F32)

    ys = []
    for p in range(D_RWKV // LANES):
        sl = slice(LANES * p, LANES * (p + 1))
        A, B, K, R, V = at[:, sl], bt[:, sl], kt[:, sl], rt[:, sl], v[:, sl]
        btkt = jnp.concatenate([B.T, K.T], axis=1)
        h0 = hst[b * n_pair + p]
        h0m = h0 * _col_bcast(e_mid[:, sl])
        sc = _bdot(jnp.concatenate([A * m0, A * m1, R * m0, R * m1], axis=0), btkt)
        aab = [jnp.where(strict, sc[e * L:(e + 1) * L, 0:L], 0.0) for e in range(2)]
        aak = [jnp.where(strict, sc[e * L:(e + 1) * L, L:2 * L], 0.0) for e in range(2)]
        arb = [jnp.where(incl, sc[(2 + e) * L:(3 + e) * L, 0:L], 0.0) for e in range(2)]
        ark = [jnp.where(incl, sc[(2 + e) * L:(3 + e) * L, L:2 * L], 0.0) for e in range(2)]
        arh = _bdot(jnp.concatenate([A, R], axis=0), h0m)
        v01 = jnp.concatenate([V * m0, V * m1], axis=0)
        x = arh[0:L] + _bdot(jnp.concatenate(aak, axis=1), v01)
        pm = jnp.concatenate(aab, axis=1)
        n_fac = int(math.log2(L))
        for it in range(n_fac):
            x = x + _bdot(pm, jnp.concatenate([x * m0, x * m1], axis=0))
            if it + 1 < n_fac:
                pd = jnp.concatenate(
                    [jnp.concatenate([pm[:, 0:L], zeros_ll], axis=1),
                     jnp.concatenate([zeros_ll, pm[:, L:2 * L]], axis=1)], axis=0)
                pm = _bdot(pm, pd)
        u = x
        yp = arh[L:2 * L] + _bdot(jnp.concatenate(arb + ark, axis=1),
                                  jnp.concatenate([u * m0, u * m1, v01], axis=0))
        upd = _bdot(btkt, jnp.concatenate([u, V], axis=0))
        hst[b * n_pair + p] = (h0 * _col_bcast(p_end[:, sl]) + upd * _col_bcast(e_end[:, sl])) * bdmask
        ys.append(yp)
    y = jnp.concatenate(ys, axis=1)

    mean = _seg_sum(y, mseg) * (1.0 / RW_HEAD)
    d = y - mean
    var = _seg_sum(d * d, mseg) * (1.0 / RW_HEAD)
    yn = d * lax.rsqrt(var + GN_EPS) * lng_ref[...] + lnb_ref[...]
    bonus = _seg_sum(r * k2 * rk_ref[...], mseg) * v
    y_ref[b] = (yn + bonus) * g


def _pad_rows(w, rows, offset):
    out = jnp.zeros((rows, w.shape[1]), w.dtype)
    return out.at[offset:offset + w.shape[0]].set(w)


def _rwkv(z_rw, hv, v_first, prm, bsz, seq):
    L = RW_CHUNK
    first = v_first is None
    row = lambda a: a.reshape(1, -1).astype(F32)
    hid = jnp.arange(D_RWKV) // RW_HEAD
    mseg = (hid[:, None] == hid[None, :]).astype(BF16)
    w2p = _pad_rows(prm['w2'], LANES, 0).astype(BF16)
    a2p = _pad_rows(prm['a2'], LANES, 64).astype(BF16)
    common = [row(prm['mu']), row(prm['w0']), w2p, row(prm['a0']), a2p, prm['g2'].astype(BF16),
              row(prm['kk']), row(prm['ka']), row(prm['rk']), row(prm['lng']), row(prm['lnb']), mseg]
    per = RW_SEQS_PER_STEP if bsz % RW_SEQS_PER_STEP == 0 else 1
    tok = lambda w: pl.BlockSpec((per, L, w), lambda b, t: (b, t, 0))
    common_specs = [_full(c.shape) for c in common]
    y_shape = jax.ShapeDtypeStruct((bsz, seq, D_RWKV), F32)
    scratch = [pltpu.VMEM((per, RW_COLS), F32), pltpu.VMEM((per * D_RWKV // LANES, LANES, LANES), F32)]
    if first:
        args = [z_rw] + common
        in_specs = [tok(RW_COLS)] + common_specs
        out_shape = [y_shape, y_shape]
        out_specs = [tok(D_RWKV), tok(D_RWKV)]
    else:
        v2p = _pad_rows(prm['v2'], LANES, 0).astype(BF16)
        extra = [row(prm['v0']), v2p]
        args = [z_rw, hv, v_first] + extra + common
        in_specs = [tok(RW_COLS), tok(LANES), tok(D_RWKV)] + [_full(c.shape) for c in extra] + common_specs
        out_shape = [y_shape]
        out_specs = [tok(D_RWKV)]
    outs = pl.pallas_call(
        functools.partial(_rwkv_kernel, first_layer=first),
        grid=(bsz // per, seq // L), in_specs=in_specs, out_specs=out_specs, out_shape=out_shape,
        scratch_shapes=scratch, compiler_params=_params(("parallel", "arbitrary")),
    )(*args)
    return (outs[0], outs[1]) if first else (outs[0], v_first)


def _s5_kernel(u_ref, wb_ref, wc_ref, lpr_ref, lpi_ref, d_ref, gw_ref, gb_ref, og_ref, o_ref,
               car_re, car_im, xre, xim):
    tb = u_ref.shape[1]

    @pl.when(pl.program_id(1) == 0)
    def _():
        car_re[...] = jnp.zeros_like(car_re)
        car_im[...] = jnp.zeros_like(car_im)

    u = u_ref[0]
    bu = _bdot(u, wb_ref[...])
    xre[...] = bu[:, 0:S5_MODES]
    xim[...] = bu[:, S5_MODES:2 * S5_MODES]
    row = lax.broadcasted_iota(jnp.int32, (SUBLANES, 1), 0)

    def tile(i, carry):
        cr, ci = carry
        rs = pl.ds(pl.multiple_of(i * SUBLANES, SUBLANES), SUBLANES)
        br, bi = xre[rs, :], xim[rs, :]
        for dist in (1, 2, 4):
            keep = row >= dist
            sr = jnp.where(keep, pltpu.roll(br, dist, 0), 0.0)
            si = jnp.where(keep, pltpu.roll(bi, dist, 0), 0.0)
            lr = lpr_ref[dist - 1:dist, :]
            li = lpi_ref[dist - 1:dist, :]
            br, bi = br + lr * sr - li * si, bi + lr * si + li * sr
        pr, pi = lpr_ref[...], lpi_ref[...]
        xr = br + pr * cr - pi * ci
        xi = bi + pr * ci + pi * cr
        xre[rs, :] = xr
        xim[rs, :] = xi
        return xr[SUBLANES - 1:SUBLANES, :], xi[SUBLANES - 1:SUBLANES, :]

    cr, ci = lax.fori_loop(0, tb // SUBLANES, tile, (car_re[...], car_im[...]))
    car_re[...] = cr
    car_im[...] = ci
    wc = wc_ref[...]
    y = _bdot(xre[...], wc[0:S5_MODES]) + _bdot(xim[...], wc[S5_MODES:2 * S5_MODES])
    y = _gelu(y + d_ref[...] * u)
    y = y * _sigmoid(_bdot(y, gw_ref[...]) + gb_ref[...])
    o_ref[0] = _rms(y, og_ref[...])


def _s5_weights(a_re, a_im, log_dt, b_re, b_im, c_re, c_im):
    lam_re = jnp.minimum(a_re.astype(F32), -1e-4)
    lam_im = a_im.astype(F32)
    dt = jnp.exp(log_dt.astype(F32))[:, None]
    mag = jnp.exp(lam_re * dt)
    lb_re = mag * jnp.cos(lam_im * dt)
    lb_im = mag * jnp.sin(lam_im * dt)
    den = lam_re * lam_re + lam_im * lam_im
    c1_re = ((lb_re - 1.0) * lam_re + lb_im * lam_im) / den
    c1_im = (lb_im * lam_re - (lb_re - 1.0) * lam_im) / den
    br, bi = b_re.astype(F32), b_im.astype(F32)
    bb_re = c1_re[..., None] * br - c1_im[..., None] * bi
    bb_im = c1_re[..., None] * bi + c1_im[..., None] * br
    eye = jnp.eye(S5_GROUPS, dtype=F32)
    wb_re = jnp.einsum('gpc,gh->gchp', bb_re, eye).reshape(D_S5, S5_MODES)
    wb_im = jnp.einsum('gpc,gh->gchp', bb_im, eye).reshape(D_S5, S5_MODES)
    wb = jnp.concatenate([wb_re, wb_im], axis=1).astype(BF16)
    wc_re = jnp.einsum('gcp,gh->gphc', c_re.astype(F32), eye).reshape(S5_MODES, D_S5)
    wc_im = jnp.einsum('gcp,gh->gphc', c_im.astype(F32), eye).reshape(S5_MODES, D_S5)
    wc = jnp.concatenate([wc_re, -wc_im], axis=0).astype(BF16)
    pr, pi = [lb_re], [lb_im]
    for _ in range(SUBLANES - 1):
        pr, pi = pr + [pr[-1] * lb_re - pi[-1] * lb_im], pi + [pr[-1] * lb_im + pi[-1] * lb_re]
    lp_re = jnp.stack(pr).reshape(SUBLANES, S5_MODES)
    lp_im = jnp.stack(pi).reshape(SUBLANES, S5_MODES)
    return wb, wc, lp_re, lp_im


def _s5(u, prm, bsz, seq):
    tb = min(S5_BLOCK, seq)
    wb, wc, lp_re, lp_im = _s5_weights(prm['a_re'], prm['a_im'], prm['log_dt'], prm['b_re'], prm['b_im'],
                                       prm['c_re'], prm['c_im'])
    row = lambda a: a.reshape(1, -1).astype(F32)
    consts = [wb, wc, lp_re, lp_im, row(prm['d']), prm['glu_w'].astype(BF16), row(prm['glu_b']),
              row(prm['out_g'])]
    tok = pl.BlockSpec((1, tb, D_S5), lambda b, t: (b, t, 0))
    return pl.pallas_call(
        _s5_kernel, grid=(bsz, seq // tb),
        in_specs=[tok] + [_full(c.shape) for c in consts], out_specs=tok,
        out_shape=jax.ShapeDtypeStruct((bsz, seq, D_S5), F32),
        scratch_shapes=[pltpu.VMEM((1, S5_MODES), F32), pltpu.VMEM((1, S5_MODES), F32),
                        pltpu.VMEM((tb, S5_MODES), F32), pltpu.VMEM((tb, S5_MODES), F32)],
        compiler_params=_params(("parallel", "arbitrary")),
    )(u, *consts)


def _mix_xattn_kernel(x_ref, yr_ref, ys_ref, wo1_ref, wo2_ref, g_ref, wq_ref, k_ref, v_ref, wo_ref, o_ref):
    x1 = x_ref[0] + _bdot(yr_ref[0], wo1_ref[...]) + _bdot(ys_ref[0], wo2_ref[...])
    h = _rms(x1, g_ref[...])
    q = _bdot(h, wq_ref[...])
    km, vm = k_ref[0], v_ref[0]
    outs = []
    for hd in range(XA_HEADS):
        sl = slice(XA_HEAD * hd, XA_HEAD * (hd + 1))
        s = _bdot_nt(q[:, sl], km[:, sl]) * (XA_HEAD ** -0.5)
        s = s - jnp.max(s, axis=-1, keepdims=True)
        e = jnp.exp(s)
        p = e / jnp.sum(e, axis=-1, keepdims=True)
        outs.append(_bdot(p, vm[:, sl]))
    o = jnp.concatenate(outs, axis=1)
    o_ref[0] = x1 + _bdot(o, wo_ref[...])


def _mix_xattn(x, y_rw, y_s5, w_out, g, wq, kmem, vmem, wo, bsz, seq):
    tm = min(XA_BLOCK, seq)
    consts_a = [w_out[:D_RWKV].astype(BF16), w_out[D_RWKV:].astype(BF16), g.reshape(1, -1).astype(F32),
                wq.astype(BF16)]
    tok = lambda w: pl.BlockSpec((1, tm, w), lambda b, t: (b, t, 0))
    mem = pl.BlockSpec((1, N_MEM, D_MODEL), lambda b, t: (b, 0, 0))
    wo_b = wo.astype(BF16)
    return pl.pallas_call(
        _mix_xattn_kernel, grid=(bsz, seq // tm),
        in_specs=[tok(D_MODEL), tok(D_RWKV), tok(D_S5)] + [_full(c.shape) for c in consts_a]
                 + [mem, mem, _full(wo_b.shape)],
        out_specs=tok(D_MODEL), out_shape=jax.ShapeDtypeStruct((bsz, seq, D_MODEL), F32),
        compiler_params=_params(("parallel", "parallel")),
    )(x, y_rw, y_s5, *consts_a, kmem, vmem, wo_b)


def _top_rows(work, order, aux, val_ref, idx_ref):
    for it in range(PEER_TOPK):
        m = jnp.max(work, axis=0, keepdims=True)
        pos = jnp.min(jnp.where(work == m, order, jnp.inf), axis=0, keepdims=True)
        hit = order == pos
        val_ref[it:it + 1, :] = m
        if aux is None:
            idx_ref[it:it + 1, :] = pos
        else:
            idx_ref[it:it + 1, :] = jnp.sum(jnp.where(hit, aux, 0.0), axis=0, keepdims=True)
        work = jnp.where(hit, -jnp.inf, work)


_CAND_ROW_BLOCKS = [(0, PEER_TOPK), (1, SUBLANES), (2, SUBLANES), (3, SUBLANES)]
_CAND_COL_BLOCKS = [(0, PEER_TOPK, 4, 15), (1, SUBLANES, 4, 7), (2, SUBLANES, 4, 4)]
N_CAND = sum(n for _, n in _CAND_ROW_BLOCKS) + sum(n for _, n, _, _ in _CAND_COL_BLOCKS)


def _cand_consts(tm):
    flat, neg = [], []
    for a, nb in _CAND_ROW_BLOCKS:
        flat += [a * PEER_TOPK + b for b in range(nb)]
        neg += [0.0] * nb
    for b, na, lo, hi in _CAND_COL_BLOCKS:
        flat += [a * PEER_TOPK + b for a in range(na)]
        neg += [0.0 if lo <= a <= hi else -float('inf') for a in range(na)]
    flat = [f if n == 0.0 else 1000.0 + i for i, (f, n) in enumerate(zip(flat, neg))]
    col = lambda v: jnp.broadcast_to(jnp.asarray(v, F32)[:, None], (N_CAND, tm))
    return col(flat), col(neg)


def _cand_rows(row_vals, col_vals, combine):
    blocks = [combine(row_vals[a:a + 1, :], col_vals[0:nb, :]) for a, nb in _CAND_ROW_BLOCKS]
    blocks += [combine(row_vals[0:na, :], col_vals[b:b + 1, :]) for b, na, _, _ in _CAND_COL_BLOCKS]
    return jnp.concatenate(blocks, axis=0)


SEL_HEADS_PER_STEP = 4


def _peer_select_kernel(x_ref, g_ref, wq_ref, keys_ref, cflat_ref, cneg_ref, x8_ref, h8_ref, base_ref, shift_ref,
                        gate_ref, q3, idx_t, gate_t, *lists):
    tm = x_ref.shape[0]
    n = SEL_HEADS_PER_STEP
    s1, i1, s2, i2, top, eid = (lists[j * n:(j + 1) * n] for j in range(6))
    x = x_ref[...]
    h = _rms(x, g_ref[...])
    for r in range(D_MODEL // LANES):
        rows = pl.ds(r, tm, stride=D_MODEL // LANES)
        h8_ref[rows, :] = h[:, LANES * r:LANES * (r + 1)]
        x8_ref[rows, :] = x[:, LANES * r:LANES * (r + 1)]
    q = _bdot(h, wq_ref[...])
    for j in range(2 * PEER_HEADS):
        q3[j] = q[:, LANES * j:LANES * (j + 1)]
    iota_k = lax.broadcasted_iota(jnp.int32, (PEER_NKEYS, tm), 0).astype(F32)

    def heads(step, _):
        for u in range(SEL_HEADS_PER_STEP):
            hd = step * SEL_HEADS_PER_STEP + u
            sc1 = _bdot_nt(keys_ref[2 * hd], q3[2 * hd])
            sc2 = _bdot_nt(keys_ref[2 * hd + 1], q3[2 * hd + 1])
            _top_rows(sc1, iota_k, None, s1[u], i1[u])
            _top_rows(sc2, iota_k, None, s2[u], i2[u])
            cand = _cand_rows(s1[u][...], s2[u][...], lambda x, y: x + y) + cneg_ref[...]
            cidx = _cand_rows(i1[u][...], i2[u][...], lambda x, y: x * float(PEER_NKEYS) + y)
            _top_rows(cand, cflat_ref[...], cidx, top[u], eid[u])
            tv = top[u][...]
            e = jnp.exp(tv - jnp.max(tv, axis=0, keepdims=True))
            rs = pl.ds(pl.multiple_of(hd * PEER_TOPK, PEER_TOPK), PEER_TOPK)
            idx_t[rs, :] = eid[u][...]
            gate_t[rs, :] = e / jnp.sum(e, axis=0, keepdims=True)
        return 0

    lax.fori_loop(0, PEER_HEADS // SEL_HEADS_PER_STEP, heads, 0)
    e_t = idx_t[...].T
    pair = jnp.floor(e_t * 0.5)
    base_ref[...] = (pair * float(SUBLANES)).astype(jnp.int32)
    shift_ref[...] = ((e_t - 2.0 * pair) * 16.0).astype(jnp.int32)
    gate_ref[...] = gate_t[...].T


def _peer_select(x2, g, wq, keys):
    t = x2.shape[0]
    tm = SEL_BLOCK
    keys_b = keys.reshape(2 * PEER_HEADS, PEER_NKEYS, LANES).astype(BF16)
    wq_b = wq.astype(BF16)
    cflat, cneg = _cand_consts(tm)
    tokspec = lambda w: pl.BlockSpec((tm, w), lambda i: (i, 0))
    vm = lambda r: pltpu.VMEM((r, tm), F32)
    return pl.pallas_call(
        _peer_select_kernel, grid=(t // tm,),
        in_specs=[tokspec(D_MODEL), _full((1, D_MODEL)), _full(wq_b.shape), _full(keys_b.shape),
                  _full(cflat.shape), _full(cneg.shape)],
        out_specs=[pl.BlockSpec((tm * SUBLANES, LANES), lambda i: (i, 0))] * 2
                  + [tokspec(PEER_SEL), tokspec(PEER_SEL), tokspec(PEER_SEL)],
        out_shape=[jax.ShapeDtypeStruct((t * SUBLANES, LANES), F32)] * 2 + [
                   jax.ShapeDtypeStruct((t, PEER_SEL), jnp.int32),
                   jax.ShapeDtypeStruct((t, PEER_SEL), jnp.int32), jax.ShapeDtypeStruct((t, PEER_SEL), F32)],
        scratch_shapes=[pltpu.VMEM((2 * PEER_HEADS, tm, LANES), F32), vm(PEER_SEL), vm(PEER_SEL)]
                       + [vm(PEER_TOPK) for _ in range(6 * SEL_HEADS_PER_STEP)],
        compiler_params=_params(("parallel",)),
    )(x2, g.reshape(1, -1).astype(F32), wq_b, keys_b, cflat, cneg)


TABLE_PREP_ROWS = 256


def _table_prep_kernel(tab_ref, tiles_ref, packed_ref):
    n = tab_ref.shape[0]
    n_chunk = D_MODEL // LANES
    x = tab_ref[...]
    for r in range(n_chunk):
        tiles_ref[pl.ds(r, n, stride=n_chunk), :] = x[:, LANES * r:LANES * (r + 1)]
    pairs = tiles_ref[...].reshape(n // 2, 2 * n_chunk, LANES)
    as_bits = lambda t: lax.bitcast_convert_type(t.astype(BF16).astype(F32), jnp.int32)
    word = as_bits(pairs[:, 0:n_chunk, :]) | lax.shift_right_logical(as_bits(pairs[:, n_chunk:, :]), 16)
    packed_ref[...] = word.reshape(n // 2 * n_chunk, LANES)


def _table_prep(tab):
    rows = tab.shape[0] * tab.shape[1]
    n = TABLE_PREP_ROWS
    n_chunk = D_MODEL // LANES
    tiles, packed = pl.pallas_call(
        _table_prep_kernel, grid=(rows // n,),
        in_specs=[pl.BlockSpec((n, D_MODEL), lambda i: (i, 0))],
        out_specs=[pl.BlockSpec((n * n_chunk, LANES), lambda i: (i, 0)),
                   pl.BlockSpec((n // 2 * n_chunk, LANES), lambda i: (i, 0))],
        out_shape=[jax.ShapeDtypeStruct((rows * n_chunk, LANES), F32),
                   jax.ShapeDtypeStruct((rows // 2 * n_chunk, LANES), jnp.int32)],
        compiler_params=_params(("parallel",)),
    )(tab.reshape(rows, D_MODEL))
    return tiles.reshape(rows, n_chunk, LANES), packed


def _splat_into(src_ref, t, dst_ref, slot):
    tile = jnp.broadcast_to(src_ref[pl.ds(t, 1), :], (LANES, LANES)).T
    dst_ref[LANES * slot:LANES * (slot + 1), :] = tile


def _bcast_row(ref, row):
    return jnp.broadcast_to(ref[row:row + 1, :], (SUBLANES, LANES))


def _expert_tile(tab_ref, base, shift_splat, row):
    w = tab_ref[pl.ds(pl.multiple_of(base, SUBLANES), SUBLANES), :]
    return lax.bitcast_convert_type((w << _bcast_row(shift_splat, row)) & jnp.int32(-65536), F32)


def _token_rows(t):
    return pl.ds(pl.multiple_of(t * SUBLANES, SUBLANES), SUBLANES)


def _pipelined_tokens(tg, prep, work):
    prep(0, 0)

    def body(i, _):
        t0 = 2 * i
        prep(t0 + 1, 1)
        work(t0, 0)
        prep(jnp.minimum(t0 + 2, tg - 1), 0)
        work(t0 + 1, 1)
        return 0

    lax.fori_loop(0, tg // 2, body, 0)


def _table_spec(packed, layer, n_layers):
    rows = packed.shape[0] // n_layers
    return pl.BlockSpec((rows, LANES), lambda i: (layer, 0), pipeline_mode=pl.Buffered(1))


N_CHUNK = D_MODEL // LANES
CHUNK_STRIDE = PEER_SEL + SUBLANES


def _peer_u_kernel(base_ref, shift_ref, h_ref, gate_ref, tab_ref, eye_ref, c_ref, shift_splat, *planes):
    tg = gate_ref.shape[0]
    ones = jnp.ones((LANES, LANES), BF16)

    half = N_CHUNK // 2

    def prep(t, slot):
        _splat_into(shift_ref, t, shift_splat, slot)

    def gather(t, slot):
        ht = h_ref[_token_rows(t), :]
        for kx in range(PEER_SEL):
            prod = _expert_tile(tab_ref, base_ref[t, kx], shift_splat, LANES * slot + kx) * ht
            fold = prod + pltpu.roll(prod, half, 0)
            planes[slot][pl.ds(kx, half, stride=CHUNK_STRIDE), :] = fold[0:half, :]

    def finish(t, slot):
        plane = planes[slot]
        acc = plane[0:PEER_SEL, :]
        for r in range(1, half):
            acc = acc + plane[CHUNK_STRIDE * r:CHUNK_STRIDE * r + PEER_SEL, :]
        hi = acc.astype(BF16)
        lo = (acc - hi.astype(F32)).astype(BF16)
        tot = jnp.dot(hi, ones, preferred_element_type=F32) + jnp.dot(lo, ones, preferred_element_type=F32)
        score = jnp.sum(tot * eye_ref[...], axis=0, keepdims=True)
        c_ref[pl.ds(t, 1), :] = gate_ref[pl.ds(t, 1), :] * _gelu(score)

    planes[1][...] = jnp.zeros_like(planes[1])
    prep(0, 0)

    def body(i, _):
        t0 = 2 * i
        prep(t0 + 1, 1)
        gather(t0, 0)
        finish(jnp.maximum(t0 - 1, 0), 1)
        prep(jnp.minimum(t0 + 2, tg - 1), 0)
        gather(t0 + 1, 1)
        finish(t0, 0)
        return 0

    lax.fori_loop(0, tg // 2, body, 0)
    finish(tg - 1, 1)


def _peer_u(base, shift, h8, gate, tab, t, layer, n_layers):
    tg = GATHER_BLOCK
    eye = jnp.eye(LANES, dtype=F32)
    tokrow = pl.BlockSpec((tg, PEER_SEL), lambda i: (i, 0))
    return pl.pallas_call(
        _peer_u_kernel, grid=(t // tg,),
        in_specs=[pl.BlockSpec((tg, PEER_SEL), lambda i: (i, 0), memory_space=pltpu.SMEM), tokrow,
                  pl.BlockSpec((tg * SUBLANES, LANES), lambda i: (i, 0)), tokrow,
                  _table_spec(tab, layer, n_layers), _full(eye.shape)],
        out_specs=tokrow, out_shape=jax.ShapeDtypeStruct((t, PEER_SEL), F32),
        scratch_shapes=[pltpu.VMEM((2 * LANES, LANES), jnp.int32)]
                       + [pltpu.VMEM((N_CHUNK // 2 * CHUNK_STRIDE, LANES), F32)] * 2,
        compiler_params=_params(("parallel",)),
    )(base, shift, h8, gate, tab, eye)


def _peer_v_kernel(base_ref, shift_ref, c_ref, x_ref, tab_ref, _aliased_out, o_ref, shift_splat, c_splat, otile):
    tg = c_ref.shape[0]
    n_acc = 4

    def prep(t, slot):
        _splat_into(shift_ref, t, shift_splat, slot)
        _splat_into(c_ref, t, c_splat, slot)

    def work(t, slot):
        accs = [jnp.zeros((SUBLANES, LANES), F32) for _ in range(n_acc)]
        for kx in range(PEER_SEL):
            row = LANES * slot + kx
            accs[kx % n_acc] = accs[kx % n_acc] + (_bcast_row(c_splat, row)
                                                   * _expert_tile(tab_ref, base_ref[t, kx], shift_splat, row))
        rs = _token_rows(t)
        otile[rs, :] = x_ref[rs, :] + ((accs[0] + accs[1]) + (accs[2] + accs[3]))

    _pipelined_tokens(tg, prep, work)
    for r in range(N_CHUNK):
        o_ref[:, LANES * r:LANES * (r + 1)] = otile[pl.ds(r, tg, stride=N_CHUNK), :]


def _peer_v(base, shift, c, x8, x2, tab, t, layer, n_layers):
    tg = GATHER_BLOCK
    smem = pl.BlockSpec((tg, PEER_SEL), lambda i: (i, 0), memory_space=pltpu.SMEM)
    tokrow = pl.BlockSpec((tg, PEER_SEL), lambda i: (i, 0))
    tile = pl.BlockSpec((tg * SUBLANES, LANES), lambda i: (i, 0))
    return pl.pallas_call(
        _peer_v_kernel, grid=(t // tg,),
        in_specs=[smem, tokrow, tokrow, tile, _table_spec(tab, layer, n_layers),
                  pl.BlockSpec(memory_space=pl.ANY)],
        out_specs=pl.BlockSpec((tg, D_MODEL), lambda i: (i, 0)),
        out_shape=jax.ShapeDtypeStruct(x2.shape, F32),
        input_output_aliases={5: 0},
        scratch_shapes=[pltpu.VMEM((2 * LANES, LANES), jnp.int32), pltpu.VMEM((2 * LANES, LANES), F32),
                        pltpu.VMEM((tg * SUBLANES, LANES), F32)],
        compiler_params=_params(("parallel",)),
    )(base, shift, c, x8, tab, x2)


SC_WORKERS = 32
SC_LANES = 16
SC_ROWS = 32
SC_SHARE_PIPELINED = 42 / 64
SC_SHARE_TAIL = (26 / 64, 40 / 64, 40 / 64, 40 / 64)
SEQ_GROUPS = 4
SC_GROUP = 8
SC_REGS = 32


def _peer_sc(eidx, gate, h8, x8, tab_u, tab_v, t_off):
    ts = eidx.shape[0]
    per_w = ts // SC_WORKERS
    n_chunk = PEER_SEL // SC_ROWS
    grp = SC_GROUP
    sel = grp * PEER_SEL
    n_pairs = grp * n_chunk // 2
    per_tile = LANES // SC_LANES
    n_blk = D_MODEL // (SC_REGS * SC_LANES)
    mesh = plsc.VectorSubcoreMesh(core_axis_name="c", subcore_axis_name="s")

    def piece(ref, lead, q):
        return ref[lead, q // per_tile, pl.ds(SC_LANES * (q % per_tile), SC_LANES)]

    def tok_piece(ref, tok, q):
        return ref[tok * N_CHUNK + q // per_tile, pl.ds(SC_LANES * (q % per_tile), SC_LANES)]

    def body(u_hbm, v_hbm, idx_hbm, gate_hbm, h_hbm, x_hbm, out_hbm,
             idx_v, c_v, h_v, acc_v, part_v, rows0, rows1, sem0, sem1):
        wid = lax.axis_index("s") * 2 + lax.axis_index("c")
        bufs, sems = (rows0, rows1), (sem0, sem1)
        lane = lax.iota(jnp.int32, SC_LANES)

        def stream(tab_hbm, compute):
            def gather(chunk, slot):
                start = pl.multiple_of(chunk * SC_ROWS, SC_ROWS)
                return pltpu.make_async_copy(tab_hbm.at[idx_v.at[pl.ds(start, SC_ROWS)]], bufs[slot], sems[slot])

            gather(0, 0).start()

            @pl.loop(0, n_pairs)
            def _(p):
                c0 = 2 * p
                gather(c0 + 1, 1).start()
                gather(c0, 0).wait()
                compute(c0, 0)
                gather(jnp.minimum(c0 + 2, 2 * n_pairs - 1), 0).start()
                gather(c0 + 1, 1).wait()
                compute(c0 + 1, 1)

            gather(0, 0).wait()

        def dots(chunk, slot):
            tok = chunk // n_chunk
            for blk in range(n_blk):
                where = [blk * SC_REGS + j for j in range(SC_REGS)]
                hs = [tok_piece(h_v, tok, q) for q in where]

                def row(k, carry):
                    parts = [None] * 4
                    for i, q in enumerate(where):
                        term = piece(bufs[slot], k, q) * hs[i]
                        parts[i % 4] = term if parts[i % 4] is None else parts[i % 4] + term
                    tot = (parts[0] + parts[1]) + (parts[2] + parts[3])
                    at = pl.ds(pl.multiple_of((chunk * SC_ROWS + k) * SC_LANES, SC_LANES), SC_LANES)
                    if blk == 0:
                        part_v[at] = tot
                    else:
                        part_v[at] = part_v[at] + tot
                    return carry

                lax.fori_loop(0, SC_ROWS, row, 0)

        def weigh(chunk, slot):
            tok = chunk // n_chunk
            for blk in range(n_blk):
                where = [blk * SC_REGS + j for j in range(SC_REGS)]

                def row(k, accs):
                    ck = plsc.load_gather(c_v, [jnp.full((SC_LANES,), chunk * SC_ROWS, jnp.int32) + k])
                    return tuple(a + ck * piece(bufs[slot], k, q) for a, q in zip(accs, where))

                accs = lax.fori_loop(0, SC_ROWS, row, tuple(tok_piece(acc_v, tok, q) for q in where))
                for a, q in zip(accs, where):
                    acc_v[tok * N_CHUNK + q // per_tile, pl.ds(SC_LANES * (q % per_tile), SC_LANES)] = a

        @pl.loop(0, per_w // grp)
        def _(g):
            t0 = wid * per_w + g * grp
            flat = pl.ds(pl.multiple_of(t0 * PEER_SEL, sel), sel)
            pltpu.sync_copy(idx_hbm.at[flat], idx_v)
            pltpu.sync_copy(gate_hbm.at[flat], c_v)
            tok_rows = grp * N_CHUNK
            src_rows = pl.ds(pl.multiple_of((t_off + t0) * N_CHUNK, tok_rows), tok_rows)
            pltpu.sync_copy(h_hbm.at[src_rows], h_v)
            pltpu.sync_copy(x_hbm.at[src_rows], acc_v)
            stream(u_hbm, dots)

            @pl.loop(0, sel // SC_LANES)
            def _(m):
                base = (m * SC_LANES + lane) * SC_LANES
                score = plsc.load_gather(part_v, [base])
                for l in range(1, SC_LANES):
                    score = score + plsc.load_gather(part_v, [base + l])
                z = math.sqrt(2.0 / math.pi) * (score + 0.044715 * (score * score * score))
                tanh_z = 1.0 - 2.0 / (jnp.exp(2.0 * z) + 1.0)
                at = pl.ds(pl.multiple_of(m * SC_LANES, SC_LANES), SC_LANES)
                c_v[at] = c_v[at] * (0.5 * score * (1.0 + tanh_z))

            stream(v_hbm, weigh)
            pltpu.sync_copy(acc_v, out_hbm.at[pl.ds(pl.multiple_of(t0 * N_CHUNK, tok_rows), tok_rows)])

    tile = lambda n: pltpu.VMEM((n, N_CHUNK, LANES), F32)
    tok_tiles = pltpu.VMEM((grp * N_CHUNK, LANES), F32)
    return pl.kernel(
        body, mesh=mesh, out_type=jax.ShapeDtypeStruct((ts * N_CHUNK, LANES), F32),
        scratch_types=[pltpu.VMEM((sel,), jnp.int32), pltpu.VMEM((sel,), F32), tok_tiles, tok_tiles,
                       pltpu.VMEM((sel * SC_LANES,), F32), tile(SC_ROWS), tile(SC_ROWS),
                       pltpu.SemaphoreType.DMA, pltpu.SemaphoreType.DMA],
        compiler_params=pltpu.CompilerParams(needs_layout_passes=False, use_tc_tiling_on_sc=True),
    )(tab_u, tab_v, eidx.reshape(-1), gate.reshape(-1), h8, x8)


def _final_norm_kernel(x_ref, g_ref, o_ref):
    o_ref[...] = _rms(x_ref[...], g_ref[...])


def _final_norm(x2, g):
    t, d = x2.shape
    tm = min(PROJ_BLOCK, t)
    spec = pl.BlockSpec((tm, d), lambda i: (i, 0))
    return pl.pallas_call(
        _final_norm_kernel, grid=(t // tm,), in_specs=[spec, _full((1, d))], out_specs=spec,
        out_shape=jax.ShapeDtypeStruct((t, d), F32), compiler_params=_params(("parallel",)),
    )(x2, g.reshape(1, d).astype(F32))


def kernel(x, mem, norm_mix, w_in, rw_mu, rw_w0, rw_w2, rw_a0, rw_a2, rw_g2, rw_kk, rw_ka, rw_rk, rw_v0, rw_v1, rw_v2, rw_lnx_g, rw_lnx_b, s5_a_re, s5_a_im, s5_log_dt, s5_b_re, s5_b_im, s5_c_re, s5_c_im, s5_d, s5_glu_w, s5_glu_b, s5_out_g, w_out, norm_xa, norm_mem, xa_wq, xa_wk, xa_wv, xa_wo, norm_ffn, peer_wq, peer_keys, peer_u, peer_v, norm_final):
    bsz, seq, d = x.shape
    depth = w_in.shape[0]

    def layer(l, xg, memg, v_first, tabs, sc_share):
        nb = xg.shape[0]
        t = nb * seq
        x2 = xg.reshape(t, d)
        ws = [w_in[l][:, :RW_COLS].astype(BF16), w_in[l][:, RW_COLS:].astype(BF16)]
        if l > 0:
            ws.append(_pad_rows(rw_v1[l - 1].T, LANES, 0).T.astype(BF16))
        outs = _norm_proj(x2, norm_mix[l], ws, [F32] * len(ws), PROJ_BLOCK)
        z_rw = outs[0].reshape(nb, seq, RW_COLS)
        u_s5 = outs[1].reshape(nb, seq, D_S5)
        hv = outs[2].reshape(nb, seq, LANES) if l > 0 else None
        rw_prm = dict(mu=rw_mu[l], w0=rw_w0[l], w2=rw_w2[l], a0=rw_a0[l], a2=rw_a2[l], g2=rw_g2[l],
                      kk=rw_kk[l], ka=rw_ka[l], rk=rw_rk[l], lng=rw_lnx_g[l], lnb=rw_lnx_b[l])
        if l > 0:
            rw_prm.update(v0=rw_v0[l - 1], v2=rw_v2[l - 1])
        y_rw, v_first = _rwkv(z_rw, hv, v_first, rw_prm, nb, seq)
        s5_prm = dict(a_re=s5_a_re[l], a_im=s5_a_im[l], log_dt=s5_log_dt[l], b_re=s5_b_re[l], b_im=s5_b_im[l],
                      c_re=s5_c_re[l], c_im=s5_c_im[l], d=s5_d[l], glu_w=s5_glu_w[l], glu_b=s5_glu_b[l],
                      out_g=s5_out_g[l])
        y_s5 = _s5(u_s5, s5_prm, nb, seq)
        kv = _norm_proj(memg.reshape(nb * N_MEM, d), norm_mem[l],
                        [xa_wk[l].astype(BF16), xa_wv[l].astype(BF16)], [BF16, BF16], PROJ_BLOCK)
        kmem = kv[0].reshape(nb, N_MEM, d)
        vmem = kv[1].reshape(nb, N_MEM, d)
        xg = _mix_xattn(xg, y_rw, y_s5, w_out[l], norm_xa[l], xa_wq[l], kmem, vmem, xa_wo[l], nb, seq)
        x2 = xg.reshape(t, d)
        x8, h8, base, shift, gate = _peer_select(x2, norm_ffn[l], peer_wq[l], peer_keys[l])
        u3, v3, u_packed, v_packed = tabs
        n_sc = int(t * sc_share)
        assert n_sc % (SC_WORKERS * SC_GROUP) == 0 and (t - n_sc) % GATHER_BLOCK == 0
        t_tc = t - n_sc
        n_exp = peer_u.shape[1]
        eidx_sc = (base[t_tc:] >> 2) + (shift[t_tc:] >> 4) + l * n_exp
        x_sc = _peer_sc(eidx_sc, gate[t_tc:], h8, x8, u3, v3, t_tc)
        c = _peer_u(base, shift, h8, gate, u_packed, t_tc, l, depth)
        x_new = _peer_v(base, shift, c, x8, x2, v_packed, t_tc, l, depth)
        xg = lax.dynamic_update_slice(x_new, x_sc.reshape(n_sc, d), (t_tc, 0)).reshape(nb, seq, d)
        return xg, v_first

    nb = bsz // SEQ_GROUPS
    xs = [x[g * nb:(g + 1) * nb] for g in range(SEQ_GROUPS)]
    mems = [mem[g * nb:(g + 1) * nb] for g in range(SEQ_GROUPS)]
    v_firsts = [None] * SEQ_GROUPS
    (u3, u_packed), (v3, v_packed) = _table_prep(peer_u), _table_prep(peer_v)
    tabs = (u3, v3, u_packed, v_packed)
    for l in range(depth):
        for g in range(SEQ_GROUPS):
            calls_after = (depth - 1 - l) * SEQ_GROUPS + (SEQ_GROUPS - 1 - g)
            share = SC_SHARE_TAIL[calls_after] if calls_after < len(SC_SHARE_TAIL) else SC_SHARE_PIPELINED
            xs[g], v_firsts[g] = layer(l, xs[g], mems[g], v_firsts[g], tabs, share)
    outs = [_final_norm(xg.reshape(nb * seq, d), norm_final).reshape(nb, seq, d) for xg in xs]
    return jnp.concatenate(outs, axis=0)
```
